```python
import jax, jax.numpy as jnp
from jax import lax
import numpy as np

D_MODEL = 1024
BATCH = 16
SEQ = 4096
DEPTH = 1

N_MEM = 256
ML_HEADS = 4
ML_DQK = 128
ML_DV = 256
ML_CHUNK = 64
ML_CONV = 4
FX_HEADS = 8
FX_DH = 128
FX_BLOCK = 128
CA_HEADS = 4
CA_DH = 256
N_EXPERTS = 32
TOP_K = 4
D_FF = D_MODEL
SWIGLU_LIMIT = 7.0
SWIGLU_ALPHA = 1.702
EPS = 1e-5
N_BRANCH = 3

ML_QK_W = ML_HEADS * ML_DQK
ML_V_W = ML_HEADS * ML_DV
FX_W = FX_HEADS * FX_DH
CA_W = CA_HEADS * CA_DH
BRANCH_W = 1024
SPLITS = (ML_QK_W, ML_QK_W, ML_V_W, ML_HEADS, ML_HEADS, ML_V_W,
          FX_W, FX_W, FX_W, FX_HEADS, CA_W, N_BRANCH * D_MODEL)
D_IN_PROJ = 2 * ML_QK_W + 2 * ML_V_W + 2 * ML_HEADS + 3 * FX_W + FX_HEADS + CA_W + N_BRANCH * D_MODEL

kernel_name = "hybrid_mlstm_fox_memxattn_moe"


def rmsnorm(x, g):
    xf = x.astype(jnp.float32)
    y = xf * lax.rsqrt(jnp.mean(xf * xf, axis=-1, keepdims=True) + EPS)
    return (y * g.astype(jnp.float32)).astype(x.dtype)


def causal_dwconv(x, w):
    K, C = w.shape
    return lax.conv_general_dilated(
        x, w[:, None, :].astype(x.dtype), window_strides=(1,), padding=[(K - 1, 0)],
        dimension_numbers=("NWC", "WIO", "NWC"), feature_group_count=C)


def mlstm_chunkwise(q, k, v, i_pre, f_pre):
    B, S, H, dqk = q.shape
    dv = v.shape[-1]
    L = ML_CHUNK
    NC = S // L
    f32 = jnp.float32
    q = q.astype(f32) * (dqk ** -0.5)

    def chunks(a):
        return a.astype(f32).reshape(B, NC, L, H, a.shape[-1]).transpose(1, 0, 3, 2, 4)

    def gchunks(a):
        return a.astype(f32).reshape(B, NC, L, H).transpose(1, 0, 3, 2)

    qc, kc, vc = chunks(q), chunks(k), chunks(v)
    ic = gchunks(i_pre)
    lfc = gchunks(jax.nn.log_sigmoid(f_pre.astype(f32)))
    causal = jnp.tril(jnp.ones((L, L), dtype=bool))

    def step(carry, inp):
        C, n, m = carry
        qb, kb, vb, ib, lfb = inp
        b = jnp.cumsum(lfb, axis=-1)
        Dm = jnp.where(causal, b[..., :, None] - b[..., None, :] + ib[..., None, :], -jnp.inf)
        m_inter = b + m[..., None]
        m_t = jnp.maximum(Dm.max(-1), m_inter)
        w_intra = jnp.exp(Dm - m_t[..., None])
        w_inter = jnp.exp(m_inter - m_t)
        s = jnp.einsum("bhtd,bhsd->bhts", qb, kb) * w_intra
        num = jnp.einsum("bhts,bhsv->bhtv", s, vb) + w_inter[..., None] * jnp.einsum("bhtd,bhdv->bhtv", qb, C)
        den = s.sum(-1) + w_inter * jnp.einsum("bhtd,bhd->bht", qb, n)
        h = num / jnp.maximum(jnp.abs(den), jnp.exp(-m_t))[..., None]
        m_new = m_t[..., -1]
        wk = jnp.exp(b[..., -1:] - b + ib - m_new[..., None])
        decay = jnp.exp(b[..., -1] + m - m_new)
        C_new = decay[..., None, None] * C + jnp.einsum("bhsd,bhsv->bhdv", kb * wk[..., None], vb)
        n_new = decay[..., None] * n + jnp.einsum("bhs,bhsd->bhd", wk, kb)
        return (C_new, n_new, m_new), h

    init = (jnp.zeros((B, H, dqk, dv), f32), jnp.zeros((B, H, dqk), f32), jnp.zeros((B, H), f32))
    _, hs = lax.scan(step, init, (qc, kc, vc, ic, lfc))
    return hs.transpose(1, 0, 3, 2, 4).reshape(B, S, H, dv)


def forgetting_attention(q, k, v, f_pre):
    B, S, H, d = q.shape
    NB = S // FX_BLOCK
    f32 = jnp.float32
    c = jnp.cumsum(jax.nn.log_sigmoid(f_pre.astype(f32)), axis=1).transpose(0, 2, 1)
    kh = k.transpose(0, 2, 1, 3)
    vh = v.transpose(0, 2, 1, 3)
    qb = q.reshape(B, NB, FX_BLOCK, H, d).transpose(1, 0, 3, 2, 4)
    cqb = c.reshape(B, H, NB, FX_BLOCK).transpose(2, 0, 1, 3)
    starts = jnp.arange(NB, dtype=jnp.int32) * FX_BLOCK
    kpos = jnp.arange(S, dtype=jnp.int32)
    scale = d ** -0.5

    def block(args):
        qblk, cq, start = args
        s = jnp.einsum("bhqd,bhkd->bhqk", qblk, kh).astype(f32) * scale + cq[..., None] - c[:, :, None, :]
        qpos = start + jnp.arange(FX_BLOCK, dtype=jnp.int32)
        s = jnp.where(kpos[None, :] <= qpos[:, None], s, -jnp.inf)
        p = jax.nn.softmax(s, axis=-1).astype(vh.dtype)
        return jnp.einsum("bhqk,bhkd->bhqd", p, vh)

    out = lax.map(block, (qb, cqb, starts))
    return out.transpose(1, 0, 3, 2, 4).reshape(B, S, H, d)


def memory_attention(q, km, vm):
    s = jnp.einsum("bshd,bmhd->bhsm", q, km).astype(jnp.float32) * (q.shape[-1] ** -0.5)
    p = jax.nn.softmax(s, axis=-1).astype(vm.dtype)
    return jnp.einsum("bhsm,bmhd->bshd", p, vm)


def moe_ffn(h, w_router, b_router, w_exp_in, b_exp_in, w_exp_out, b_exp_out):
    B, S, D = h.shape
    t = h.reshape(B * S, D)
    logits = (t @ w_router + b_router).astype(jnp.float32)
    top_v, top_i = lax.top_k(logits, TOP_K)
    top_w = jax.nn.softmax(top_v, axis=-1)
    combine = jnp.sum(jax.nn.one_hot(top_i, N_EXPERTS, dtype=jnp.float32) * top_w[..., None], axis=1)

    def expert_step(acc, p):
        w1, b1, w2, b2, ce = p
        u = t @ w1 + b1
        g = jnp.minimum(u[:, :D_FF], SWIGLU_LIMIT)
        lin = jnp.clip(u[:, D_FF:], -SWIGLU_LIMIT, SWIGLU_LIMIT)
        a = g * jax.nn.sigmoid(SWIGLU_ALPHA * g) * (lin + 1)
        y = a @ w2 + b2
        return acc + ce[:, None].astype(y.dtype) * y, None

    out, _ = lax.scan(expert_step, jnp.zeros_like(t), (w_exp_in, b_exp_in, w_exp_out, b_exp_out, combine.T))
    return out.reshape(B, S, D)


def setup_inputs(seed: int = 0) -> dict:
    key = jax.random.key(seed)
    ks = jax.random.split(key, 24)
    f32 = jnp.float32
    nrm = lambda k, shape, s: jax.random.normal(k, shape, f32) * s
    Ld = DEPTH
    return {
        "x": nrm(ks[0], (BATCH, SEQ, D_MODEL), 1.0),
        "mem": nrm(ks[1], (BATCH, N_MEM, D_MODEL), 1.0),
        "norm_mix": 1.0 + nrm(ks[2], (Ld, D_MODEL), 0.1),
        "w_in": nrm(ks[3], (Ld, D_MODEL, D_IN_PROJ), D_MODEL ** -0.5),
        "b_ml_gates": jnp.concatenate([nrm(ks[4], (Ld, ML_HEADS), 0.1),
                                       3.0 + nrm(ks[5], (Ld, ML_HEADS), 0.5)], axis=-1),
        "conv_ml": nrm(ks[6], (Ld, ML_CONV, 2 * ML_QK_W), ML_CONV ** -0.5),
        "ml_head_norm": 1.0 + nrm(ks[7], (Ld, ML_V_W), 0.1),
        "b_fx_gate": 1.0 + nrm(ks[8], (Ld, FX_HEADS), 0.5),
        "norm_mem": 1.0 + nrm(ks[9], (Ld, D_MODEL), 0.1),
        "w_mem_kv": nrm(ks[10], (Ld, D_MODEL, 2 * CA_W), D_MODEL ** -0.5),
        "w_branch": nrm(ks[11], (Ld, N_BRANCH, BRANCH_W, D_MODEL), BRANCH_W ** -0.5),
        "w_out": nrm(ks[12], (Ld, D_MODEL, D_MODEL), D_MODEL ** -0.5),
        "norm_moe": 1.0 + nrm(ks[13], (Ld, D_MODEL), 0.1),
        "w_router": nrm(ks[14], (Ld, D_MODEL, N_EXPERTS), D_MODEL ** -0.5),
        "b_router": nrm(ks[15], (Ld, N_EXPERTS), 0.01),
        "w_exp_in": nrm(ks[16], (Ld, N_EXPERTS, D_MODEL, 2 * D_FF), D_MODEL ** -0.5),
        "b_exp_in": nrm(ks[17], (Ld, N_EXPERTS, 2 * D_FF), 0.01),
        "w_exp_out": nrm(ks[18], (Ld, N_EXPERTS, D_FF, D_MODEL), D_FF ** -0.5),
        "b_exp_out": nrm(ks[19], (Ld, N_EXPERTS, D_MODEL), 0.01),
        "norm_final": 1.0 + nrm(ks[20], (D_MODEL,), 0.1),
    }


def reference(x, mem, norm_mix, w_in, b_ml_gates, conv_ml, ml_head_norm, b_fx_gate, norm_mem,
              w_mem_kv, w_branch, w_out, norm_moe, w_router, b_router, w_exp_in, b_exp_in,
              w_exp_out, b_exp_out, norm_final):
    B, S, D = x.shape
    M = mem.shape[1]
    offsets = np.cumsum(SPLITS)[:-1].tolist()
    for l in range(DEPTH):
        h = rmsnorm(x, norm_mix[l])
        proj = h @ w_in[l]
        (ml_q, ml_k, ml_v, ml_i, ml_f, ml_o, fx_q, fx_k, fx_v, fx_f, ca_q, gates) = jnp.split(proj, offsets, axis=-1)

        qk = jax.nn.silu(causal_dwconv(jnp.concatenate([ml_q, ml_k], axis=-1), conv_ml[l]))
        mq = qk[..., :ML_QK_W].reshape(B, S, ML_HEADS, ML_DQK)
        mk = qk[..., ML_QK_W:].reshape(B, S, ML_HEADS, ML_DQK)
        mv = ml_v.reshape(B, S, ML_HEADS, ML_DV)
        hm = mlstm_chunkwise(mq, mk, mv, ml_i + b_ml_gates[l, :ML_HEADS], ml_f + b_ml_gates[l, ML_HEADS:])
        hm = hm * lax.rsqrt(jnp.mean(hm * hm, axis=-1, keepdims=True) + EPS)
        hm = hm.reshape(B, S, ML_V_W) * ml_head_norm[l].astype(jnp.float32)
        y_ml = (hm * jax.nn.sigmoid(ml_o.astype(jnp.float32))).astype(x.dtype)

        y_fx = forgetting_attention(fx_q.reshape(B, S, FX_HEADS, FX_DH), fx_k.reshape(B, S, FX_HEADS, FX_DH),
                                    fx_v.reshape(B, S, FX_HEADS, FX_DH), fx_f + b_fx_gate[l]).reshape(B, S, FX_W)

        kv = rmsnorm(mem, norm_mem[l]) @ w_mem_kv[l]
        y_ca = memory_attention(ca_q.reshape(B, S, CA_HEADS, CA_DH),
                                kv[..., :CA_W].reshape(B, M, CA_HEADS, CA_DH),
                                kv[..., CA_W:].reshape(B, M, CA_HEADS, CA_DH)).reshape(B, S, CA_W)

        ybr = jnp.stack([y_ml, y_fx, y_ca], axis=2)
        pbr = jnp.einsum("bsnc,ncd->bsnd", ybr, w_branch[l])
        g = jax.nn.sigmoid(gates.reshape(B, S, N_BRANCH, D))
        merged = jnp.sum(g * pbr, axis=2)
        x = x + merged @ w_out[l]

        x = x + moe_ffn(rmsnorm(x, norm_moe[l]), w_router[l], b_router[l], w_exp_in[l], b_exp_in[l],
                        w_exp_out[l], b_exp_out[l])
    return rmsnorm(x, norm_final)
```

```python
import functools

import jax
import jax.numpy as jnp
from jax import lax
from jax.experimental import pallas as pl
from jax.experimental.pallas import tpu as pltpu

F32 = jnp.float32
BF16 = jnp.bfloat16
I32 = jnp.int32

D_MODEL = 1024
N_MEM_HEADS = 4
ML_HEADS = 4
ML_DQK = 128
ML_DV = 256
ML_CONV = 4
FX_HEADS = 8
FX_DH = 128
CA_HEADS = 4
CA_DH = 256
N_EXPERTS = 32
TOP_K = 4
D_FF = D_MODEL
SWIGLU_LIMIT = 7.0
SWIGLU_ALPHA = 1.702
EPS = 1e-5
LANES = 128
HALF = D_MODEL // 2
HI_MASK = -65536

COL_MLQK, COL_MLV, COL_MLO, COL_FXQ, COL_FXK, COL_FXV, COL_CAQ, COL_GATE0 = 0, 1, 2, 3, 4, 5, 6, 7
N_BIG = 10 * D_MODEL

VMEM_LIMIT = 56 * 1024 * 1024


def _cparams(sem):
    return pltpu.CompilerParams(dimension_semantics=sem, vmem_limit_bytes=VMEM_LIMIT)


def _rms(x, g):
    return x * lax.rsqrt(jnp.mean(x * x, axis=-1, keepdims=True) + EPS) * g


def _log_sigmoid(x):
    return jnp.minimum(x, 0.0) - jnp.log1p(jnp.exp(-jnp.abs(x)))


def _pack_rows(y):
    bits = lax.bitcast_convert_type(y.astype(BF16).astype(F32), I32)
    return lax.shift_right_logical(bits[:, :HALF], 16) | (bits[:, HALF:] & HI_MASK)


def _unpack_rows(w):
    lo = lax.bitcast_convert_type(lax.shift_left(w, 16), F32)
    hi = lax.bitcast_convert_type(w & HI_MASK, F32)
    return lo, hi


def _inproj_body(x_ref, g_ref, w_ref, ws_ref, o_ref, os_ref, h_ref):
    @pl.when(pl.program_id(1) == 0)
    def _():
        hb = _rms(x_ref[...], g_ref[...]).astype(BF16)
        h_ref[...] = hb
        os_ref[...] = jnp.dot(hb, ws_ref[...], preferred_element_type=F32)

    o_ref[...] = jnp.dot(h_ref[...], w_ref[...], preferred_element_type=F32).astype(BF16)


def _inproj(x2d, g, w_big, w_small):
    T = x2d.shape[0]
    tm = min(1024, T)
    tn = 2048
    return pl.pallas_call(
        _inproj_body,
        grid=(T // tm, N_BIG // tn),
        in_specs=[
            pl.BlockSpec((tm, D_MODEL), lambda i, j: (i, 0)),
            pl.BlockSpec((1, D_MODEL), lambda i, j: (0, 0)),
            pl.BlockSpec((D_MODEL, tn), lambda i, j: (0, j)),
            pl.BlockSpec((D_MODEL, LANES), lambda i, j: (0, 0)),
        ],
        out_specs=[
            pl.BlockSpec((tm, tn), lambda i, j: (i, j)),
            pl.BlockSpec((tm, LANES), lambda i, j: (i, 0)),
        ],
        out_shape=[
            jax.ShapeDtypeStruct((T, N_BIG), BF16),
            jax.ShapeDtypeStruct((T, LANES), F32),
        ],
        scratch_shapes=[pltpu.VMEM((tm, D_MODEL), BF16)],
        compiler_params=_cparams(("parallel", "arbitrary")),
        name="inproj",
    )(x2d, g, w_big, w_small)


ML_BLOCK = 512
ML_CHUNK = 128
CONV_PAD = 8


def _mlstm_body(qk_ref, v_ref, o_ref, g_ref, cw_ref, bg_ref, hn_ref, y_ref, xbuf, c_st, n_st, m_st):
    L = ML_CHUNK

    @pl.when(pl.program_id(1) == 0)
    def _():
        xbuf[0:CONV_PAD, :] = jnp.zeros((CONV_PAD, D_MODEL), F32)
        c_st[...] = jnp.zeros_like(c_st)
        n_st[...] = jnp.zeros_like(n_st)
        m_st[...] = jnp.zeros_like(m_st)

    xbuf[CONV_PAD:CONV_PAD + ML_BLOCK, :] = qk_ref[...].astype(F32)
    cw = cw_ref[...]
    row = lax.broadcasted_iota(I32, (L, L), 0)
    col = lax.broadcasted_iota(I32, (L, L), 1)
    causal = row >= col
    tri = causal.astype(F32)
    bg = bg_ref[...]
    scale = ML_DQK ** -0.5

    for c in range(ML_BLOCK // L):
        r0 = c * L
        conv = cw[0:1, :] * xbuf[r0 + CONV_PAD - 3:r0 + CONV_PAD - 3 + L, :]
        for j in range(1, ML_CONV):
            s0 = r0 + CONV_PAD - 3 + j
            conv = conv + cw[j:j + 1, :] * xbuf[s0:s0 + L, :]
        act = conv * jax.nn.sigmoid(conv)

        gates = g_ref[r0:r0 + L, :] + bg
        cum = jnp.dot(tri, _log_sigmoid(gates), precision=lax.Precision.HIGHEST,
                      preferred_element_type=F32)
        gates_t = gates.T
        cum_t = cum.T
        for h in range(ML_HEADS):
            b_col = cum[:, ML_HEADS + h:ML_HEADS + h + 1]
            i_col = gates[:, h:h + 1]
            b_row = cum_t[ML_HEADS + h:ML_HEADS + h + 1, :]
            i_row = gates_t[h:h + 1, :]
            m_prev = m_st[h]
            dm = jnp.where(causal, b_col + (i_row - b_row), -jnp.inf)
            m_inter = b_col + m_prev
            m_t = jnp.maximum(jnp.max(dm, axis=-1, keepdims=True), m_inter)
            w_intra = jnp.exp(dm - m_t)
            w_inter = jnp.exp(m_inter - m_t)

            qh = act[:, h * ML_DQK:(h + 1) * ML_DQK] * scale
            kh = act[:, (ML_HEADS + h) * ML_DQK:(ML_HEADS + h + 1) * ML_DQK]
            qb = qh.astype(BF16)
            vb = v_ref[r0:r0 + L, h * ML_DV:(h + 1) * ML_DV]
            s = lax.dot_general(qb, kh.astype(BF16), (((1,), (1,)), ((), ())),
                                preferred_element_type=F32) * w_intra
            c_old = c_st[h]
            n_old = n_st[h]
            num = jnp.dot(s.astype(BF16), vb, preferred_element_type=F32) + w_inter * jnp.dot(
                qb, c_old.astype(BF16), preferred_element_type=F32)
            den = jnp.sum(s, axis=-1, keepdims=True) + w_inter * jnp.sum(qh * n_old, axis=-1, keepdims=True)
            hv = num / jnp.maximum(jnp.abs(den), jnp.exp(-m_t))

            m_new = m_t[L - 1:L, :]
            b_last = b_col[L - 1:L, :]
            wk = jnp.exp(b_last - b_col + i_col - m_new)
            decay = jnp.exp(b_last + m_prev - m_new)
            kw = kh * wk
            c_st[h] = decay * c_old + jnp.dot(kw.T.astype(BF16), vb, preferred_element_type=F32)
            n_st[h] = decay * n_old + jnp.sum(kw, axis=0, keepdims=True)
            m_st[h] = m_new

            hn = _rms(hv, hn_ref[:, h * ML_DV:(h + 1) * ML_DV])
            og = o_ref[r0:r0 + L, h * ML_DV:(h + 1) * ML_DV].astype(F32)
            y_ref[r0:r0 + L, h * ML_DV:(h + 1) * ML_DV] = (hn * jax.nn.sigmoid(og)).astype(BF16)

    xbuf[0:CONV_PAD, :] = xbuf[ML_BLOCK:ML_BLOCK + CONV_PAD, :]


def _mlstm(proj, small, conv_w, b_gates, head_norm, B, S):
    T = B * S
    ns = S // ML_BLOCK
    return pl.pallas_call(
        _mlstm_body,
        grid=(B, ns),
        in_specs=[
            pl.BlockSpec((ML_BLOCK, D_MODEL), lambda b, s: (b * ns + s, COL_MLQK)),
            pl.BlockSpec((ML_BLOCK, D_MODEL), lambda b, s: (b * ns + s, COL_MLV)),
            pl.BlockSpec((ML_BLOCK, D_MODEL), lambda b, s: (b * ns + s, COL_MLO)),
            pl.BlockSpec((ML_BLOCK, LANES), lambda b, s: (b * ns + s, 0)),
            pl.BlockSpec((ML_CONV, D_MODEL), lambda b, s: (0, 0)),
            pl.BlockSpec((1, LANES), lambda b, s: (0, 0)),
            pl.BlockSpec((1, D_MODEL), lambda b, s: (0, 0)),
        ],
        out_specs=pl.BlockSpec((ML_BLOCK, D_MODEL), lambda b, s: (b * ns + s, 0)),
        out_shape=jax.ShapeDtypeStruct((T, D_MODEL), BF16),
        scratch_shapes=[
            pltpu.VMEM((ML_BLOCK + CONV_PAD, D_MODEL), F32),
            pltpu.VMEM((ML_HEADS, ML_DQK, ML_DV), F32),
            pltpu.VMEM((ML_HEADS, 1, ML_DQK), F32),
            pltpu.VMEM((ML_HEADS, 1, 1), F32),
        ],
        compiler_params=_cparams(("parallel", "arbitrary")),
        name="mlstm",
    )(proj, proj, proj, small, conv_w, b_gates, head_norm)


FX_T = 512


def _fox_gate_body(g_ref, b_ref, o_ref):
    S = g_ref.shape[0]
    nk = S // FX_T
    row = lax.broadcasted_iota(I32, (FX_T, FX_T), 0)
    col = lax.broadcasted_iota(I32, (FX_T, FX_T), 1)
    tri = (row >= col).astype(F32)
    carry = jnp.zeros((1, LANES), F32)
    for blk in range(nk):
        lf = _log_sigmoid(g_ref[blk * FX_T:(blk + 1) * FX_T, :] + b_ref[...])
        cum = jnp.dot(tri, lf, precision=lax.Precision.HIGHEST, preferred_element_type=F32) + carry
        carry = cum[FX_T - 1:FX_T, :]
        cum_t = cum.T
        for h in range(FX_HEADS):
            o_ref[h, blk] = cum_t[2 * ML_HEADS + h:2 * ML_HEADS + h + 1, :]


def _fox_gate(small, b_fx, B, S):
    nk = S // FX_T
    return pl.pallas_call(
        _fox_gate_body,
        grid=(B,),
        in_specs=[
            pl.BlockSpec((S, LANES), lambda b: (b, 0)),
            pl.BlockSpec((1, LANES), lambda b: (0, 0)),
        ],
        out_specs=pl.BlockSpec((None, FX_HEADS, nk, 1, FX_T), lambda b: (b, 0, 0, 0, 0)),
        out_shape=jax.ShapeDtypeStruct((B, FX_HEADS, nk, 1, FX_T), F32),
        compiler_params=_cparams(("parallel",)),
        name="fox_gate",
    )(small, b_fx)


def _fox_attn_body(q_ref, k_ref, v_ref, c_ref, o_ref, m_ref, l_ref, acc_ref):
    i = pl.program_id(2)
    q = (q_ref[...].astype(F32) * (FX_DH ** -0.5)).astype(BF16)
    c_q = c_ref[i][:, 0:1]
    m_ref[...] = jnp.full(m_ref.shape, -jnp.inf, F32)
    l_ref[...] = jnp.zeros(l_ref.shape, F32)
    acc_ref[...] = jnp.zeros(acc_ref.shape, F32)

    def block(j, masked):
        r0 = pl.multiple_of(j * FX_T, FX_T)
        s = lax.dot_general(q, k_ref[pl.ds(r0, FX_T), :], (((1,), (1,)), ((), ())),
                            preferred_element_type=F32)
        s = s + (c_q - c_ref[j])
        if masked:
            row = lax.broadcasted_iota(I32, (FX_T, FX_T), 0)
            col = lax.broadcasted_iota(I32, (FX_T, FX_T), 1)
            s = jnp.where(row >= col, s, -jnp.inf)
        m_old = m_ref[...]
        m_new = jnp.maximum(m_old, jnp.max(s, axis=-1, keepdims=True))
        alpha = jnp.exp(m_old - m_new)
        p = jnp.exp(s - m_new)
        l_ref[...] = alpha * l_ref[...] + jnp.sum(p, axis=-1, keepdims=True)
        acc_ref[...] = alpha * acc_ref[...] + jnp.dot(p.astype(BF16), v_ref[pl.ds(r0, FX_T), :],
                                                      preferred_element_type=F32)
        m_ref[...] = m_new

    def loop_body(j, carry):
        block(j, False)
        return carry

    lax.fori_loop(0, i, loop_body, 0)
    block(i, True)
    o_ref[...] = (acc_ref[...] / l_ref[...]).astype(BF16)


def _fox_attn(proj, c_rows, B, S):
    T = B * S
    nq = S // FX_T
    hq = COL_FXQ * (D_MODEL // FX_DH)
    hk = COL_FXK * (D_MODEL // FX_DH)
    hv = COL_FXV * (D_MODEL // FX_DH)
    proj3 = proj.reshape(B, S, N_BIG)
    out = pl.pallas_call(
        _fox_attn_body,
        grid=(B, FX_HEADS, nq),
        in_specs=[
            pl.BlockSpec((None, FX_T, FX_DH), lambda b, h, i: (b, i, hq + h)),
            pl.BlockSpec((None, S, FX_DH), lambda b, h, i: (b, 0, hk + h)),
            pl.BlockSpec((None, S, FX_DH), lambda b, h, i: (b, 0, hv + h)),
            pl.BlockSpec((None, None, nq, 1, FX_T), lambda b, h, i: (b, h, 0, 0, 0)),
        ],
        out_specs=pl.BlockSpec((None, FX_T, FX_DH), lambda b, h, i: (b, i, h)),
        out_shape=jax.ShapeDtypeStruct((B, S, D_MODEL), BF16),
        scratch_shapes=[
            pltpu.VMEM((FX_T, 1), F32),
            pltpu.VMEM((FX_T, 1), F32),
            pltpu.VMEM((FX_T, FX_DH), F32),
        ],
        compiler_params=_cparams(("parallel", "parallel", "arbitrary")),
        name="fox_attn",
    )(proj3, proj3, proj3, c_rows)
    return out.reshape(T, D_MODEL)


def _memkv_body(x_ref, g_ref, w_ref, o_ref):
    hb = _rms(x_ref[...], g_ref[...]).astype(BF16)
    o_ref[...] = jnp.dot(hb, w_ref[...], preferred_element_type=F32).astype(BF16)


def _memkv(mem2d, g, w_kv):
    R = mem2d.shape[0]
    tm = min(512, R)
    N = w_kv.shape[1]
    return pl.pallas_call(
        _memkv_body,
        grid=(R // tm,),
        in_specs=[
            pl.BlockSpec((tm, D_MODEL), lambda i: (i, 0)),
            pl.BlockSpec((1, D_MODEL), lambda i: (0, 0)),
            pl.BlockSpec((D_MODEL, N), lambda i: (0, 0)),
        ],
        out_specs=pl.BlockSpec((tm, N), lambda i: (i, 0)),
        out_shape=jax.ShapeDtypeStruct((R, N), BF16),
        compiler_params=_cparams(("parallel",)),
        name="memkv",
    )(mem2d, g, w_kv)


CA_TQ = 512


def _memattn_body(q_ref, k_ref, v_ref, o_ref):
    scale = CA_DH ** -0.5
    for h in range(CA_HEADS):
        sl = slice(h * CA_DH, (h + 1) * CA_DH)
        s = lax.dot_general(q_ref[:, sl], k_ref[:, sl], (((1,), (1,)), ((), ())),
                            preferred_element_type=F32) * scale
        p = jnp.exp(s - jnp.max(s, axis=-1, keepdims=True))
        l = jnp.sum(p, axis=-1, keepdims=True)
        o = jnp.dot(p.astype(BF16), v_ref[:, sl], preferred_element_type=F32) / l
        o_ref[:, sl] = o.astype(BF16)


def _memattn(proj, kv, B, S, M):
    T = B * S
    nq = S // CA_TQ
    kv3 = kv.reshape(B, M, 2 * D_MODEL)
    return pl.pallas_call(
        _memattn_body,
        grid=(B, nq),
        in_specs=[
            pl.BlockSpec((CA_TQ, D_MODEL), lambda b, i: (b * nq + i, COL_CAQ)),
            pl.BlockSpec((None, M, D_MODEL), lambda b, i: (b, 0, 0)),
            pl.BlockSpec((None, M, D_MODEL), lambda b, i: (b, 0, 1)),
        ],
        out_specs=pl.BlockSpec((CA_TQ, D_MODEL), lambda b, i: (b * nq + i, 0)),
        out_shape=jax.ShapeDtypeStruct((T, D_MODEL), BF16),
        compiler_params=_cparams(("parallel", "arbitrary")),
        name="memattn",
    )(proj, kv3, kv3)


MERGE_TM = 512


def _merge_body(y0_ref, y1_ref, y2_ref, g0_ref, g1_ref, g2_ref, x_ref, wb_ref, wo_ref, o_ref):
    merged = None
    for n, (y_ref, g_ref) in enumerate(((y0_ref, g0_ref), (y1_ref, g1_ref), (y2_ref, g2_ref))):
        p = jnp.dot(y_ref[...], wb_ref[n], preferred_element_type=F32)
        t = jax.nn.sigmoid(g_ref[...].astype(F32)) * p
        merged = t if merged is None else merged + t
    o_ref[...] = x_ref[...] + jnp.dot(merged.astype(BF16), wo_ref[...], preferred_element_type=F32)


def _merge(y_ml, y_fx, y_ca, proj, x2d, w_branch, w_out):
    T = x2d.shape[0]
    tm = MERGE_TM
    row = lambda i: (i, 0)
    return pl.pallas_call(
        _merge_body,
        grid=(T // tm,),
        in_specs=[
            pl.BlockSpec((tm, D_MODEL), row),
            pl.BlockSpec((tm, D_MODEL), row),
            pl.BlockSpec((tm, D_MODEL), row),
            pl.BlockSpec((tm, D_MODEL), lambda i: (i, COL_GATE0)),
            pl.BlockSpec((tm, D_MODEL), lambda i: (i, COL_GATE0 + 1)),
            pl.BlockSpec((tm, D_MODEL), lambda i: (i, COL_GATE0 + 2)),
            pl.BlockSpec((tm, D_MODEL), row),
            pl.BlockSpec((3, D_MODEL, D_MODEL), lambda i: (0, 0, 0)),
            pl.BlockSpec((D_MODEL, D_MODEL), lambda i: (0, 0)),
        ],
        out_specs=pl.BlockSpec((tm, D_MODEL), row),
        out_shape=jax.ShapeDtypeStruct((T, D_MODEL), F32),
        compiler_params=_cparams(("parallel",)),
        name="merge",
    )(y_ml, y_fx, y_ca, proj, proj, proj, x2d, w_branch, w_out)


ROUTER_TM = 512


def _router_body(x_ref, g_ref, wr_ref, br_ref, hp_ref, ri_ref, rw_ref, cnt_ref, carry_ref):
    tm = ROUTER_TM

    @pl.when(pl.program_id(0) == 0)
    def _():
        carry_ref[...] = jnp.zeros_like(carry_ref)

    h = _rms(x_ref[...], g_ref[...])
    hp_ref[...] = _pack_rows(h)
    logits = jnp.dot(h.astype(BF16), wr_ref[...], preferred_element_type=F32) + br_ref[...]
    lane = lax.broadcasted_iota(I32, (tm, LANES), 1)
    lane_f = lane.astype(F32)

    work = logits
    onehot_sum = jnp.zeros((tm, LANES), F32)
    vals, sels, idxs = [], [], []
    for _ in range(TOP_K):
        mx = jnp.max(work, axis=-1, keepdims=True)
        idx = jnp.min(jnp.where(work == mx, lane_f, float(LANES)), axis=-1, keepdims=True)
        sel = lane_f == idx
        onehot_sum = onehot_sum + sel.astype(F32)
        work = jnp.where(sel, -jnp.inf, work)
        vals.append(mx)
        sels.append(sel)
        idxs.append(idx)
    exps = [jnp.exp(v - vals[0]) for v in vals]
    total = exps[0] + exps[1] + exps[2] + exps[3]

    row = lax.broadcasted_iota(I32, (tm, tm), 0)
    col = lax.broadcasted_iota(I32, (tm, tm), 1)
    strict = (row > col).astype(BF16)
    before = jnp.dot(strict, onehot_sum.astype(BF16), preferred_element_type=F32) + carry_ref[...]
    carry_ref[...] = carry_ref[...] + jnp.sum(onehot_sum, axis=0, keepdims=True)
    cnt_ref[...] = carry_ref[...]

    ri = jnp.zeros((tm, LANES), I32)
    rw = jnp.zeros((tm, LANES), F32)
    for k in range(TOP_K):
        rank = jnp.sum(jnp.where(sels[k], before, 0.0), axis=-1, keepdims=True)
        ri = jnp.where(lane == k, idxs[k].astype(I32), ri)
        ri = jnp.where(lane == TOP_K + k, rank.astype(I32), ri)
        rw = jnp.where(lane == k, exps[k] / total, rw)
    ri_ref[...] = ri
    rw_ref[...] = rw


def _router(x2, g, w_router, b_router):
    T = x2.shape[0]
    tm = ROUTER_TM
    return pl.pallas_call(
        _router_body,
        grid=(T // tm,),
        in_specs=[
            pl.BlockSpec((tm, D_MODEL), lambda i: (i, 0)),
            pl.BlockSpec((1, D_MODEL), lambda i: (0, 0)),
            pl.BlockSpec((D_MODEL, LANES), lambda i: (0, 0)),
            pl.BlockSpec((1, LANES), lambda i: (0, 0)),
        ],
        out_specs=[
            pl.BlockSpec((tm, HALF), lambda i: (i, 0)),
            pl.BlockSpec((tm, LANES), lambda i: (i, 0)),
            pl.BlockSpec((tm, LANES), lambda i: (i, 0)),
            pl.BlockSpec((1, LANES), lambda i: (0, 0)),
        ],
        out_shape=[
            jax.ShapeDtypeStruct((T, HALF), I32),
            jax.ShapeDtypeStruct((T, LANES), I32),
            jax.ShapeDtypeStruct((T, LANES), F32),
            jax.ShapeDtypeStruct((1, LANES), F32),
        ],
        scratch_shapes=[pltpu.VMEM((1, LANES), F32)],
        compiler_params=_cparams(("arbitrary",)),
        name="router",
    )(x2, g, w_router, b_router)


DISPATCH_TM = 512
EXPERT_TM = 512


def _dispatch_body(dest_ref, hp_ref, xs_in_ref, xs_ref, sem):
    del xs_in_ref

    def row_copy(r, d):
        return pltpu.make_async_copy(hp_ref.at[pl.ds(r, 1)], xs_ref.at[pl.ds(d, 1)], sem)

    def start(r, carry):
        for k in range(TOP_K):
            row_copy(r, dest_ref[r * TOP_K + k]).start()
        return carry

    def wait(r, carry):
        for k in range(TOP_K):
            row_copy(r, dest_ref[r * TOP_K + k]).wait()
        return carry

    lax.fori_loop(0, DISPATCH_TM, start, 0)
    lax.fori_loop(0, DISPATCH_TM, wait, 0)


def _dispatch(dest, hp, n_rows):
    T = hp.shape[0]
    tm = DISPATCH_TM
    xs0 = jnp.zeros((n_rows, HALF), I32)
    return pl.pallas_call(
        _dispatch_body,
        grid=(T // tm,),
        in_specs=[
            pl.BlockSpec((tm * TOP_K,), lambda i: (i,), memory_space=pltpu.SMEM),
            pl.BlockSpec((tm, HALF), lambda i: (i, 0)),
            pl.BlockSpec(memory_space=pl.ANY),
        ],
        out_specs=pl.BlockSpec(memory_space=pl.ANY),
        out_shape=jax.ShapeDtypeStruct((n_rows, HALF), I32),
        scratch_shapes=[pltpu.SemaphoreType.DMA(())],
        input_output_aliases={2: 0},
        compiler_params=_cparams(("arbitrary",)),
        name="dispatch",
    )(dest, hp, xs0)


FF_CHUNK = 512


def _expert_body(te_ref, nv_ref, x_ref, w1_ref, b1_ref, w2_ref, b2_ref, y_ref):
    del te_ref

    @pl.when(pl.program_id(0) < nv_ref[0])
    def _():
        lo, hi = _unpack_rows(x_ref[...])
        xlo = lo.astype(BF16)
        xhi = hi.astype(BF16)
        acc = jnp.zeros((EXPERT_TM, D_MODEL), F32) + b2_ref[...]
        for c in range(D_FF // FF_CHUNK):
            def up(off):
                cs = slice(off + c * FF_CHUNK, off + (c + 1) * FF_CHUNK)
                return (jnp.dot(xlo, w1_ref[0:HALF, cs], preferred_element_type=F32)
                        + jnp.dot(xhi, w1_ref[HALF:D_MODEL, cs], preferred_element_type=F32)
                        + b1_ref[:, cs])
            g = jnp.minimum(up(0), SWIGLU_LIMIT)
            lin = jnp.clip(up(D_FF), -SWIGLU_LIMIT, SWIGLU_LIMIT)
            a = g * jax.nn.sigmoid(SWIGLU_ALPHA * g) * (lin + 1.0)
            acc = acc + jnp.dot(a.astype(BF16), w2_ref[c * FF_CHUNK:(c + 1) * FF_CHUNK, :],
                                preferred_element_type=F32)
        y_ref[...] = _pack_rows(acc)


def _experts(tile_expert, n_valid, xs, w1, b1, w2, b2):
    n_rows = xs.shape[0]
    tm = EXPERT_TM
    n_tiles = n_rows // tm
    row = lambda i, te, nv: (jnp.minimum(i, nv[0] - 1), 0)
    grid_spec = pltpu.PrefetchScalarGridSpec(
        num_scalar_prefetch=2,
        grid=(n_tiles,),
        in_specs=[
            pl.BlockSpec((tm, HALF), row),
            pl.BlockSpec((None, D_MODEL, 2 * D_FF), lambda i, te, nv: (te[i], 0, 0)),
            pl.BlockSpec((None, 1, 2 * D_FF), lambda i, te, nv: (te[i], 0, 0)),
            pl.BlockSpec((None, D_FF, D_MODEL), lambda i, te, nv: (te[i], 0, 0)),
            pl.BlockSpec((None, 1, D_MODEL), lambda i, te, nv: (te[i], 0, 0)),
        ],
        out_specs=pl.BlockSpec((tm, HALF), row),
    )
    return pl.pallas_call(
        _expert_body,
        grid_spec=grid_spec,
        out_shape=jax.ShapeDtypeStruct((n_rows, HALF), I32),
        compiler_params=_cparams(("arbitrary",)),
        name="experts",
    )(tile_expert, n_valid, xs, w1, b1, w2, b2)


COMBINE_TM = 256


def _combine_body(dcur_ref, dnext_ref, rw_ref, x_ref, g_ref, ys_ref, o_ref, buf, sem):
    tm = COMBINE_TM
    i = pl.program_id(0)
    n = pl.num_programs(0)

    def row_copy(d, slot, k, r):
        return pltpu.make_async_copy(ys_ref.at[pl.ds(d, 1)], buf.at[slot, k, pl.ds(r, 1)], sem.at[slot])

    def gather(dref, slot, wait):
        def body(r, carry):
            for k in range(TOP_K):
                cp = row_copy(dref[r * TOP_K + k], slot, k, r)
                if wait:
                    cp.wait()
                else:
                    cp.start()
            return carry
        lax.fori_loop(0, tm, body, 0)

    @pl.when(i == 0)
    def _():
        gather(dcur_ref, 0, False)

    @pl.when(i + 1 < n)
    def _():
        gather(dnext_ref, (i + 1) % 2, False)

    slot = i % 2
    gather(dcur_ref, slot, True)
    acc = x_ref[...]
    rw = rw_ref[...]
    for k in range(TOP_K):
        lo, hi = _unpack_rows(buf[slot, k])
        acc = acc + rw[:, k:k + 1] * jnp.concatenate([lo, hi], axis=-1)
    o_ref[...] = _rms(acc, g_ref[...])


def _combine(dest, rw, x2, g, ys):
    T = x2.shape[0]
    tm = COMBINE_TM
    n = T // tm
    return pl.pallas_call(
        _combine_body,
        grid=(n,),
        in_specs=[
            pl.BlockSpec((tm * TOP_K,), lambda i: (i,), memory_space=pltpu.SMEM),
            pl.BlockSpec((tm * TOP_K,), lambda i: (jnp.minimum(i + 1, n - 1),), memory_space=pltpu.SMEM),
            pl.BlockSpec((tm, LANES), lambda i: (i, 0)),
            pl.BlockSpec((tm, D_MODEL), lambda i: (i, 0)),
            pl.BlockSpec((1, D_MODEL), lambda i: (0, 0)),
            pl.BlockSpec(memory_space=pl.ANY),
        ],
        out_specs=pl.BlockSpec((tm, D_MODEL), lambda i: (i, 0)),
        out_shape=jax.ShapeDtypeStruct((T, D_MODEL), F32),
        scratch_shapes=[
            pltpu.VMEM((2, TOP_K, tm, HALF), I32),
            pltpu.SemaphoreType.DMA((2,)),
        ],
        compiler_params=_cparams(("arbitrary",)),
        name="combine",
    )(dest, dest, rw, x2, g, ys)


def _pad_lanes(v, fill=0.0):
    v = v.reshape(1, -1).astype(F32)
    return jnp.pad(v, ((0, 0), (0, LANES - v.shape[1])), constant_values=fill)


def _layer(x2d, mem2d, B, S, M, norm_mix, w_in, b_ml_gates, conv_ml, ml_head_norm, b_fx_gate, norm_mem,
           w_mem_kv, w_branch, w_out, norm_moe, w_router, b_router, w_exp_in, b_exp_in, w_exp_out,
           b_exp_out, norm_out):
    T = B * S
    w_big = jnp.concatenate([w_in[:, 0:2048], w_in[:, 2056:3080], w_in[:, 3080:6152], w_in[:, 6160:7184],
                             w_in[:, 7184:10256]], axis=1).astype(BF16)
    w_small = jnp.concatenate([w_in[:, 2048:2056], w_in[:, 6152:6160]], axis=1)
    w_small = jnp.pad(w_small, ((0, 0), (0, LANES - w_small.shape[1]))).astype(BF16)
    row = lambda v: v.reshape(1, -1).astype(F32)

    proj, small = _inproj(x2d, row(norm_mix), w_big, w_small)

    y_ml = _mlstm(proj, small, conv_ml.astype(F32), _pad_lanes(b_ml_gates), row(ml_head_norm), B, S)

    b_fx = jnp.pad(b_fx_gate.reshape(1, -1).astype(F32), ((0, 0), (2 * ML_HEADS, LANES - 2 * ML_HEADS - FX_HEADS)))
    c_rows = _fox_gate(small, b_fx, B, S)
    y_fx = _fox_attn(proj, c_rows, B, S)

    kv = _memkv(mem2d, row(norm_mem), w_mem_kv.astype(BF16))
    y_ca = _memattn(proj, kv, B, S, M)

    x2 = _merge(y_ml, y_fx, y_ca, proj, x2d, w_branch.astype(BF16), w_out.astype(BF16))

    w_r = jnp.pad(w_router, ((0, 0), (0, LANES - N_EXPERTS))).astype(BF16)
    hp, ri, rw, cnt = _router(x2, row(norm_moe), w_r, _pad_lanes(b_router, fill=-1e30))

    tm = EXPERT_TM
    n_tiles = (T * TOP_K) // tm + N_EXPERTS
    counts = cnt[0, :N_EXPERTS].astype(I32)
    padded = ((counts + tm - 1) // tm) * tm
    gend = jnp.cumsum(padded)
    gstart = gend - padded
    dest = (gstart[ri[:, 0:TOP_K]] + ri[:, TOP_K:2 * TOP_K]).reshape(-1)
    n_valid = gend[-1] // tm
    tile_ids = jnp.arange(n_tiles, dtype=I32)
    tile_e = jnp.minimum(jnp.searchsorted(gend, tile_ids * tm, side="right"), N_EXPERTS - 1).astype(I32)
    tile_e = jnp.where(tile_ids < n_valid, tile_e, tile_e[jnp.maximum(n_valid - 1, 0)])

    xs = _dispatch(dest, hp, n_tiles * tm)
    ys = _experts(tile_e, n_valid.reshape(1).astype(I32), xs, w_exp_in.astype(BF16),
                  b_exp_in.reshape(N_EXPERTS, 1, -1).astype(F32), w_exp_out.astype(BF16),
                  b_exp_out.reshape(N_EXPERTS, 1, -1).astype(F32))
    return _combine(dest, rw, x2, row(norm_out), ys)


def kernel(x, mem, norm_mix, w_in, b_ml_gates, conv_ml, ml_head_norm, b_fx_gate, norm_mem, w_mem_kv, w_branch,
           w_out, norm_moe, w_router, b_router, w_exp_in, b_exp_in, w_exp_out, b_exp_out, norm_final):
    B, S, D = x.shape
    M = mem.shape[1]
    depth = norm_mix.shape[0]
    assert depth == 1, "the combine kernel fuses the final norm, so exactly one layer is supported"
    assert D == D_MODEL and S % ML_BLOCK == 0 and S % FX_T == 0 and S % CA_TQ == 0
    out = _layer(x.reshape(B * S, D), mem.reshape(B * M, D), B, S, M, norm_mix[0], w_in[0], b_ml_gates[0],
                 conv_ml[0], ml_head_norm[0], b_fx_gate[0], norm_mem[0], w_mem_kv[0], w_branch[0], w_out[0],
                 norm_moe[0], w_router[0], b_router[0], w_exp_in[0], b_exp_in[0], w_exp_out[0], b_exp_out[0],
                 norm_final)
    return out.reshape(B, S, D)
```

```python
import functools

import jax
import jax.numpy as jnp
from jax import lax
from jax.experimental import pallas as pl
from jax.experimental.pallas import tpu as pltpu

F32 = jnp.float32
BF16 = jnp.bfloat16
I32 = jnp.int32

D_MODEL = 1024
N_MEM_HEADS = 4
ML_HEADS = 4
ML_DQK = 128
ML_DV = 256
ML_CONV = 4
FX_HEADS = 8
FX_DH = 128
CA_HEADS = 4
CA_DH = 256
N_EXPERTS = 32
TOP_K = 4
D_FF = D_MODEL
SWIGLU_LIMIT = 7.0
SWIGLU_ALPHA = 1.702
EPS = 1e-5
LANES = 128
HALF = D_MODEL // 2
HI_MASK = -65536

COL_MLQK, COL_MLV, COL_MLO, COL_FXQ, COL_FXK, COL_FXV, COL_CAQ, COL_GATE0 = 0, 1, 2, 3, 4, 5, 6, 7
N_BIG = 10 * D_MODEL

VMEM_LIMIT = 56 * 1024 * 1024


def _cparams(sem):
    return pltpu.CompilerParams(dimension_semantics=sem, vmem_limit_bytes=VMEM_LIMIT)


def _rms(x, g):
    return x * lax.rsqrt(jnp.mean(x * x, axis=-1, keepdims=True) + EPS) * g


def _log_sigmoid(x):
    return jnp.minimum(x, 0.0) - jnp.log1p(jnp.exp(-jnp.abs(x)))


def _pack_rows(y):
    bits = lax.bitcast_convert_type(y.astype(BF16).astype(F32), I32)
    return lax.shift_right_logical(bits[:, :HALF], 16) | (bits[:, HALF:] & HI_MASK)


def _unpack_rows(w):
    lo = lax.bitcast_convert_type(lax.shift_left(w, 16), F32)
    hi = lax.bitcast_convert_type(w & HI_MASK, F32)
    return lo, hi


def _inproj_body(x_ref, g_ref, w_ref, ws_ref, o_ref, os_ref, h_ref):
    @pl.when(pl.program_id(1) == 0)
    def _():
        hb = _rms(x_ref[...], g_ref[...]).astype(BF16)
        h_ref[...] = hb
        os_ref[...] = jnp.dot(hb, ws_ref[...], preferred_element_type=F32)

    o_ref[...] = jnp.dot(h_ref[...], w_ref[...], preferred_element_type=F32).astype(BF16)


def _inproj(x2d, g, w_big, w_small):
    T = x2d.shape[0]
    tm = min(1024, T)
    tn = 2048
    return pl.pallas_call(
        _inproj_body,
        grid=(T // tm, N_BIG // tn),
        in_specs=[
            pl.BlockSpec((tm, D_MODEL), lambda i, j: (i, 0)),
            pl.BlockSpec((1, D_MODEL), lambda i, j: (0, 0)),
            pl.BlockSpec((D_MODEL, tn), lambda i, j: (0, j)),
            pl.BlockSpec((D_MODEL, LANES), lambda i, j: (0, 0)),
        ],
        out_specs=[
            pl.BlockSpec((tm, tn), lambda i, j: (i, j)),
            pl.BlockSpec((tm, LANES), lambda i, j: (i, 0)),
        ],
        out_shape=[
            jax.ShapeDtypeStruct((T, N_BIG), BF16),
            jax.ShapeDtypeStruct((T, LANES), F32),
        ],
        scratch_shapes=[pltpu.VMEM((tm, D_MODEL), BF16)],
        compiler_params=_cparams(("parallel", "arbitrary")),
        name="inproj",
    )(x2d, g, w_big, w_small)


ML_BLOCK = 512
ML_CHUNK = 128
CONV_PAD = 8


def _mlstm_body(qk_ref, v_ref, o_ref, g_ref, cw_ref, bg_ref, hn_ref, y_ref, xbuf, c_st, n_st, m_st):
    L = ML_CHUNK

    @pl.when(pl.program_id(1) == 0)
    def _():
        xbuf[0:CONV_PAD, :] = jnp.zeros((CONV_PAD, D_MODEL), F32)
        c_st[...] = jnp.zeros_like(c_st)
        n_st[...] = jnp.zeros_like(n_st)
        m_st[...] = jnp.zeros_like(m_st)

    xbuf[CONV_PAD:CONV_PAD + ML_BLOCK, :] = qk_ref[...].astype(F32)
    cw = cw_ref[...]
    row = lax.broadcasted_iota(I32, (L, L), 0)
    col = lax.broadcasted_iota(I32, (L, L), 1)
    causal = row >= col
    tri = causal.astype(F32)
    bg = bg_ref[...]
    scale = ML_DQK ** -0.5

    for c in range(ML_BLOCK // L):
        r0 = c * L
        conv = cw[0:1, :] * xbuf[r0 + CONV_PAD - 3:r0 + CONV_PAD - 3 + L, :]
        for j in range(1, ML_CONV):
            s0 = r0 + CONV_PAD - 3 + j
            conv = conv + cw[j:j + 1, :] * xbuf[s0:s0 + L, :]
        act = conv * jax.nn.sigmoid(conv)

        gates = g_ref[r0:r0 + L, :] + bg
        cum = jnp.dot(tri, _log_sigmoid(gates), precision=lax.Precision.HIGHEST,
                      preferred_element_type=F32)
        gates_t = gates.T
        cum_t = cum.T
        for h in range(ML_HEADS):
            b_col = cum[:, ML_HEADS + h:ML_HEADS + h + 1]
            i_col = gates[:, h:h + 1]
            b_row = cum_t[ML_HEADS + h:ML_HEADS + h + 1, :]
            i_row = gates_t[h:h + 1, :]
            m_prev = m_st[h]
            dm = jnp.where(causal, b_col + (i_row - b_row), -jnp.inf)
            m_inter = b_col + m_prev
            m_t = jnp.maximum(jnp.max(dm, axis=-1, keepdims=True), m_inter)
            w_intra = jnp.exp(dm - m_t)
            w_inter = jnp.exp(m_inter - m_t)

            qh = act[:, h * ML_DQK:(h + 1) * ML_DQK] * scale
            kh = act[:, (ML_HEADS + h) * ML_DQK:(ML_HEADS + h + 1) * ML_DQK]
            qb = qh.astype(BF16)
            vb = v_ref[r0:r0 + L, h * ML_DV:(h + 1) * ML_DV]
            s = lax.dot_general(qb, kh.astype(BF16), (((1,), (1,)), ((), ())),
                                preferred_element_type=F32) * w_intra
            c_old = c_st[h]
            n_old = n_st[h]
            num = jnp.dot(s.astype(BF16), vb, preferred_element_type=F32) + w_inter * jnp.dot(
                qb, c_old.astype(BF16), preferred_element_type=F32)
            den = jnp.sum(s, axis=-1, keepdims=True) + w_inter * jnp.sum(qh * n_old, axis=-1, keepdims=True)
            hv = num / jnp.maximum(jnp.abs(den), jnp.exp(-m_t))

            m_new = m_t[L - 1:L, :]
            b_last = b_col[L - 1:L, :]
            wk = jnp.exp(b_last - b_col + i_col - m_new)
            decay = jnp.exp(b_last + m_prev - m_new)
            kw = kh * wk
            c_st[h] = decay * c_old + jnp.dot(kw.T.astype(BF16), vb, preferred_element_type=F32)
            n_st[h] = decay * n_old + jnp.sum(kw, axis=0, keepdims=True)
            m_st[h] = m_new

            hn = _rms(hv, hn_ref[:, h * ML_DV:(h + 1) * ML_DV])
            og = o_ref[r0:r0 + L, h * ML_DV:(h + 1) * ML_DV].astype(F32)
            y_ref[r0:r0 + L, h * ML_DV:(h + 1) * ML_DV] = (hn * jax.nn.sigmoid(og)).astype(BF16)

    xbuf[0:CONV_PAD, :] = xbuf[ML_BLOCK:ML_BLOCK + CONV_PAD, :]


def _mlstm(proj, small, conv_w, b_gates, head_norm, B, S):
    T = B * S
    ns = S // ML_BLOCK
    return pl.pallas_call(
        _mlstm_body,
        grid=(B, ns),
        in_specs=[
            pl.BlockSpec((ML_BLOCK, D_MODEL), lambda b, s: (b * ns + s, COL_MLQK)),
            pl.BlockSpec((ML_BLOCK, D_MODEL), lambda b, s: (b * ns + s, COL_MLV)),
            pl.BlockSpec((ML_BLOCK, D_MODEL), lambda b, s: (b * ns + s, COL_MLO)),
            pl.BlockSpec((ML_BLOCK, LANES), lambda b, s: (b * ns + s, 0)),
            pl.BlockSpec((ML_CONV, D_MODEL), lambda b, s: (0, 0)),
            pl.BlockSpec((1, LANES), lambda b, s: (0, 0)),
            pl.BlockSpec((1, D_MODEL), lambda b, s: (0, 0)),
        ],
        out_specs=pl.BlockSpec((ML_BLOCK, D_MODEL), lambda b, s: (b * ns + s, 0)),
        out_shape=jax.ShapeDtypeStruct((T, D_MODEL), BF16),
        scratch_shapes=[
            pltpu.VMEM((ML_BLOCK + CONV_PAD, D_MODEL), F32),
            pltpu.VMEM((ML_HEADS, ML_DQK, ML_DV), F32),
            pltpu.VMEM((ML_HEADS, 1, ML_DQK), F32),
            pltpu.VMEM((ML_HEADS, 1, 1), F32),
        ],
        compiler_params=_cparams(("parallel", "arbitrary")),
        name="mlstm",
    )(proj, proj, proj, small, conv_w, b_gates, head_norm)


FX_T = 512
FX_HP = 2
LOG2E = 1.4426950408889634


def _fox_gate_body(g_ref, b_ref, o_ref):
    S = g_ref.shape[0]
    nk = S // FX_T
    row = lax.broadcasted_iota(I32, (FX_T, FX_T), 0)
    col = lax.broadcasted_iota(I32, (FX_T, FX_T), 1)
    tri = (row >= col).astype(F32)
    carry = jnp.zeros((1, LANES), F32)
    for blk in range(nk):
        lf = _log_sigmoid(g_ref[blk * FX_T:(blk + 1) * FX_T, :] + b_ref[...])
        cum = jnp.dot(tri, lf, precision=lax.Precision.HIGHEST, preferred_element_type=F32) + carry
        carry = cum[FX_T - 1:FX_T, :]
        cum_t = (cum * LOG2E).T
        for h in range(FX_HEADS):
            o_ref[h, blk] = cum_t[2 * ML_HEADS + h:2 * ML_HEADS + h + 1, :]


def _fox_gate(small, b_fx, B, S):
    nk = S // FX_T
    return pl.pallas_call(
        _fox_gate_body,
        grid=(B,),
        in_specs=[
            pl.BlockSpec((S, LANES), lambda b: (b, 0)),
            pl.BlockSpec((1, LANES), lambda b: (0, 0)),
        ],
        out_specs=pl.BlockSpec((None, FX_HEADS, nk, 1, FX_T), lambda b: (b, 0, 0, 0, 0)),
        out_shape=jax.ShapeDtypeStruct((B, FX_HEADS, nk, 1, FX_T), F32),
        compiler_params=_cparams(("parallel",)),
        name="fox_gate",
    )(small, b_fx)


def _fox_attn_body(q_ref, k_ref, v_ref, c_ref, o_ref, m_ref, acc_ref):
    i = pl.program_id(2)
    ones_col = (lax.broadcasted_iota(I32, (FX_T, FX_DH), 1) == 0).astype(BF16)
    heads = []
    for hh in range(FX_HP):
        sl = slice(hh * FX_DH, (hh + 1) * FX_DH)
        q = (q_ref[:, sl].astype(F32) * (FX_DH ** -0.5 * LOG2E)).astype(BF16)
        heads.append((hh, sl, q, c_ref[hh, i][:, 0:1]))
    m_ref[...] = jnp.full(m_ref.shape, -jnp.inf, F32)
    acc_ref[...] = jnp.zeros(acc_ref.shape, F32)

    def block(j, masked):
        r0 = pl.multiple_of(j * FX_T, FX_T)
        for hh, sl, q, c_q in heads:
            s = lax.dot_general(q, k_ref[pl.ds(r0, FX_T), sl], (((1,), (1,)), ((), ())),
                                preferred_element_type=F32)
            s = s + (c_q - c_ref[hh, j])
            if masked:
                row = lax.broadcasted_iota(I32, (FX_T, FX_T), 0)
                col = lax.broadcasted_iota(I32, (FX_T, FX_T), 1)
                s = jnp.where(row >= col, s, -jnp.inf)
            m_old = m_ref[hh]
            m_new = jnp.maximum(m_old, jnp.max(s, axis=-1, keepdims=True))
            p = jnp.exp2(s - m_new).astype(BF16)
            v1 = jnp.concatenate([v_ref[pl.ds(r0, FX_T), sl], ones_col], axis=-1)
            acc_ref[hh] = jnp.exp2(m_old - m_new) * acc_ref[hh] + jnp.dot(p, v1, preferred_element_type=F32)
            m_ref[hh] = m_new

    def loop_body(j, carry):
        block(j, False)
        return carry

    lax.fori_loop(0, i, loop_body, 0)
    block(i, True)
    for hh, sl, _, _ in heads:
        acc = acc_ref[hh]
        o_ref[:, sl] = (acc[:, :FX_DH] / acc[:, FX_DH:FX_DH + 1]).astype(BF16)


def _fox_attn(proj, c_rows, B, S):
    T = B * S
    nq = S // FX_T
    wide = FX_HP * FX_DH
    cq = COL_FXQ * (D_MODEL // wide)
    ck = COL_FXK * (D_MODEL // wide)
    cv = COL_FXV * (D_MODEL // wide)
    proj3 = proj.reshape(B, S, N_BIG)
    out = pl.pallas_call(
        _fox_attn_body,
        grid=(B, FX_HEADS // FX_HP, nq),
        in_specs=[
            pl.BlockSpec((None, FX_T, wide), lambda b, h, i: (b, i, cq + h)),
            pl.BlockSpec((None, S, wide), lambda b, h, i: (b, 0, ck + h)),
            pl.BlockSpec((None, S, wide), lambda b, h, i: (b, 0, cv + h)),
            pl.BlockSpec((None, FX_HP, nq, 1, FX_T), lambda b, h, i: (b, h, 0, 0, 0)),
        ],
        out_specs=pl.BlockSpec((None, FX_T, wide), lambda b, h, i: (b, i, h)),
        out_shape=jax.ShapeDtypeStruct((B, S, D_MODEL), BF16),
        scratch_shapes=[
            pltpu.VMEM((FX_HP, FX_T, 1), F32),
            pltpu.VMEM((FX_HP, FX_T, 2 * FX_DH), F32),
        ],
        compiler_params=_cparams(("parallel", "parallel", "arbitrary")),
        name="fox_attn",
    )(proj3, proj3, proj3, c_rows)
    return out.reshape(T, D_MODEL)


def _memkv_body(x_ref, g_ref, w_ref, o_ref):
    hb = _rms(x_ref[...], g_ref[...]).astype(BF16)
    o_ref[...] = jnp.dot(hb, w_ref[...], preferred_element_type=F32).astype(BF16)


def _memkv(mem2d, g, w_kv):
    R = mem2d.shape[0]
    tm = min(512, R)
    N = w_kv.shape[1]
    return pl.pallas_call(
        _memkv_body,
        grid=(R // tm,),
        in_specs=[
            pl.BlockSpec((tm, D_MODEL), lambda i: (i, 0)),
            pl.BlockSpec((1, D_MODEL), lambda i: (0, 0)),
            pl.BlockSpec((D_MODEL, N), lambda i: (0, 0)),
        ],
        out_specs=pl.BlockSpec((tm, N), lambda i: (i, 0)),
        out_shape=jax.ShapeDtypeStruct((R, N), BF16),
        compiler_params=_cparams(("parallel",)),
        name="memkv",
    )(mem2d, g, w_kv)


CA_TQ = 512


def _memattn_body(q_ref, k_ref, v_ref, o_ref):
    scale = CA_DH ** -0.5
    for h in range(CA_HEADS):
        sl = slice(h * CA_DH, (h + 1) * CA_DH)
        s = lax.dot_general(q_ref[:, sl], k_ref[:, sl], (((1,), (1,)), ((), ())),
                            preferred_element_type=F32) * scale
        p = jnp.exp(s - jnp.max(s, axis=-1, keepdims=True))
        l = jnp.sum(p, axis=-1, keepdims=True)
        o = jnp.dot(p.astype(BF16), v_ref[:, sl], preferred_element_type=F32) / l
        o_ref[:, sl] = o.astype(BF16)


def _memattn(proj, kv, B, S, M):
    T = B * S
    nq = S // CA_TQ
    kv3 = kv.reshape(B, M, 2 * D_MODEL)
    return pl.pallas_call(
        _memattn_body,
        grid=(B, nq),
        in_specs=[
            pl.BlockSpec((CA_TQ, D_MODEL), lambda b, i: (b * nq + i, COL_CAQ)),
            pl.BlockSpec((None, M, D_MODEL), lambda b, i: (b, 0, 0)),
            pl.BlockSpec((None, M, D_MODEL), lambda b, i: (b, 0, 1)),
        ],
        out_specs=pl.BlockSpec((CA_TQ, D_MODEL), lambda b, i: (b * nq + i, 0)),
        out_shape=jax.ShapeDtypeStruct((T, D_MODEL), BF16),
        compiler_params=_cparams(("parallel", "arbitrary")),
        name="memattn",
    )(proj, kv3, kv3)


MERGE_TM = 512


def _merge_body(y0_ref, y1_ref, y2_ref, g0_ref, g1_ref, g2_ref, x_ref, wb_ref, wo_ref, o_ref):
    merged = None
    for n, (y_ref, g_ref) in enumerate(((y0_ref, g0_ref), (y1_ref, g1_ref), (y2_ref, g2_ref))):
        p = jnp.dot(y_ref[...], wb_ref[n], preferred_element_type=F32)
        t = jax.nn.sigmoid(g_ref[...].astype(F32)) * p
        merged = t if merged is None else merged + t
    o_ref[...] = x_ref[...] + jnp.dot(merged.astype(BF16), wo_ref[...], preferred_element_type=F32)


def _merge(y_ml, y_fx, y_ca, proj, x2d, w_branch, w_out):
    T = x2d.shape[0]
    tm = MERGE_TM
    row = lambda i: (i, 0)
    return pl.pallas_call(
        _merge_body,
        grid=(T // tm,),
        in_specs=[
            pl.BlockSpec((tm, D_MODEL), row),
            pl.BlockSpec((tm, D_MODEL), row),
            pl.BlockSpec((tm, D_MODEL), row),
            pl.BlockSpec((tm, D_MODEL), lambda i: (i, COL_GATE0)),
            pl.BlockSpec((tm, D_MODEL), lambda i: (i, COL_GATE0 + 1)),
            pl.BlockSpec((tm, D_MODEL), lambda i: (i, COL_GATE0 + 2)),
            pl.BlockSpec((tm, D_MODEL), row),
            pl.BlockSpec((3, D_MODEL, D_MODEL), lambda i: (0, 0, 0)),
            pl.BlockSpec((D_MODEL, D_MODEL), lambda i: (0, 0)),
        ],
        out_specs=pl.BlockSpec((tm, D_MODEL), row),
        out_shape=jax.ShapeDtypeStruct((T, D_MODEL), F32),
        compiler_params=_cparams(("parallel",)),
        name="merge",
    )(y_ml, y_fx, y_ca, proj, proj, proj, x2d, w_branch, w_out)


ROUTER_TM = 512


def _router_body(x_ref, g_ref, wr_ref, br_ref, hp_ref, ri_ref, rw_ref, cnt_ref, carry_ref):
    tm = ROUTER_TM

    @pl.when(pl.program_id(0) == 0)
    def _():
        carry_ref[...] = jnp.zeros_like(carry_ref)

    h = _rms(x_ref[...], g_ref[...])
    hp_ref[...] = _pack_rows(h)
    logits = jnp.dot(h.astype(BF16), wr_ref[...], preferred_element_type=F32) + br_ref[...]
    lane = lax.broadcasted_iota(I32, (tm, LANES), 1)
    lane_f = lane.astype(F32)

    work = logits
    onehot_sum = jnp.zeros((tm, LANES), F32)
    vals, sels, idxs = [], [], []
    for _ in range(TOP_K):
        mx = jnp.max(work, axis=-1, keepdims=True)
        idx = jnp.min(jnp.where(work == mx, lane_f, float(LANES)), axis=-1, keepdims=True)
        sel = lane_f == idx
        onehot_sum = onehot_sum + sel.astype(F32)
        work = jnp.where(sel, -jnp.inf, work)
        vals.append(mx)
        sels.append(sel)
        idxs.append(idx)
    exps = [jnp.exp(v - vals[0]) for v in vals]
    total = exps[0] + exps[1] + exps[2] + exps[3]

    row = lax.broadcasted_iota(I32, (tm, tm), 0)
    col = lax.broadcasted_iota(I32, (tm, tm), 1)
    strict = (row > col).astype(BF16)
    before = jnp.dot(strict, onehot_sum.astype(BF16), preferred_element_type=F32) + carry_ref[...]
    carry_ref[...] = carry_ref[...] + jnp.sum(onehot_sum, axis=0, keepdims=True)
    cnt_ref[...] = carry_ref[...]

    ri = jnp.zeros((tm, LANES), I32)
    rw = jnp.zeros((tm, LANES), F32)
    for k in range(TOP_K):
        rank = jnp.sum(jnp.where(sels[k], before, 0.0), axis=-1, keepdims=True)
        ri = jnp.where(lane == k, idxs[k].astype(I32), ri)
        ri = jnp.where(lane == TOP_K + k, rank.astype(I32), ri)
        rw = jnp.where(lane == k, exps[k] / total, rw)
    ri_ref[...] = ri
    rw_ref[...] = rw


def _router(x2, g, w_router, b_router):
    T = x2.shape[0]
    tm = ROUTER_TM
    return pl.pallas_call(
        _router_body,
        grid=(T // tm,),
        in_specs=[
            pl.BlockSpec((tm, D_MODEL), lambda i: (i, 0)),
            pl.BlockSpec((1, D_MODEL), lambda i: (0, 0)),
            pl.BlockSpec((D_MODEL, LANES), lambda i: (0, 0)),
            pl.BlockSpec((1, LANES), lambda i: (0, 0)),
        ],
        out_specs=[
            pl.BlockSpec((tm, HALF), lambda i: (i, 0)),
            pl.BlockSpec((tm, LANES), lambda i: (i, 0)),
            pl.BlockSpec((tm, LANES), lambda i: (i, 0)),
            pl.BlockSpec((1, LANES), lambda i: (0, 0)),
        ],
        out_shape=[
            jax.ShapeDtypeStruct((T, HALF), I32),
            jax.ShapeDtypeStruct((T, LANES), I32),
            jax.ShapeDtypeStruct((T, LANES), F32),
            jax.ShapeDtypeStruct((1, LANES), F32),
        ],
        scratch_shapes=[pltpu.VMEM((1, LANES), F32)],
        compiler_params=_cparams(("arbitrary",)),
        name="router",
    )(x2, g, w_router, b_router)


DISPATCH_TM = 512
EXPERT_TM = 512


def _dispatch_body(dest_ref, hp_ref, xs_in_ref, xs_ref, sem):
    del xs_in_ref

    def row_copy(r, d):
        return pltpu.make_async_copy(hp_ref.at[pl.ds(r, 1)], xs_ref.at[pl.ds(d, 1)], sem)

    def start(r, carry):
        for k in range(TOP_K):
            row_copy(r, dest_ref[r * TOP_K + k]).start(priority=k % 2)
        return carry

    def wait(r, carry):
        for k in range(TOP_K):
            row_copy(r, dest_ref[r * TOP_K + k]).wait()
        return carry

    lax.fori_loop(0, DISPATCH_TM, start, 0)
    lax.fori_loop(0, DISPATCH_TM, wait, 0)


def _dispatch(dest, hp, n_rows):
    T = hp.shape[0]
    tm = DISPATCH_TM
    xs0 = jnp.zeros((n_rows, HALF), I32)
    return pl.pallas_call(
        _dispatch_body,
        grid=(T // tm,),
        in_specs=[
            pl.BlockSpec((tm * TOP_K,), lambda i: (i,), memory_space=pltpu.SMEM),
            pl.BlockSpec((tm, HALF), lambda i: (i, 0)),
            pl.BlockSpec(memory_space=pl.ANY),
        ],
        out_specs=pl.BlockSpec(memory_space=pl.ANY),
        out_shape=jax.ShapeDtypeStruct((n_rows, HALF), I32),
        scratch_shapes=[pltpu.SemaphoreType.DMA(())],
        input_output_aliases={2: 0},
        compiler_params=_cparams(("arbitrary",)),
        name="dispatch",
    )(dest, hp, xs0)


FF_CHUNK = 512


def _expert_body(te_ref, nv_ref, x_ref, w1f_ref, b1_ref, w2f_ref, b2_ref, y_ref, w1_ref, w2_ref):
    i = pl.program_id(0)

    @pl.when(jnp.logical_or(i == 0, te_ref[i] != te_ref[jnp.maximum(i - 1, 0)]))
    def _():
        w1_ref[...] = w1f_ref[...].astype(BF16)
        w2_ref[...] = w2f_ref[...].astype(BF16)

    @pl.when(i < nv_ref[0])
    def _():
        lo, hi = _unpack_rows(x_ref[...])
        xlo = lo.astype(BF16)
        xhi = hi.astype(BF16)
        acc = jnp.zeros((EXPERT_TM, D_MODEL), F32) + b2_ref[...]
        for c in range(D_FF // FF_CHUNK):
            def up(off):
                cs = slice(off + c * FF_CHUNK, off + (c + 1) * FF_CHUNK)
                return (jnp.dot(xlo, w1_ref[0:HALF, cs], preferred_element_type=F32)
                        + jnp.dot(xhi, w1_ref[HALF:D_MODEL, cs], preferred_element_type=F32)
                        + b1_ref[:, cs])
            g = jnp.minimum(up(0), SWIGLU_LIMIT)
            lin = jnp.clip(up(D_FF), -SWIGLU_LIMIT, SWIGLU_LIMIT)
            a = g * jax.nn.sigmoid(SWIGLU_ALPHA * g) * (lin + 1.0)
            acc = acc + jnp.dot(a.astype(BF16), w2_ref[c * FF_CHUNK:(c + 1) * FF_CHUNK, :],
                                preferred_element_type=F32)
        y_ref[...] = _pack_rows(acc)


def _experts(tile_expert, n_valid, xs, w1, b1, w2, b2):
    n_rows = xs.shape[0]
    tm = EXPERT_TM
    n_tiles = n_rows // tm
    row = lambda i, te, nv: (jnp.minimum(i, nv[0] - 1), 0)
    grid_spec = pltpu.PrefetchScalarGridSpec(
        num_scalar_prefetch=2,
        grid=(n_tiles,),
        in_specs=[
            pl.BlockSpec((tm, HALF), row),
            pl.BlockSpec((None, D_MODEL, 2 * D_FF), lambda i, te, nv: (te[i], 0, 0)),
            pl.BlockSpec((None, 1, 2 * D_FF), lambda i, te, nv: (te[i], 0, 0)),
            pl.BlockSpec((None, D_FF, D_MODEL), lambda i, te, nv: (te[i], 0, 0)),
            pl.BlockSpec((None, 1, D_MODEL), lambda i, te, nv: (te[i], 0, 0)),
        ],
        out_specs=pl.BlockSpec((tm, HALF), row),
        scratch_shapes=[pltpu.VMEM((D_MODEL, 2 * D_FF), BF16), pltpu.VMEM((D_FF, D_MODEL), BF16)],
    )
    return pl.pallas_call(
        _expert_body,
        grid_spec=grid_spec,
        out_shape=jax.ShapeDtypeStruct((n_rows, HALF), I32),
        compiler_params=_cparams(("arbitrary",)),
        name="experts",
    )(tile_expert, n_valid, xs, w1, b1, w2, b2)


COMBINE_TM = 256


def _combine_body(dcur_ref, dnext_ref, rw_ref, x_ref, g_ref, ys_ref, o_ref, buf, sem):
    tm = COMBINE_TM
    i = pl.program_id(0)
    n = pl.num_programs(0)

    def row_copy(d, slot, k, r):
        return pltpu.make_async_copy(ys_ref.at[pl.ds(d, 1)], buf.at[slot, k, pl.ds(r, 1)], sem.at[slot])

    def gather(dref, slot, wait):
        def body(r, carry):
            for k in range(TOP_K):
                cp = row_copy(dref[r * TOP_K + k], slot, k, r)
                if wait:
                    cp.wait()
                else:
                    cp.start(priority=k % 2)
            return carry
        lax.fori_loop(0, tm, body, 0)

    @pl.when(i == 0)
    def _():
        gather(dcur_ref, 0, False)

    @pl.when(i + 1 < n)
    def _():
        gather(dnext_ref, (i + 1) % 2, False)

    slot = i % 2
    gather(dcur_ref, slot, True)
    acc = x_ref[...]
    rw = rw_ref[...]
    for k in range(TOP_K):
        lo, hi = _unpack_rows(buf[slot, k])
        acc = acc + rw[:, k:k + 1] * jnp.concatenate([lo, hi], axis=-1)
    o_ref[...] = _rms(acc, g_ref[...])


def _combine(dest, rw, x2, g, ys):
    T = x2.shape[0]
    tm = COMBINE_TM
    n = T // tm
    return pl.pallas_call(
        _combine_body,
        grid=(n,),
        in_specs=[
            pl.BlockSpec((tm * TOP_K,), lambda i: (i,), memory_space=pltpu.SMEM),
            pl.BlockSpec((tm * TOP_K,), lambda i: (jnp.minimum(i + 1, n - 1),), memory_space=pltpu.SMEM),
            pl.BlockSpec((tm, LANES), lambda i: (i, 0)),
            pl.BlockSpec((tm, D_MODEL), lambda i: (i, 0)),
            pl.BlockSpec((1, D_MODEL), lambda i: (0, 0)),
            pl.BlockSpec(memory_space=pl.ANY),
        ],
        out_specs=pl.BlockSpec((tm, D_MODEL), lambda i: (i, 0)),
        out_shape=jax.ShapeDtypeStruct((T, D_MODEL), F32),
        scratch_shapes=[
            pltpu.VMEM((2, TOP_K, tm, HALF), I32),
            pltpu.SemaphoreType.DMA((2,)),
        ],
        compiler_params=_cparams(("arbitrary",)),
        name="combine",
    )(dest, dest, rw, x2, g, ys)


def _pad_lanes(v, fill=0.0):
    v = v.reshape(1, -1).astype(F32)
    return jnp.pad(v, ((0, 0), (0, LANES - v.shape[1])), constant_values=fill)


def _layer(x2d, mem2d, B, S, M, norm_mix, w_in, b_ml_gates, conv_ml, ml_head_norm, b_fx_gate, norm_mem,
           w_mem_kv, w_branch, w_out, norm_moe, w_router, b_router, w_exp_in, b_exp_in, w_exp_out,
           b_exp_out, norm_out):
    T = B * S
    w_big = jnp.concatenate([w_in[:, 0:2048], w_in[:, 2056:3080], w_in[:, 3080:6152], w_in[:, 6160:7184],
                             w_in[:, 7184:10256]], axis=1).astype(BF16)
    w_small = jnp.concatenate([w_in[:, 2048:2056], w_in[:, 6152:6160]], axis=1)
    w_small = jnp.pad(w_small, ((0, 0), (0, LANES - w_small.shape[1]))).astype(BF16)
    row = lambda v: v.reshape(1, -1).astype(F32)

    proj, small = _inproj(x2d, row(norm_mix), w_big, w_small)

    y_ml = _mlstm(proj, small, conv_ml.astype(F32), _pad_lanes(b_ml_gates), row(ml_head_norm), B, S)

    b_fx = jnp.pad(b_fx_gate.reshape(1, -1).astype(F32), ((0, 0), (2 * ML_HEADS, LANES - 2 * ML_HEADS - FX_HEADS)))
    c_rows = _fox_gate(small, b_fx, B, S)
    y_fx = _fox_attn(proj, c_rows, B, S)

    kv = _memkv(mem2d, row(norm_mem), w_mem_kv.astype(BF16))
    y_ca = _memattn(proj, kv, B, S, M)

    x2 = _merge(y_ml, y_fx, y_ca, proj, x2d, w_branch.astype(BF16), w_out.astype(BF16))

    w_r = jnp.pad(w_router, ((0, 0), (0, LANES - N_EXPERTS))).astype(BF16)
    hp, ri, rw, cnt = _router(x2, row(norm_moe), w_r, _pad_lanes(b_router, fill=-1e30))

    tm = EXPERT_TM
    n_tiles = (T * TOP_K) // tm + N_EXPERTS
    counts = cnt[0, :N_EXPERTS].astype(I32)
    padded = ((counts + tm - 1) // tm) * tm
    gend = jnp.cumsum(padded)
    gstart = gend - padded
    expert_ids = jnp.arange(N_EXPERTS, dtype=I32)
    start_of = jnp.sum(jnp.where(ri[:, 0:TOP_K, None] == expert_ids, gstart, 0), axis=-1)
    dest = (start_of + ri[:, TOP_K:2 * TOP_K]).reshape(-1)
    n_valid = gend[-1] // tm
    tile_ids = jnp.arange(n_tiles, dtype=I32)
    last_tile = jnp.minimum(tile_ids, n_valid - 1)
    tile_e = jnp.minimum(jnp.sum((gend[None, :] <= last_tile[:, None] * tm).astype(I32), axis=1), N_EXPERTS - 1)

    xs = _dispatch(dest, hp, n_tiles * tm)
    ys = _experts(tile_e.astype(I32), n_valid.reshape(1).astype(I32), xs, w_exp_in.astype(F32),
                  b_exp_in.reshape(N_EXPERTS, 1, -1).astype(F32), w_exp_out.astype(F32),
                  b_exp_out.reshape(N_EXPERTS, 1, -1).astype(F32))
    return _combine(dest, rw, x2, row(norm_out), ys)


def kernel(x, mem, norm_mix, w_in, b_ml_gates, conv_ml, ml_head_norm, b_fx_gate, norm_mem, w_mem_kv, w_branch,
           w_out, norm_moe, w_router, b_router, w_exp_in, b_exp_in, w_exp_out, b_exp_out, norm_final):
    B, S, D = x.shape
    M = mem.shape[1]
    depth = norm_mix.shape[0]
    assert depth == 1, "the combine kernel fuses the final norm, so exactly one layer is supported"
    assert D == D_MODEL and S % ML_BLOCK == 0 and S % FX_T == 0 and S % CA_TQ == 0
    out = _layer(x.reshape(B * S, D), mem.reshape(B * M, D), B, S, M, norm_mix[0], w_in[0], b_ml_gates[0],
                 conv_ml[0], ml_head_norm[0], b_fx_gate[0], norm_mem[0], w_mem_kv[0], w_branch[0], w_out[0],
                 norm_moe[0], w_router[0], b_router[0], w_exp_in[0], b_exp_in[0], w_exp_out[0], b_exp_out[0],
                 norm_final)
    return out.reshape(B, S, D)
```

```python
import functools

import jax
import jax.numpy as jnp
from jax import lax
from jax.experimental import pallas as pl
from jax.experimental.pallas import tpu as pltpu

F32 = jnp.float32
BF16 = jnp.bfloat16
I32 = jnp.int32

D_MODEL = 1024
N_MEM_HEADS = 4
ML_HEADS = 4
ML_DQK = 128
ML_DV = 256
ML_CONV = 4
FX_HEADS = 8
FX_DH = 128
CA_HEADS = 4
CA_DH = 256
N_EXPERTS = 32
TOP_K = 4
D_FF = D_MODEL
SWIGLU_LIMIT = 7.0
SWIGLU_ALPHA = 1.702
EPS = 1e-5
LANES = 128
HALF = D_MODEL // 2
HI_MASK = -65536

COL_MLQK, COL_MLV, COL_MLO, COL_FXQ, COL_FXK, COL_FXV, COL_CAQ, COL_GATE0 = 0, 1, 2, 3, 4, 5, 6, 7
N_BIG = 10 * D_MODEL

VMEM_LIMIT = 56 * 1024 * 1024


def _cparams(sem):
    return pltpu.CompilerParams(dimension_semantics=sem, vmem_limit_bytes=VMEM_LIMIT)


def _rms(x, g):
    return x * lax.rsqrt(jnp.mean(x * x, axis=-1, keepdims=True) + EPS) * g


def _log_sigmoid(x):
    return jnp.minimum(x, 0.0) - jnp.log1p(jnp.exp(-jnp.abs(x)))


def _pack_rows(y):
    bits = lax.bitcast_convert_type(y.astype(BF16).astype(F32), I32)
    return lax.shift_right_logical(bits[:, :HALF], 16) | (bits[:, HALF:] & HI_MASK)


def _unpack_rows(w):
    lo = lax.bitcast_convert_type(lax.shift_left(w, 16), F32)
    hi = lax.bitcast_convert_type(w & HI_MASK, F32)
    return lo, hi


def _inproj_body(x_ref, g_ref, w_ref, ws_ref, o_ref, os_ref, h_ref):
    @pl.when(pl.program_id(1) == 0)
    def _():
        hb = _rms(x_ref[...], g_ref[...]).astype(BF16)
        h_ref[...] = hb
        os_ref[...] = jnp.dot(hb, ws_ref[...], preferred_element_type=F32)

    o_ref[...] = jnp.dot(h_ref[...], w_ref[...], preferred_element_type=F32).astype(BF16)


def _inproj(x2d, g, w_big, w_small):
    T = x2d.shape[0]
    tm = min(1024, T)
    tn = 2048
    return pl.pallas_call(
        _inproj_body,
        grid=(T // tm, N_BIG // tn),
        in_specs=[
            pl.BlockSpec((tm, D_MODEL), lambda i, j: (i, 0)),
            pl.BlockSpec((1, D_MODEL), lambda i, j: (0, 0)),
            pl.BlockSpec((D_MODEL, tn), lambda i, j: (0, j)),
            pl.BlockSpec((D_MODEL, LANES), lambda i, j: (0, 0)),
        ],
        out_specs=[
            pl.BlockSpec((tm, tn), lambda i, j: (i, j)),
            pl.BlockSpec((tm, LANES), lambda i, j: (i, 0)),
        ],
        out_shape=[
            jax.ShapeDtypeStruct((T, N_BIG), BF16),
            jax.ShapeDtypeStruct((T, LANES), F32),
        ],
        scratch_shapes=[pltpu.VMEM((tm, D_MODEL), BF16)],
        compiler_params=_cparams(("parallel", "arbitrary")),
        name="inproj",
    )(x2d, g, w_big, w_small)


ML_BLOCK = 512
ML_CHUNK = 128
CONV_PAD = 8


def _mlstm_body(qk_ref, v_ref, o_ref, g_ref, cw_ref, bg_ref, hn_ref, y_ref, xbuf, c_st, n_st, m_st):
    L = ML_CHUNK

    @pl.when(pl.program_id(1) == 0)
    def _():
        xbuf[0:CONV_PAD, :] = jnp.zeros((CONV_PAD, D_MODEL), F32)
        c_st[...] = jnp.zeros_like(c_st)
        n_st[...] = jnp.zeros_like(n_st)
        m_st[...] = jnp.zeros_like(m_st)

    xbuf[CONV_PAD:CONV_PAD + ML_BLOCK, :] = qk_ref[...].astype(F32)
    cw = cw_ref[...]
    row = lax.broadcasted_iota(I32, (L, L), 0)
    col = lax.broadcasted_iota(I32, (L, L), 1)
    causal = row >= col
    tri = causal.astype(F32)
    bg = bg_ref[...]
    scale = ML_DQK ** -0.5

    for c in range(ML_BLOCK // L):
        r0 = c * L
        conv = cw[0:1, :] * xbuf[r0 + CONV_PAD - 3:r0 + CONV_PAD - 3 + L, :]
        for j in range(1, ML_CONV):
            s0 = r0 + CONV_PAD - 3 + j
            conv = conv + cw[j:j + 1, :] * xbuf[s0:s0 + L, :]
        act = conv * jax.nn.sigmoid(conv)

        gates = g_ref[r0:r0 + L, :] + bg
        cum = jnp.dot(tri, _log_sigmoid(gates), precision=lax.Precision.HIGHEST,
                      preferred_element_type=F32)
        gates_t = gates.T
        cum_t = cum.T
        for h in range(ML_HEADS):
            b_col = cum[:, ML_HEADS + h:ML_HEADS + h + 1]
            i_col = gates[:, h:h + 1]
            b_row = cum_t[ML_HEADS + h:ML_HEADS + h + 1, :]
            i_row = gates_t[h:h + 1, :]
            m_prev = m_st[h]
            dm = jnp.where(causal, b_col + (i_row - b_row), -jnp.inf)
            m_inter = b_col + m_prev
            m_t = jnp.maximum(jnp.max(dm, axis=-1, keepdims=True), m_inter)
            w_intra = jnp.exp(dm - m_t)
            w_inter = jnp.exp(m_inter - m_t)

            qh = act[:, h * ML_DQK:(h + 1) * ML_DQK] * scale
            kh = act[:, (ML_HEADS + h) * ML_DQK:(ML_HEADS + h + 1) * ML_DQK]
            qb = qh.astype(BF16)
            vb = v_ref[r0:r0 + L, h * ML_DV:(h + 1) * ML_DV]
            s = lax.dot_general(qb, kh.astype(BF16), (((1,), (1,)), ((), ())),
                                preferred_element_type=F32) * w_intra
            c_old = c_st[h]
            n_old = n_st[h]
            num = jnp.dot(s.astype(BF16), vb, preferred_element_type=F32) + w_inter * jnp.dot(
                qb, c_old.astype(BF16), preferred_element_type=F32)
            den = jnp.sum(s, axis=-1, keepdims=True) + w_inter * jnp.sum(qh * n_old, axis=-1, keepdims=True)
            hv = num / jnp.maximum(jnp.abs(den), jnp.exp(-m_t))

            m_new = m_t[L - 1:L, :]
            b_last = b_col[L - 1:L, :]
            wk = jnp.exp(b_last - b_col + i_col - m_new)
            decay = jnp.exp(b_last + m_prev - m_new)
            kw = kh * wk
            c_st[h] = decay * c_old + jnp.dot(kw.T.astype(BF16), vb, preferred_element_type=F32)
            n_st[h] = decay * n_old + jnp.sum(kw, axis=0, keepdims=True)
            m_st[h] = m_new

            hn = _rms(hv, hn_ref[:, h * ML_DV:(h + 1) * ML_DV])
            og = o_ref[r0:r0 + L, h * ML_DV:(h + 1) * ML_DV].astype(F32)
            y_ref[r0:r0 + L, h * ML_DV:(h + 1) * ML_DV] = (hn * jax.nn.sigmoid(og)).astype(BF16)

    xbuf[0:CONV_PAD, :] = xbuf[ML_BLOCK:ML_BLOCK + CONV_PAD, :]


def _mlstm(proj, small, conv_w, b_gates, head_norm, B, S):
    T = B * S
    ns = S // ML_BLOCK
    return pl.pallas_call(
        _mlstm_body,
        grid=(B, ns),
        in_specs=[
            pl.BlockSpec((ML_BLOCK, D_MODEL), lambda b, s: (b * ns + s, COL_MLQK)),
            pl.BlockSpec((ML_BLOCK, D_MODEL), lambda b, s: (b * ns + s, COL_MLV)),
            pl.BlockSpec((ML_BLOCK, D_MODEL), lambda b, s: (b * ns + s, COL_MLO)),
            pl.BlockSpec((ML_BLOCK, LANES), lambda b, s: (b * ns + s, 0)),
            pl.BlockSpec((ML_CONV, D_MODEL), lambda b, s: (0, 0)),
            pl.BlockSpec((1, LANES), lambda b, s: (0, 0)),
            pl.BlockSpec((1, D_MODEL), lambda b, s: (0, 0)),
        ],
        out_specs=pl.BlockSpec((ML_BLOCK, D_MODEL), lambda b, s: (b * ns + s, 0)),
        out_shape=jax.ShapeDtypeStruct((T, D_MODEL), BF16),
        scratch_shapes=[
            pltpu.VMEM((ML_BLOCK + CONV_PAD, D_MODEL), F32),
            pltpu.VMEM((ML_HEADS, ML_DQK, ML_DV), F32),
            pltpu.VMEM((ML_HEADS, 1, ML_DQK), F32),
            pltpu.VMEM((ML_HEADS, 1, 1), F32),
        ],
        compiler_params=_cparams(("parallel", "arbitrary")),
        name="mlstm",
    )(proj, proj, proj, small, conv_w, b_gates, head_norm)


FX_T = 512
FX_HP = 2
LOG2E = 1.4426950408889634


def _fox_gate_body(g_ref, b_ref, o_ref):
    S = g_ref.shape[0]
    nk = S // FX_T
    row = lax.broadcasted_iota(I32, (FX_T, FX_T), 0)
    col = lax.broadcasted_iota(I32, (FX_T, FX_T), 1)
    tri = (row >= col).astype(F32)
    carry = jnp.zeros((1, LANES), F32)
    for blk in range(nk):
        lf = _log_sigmoid(g_ref[blk * FX_T:(blk + 1) * FX_T, :] + b_ref[...])
        cum = jnp.dot(tri, lf, precision=lax.Precision.HIGHEST, preferred_element_type=F32) + carry
        carry = cum[FX_T - 1:FX_T, :]
        cum_t = (cum * LOG2E).T
        for h in range(FX_HEADS):
            o_ref[h, blk] = cum_t[2 * ML_HEADS + h:2 * ML_HEADS + h + 1, :]


def _fox_gate(small, b_fx, B, S):
    nk = S // FX_T
    return pl.pallas_call(
        _fox_gate_body,
        grid=(B,),
        in_specs=[
            pl.BlockSpec((S, LANES), lambda b: (b, 0)),
            pl.BlockSpec((1, LANES), lambda b: (0, 0)),
        ],
        out_specs=pl.BlockSpec((None, FX_HEADS, nk, 1, FX_T), lambda b: (b, 0, 0, 0, 0)),
        out_shape=jax.ShapeDtypeStruct((B, FX_HEADS, nk, 1, FX_T), F32),
        compiler_params=_cparams(("parallel",)),
        name="fox_gate",
    )(small, b_fx)


def _fox_attn_body(q_ref, k_ref, v_ref, c_ref, o_ref, m_ref, acc_ref, s_ref):
    i = pl.program_id(2)
    ones_col = (lax.broadcasted_iota(I32, (FX_T, FX_DH), 1) == 0).astype(BF16)
    heads = []
    for hh in range(FX_HP):
        sl = slice(hh * FX_DH, (hh + 1) * FX_DH)
        q = (q_ref[:, sl].astype(F32) * (FX_DH ** -0.5 * LOG2E)).astype(BF16)
        heads.append((hh, sl, q, c_ref[hh, i][:, 0:1]))
    m_ref[...] = jnp.full(m_ref.shape, -jnp.inf, F32)
    acc_ref[...] = jnp.zeros(acc_ref.shape, F32)

    def scores(j, slot):
        r0 = pl.multiple_of(j * FX_T, FX_T)
        for hh, sl, q, c_q in heads:
            s = lax.dot_general(q, k_ref[pl.ds(r0, FX_T), sl], (((1,), (1,)), ((), ())),
                                preferred_element_type=F32)
            s_ref[slot, hh] = s + (c_q - c_ref[hh, j])

    def consume(j, slot, masked):
        r0 = pl.multiple_of(j * FX_T, FX_T)
        for hh, sl, _, _ in heads:
            s = s_ref[slot, hh]
            if masked:
                row = lax.broadcasted_iota(I32, (FX_T, FX_T), 0)
                col = lax.broadcasted_iota(I32, (FX_T, FX_T), 1)
                s = jnp.where(row >= col, s, -jnp.inf)
            m_old = m_ref[hh]
            m_new = jnp.maximum(m_old, jnp.max(s, axis=-1, keepdims=True))
            p = jnp.exp2(s - m_new).astype(BF16)
            v1 = jnp.concatenate([v_ref[pl.ds(r0, FX_T), sl], ones_col], axis=-1)
            acc_ref[hh] = jnp.exp2(m_old - m_new) * acc_ref[hh] + jnp.dot(p, v1, preferred_element_type=F32)
            m_ref[hh] = m_new

    scores(0, 0)

    def pair(jj, carry):
        j = 2 * jj
        scores(j + 1, 1)
        consume(j, 0, False)
        scores(j + 2, 0)
        consume(j + 1, 1, False)
        return carry

    lax.fori_loop(0, i // 2, pair, 0)

    @pl.when(i % 2 == 1)
    def _():
        scores(i, 1)
        consume(i - 1, 0, False)
        consume(i, 1, True)

    @pl.when(i % 2 == 0)
    def _():
        consume(i, 0, True)

    for hh, sl, _, _ in heads:
        acc = acc_ref[hh]
        o_ref[:, sl] = (acc[:, :FX_DH] / acc[:, FX_DH:FX_DH + 1]).astype(BF16)


def _fox_attn(proj, c_rows, B, S):
    T = B * S
    nq = S // FX_T
    wide = FX_HP * FX_DH
    cq = COL_FXQ * (D_MODEL // wide)
    ck = COL_FXK * (D_MODEL // wide)
    cv = COL_FXV * (D_MODEL // wide)
    proj3 = proj.reshape(B, S, N_BIG)
    out = pl.pallas_call(
        _fox_attn_body,
        grid=(B, FX_HEADS // FX_HP, nq),
        in_specs=[
            pl.BlockSpec((None, FX_T, wide), lambda b, h, i: (b, i, cq + h)),
            pl.BlockSpec((None, S, wide), lambda b, h, i: (b, 0, ck + h)),
            pl.BlockSpec((None, S, wide), lambda b, h, i: (b, 0, cv + h)),
            pl.BlockSpec((None, FX_HP, nq, 1, FX_T), lambda b, h, i: (b, h, 0, 0, 0)),
        ],
        out_specs=pl.BlockSpec((None, FX_T, wide), lambda b, h, i: (b, i, h)),
        out_shape=jax.ShapeDtypeStruct((B, S, D_MODEL), BF16),
        scratch_shapes=[
            pltpu.VMEM((FX_HP, FX_T, 1), F32),
            pltpu.VMEM((FX_HP, FX_T, 2 * FX_DH), F32),
            pltpu.VMEM((2, FX_HP, FX_T, FX_T), F32),
        ],
        compiler_params=_cparams(("parallel", "parallel", "arbitrary")),
        name="fox_attn",
    )(proj3, proj3, proj3, c_rows)
    return out.reshape(T, D_MODEL)


def _memkv_body(x_ref, g_ref, w_ref, o_ref):
    hb = _rms(x_ref[...], g_ref[...]).astype(BF16)
    o_ref[...] = jnp.dot(hb, w_ref[...], preferred_element_type=F32).astype(BF16)


def _memkv(mem2d, g, w_kv):
    R = mem2d.shape[0]
    tm = min(512, R)
    N = w_kv.shape[1]
    return pl.pallas_call(
        _memkv_body,
        grid=(R // tm,),
        in_specs=[
            pl.BlockSpec((tm, D_MODEL), lambda i: (i, 0)),
            pl.BlockSpec((1, D_MODEL), lambda i: (0, 0)),
            pl.BlockSpec((D_MODEL, N), lambda i: (0, 0)),
        ],
        out_specs=pl.BlockSpec((tm, N), lambda i: (i, 0)),
        out_shape=jax.ShapeDtypeStruct((R, N), BF16),
        compiler_params=_cparams(("parallel",)),
        name="memkv",
    )(mem2d, g, w_kv)


CA_TQ = 512


def _memattn_body(q_ref, k_ref, v_ref, o_ref):
    scale = CA_DH ** -0.5
    for h in range(CA_HEADS):
        sl = slice(h * CA_DH, (h + 1) * CA_DH)
        s = lax.dot_general(q_ref[:, sl], k_ref[:, sl], (((1,), (1,)), ((), ())),
                            preferred_element_type=F32) * scale
        p = jnp.exp(s - jnp.max(s, axis=-1, keepdims=True))
        l = jnp.sum(p, axis=-1, keepdims=True)
        o = jnp.dot(p.astype(BF16), v_ref[:, sl], preferred_element_type=F32) / l
        o_ref[:, sl] = o.astype(BF16)


def _memattn(proj, kv, B, S, M):
    T = B * S
    nq = S // CA_TQ
    kv3 = kv.reshape(B, M, 2 * D_MODEL)
    return pl.pallas_call(
        _memattn_body,
        grid=(B, nq),
        in_specs=[
            pl.BlockSpec((CA_TQ, D_MODEL), lambda b, i: (b * nq + i, COL_CAQ)),
            pl.BlockSpec((None, M, D_MODEL), lambda b, i: (b, 0, 0)),
            pl.BlockSpec((None, M, D_MODEL), lambda b, i: (b, 0, 1)),
        ],
        out_specs=pl.BlockSpec((CA_TQ, D_MODEL), lambda b, i: (b * nq + i, 0)),
        out_shape=jax.ShapeDtypeStruct((T, D_MODEL), BF16),
        compiler_params=_cparams(("parallel", "arbitrary")),
        name="memattn",
    )(proj, kv3, kv3)


MERGE_TM = 512


def _merge_body(y0_ref, y1_ref, y2_ref, g0_ref, g1_ref, g2_ref, x_ref, wb_ref, wo_ref, o_ref):
    merged = None
    for n, (y_ref, g_ref) in enumerate(((y0_ref, g0_ref), (y1_ref, g1_ref), (y2_ref, g2_ref))):
        p = jnp.dot(y_ref[...], wb_ref[n], preferred_element_type=F32)
        t = jax.nn.sigmoid(g_ref[...].astype(F32)) * p
        merged = t if merged is None else merged + t
    o_ref[...] = x_ref[...] + jnp.dot(merged.astype(BF16), wo_ref[...], preferred_element_type=F32)


def _merge(y_ml, y_fx, y_ca, proj, x2d, w_branch, w_out):
    T = x2d.shape[0]
    tm = MERGE_TM
    row = lambda i: (i, 0)
    return pl.pallas_call(
        _merge_body,
        grid=(T // tm,),
        in_specs=[
            pl.BlockSpec((tm, D_MODEL), row),
            pl.BlockSpec((tm, D_MODEL), row),
            pl.BlockSpec((tm, D_MODEL), row),
            pl.BlockSpec((tm, D_MODEL), lambda i: (i, COL_GATE0)),
            pl.BlockSpec((tm, D_MODEL), lambda i: (i, COL_GATE0 + 1)),
            pl.BlockSpec((tm, D_MODEL), lambda i: (i, COL_GATE0 + 2)),
            pl.BlockSpec((tm, D_MODEL), row),
            pl.BlockSpec((3, D_MODEL, D_MODEL), lambda i: (0, 0, 0)),
            pl.BlockSpec((D_MODEL, D_MODEL), lambda i: (0, 0)),
        ],
        out_specs=pl.BlockSpec((tm, D_MODEL), row),
        out_shape=jax.ShapeDtypeStruct((T, D_MODEL), F32),
        compiler_params=_cparams(("parallel",)),
        name="merge",
    )(y_ml, y_fx, y_ca, proj, proj, proj, x2d, w_branch, w_out)


ROUTER_TM = 512


def _router_body(x_ref, g_ref, wr_ref, br_ref, hp_ref, ri_ref, rw_ref, cnt_ref, carry_ref):
    tm = ROUTER_TM

    @pl.when(pl.program_id(0) == 0)
    def _():
        carry_ref[...] = jnp.zeros_like(carry_ref)

    h = _rms(x_ref[...], g_ref[...])
    hp_ref[...] = _pack_rows(h)
    logits = jnp.dot(h.astype(BF16), wr_ref[...], preferred_element_type=F32) + br_ref[...]
    lane = lax.broadcasted_iota(I32, (tm, LANES), 1)
    lane_f = lane.astype(F32)

    work = logits
    onehot_sum = jnp.zeros((tm, LANES), F32)
    vals, sels, idxs = [], [], []
    for _ in range(TOP_K):
        mx = jnp.max(work, axis=-1, keepdims=True)
        idx = jnp.min(jnp.where(work == mx, lane_f, float(LANES)), axis=-1, keepdims=True)
        sel = lane_f == idx
        onehot_sum = onehot_sum + sel.astype(F32)
        work = jnp.where(sel, -jnp.inf, work)
        vals.append(mx)
        sels.append(sel)
        idxs.append(idx)
    exps = [jnp.exp(v - vals[0]) for v in vals]
    total = exps[0] + exps[1] + exps[2] + exps[3]

    row = lax.broadcasted_iota(I32, (tm, tm), 0)
    col = lax.broadcasted_iota(I32, (tm, tm), 1)
    strict = (row > col).astype(BF16)
    before = jnp.dot(strict, onehot_sum.astype(BF16), preferred_element_type=F32) + carry_ref[...]
    carry_ref[...] = carry_ref[...] + jnp.sum(onehot_sum, axis=0, keepdims=True)
    cnt_ref[...] = carry_ref[...]

    ri = jnp.zeros((tm, LANES), I32)
    rw = jnp.zeros((tm, LANES), F32)
    for k in range(TOP_K):
        rank = jnp.sum(jnp.where(sels[k], before, 0.0), axis=-1, keepdims=True)
        ri = jnp.where(lane == k, idxs[k].astype(I32), ri)
        ri = jnp.where(lane == TOP_K + k, rank.astype(I32), ri)
        rw = jnp.where(lane == k, exps[k] / total, rw)
    ri_ref[...] = ri
    rw_ref[...] = rw


def _router(x2, g, w_router, b_router):
    T = x2.shape[0]
    tm = ROUTER_TM
    return pl.pallas_call(
        _router_body,
        grid=(T // tm,),
        in_specs=[
            pl.BlockSpec((tm, D_MODEL), lambda i: (i, 0)),
            pl.BlockSpec((1, D_MODEL), lambda i: (0, 0)),
            pl.BlockSpec((D_MODEL, LANES), lambda i: (0, 0)),
            pl.BlockSpec((1, LANES), lambda i: (0, 0)),
        ],
        out_specs=[
            pl.BlockSpec((tm, HALF), lambda i: (i, 0)),
            pl.BlockSpec((tm, LANES), lambda i: (i, 0)),
            pl.BlockSpec((tm, LANES), lambda i: (i, 0)),
            pl.BlockSpec((1, LANES), lambda i: (0, 0)),
        ],
        out_shape=[
            jax.ShapeDtypeStruct((T, HALF), I32),
            jax.ShapeDtypeStruct((T, LANES), I32),
            jax.ShapeDtypeStruct((T, LANES), F32),
            jax.ShapeDtypeStruct((1, LANES), F32),
        ],
        scratch_shapes=[pltpu.VMEM((1, LANES), F32)],
        compiler_params=_cparams(("arbitrary",)),
        name="router",
    )(x2, g, w_router, b_router)


DISPATCH_TM = 512
EXPERT_TM = 512


def _dispatch_body(dest_ref, hp_ref, xs_in_ref, xs_ref, sem):
    del xs_in_ref

    def row_copy(r, d):
        return pltpu.make_async_copy(hp_ref.at[pl.ds(r, 1)], xs_ref.at[pl.ds(d, 1)], sem)

    def start(r, carry):
        for k in range(TOP_K):
            row_copy(r, dest_ref[r * TOP_K + k]).start(priority=k % 2)
        return carry

    def wait(r, carry):
        for k in range(TOP_K):
            row_copy(r, dest_ref[r * TOP_K + k]).wait()
        return carry

    lax.fori_loop(0, DISPATCH_TM, start, 0)
    lax.fori_loop(0, DISPATCH_TM, wait, 0)


def _dispatch(dest, hp, n_rows):
    T = hp.shape[0]
    tm = DISPATCH_TM
    xs0 = jnp.zeros((n_rows, HALF), I32)
    return pl.pallas_call(
        _dispatch_body,
        grid=(T // tm,),
        in_specs=[
            pl.BlockSpec((tm * TOP_K,), lambda i: (i,), memory_space=pltpu.SMEM),
            pl.BlockSpec((tm, HALF), lambda i: (i, 0)),
            pl.BlockSpec(memory_space=pl.ANY),
        ],
        out_specs=pl.BlockSpec(memory_space=pl.ANY),
        out_shape=jax.ShapeDtypeStruct((n_rows, HALF), I32),
        scratch_shapes=[pltpu.SemaphoreType.DMA(())],
        input_output_aliases={2: 0},
        compiler_params=_cparams(("arbitrary",)),
        name="dispatch",
    )(dest, hp, xs0)


FF_CHUNK = 512


def _expert_body(te_ref, nv_ref, x_ref, w1f_ref, b1_ref, w2f_ref, b2_ref, y_ref, w1_ref, w2_ref):
    i = pl.program_id(0)

    @pl.when(jnp.logical_or(i == 0, te_ref[i] != te_ref[jnp.maximum(i - 1, 0)]))
    def _():
        w1_ref[...] = w1f_ref[...].astype(BF16)
        w2_ref[...] = w2f_ref[...].astype(BF16)

    @pl.when(i < nv_ref[0])
    def _():
        lo, hi = _unpack_rows(x_ref[...])
        xlo = lo.astype(BF16)
        xhi = hi.astype(BF16)
        acc = jnp.zeros((EXPERT_TM, D_MODEL), F32) + b2_ref[...]
        for c in range(D_FF // FF_CHUNK):
            def up(off):
                cs = slice(off + c * FF_CHUNK, off + (c + 1) * FF_CHUNK)
                return (jnp.dot(xlo, w1_ref[0:HALF, cs], preferred_element_type=F32)
                        + jnp.dot(xhi, w1_ref[HALF:D_MODEL, cs], preferred_element_type=F32)
                        + b1_ref[:, cs])
            g = jnp.minimum(up(0), SWIGLU_LIMIT)
            lin = jnp.clip(up(D_FF), -SWIGLU_LIMIT, SWIGLU_LIMIT)
            a = g * jax.nn.sigmoid(SWIGLU_ALPHA * g) * (lin + 1.0)
            acc = acc + jnp.dot(a.astype(BF16), w2_ref[c * FF_CHUNK:(c + 1) * FF_CHUNK, :],
                                preferred_element_type=F32)
        y_ref[...] = _pack_rows(acc)


def _experts(tile_expert, n_valid, xs, w1, b1, w2, b2):
    n_rows = xs.shape[0]
    tm = EXPERT_TM
    n_tiles = n_rows // tm
    row = lambda i, te, nv: (jnp.minimum(i, nv[0] - 1), 0)
    grid_spec = pltpu.PrefetchScalarGridSpec(
        num_scalar_prefetch=2,
        grid=(n_tiles,),
        in_specs=[
            pl.BlockSpec((tm, HALF), row),
            pl.BlockSpec((None, D_MODEL, 2 * D_FF), lambda i, te, nv: (te[i], 0, 0)),
            pl.BlockSpec((None, 1, 2 * D_FF), lambda i, te, nv: (te[i], 0, 0)),
            pl.BlockSpec((None, D_FF, D_MODEL), lambda i, te, nv: (te[i], 0, 0)),
            pl.BlockSpec((None, 1, D_MODEL), lambda i, te, nv: (te[i], 0, 0)),
        ],
        out_specs=pl.BlockSpec((tm, HALF), row),
        scratch_shapes=[pltpu.VMEM((D_MODEL, 2 * D_FF), BF16), pltpu.VMEM((D_FF, D_MODEL), BF16)],
    )
    return pl.pallas_call(
        _expert_body,
        grid_spec=grid_spec,
        out_shape=jax.ShapeDtypeStruct((n_rows, HALF), I32),
        compiler_params=_cparams(("arbitrary",)),
        name="experts",
    )(tile_expert, n_valid, xs, w1, b1, w2, b2)


COMBINE_TM = 256


def _combine_body(dcur_ref, dnext_ref, rw_ref, x_ref, g_ref, ys_ref, o_ref, buf, sem):
    tm = COMBINE_TM
    i = pl.program_id(0)
    n = pl.num_programs(0)

    def row_copy(d, slot, k, r):
        return pltpu.make_async_copy(ys_ref.at[pl.ds(d, 1)], buf.at[slot, k, pl.ds(r, 1)], sem.at[slot])

    def gather(dref, slot, wait):
        def body(r, carry):
            for k in range(TOP_K):
                cp = row_copy(dref[r * TOP_K + k], slot, k, r)
                if wait:
                    cp.wait()
                else:
                    cp.start(priority=k % 2)
            return carry
        lax.fori_loop(0, tm, body, 0)

    @pl.when(i == 0)
    def _():
        gather(dcur_ref, 0, False)

    @pl.when(i + 1 < n)
    def _():
        gather(dnext_ref, (i + 1) % 2, False)

    slot = i % 2
    gather(dcur_ref, slot, True)
    acc = x_ref[...]
    rw = rw_ref[...]
    for k in range(TOP_K):
        lo, hi = _unpack_rows(buf[slot, k])
        acc = acc + rw[:, k:k + 1] * jnp.concatenate([lo, hi], axis=-1)
    o_ref[...] = _rms(acc, g_ref[...])


def _combine(dest, rw, x2, g, ys):
    T = x2.shape[0]
    tm = COMBINE_TM
    n = T // tm
    return pl.pallas_call(
        _combine_body,
        grid=(n,),
        in_specs=[
            pl.BlockSpec((tm * TOP_K,), lambda i: (i,), memory_space=pltpu.SMEM),
            pl.BlockSpec((tm * TOP_K,), lambda i: (jnp.minimum(i + 1, n - 1),), memory_space=pltpu.SMEM),
            pl.BlockSpec((tm, LANES), lambda i: (i, 0)),
            pl.BlockSpec((tm, D_MODEL), lambda i: (i, 0)),
            pl.BlockSpec((1, D_MODEL), lambda i: (0, 0)),
            pl.BlockSpec(memory_space=pl.ANY),
        ],
        out_specs=pl.BlockSpec((tm, D_MODEL), lambda i: (i, 0)),
        out_shape=jax.ShapeDtypeStruct((T, D_MODEL), F32),
        scratch_shapes=[
            pltpu.VMEM((2, TOP_K, tm, HALF), I32),
            pltpu.SemaphoreType.DMA((2,)),
        ],
        compiler_params=_cparams(("arbitrary",)),
        name="combine",
    )(dest, dest, rw, x2, g, ys)


def _pad_lanes(v, fill=0.0):
    v = v.reshape(1, -1).astype(F32)
    return jnp.pad(v, ((0, 0), (0, LANES - v.shape[1])), constant_values=fill)


def _layer(x2d, mem2d, B, S, M, norm_mix, w_in, b_ml_gates, conv_ml, ml_head_norm, b_fx_gate, norm_mem,
           w_mem_kv, w_branch, w_out, norm_moe, w_router, b_router, w_exp_in, b_exp_in, w_exp_out,
           b_exp_out, norm_out):
    T = B * S
    w_big = jnp.concatenate([w_in[:, 0:2048], w_in[:, 2056:3080], w_in[:, 3080:6152], w_in[:, 6160:7184],
                             w_in[:, 7184:10256]], axis=1).astype(BF16)
    w_small = jnp.concatenate([w_in[:, 2048:2056], w_in[:, 6152:6160]], axis=1)
    w_small = jnp.pad(w_small, ((0, 0), (0, LANES - w_small.shape[1]))).astype(BF16)
    row = lambda v: v.reshape(1, -1).astype(F32)

    proj, small = _inproj(x2d, row(norm_mix), w_big, w_small)

    y_ml = _mlstm(proj, small, conv_ml.astype(F32), _pad_lanes(b_ml_gates), row(ml_head_norm), B, S)

    b_fx = jnp.pad(b_fx_gate.reshape(1, -1).astype(F32), ((0, 0), (2 * ML_HEADS, LANES - 2 * ML_HEADS - FX_HEADS)))
    c_rows = _fox_gate(small, b_fx, B, S)
    y_fx = _fox_attn(proj, c_rows, B, S)

    kv = _memkv(mem2d, row(norm_mem), w_mem_kv.astype(BF16))
    y_ca = _memattn(proj, kv, B, S, M)

    x2 = _merge(y_ml, y_fx, y_ca, proj, x2d, w_branch.astype(BF16), w_out.astype(BF16))

    w_r = jnp.pad(w_router, ((0, 0), (0, LANES - N_EXPERTS))).astype(BF16)
    hp, ri, rw, cnt = _router(x2, row(norm_moe), w_r, _pad_lanes(b_router, fill=-1e30))

    tm = EXPERT_TM
    n_tiles = (T * TOP_K) // tm + N_EXPERTS
    counts = cnt[0, :N_EXPERTS].astype(I32)
    padded = ((counts + tm - 1) // tm) * tm
    gend = jnp.cumsum(padded)
    gstart = gend - padded
    expert_ids = jnp.arange(N_EXPERTS, dtype=I32)
    start_of = jnp.sum(jnp.where(ri[:, 0:TOP_K, None] == expert_ids, gstart, 0), axis=-1)
    dest = (start_of + ri[:, TOP_K:2 * TOP_K]).reshape(-1)
    n_valid = gend[-1] // tm
    tile_ids = jnp.arange(n_tiles, dtype=I32)
    last_tile = jnp.minimum(tile_ids, n_valid - 1)
    tile_e = jnp.minimum(jnp.sum((gend[None, :] <= last_tile[:, None] * tm).astype(I32), axis=1), N_EXPERTS - 1)

    xs = _dispatch(dest, hp, n_tiles * tm)
    ys = _experts(tile_e.astype(I32), n_valid.reshape(1).astype(I32), xs, w_exp_in.astype(F32),
                  b_exp_in.reshape(N_EXPERTS, 1, -1).astype(F32), w_exp_out.astype(F32),
                  b_exp_out.reshape(N_EXPERTS, 1, -1).astype(F32))
    return _combine(dest, rw, x2, row(norm_out), ys)


def kernel(x, mem, norm_mix, w_in, b_ml_gates, conv_ml, ml_head_norm, b_fx_gate, norm_mem, w_mem_kv, w_branch,
           w_out, norm_moe, w_router, b_router, w_exp_in, b_exp_in, w_exp_out, b_exp_out, norm_final):
    B, S, D = x.shape
    M = mem.shape[1]
    depth = norm_mix.shape[0]
    assert depth == 1, "the combine kernel fuses the final norm, so exactly one layer is supported"
    assert D == D_MODEL and S % ML_BLOCK == 0 and S % FX_T == 0 and S % CA_TQ == 0
    out = _layer(x.reshape(B * S, D), mem.reshape(B * M, D), B, S, M, norm_mix[0], w_in[0], b_ml_gates[0],
                 conv_ml[0], ml_head_norm[0], b_fx_gate[0], norm_mem[0], w_mem_kv[0], w_branch[0], w_out[0],
                 norm_moe[0], w_router[0], b_router[0], w_exp_in[0], b_exp_in[0], w_exp_out[0], b_exp_out[0],
                 norm_final)
    return out.reshape(B, S, D)
```

```python
import functools

import jax
import jax.numpy as jnp
from jax import lax
from jax.experimental import pallas as pl
from jax.experimental.pallas import tpu as pltpu
from jax.experimental.pallas import tpu_sc as plsc

F32 = jnp.float32
BF16 = jnp.bfloat16
I32 = jnp.int32

D_MODEL = 1024
N_MEM_HEADS = 4
ML_HEADS = 4
ML_DQK = 128
ML_DV = 256
ML_CONV = 4
FX_HEADS = 8
FX_DH = 128
CA_HEADS = 4
CA_DH = 256
N_EXPERTS = 32
TOP_K = 4
D_FF = D_MODEL
SWIGLU_LIMIT = 7.0
SWIGLU_ALPHA = 1.702
EPS = 1e-5
LANES = 128
HALF = D_MODEL // 2
HI_MASK = -65536

COL_MLQK, COL_MLV, COL_MLO, COL_FXQ, COL_FXK, COL_FXV, COL_CAQ, COL_GATE0 = 0, 1, 2, 3, 4, 5, 6, 7
N_BIG = 10 * D_MODEL

VMEM_LIMIT = 56 * 1024 * 1024


def _cparams(sem):
    return pltpu.CompilerParams(dimension_semantics=sem, vmem_limit_bytes=VMEM_LIMIT)


def _rms(x, g):
    return x * lax.rsqrt(jnp.mean(x * x, axis=-1, keepdims=True) + EPS) * g


def _log_sigmoid(x):
    return jnp.minimum(x, 0.0) - jnp.log1p(jnp.exp(-jnp.abs(x)))


def _pack_rows(y):
    bits = lax.bitcast_convert_type(y.astype(BF16).astype(F32), I32)
    return lax.shift_right_logical(bits[:, :HALF], 16) | (bits[:, HALF:] & HI_MASK)


def _unpack_rows(w):
    lo = lax.bitcast_convert_type(lax.shift_left(w, 16), F32)
    hi = lax.bitcast_convert_type(w & HI_MASK, F32)
    return lo, hi


def _inproj_body(x_ref, g_ref, w_ref, ws_ref, o_ref, os_ref, h_ref):
    @pl.when(pl.program_id(1) == 0)
    def _():
        hb = _rms(x_ref[...], g_ref[...]).astype(BF16)
        h_ref[...] = hb
        os_ref[...] = jnp.dot(hb, ws_ref[...], preferred_element_type=F32)

    o_ref[...] = jnp.dot(h_ref[...], w_ref[...], preferred_element_type=F32).astype(BF16)


def _inproj(x2d, g, w_big, w_small):
    T = x2d.shape[0]
    tm = min(1024, T)
    tn = 2048
    return pl.pallas_call(
        _inproj_body,
        grid=(T // tm, N_BIG // tn),
        in_specs=[
            pl.BlockSpec((tm, D_MODEL), lambda i, j: (i, 0)),
            pl.BlockSpec((1, D_MODEL), lambda i, j: (0, 0)),
            pl.BlockSpec((D_MODEL, tn), lambda i, j: (0, j)),
            pl.BlockSpec((D_MODEL, LANES), lambda i, j: (0, 0)),
        ],
        out_specs=[
            pl.BlockSpec((tm, tn), lambda i, j: (i, j)),
            pl.BlockSpec((tm, LANES), lambda i, j: (i, 0)),
        ],
        out_shape=[
            jax.ShapeDtypeStruct((T, N_BIG), BF16),
            jax.ShapeDtypeStruct((T, LANES), F32),
        ],
        scratch_shapes=[pltpu.VMEM((tm, D_MODEL), BF16)],
        compiler_params=_cparams(("parallel", "arbitrary")),
        name="inproj",
    )(x2d, g, w_big, w_small)


ML_BLOCK = 512
ML_CHUNK = 128
CONV_PAD = 8


def _mlstm_body(qk_ref, v_ref, o_ref, g_ref, cw_ref, bg_ref, hn_ref, y_ref, xbuf, c_st, n_st, m_st):
    L = ML_CHUNK

    @pl.when(pl.program_id(1) == 0)
    def _():
        xbuf[0:CONV_PAD, :] = jnp.zeros((CONV_PAD, D_MODEL), F32)
        c_st[...] = jnp.zeros_like(c_st)
        n_st[...] = jnp.zeros_like(n_st)
        m_st[...] = jnp.zeros_like(m_st)

    xbuf[CONV_PAD:CONV_PAD + ML_BLOCK, :] = qk_ref[...].astype(F32)
    cw = cw_ref[...]
    row = lax.broadcasted_iota(I32, (L, L), 0)
    col = lax.broadcasted_iota(I32, (L, L), 1)
    causal = row >= col
    tri = causal.astype(F32)
    bg = bg_ref[...]
    scale = ML_DQK ** -0.5

    for c in range(ML_BLOCK // L):
        r0 = c * L
        conv = cw[0:1, :] * xbuf[r0 + CONV_PAD - 3:r0 + CONV_PAD - 3 + L, :]
        for j in range(1, ML_CONV):
            s0 = r0 + CONV_PAD - 3 + j
            conv = conv + cw[j:j + 1, :] * xbuf[s0:s0 + L, :]
        act = conv * jax.nn.sigmoid(conv)

        gates = g_ref[r0:r0 + L, :] + bg
        cum = jnp.dot(tri, _log_sigmoid(gates), precision=lax.Precision.HIGHEST,
                      preferred_element_type=F32)
        gates_t = gates.T
        cum_t = cum.T
        for h in range(ML_HEADS):
            b_col = cum[:, ML_HEADS + h:ML_HEADS + h + 1]
            i_col = gates[:, h:h + 1]
            b_row = cum_t[ML_HEADS + h:ML_HEADS + h + 1, :]
            i_row = gates_t[h:h + 1, :]
            m_prev = m_st[h]
            dm = jnp.where(causal, b_col + (i_row - b_row), -jnp.inf)
            m_inter = b_col + m_prev
            m_t = jnp.maximum(jnp.max(dm, axis=-1, keepdims=True), m_inter)
            w_intra = jnp.exp(dm - m_t)
            w_inter = jnp.exp(m_inter - m_t)

            qh = act[:, h * ML_DQK:(h + 1) * ML_DQK] * scale
            kh = act[:, (ML_HEADS + h) * ML_DQK:(ML_HEADS + h + 1) * ML_DQK]
            qb = qh.astype(BF16)
            vb = v_ref[r0:r0 + L, h * ML_DV:(h + 1) * ML_DV]
            s = lax.dot_general(qb, kh.astype(BF16), (((1,), (1,)), ((), ())),
                                preferred_element_type=F32) * w_intra
            c_old = c_st[h]
            n_old = n_st[h]
            num = jnp.dot(s.astype(BF16), vb, preferred_element_type=F32) + w_inter * jnp.dot(
                qb, c_old.astype(BF16), preferred_element_type=F32)
            den = jnp.sum(s, axis=-1, keepdims=True) + w_inter * jnp.sum(qh * n_old, axis=-1, keepdims=True)
            hv = num / jnp.maximum(jnp.abs(den), jnp.exp(-m_t))

            m_new = m_t[L - 1:L, :]
            b_last = b_col[L - 1:L, :]
            wk = jnp.exp(b_last - b_col + i_col - m_new)
            decay = jnp.exp(b_last + m_prev - m_new)
            kw = kh * wk
            c_st[h] = decay * c_old + jnp.dot(kw.T.astype(BF16), vb, preferred_element_type=F32)
            n_st[h] = decay * n_old + jnp.sum(kw, axis=0, keepdims=True)
            m_st[h] = m_new

            hn = _rms(hv, hn_ref[:, h * ML_DV:(h + 1) * ML_DV])
            og = o_ref[r0:r0 + L, h * ML_DV:(h + 1) * ML_DV].astype(F32)
            y_ref[r0:r0 + L, h * ML_DV:(h + 1) * ML_DV] = (hn * jax.nn.sigmoid(og)).astype(BF16)

    xbuf[0:CONV_PAD, :] = xbuf[ML_BLOCK:ML_BLOCK + CONV_PAD, :]


def _mlstm(proj, small, conv_w, b_gates, head_norm, B, S):
    T = B * S
    ns = S // ML_BLOCK
    return pl.pallas_call(
        _mlstm_body,
        grid=(B, ns),
        in_specs=[
            pl.BlockSpec((ML_BLOCK, D_MODEL), lambda b, s: (b * ns + s, COL_MLQK)),
            pl.BlockSpec((ML_BLOCK, D_MODEL), lambda b, s: (b * ns + s, COL_MLV)),
            pl.BlockSpec((ML_BLOCK, D_MODEL), lambda b, s: (b * ns + s, COL_MLO)),
            pl.BlockSpec((ML_BLOCK, LANES), lambda b, s: (b * ns + s, 0)),
            pl.BlockSpec((ML_CONV, D_MODEL), lambda b, s: (0, 0)),
            pl.BlockSpec((1, LANES), lambda b, s: (0, 0)),
            pl.BlockSpec((1, D_MODEL), lambda b, s: (0, 0)),
        ],
        out_specs=pl.BlockSpec((ML_BLOCK, D_MODEL), lambda b, s: (b * ns + s, 0)),
        out_shape=jax.ShapeDtypeStruct((T, D_MODEL), BF16),
        scratch_shapes=[
            pltpu.VMEM((ML_BLOCK + CONV_PAD, D_MODEL), F32),
            pltpu.VMEM((ML_HEADS, ML_DQK, ML_DV), F32),
            pltpu.VMEM((ML_HEADS, 1, ML_DQK), F32),
            pltpu.VMEM((ML_HEADS, 1, 1), F32),
        ],
        compiler_params=_cparams(("parallel", "arbitrary")),
        name="mlstm",
    )(proj, proj, proj, small, conv_w, b_gates, head_norm)


FX_T = 512
FX_HP = 2
LOG2E = 1.4426950408889634


def _fox_gate_body(g_ref, b_ref, o_ref):
    S = g_ref.shape[0]
    nk = S // FX_T
    row = lax.broadcasted_iota(I32, (FX_T, FX_T), 0)
    col = lax.broadcasted_iota(I32, (FX_T, FX_T), 1)
    tri = (row >= col).astype(F32)
    carry = jnp.zeros((1, LANES), F32)
    for blk in range(nk):
        lf = _log_sigmoid(g_ref[blk * FX_T:(blk + 1) * FX_T, :] + b_ref[...])
        cum = jnp.dot(tri, lf, precision=lax.Precision.HIGHEST, preferred_element_type=F32) + carry
        carry = cum[FX_T - 1:FX_T, :]
        cum_t = (cum * LOG2E).T
        for h in range(FX_HEADS):
            o_ref[h, blk] = cum_t[2 * ML_HEADS + h:2 * ML_HEADS + h + 1, :]


def _fox_gate(small, b_fx, B, S):
    nk = S // FX_T
    return pl.pallas_call(
        _fox_gate_body,
        grid=(B,),
        in_specs=[
            pl.BlockSpec((S, LANES), lambda b: (b, 0)),
            pl.BlockSpec((1, LANES), lambda b: (0, 0)),
        ],
        out_specs=pl.BlockSpec((None, FX_HEADS, nk, 1, FX_T), lambda b: (b, 0, 0, 0, 0)),
        out_shape=jax.ShapeDtypeStruct((B, FX_HEADS, nk, 1, FX_T), F32),
        compiler_params=_cparams(("parallel",)),
        name="fox_gate",
    )(small, b_fx)


def _fox_attn_body(q_ref, k_ref, v_ref, c_ref, o_ref, m_ref, acc_ref, s_ref):
    i = pl.program_id(2)
    ones_col = (lax.broadcasted_iota(I32, (FX_T, FX_DH), 1) == 0).astype(BF16)
    heads = []
    for hh in range(FX_HP):
        sl = slice(hh * FX_DH, (hh + 1) * FX_DH)
        q = (q_ref[:, sl].astype(F32) * (FX_DH ** -0.5 * LOG2E)).astype(BF16)
        heads.append((hh, sl, q, c_ref[hh, i][:, 0:1]))
    m_ref[...] = jnp.full(m_ref.shape, -jnp.inf, F32)
    acc_ref[...] = jnp.zeros(acc_ref.shape, F32)

    def scores(j, slot):
        r0 = pl.multiple_of(j * FX_T, FX_T)
        for hh, sl, q, c_q in heads:
            s = lax.dot_general(q, k_ref[pl.ds(r0, FX_T), sl], (((1,), (1,)), ((), ())),
                                preferred_element_type=F32)
            s_ref[slot, hh] = s + (c_q - c_ref[hh, j])

    def consume(j, slot, masked):
        r0 = pl.multiple_of(j * FX_T, FX_T)
        for hh, sl, _, _ in heads:
            s = s_ref[slot, hh]
            if masked:
                row = lax.broadcasted_iota(I32, (FX_T, FX_T), 0)
                col = lax.broadcasted_iota(I32, (FX_T, FX_T), 1)
                s = jnp.where(row >= col, s, -jnp.inf)
            m_old = m_ref[hh]
            m_new = jnp.maximum(m_old, jnp.max(s, axis=-1, keepdims=True))
            p = jnp.exp2(s - m_new).astype(BF16)
            v1 = jnp.concatenate([v_ref[pl.ds(r0, FX_T), sl], ones_col], axis=-1)
            acc_ref[hh] = jnp.exp2(m_old - m_new) * acc_ref[hh] + jnp.dot(p, v1, preferred_element_type=F32)
            m_ref[hh] = m_new

    scores(0, 0)

    def pair(jj, carry):
        j = 2 * jj
        scores(j + 1, 1)
        consume(j, 0, False)
        scores(j + 2, 0)
        consume(j + 1, 1, False)
        return carry

    lax.fori_loop(0, i // 2, pair, 0)

    @pl.when(i % 2 == 1)
    def _():
        scores(i, 1)
        consume(i - 1, 0, False)
        consume(i, 1, True)

    @pl.when(i % 2 == 0)
    def _():
        consume(i, 0, True)

    for hh, sl, _, _ in heads:
        acc = acc_ref[hh]
        o_ref[:, sl] = (acc[:, :FX_DH] / acc[:, FX_DH:FX_DH + 1]).astype(BF16)


def _fox_attn(proj, c_rows, B, S):
    T = B * S
    nq = S // FX_T
    wide = FX_HP * FX_DH
    cq = COL_FXQ * (D_MODEL // wide)
    ck = COL_FXK * (D_MODEL // wide)
    cv = COL_FXV * (D_MODEL // wide)
    proj3 = proj.reshape(B, S, N_BIG)
    out = pl.pallas_call(
        _fox_attn_body,
        grid=(B, FX_HEADS // FX_HP, nq),
        in_specs=[
            pl.BlockSpec((None, FX_T, wide), lambda b, h, i: (b, i, cq + h)),
            pl.BlockSpec((None, S, wide), lambda b, h, i: (b, 0, ck + h)),
            pl.BlockSpec((None, S, wide), lambda b, h, i: (b, 0, cv + h)),
            pl.BlockSpec((None, FX_HP, nq, 1, FX_T), lambda b, h, i: (b, h, 0, 0, 0)),
        ],
        out_specs=pl.BlockSpec((None, FX_T, wide), lambda b, h, i: (b, i, h)),
        out_shape=jax.ShapeDtypeStruct((B, S, D_MODEL), BF16),
        scratch_shapes=[
            pltpu.VMEM((FX_HP, FX_T, 1), F32),
            pltpu.VMEM((FX_HP, FX_T, 2 * FX_DH), F32),
            pltpu.VMEM((2, FX_HP, FX_T, FX_T), F32),
        ],
        compiler_params=_cparams(("parallel", "parallel", "arbitrary")),
        name="fox_attn",
    )(proj3, proj3, proj3, c_rows)
    return out.reshape(T, D_MODEL)


def _memkv_body(x_ref, g_ref, w_ref, o_ref):
    hb = _rms(x_ref[...], g_ref[...]).astype(BF16)
    o_ref[...] = jnp.dot(hb, w_ref[...], preferred_element_type=F32).astype(BF16)


def _memkv(mem2d, g, w_kv):
    R = mem2d.shape[0]
    tm = min(512, R)
    N = w_kv.shape[1]
    return pl.pallas_call(
        _memkv_body,
        grid=(R // tm,),
        in_specs=[
            pl.BlockSpec((tm, D_MODEL), lambda i: (i, 0)),
            pl.BlockSpec((1, D_MODEL), lambda i: (0, 0)),
            pl.BlockSpec((D_MODEL, N), lambda i: (0, 0)),
        ],
        out_specs=pl.BlockSpec((tm, N), lambda i: (i, 0)),
        out_shape=jax.ShapeDtypeStruct((R, N), BF16),
        compiler_params=_cparams(("parallel",)),
        name="memkv",
    )(mem2d, g, w_kv)


CA_TQ = 512


def _memattn_body(q_ref, k_ref, v_ref, o_ref):
    scale = CA_DH ** -0.5
    for h in range(CA_HEADS):
        sl = slice(h * CA_DH, (h + 1) * CA_DH)
        s = lax.dot_general(q_ref[:, sl], k_ref[:, sl], (((1,), (1,)), ((), ())),
                            preferred_element_type=F32) * scale
        p = jnp.exp(s - jnp.max(s, axis=-1, keepdims=True))
        l = jnp.sum(p, axis=-1, keepdims=True)
        o = jnp.dot(p.astype(BF16), v_ref[:, sl], preferred_element_type=F32) / l
        o_ref[:, sl] = o.astype(BF16)


def _memattn(proj, kv, B, S, M):
    T = B * S
    nq = S // CA_TQ
    kv3 = kv.reshape(B, M, 2 * D_MODEL)
    return pl.pallas_call(
        _memattn_body,
        grid=(B, nq),
        in_specs=[
            pl.BlockSpec((CA_TQ, D_MODEL), lambda b, i: (b * nq + i, COL_CAQ)),
            pl.BlockSpec((None, M, D_MODEL), lambda b, i: (b, 0, 0)),
            pl.BlockSpec((None, M, D_MODEL), lambda b, i: (b, 0, 1)),
        ],
        out_specs=pl.BlockSpec((CA_TQ, D_MODEL), lambda b, i: (b * nq + i, 0)),
        out_shape=jax.ShapeDtypeStruct((T, D_MODEL), BF16),
        compiler_params=_cparams(("parallel", "arbitrary")),
        name="memattn",
    )(proj, kv3, kv3)


MERGE_TM = 512


def _merge_body(y0_ref, y1_ref, y2_ref, g0_ref, g1_ref, g2_ref, x_ref, wb_ref, wo_ref, o_ref):
    merged = None
    for n, (y_ref, g_ref) in enumerate(((y0_ref, g0_ref), (y1_ref, g1_ref), (y2_ref, g2_ref))):
        p = jnp.dot(y_ref[...], wb_ref[n], preferred_element_type=F32)
        t = jax.nn.sigmoid(g_ref[...].astype(F32)) * p
        merged = t if merged is None else merged + t
    o_ref[...] = x_ref[...] + jnp.dot(merged.astype(BF16), wo_ref[...], preferred_element_type=F32)


def _merge(y_ml, y_fx, y_ca, proj, x2d, w_branch, w_out):
    T = x2d.shape[0]
    tm = MERGE_TM
    row = lambda i: (i, 0)
    return pl.pallas_call(
        _merge_body,
        grid=(T // tm,),
        in_specs=[
            pl.BlockSpec((tm, D_MODEL), row),
            pl.BlockSpec((tm, D_MODEL), row),
            pl.BlockSpec((tm, D_MODEL), row),
            pl.BlockSpec((tm, D_MODEL), lambda i: (i, COL_GATE0)),
            pl.BlockSpec((tm, D_MODEL), lambda i: (i, COL_GATE0 + 1)),
            pl.BlockSpec((tm, D_MODEL), lambda i: (i, COL_GATE0 + 2)),
            pl.BlockSpec((tm, D_MODEL), row),
            pl.BlockSpec((3, D_MODEL, D_MODEL), lambda i: (0, 0, 0)),
            pl.BlockSpec((D_MODEL, D_MODEL), lambda i: (0, 0)),
        ],
        out_specs=pl.BlockSpec((tm, D_MODEL), row),
        out_shape=jax.ShapeDtypeStruct((T, D_MODEL), F32),
        compiler_params=_cparams(("parallel",)),
        name="merge",
    )(y_ml, y_fx, y_ca, proj, proj, proj, x2d, w_branch, w_out)


ROUTER_TM = 512


def _router_body(x_ref, g_ref, wr_ref, br_ref, hp_ref, ri_ref, rw_ref, cnt_ref, carry_ref):
    tm = ROUTER_TM

    @pl.when(pl.program_id(0) == 0)
    def _():
        carry_ref[...] = jnp.zeros_like(carry_ref)

    h = _rms(x_ref[...], g_ref[...])
    hp_ref[...] = _pack_rows(h)
    logits = jnp.dot(h.astype(BF16), wr_ref[...], preferred_element_type=F32) + br_ref[...]
    lane = lax.broadcasted_iota(I32, (tm, LANES), 1)
    lane_f = lane.astype(F32)

    work = logits
    onehot_sum = jnp.zeros((tm, LANES), F32)
    vals, sels, idxs = [], [], []
    for _ in range(TOP_K):
        mx = jnp.max(work, axis=-1, keepdims=True)
        idx = jnp.min(jnp.where(work == mx, lane_f, float(LANES)), axis=-1, keepdims=True)
        sel = lane_f == idx
        onehot_sum = onehot_sum + sel.astype(F32)
        work = jnp.where(sel, -jnp.inf, work)
        vals.append(mx)
        sels.append(sel)
        idxs.append(idx)
    exps = [jnp.exp(v - vals[0]) for v in vals]
    total = exps[0] + exps[1] + exps[2] + exps[3]

    row = lax.broadcasted_iota(I32, (tm, tm), 0)
    col = lax.broadcasted_iota(I32, (tm, tm), 1)
    strict = (row > col).astype(BF16)
    before = jnp.dot(strict, onehot_sum.astype(BF16), preferred_element_type=F32) + carry_ref[...]
    carry_ref[...] = carry_ref[...] + jnp.sum(onehot_sum, axis=0, keepdims=True)
    cnt_ref[...] = carry_ref[...]

    ri = jnp.zeros((tm, LANES), I32)
    rw = jnp.zeros((tm, LANES), F32)
    for k in range(TOP_K):
        rank = jnp.sum(jnp.where(sels[k], before, 0.0), axis=-1, keepdims=True)
        ri = jnp.where(lane == k, idxs[k].astype(I32), ri)
        ri = jnp.where(lane == TOP_K + k, rank.astype(I32), ri)
        rw = jnp.where(lane == k, exps[k] / total, rw)
    ri_ref[...] = ri
    rw_ref[...] = rw


def _router(x2, g, w_router, b_router):
    T = x2.shape[0]
    tm = ROUTER_TM
    return pl.pallas_call(
        _router_body,
        grid=(T // tm,),
        in_specs=[
            pl.BlockSpec((tm, D_MODEL), lambda i: (i, 0)),
            pl.BlockSpec((1, D_MODEL), lambda i: (0, 0)),
            pl.BlockSpec((D_MODEL, LANES), lambda i: (0, 0)),
            pl.BlockSpec((1, LANES), lambda i: (0, 0)),
        ],
        out_specs=[
            pl.BlockSpec((tm, HALF), lambda i: (i, 0)),
            pl.BlockSpec((tm, LANES), lambda i: (i, 0)),
            pl.BlockSpec((tm, LANES), lambda i: (i, 0)),
            pl.BlockSpec((1, LANES), lambda i: (0, 0)),
        ],
        out_shape=[
            jax.ShapeDtypeStruct((T, HALF), I32),
            jax.ShapeDtypeStruct((T, LANES), I32),
            jax.ShapeDtypeStruct((T, LANES), F32),
            jax.ShapeDtypeStruct((1, LANES), F32),
        ],
        scratch_shapes=[pltpu.VMEM((1, LANES), F32)],
        compiler_params=_cparams(("arbitrary",)),
        name="router",
    )(x2, g, w_router, b_router)


EXPERT_TM = 512
SC_CORES = 2
SC_SUBCORES = 16
SC_WORKERS = SC_CORES * SC_SUBCORES
SC_CHUNK = 64
PAD_SLOTS = N_EXPERTS * EXPERT_TM


def _sc_mesh():
    return plsc.VectorSubcoreMesh(core_axis_name="c", subcore_axis_name="s")


def _sc_worker():
    return lax.axis_index("s") * SC_CORES + lax.axis_index("c")


def _sc_dispatch(hp, dest, pad_idx, n_rows):
    T = hp.shape[0]
    per_w = T // SC_WORKERS
    n_ch = per_w // SC_CHUNK
    n_pc = PAD_SLOTS // (SC_WORKERS * SC_CHUNK)
    assert per_w % SC_CHUNK == 0 and n_ch >= 2 and n_ch % 2 == 0
    idx = dest.reshape(SC_WORKERS, n_ch, SC_CHUNK, TOP_K).transpose(0, 1, 3, 2)
    idx = idx.reshape(SC_WORKERS, n_ch * TOP_K, SC_CHUNK)
    pidx = pad_idx.reshape(SC_WORKERS, n_pc, SC_CHUNK)
    zeros = jnp.zeros((SC_CHUNK, HALF), I32)

    @functools.partial(
        pl.kernel, mesh=_sc_mesh(),
        out_type=jax.ShapeDtypeStruct((n_rows, HALF), I32),
        scratch_types=[
            pltpu.VMEM((n_ch * TOP_K, SC_CHUNK), I32),
            pltpu.VMEM((n_pc, SC_CHUNK), I32),
            pltpu.VMEM((2, SC_CHUNK, HALF), I32),
            pltpu.SemaphoreType.DMA((2,)),
            pltpu.SemaphoreType.DMA((2,)),
        ],
        name="sc_dispatch",
    )
    def k(hp_hbm, idx_hbm, pidx_hbm, zeros_hbm, xs_hbm, idx_v, pidx_v, rows_v, lsem, ssem):
        wid = _sc_worker()
        base = wid * per_w
        pltpu.sync_copy(idx_hbm.at[wid], idx_v)
        pltpu.sync_copy(pidx_hbm.at[wid], pidx_v)

        pltpu.sync_copy(zeros_hbm, rows_v.at[0])
        for p in range(n_pc):
            pltpu.make_async_copy(rows_v.at[0], xs_hbm.at[pidx_v.at[p]], ssem.at[0]).start()
        for p in range(n_pc):
            pltpu.make_async_copy(rows_v.at[0], xs_hbm.at[pidx_v.at[p]], ssem.at[0]).wait()

        def load(i, slot):
            return pltpu.make_async_copy(hp_hbm.at[pl.ds(base + i * SC_CHUNK, SC_CHUNK)], rows_v.at[slot],
                                         lsem.at[slot])

        def scatter(i, kk, slot):
            return pltpu.make_async_copy(rows_v.at[slot], xs_hbm.at[idx_v.at[i * TOP_K + kk]], ssem.at[slot])

        load(0, 0).start()

        def body(i2, carry):
            for slot in range(2):
                i = i2 * 2 + slot
                nxt = 1 - slot

                @pl.when(i + 1 < n_ch)
                def _():
                    @pl.when(i >= 1)
                    def _():
                        for kk in range(TOP_K):
                            scatter(i - 1, kk, nxt).wait()
                    load(i + 1, nxt).start()

                load(i, slot).wait()
                for kk in range(TOP_K):
                    scatter(i, kk, slot).start()
            return carry

        lax.fori_loop(0, n_ch // 2, body, 0)
        for kk in range(TOP_K):
            scatter(n_ch - 2, kk, 0).wait()
            scatter(n_ch - 1, kk, 1).wait()

    return k(hp, idx, pidx, zeros)


def _sc_gather(table, idx):
    n = idx.shape[0]
    per_w = n // SC_WORKERS
    n_ch = per_w // SC_CHUNK
    assert per_w % SC_CHUNK == 0 and n_ch >= 2 and n_ch % 2 == 0

    @functools.partial(
        pl.kernel, mesh=_sc_mesh(),
        out_type=jax.ShapeDtypeStruct((n, HALF), I32),
        scratch_types=[
            pltpu.VMEM((n_ch, SC_CHUNK), I32),
            pltpu.VMEM((2, SC_CHUNK, HALF), I32),
            pltpu.SemaphoreType.DMA((2,)),
            pltpu.SemaphoreType.DMA((2,)),
        ],
        name="sc_gather",
    )
    def k(table_hbm, idx_hbm, out_hbm, idx_v, rows_v, gsem, wsem):
        wid = _sc_worker()
        base = wid * per_w
        pltpu.sync_copy(idx_hbm.at[wid], idx_v)

        def gather(i, slot):
            return pltpu.make_async_copy(table_hbm.at[idx_v.at[i]], rows_v.at[slot], gsem.at[slot])

        def writeback(i, slot):
            return pltpu.make_async_copy(rows_v.at[slot], out_hbm.at[pl.ds(base + i * SC_CHUNK, SC_CHUNK)],
                                         wsem.at[slot])

        gather(0, 0).start()

        def body(i2, carry):
            for slot in range(2):
                i = i2 * 2 + slot
                nxt = 1 - slot

                @pl.when(i + 1 < n_ch)
                def _():
                    @pl.when(i >= 1)
                    def _():
                        writeback(i - 1, nxt).wait()
                    gather(i + 1, nxt).start()

                gather(i, slot).wait()
                writeback(i, slot).start()
            return carry

        lax.fori_loop(0, n_ch // 2, body, 0)
        writeback(n_ch - 2, 0).wait()
        writeback(n_ch - 1, 1).wait()

    return k(table, idx.reshape(SC_WORKERS, n_ch, SC_CHUNK))


FF_CHUNK = 512


def _expert_body(te_ref, nv_ref, x_ref, w1f_ref, b1_ref, w2f_ref, b2_ref, y_ref, w1_ref, w2_ref):
    i = pl.program_id(0)

    @pl.when(jnp.logical_or(i == 0, te_ref[i] != te_ref[jnp.maximum(i - 1, 0)]))
    def _():
        w1_ref[...] = w1f_ref[...].astype(BF16)
        w2_ref[...] = w2f_ref[...].astype(BF16)

    @pl.when(i < nv_ref[0])
    def _():
        lo, hi = _unpack_rows(x_ref[...])
        xlo = lo.astype(BF16)
        xhi = hi.astype(BF16)
        acc = jnp.zeros((EXPERT_TM, D_MODEL), F32) + b2_ref[...]
        for c in range(D_FF // FF_CHUNK):
            def up(off):
                cs = slice(off + c * FF_CHUNK, off + (c + 1) * FF_CHUNK)
                return (jnp.dot(xlo, w1_ref[0:HALF, cs], preferred_element_type=F32)
                        + jnp.dot(xhi, w1_ref[HALF:D_MODEL, cs], preferred_element_type=F32)
                        + b1_ref[:, cs])
            g = jnp.minimum(up(0), SWIGLU_LIMIT)
            lin = jnp.clip(up(D_FF), -SWIGLU_LIMIT, SWIGLU_LIMIT)
            a = g * jax.nn.sigmoid(SWIGLU_ALPHA * g) * (lin + 1.0)
            acc = acc + jnp.dot(a.astype(BF16), w2_ref[c * FF_CHUNK:(c + 1) * FF_CHUNK, :],
                                preferred_element_type=F32)
        y_ref[...] = _pack_rows(acc)


def _experts(tile_expert, n_valid, xs, w1, b1, w2, b2):
    n_rows = xs.shape[0]
    tm = EXPERT_TM
    n_tiles = n_rows // tm
    row = lambda i, te, nv: (jnp.minimum(i, nv[0] - 1), 0)
    grid_spec = pltpu.PrefetchScalarGridSpec(
        num_scalar_prefetch=2,
        grid=(n_tiles,),
        in_specs=[
            pl.BlockSpec((tm, HALF), row),
            pl.BlockSpec((None, D_MODEL, 2 * D_FF), lambda i, te, nv: (te[i], 0, 0)),
            pl.BlockSpec((None, 1, 2 * D_FF), lambda i, te, nv: (te[i], 0, 0)),
            pl.BlockSpec((None, D_FF, D_MODEL), lambda i, te, nv: (te[i], 0, 0)),
            pl.BlockSpec((None, 1, D_MODEL), lambda i, te, nv: (te[i], 0, 0)),
        ],
        out_specs=pl.BlockSpec((tm, HALF), row),
        scratch_shapes=[pltpu.VMEM((D_MODEL, 2 * D_FF), BF16), pltpu.VMEM((D_FF, D_MODEL), BF16)],
    )
    return pl.pallas_call(
        _expert_body,
        grid_spec=grid_spec,
        out_shape=jax.ShapeDtypeStruct((n_rows, HALF), I32),
        compiler_params=_cparams(("arbitrary",)),
        name="experts",
    )(tile_expert, n_valid, xs, w1, b1, w2, b2)


COMBINE_TM = 512


def _combine_body(yg_ref, rw_ref, x_ref, g_ref, o_ref):
    acc = x_ref[...]
    rw = rw_ref[...]
    for k in range(TOP_K):
        lo, hi = _unpack_rows(yg_ref[:, k * HALF:(k + 1) * HALF])
        acc = acc + rw[:, k:k + 1] * jnp.concatenate([lo, hi], axis=-1)
    o_ref[...] = _rms(acc, g_ref[...])


def _combine(yg, rw, x2, g):
    T = x2.shape[0]
    tm = COMBINE_TM
    return pl.pallas_call(
        _combine_body,
        grid=(T // tm,),
        in_specs=[
            pl.BlockSpec((tm, TOP_K * HALF), lambda i: (i, 0)),
            pl.BlockSpec((tm, LANES), lambda i: (i, 0)),
            pl.BlockSpec((tm, D_MODEL), lambda i: (i, 0)),
            pl.BlockSpec((1, D_MODEL), lambda i: (0, 0)),
        ],
        out_specs=pl.BlockSpec((tm, D_MODEL), lambda i: (i, 0)),
        out_shape=jax.ShapeDtypeStruct((T, D_MODEL), F32),
        compiler_params=_cparams(("parallel",)),
        name="combine",
    )(yg, rw, x2, g)


def _pad_lanes(v, fill=0.0):
    v = v.reshape(1, -1).astype(F32)
    return jnp.pad(v, ((0, 0), (0, LANES - v.shape[1])), constant_values=fill)


def _layer(x2d, mem2d, B, S, M, norm_mix, w_in, b_ml_gates, conv_ml, ml_head_norm, b_fx_gate, norm_mem,
           w_mem_kv, w_branch, w_out, norm_moe, w_router, b_router, w_exp_in, b_exp_in, w_exp_out,
           b_exp_out, norm_out):
    T = B * S
    w_big = jnp.concatenate([w_in[:, 0:2048], w_in[:, 2056:3080], w_in[:, 3080:6152], w_in[:, 6160:7184],
                             w_in[:, 7184:10256]], axis=1).astype(BF16)
    w_small = jnp.concatenate([w_in[:, 2048:2056], w_in[:, 6152:6160]], axis=1)
    w_small = jnp.pad(w_small, ((0, 0), (0, LANES - w_small.shape[1]))).astype(BF16)
    row = lambda v: v.reshape(1, -1).astype(F32)

    proj, small = _inproj(x2d, row(norm_mix), w_big, w_small)

    y_ml = _mlstm(proj, small, conv_ml.astype(F32), _pad_lanes(b_ml_gates), row(ml_head_norm), B, S)

    b_fx = jnp.pad(b_fx_gate.reshape(1, -1).astype(F32), ((0, 0), (2 * ML_HEADS, LANES - 2 * ML_HEADS - FX_HEADS)))
    c_rows = _fox_gate(small, b_fx, B, S)
    y_fx = _fox_attn(proj, c_rows, B, S)

    kv = _memkv(mem2d, row(norm_mem), w_mem_kv.astype(BF16))
    y_ca = _memattn(proj, kv, B, S, M)

    x2 = _merge(y_ml, y_fx, y_ca, proj, x2d, w_branch.astype(BF16), w_out.astype(BF16))

    w_r = jnp.pad(w_router, ((0, 0), (0, LANES - N_EXPERTS))).astype(BF16)
    hp, ri, rw, cnt = _router(x2, row(norm_moe), w_r, _pad_lanes(b_router, fill=-1e30))

    tm = EXPERT_TM
    n_tiles = (T * TOP_K) // tm + N_EXPERTS
    counts = cnt[0, :N_EXPERTS].astype(I32)
    padded = ((counts + tm - 1) // tm) * tm
    gend = jnp.cumsum(padded)
    gstart = gend - padded
    expert_ids = jnp.arange(N_EXPERTS, dtype=I32)
    start_of = jnp.sum(jnp.where(ri[:, 0:TOP_K, None] == expert_ids, gstart, 0), axis=-1)
    dest = (start_of + ri[:, TOP_K:2 * TOP_K]).reshape(-1)
    n_valid = gend[-1] // tm
    tile_ids = jnp.arange(n_tiles, dtype=I32)
    last_tile = jnp.minimum(tile_ids, n_valid - 1)
    tile_e = jnp.minimum(jnp.sum((gend[None, :] <= last_tile[:, None] * tm).astype(I32), axis=1), N_EXPERTS - 1)

    slot = jnp.arange(tm, dtype=I32)
    spare = n_tiles * tm + slot % SC_CHUNK
    pad_idx = jnp.where(slot[None, :] < (padded - counts)[:, None], (gstart + counts)[:, None] + slot[None, :],
                        spare[None, :]).reshape(-1)

    xs = _sc_dispatch(hp, dest, pad_idx, n_tiles * tm + SC_CHUNK)
    ys = _experts(tile_e.astype(I32), n_valid.reshape(1).astype(I32), xs, w_exp_in.astype(F32),
                  b_exp_in.reshape(N_EXPERTS, 1, -1).astype(F32), w_exp_out.astype(F32),
                  b_exp_out.reshape(N_EXPERTS, 1, -1).astype(F32))
    yg = _sc_gather(ys, dest).reshape(T, TOP_K * HALF)
    return _combine(yg, rw, x2, row(norm_out))


def kernel(x, mem, norm_mix, w_in, b_ml_gates, conv_ml, ml_head_norm, b_fx_gate, norm_mem, w_mem_kv, w_branch,
           w_out, norm_moe, w_router, b_router, w_exp_in, b_exp_in, w_exp_out, b_exp_out, norm_final):
    B, S, D = x.shape
    M = mem.shape[1]
    depth = norm_mix.shape[0]
    assert depth == 1, "the combine kernel fuses the final norm, so exactly one layer is supported"
    assert D == D_MODEL and S % ML_BLOCK == 0 and S % FX_T == 0 and S % CA_TQ == 0
    out = _layer(x.reshape(B * S, D), mem.reshape(B * M, D), B, S, M, norm_mix[0], w_in[0], b_ml_gates[0],
                 conv_ml[0], ml_head_norm[0], b_fx_gate[0], norm_mem[0], w_mem_kv[0], w_branch[0], w_out[0],
                 norm_moe[0], w_router[0], b_router[0], w_exp_in[0], b_exp_in[0], w_exp_out[0], b_exp_out[0],
                 norm_final)
    return out.reshape(B, S, D)
```

```python
import functools

import jax
import jax.numpy as jnp
from jax import lax
from jax.experimental import pallas as pl
from jax.experimental.pallas import tpu as pltpu
from jax.experimental.pallas import tpu_sc as plsc

F32 = jnp.float32
BF16 = jnp.bfloat16
I32 = jnp.int32

D_MODEL = 1024
N_MEM_HEADS = 4
ML_HEADS = 4
ML_DQK = 128
ML_DV = 256
ML_CONV = 4
FX_HEADS = 8
FX_DH = 128
CA_HEADS = 4
CA_DH = 256
N_EXPERTS = 32
TOP_K = 4
D_FF = D_MODEL
SWIGLU_LIMIT = 7.0
SWIGLU_ALPHA = 1.702
EPS = 1e-5
LANES = 128
HALF = D_MODEL // 2
HI_MASK = -65536

COL_MLQK, COL_MLV, COL_MLO, COL_FXQ, COL_FXK, COL_FXV, COL_CAQ, COL_GATE0 = 0, 1, 2, 3, 4, 5, 6, 7
N_BIG = 10 * D_MODEL

VMEM_LIMIT = 56 * 1024 * 1024


def _cparams(sem):
    return pltpu.CompilerParams(dimension_semantics=sem, vmem_limit_bytes=VMEM_LIMIT)


def _rms(x, g):
    return x * lax.rsqrt(jnp.mean(x * x, axis=-1, keepdims=True) + EPS) * g


def _log_sigmoid(x):
    return jnp.minimum(x, 0.0) - jnp.log1p(jnp.exp(-jnp.abs(x)))


def _pack_rows(y):
    bits = lax.bitcast_convert_type(y.astype(BF16).astype(F32), I32)
    return lax.shift_right_logical(bits[:, :HALF], 16) | (bits[:, HALF:] & HI_MASK)


def _unpack_rows(w):
    lo = lax.bitcast_convert_type(lax.shift_left(w, 16), F32)
    hi = lax.bitcast_convert_type(w & HI_MASK, F32)
    return lo, hi


def _inproj_body(x_ref, g_ref, w_ref, ws_ref, o_ref, os_ref, h_ref):
    @pl.when(pl.program_id(1) == 0)
    def _():
        hb = _rms(x_ref[...], g_ref[...]).astype(BF16)
        h_ref[...] = hb
        os_ref[...] = jnp.dot(hb, ws_ref[...], preferred_element_type=F32)

    o_ref[...] = jnp.dot(h_ref[...], w_ref[...], preferred_element_type=F32).astype(BF16)


def _inproj(x2d, g, w_big, w_small):
    T = x2d.shape[0]
    tm = min(1024, T)
    tn = 2048
    return pl.pallas_call(
        _inproj_body,
        grid=(T // tm, N_BIG // tn),
        in_specs=[
            pl.BlockSpec((tm, D_MODEL), lambda i, j: (i, 0)),
            pl.BlockSpec((1, D_MODEL), lambda i, j: (0, 0)),
            pl.BlockSpec((D_MODEL, tn), lambda i, j: (0, j)),
            pl.BlockSpec((D_MODEL, LANES), lambda i, j: (0, 0)),
        ],
        out_specs=[
            pl.BlockSpec((tm, tn), lambda i, j: (i, j)),
            pl.BlockSpec((tm, LANES), lambda i, j: (i, 0)),
        ],
        out_shape=[
            jax.ShapeDtypeStruct((T, N_BIG), BF16),
            jax.ShapeDtypeStruct((T, LANES), F32),
        ],
        scratch_shapes=[pltpu.VMEM((tm, D_MODEL), BF16)],
        compiler_params=_cparams(("parallel", "arbitrary")),
        name="inproj",
    )(x2d, g, w_big, w_small)


ML_BLOCK = 512
ML_MB = 1
ML_CHUNK = 128
CONV_PAD = 8


def _mlstm_body(qk_ref, v_ref, o_ref, g_ref, cw_ref, bg_ref, hn_ref, y_ref, xbuf, c_st, n_st, m_st):
    L = ML_CHUNK

    @pl.when(pl.program_id(1) == 0)
    def _():
        xbuf[:, 0:CONV_PAD, :] = jnp.zeros((ML_MB, CONV_PAD, D_MODEL), F32)
        c_st[...] = jnp.zeros_like(c_st)
        n_st[...] = jnp.zeros_like(n_st)
        m_st[...] = jnp.zeros_like(m_st)

    for bb in range(ML_MB):
        xbuf[bb, CONV_PAD:CONV_PAD + ML_BLOCK, :] = qk_ref[bb].astype(F32)
    cw = cw_ref[...]
    row = lax.broadcasted_iota(I32, (L, L), 0)
    col = lax.broadcasted_iota(I32, (L, L), 1)
    causal = row >= col
    tri = causal.astype(F32)
    bg = bg_ref[...]
    scale = ML_DQK ** -0.5

    def chunk(bb, c):
        r0 = c * L
        conv = cw[0:1, :] * xbuf[bb, r0 + CONV_PAD - 3:r0 + CONV_PAD - 3 + L, :]
        for j in range(1, ML_CONV):
            s0 = r0 + CONV_PAD - 3 + j
            conv = conv + cw[j:j + 1, :] * xbuf[bb, s0:s0 + L, :]
        act = conv * jax.nn.sigmoid(conv)

        gates = g_ref[bb, r0:r0 + L, :] + bg
        cum = jnp.dot(tri, _log_sigmoid(gates), precision=lax.Precision.HIGHEST,
                      preferred_element_type=F32)
        gates_t = gates.T
        cum_t = cum.T
        for h in range(ML_HEADS):
            b_col = cum[:, ML_HEADS + h:ML_HEADS + h + 1]
            i_col = gates[:, h:h + 1]
            b_row = cum_t[ML_HEADS + h:ML_HEADS + h + 1, :]
            i_row = gates_t[h:h + 1, :]
            st = bb * ML_HEADS + h
            m_prev = m_st[st]
            dm = jnp.where(causal, b_col + (i_row - b_row), -jnp.inf)
            m_inter = b_col + m_prev
            m_t = jnp.maximum(jnp.max(dm, axis=-1, keepdims=True), m_inter)
            w_intra = jnp.exp(dm - m_t)
            w_inter = jnp.exp(m_inter - m_t)

            qh = act[:, h * ML_DQK:(h + 1) * ML_DQK] * scale
            kh = act[:, (ML_HEADS + h) * ML_DQK:(ML_HEADS + h + 1) * ML_DQK]
            qb = qh.astype(BF16)
            vb = v_ref[bb, r0:r0 + L, h * ML_DV:(h + 1) * ML_DV]
            s = lax.dot_general(qb, kh.astype(BF16), (((1,), (1,)), ((), ())),
                                preferred_element_type=F32) * w_intra
            c_old = c_st[st]
            n_old = n_st[st]
            num = jnp.dot(s.astype(BF16), vb, preferred_element_type=F32) + w_inter * jnp.dot(
                qb, c_old.astype(BF16), preferred_element_type=F32)
            den = jnp.sum(s, axis=-1, keepdims=True) + w_inter * jnp.sum(qh * n_old, axis=-1, keepdims=True)
            hv = num / jnp.maximum(jnp.abs(den), jnp.exp(-m_t))

            m_new = m_t[L - 1:L, :]
            b_last = b_col[L - 1:L, :]
            wk = jnp.exp(b_last - b_col + i_col - m_new)
            decay = jnp.exp(b_last + m_prev - m_new)
            kw = kh * wk
            c_st[st] = decay * c_old + jnp.dot(kw.T.astype(BF16), vb, preferred_element_type=F32)
            n_st[st] = decay * n_old + jnp.sum(kw, axis=0, keepdims=True)
            m_st[st] = m_new

            hn = _rms(hv, hn_ref[:, h * ML_DV:(h + 1) * ML_DV])
            og = o_ref[bb, r0:r0 + L, h * ML_DV:(h + 1) * ML_DV].astype(F32)
            y_ref[bb, r0:r0 + L, h * ML_DV:(h + 1) * ML_DV] = (hn * jax.nn.sigmoid(og)).astype(BF16)

    for c in range(ML_BLOCK // L):
        for bb in range(ML_MB):
            chunk(bb, c)

    xbuf[:, 0:CONV_PAD, :] = xbuf[:, ML_BLOCK:ML_BLOCK + CONV_PAD, :]


def _mlstm(proj, small, conv_w, b_gates, head_norm, B, S):
    T = B * S
    ns = S // ML_BLOCK
    assert B % ML_MB == 0
    proj3 = proj.reshape(B, S, N_BIG)
    blk = lambda col: pl.BlockSpec((ML_MB, ML_BLOCK, D_MODEL), lambda b, s: (b, s, col))
    out = pl.pallas_call(
        _mlstm_body,
        grid=(B // ML_MB, ns),
        in_specs=[
            blk(COL_MLQK),
            blk(COL_MLV),
            blk(COL_MLO),
            pl.BlockSpec((ML_MB, ML_BLOCK, LANES), lambda b, s: (b, s, 0)),
            pl.BlockSpec((ML_CONV, D_MODEL), lambda b, s: (0, 0)),
            pl.BlockSpec((1, LANES), lambda b, s: (0, 0)),
            pl.BlockSpec((1, D_MODEL), lambda b, s: (0, 0)),
        ],
        out_specs=blk(0),
        out_shape=jax.ShapeDtypeStruct((B, S, D_MODEL), BF16),
        scratch_shapes=[
            pltpu.VMEM((ML_MB, ML_BLOCK + CONV_PAD, D_MODEL), F32),
            pltpu.VMEM((ML_MB * ML_HEADS, ML_DQK, ML_DV), F32),
            pltpu.VMEM((ML_MB * ML_HEADS, 1, ML_DQK), F32),
            pltpu.VMEM((ML_MB * ML_HEADS, 1, 1), F32),
        ],
        compiler_params=_cparams(("parallel", "arbitrary")),
        name="mlstm",
    )(proj3, proj3, proj3, small.reshape(B, S, LANES), conv_w, b_gates, head_norm)
    return out.reshape(T, D_MODEL)


FX_T = 512
FX_HP = 2
LOG2E = 1.4426950408889634


def _fox_gate_body(g_ref, b_ref, o_ref):
    S = g_ref.shape[0]
    nk = S // FX_T
    row = lax.broadcasted_iota(I32, (FX_T, FX_T), 0)
    col = lax.broadcasted_iota(I32, (FX_T, FX_T), 1)
    tri = (row >= col).astype(F32)
    carry = jnp.zeros((1, LANES), F32)
    for blk in range(nk):
        lf = _log_sigmoid(g_ref[blk * FX_T:(blk + 1) * FX_T, :] + b_ref[...])
        cum = jnp.dot(tri, lf, precision=lax.Precision.HIGHEST, preferred_element_type=F32) + carry
        carry = cum[FX_T - 1:FX_T, :]
        cum_t = (cum * LOG2E).T
        for h in range(FX_HEADS):
            o_ref[h, blk] = cum_t[2 * ML_HEADS + h:2 * ML_HEADS + h + 1, :]


def _fox_gate(small, b_fx, B, S):
    nk = S // FX_T
    return pl.pallas_call(
        _fox_gate_body,
        grid=(B,),
        in_specs=[
            pl.BlockSpec((S, LANES), lambda b: (b, 0)),
            pl.BlockSpec((1, LANES), lambda b: (0, 0)),
        ],
        out_specs=pl.BlockSpec((None, FX_HEADS, nk, 1, FX_T), lambda b: (b, 0, 0, 0, 0)),
        out_shape=jax.ShapeDtypeStruct((B, FX_HEADS, nk, 1, FX_T), F32),
        compiler_params=_cparams(("parallel",)),
        name="fox_gate",
    )(small, b_fx)


def _fox_attn_body(q_ref, k_ref, v_ref, c_ref, o_ref, m_ref, acc_ref, s_ref):
    i = pl.program_id(2)
    ones_col = (lax.broadcasted_iota(I32, (FX_T, FX_DH), 1) == 0).astype(BF16)
    heads = []
    for hh in range(FX_HP):
        sl = slice(hh * FX_DH, (hh + 1) * FX_DH)
        q = (q_ref[:, sl].astype(F32) * (FX_DH ** -0.5 * LOG2E)).astype(BF16)
        heads.append((hh, sl, q, c_ref[hh, i][:, 0:1]))
    m_ref[...] = jnp.full(m_ref.shape, -jnp.inf, F32)
    acc_ref[...] = jnp.zeros(acc_ref.shape, F32)

    def scores(j, slot):
        r0 = pl.multiple_of(j * FX_T, FX_T)
        for hh, sl, q, c_q in heads:
            s = lax.dot_general(q, k_ref[pl.ds(r0, FX_T), sl], (((1,), (1,)), ((), ())),
                                preferred_element_type=F32)
            s_ref[slot, hh] = s + (c_q - c_ref[hh, j])

    def consume(j, slot, masked):
        r0 = pl.multiple_of(j * FX_T, FX_T)
        for hh, sl, _, _ in heads:
            s = s_ref[slot, hh]
            if masked:
                row = lax.broadcasted_iota(I32, (FX_T, FX_T), 0)
                col = lax.broadcasted_iota(I32, (FX_T, FX_T), 1)
                s = jnp.where(row >= col, s, -jnp.inf)
            m_old = m_ref[hh]
            m_new = jnp.maximum(m_old, jnp.max(s, axis=-1, keepdims=True))
            p = jnp.exp2(s - m_new).astype(BF16)
            v1 = jnp.concatenate([v_ref[pl.ds(r0, FX_T), sl], ones_col], axis=-1)
            acc_ref[hh] = jnp.exp2(m_old - m_new) * acc_ref[hh] + jnp.dot(p, v1, preferred_element_type=F32)
            m_ref[hh] = m_new

    scores(0, 0)

    def pair(jj, carry):
        j = 2 * jj
        scores(j + 1, 1)
        consume(j, 0, False)
        scores(j + 2, 0)
        consume(j + 1, 1, False)
        return carry

    lax.fori_loop(0, i // 2, pair, 0)

    @pl.when(i % 2 == 1)
    def _():
        scores(i, 1)
        consume(i - 1, 0, False)
        consume(i, 1, True)

    @pl.when(i % 2 == 0)
    def _():
        consume(i, 0, True)

    for hh, sl, _, _ in heads:
        acc = acc_ref[hh]
        o_ref[:, sl] = (acc[:, :FX_DH] / acc[:, FX_DH:FX_DH + 1]).astype(BF16)


def _fox_attn(proj, c_rows, B, S):
    T = B * S
    nq = S // FX_T
    wide = FX_HP * FX_DH
    cq = COL_FXQ * (D_MODEL // wide)
    ck = COL_FXK * (D_MODEL // wide)
    cv = COL_FXV * (D_MODEL // wide)
    proj3 = proj.reshape(B, S, N_BIG)
    out = pl.pallas_call(
        _fox_attn_body,
        grid=(B, FX_HEADS // FX_HP, nq),
        in_specs=[
            pl.BlockSpec((None, FX_T, wide), lambda b, h, i: (b, i, cq + h)),
            pl.BlockSpec((None, S, wide), lambda b, h, i: (b, 0, ck + h)),
            pl.BlockSpec((None, S, wide), lambda b, h, i: (b, 0, cv + h)),
            pl.BlockSpec((None, FX_HP, nq, 1, FX_T), lambda b, h, i: (b, h, 0, 0, 0)),
        ],
        out_specs=pl.BlockSpec((None, FX_T, wide), lambda b, h, i: (b, i, h)),
        out_shape=jax.ShapeDtypeStruct((B, S, D_MODEL), BF16),
        scratch_shapes=[
            pltpu.VMEM((FX_HP, FX_T, 1), F32),
            pltpu.VMEM((FX_HP, FX_T, 2 * FX_DH), F32),
            pltpu.VMEM((2, FX_HP, FX_T, FX_T), F32),
        ],
        compiler_params=_cparams(("parallel", "parallel", "arbitrary")),
        name="fox_attn",
    )(proj3, proj3, proj3, c_rows)
    return out.reshape(T, D_MODEL)


def _memkv_body(x_ref, g_ref, w_ref, o_ref):
    hb = _rms(x_ref[...], g_ref[...]).astype(BF16)
    o_ref[...] = jnp.dot(hb, w_ref[...], preferred_element_type=F32).astype(BF16)


def _memkv(mem2d, g, w_kv):
    R = mem2d.shape[0]
    tm = min(512, R)
    N = w_kv.shape[1]
    return pl.pallas_call(
        _memkv_body,
        grid=(R // tm,),
        in_specs=[
            pl.BlockSpec((tm, D_MODEL), lambda i: (i, 0)),
            pl.BlockSpec((1, D_MODEL), lambda i: (0, 0)),
            pl.BlockSpec((D_MODEL, N), lambda i: (0, 0)),
        ],
        out_specs=pl.BlockSpec((tm, N), lambda i: (i, 0)),
        out_shape=jax.ShapeDtypeStruct((R, N), BF16),
        compiler_params=_cparams(("parallel",)),
        name="memkv",
    )(mem2d, g, w_kv)


CA_TQ = 512


def _memattn_body(q_ref, k_ref, v_ref, o_ref):
    scale = CA_DH ** -0.5
    for h in range(CA_HEADS):
        sl = slice(h * CA_DH, (h + 1) * CA_DH)
        s = lax.dot_general(q_ref[:, sl], k_ref[:, sl], (((1,), (1,)), ((), ())),
                            preferred_element_type=F32) * scale
        p = jnp.exp(s - jnp.max(s, axis=-1, keepdims=True))
        l = jnp.sum(p, axis=-1, keepdims=True)
        o = jnp.dot(p.astype(BF16), v_ref[:, sl], preferred_element_type=F32) / l
        o_ref[:, sl] = o.astype(BF16)


def _memattn(proj, kv, B, S, M):
    T = B * S
    nq = S // CA_TQ
    kv3 = kv.reshape(B, M, 2 * D_MODEL)
    return pl.pallas_call(
        _memattn_body,
        grid=(B, nq),
        in_specs=[
            pl.BlockSpec((CA_TQ, D_MODEL), lambda b, i: (b * nq + i, COL_CAQ)),
            pl.BlockSpec((None, M, D_MODEL), lambda b, i: (b, 0, 0)),
            pl.BlockSpec((None, M, D_MODEL), lambda b, i: (b, 0, 1)),
        ],
        out_specs=pl.BlockSpec((CA_TQ, D_MODEL), lambda b, i: (b * nq + i, 0)),
        out_shape=jax.ShapeDtypeStruct((T, D_MODEL), BF16),
        compiler_params=_cparams(("parallel", "arbitrary")),
        name="memattn",
    )(proj, kv3, kv3)


MERGE_TM = 512


def _merge_body(y0_ref, y1_ref, y2_ref, g0_ref, g1_ref, g2_ref, x_ref, wb_ref, wo_ref, gn_ref, wr_ref, br_ref,
                o_ref, hp_ref, ri_ref, rw_ref, cnt_ref, carry_ref):
    merged = None
    for n, (y_ref, g_ref) in enumerate(((y0_ref, g0_ref), (y1_ref, g1_ref), (y2_ref, g2_ref))):
        p = jnp.dot(y_ref[...], wb_ref[n], preferred_element_type=F32)
        t = jax.nn.sigmoid(g_ref[...].astype(F32)) * p
        merged = t if merged is None else merged + t
    x2 = x_ref[...] + jnp.dot(merged.astype(BF16), wo_ref[...], preferred_element_type=F32)
    o_ref[...] = x2
    _route(x2, gn_ref, wr_ref, br_ref, hp_ref, ri_ref, rw_ref, cnt_ref, carry_ref)


def _merge(y_ml, y_fx, y_ca, proj, x2d, w_branch, w_out, g_moe, w_router, b_router):
    T = x2d.shape[0]
    tm = MERGE_TM
    row = lambda i: (i, 0)
    const = lambda i: (0, 0)
    return pl.pallas_call(
        _merge_body,
        grid=(T // tm,),
        in_specs=[
            pl.BlockSpec((tm, D_MODEL), row),
            pl.BlockSpec((tm, D_MODEL), row),
            pl.BlockSpec((tm, D_MODEL), row),
            pl.BlockSpec((tm, D_MODEL), lambda i: (i, COL_GATE0)),
            pl.BlockSpec((tm, D_MODEL), lambda i: (i, COL_GATE0 + 1)),
            pl.BlockSpec((tm, D_MODEL), lambda i: (i, COL_GATE0 + 2)),
            pl.BlockSpec((tm, D_MODEL), row),
            pl.BlockSpec((3, D_MODEL, D_MODEL), lambda i: (0, 0, 0)),
            pl.BlockSpec((D_MODEL, D_MODEL), const),
            pl.BlockSpec((1, D_MODEL), const),
            pl.BlockSpec((D_MODEL, LANES), const),
            pl.BlockSpec((1, LANES), const),
        ],
        out_specs=[
            pl.BlockSpec((tm, D_MODEL), row),
            pl.BlockSpec((tm, HALF), row),
            pl.BlockSpec((tm, LANES), row),
            pl.BlockSpec((tm, LANES), row),
            pl.BlockSpec((1, LANES), const),
        ],
        out_shape=[
            jax.ShapeDtypeStruct((T, D_MODEL), F32),
            jax.ShapeDtypeStruct((T, HALF), I32),
            jax.ShapeDtypeStruct((T, LANES), I32),
            jax.ShapeDtypeStruct((T, LANES), F32),
            jax.ShapeDtypeStruct((1, LANES), F32),
        ],
        scratch_shapes=[pltpu.VMEM((1, LANES), F32)],
        compiler_params=_cparams(("arbitrary",)),
        name="merge_router",
    )(y_ml, y_fx, y_ca, proj, proj, proj, x2d, w_branch, w_out, g_moe, w_router, b_router)


def _route(x2, g_ref, wr_ref, br_ref, hp_ref, ri_ref, rw_ref, cnt_ref, carry_ref):
    tm = MERGE_TM

    @pl.when(pl.program_id(0) == 0)
    def _():
        carry_ref[...] = jnp.zeros_like(carry_ref)

    h = _rms(x2, g_ref[...])
    hp_ref[...] = _pack_rows(h)
    logits = jnp.dot(h.astype(BF16), wr_ref[...], preferred_element_type=F32) + br_ref[...]
    lane = lax.broadcasted_iota(I32, (tm, LANES), 1)
    lane_f = lane.astype(F32)

    work = logits
    onehot_sum = jnp.zeros((tm, LANES), F32)
    vals, sels, idxs = [], [], []
    for _ in range(TOP_K):
        mx = jnp.max(work, axis=-1, keepdims=True)
        idx = jnp.min(jnp.where(work == mx, lane_f, float(LANES)), axis=-1, keepdims=True)
        sel = lane_f == idx
        onehot_sum = onehot_sum + sel.astype(F32)
        work = jnp.where(sel, -jnp.inf, work)
        vals.append(mx)
        sels.append(sel)
        idxs.append(idx)
    exps = [jnp.exp(v - vals[0]) for v in vals]
    total = exps[0] + exps[1] + exps[2] + exps[3]

    row = lax.broadcasted_iota(I32, (tm, tm), 0)
    col = lax.broadcasted_iota(I32, (tm, tm), 1)
    strict = (row > col).astype(BF16)
    before = jnp.dot(strict, onehot_sum.astype(BF16), preferred_element_type=F32) + carry_ref[...]
    carry_ref[...] = carry_ref[...] + jnp.sum(onehot_sum, axis=0, keepdims=True)
    cnt_ref[...] = carry_ref[...]

    ri = jnp.zeros((tm, LANES), I32)
    rw = jnp.zeros((tm, LANES), F32)
    for k in range(TOP_K):
        rank = jnp.sum(jnp.where(sels[k], before, 0.0), axis=-1, keepdims=True)
        ri = jnp.where(lane == k, idxs[k].astype(I32), ri)
        ri = jnp.where(lane == TOP_K + k, rank.astype(I32), ri)
        rw = jnp.where(lane == k, exps[k] / total, rw)
    ri_ref[...] = ri
    rw_ref[...] = rw


EXPERT_TM = 512
SC_CORES = 2
SC_SUBCORES = 16
SC_WORKERS = SC_CORES * SC_SUBCORES
SC_CHUNK = 64
PAD_SLOTS = N_EXPERTS * EXPERT_TM


def _sc_mesh():
    return plsc.VectorSubcoreMesh(core_axis_name="c", subcore_axis_name="s")


def _sc_worker():
    return lax.axis_index("s") * SC_CORES + lax.axis_index("c")


def _sc_dispatch(hp, dest, pad_idx, n_rows):
    T = hp.shape[0]
    per_w = T // SC_WORKERS
    n_ch = per_w // SC_CHUNK
    n_pc = PAD_SLOTS // (SC_WORKERS * SC_CHUNK)
    assert per_w % SC_CHUNK == 0 and n_ch >= 2 and n_ch % 2 == 0
    idx = dest.reshape(SC_WORKERS, n_ch, SC_CHUNK, TOP_K).transpose(0, 1, 3, 2)
    idx = idx.reshape(SC_WORKERS, n_ch * TOP_K, SC_CHUNK)
    pidx = pad_idx.reshape(SC_WORKERS, n_pc, SC_CHUNK)
    zeros = jnp.zeros((SC_CHUNK, HALF), I32)

    @functools.partial(
        pl.kernel, mesh=_sc_mesh(),
        out_type=jax.ShapeDtypeStruct((n_rows, HALF), I32),
        scratch_types=[
            pltpu.VMEM((n_ch * TOP_K, SC_CHUNK), I32),
            pltpu.VMEM((n_pc, SC_CHUNK), I32),
            pltpu.VMEM((2, SC_CHUNK, HALF), I32),
            pltpu.SemaphoreType.DMA((2,)),
            pltpu.SemaphoreType.DMA((2,)),
        ],
        name="sc_dispatch",
    )
    def k(hp_hbm, idx_hbm, pidx_hbm, zeros_hbm, xs_hbm, idx_v, pidx_v, rows_v, lsem, ssem):
        wid = _sc_worker()
        base = wid * per_w
        pltpu.sync_copy(idx_hbm.at[wid], idx_v)
        pltpu.sync_copy(pidx_hbm.at[wid], pidx_v)

        pltpu.sync_copy(zeros_hbm, rows_v.at[0])
        for p in range(n_pc):
            pltpu.make_async_copy(rows_v.at[0], xs_hbm.at[pidx_v.at[p]], ssem.at[0]).start()
        for p in range(n_pc):
            pltpu.make_async_copy(rows_v.at[0], xs_hbm.at[pidx_v.at[p]], ssem.at[0]).wait()

        def load(i, slot):
            return pltpu.make_async_copy(hp_hbm.at[pl.ds(base + i * SC_CHUNK, SC_CHUNK)], rows_v.at[slot],
                                         lsem.at[slot])

        def scatter(i, kk, slot):
            return pltpu.make_async_copy(rows_v.at[slot], xs_hbm.at[idx_v.at[i * TOP_K + kk]], ssem.at[slot])

        load(0, 0).start()

        def body(i2, carry):
            for slot in range(2):
                i = i2 * 2 + slot
                nxt = 1 - slot

                @pl.when(i + 1 < n_ch)
                def _():
                    @pl.when(i >= 1)
                    def _():
                        for kk in range(TOP_K):
                            scatter(i - 1, kk, nxt).wait()
                    load(i + 1, nxt).start()

                load(i, slot).wait()
                for kk in range(TOP_K):
                    scatter(i, kk, slot).start()
            return carry

        lax.fori_loop(0, n_ch // 2, body, 0)
        for kk in range(TOP_K):
            scatter(n_ch - 2, kk, 0).wait()
            scatter(n_ch - 1, kk, 1).wait()

    return k(hp, idx, pidx, zeros)


def _sc_gather(table, idx):
    n = idx.shape[0]
    per_w = n // SC_WORKERS
    n_ch = per_w // SC_CHUNK
    assert per_w % SC_CHUNK == 0 and n_ch >= 2 and n_ch % 2 == 0

    @functools.partial(
        pl.kernel, mesh=_sc_mesh(),
        out_type=jax.ShapeDtypeStruct((n, HALF), I32),
        scratch_types=[
            pltpu.VMEM((n_ch, SC_CHUNK), I32),
            pltpu.VMEM((2, SC_CHUNK, HALF), I32),
            pltpu.SemaphoreType.DMA((2,)),
            pltpu.SemaphoreType.DMA((2,)),
        ],
        name="sc_gather",
    )
    def k(table_hbm, idx_hbm, out_hbm, idx_v, rows_v, gsem, wsem):
        wid = _sc_worker()
        base = wid * per_w
        pltpu.sync_copy(idx_hbm.at[wid], idx_v)

        def gather(i, slot):
            return pltpu.make_async_copy(table_hbm.at[idx_v.at[i]], rows_v.at[slot], gsem.at[slot])

        def writeback(i, slot):
            return pltpu.make_async_copy(rows_v.at[slot], out_hbm.at[pl.ds(base + i * SC_CHUNK, SC_CHUNK)],
                                         wsem.at[slot])

        gather(0, 0).start()

        def body(i2, carry):
            for slot in range(2):
                i = i2 * 2 + slot
                nxt = 1 - slot

                @pl.when(i + 1 < n_ch)
                def _():
                    @pl.when(i >= 1)
                    def _():
                        writeback(i - 1, nxt).wait()
                    gather(i + 1, nxt).start()

                gather(i, slot).wait()
                writeback(i, slot).start()
            return carry

        lax.fori_loop(0, n_ch // 2, body, 0)
        writeback(n_ch - 2, 0).wait()
        writeback(n_ch - 1, 1).wait()

    return k(table, idx.reshape(SC_WORKERS, n_ch, SC_CHUNK))


FF_CHUNK = 512


def _expert_body(te_ref, nv_ref, x_ref, w1f_ref, b1_ref, w2f_ref, b2_ref, y_ref, w1_ref, w2_ref):
    i = pl.program_id(0)

    @pl.when(jnp.logical_or(i == 0, te_ref[i] != te_ref[jnp.maximum(i - 1, 0)]))
    def _():
        w1_ref[...] = w1f_ref[...].astype(BF16)
        w2_ref[...] = w2f_ref[...].astype(BF16)

    @pl.when(i < nv_ref[0])
    def _():
        lo, hi = _unpack_rows(x_ref[...])
        xlo = lo.astype(BF16)
        xhi = hi.astype(BF16)
        acc = jnp.zeros((EXPERT_TM, D_MODEL), F32) + b2_ref[...]
        for c in range(D_FF // FF_CHUNK):
            def up(off):
                cs = slice(off + c * FF_CHUNK, off + (c + 1) * FF_CHUNK)
                return (jnp.dot(xlo, w1_ref[0:HALF, cs], preferred_element_type=F32)
                        + jnp.dot(xhi, w1_ref[HALF:D_MODEL, cs], preferred_element_type=F32)
                        + b1_ref[:, cs])
            g = jnp.minimum(up(0), SWIGLU_LIMIT)
            lin = jnp.clip(up(D_FF), -SWIGLU_LIMIT, SWIGLU_LIMIT)
            a = g * jax.nn.sigmoid(SWIGLU_ALPHA * g) * (lin + 1.0)
            acc = acc + jnp.dot(a.astype(BF16), w2_ref[c * FF_CHUNK:(c + 1) * FF_CHUNK, :],
                                preferred_element_type=F32)
        y_ref[...] = _pack_rows(acc)


def _experts(tile_expert, n_valid, xs, w1, b1, w2, b2):
    n_rows = xs.shape[0]
    tm = EXPERT_TM
    n_tiles = n_rows // tm
    row = lambda i, te, nv: (jnp.minimum(i, nv[0] - 1), 0)
    grid_spec = pltpu.PrefetchScalarGridSpec(
        num_scalar_prefetch=2,
        grid=(n_tiles,),
        in_specs=[
            pl.BlockSpec((tm, HALF), row),
            pl.BlockSpec((None, D_MODEL, 2 * D_FF), lambda i, te, nv: (te[i], 0, 0)),
            pl.BlockSpec((None, 1, 2 * D_FF), lambda i, te, nv: (te[i], 0, 0)),
            pl.BlockSpec((None, D_FF, D_MODEL), lambda i, te, nv: (te[i], 0, 0)),
            pl.BlockSpec((None, 1, D_MODEL), lambda i, te, nv: (te[i], 0, 0)),
        ],
        out_specs=pl.BlockSpec((tm, HALF), row),
        scratch_shapes=[pltpu.VMEM((D_MODEL, 2 * D_FF), BF16), pltpu.VMEM((D_FF, D_MODEL), BF16)],
    )
    return pl.pallas_call(
        _expert_body,
        grid_spec=grid_spec,
        out_shape=jax.ShapeDtypeStruct((n_rows, HALF), I32),
        compiler_params=_cparams(("arbitrary",)),
        name="experts",
    )(tile_expert, n_valid, xs, w1, b1, w2, b2)


COMBINE_TM = 512


def _combine_body(y0_ref, y1_ref, y2_ref, y3_ref, rw_ref, x_ref, g_ref, o_ref):
    acc = x_ref[...]
    rw = rw_ref[...]
    for k, y_ref in enumerate((y0_ref, y1_ref, y2_ref, y3_ref)):
        lo, hi = _unpack_rows(y_ref[...])
        acc = acc + rw[:, k:k + 1] * jnp.concatenate([lo, hi], axis=-1)
    o_ref[...] = _rms(acc, g_ref[...])


def _combine(yg, rw, x2, g):
    T = x2.shape[0]
    tm = COMBINE_TM
    nt = T // tm
    return pl.pallas_call(
        _combine_body,
        grid=(nt,),
        in_specs=[
            pl.BlockSpec((tm, HALF), lambda i: (i, 0)),
            pl.BlockSpec((tm, HALF), lambda i: (nt + i, 0)),
            pl.BlockSpec((tm, HALF), lambda i: (2 * nt + i, 0)),
            pl.BlockSpec((tm, HALF), lambda i: (3 * nt + i, 0)),
            pl.BlockSpec((tm, LANES), lambda i: (i, 0)),
            pl.BlockSpec((tm, D_MODEL), lambda i: (i, 0)),
            pl.BlockSpec((1, D_MODEL), lambda i: (0, 0)),
        ],
        out_specs=pl.BlockSpec((tm, D_MODEL), lambda i: (i, 0)),
        out_shape=jax.ShapeDtypeStruct((T, D_MODEL), F32),
        compiler_params=_cparams(("parallel",)),
        name="combine",
    )(yg, yg, yg, yg, rw, x2, g)


def _pad_lanes(v, fill=0.0):
    v = v.reshape(1, -1).astype(F32)
    return jnp.pad(v, ((0, 0), (0, LANES - v.shape[1])), constant_values=fill)


def _layer(x2d, mem2d, B, S, M, norm_mix, w_in, b_ml_gates, conv_ml, ml_head_norm, b_fx_gate, norm_mem,
           w_mem_kv, w_branch, w_out, norm_moe, w_router, b_router, w_exp_in, b_exp_in, w_exp_out,
           b_exp_out, norm_out):
    T = B * S
    w_big = jnp.concatenate([w_in[:, 0:2048], w_in[:, 2056:3080], w_in[:, 3080:6152], w_in[:, 6160:7184],
                             w_in[:, 7184:10256]], axis=1).astype(BF16)
    w_small = jnp.concatenate([w_in[:, 2048:2056], w_in[:, 6152:6160]], axis=1)
    w_small = jnp.pad(w_small, ((0, 0), (0, LANES - w_small.shape[1]))).astype(BF16)
    row = lambda v: v.reshape(1, -1).astype(F32)

    proj, small = _inproj(x2d, row(norm_mix), w_big, w_small)

    y_ml = _mlstm(proj, small, conv_ml.astype(F32), _pad_lanes(b_ml_gates), row(ml_head_norm), B, S)

    b_fx = jnp.pad(b_fx_gate.reshape(1, -1).astype(F32), ((0, 0), (2 * ML_HEADS, LANES - 2 * ML_HEADS - FX_HEADS)))
    c_rows = _fox_gate(small, b_fx, B, S)
    y_fx = _fox_attn(proj, c_rows, B, S)

    kv = _memkv(mem2d, row(norm_mem), w_mem_kv.astype(BF16))
    y_ca = _memattn(proj, kv, B, S, M)

    w_r = jnp.pad(w_router, ((0, 0), (0, LANES - N_EXPERTS))).astype(BF16)
    x2, hp, ri, rw, cnt = _merge(y_ml, y_fx, y_ca, proj, x2d, w_branch.astype(BF16), w_out.astype(BF16),
                                 row(norm_moe), w_r, _pad_lanes(b_router, fill=-1e30))

    tm = EXPERT_TM
    n_tiles = (T * TOP_K) // tm + N_EXPERTS
    counts = cnt[0, :N_EXPERTS].astype(I32)
    padded = ((counts + tm - 1) // tm) * tm
    gend = jnp.cumsum(padded)
    gstart = gend - padded
    expert_ids = jnp.arange(N_EXPERTS, dtype=I32)
    start_of = jnp.sum(jnp.where(ri[:, 0:TOP_K, None] == expert_ids, gstart, 0), axis=-1)
    dest = (start_of + ri[:, TOP_K:2 * TOP_K]).reshape(-1)
    n_valid = gend[-1] // tm
    tile_ids = jnp.arange(n_tiles, dtype=I32)
    last_tile = jnp.minimum(tile_ids, n_valid - 1)
    tile_e = jnp.minimum(jnp.sum((gend[None, :] <= last_tile[:, None] * tm).astype(I32), axis=1), N_EXPERTS - 1)

    slot = jnp.arange(tm, dtype=I32)
    spare = n_tiles * tm + slot % SC_CHUNK
    pad_idx = jnp.where(slot[None, :] < (padded - counts)[:, None], (gstart + counts)[:, None] + slot[None, :],
                        spare[None, :]).reshape(-1)

    xs = _sc_dispatch(hp, dest, pad_idx, n_tiles * tm + SC_CHUNK)
    ys = _experts(tile_e.astype(I32), n_valid.reshape(1).astype(I32), xs, w_exp_in.astype(F32),
                  b_exp_in.reshape(N_EXPERTS, 1, -1).astype(F32), w_exp_out.astype(F32),
                  b_exp_out.reshape(N_EXPERTS, 1, -1).astype(F32))
    yg = _sc_gather(ys, dest.reshape(T, TOP_K).T.reshape(-1))
    return _combine(yg, rw, x2, row(norm_out))


def kernel(x, mem, norm_mix, w_in, b_ml_gates, conv_ml, ml_head_norm, b_fx_gate, norm_mem, w_mem_kv, w_branch,
           w_out, norm_moe, w_router, b_router, w_exp_in, b_exp_in, w_exp_out, b_exp_out, norm_final):
    B, S, D = x.shape
    M = mem.shape[1]
    depth = norm_mix.shape[0]
    assert depth == 1, "the combine kernel fuses the final norm, so exactly one layer is supported"
    assert D == D_MODEL and S % ML_BLOCK == 0 and S % FX_T == 0 and S % CA_TQ == 0
    out = _layer(x.reshape(B * S, D), mem.reshape(B * M, D), B, S, M, norm_mix[0], w_in[0], b_ml_gates[0],
                 conv_ml[0], ml_head_norm[0], b_fx_gate[0], norm_mem[0], w_mem_kv[0], w_branch[0], w_out[0],
                 norm_moe[0], w_router[0], b_router[0], w_exp_in[0], b_exp_in[0], w_exp_out[0], b_exp_out[0],
                 norm_final)
    return out.reshape(B, S, D)
```

```python
import functools

import jax
import jax.numpy as jnp
from jax import lax
from jax.experimental import pallas as pl
from jax.experimental.pallas import tpu as pltpu
from jax.experimental.pallas import tpu_sc as plsc

F32 = jnp.float32
BF16 = jnp.bfloat16
I32 = jnp.int32

D_MODEL = 1024
N_MEM_HEADS = 4
ML_HEADS = 4
ML_DQK = 128
ML_DV = 256
ML_CONV = 4
FX_HEADS = 8
FX_DH = 128
CA_HEADS = 4
CA_DH = 256
N_EXPERTS = 32
TOP_K = 4
D_FF = D_MODEL
SWIGLU_LIMIT = 7.0
SWIGLU_ALPHA = 1.702
EPS = 1e-5
LANES = 128
HALF = D_MODEL // 2
HI_MASK = -65536

COL_MLQK, COL_MLV, COL_MLO, COL_FXQ, COL_FXK, COL_FXV, COL_CAQ, COL_GATE0 = 0, 1, 2, 3, 4, 5, 6, 7
N_BIG = 10 * D_MODEL

VMEM_LIMIT = 56 * 1024 * 1024


def _cparams(sem):
    return pltpu.CompilerParams(dimension_semantics=sem, vmem_limit_bytes=VMEM_LIMIT)


def _rms(x, g):
    return x * lax.rsqrt(jnp.mean(x * x, axis=-1, keepdims=True) + EPS) * g


def _log_sigmoid(x):
    return jnp.minimum(x, 0.0) - jnp.log1p(jnp.exp(-jnp.abs(x)))


def _pack_rows(y):
    bits = lax.bitcast_convert_type(y.astype(BF16).astype(F32), I32)
    return lax.shift_right_logical(bits[:, :HALF], 16) | (bits[:, HALF:] & HI_MASK)


def _unpack_rows(w):
    lo = lax.bitcast_convert_type(lax.shift_left(w, 16), F32)
    hi = lax.bitcast_convert_type(w & HI_MASK, F32)
    return lo, hi


def _inproj_body(x_ref, g_ref, w_ref, ws_ref, o_ref, os_ref, h_ref):
    @pl.when(pl.program_id(1) == 0)
    def _():
        hb = _rms(x_ref[...], g_ref[...]).astype(BF16)
        h_ref[...] = hb
        os_ref[...] = jnp.dot(hb, ws_ref[...], preferred_element_type=F32)

    o_ref[...] = jnp.dot(h_ref[...], w_ref[...], preferred_element_type=F32).astype(BF16)


def _inproj(x2d, g, w_big, w_small):
    T = x2d.shape[0]
    tm = min(1024, T)
    tn = 2048
    return pl.pallas_call(
        _inproj_body,
        grid=(T // tm, N_BIG // tn),
        in_specs=[
            pl.BlockSpec((tm, D_MODEL), lambda i, j: (i, 0)),
            pl.BlockSpec((1, D_MODEL), lambda i, j: (0, 0)),
            pl.BlockSpec((D_MODEL, tn), lambda i, j: (0, j)),
            pl.BlockSpec((D_MODEL, LANES), lambda i, j: (0, 0)),
        ],
        out_specs=[
            pl.BlockSpec((tm, tn), lambda i, j: (i, j)),
            pl.BlockSpec((tm, LANES), lambda i, j: (i, 0)),
        ],
        out_shape=[
            jax.ShapeDtypeStruct((T, N_BIG), BF16),
            jax.ShapeDtypeStruct((T, LANES), F32),
        ],
        scratch_shapes=[pltpu.VMEM((tm, D_MODEL), BF16)],
        compiler_params=_cparams(("parallel", "arbitrary")),
        name="inproj",
    )(x2d, g, w_big, w_small)


ML_BLOCK = 512
ML_MB = 1
ML_CHUNK = 128
CONV_PAD = 8


def _mlstm_body(qk_ref, v_ref, o_ref, g_ref, cw_ref, bg_ref, hn_ref, y_ref, xbuf, c_st, n_st, m_st):
    L = ML_CHUNK

    @pl.when(pl.program_id(1) == 0)
    def _():
        xbuf[:, 0:CONV_PAD, :] = jnp.zeros((ML_MB, CONV_PAD, D_MODEL), F32)
        c_st[...] = jnp.zeros_like(c_st)
        n_st[...] = jnp.zeros_like(n_st)
        m_st[...] = jnp.zeros_like(m_st)

    for bb in range(ML_MB):
        xbuf[bb, CONV_PAD:CONV_PAD + ML_BLOCK, :] = qk_ref[bb].astype(F32)
    cw = cw_ref[...]
    row = lax.broadcasted_iota(I32, (L, L), 0)
    col = lax.broadcasted_iota(I32, (L, L), 1)
    tri = (row >= col).astype(F32)
    causal_t = col >= row
    bg = bg_ref[...]
    scale = ML_DQK ** -0.5
    nt_dims = (((1,), (1,)), ((), ()))

    def chunk(bb, c):
        r0 = c * L
        conv = cw[0:1, :] * xbuf[bb, r0 + CONV_PAD - 3:r0 + CONV_PAD - 3 + L, :]
        for j in range(1, ML_CONV):
            s0 = r0 + CONV_PAD - 3 + j
            conv = conv + cw[j:j + 1, :] * xbuf[bb, s0:s0 + L, :]
        act = conv * jax.nn.sigmoid(conv)

        gates = g_ref[bb, r0:r0 + L, :] + bg
        cum = jnp.dot(tri, _log_sigmoid(gates), precision=lax.Precision.HIGHEST,
                      preferred_element_type=F32)
        gates_t = gates.T
        cum_t = cum.T
        for h in range(ML_HEADS):
            b_row = cum_t[ML_HEADS + h:ML_HEADS + h + 1, :]
            i_row = gates_t[h:h + 1, :]
            a_col = gates[:, h:h + 1] - cum[:, ML_HEADS + h:ML_HEADS + h + 1]
            st = bb * ML_HEADS + h
            m_prev = m_st[st]
            dm = jnp.where(causal_t, a_col + b_row, -jnp.inf)
            m_inter = b_row + m_prev
            m_t = jnp.maximum(jnp.max(dm, axis=0, keepdims=True), m_inter)
            w_intra = jnp.exp(dm - m_t)
            w_inter = jnp.exp(m_inter - m_t)

            qb = (act[:, h * ML_DQK:(h + 1) * ML_DQK] * scale).astype(BF16)
            kb = act[:, (ML_HEADS + h) * ML_DQK:(ML_HEADS + h + 1) * ML_DQK].astype(BF16)
            v_t = v_ref[bb, r0:r0 + L, h * ML_DV:(h + 1) * ML_DV].astype(F32).T
            p_t = lax.dot_general(kb, qb, nt_dims, preferred_element_type=F32) * w_intra
            c_old = c_st[st]
            n_old = n_st[st]
            num = jnp.dot(v_t.astype(BF16), p_t.astype(BF16), preferred_element_type=F32) + w_inter * (
                lax.dot_general(c_old.astype(BF16), qb, nt_dims, preferred_element_type=F32))
            qn = lax.dot_general(jnp.broadcast_to(n_old, (8, ML_DQK)).astype(BF16), qb, nt_dims,
                                 preferred_element_type=F32)[0:1, :]
            den = jnp.sum(p_t, axis=0, keepdims=True) + w_inter * qn
            hv = num / jnp.maximum(jnp.abs(den), jnp.exp(-m_t))

            m_new = m_t[:, L - 1:L]
            b_last = b_row[:, L - 1:L]
            wk = jnp.exp(b_last - b_row + i_row - m_new)
            decay = jnp.exp(b_last + m_prev - m_new)
            c_st[st] = decay * c_old + jnp.dot((v_t * wk).astype(BF16), kb, preferred_element_type=F32)
            n_st[st] = decay * n_old + jnp.dot(jnp.broadcast_to(wk, (8, L)).astype(BF16), kb,
                                               preferred_element_type=F32)[0:1, :]
            m_st[st] = m_new

            hn = (hv * lax.rsqrt(jnp.mean(hv * hv, axis=0, keepdims=True) + EPS)).T
            og = o_ref[bb, r0:r0 + L, h * ML_DV:(h + 1) * ML_DV].astype(F32)
            y_ref[bb, r0:r0 + L, h * ML_DV:(h + 1) * ML_DV] = (
                hn * hn_ref[:, h * ML_DV:(h + 1) * ML_DV] * jax.nn.sigmoid(og)).astype(BF16)

    for c in range(ML_BLOCK // L):
        for bb in range(ML_MB):
            chunk(bb, c)

    xbuf[:, 0:CONV_PAD, :] = xbuf[:, ML_BLOCK:ML_BLOCK + CONV_PAD, :]


def _mlstm(proj, small, conv_w, b_gates, head_norm, B, S):
    T = B * S
    ns = S // ML_BLOCK
    assert B % ML_MB == 0
    proj3 = proj.reshape(B, S, N_BIG)
    blk = lambda col: pl.BlockSpec((ML_MB, ML_BLOCK, D_MODEL), lambda b, s: (b, s, col))
    out = pl.pallas_call(
        _mlstm_body,
        grid=(B // ML_MB, ns),
        in_specs=[
            blk(COL_MLQK),
            blk(COL_MLV),
            blk(COL_MLO),
            pl.BlockSpec((ML_MB, ML_BLOCK, LANES), lambda b, s: (b, s, 0)),
            pl.BlockSpec((ML_CONV, D_MODEL), lambda b, s: (0, 0)),
            pl.BlockSpec((1, LANES), lambda b, s: (0, 0)),
            pl.BlockSpec((1, D_MODEL), lambda b, s: (0, 0)),
        ],
        out_specs=blk(0),
        out_shape=jax.ShapeDtypeStruct((B, S, D_MODEL), BF16),
        scratch_shapes=[
            pltpu.VMEM((ML_MB, ML_BLOCK + CONV_PAD, D_MODEL), F32),
            pltpu.VMEM((ML_MB * ML_HEADS, ML_DV, ML_DQK), F32),
            pltpu.VMEM((ML_MB * ML_HEADS, 1, ML_DQK), F32),
            pltpu.VMEM((ML_MB * ML_HEADS, 1, 1), F32),
        ],
        compiler_params=_cparams(("parallel", "arbitrary")),
        name="mlstm",
    )(proj3, proj3, proj3, small.reshape(B, S, LANES), conv_w, b_gates, head_norm)
    return out.reshape(T, D_MODEL)


FX_T = 512
FX_HP = 2
LOG2E = 1.4426950408889634


def _fox_gate_body(g_ref, b_ref, o_ref):
    S = g_ref.shape[0]
    nk = S // FX_T
    row = lax.broadcasted_iota(I32, (FX_T, FX_T), 0)
    col = lax.broadcasted_iota(I32, (FX_T, FX_T), 1)
    tri = (row >= col).astype(F32)
    carry = jnp.zeros((1, LANES), F32)
    for blk in range(nk):
        lf = _log_sigmoid(g_ref[blk * FX_T:(blk + 1) * FX_T, :] + b_ref[...])
        cum = jnp.dot(tri, lf, precision=lax.Precision.HIGHEST, preferred_element_type=F32) + carry
        carry = cum[FX_T - 1:FX_T, :]
        cum_t = (cum * LOG2E).T
        for h in range(FX_HEADS):
            o_ref[h, blk] = cum_t[2 * ML_HEADS + h:2 * ML_HEADS + h + 1, :]


def _fox_gate(small, b_fx, B, S):
    nk = S // FX_T
    return pl.pallas_call(
        _fox_gate_body,
        grid=(B,),
        in_specs=[
            pl.BlockSpec((S, LANES), lambda b: (b, 0)),
            pl.BlockSpec((1, LANES), lambda b: (0, 0)),
        ],
        out_specs=pl.BlockSpec((None, FX_HEADS, nk, 1, FX_T), lambda b: (b, 0, 0, 0, 0)),
        out_shape=jax.ShapeDtypeStruct((B, FX_HEADS, nk, 1, FX_T), F32),
        compiler_params=_cparams(("parallel",)),
        name="fox_gate",
    )(small, b_fx)


def _fox_attn_body(q_ref, k_ref, v_ref, c_ref, o_ref, m_ref, acc_ref, s_ref):
    i = pl.program_id(2)
    ones_col = (lax.broadcasted_iota(I32, (FX_T, FX_DH), 1) == 0).astype(BF16)
    heads = []
    for hh in range(FX_HP):
        sl = slice(hh * FX_DH, (hh + 1) * FX_DH)
        q = (q_ref[:, sl].astype(F32) * (FX_DH ** -0.5 * LOG2E)).astype(BF16)
        heads.append((hh, sl, q, c_ref[hh, i][:, 0:1]))
    m_ref[...] = jnp.full(m_ref.shape, -jnp.inf, F32)
    acc_ref[...] = jnp.zeros(acc_ref.shape, F32)

    def scores(j, slot):
        r0 = pl.multiple_of(j * FX_T, FX_T)
        for hh, sl, q, c_q in heads:
            s = lax.dot_general(q, k_ref[pl.ds(r0, FX_T), sl], (((1,), (1,)), ((), ())),
                                preferred_element_type=F32)
            s_ref[slot, hh] = s + (c_q - c_ref[hh, j])

    def consume(j, slot, masked):
        r0 = pl.multiple_of(j * FX_T, FX_T)
        for hh, sl, _, _ in heads:
            s = s_ref[slot, hh]
            if masked:
                row = lax.broadcasted_iota(I32, (FX_T, FX_T), 0)
                col = lax.broadcasted_iota(I32, (FX_T, FX_T), 1)
                s = jnp.where(row >= col, s, -jnp.inf)
            m_old = m_ref[hh]
            m_new = jnp.maximum(m_old, jnp.max(s, axis=-1, keepdims=True))
            p = jnp.exp2(s - m_new).astype(BF16)
            v1 = jnp.concatenate([v_ref[pl.ds(r0, FX_T), sl], ones_col], axis=-1)
            acc_ref[hh] = jnp.exp2(m_old - m_new) * acc_ref[hh] + jnp.dot(p, v1, preferred_element_type=F32)
            m_ref[hh] = m_new

    scores(0, 0)

    def pair(jj, carry):
        j = 2 * jj
        scores(j + 1, 1)
        consume(j, 0, False)
        scores(j + 2, 0)
        consume(j + 1, 1, False)
        return carry

    lax.fori_loop(0, i // 2, pair, 0)

    @pl.when(i % 2 == 1)
    def _():
        scores(i, 1)
        consume(i - 1, 0, False)
        consume(i, 1, True)

    @pl.when(i % 2 == 0)
    def _():
        consume(i, 0, True)

    for hh, sl, _, _ in heads:
        acc = acc_ref[hh]
        o_ref[:, sl] = (acc[:, :FX_DH] / acc[:, FX_DH:FX_DH + 1]).astype(BF16)


def _fox_attn(proj, c_rows, B, S):
    T = B * S
    nq = S // FX_T
    wide = FX_HP * FX_DH
    cq = COL_FXQ * (D_MODEL // wide)
    ck = COL_FXK * (D_MODEL // wide)
    cv = COL_FXV * (D_MODEL // wide)
    proj3 = proj.reshape(B, S, N_BIG)
    out = pl.pallas_call(
        _fox_attn_body,
        grid=(B, FX_HEADS // FX_HP, nq),
        in_specs=[
            pl.BlockSpec((None, FX_T, wide), lambda b, h, i: (b, i, cq + h)),
            pl.BlockSpec((None, S, wide), lambda b, h, i: (b, 0, ck + h)),
            pl.BlockSpec((None, S, wide), lambda b, h, i: (b, 0, cv + h)),
            pl.BlockSpec((None, FX_HP, nq, 1, FX_T), lambda b, h, i: (b, h, 0, 0, 0)),
        ],
        out_specs=pl.BlockSpec((None, FX_T, wide), lambda b, h, i: (b, i, h)),
        out_shape=jax.ShapeDtypeStruct((B, S, D_MODEL), BF16),
        scratch_shapes=[
            pltpu.VMEM((FX_HP, FX_T, 1), F32),
            pltpu.VMEM((FX_HP, FX_T, 2 * FX_DH), F32),
            pltpu.VMEM((2, FX_HP, FX_T, FX_T), F32),
        ],
        compiler_params=_cparams(("parallel", "parallel", "arbitrary")),
        name="fox_attn",
    )(proj3, proj3, proj3, c_rows)
    return out.reshape(T, D_MODEL)


def _memkv_body(x_ref, g_ref, w_ref, o_ref):
    hb = _rms(x_ref[...], g_ref[...]).astype(BF16)
    o_ref[...] = jnp.dot(hb, w_ref[...], preferred_element_type=F32).astype(BF16)


def _memkv(mem2d, g, w_kv):
    R = mem2d.shape[0]
    tm = min(512, R)
    N = w_kv.shape[1]
    return pl.pallas_call(
        _memkv_body,
        grid=(R // tm,),
        in_specs=[
            pl.BlockSpec((tm, D_MODEL), lambda i: (i, 0)),
            pl.BlockSpec((1, D_MODEL), lambda i: (0, 0)),
            pl.BlockSpec((D_MODEL, N), lambda i: (0, 0)),
        ],
        out_specs=pl.BlockSpec((tm, N), lambda i: (i, 0)),
        out_shape=jax.ShapeDtypeStruct((R, N), BF16),
        compiler_params=_cparams(("parallel",)),
        name="memkv",
    )(mem2d, g, w_kv)


CA_TQ = 512


def _memattn_body(q_ref, k_ref, v_ref, o_ref):
    scale = CA_DH ** -0.5
    for h in range(CA_HEADS):
        sl = slice(h * CA_DH, (h + 1) * CA_DH)
        s = lax.dot_general(q_ref[:, sl], k_ref[:, sl], (((1,), (1,)), ((), ())),
                            preferred_element_type=F32) * scale
        p = jnp.exp(s - jnp.max(s, axis=-1, keepdims=True))
        l = jnp.sum(p, axis=-1, keepdims=True)
        o = jnp.dot(p.astype(BF16), v_ref[:, sl], preferred_element_type=F32) / l
        o_ref[:, sl] = o.astype(BF16)


def _memattn(proj, kv, B, S, M):
    T = B * S
    nq = S // CA_TQ
    kv3 = kv.reshape(B, M, 2 * D_MODEL)
    return pl.pallas_call(
        _memattn_body,
        grid=(B, nq),
        in_specs=[
            pl.BlockSpec((CA_TQ, D_MODEL), lambda b, i: (b * nq + i, COL_CAQ)),
            pl.BlockSpec((None, M, D_MODEL), lambda b, i: (b, 0, 0)),
            pl.BlockSpec((None, M, D_MODEL), lambda b, i: (b, 0, 1)),
        ],
        out_specs=pl.BlockSpec((CA_TQ, D_MODEL), lambda b, i: (b * nq + i, 0)),
        out_shape=jax.ShapeDtypeStruct((T, D_MODEL), BF16),
        compiler_params=_cparams(("parallel", "arbitrary")),
        name="memattn",
    )(proj, kv3, kv3)


MERGE_TM = 512


def _merge_body(y0_ref, y1_ref, y2_ref, g0_ref, g1_ref, g2_ref, x_ref, wb_ref, wo_ref, gn_ref, wr_ref, br_ref,
                o_ref, hp_ref, ri_ref, rw_ref, cnt_ref, carry_ref):
    merged = None
    for n, (y_ref, g_ref) in enumerate(((y0_ref, g0_ref), (y1_ref, g1_ref), (y2_ref, g2_ref))):
        p = jnp.dot(y_ref[...], wb_ref[n], preferred_element_type=F32)
        t = jax.nn.sigmoid(g_ref[...].astype(F32)) * p
        merged = t if merged is None else merged + t
    x2 = x_ref[...] + jnp.dot(merged.astype(BF16), wo_ref[...], preferred_element_type=F32)
    o_ref[...] = x2
    _route(x2, gn_ref, wr_ref, br_ref, hp_ref, ri_ref, rw_ref, cnt_ref, carry_ref)


def _merge(y_ml, y_fx, y_ca, proj, x2d, w_branch, w_out, g_moe, w_router, b_router):
    T = x2d.shape[0]
    tm = MERGE_TM
    row = lambda i: (i, 0)
    const = lambda i: (0, 0)
    return pl.pallas_call(
        _merge_body,
        grid=(T // tm,),
        in_specs=[
            pl.BlockSpec((tm, D_MODEL), row),
            pl.BlockSpec((tm, D_MODEL), row),
            pl.BlockSpec((tm, D_MODEL), row),
            pl.BlockSpec((tm, D_MODEL), lambda i: (i, COL_GATE0)),
            pl.BlockSpec((tm, D_MODEL), lambda i: (i, COL_GATE0 + 1)),
            pl.BlockSpec((tm, D_MODEL), lambda i: (i, COL_GATE0 + 2)),
            pl.BlockSpec((tm, D_MODEL), row),
            pl.BlockSpec((3, D_MODEL, D_MODEL), lambda i: (0, 0, 0)),
            pl.BlockSpec((D_MODEL, D_MODEL), const),
            pl.BlockSpec((1, D_MODEL), const),
            pl.BlockSpec((D_MODEL, LANES), const),
            pl.BlockSpec((1, LANES), const),
        ],
        out_specs=[
            pl.BlockSpec((tm, D_MODEL), row),
            pl.BlockSpec((tm, HALF), row),
            pl.BlockSpec((tm, LANES), row),
            pl.BlockSpec((tm, LANES), row),
            pl.BlockSpec((1, LANES), const),
        ],
        out_shape=[
            jax.ShapeDtypeStruct((T, D_MODEL), F32),
            jax.ShapeDtypeStruct((T, HALF), I32),
            jax.ShapeDtypeStruct((T, LANES), I32),
            jax.ShapeDtypeStruct((T, LANES), F32),
            jax.ShapeDtypeStruct((1, LANES), F32),
        ],
        scratch_shapes=[pltpu.VMEM((1, LANES), F32)],
        compiler_params=_cparams(("arbitrary",)),
        name="merge_router",
    )(y_ml, y_fx, y_ca, proj, proj, proj, x2d, w_branch, w_out, g_moe, w_router, b_router)


def _route(x2, g_ref, wr_ref, br_ref, hp_ref, ri_ref, rw_ref, cnt_ref, carry_ref):
    tm = MERGE_TM

    @pl.when(pl.program_id(0) == 0)
    def _():
        carry_ref[...] = jnp.zeros_like(carry_ref)

    h = _rms(x2, g_ref[...])
    hp_ref[...] = _pack_rows(h)
    logits = jnp.dot(h.astype(BF16), wr_ref[...], preferred_element_type=F32) + br_ref[...]
    lane = lax.broadcasted_iota(I32, (tm, LANES), 1)
    lane_f = lane.astype(F32)

    work = logits
    onehot_sum = jnp.zeros((tm, LANES), F32)
    vals, sels, idxs = [], [], []
    for _ in range(TOP_K):
        mx = jnp.max(work, axis=-1, keepdims=True)
        idx = jnp.min(jnp.where(work == mx, lane_f, float(LANES)), axis=-1, keepdims=True)
        sel = lane_f == idx
        onehot_sum = onehot_sum + sel.astype(F32)
        work = jnp.where(sel, -jnp.inf, work)
        vals.append(mx)
        sels.append(sel)
        idxs.append(idx)
    exps = [jnp.exp(v - vals[0]) for v in vals]
    total = exps[0] + exps[1] + exps[2] + exps[3]

    row = lax.broadcasted_iota(I32, (tm, tm), 0)
    col = lax.broadcasted_iota(I32, (tm, tm), 1)
    strict = (row > col).astype(BF16)
    before = jnp.dot(strict, onehot_sum.astype(BF16), preferred_element_type=F32) + carry_ref[...]
    carry_ref[...] = carry_ref[...] + jnp.sum(onehot_sum, axis=0, keepdims=True)
    cnt_ref[...] = carry_ref[...]

    ri = jnp.zeros((tm, LANES), I32)
    rw = jnp.zeros((tm, LANES), F32)
    for k in range(TOP_K):
        rank = jnp.sum(jnp.where(sels[k], before, 0.0), axis=-1, keepdims=True)
        ri = jnp.where(lane == k, idxs[k].astype(I32), ri)
        ri = jnp.where(lane == TOP_K + k, rank.astype(I32), ri)
        rw = jnp.where(lane == k, exps[k] / total, rw)
    ri_ref[...] = ri
    rw_ref[...] = rw


EXPERT_TM = 512
SC_CORES = 2
SC_SUBCORES = 16
SC_WORKERS = SC_CORES * SC_SUBCORES
SC_CHUNK = 64
PAD_SLOTS = N_EXPERTS * EXPERT_TM


def _sc_mesh():
    return plsc.VectorSubcoreMesh(core_axis_name="c", subcore_axis_name="s")


def _sc_worker():
    return lax.axis_index("s") * SC_CORES + lax.axis_index("c")


def _sc_dispatch(hp, dest, pad_idx, n_rows):
    T = hp.shape[0]
    per_w = T // SC_WORKERS
    n_ch = per_w // SC_CHUNK
    n_pc = PAD_SLOTS // (SC_WORKERS * SC_CHUNK)
    assert per_w % SC_CHUNK == 0 and n_ch >= 2 and n_ch % 2 == 0
    idx = dest.reshape(SC_WORKERS, n_ch, SC_CHUNK, TOP_K).transpose(0, 1, 3, 2)
    idx = idx.reshape(SC_WORKERS, n_ch * TOP_K, SC_CHUNK)
    pidx = pad_idx.reshape(SC_WORKERS, n_pc, SC_CHUNK)
    zeros = jnp.zeros((SC_CHUNK, HALF), I32)

    @functools.partial(
        pl.kernel, mesh=_sc_mesh(),
        out_type=jax.ShapeDtypeStruct((n_rows, HALF), I32),
        scratch_types=[
            pltpu.VMEM((n_ch * TOP_K, SC_CHUNK), I32),
            pltpu.VMEM((n_pc, SC_CHUNK), I32),
            pltpu.VMEM((2, SC_CHUNK, HALF), I32),
            pltpu.SemaphoreType.DMA((2,)),
            pltpu.SemaphoreType.DMA((2,)),
        ],
        name="sc_dispatch",
    )
    def k(hp_hbm, idx_hbm, pidx_hbm, zeros_hbm, xs_hbm, idx_v, pidx_v, rows_v, lsem, ssem):
        wid = _sc_worker()
        base = wid * per_w
        pltpu.sync_copy(idx_hbm.at[wid], idx_v)
        pltpu.sync_copy(pidx_hbm.at[wid], pidx_v)

        pltpu.sync_copy(zeros_hbm, rows_v.at[0])
        for p in range(n_pc):
            pltpu.make_async_copy(rows_v.at[0], xs_hbm.at[pidx_v.at[p]], ssem.at[0]).start()
        for p in range(n_pc):
            pltpu.make_async_copy(rows_v.at[0], xs_hbm.at[pidx_v.at[p]], ssem.at[0]).wait()

        def load(i, slot):
            return pltpu.make_async_copy(hp_hbm.at[pl.ds(base + i * SC_CHUNK, SC_CHUNK)], rows_v.at[slot],
                                         lsem.at[slot])

        def scatter(i, kk, slot):
            return pltpu.make_async_copy(rows_v.at[slot], xs_hbm.at[idx_v.at[i * TOP_K + kk]], ssem.at[slot])

        load(0, 0).start()

        def body(i2, carry):
            for slot in range(2):
                i = i2 * 2 + slot
                nxt = 1 - slot

                @pl.when(i + 1 < n_ch)
                def _():
                    @pl.when(i >= 1)
                    def _():
                        for kk in range(TOP_K):
                            scatter(i - 1, kk, nxt).wait()
                    load(i + 1, nxt).start()

                load(i, slot).wait()
                for kk in range(TOP_K):
                    scatter(i, kk, slot).start()
            return carry

        lax.fori_loop(0, n_ch // 2, body, 0)
        for kk in range(TOP_K):
            scatter(n_ch - 2, kk, 0).wait()
            scatter(n_ch - 1, kk, 1).wait()

    return k(hp, idx, pidx, zeros)


def _sc_gather(table, idx):
    n = idx.shape[0]
    per_w = n // SC_WORKERS
    n_ch = per_w // SC_CHUNK
    assert per_w % SC_CHUNK == 0 and n_ch >= 2 and n_ch % 2 == 0

    @functools.partial(
        pl.kernel, mesh=_sc_mesh(),
        out_type=jax.ShapeDtypeStruct((n, HALF), I32),
        scratch_types=[
            pltpu.VMEM((n_ch, SC_CHUNK), I32),
            pltpu.VMEM((2, SC_CHUNK, HALF), I32),
            pltpu.SemaphoreType.DMA((2,)),
            pltpu.SemaphoreType.DMA((2,)),
        ],
        name="sc_gather",
    )
    def k(table_hbm, idx_hbm, out_hbm, idx_v, rows_v, gsem, wsem):
        wid = _sc_worker()
        base = wid * per_w
        pltpu.sync_copy(idx_hbm.at[wid], idx_v)

        def gather(i, slot):
            return pltpu.make_async_copy(table_hbm.at[idx_v.at[i]], rows_v.at[slot], gsem.at[slot])

        def writeback(i, slot):
            return pltpu.make_async_copy(rows_v.at[slot], out_hbm.at[pl.ds(base + i * SC_CHUNK, SC_CHUNK)],
                                         wsem.at[slot])

        gather(0, 0).start()

        def body(i2, carry):
            for slot in range(2):
                i = i2 * 2 + slot
                nxt = 1 - slot

                @pl.when(i + 1 < n_ch)
                def _():
                    @pl.when(i >= 1)
                    def _():
                        writeback(i - 1, nxt).wait()
                    gather(i + 1, nxt).start()

                gather(i, slot).wait()
                writeback(i, slot).start()
            return carry

        lax.fori_loop(0, n_ch // 2, body, 0)
        writeback(n_ch - 2, 0).wait()
        writeback(n_ch - 1, 1).wait()

    return k(table, idx.reshape(SC_WORKERS, n_ch, SC_CHUNK))


FF_CHUNK = 512


def _expert_body(te_ref, nv_ref, x_ref, w1f_ref, b1_ref, w2f_ref, b2_ref, y_ref, w1_ref, w2_ref):
    i = pl.program_id(0)

    @pl.when(jnp.logical_or(i == 0, te_ref[i] != te_ref[jnp.maximum(i - 1, 0)]))
    def _():
        w1_ref[...] = w1f_ref[...].astype(BF16)
        w2_ref[...] = w2f_ref[...].astype(BF16)

    @pl.when(i < nv_ref[0])
    def _():
        lo, hi = _unpack_rows(x_ref[...])
        xlo = lo.astype(BF16)
        xhi = hi.astype(BF16)
        acc = jnp.zeros((EXPERT_TM, D_MODEL), F32) + b2_ref[...]
        for c in range(D_FF // FF_CHUNK):
            def up(off):
                cs = slice(off + c * FF_CHUNK, off + (c + 1) * FF_CHUNK)
                return (jnp.dot(xlo, w1_ref[0:HALF, cs], preferred_element_type=F32)
                        + jnp.dot(xhi, w1_ref[HALF:D_MODEL, cs], preferred_element_type=F32)
                        + b1_ref[:, cs])
            g = jnp.minimum(up(0), SWIGLU_LIMIT)
            lin = jnp.clip(up(D_FF), -SWIGLU_LIMIT, SWIGLU_LIMIT)
            a = g * jax.nn.sigmoid(SWIGLU_ALPHA * g) * (lin + 1.0)
            acc = acc + jnp.dot(a.astype(BF16), w2_ref[c * FF_CHUNK:(c + 1) * FF_CHUNK, :],
                                preferred_element_type=F32)
        y_ref[...] = _pack_rows(acc)


def _experts(tile_expert, n_valid, xs, w1, b1, w2, b2):
    n_rows = xs.shape[0]
    tm = EXPERT_TM
    n_tiles = n_rows // tm
    row = lambda i, te, nv: (jnp.minimum(i, nv[0] - 1), 0)
    grid_spec = pltpu.PrefetchScalarGridSpec(
        num_scalar_prefetch=2,
        grid=(n_tiles,),
        in_specs=[
            pl.BlockSpec((tm, HALF), row),
            pl.BlockSpec((None, D_MODEL, 2 * D_FF), lambda i, te, nv: (te[i], 0, 0)),
            pl.BlockSpec((None, 1, 2 * D_FF), lambda i, te, nv: (te[i], 0, 0)),
            pl.BlockSpec((None, D_FF, D_MODEL), lambda i, te, nv: (te[i], 0, 0)),
            pl.BlockSpec((None, 1, D_MODEL), lambda i, te, nv: (te[i], 0, 0)),
        ],
        out_specs=pl.BlockSpec((tm, HALF), row),
        scratch_shapes=[pltpu.VMEM((D_MODEL, 2 * D_FF), BF16), pltpu.VMEM((D_FF, D_MODEL), BF16)],
    )
    return pl.pallas_call(
        _expert_body,
        grid_spec=grid_spec,
        out_shape=jax.ShapeDtypeStruct((n_rows, HALF), I32),
        compiler_params=_cparams(("arbitrary",)),
        name="experts",
    )(tile_expert, n_valid, xs, w1, b1, w2, b2)


COMBINE_TM = 512


def _combine_body(y0_ref, y1_ref, y2_ref, y3_ref, rw_ref, x_ref, g_ref, o_ref):
    acc = x_ref[...]
    rw = rw_ref[...]
    for k, y_ref in enumerate((y0_ref, y1_ref, y2_ref, y3_ref)):
        lo, hi = _unpack_rows(y_ref[...])
        acc = acc + rw[:, k:k + 1] * jnp.concatenate([lo, hi], axis=-1)
    o_ref[...] = _rms(acc, g_ref[...])


def _combine(yg, rw, x2, g):
    T = x2.shape[0]
    tm = COMBINE_TM
    nt = T // tm
    return pl.pallas_call(
        _combine_body,
        grid=(nt,),
        in_specs=[
            pl.BlockSpec((tm, HALF), lambda i: (i, 0)),
            pl.BlockSpec((tm, HALF), lambda i: (nt + i, 0)),
            pl.BlockSpec((tm, HALF), lambda i: (2 * nt + i, 0)),
            pl.BlockSpec((tm, HALF), lambda i: (3 * nt + i, 0)),
            pl.BlockSpec((tm, LANES), lambda i: (i, 0)),
            pl.BlockSpec((tm, D_MODEL), lambda i: (i, 0)),
            pl.BlockSpec((1, D_MODEL), lambda i: (0, 0)),
        ],
        out_specs=pl.BlockSpec((tm, D_MODEL), lambda i: (i, 0)),
        out_shape=jax.ShapeDtypeStruct((T, D_MODEL), F32),
        compiler_params=_cparams(("parallel",)),
        name="combine",
    )(yg, yg, yg, yg, rw, x2, g)


def _pad_lanes(v, fill=0.0):
    v = v.reshape(1, -1).astype(F32)
    return jnp.pad(v, ((0, 0), (0, LANES - v.shape[1])), constant_values=fill)


def _layer(x2d, mem2d, B, S, M, norm_mix, w_in, b_ml_gates, conv_ml, ml_head_norm, b_fx_gate, norm_mem,
           w_mem_kv, w_branch, w_out, norm_moe, w_router, b_router, w_exp_in, b_exp_in, w_exp_out,
           b_exp_out, norm_out):
    T = B * S
    w_big = jnp.concatenate([w_in[:, 0:2048], w_in[:, 2056:3080], w_in[:, 3080:6152], w_in[:, 6160:7184],
                             w_in[:, 7184:10256]], axis=1).astype(BF16)
    w_small = jnp.concatenate([w_in[:, 2048:2056], w_in[:, 6152:6160]], axis=1)
    w_small = jnp.pad(w_small, ((0, 0), (0, LANES - w_small.shape[1]))).astype(BF16)
    row = lambda v: v.reshape(1, -1).astype(F32)

    proj, small = _inproj(x2d, row(norm_mix), w_big, w_small)

    y_ml = _mlstm(proj, small, conv_ml.astype(F32), _pad_lanes(b_ml_gates), row(ml_head_norm), B, S)

    b_fx = jnp.pad(b_fx_gate.reshape(1, -1).astype(F32), ((0, 0), (2 * ML_HEADS, LANES - 2 * ML_HEADS - FX_HEADS)))
    c_rows = _fox_gate(small, b_fx, B, S)
    y_fx = _fox_attn(proj, c_rows, B, S)

    kv = _memkv(mem2d, row(norm_mem), w_mem_kv.astype(BF16))
    y_ca = _memattn(proj, kv, B, S, M)

    w_r = jnp.pad(w_router, ((0, 0), (0, LANES - N_EXPERTS))).astype(BF16)
    x2, hp, ri, rw, cnt = _merge(y_ml, y_fx, y_ca, proj, x2d, w_branch.astype(BF16), w_out.astype(BF16),
                                 row(norm_moe), w_r, _pad_lanes(b_router, fill=-1e30))

    tm = EXPERT_TM
    n_tiles = (T * TOP_K) // tm + N_EXPERTS
    counts = cnt[0, :N_EXPERTS].astype(I32)
    padded = ((counts + tm - 1) // tm) * tm
    gend = jnp.cumsum(padded)
    gstart = gend - padded
    expert_ids = jnp.arange(N_EXPERTS, dtype=I32)
    start_of = jnp.sum(jnp.where(ri[:, 0:TOP_K, None] == expert_ids, gstart, 0), axis=-1)
    dest = (start_of + ri[:, TOP_K:2 * TOP_K]).reshape(-1)
    n_valid = gend[-1] // tm
    tile_ids = jnp.arange(n_tiles, dtype=I32)
    last_tile = jnp.minimum(tile_ids, n_valid - 1)
    tile_e = jnp.minimum(jnp.sum((gend[None, :] <= last_tile[:, None] * tm).astype(I32), axis=1), N_EXPERTS - 1)

    slot = jnp.arange(tm, dtype=I32)
    spare = n_tiles * tm + slot % SC_CHUNK
    pad_idx = jnp.where(slot[None, :] < (padded - counts)[:, None], (gstart + counts)[:, None] + slot[None, :],
                        spare[None, :]).reshape(-1)

    xs = _sc_dispatch(hp, dest, pad_idx, n_tiles * tm + SC_CHUNK)
    ys = _experts(tile_e.astype(I32), n_valid.reshape(1).astype(I32), xs, w_exp_in.astype(F32),
                  b_exp_in.reshape(N_EXPERTS, 1, -1).astype(F32), w_exp_out.astype(F32),
                  b_exp_out.reshape(N_EXPERTS, 1, -1).astype(F32))
    yg = _sc_gather(ys, dest.reshape(T, TOP_K).T.reshape(-1))
    return _combine(yg, rw, x2, row(norm_out))


def kernel(x, mem, norm_mix, w_in, b_ml_gates, conv_ml, ml_head_norm, b_fx_gate, norm_mem, w_mem_kv, w_branch,
           w_out, norm_moe, w_router, b_router, w_exp_in, b_exp_in, w_exp_out, b_exp_out, norm_final):
    B, S, D = x.shape
    M = mem.shape[1]
    depth = norm_mix.shape[0]
    assert depth == 1, "the combine kernel fuses the final norm, so exactly one layer is supported"
    assert D == D_MODEL and S % ML_BLOCK == 0 and S % FX_T == 0 and S % CA_TQ == 0
    out = _layer(x.reshape(B * S, D), mem.reshape(B * M, D), B, S, M, norm_mix[0], w_in[0], b_ml_gates[0],
                 conv_ml[0], ml_head_norm[0], b_fx_gate[0], norm_mem[0], w_mem_kv[0], w_branch[0], w_out[0],
                 norm_moe[0], w_router[0], b_router[0], w_exp_in[0], b_exp_in[0], w_exp_out[0], b_exp_out[0],
                 norm_final)
    return out.reshape(B, S, D)
```

```python
import functools

import jax
import jax.numpy as jnp
from jax import lax
from jax.experimental import pallas as pl
from jax.experimental.pallas import tpu as pltpu
from jax.experimental.pallas import tpu_sc as plsc

F32 = jnp.float32
BF16 = jnp.bfloat16
I32 = jnp.int32

D_MODEL = 1024
N_MEM_HEADS = 4
ML_HEADS = 4
ML_DQK = 128
ML_DV = 256
ML_CONV = 4
FX_HEADS = 8
FX_DH = 128
CA_HEADS = 4
CA_DH = 256
N_EXPERTS = 32
TOP_K = 4
D_FF = D_MODEL
SWIGLU_LIMIT = 7.0
SWIGLU_ALPHA = 1.702
EPS = 1e-5
LANES = 128
HALF = D_MODEL // 2
HI_MASK = -65536

COL_MLQK, COL_MLV, COL_MLO, COL_FXQ, COL_FXK, COL_FXV, COL_CAQ, COL_GATE0 = 0, 1, 2, 3, 4, 5, 6, 7
N_BIG = 10 * D_MODEL

VMEM_LIMIT = 56 * 1024 * 1024


def _cparams(sem):
    return pltpu.CompilerParams(dimension_semantics=sem, vmem_limit_bytes=VMEM_LIMIT)


def _rms(x, g):
    return x * lax.rsqrt(jnp.mean(x * x, axis=-1, keepdims=True) + EPS) * g


def _log_sigmoid(x):
    return jnp.minimum(x, 0.0) - jnp.log1p(jnp.exp(-jnp.abs(x)))


def _pack_rows(y):
    bits = lax.bitcast_convert_type(y.astype(BF16).astype(F32), I32)
    return lax.shift_right_logical(bits[:, :HALF], 16) | (bits[:, HALF:] & HI_MASK)


def _unpack_rows(w):
    lo = lax.bitcast_convert_type(lax.shift_left(w, 16), F32)
    hi = lax.bitcast_convert_type(w & HI_MASK, F32)
    return lo, hi


def _inproj_body(x_ref, g_ref, w_ref, ws_ref, o_ref, os_ref, h_ref):
    @pl.when(pl.program_id(1) == 0)
    def _():
        hb = _rms(x_ref[...], g_ref[...]).astype(BF16)
        h_ref[...] = hb
        os_ref[...] = jnp.dot(hb, ws_ref[...], preferred_element_type=F32)

    o_ref[...] = jnp.dot(h_ref[...], w_ref[...], preferred_element_type=F32).astype(BF16)


def _inproj(x2d, g, w_big, w_small):
    T = x2d.shape[0]
    tm = min(1024, T)
    tn = 2048
    return pl.pallas_call(
        _inproj_body,
        grid=(T // tm, N_BIG // tn),
        in_specs=[
            pl.BlockSpec((tm, D_MODEL), lambda i, j: (i, 0)),
            pl.BlockSpec((1, D_MODEL), lambda i, j: (0, 0)),
            pl.BlockSpec((D_MODEL, tn), lambda i, j: (0, j)),
            pl.BlockSpec((D_MODEL, LANES), lambda i, j: (0, 0)),
        ],
        out_specs=[
            pl.BlockSpec((tm, tn), lambda i, j: (i, j)),
            pl.BlockSpec((tm, LANES), lambda i, j: (i, 0)),
        ],
        out_shape=[
            jax.ShapeDtypeStruct((T, N_BIG), BF16),
            jax.ShapeDtypeStruct((T, LANES), F32),
        ],
        scratch_shapes=[pltpu.VMEM((tm, D_MODEL), BF16)],
        compiler_params=_cparams(("parallel", "arbitrary")),
        name="inproj",
    )(x2d, g, w_big, w_small)


ML_BLOCK = 512
ML_MB = 1
ML_CHUNK = 128
CONV_PAD = 8


def _mlstm_body(qk_ref, v_ref, o_ref, g_ref, cw_ref, bg_ref, hn_ref, y_ref, xbuf, c_st, n_st, m_st):
    L = ML_CHUNK

    @pl.when(pl.program_id(1) == 0)
    def _():
        xbuf[:, 0:CONV_PAD, :] = jnp.zeros((ML_MB, CONV_PAD, D_MODEL), F32)
        c_st[...] = jnp.zeros_like(c_st)
        n_st[...] = jnp.zeros_like(n_st)
        m_st[...] = jnp.zeros_like(m_st)

    for bb in range(ML_MB):
        xbuf[bb, CONV_PAD:CONV_PAD + ML_BLOCK, :] = qk_ref[bb].astype(F32)
    cw = cw_ref[...]
    row = lax.broadcasted_iota(I32, (L, L), 0)
    col = lax.broadcasted_iota(I32, (L, L), 1)
    tri = (row >= col).astype(F32)
    causal_t = col >= row
    bg = bg_ref[...]
    scale = ML_DQK ** -0.5
    nt_dims = (((1,), (1,)), ((), ()))

    def chunk(bb, c):
        r0 = c * L
        conv = cw[0:1, :] * xbuf[bb, r0 + CONV_PAD - 3:r0 + CONV_PAD - 3 + L, :]
        for j in range(1, ML_CONV):
            s0 = r0 + CONV_PAD - 3 + j
            conv = conv + cw[j:j + 1, :] * xbuf[bb, s0:s0 + L, :]
        act = conv * jax.nn.sigmoid(conv)

        gates = g_ref[bb, r0:r0 + L, :] + bg
        cum = jnp.dot(tri, _log_sigmoid(gates), precision=lax.Precision.HIGHEST,
                      preferred_element_type=F32)
        gates_t = gates.T
        cum_t = cum.T
        for h in range(ML_HEADS):
            b_row = cum_t[ML_HEADS + h:ML_HEADS + h + 1, :]
            i_row = gates_t[h:h + 1, :]
            a_col = gates[:, h:h + 1] - cum[:, ML_HEADS + h:ML_HEADS + h + 1]
            st = bb * ML_HEADS + h
            m_prev = m_st[st]
            dm = jnp.where(causal_t, a_col + b_row, -jnp.inf)
            m_inter = b_row + m_prev
            m_t = jnp.maximum(jnp.max(dm, axis=0, keepdims=True), m_inter)
            w_intra = jnp.exp(dm - m_t)
            w_inter = jnp.exp(m_inter - m_t)

            qb = (act[:, h * ML_DQK:(h + 1) * ML_DQK] * scale).astype(BF16)
            kb = act[:, (ML_HEADS + h) * ML_DQK:(ML_HEADS + h + 1) * ML_DQK].astype(BF16)
            v_t = v_ref[bb, r0:r0 + L, h * ML_DV:(h + 1) * ML_DV].astype(F32).T
            p_t = lax.dot_general(kb, qb, nt_dims, preferred_element_type=F32) * w_intra
            c_old = c_st[st]
            n_old = n_st[st]
            num = jnp.dot(v_t.astype(BF16), p_t.astype(BF16), preferred_element_type=F32) + w_inter * (
                lax.dot_general(c_old.astype(BF16), qb, nt_dims, preferred_element_type=F32))
            qn = lax.dot_general(jnp.broadcast_to(n_old, (8, ML_DQK)).astype(BF16), qb, nt_dims,
                                 preferred_element_type=F32)[0:1, :]
            den = jnp.sum(p_t, axis=0, keepdims=True) + w_inter * qn
            hv = num / jnp.maximum(jnp.abs(den), jnp.exp(-m_t))

            m_new = m_t[:, L - 1:L]
            b_last = b_row[:, L - 1:L]
            wk = jnp.exp(b_last - b_row + i_row - m_new)
            decay = jnp.exp(b_last + m_prev - m_new)
            c_st[st] = decay * c_old + jnp.dot((v_t * wk).astype(BF16), kb, preferred_element_type=F32)
            n_st[st] = decay * n_old + jnp.dot(jnp.broadcast_to(wk, (8, L)).astype(BF16), kb,
                                               preferred_element_type=F32)[0:1, :]
            m_st[st] = m_new

            hn = (hv * lax.rsqrt(jnp.mean(hv * hv, axis=0, keepdims=True) + EPS)).T
            og = o_ref[bb, r0:r0 + L, h * ML_DV:(h + 1) * ML_DV].astype(F32)
            y_ref[bb, r0:r0 + L, h * ML_DV:(h + 1) * ML_DV] = (
                hn * hn_ref[:, h * ML_DV:(h + 1) * ML_DV] * jax.nn.sigmoid(og)).astype(BF16)

    for c in range(ML_BLOCK // L):
        for bb in range(ML_MB):
            chunk(bb, c)

    xbuf[:, 0:CONV_PAD, :] = xbuf[:, ML_BLOCK:ML_BLOCK + CONV_PAD, :]


def _mlstm(proj, small, conv_w, b_gates, head_norm, B, S):
    T = B * S
    ns = S // ML_BLOCK
    assert B % ML_MB == 0
    proj3 = proj.reshape(B, S, N_BIG)
    blk = lambda col: pl.BlockSpec((ML_MB, ML_BLOCK, D_MODEL), lambda b, s: (b, s, col))
    out = pl.pallas_call(
        _mlstm_body,
        grid=(B // ML_MB, ns),
        in_specs=[
            blk(COL_MLQK),
            blk(COL_MLV),
            blk(COL_MLO),
            pl.BlockSpec((ML_MB, ML_BLOCK, LANES), lambda b, s: (b, s, 0)),
            pl.BlockSpec((ML_CONV, D_MODEL), lambda b, s: (0, 0)),
            pl.BlockSpec((1, LANES), lambda b, s: (0, 0)),
            pl.BlockSpec((1, D_MODEL), lambda b, s: (0, 0)),
        ],
        out_specs=blk(0),
        out_shape=jax.ShapeDtypeStruct((B, S, D_MODEL), BF16),
        scratch_shapes=[
            pltpu.VMEM((ML_MB, ML_BLOCK + CONV_PAD, D_MODEL), F32),
            pltpu.VMEM((ML_MB * ML_HEADS, ML_DV, ML_DQK), F32),
            pltpu.VMEM((ML_MB * ML_HEADS, 1, ML_DQK), F32),
            pltpu.VMEM((ML_MB * ML_HEADS, 1, 1), F32),
        ],
        compiler_params=_cparams(("parallel", "arbitrary")),
        name="mlstm",
    )(proj3, proj3, proj3, small.reshape(B, S, LANES), conv_w, b_gates, head_norm)
    return out.reshape(T, D_MODEL)


FX_T = 512
FX_HP = 2
LOG2E = 1.4426950408889634


def _fox_gate_body(g_ref, b_ref, o_ref):
    S = g_ref.shape[0]
    nk = S // FX_T
    row = lax.broadcasted_iota(I32, (FX_T, FX_T), 0)
    col = lax.broadcasted_iota(I32, (FX_T, FX_T), 1)
    tri = (row >= col).astype(F32)
    carry = jnp.zeros((1, LANES), F32)
    for blk in range(nk):
        lf = _log_sigmoid(g_ref[blk * FX_T:(blk + 1) * FX_T, :] + b_ref[...])
        cum = jnp.dot(tri, lf, precision=lax.Precision.HIGHEST, preferred_element_type=F32) + carry
        carry = cum[FX_T - 1:FX_T, :]
        cum_t = (cum * LOG2E).T
        for h in range(FX_HEADS):
            o_ref[h, blk] = cum_t[2 * ML_HEADS + h:2 * ML_HEADS + h + 1, :]


def _fox_gate(small, b_fx, B, S):
    nk = S // FX_T
    return pl.pallas_call(
        _fox_gate_body,
        grid=(B,),
        in_specs=[
            pl.BlockSpec((S, LANES), lambda b: (b, 0)),
            pl.BlockSpec((1, LANES), lambda b: (0, 0)),
        ],
        out_specs=pl.BlockSpec((None, FX_HEADS, nk, 1, FX_T), lambda b: (b, 0, 0, 0, 0)),
        out_shape=jax.ShapeDtypeStruct((B, FX_HEADS, nk, 1, FX_T), F32),
        compiler_params=_cparams(("parallel",)),
        name="fox_gate",
    )(small, b_fx)


def _fox_attn_body(q_ref, k_ref, v_ref, c_ref, o_ref, m_ref, acc_ref, s_ref):
    i = pl.program_id(2)
    ones_col = (lax.broadcasted_iota(I32, (FX_T, FX_DH), 1) == 0).astype(BF16)
    heads = []
    for hh in range(FX_HP):
        sl = slice(hh * FX_DH, (hh + 1) * FX_DH)
        q = (q_ref[:, sl].astype(F32) * (FX_DH ** -0.5 * LOG2E)).astype(BF16)
        heads.append((hh, sl, q, c_ref[hh, i][:, 0:1]))
    m_ref[...] = jnp.full(m_ref.shape, -jnp.inf, F32)
    acc_ref[...] = jnp.zeros(acc_ref.shape, F32)

    def scores(j, slot):
        r0 = pl.multiple_of(j * FX_T, FX_T)
        for hh, sl, q, c_q in heads:
            s = lax.dot_general(q, k_ref[pl.ds(r0, FX_T), sl], (((1,), (1,)), ((), ())),
                                preferred_element_type=F32)
            s_ref[slot, hh] = s + (c_q - c_ref[hh, j])

    def consume(j, slot, masked):
        r0 = pl.multiple_of(j * FX_T, FX_T)
        for hh, sl, _, _ in heads:
            s = s_ref[slot, hh]
            if masked:
                row = lax.broadcasted_iota(I32, (FX_T, FX_T), 0)
                col = lax.broadcasted_iota(I32, (FX_T, FX_T), 1)
                s = jnp.where(row >= col, s, -jnp.inf)
            m_old = m_ref[hh]
            m_new = jnp.maximum(m_old, jnp.max(s, axis=-1, keepdims=True))
            p = jnp.exp2(s - m_new).astype(BF16)
            v1 = jnp.concatenate([v_ref[pl.ds(r0, FX_T), sl], ones_col], axis=-1)
            acc_ref[hh] = jnp.exp2(m_old - m_new) * acc_ref[hh] + jnp.dot(p, v1, preferred_element_type=F32)
            m_ref[hh] = m_new

    scores(0, 0)

    def pair(jj, carry):
        j = 2 * jj
        scores(j + 1, 1)
        consume(j, 0, False)
        scores(j + 2, 0)
        consume(j + 1, 1, False)
        return carry

    lax.fori_loop(0, i // 2, pair, 0)

    @pl.when(i % 2 == 1)
    def _():
        scores(i, 1)
        consume(i - 1, 0, False)
        consume(i, 1, True)

    @pl.when(i % 2 == 0)
    def _():
        consume(i, 0, True)

    for hh, sl, _, _ in heads:
        acc = acc_ref[hh]
        o_ref[:, sl] = (acc[:, :FX_DH] / acc[:, FX_DH:FX_DH + 1]).astype(BF16)


def _fox_attn(proj, c_rows, B, S):
    T = B * S
    nq = S // FX_T
    wide = FX_HP * FX_DH
    cq = COL_FXQ * (D_MODEL // wide)
    ck = COL_FXK * (D_MODEL // wide)
    cv = COL_FXV * (D_MODEL // wide)
    proj3 = proj.reshape(B, S, N_BIG)
    out = pl.pallas_call(
        _fox_attn_body,
        grid=(B, FX_HEADS // FX_HP, nq),
        in_specs=[
            pl.BlockSpec((None, FX_T, wide), lambda b, h, i: (b, i, cq + h)),
            pl.BlockSpec((None, S, wide), lambda b, h, i: (b, 0, ck + h)),
            pl.BlockSpec((None, S, wide), lambda b, h, i: (b, 0, cv + h)),
            pl.BlockSpec((None, FX_HP, nq, 1, FX_T), lambda b, h, i: (b, h, 0, 0, 0)),
        ],
        out_specs=pl.BlockSpec((None, FX_T, wide), lambda b, h, i: (b, i, h)),
        out_shape=jax.ShapeDtypeStruct((B, S, D_MODEL), BF16),
        scratch_shapes=[
            pltpu.VMEM((FX_HP, FX_T, 1), F32),
            pltpu.VMEM((FX_HP, FX_T, 2 * FX_DH), F32),
            pltpu.VMEM((2, FX_HP, FX_T, FX_T), F32),
        ],
        compiler_params=_cparams(("parallel", "parallel", "arbitrary")),
        name="fox_attn",
    )(proj3, proj3, proj3, c_rows)
    return out.reshape(T, D_MODEL)


def _memkv_body(x_ref, g_ref, w_ref, o_ref):
    hb = _rms(x_ref[...], g_ref[...]).astype(BF16)
    o_ref[...] = jnp.dot(hb, w_ref[...], preferred_element_type=F32).astype(BF16)


def _memkv(mem2d, g, w_kv):
    R = mem2d.shape[0]
    tm = min(512, R)
    N = w_kv.shape[1]
    return pl.pallas_call(
        _memkv_body,
        grid=(R // tm,),
        in_specs=[
            pl.BlockSpec((tm, D_MODEL), lambda i: (i, 0)),
            pl.BlockSpec((1, D_MODEL), lambda i: (0, 0)),
            pl.BlockSpec((D_MODEL, N), lambda i: (0, 0)),
        ],
        out_specs=pl.BlockSpec((tm, N), lambda i: (i, 0)),
        out_shape=jax.ShapeDtypeStruct((R, N), BF16),
        compiler_params=_cparams(("parallel",)),
        name="memkv",
    )(mem2d, g, w_kv)


CA_TQ = 512


def _memattn_body(q_ref, k_ref, v_ref, o_ref):
    scale = CA_DH ** -0.5
    for h in range(CA_HEADS):
        sl = slice(h * CA_DH, (h + 1) * CA_DH)
        s = lax.dot_general(q_ref[:, sl], k_ref[:, sl], (((1,), (1,)), ((), ())),
                            preferred_element_type=F32) * scale
        p = jnp.exp(s - jnp.max(s, axis=-1, keepdims=True))
        l = jnp.sum(p, axis=-1, keepdims=True)
        o = jnp.dot(p.astype(BF16), v_ref[:, sl], preferred_element_type=F32) / l
        o_ref[:, sl] = o.astype(BF16)


def _memattn(proj, kv, B, S, M):
    T = B * S
    nq = S // CA_TQ
    kv3 = kv.reshape(B, M, 2 * D_MODEL)
    return pl.pallas_call(
        _memattn_body,
        grid=(B, nq),
        in_specs=[
            pl.BlockSpec((CA_TQ, D_MODEL), lambda b, i: (b * nq + i, COL_CAQ)),
            pl.BlockSpec((None, M, D_MODEL), lambda b, i: (b, 0, 0)),
            pl.BlockSpec((None, M, D_MODEL), lambda b, i: (b, 0, 1)),
        ],
        out_specs=pl.BlockSpec((CA_TQ, D_MODEL), lambda b, i: (b * nq + i, 0)),
        out_shape=jax.ShapeDtypeStruct((T, D_MODEL), BF16),
        compiler_params=_cparams(("parallel", "arbitrary")),
        name="memattn",
    )(proj, kv3, kv3)


MERGE_TM = 512
MOE_PARTS = 2


def _merge_body(y0_ref, y1_ref, y2_ref, g0_ref, g1_ref, g2_ref, x_ref, wb_ref, wo_ref, gn_ref, wr_ref, br_ref,
                o_ref, hp_ref, ri_ref, rw_ref, cnt_ref, carry_ref):
    merged = None
    for n, (y_ref, g_ref) in enumerate(((y0_ref, g0_ref), (y1_ref, g1_ref), (y2_ref, g2_ref))):
        p = jnp.dot(y_ref[...], wb_ref[n], preferred_element_type=F32)
        t = jax.nn.sigmoid(g_ref[...].astype(F32)) * p
        merged = t if merged is None else merged + t
    x2 = x_ref[...] + jnp.dot(merged.astype(BF16), wo_ref[...], preferred_element_type=F32)
    o_ref[...] = x2
    _route(x2, gn_ref, wr_ref, br_ref, hp_ref, ri_ref, rw_ref, cnt_ref, carry_ref)


def _merge(y_ml, y_fx, y_ca, proj, x2d, w_branch, w_out, g_moe, w_router, b_router, part):
    T = x2d.shape[0] // MOE_PARTS
    tm = MERGE_TM
    off = part * (T // tm)
    src = lambda i: (off + i, 0)
    row = lambda i: (i, 0)
    const = lambda i: (0, 0)
    return pl.pallas_call(
        _merge_body,
        grid=(T // tm,),
        in_specs=[
            pl.BlockSpec((tm, D_MODEL), src),
            pl.BlockSpec((tm, D_MODEL), src),
            pl.BlockSpec((tm, D_MODEL), src),
            pl.BlockSpec((tm, D_MODEL), lambda i: (off + i, COL_GATE0)),
            pl.BlockSpec((tm, D_MODEL), lambda i: (off + i, COL_GATE0 + 1)),
            pl.BlockSpec((tm, D_MODEL), lambda i: (off + i, COL_GATE0 + 2)),
            pl.BlockSpec((tm, D_MODEL), src),
            pl.BlockSpec((3, D_MODEL, D_MODEL), lambda i: (0, 0, 0)),
            pl.BlockSpec((D_MODEL, D_MODEL), const),
            pl.BlockSpec((1, D_MODEL), const),
            pl.BlockSpec((D_MODEL, LANES), const),
            pl.BlockSpec((1, LANES), const),
        ],
        out_specs=[
            pl.BlockSpec((tm, D_MODEL), row),
            pl.BlockSpec((tm, HALF), row),
            pl.BlockSpec((tm, LANES), row),
            pl.BlockSpec((tm, LANES), row),
            pl.BlockSpec((1, LANES), const),
        ],
        out_shape=[
            jax.ShapeDtypeStruct((T, D_MODEL), F32),
            jax.ShapeDtypeStruct((T, HALF), I32),
            jax.ShapeDtypeStruct((T, LANES), I32),
            jax.ShapeDtypeStruct((T, LANES), F32),
            jax.ShapeDtypeStruct((1, LANES), F32),
        ],
        scratch_shapes=[pltpu.VMEM((1, LANES), F32)],
        compiler_params=_cparams(("arbitrary",)),
        name="merge_router",
    )(y_ml, y_fx, y_ca, proj, proj, proj, x2d, w_branch, w_out, g_moe, w_router, b_router)


def _route(x2, g_ref, wr_ref, br_ref, hp_ref, ri_ref, rw_ref, cnt_ref, carry_ref):
    tm = MERGE_TM

    @pl.when(pl.program_id(0) == 0)
    def _():
        carry_ref[...] = jnp.zeros_like(carry_ref)

    h = _rms(x2, g_ref[...])
    hp_ref[...] = _pack_rows(h)
    logits = jnp.dot(h.astype(BF16), wr_ref[...], preferred_element_type=F32) + br_ref[...]
    lane = lax.broadcasted_iota(I32, (tm, LANES), 1)
    lane_f = lane.astype(F32)

    work = logits
    onehot_sum = jnp.zeros((tm, LANES), F32)
    vals, sels, idxs = [], [], []
    for _ in range(TOP_K):
        mx = jnp.max(work, axis=-1, keepdims=True)
        idx = jnp.min(jnp.where(work == mx, lane_f, float(LANES)), axis=-1, keepdims=True)
        sel = lane_f == idx
        onehot_sum = onehot_sum + sel.astype(F32)
        work = jnp.where(sel, -jnp.inf, work)
        vals.append(mx)
        sels.append(sel)
        idxs.append(idx)
    exps = [jnp.exp(v - vals[0]) for v in vals]
    total = exps[0] + exps[1] + exps[2] + exps[3]

    row = lax.broadcasted_iota(I32, (tm, tm), 0)
    col = lax.broadcasted_iota(I32, (tm, tm), 1)
    strict = (row > col).astype(BF16)
    before = jnp.dot(strict, onehot_sum.astype(BF16), preferred_element_type=F32) + carry_ref[...]
    carry_ref[...] = carry_ref[...] + jnp.sum(onehot_sum, axis=0, keepdims=True)
    cnt_ref[...] = carry_ref[...]

    ri = jnp.zeros((tm, LANES), I32)
    rw = jnp.zeros((tm, LANES), F32)
    for k in range(TOP_K):
        rank = jnp.sum(jnp.where(sels[k], before, 0.0), axis=-1, keepdims=True)
        ri = jnp.where(lane == k, idxs[k].astype(I32), ri)
        ri = jnp.where(lane == TOP_K + k, rank.astype(I32), ri)
        rw = jnp.where(lane == k, exps[k] / total, rw)
    ri_ref[...] = ri
    rw_ref[...] = rw


EXPERT_TM = 512
SC_CORES = 2
SC_SUBCORES = 16
SC_WORKERS = SC_CORES * SC_SUBCORES
SC_CHUNK = 64
PAD_SLOTS = N_EXPERTS * EXPERT_TM


def _sc_mesh():
    return plsc.VectorSubcoreMesh(core_axis_name="c", subcore_axis_name="s")


def _sc_worker():
    return lax.axis_index("s") * SC_CORES + lax.axis_index("c")


def _sc_dispatch(hp, dest, pad_idx, n_rows):
    T = hp.shape[0]
    per_w = T // SC_WORKERS
    n_ch = per_w // SC_CHUNK
    n_pc = PAD_SLOTS // (SC_WORKERS * SC_CHUNK)
    assert per_w % SC_CHUNK == 0 and n_ch >= 2 and n_ch % 2 == 0
    idx = dest.reshape(SC_WORKERS, n_ch, SC_CHUNK, TOP_K).transpose(0, 1, 3, 2)
    idx = idx.reshape(SC_WORKERS, n_ch * TOP_K, SC_CHUNK)
    pidx = pad_idx.reshape(SC_WORKERS, n_pc, SC_CHUNK)
    zeros = jnp.zeros((SC_CHUNK, HALF), I32)

    @functools.partial(
        pl.kernel, mesh=_sc_mesh(),
        out_type=jax.ShapeDtypeStruct((n_rows, HALF), I32),
        scratch_types=[
            pltpu.VMEM((n_ch * TOP_K, SC_CHUNK), I32),
            pltpu.VMEM((n_pc, SC_CHUNK), I32),
            pltpu.VMEM((2, SC_CHUNK, HALF), I32),
            pltpu.SemaphoreType.DMA((2,)),
            pltpu.SemaphoreType.DMA((2,)),
        ],
        name="sc_dispatch",
    )
    def k(hp_hbm, idx_hbm, pidx_hbm, zeros_hbm, xs_hbm, idx_v, pidx_v, rows_v, lsem, ssem):
        wid = _sc_worker()
        base = wid * per_w
        pltpu.sync_copy(idx_hbm.at[wid], idx_v)
        pltpu.sync_copy(pidx_hbm.at[wid], pidx_v)

        pltpu.sync_copy(zeros_hbm, rows_v.at[0])
        for p in range(n_pc):
            pltpu.make_async_copy(rows_v.at[0], xs_hbm.at[pidx_v.at[p]], ssem.at[0]).start()
        for p in range(n_pc):
            pltpu.make_async_copy(rows_v.at[0], xs_hbm.at[pidx_v.at[p]], ssem.at[0]).wait()

        def load(i, slot):
            return pltpu.make_async_copy(hp_hbm.at[pl.ds(base + i * SC_CHUNK, SC_CHUNK)], rows_v.at[slot],
                                         lsem.at[slot])

        def scatter(i, kk, slot):
            return pltpu.make_async_copy(rows_v.at[slot], xs_hbm.at[idx_v.at[i * TOP_K + kk]], ssem.at[slot])

        load(0, 0).start()

        def body(i2, carry):
            for slot in range(2):
                i = i2 * 2 + slot
                nxt = 1 - slot

                @pl.when(i + 1 < n_ch)
                def _():
                    @pl.when(i >= 1)
                    def _():
                        for kk in range(TOP_K):
                            scatter(i - 1, kk, nxt).wait()
                    load(i + 1, nxt).start()

                load(i, slot).wait()
                for kk in range(TOP_K):
                    scatter(i, kk, slot).start()
            return carry

        lax.fori_loop(0, n_ch // 2, body, 0)
        for kk in range(TOP_K):
            scatter(n_ch - 2, kk, 0).wait()
            scatter(n_ch - 1, kk, 1).wait()

    return k(hp, idx, pidx, zeros)


def _sc_gather(table, idx):
    n = idx.shape[0]
    per_w = n // SC_WORKERS
    n_ch = per_w // SC_CHUNK
    assert per_w % SC_CHUNK == 0 and n_ch >= 2 and n_ch % 2 == 0

    @functools.partial(
        pl.kernel, mesh=_sc_mesh(),
        out_type=jax.ShapeDtypeStruct((n, HALF), I32),
        scratch_types=[
            pltpu.VMEM((n_ch, SC_CHUNK), I32),
            pltpu.VMEM((2, SC_CHUNK, HALF), I32),
            pltpu.SemaphoreType.DMA((2,)),
            pltpu.SemaphoreType.DMA((2,)),
        ],
        name="sc_gather",
    )
    def k(table_hbm, idx_hbm, out_hbm, idx_v, rows_v, gsem, wsem):
        wid = _sc_worker()
        base = wid * per_w
        pltpu.sync_copy(idx_hbm.at[wid], idx_v)

        def gather(i, slot):
            return pltpu.make_async_copy(table_hbm.at[idx_v.at[i]], rows_v.at[slot], gsem.at[slot])

        def writeback(i, slot):
            return pltpu.make_async_copy(rows_v.at[slot], out_hbm.at[pl.ds(base + i * SC_CHUNK, SC_CHUNK)],
                                         wsem.at[slot])

        gather(0, 0).start()

        def body(i2, carry):
            for slot in range(2):
                i = i2 * 2 + slot
                nxt = 1 - slot

                @pl.when(i + 1 < n_ch)
                def _():
                    @pl.when(i >= 1)
                    def _():
                        writeback(i - 1, nxt).wait()
                    gather(i + 1, nxt).start()

                gather(i, slot).wait()
                writeback(i, slot).start()
            return carry

        lax.fori_loop(0, n_ch // 2, body, 0)
        writeback(n_ch - 2, 0).wait()
        writeback(n_ch - 1, 1).wait()

    return k(table, idx.reshape(SC_WORKERS, n_ch, SC_CHUNK))


FF_CHUNK = 512


def _expert_body(te_ref, nv_ref, x_ref, w1f_ref, b1_ref, w2f_ref, b2_ref, y_ref, w1_ref, w2_ref):
    i = pl.program_id(0)

    @pl.when(jnp.logical_or(i == 0, te_ref[i] != te_ref[jnp.maximum(i - 1, 0)]))
    def _():
        w1_ref[...] = w1f_ref[...].astype(BF16)
        w2_ref[...] = w2f_ref[...].astype(BF16)

    @pl.when(i < nv_ref[0])
    def _():
        lo, hi = _unpack_rows(x_ref[...])
        xlo = lo.astype(BF16)
        xhi = hi.astype(BF16)
        acc = jnp.zeros((EXPERT_TM, D_MODEL), F32) + b2_ref[...]
        for c in range(D_FF // FF_CHUNK):
            def up(off):
                cs = slice(off + c * FF_CHUNK, off + (c + 1) * FF_CHUNK)
                return (jnp.dot(xlo, w1_ref[0:HALF, cs], preferred_element_type=F32)
                        + jnp.dot(xhi, w1_ref[HALF:D_MODEL, cs], preferred_element_type=F32)
                        + b1_ref[:, cs])
            g = jnp.minimum(up(0), SWIGLU_LIMIT)
            lin = jnp.clip(up(D_FF), -SWIGLU_LIMIT, SWIGLU_LIMIT)
            a = g * jax.nn.sigmoid(SWIGLU_ALPHA * g) * (lin + 1.0)
            acc = acc + jnp.dot(a.astype(BF16), w2_ref[c * FF_CHUNK:(c + 1) * FF_CHUNK, :],
                                preferred_element_type=F32)
        y_ref[...] = _pack_rows(acc)


def _experts(tile_expert, n_valid, xs, w1, b1, w2, b2):
    n_rows = xs.shape[0]
    tm = EXPERT_TM
    n_tiles = n_rows // tm
    row = lambda i, te, nv: (jnp.minimum(i, nv[0] - 1), 0)
    grid_spec = pltpu.PrefetchScalarGridSpec(
        num_scalar_prefetch=2,
        grid=(n_tiles,),
        in_specs=[
            pl.BlockSpec((tm, HALF), row),
            pl.BlockSpec((None, D_MODEL, 2 * D_FF), lambda i, te, nv: (te[i], 0, 0)),
            pl.BlockSpec((None, 1, 2 * D_FF), lambda i, te, nv: (te[i], 0, 0)),
            pl.BlockSpec((None, D_FF, D_MODEL), lambda i, te, nv: (te[i], 0, 0)),
            pl.BlockSpec((None, 1, D_MODEL), lambda i, te, nv: (te[i], 0, 0)),
        ],
        out_specs=pl.BlockSpec((tm, HALF), row),
        scratch_shapes=[pltpu.VMEM((D_MODEL, 2 * D_FF), BF16), pltpu.VMEM((D_FF, D_MODEL), BF16)],
    )
    return pl.pallas_call(
        _expert_body,
        grid_spec=grid_spec,
        out_shape=jax.ShapeDtypeStruct((n_rows, HALF), I32),
        compiler_params=_cparams(("arbitrary",)),
        name="experts",
    )(tile_expert, n_valid, xs, w1, b1, w2, b2)


COMBINE_TM = 512


def _combine_body(y0_ref, y1_ref, y2_ref, y3_ref, rw_ref, x_ref, g_ref, *rest):
    o_ref = rest[-1]
    acc = x_ref[...]
    rw = rw_ref[...]
    for k, y_ref in enumerate((y0_ref, y1_ref, y2_ref, y3_ref)):
        lo, hi = _unpack_rows(y_ref[...])
        acc = acc + rw[:, k:k + 1] * jnp.concatenate([lo, hi], axis=-1)
    o_ref[...] = _rms(acc, g_ref[...])


def _combine(yg, rw, x2, g, part, out_prev):
    T = x2.shape[0]
    tm = COMBINE_TM
    nt = T // tm
    in_specs = [
        pl.BlockSpec((tm, HALF), lambda i: (i, 0)),
        pl.BlockSpec((tm, HALF), lambda i: (nt + i, 0)),
        pl.BlockSpec((tm, HALF), lambda i: (2 * nt + i, 0)),
        pl.BlockSpec((tm, HALF), lambda i: (3 * nt + i, 0)),
        pl.BlockSpec((tm, LANES), lambda i: (i, 0)),
        pl.BlockSpec((tm, D_MODEL), lambda i: (i, 0)),
        pl.BlockSpec((1, D_MODEL), lambda i: (0, 0)),
    ]
    args = [yg, yg, yg, yg, rw, x2, g]
    aliases = {}
    if out_prev is not None:
        in_specs.append(pl.BlockSpec(memory_space=pl.ANY))
        args.append(out_prev)
        aliases = {len(args) - 1: 0}
    return pl.pallas_call(
        _combine_body,
        grid=(nt,),
        in_specs=in_specs,
        out_specs=pl.BlockSpec((tm, D_MODEL), lambda i: (part * nt + i, 0)),
        out_shape=jax.ShapeDtypeStruct((T * MOE_PARTS, D_MODEL), F32),
        input_output_aliases=aliases,
        compiler_params=_cparams(("parallel",)),
        name="combine",
    )(*args)


def _pad_lanes(v, fill=0.0):
    v = v.reshape(1, -1).astype(F32)
    return jnp.pad(v, ((0, 0), (0, LANES - v.shape[1])), constant_values=fill)


def _layer(x2d, mem2d, B, S, M, norm_mix, w_in, b_ml_gates, conv_ml, ml_head_norm, b_fx_gate, norm_mem,
           w_mem_kv, w_branch, w_out, norm_moe, w_router, b_router, w_exp_in, b_exp_in, w_exp_out,
           b_exp_out, norm_out):
    T = B * S
    w_big = jnp.concatenate([w_in[:, 0:2048], w_in[:, 2056:3080], w_in[:, 3080:6152], w_in[:, 6160:7184],
                             w_in[:, 7184:10256]], axis=1).astype(BF16)
    w_small = jnp.concatenate([w_in[:, 2048:2056], w_in[:, 6152:6160]], axis=1)
    w_small = jnp.pad(w_small, ((0, 0), (0, LANES - w_small.shape[1]))).astype(BF16)
    row = lambda v: v.reshape(1, -1).astype(F32)

    proj, small = _inproj(x2d, row(norm_mix), w_big, w_small)

    y_ml = _mlstm(proj, small, conv_ml.astype(F32), _pad_lanes(b_ml_gates), row(ml_head_norm), B, S)

    b_fx = jnp.pad(b_fx_gate.reshape(1, -1).astype(F32), ((0, 0), (2 * ML_HEADS, LANES - 2 * ML_HEADS - FX_HEADS)))
    c_rows = _fox_gate(small, b_fx, B, S)
    y_fx = _fox_attn(proj, c_rows, B, S)

    kv = _memkv(mem2d, row(norm_mem), w_mem_kv.astype(BF16))
    y_ca = _memattn(proj, kv, B, S, M)

    w_r = jnp.pad(w_router, ((0, 0), (0, LANES - N_EXPERTS))).astype(BF16)
    moe_weights = (w_exp_in.astype(F32), b_exp_in.reshape(N_EXPERTS, 1, -1).astype(F32), w_exp_out.astype(F32),
                   b_exp_out.reshape(N_EXPERTS, 1, -1).astype(F32))
    routed = [_merge(y_ml, y_fx, y_ca, proj, x2d, w_branch.astype(BF16), w_out.astype(BF16), row(norm_moe), w_r,
                     _pad_lanes(b_router, fill=-1e30), part) for part in range(MOE_PARTS)]
    out = None
    for part, (x2, hp, ri, rw, cnt) in enumerate(routed):
        yg = _moe_rows(hp, ri, cnt, moe_weights)
        out = _combine(yg, rw, x2, row(norm_out), part, out)
    return out


def _moe_rows(hp, ri, cnt, moe_weights):
    T = hp.shape[0]
    tm = EXPERT_TM
    n_tiles = (T * TOP_K) // tm + N_EXPERTS
    counts = cnt[0, :N_EXPERTS].astype(I32)
    padded = ((counts + tm - 1) // tm) * tm
    gend = jnp.cumsum(padded)
    gstart = gend - padded
    expert_ids = jnp.arange(N_EXPERTS, dtype=I32)
    start_of = jnp.sum(jnp.where(ri[:, 0:TOP_K, None] == expert_ids, gstart, 0), axis=-1)
    dest = (start_of + ri[:, TOP_K:2 * TOP_K]).reshape(-1)
    n_valid = gend[-1] // tm
    tile_ids = jnp.arange(n_tiles, dtype=I32)
    last_tile = jnp.minimum(tile_ids, n_valid - 1)
    tile_e = jnp.minimum(jnp.sum((gend[None, :] <= last_tile[:, None] * tm).astype(I32), axis=1), N_EXPERTS - 1)

    slot = jnp.arange(tm, dtype=I32)
    spare = n_tiles * tm + slot % SC_CHUNK
    pad_idx = jnp.where(slot[None, :] < (padded - counts)[:, None], (gstart + counts)[:, None] + slot[None, :],
                        spare[None, :]).reshape(-1)

    xs = _sc_dispatch(hp, dest, pad_idx, n_tiles * tm + SC_CHUNK)
    ys = _experts(tile_e.astype(I32), n_valid.reshape(1).astype(I32), xs, *moe_weights)
    return _sc_gather(ys, dest.reshape(T, TOP_K).T.reshape(-1))


def kernel(x, mem, norm_mix, w_in, b_ml_gates, conv_ml, ml_head_norm, b_fx_gate, norm_mem, w_mem_kv, w_branch,
           w_out, norm_moe, w_router, b_router, w_exp_in, b_exp_in, w_exp_out, b_exp_out, norm_final):
    B, S, D = x.shape
    M = mem.shape[1]
    depth = norm_mix.shape[0]
    assert depth == 1, "the combine kernel fuses the final norm, so exactly one layer is supported"
    assert D == D_MODEL and S % ML_BLOCK == 0 and S % FX_T == 0 and S % CA_TQ == 0
    out = _layer(x.reshape(B * S, D), mem.reshape(B * M, D), B, S, M, norm_mix[0], w_in[0], b_ml_gates[0],
                 conv_ml[0], ml_head_norm[0], b_fx_gate[0], norm_mem[0], w_mem_kv[0], w_branch[0], w_out[0],
                 norm_moe[0], w_router[0], b_router[0], w_exp_in[0], b_exp_in[0], w_exp_out[0], b_exp_out[0],
                 norm_final)
    return out.reshape(B, S, D)
```

```python
import functools

import jax
import jax.numpy as jnp
from jax import lax
from jax.experimental import pallas as pl
from jax.experimental.pallas import tpu as pltpu
from jax.experimental.pallas import tpu_sc as plsc

F32 = jnp.float32
BF16 = jnp.bfloat16
I32 = jnp.int32

D_MODEL = 1024
N_MEM_HEADS = 4
ML_HEADS = 4
ML_DQK = 128
ML_DV = 256
ML_CONV = 4
FX_HEADS = 8
FX_DH = 128
CA_HEADS = 4
CA_DH = 256
N_EXPERTS = 32
TOP_K = 4
D_FF = D_MODEL
SWIGLU_LIMIT = 7.0
SWIGLU_ALPHA = 1.702
EPS = 1e-5
LANES = 128
HALF = D_MODEL // 2
HI_MASK = -65536

COL_MLQK, COL_MLV, COL_MLO, COL_FXQ, COL_FXK, COL_FXV, COL_CAQ, COL_GATE0 = 0, 1, 2, 3, 4, 5, 6, 7
N_BIG = 10 * D_MODEL

VMEM_LIMIT = 56 * 1024 * 1024


def _cparams(sem):
    return pltpu.CompilerParams(dimension_semantics=sem, vmem_limit_bytes=VMEM_LIMIT)


def _rms(x, g):
    return x * lax.rsqrt(jnp.mean(x * x, axis=-1, keepdims=True) + EPS) * g


def _log_sigmoid(x):
    return jnp.minimum(x, 0.0) - jnp.log1p(jnp.exp(-jnp.abs(x)))


def _pack_rows(y):
    bits = lax.bitcast_convert_type(y.astype(BF16).astype(F32), I32)
    return lax.shift_right_logical(bits[:, :HALF], 16) | (bits[:, HALF:] & HI_MASK)


def _unpack_rows(w):
    lo = lax.bitcast_convert_type(lax.shift_left(w, 16), F32)
    hi = lax.bitcast_convert_type(w & HI_MASK, F32)
    return lo, hi


def _inproj_body(x_ref, g_ref, w_ref, ws_ref, o_ref, os_ref, h_ref):
    @pl.when(pl.program_id(1) == 0)
    def _():
        hb = _rms(x_ref[...], g_ref[...]).astype(BF16)
        h_ref[...] = hb
        os_ref[...] = jnp.dot(hb, ws_ref[...], preferred_element_type=F32)

    o_ref[...] = jnp.dot(h_ref[...], w_ref[...], preferred_element_type=F32).astype(BF16)


def _inproj(x2d, g, w_big, w_small):
    T = x2d.shape[0]
    tm = min(1024, T)
    tn = 2048
    return pl.pallas_call(
        _inproj_body,
        grid=(T // tm, N_BIG // tn),
        in_specs=[
            pl.BlockSpec((tm, D_MODEL), lambda i, j: (i, 0)),
            pl.BlockSpec((1, D_MODEL), lambda i, j: (0, 0)),
            pl.BlockSpec((D_MODEL, tn), lambda i, j: (0, j)),
            pl.BlockSpec((D_MODEL, LANES), lambda i, j: (0, 0)),
        ],
        out_specs=[
            pl.BlockSpec((tm, tn), lambda i, j: (i, j)),
            pl.BlockSpec((tm, LANES), lambda i, j: (i, 0)),
        ],
        out_shape=[
            jax.ShapeDtypeStruct((T, N_BIG), BF16),
            jax.ShapeDtypeStruct((T, LANES), F32),
        ],
        scratch_shapes=[pltpu.VMEM((tm, D_MODEL), BF16)],
        compiler_params=_cparams(("parallel", "arbitrary")),
        name="inproj",
    )(x2d, g, w_big, w_small)


ML_BLOCK = 512
ML_MB = 1
ML_CHUNK = 128
CONV_PAD = 8


def _mlstm_body(qk_ref, v_ref, o_ref, g_ref, cw_ref, bg_ref, hn_ref, y_ref, xbuf, c_st, n_st, m_st):
    L = ML_CHUNK

    @pl.when(pl.program_id(1) == 0)
    def _():
        xbuf[:, 0:CONV_PAD, :] = jnp.zeros((ML_MB, CONV_PAD, D_MODEL), F32)
        c_st[...] = jnp.zeros_like(c_st)
        n_st[...] = jnp.zeros_like(n_st)
        m_st[...] = jnp.zeros_like(m_st)

    for bb in range(ML_MB):
        xbuf[bb, CONV_PAD:CONV_PAD + ML_BLOCK, :] = qk_ref[bb].astype(F32)
    cw = cw_ref[...]
    row = lax.broadcasted_iota(I32, (L, L), 0)
    col = lax.broadcasted_iota(I32, (L, L), 1)
    tri = (row >= col).astype(F32)
    causal_t = col >= row
    bg = bg_ref[...]
    scale = ML_DQK ** -0.5
    nt_dims = (((1,), (1,)), ((), ()))

    def chunk(bb, c):
        r0 = c * L
        conv = cw[0:1, :] * xbuf[bb, r0 + CONV_PAD - 3:r0 + CONV_PAD - 3 + L, :]
        for j in range(1, ML_CONV):
            s0 = r0 + CONV_PAD - 3 + j
            conv = conv + cw[j:j + 1, :] * xbuf[bb, s0:s0 + L, :]
        act = conv * jax.nn.sigmoid(conv)

        gates = g_ref[bb, r0:r0 + L, :] + bg
        cum = jnp.dot(tri, _log_sigmoid(gates), precision=lax.Precision.HIGHEST,
                      preferred_element_type=F32)
        gates_t = gates.T
        cum_t = cum.T
        for h in range(ML_HEADS):
            b_row = cum_t[ML_HEADS + h:ML_HEADS + h + 1, :]
            i_row = gates_t[h:h + 1, :]
            a_col = gates[:, h:h + 1] - cum[:, ML_HEADS + h:ML_HEADS + h + 1]
            st = bb * ML_HEADS + h
            m_prev = m_st[st]
            dm = jnp.where(causal_t, a_col + b_row, -jnp.inf)
            m_inter = b_row + m_prev
            m_t = jnp.maximum(jnp.max(dm, axis=0, keepdims=True), m_inter)
            w_intra = jnp.exp(dm - m_t)
            w_inter = jnp.exp(m_inter - m_t)

            qb = (act[:, h * ML_DQK:(h + 1) * ML_DQK] * scale).astype(BF16)
            kb = act[:, (ML_HEADS + h) * ML_DQK:(ML_HEADS + h + 1) * ML_DQK].astype(BF16)
            v_t = v_ref[bb, r0:r0 + L, h * ML_DV:(h + 1) * ML_DV].astype(F32).T
            p_t = lax.dot_general(kb, qb, nt_dims, preferred_element_type=F32) * w_intra
            c_old = c_st[st]
            n_old = n_st[st]
            num = jnp.dot(v_t.astype(BF16), p_t.astype(BF16), preferred_element_type=F32) + w_inter * (
                lax.dot_general(c_old.astype(BF16), qb, nt_dims, preferred_element_type=F32))
            qn = lax.dot_general(jnp.broadcast_to(n_old, (8, ML_DQK)).astype(BF16), qb, nt_dims,
                                 preferred_element_type=F32)[0:1, :]
            den = jnp.sum(p_t, axis=0, keepdims=True) + w_inter * qn
            hv = num / jnp.maximum(jnp.abs(den), jnp.exp(-m_t))

            m_new = m_t[:, L - 1:L]
            b_last = b_row[:, L - 1:L]
            wk = jnp.exp(b_last - b_row + i_row - m_new)
            decay = jnp.exp(b_last + m_prev - m_new)
            c_st[st] = decay * c_old + jnp.dot((v_t * wk).astype(BF16), kb, preferred_element_type=F32)
            n_st[st] = decay * n_old + jnp.dot(jnp.broadcast_to(wk, (8, L)).astype(BF16), kb,
                                               preferred_element_type=F32)[0:1, :]
            m_st[st] = m_new

            hn = (hv * lax.rsqrt(jnp.mean(hv * hv, axis=0, keepdims=True) + EPS)).T
            og = o_ref[bb, r0:r0 + L, h * ML_DV:(h + 1) * ML_DV].astype(F32)
            y_ref[bb, r0:r0 + L, h * ML_DV:(h + 1) * ML_DV] = (
                hn * hn_ref[:, h * ML_DV:(h + 1) * ML_DV] * jax.nn.sigmoid(og)).astype(BF16)

    for c in range(ML_BLOCK // L):
        for bb in range(ML_MB):
            chunk(bb, c)

    xbuf[:, 0:CONV_PAD, :] = xbuf[:, ML_BLOCK:ML_BLOCK + CONV_PAD, :]


def _mlstm(proj, small, conv_w, b_gates, head_norm, B, S):
    T = B * S
    ns = S // ML_BLOCK
    assert B % ML_MB == 0
    proj3 = proj.reshape(B, S, N_BIG)
    blk = lambda col: pl.BlockSpec((ML_MB, ML_BLOCK, D_MODEL), lambda b, s: (b, s, col))
    out = pl.pallas_call(
        _mlstm_body,
        grid=(B // ML_MB, ns),
        in_specs=[
            blk(COL_MLQK),
            blk(COL_MLV),
            blk(COL_MLO),
            pl.BlockSpec((ML_MB, ML_BLOCK, LANES), lambda b, s: (b, s, 0)),
            pl.BlockSpec((ML_CONV, D_MODEL), lambda b, s: (0, 0)),
            pl.BlockSpec((1, LANES), lambda b, s: (0, 0)),
            pl.BlockSpec((1, D_MODEL), lambda b, s: (0, 0)),
        ],
        out_specs=blk(0),
        out_shape=jax.ShapeDtypeStruct((B, S, D_MODEL), BF16),
        scratch_shapes=[
            pltpu.VMEM((ML_MB, ML_BLOCK + CONV_PAD, D_MODEL), F32),
            pltpu.VMEM((ML_MB * ML_HEADS, ML_DV, ML_DQK), F32),
            pltpu.VMEM((ML_MB * ML_HEADS, 1, ML_DQK), F32),
            pltpu.VMEM((ML_MB * ML_HEADS, 1, 1), F32),
        ],
        compiler_params=_cparams(("parallel", "arbitrary")),
        name="mlstm",
    )(proj3, proj3, proj3, small.reshape(B, S, LANES), conv_w, b_gates, head_norm)
    return out.reshape(T, D_MODEL)


FX_T = 512
FX_HP = 2
FX_VR = FX_DH + 16
LOG2E = 1.4426950408889634
N_PIECES = 3


def _fox_gate_body(g_ref, b_ref, o_ref):
    S = g_ref.shape[0]
    row = lax.broadcasted_iota(I32, (FX_T, FX_T), 0)
    col = lax.broadcasted_iota(I32, (FX_T, FX_T), 1)
    tri = (row >= col).astype(F32)
    carry = jnp.zeros((1, LANES), F32)
    for blk in range(S // FX_T):
        rows = slice(blk * FX_T, (blk + 1) * FX_T)
        lf = _log_sigmoid(g_ref[rows, :] + b_ref[...])
        cum = jnp.dot(tri, lf, precision=lax.Precision.HIGHEST, preferred_element_type=F32) + carry
        carry = cum[FX_T - 1:FX_T, :]
        o_ref[rows, :] = cum * (-LOG2E)


def _fox_gate(small, b_fx, B, S):
    return pl.pallas_call(
        _fox_gate_body,
        grid=(B,),
        in_specs=[
            pl.BlockSpec((S, LANES), lambda b: (b, 0)),
            pl.BlockSpec((1, LANES), lambda b: (0, 0)),
        ],
        out_specs=pl.BlockSpec((S, LANES), lambda b: (b, 0)),
        out_shape=jax.ShapeDtypeStruct((B * S, LANES), F32),
        compiler_params=_cparams(("parallel",)),
        name="fox_gate",
    )(small, b_fx)


def _fox_attn_body(q_ref, k_ref, v_ref, c_ref, o_ref, kx_ref, vt_ref, m_ref, acc_ref, s_ref):
    i = pl.program_id(2)
    S = k_ref.shape[0]
    nk = S // FX_T

    @pl.when(i == 0)
    def _():
        c = c_ref[...]
        hi = c.astype(BF16)
        r1 = c - hi.astype(F32)
        mid = r1.astype(BF16)
        lo = (r1 - mid.astype(F32)).astype(BF16)
        sel_row = lax.broadcasted_iota(I32, (LANES, LANES), 0)
        sel_col = lax.broadcasted_iota(I32, (LANES, LANES), 1)
        ones_rows = (lax.broadcasted_iota(I32, (FX_VR - FX_DH, FX_T), 0) == 0).astype(BF16)
        for hh in range(FX_HP):
            sl = slice(hh * FX_DH, (hh + 1) * FX_DH)
            lane = 2 * ML_HEADS + pl.program_id(1) * FX_HP + hh
            pieces = None
            for p, part in enumerate((hi, mid, lo)):
                pick = jnp.logical_and(sel_row == lane, sel_col == p).astype(BF16)
                t = jnp.dot(part, pick, preferred_element_type=F32)
                pieces = t if pieces is None else pieces + t
            kx_ref[hh, :, 0:FX_DH] = k_ref[:, sl]
            kx_ref[hh, :, FX_DH:2 * FX_DH] = pieces.astype(BF16)
            for j in range(nk):
                vt = v_ref[j * FX_T:(j + 1) * FX_T, sl].astype(F32).T.astype(BF16)
                vt_ref[hh, j] = jnp.concatenate([vt, ones_rows], axis=0)

    piece_rows = lax.broadcasted_iota(I32, (FX_DH, FX_T), 0) < N_PIECES
    heads = []
    for hh in range(FX_HP):
        sl = slice(hh * FX_DH, (hh + 1) * FX_DH)
        q_t = (q_ref[:, sl].astype(F32) * (FX_DH ** -0.5 * LOG2E)).T.astype(BF16)
        heads.append((hh, sl, jnp.concatenate([q_t, piece_rows.astype(BF16)], axis=0)))
    m_ref[...] = jnp.full(m_ref.shape, -jnp.inf, F32)
    acc_ref[...] = jnp.zeros(acc_ref.shape, F32)

    def scores(j, slot):
        r0 = pl.multiple_of(j * FX_T, FX_T)
        for hh, _, q_x in heads:
            s_ref[slot, hh] = jnp.dot(kx_ref[hh, pl.ds(r0, FX_T), :], q_x, preferred_element_type=F32)

    def consume(j, slot, masked):
        for hh, _, _ in heads:
            s = s_ref[slot, hh]
            if masked:
                key = lax.broadcasted_iota(I32, (FX_T, FX_T), 0)
                qry = lax.broadcasted_iota(I32, (FX_T, FX_T), 1)
                s = jnp.where(qry >= key, s, -jnp.inf)
            m_old = m_ref[hh]
            m_new = jnp.maximum(m_old, jnp.max(s, axis=0, keepdims=True))
            p = jnp.exp2(s - m_new).astype(BF16)
            acc_ref[hh] = jnp.exp2(m_old - m_new) * acc_ref[hh] + jnp.dot(vt_ref[hh, j], p,
                                                                         preferred_element_type=F32)
            m_ref[hh] = m_new

    scores(0, 0)

    def pair(jj, carry):
        j = 2 * jj
        scores(j + 1, 1)
        consume(j, 0, False)
        scores(j + 2, 0)
        consume(j + 1, 1, False)
        return carry

    lax.fori_loop(0, i // 2, pair, 0)

    @pl.when(i % 2 == 1)
    def _():
        scores(i, 1)
        consume(i - 1, 0, False)
        consume(i, 1, True)

    @pl.when(i % 2 == 0)
    def _():
        consume(i, 0, True)

    for hh, sl, _ in heads:
        acc = acc_ref[hh]
        o_ref[:, sl] = (acc[0:FX_DH, :] / acc[FX_DH:FX_DH + 1, :]).T.astype(BF16)


def _fox_attn(proj, c_neg, B, S):
    T = B * S
    nq = S // FX_T
    wide = FX_HP * FX_DH
    cq = COL_FXQ * (D_MODEL // wide)
    ck = COL_FXK * (D_MODEL // wide)
    cv = COL_FXV * (D_MODEL // wide)
    proj3 = proj.reshape(B, S, N_BIG)
    out = pl.pallas_call(
        _fox_attn_body,
        grid=(B, FX_HEADS // FX_HP, nq),
        in_specs=[
            pl.BlockSpec((None, FX_T, wide), lambda b, h, i: (b, i, cq + h)),
            pl.BlockSpec((None, S, wide), lambda b, h, i: (b, 0, ck + h)),
            pl.BlockSpec((None, S, wide), lambda b, h, i: (b, 0, cv + h)),
            pl.BlockSpec((None, S, LANES), lambda b, h, i: (b, 0, 0)),
        ],
        out_specs=pl.BlockSpec((None, FX_T, wide), lambda b, h, i: (b, i, h)),
        out_shape=jax.ShapeDtypeStruct((B, S, D_MODEL), BF16),
        scratch_shapes=[
            pltpu.VMEM((FX_HP, S, 2 * FX_DH), BF16),
            pltpu.VMEM((FX_HP, nq, FX_VR, FX_T), BF16),
            pltpu.VMEM((FX_HP, 1, FX_T), F32),
            pltpu.VMEM((FX_HP, FX_VR, FX_T), F32),
            pltpu.VMEM((2, FX_HP, FX_T, FX_T), F32),
        ],
        compiler_params=_cparams(("parallel", "parallel", "arbitrary")),
        name="fox_attn",
    )(proj3, proj3, proj3, c_neg.reshape(B, S, LANES))
    return out.reshape(T, D_MODEL)


def _memkv_body(x_ref, g_ref, w_ref, o_ref):
    hb = _rms(x_ref[...], g_ref[...]).astype(BF16)
    o_ref[...] = jnp.dot(hb, w_ref[...], preferred_element_type=F32).astype(BF16)


def _memkv(mem2d, g, w_kv):
    R = mem2d.shape[0]
    tm = min(512, R)
    N = w_kv.shape[1]
    return pl.pallas_call(
        _memkv_body,
        grid=(R // tm,),
        in_specs=[
            pl.BlockSpec((tm, D_MODEL), lambda i: (i, 0)),
            pl.BlockSpec((1, D_MODEL), lambda i: (0, 0)),
            pl.BlockSpec((D_MODEL, N), lambda i: (0, 0)),
        ],
        out_specs=pl.BlockSpec((tm, N), lambda i: (i, 0)),
        out_shape=jax.ShapeDtypeStruct((R, N), BF16),
        compiler_params=_cparams(("parallel",)),
        name="memkv",
    )(mem2d, g, w_kv)


CA_TQ = 512


def _memattn_body(q_ref, k_ref, v_ref, o_ref):
    scale = CA_DH ** -0.5
    for h in range(CA_HEADS):
        sl = slice(h * CA_DH, (h + 1) * CA_DH)
        s = lax.dot_general(q_ref[:, sl], k_ref[:, sl], (((1,), (1,)), ((), ())),
                            preferred_element_type=F32) * scale
        p = jnp.exp(s - jnp.max(s, axis=-1, keepdims=True))
        l = jnp.sum(p, axis=-1, keepdims=True)
        o = jnp.dot(p.astype(BF16), v_ref[:, sl], preferred_element_type=F32) / l
        o_ref[:, sl] = o.astype(BF16)


def _memattn(proj, kv, B, S, M):
    T = B * S
    nq = S // CA_TQ
    kv3 = kv.reshape(B, M, 2 * D_MODEL)
    return pl.pallas_call(
        _memattn_body,
        grid=(B, nq),
        in_specs=[
            pl.BlockSpec((CA_TQ, D_MODEL), lambda b, i: (b * nq + i, COL_CAQ)),
            pl.BlockSpec((None, M, D_MODEL), lambda b, i: (b, 0, 0)),
            pl.BlockSpec((None, M, D_MODEL), lambda b, i: (b, 0, 1)),
        ],
        out_specs=pl.BlockSpec((CA_TQ, D_MODEL), lambda b, i: (b * nq + i, 0)),
        out_shape=jax.ShapeDtypeStruct((T, D_MODEL), BF16),
        compiler_params=_cparams(("parallel", "arbitrary")),
        name="memattn",
    )(proj, kv3, kv3)


MERGE_TM = 512
MOE_PARTS = 2


def _merge_body(y0_ref, y1_ref, y2_ref, g0_ref, g1_ref, g2_ref, x_ref, wb_ref, wo_ref, gn_ref, wr_ref, br_ref,
                o_ref, hp_ref, ri_ref, rw_ref, cnt_ref, carry_ref):
    merged = None
    for n, (y_ref, g_ref) in enumerate(((y0_ref, g0_ref), (y1_ref, g1_ref), (y2_ref, g2_ref))):
        p = jnp.dot(y_ref[...], wb_ref[n], preferred_element_type=F32)
        t = jax.nn.sigmoid(g_ref[...].astype(F32)) * p
        merged = t if merged is None else merged + t
    x2 = x_ref[...] + jnp.dot(merged.astype(BF16), wo_ref[...], preferred_element_type=F32)
    o_ref[...] = x2
    _route(x2, gn_ref, wr_ref, br_ref, hp_ref, ri_ref, rw_ref, cnt_ref, carry_ref)


def _merge(y_ml, y_fx, y_ca, proj, x2d, w_branch, w_out, g_moe, w_router, b_router, part):
    T = x2d.shape[0] // MOE_PARTS
    tm = MERGE_TM
    off = part * (T // tm)
    src = lambda i: (off + i, 0)
    row = lambda i: (i, 0)
    const = lambda i: (0, 0)
    return pl.pallas_call(
        _merge_body,
        grid=(T // tm,),
        in_specs=[
            pl.BlockSpec((tm, D_MODEL), src),
            pl.BlockSpec((tm, D_MODEL), src),
            pl.BlockSpec((tm, D_MODEL), src),
            pl.BlockSpec((tm, D_MODEL), lambda i: (off + i, COL_GATE0)),
            pl.BlockSpec((tm, D_MODEL), lambda i: (off + i, COL_GATE0 + 1)),
            pl.BlockSpec((tm, D_MODEL), lambda i: (off + i, COL_GATE0 + 2)),
            pl.BlockSpec((tm, D_MODEL), src),
            pl.BlockSpec((3, D_MODEL, D_MODEL), lambda i: (0, 0, 0)),
            pl.BlockSpec((D_MODEL, D_MODEL), const),
            pl.BlockSpec((1, D_MODEL), const),
            pl.BlockSpec((D_MODEL, LANES), const),
            pl.BlockSpec((1, LANES), const),
        ],
        out_specs=[
            pl.BlockSpec((tm, D_MODEL), row),
            pl.BlockSpec((tm, HALF), row),
            pl.BlockSpec((tm, LANES), row),
            pl.BlockSpec((tm, LANES), row),
            pl.BlockSpec((1, LANES), const),
        ],
        out_shape=[
            jax.ShapeDtypeStruct((T, D_MODEL), F32),
            jax.ShapeDtypeStruct((T, HALF), I32),
            jax.ShapeDtypeStruct((T, LANES), I32),
            jax.ShapeDtypeStruct((T, LANES), F32),
            jax.ShapeDtypeStruct((1, LANES), F32),
        ],
        scratch_shapes=[pltpu.VMEM((1, LANES), F32)],
        compiler_params=_cparams(("arbitrary",)),
        name="merge_router",
    )(y_ml, y_fx, y_ca, proj, proj, proj, x2d, w_branch, w_out, g_moe, w_router, b_router)


def _route(x2, g_ref, wr_ref, br_ref, hp_ref, ri_ref, rw_ref, cnt_ref, carry_ref):
    tm = MERGE_TM

    @pl.when(pl.program_id(0) == 0)
    def _():
        carry_ref[...] = jnp.zeros_like(carry_ref)

    h = _rms(x2, g_ref[...])
    hp_ref[...] = _pack_rows(h)
    logits = jnp.dot(h.astype(BF16), wr_ref[...], preferred_element_type=F32) + br_ref[...]
    lane = lax.broadcasted_iota(I32, (tm, LANES), 1)
    lane_f = lane.astype(F32)

    work = logits
    onehot_sum = jnp.zeros((tm, LANES), F32)
    vals, sels, idxs = [], [], []
    for _ in range(TOP_K):
        mx = jnp.max(work, axis=-1, keepdims=True)
        idx = jnp.min(jnp.where(work == mx, lane_f, float(LANES)), axis=-1, keepdims=True)
        sel = lane_f == idx
        onehot_sum = onehot_sum + sel.astype(F32)
        work = jnp.where(sel, -jnp.inf, work)
        vals.append(mx)
        sels.append(sel)
        idxs.append(idx)
    exps = [jnp.exp(v - vals[0]) for v in vals]
    total = exps[0] + exps[1] + exps[2] + exps[3]

    row = lax.broadcasted_iota(I32, (tm, tm), 0)
    col = lax.broadcasted_iota(I32, (tm, tm), 1)
    strict = (row > col).astype(BF16)
    before = jnp.dot(strict, onehot_sum.astype(BF16), preferred_element_type=F32) + carry_ref[...]
    carry_ref[...] = carry_ref[...] + jnp.sum(onehot_sum, axis=0, keepdims=True)
    cnt_ref[...] = carry_ref[...]

    ri = jnp.zeros((tm, LANES), I32)
    rw = jnp.zeros((tm, LANES), F32)
    for k in range(TOP_K):
        rank = jnp.sum(jnp.where(sels[k], before, 0.0), axis=-1, keepdims=True)
        ri = jnp.where(lane == k, idxs[k].astype(I32), ri)
        ri = jnp.where(lane == TOP_K + k, rank.astype(I32), ri)
        rw = jnp.where(lane == k, exps[k] / total, rw)
    ri_ref[...] = ri
    rw_ref[...] = rw


EXPERT_TM = 512
SC_CORES = 2
SC_SUBCORES = 16
SC_WORKERS = SC_CORES * SC_SUBCORES
SC_CHUNK = 64
PAD_SLOTS = N_EXPERTS * EXPERT_TM


def _sc_mesh():
    return plsc.VectorSubcoreMesh(core_axis_name="c", subcore_axis_name="s")


def _sc_worker():
    return lax.axis_index("s") * SC_CORES + lax.axis_index("c")


def _sc_dispatch(hp, dest, pad_idx, n_rows):
    T = hp.shape[0]
    per_w = T // SC_WORKERS
    n_ch = per_w // SC_CHUNK
    n_pc = PAD_SLOTS // (SC_WORKERS * SC_CHUNK)
    assert per_w % SC_CHUNK == 0 and n_ch >= 2 and n_ch % 2 == 0
    idx = dest.reshape(SC_WORKERS, n_ch, SC_CHUNK, TOP_K).transpose(0, 1, 3, 2)
    idx = idx.reshape(SC_WORKERS, n_ch * TOP_K, SC_CHUNK)
    pidx = pad_idx.reshape(SC_WORKERS, n_pc, SC_CHUNK)
    zeros = jnp.zeros((SC_CHUNK, HALF), I32)

    @functools.partial(
        pl.kernel, mesh=_sc_mesh(),
        out_type=jax.ShapeDtypeStruct((n_rows, HALF), I32),
        scratch_types=[
            pltpu.VMEM((n_ch * TOP_K, SC_CHUNK), I32),
            pltpu.VMEM((n_pc, SC_CHUNK), I32),
            pltpu.VMEM((2, SC_CHUNK, HALF), I32),
            pltpu.SemaphoreType.DMA((2,)),
            pltpu.SemaphoreType.DMA((2,)),
        ],
        name="sc_dispatch",
    )
    def k(hp_hbm, idx_hbm, pidx_hbm, zeros_hbm, xs_hbm, idx_v, pidx_v, rows_v, lsem, ssem):
        wid = _sc_worker()
        base = wid * per_w
        pltpu.sync_copy(idx_hbm.at[wid], idx_v)
        pltpu.sync_copy(pidx_hbm.at[wid], pidx_v)

        pltpu.sync_copy(zeros_hbm, rows_v.at[0])
        for p in range(n_pc):
            pltpu.make_async_copy(rows_v.at[0], xs_hbm.at[pidx_v.at[p]], ssem.at[0]).start()
        for p in range(n_pc):
            pltpu.make_async_copy(rows_v.at[0], xs_hbm.at[pidx_v.at[p]], ssem.at[0]).wait()

        def load(i, slot):
            return pltpu.make_async_copy(hp_hbm.at[pl.ds(base + i * SC_CHUNK, SC_CHUNK)], rows_v.at[slot],
                                         lsem.at[slot])

        def scatter(i, kk, slot):
            return pltpu.make_async_copy(rows_v.at[slot], xs_hbm.at[idx_v.at[i * TOP_K + kk]], ssem.at[slot])

        load(0, 0).start()

        def body(i2, carry):
            for slot in range(2):
                i = i2 * 2 + slot
                nxt = 1 - slot

                @pl.when(i + 1 < n_ch)
                def _():
                    @pl.when(i >= 1)
                    def _():
                        for kk in range(TOP_K):
                            scatter(i - 1, kk, nxt).wait()
                    load(i + 1, nxt).start()

                load(i, slot).wait()
                for kk in range(TOP_K):
                    scatter(i, kk, slot).start()
            return carry

        lax.fori_loop(0, n_ch // 2, body, 0)
        for kk in range(TOP_K):
            scatter(n_ch - 2, kk, 0).wait()
            scatter(n_ch - 1, kk, 1).wait()

    return k(hp, idx, pidx, zeros)


def _sc_gather(table, idx):
    n = idx.shape[0]
    per_w = n // SC_WORKERS
    n_ch = per_w // SC_CHUNK
    assert per_w % SC_CHUNK == 0 and n_ch >= 2 and n_ch % 2 == 0

    @functools.partial(
        pl.kernel, mesh=_sc_mesh(),
        out_type=jax.ShapeDtypeStruct((n, HALF), I32),
        scratch_types=[
            pltpu.VMEM((n_ch, SC_CHUNK), I32),
            pltpu.VMEM((2, SC_CHUNK, HALF), I32),
            pltpu.SemaphoreType.DMA((2,)),
            pltpu.SemaphoreType.DMA((2,)),
        ],
        name="sc_gather",
    )
    def k(table_hbm, idx_hbm, out_hbm, idx_v, rows_v, gsem, wsem):
        wid = _sc_worker()
        base = wid * per_w
        pltpu.sync_copy(idx_hbm.at[wid], idx_v)

        def gather(i, slot):
            return pltpu.make_async_copy(table_hbm.at[idx_v.at[i]], rows_v.at[slot], gsem.at[slot])

        def writeback(i, slot):
            return pltpu.make_async_copy(rows_v.at[slot], out_hbm.at[pl.ds(base + i * SC_CHUNK, SC_CHUNK)],
                                         wsem.at[slot])

        gather(0, 0).start()

        def body(i2, carry):
            for slot in range(2):
                i = i2 * 2 + slot
                nxt = 1 - slot

                @pl.when(i + 1 < n_ch)
                def _():
                    @pl.when(i >= 1)
                    def _():
                        writeback(i - 1, nxt).wait()
                    gather(i + 1, nxt).start()

                gather(i, slot).wait()
                writeback(i, slot).start()
            return carry

        lax.fori_loop(0, n_ch // 2, body, 0)
        writeback(n_ch - 2, 0).wait()
        writeback(n_ch - 1, 1).wait()

    return k(table, idx.reshape(SC_WORKERS, n_ch, SC_CHUNK))


FF_CHUNK = 512


def _expert_body(te_ref, nv_ref, x_ref, w1f_ref, b1_ref, w2f_ref, b2_ref, y_ref, w1_ref, w2_ref):
    i = pl.program_id(0)

    @pl.when(jnp.logical_or(i == 0, te_ref[i] != te_ref[jnp.maximum(i - 1, 0)]))
    def _():
        w1_ref[...] = w1f_ref[...].astype(BF16)
        w2_ref[...] = w2f_ref[...].astype(BF16)

    @pl.when(i < nv_ref[0])
    def _():
        lo, hi = _unpack_rows(x_ref[...])
        xlo = lo.astype(BF16)
        xhi = hi.astype(BF16)
        acc = jnp.zeros((EXPERT_TM, D_MODEL), F32) + b2_ref[...]
        for c in range(D_FF // FF_CHUNK):
            def up(off):
                cs = slice(off + c * FF_CHUNK, off + (c + 1) * FF_CHUNK)
                return (jnp.dot(xlo, w1_ref[0:HALF, cs], preferred_element_type=F32)
                        + jnp.dot(xhi, w1_ref[HALF:D_MODEL, cs], preferred_element_type=F32)
                        + b1_ref[:, cs])
            g = jnp.minimum(up(0), SWIGLU_LIMIT)
            lin = jnp.clip(up(D_FF), -SWIGLU_LIMIT, SWIGLU_LIMIT)
            a = g * jax.nn.sigmoid(SWIGLU_ALPHA * g) * (lin + 1.0)
            acc = acc + jnp.dot(a.astype(BF16), w2_ref[c * FF_CHUNK:(c + 1) * FF_CHUNK, :],
                                preferred_element_type=F32)
        y_ref[...] = _pack_rows(acc)


def _experts(tile_expert, n_valid, xs, w1, b1, w2, b2):
    n_rows = xs.shape[0]
    tm = EXPERT_TM
    n_tiles = n_rows // tm
    row = lambda i, te, nv: (jnp.minimum(i, nv[0] - 1), 0)
    grid_spec = pltpu.PrefetchScalarGridSpec(
        num_scalar_prefetch=2,
        grid=(n_tiles,),
        in_specs=[
            pl.BlockSpec((tm, HALF), row),
            pl.BlockSpec((None, D_MODEL, 2 * D_FF), lambda i, te, nv: (te[i], 0, 0)),
            pl.BlockSpec((None, 1, 2 * D_FF), lambda i, te, nv: (te[i], 0, 0)),
            pl.BlockSpec((None, D_FF, D_MODEL), lambda i, te, nv: (te[i], 0, 0)),
            pl.BlockSpec((None, 1, D_MODEL), lambda i, te, nv: (te[i], 0, 0)),
        ],
        out_specs=pl.BlockSpec((tm, HALF), row),
        scratch_shapes=[pltpu.VMEM((D_MODEL, 2 * D_FF), BF16), pltpu.VMEM((D_FF, D_MODEL), BF16)],
    )
    return pl.pallas_call(
        _expert_body,
        grid_spec=grid_spec,
        out_shape=jax.ShapeDtypeStruct((n_rows, HALF), I32),
        compiler_params=_cparams(("arbitrary",)),
        name="experts",
    )(tile_expert, n_valid, xs, w1, b1, w2, b2)


COMBINE_TM = 512


def _combine_body(y0_ref, y1_ref, y2_ref, y3_ref, rw_ref, x_ref, g_ref, *rest):
    o_ref = rest[-1]
    acc = x_ref[...]
    rw = rw_ref[...]
    for k, y_ref in enumerate((y0_ref, y1_ref, y2_ref, y3_ref)):
        lo, hi = _unpack_rows(y_ref[...])
        acc = acc + rw[:, k:k + 1] * jnp.concatenate([lo, hi], axis=-1)
    o_ref[...] = _rms(acc, g_ref[...])


def _combine(yg, rw, x2, g, part, out_prev):
    T = x2.shape[0]
    tm = COMBINE_TM
    nt = T // tm
    in_specs = [
        pl.BlockSpec((tm, HALF), lambda i: (i, 0)),
        pl.BlockSpec((tm, HALF), lambda i: (nt + i, 0)),
        pl.BlockSpec((tm, HALF), lambda i: (2 * nt + i, 0)),
        pl.BlockSpec((tm, HALF), lambda i: (3 * nt + i, 0)),
        pl.BlockSpec((tm, LANES), lambda i: (i, 0)),
        pl.BlockSpec((tm, D_MODEL), lambda i: (i, 0)),
        pl.BlockSpec((1, D_MODEL), lambda i: (0, 0)),
    ]
    args = [yg, yg, yg, yg, rw, x2, g]
    aliases = {}
    if out_prev is not None:
        in_specs.append(pl.BlockSpec(memory_space=pl.ANY))
        args.append(out_prev)
        aliases = {len(args) - 1: 0}
    return pl.pallas_call(
        _combine_body,
        grid=(nt,),
        in_specs=in_specs,
        out_specs=pl.BlockSpec((tm, D_MODEL), lambda i: (part * nt + i, 0)),
        out_shape=jax.ShapeDtypeStruct((T * MOE_PARTS, D_MODEL), F32),
        input_output_aliases=aliases,
        compiler_params=_cparams(("parallel",)),
        name="combine",
    )(*args)


def _pad_lanes(v, fill=0.0):
    v = v.reshape(1, -1).astype(F32)
    return jnp.pad(v, ((0, 0), (0, LANES - v.shape[1])), constant_values=fill)


def _layer(x2d, mem2d, B, S, M, norm_mix, w_in, b_ml_gates, conv_ml, ml_head_norm, b_fx_gate, norm_mem,
           w_mem_kv, w_branch, w_out, norm_moe, w_router, b_router, w_exp_in, b_exp_in, w_exp_out,
           b_exp_out, norm_out):
    T = B * S
    w_big = jnp.concatenate([w_in[:, 0:2048], w_in[:, 2056:3080], w_in[:, 3080:6152], w_in[:, 6160:7184],
                             w_in[:, 7184:10256]], axis=1).astype(BF16)
    w_small = jnp.concatenate([w_in[:, 2048:2056], w_in[:, 6152:6160]], axis=1)
    w_small = jnp.pad(w_small, ((0, 0), (0, LANES - w_small.shape[1]))).astype(BF16)
    row = lambda v: v.reshape(1, -1).astype(F32)

    proj, small = _inproj(x2d, row(norm_mix), w_big, w_small)

    y_ml = _mlstm(proj, small, conv_ml.astype(F32), _pad_lanes(b_ml_gates), row(ml_head_norm), B, S)

    b_fx = jnp.pad(b_fx_gate.reshape(1, -1).astype(F32), ((0, 0), (2 * ML_HEADS, LANES - 2 * ML_HEADS - FX_HEADS)))
    y_fx = _fox_attn(proj, _fox_gate(small, b_fx, B, S), B, S)

    kv = _memkv(mem2d, row(norm_mem), w_mem_kv.astype(BF16))
    y_ca = _memattn(proj, kv, B, S, M)

    w_r = jnp.pad(w_router, ((0, 0), (0, LANES - N_EXPERTS))).astype(BF16)
    moe_weights = (w_exp_in.astype(F32), b_exp_in.reshape(N_EXPERTS, 1, -1).astype(F32), w_exp_out.astype(F32),
                   b_exp_out.reshape(N_EXPERTS, 1, -1).astype(F32))
    routed = [_merge(y_ml, y_fx, y_ca, proj, x2d, w_branch.astype(BF16), w_out.astype(BF16), row(norm_moe), w_r,
                     _pad_lanes(b_router, fill=-1e30), part) for part in range(MOE_PARTS)]
    out = None
    for part, (x2, hp, ri, rw, cnt) in enumerate(routed):
        yg = _moe_rows(hp, ri, cnt, moe_weights)
        out = _combine(yg, rw, x2, row(norm_out), part, out)
    return out


def _moe_rows(hp, ri, cnt, moe_weights):
    T = hp.shape[0]
    tm = EXPERT_TM
    n_tiles = (T * TOP_K) // tm + N_EXPERTS
    counts = cnt[0, :N_EXPERTS].astype(I32)
    padded = ((counts + tm - 1) // tm) * tm
    gend = jnp.cumsum(padded)
    gstart = gend - padded
    expert_ids = jnp.arange(N_EXPERTS, dtype=I32)
    start_of = jnp.sum(jnp.where(ri[:, 0:TOP_K, None] == expert_ids, gstart, 0), axis=-1)
    dest = (start_of + ri[:, TOP_K:2 * TOP_K]).reshape(-1)
    n_valid = gend[-1] // tm
    tile_ids = jnp.arange(n_tiles, dtype=I32)
    last_tile = jnp.minimum(tile_ids, n_valid - 1)
    tile_e = jnp.minimum(jnp.sum((gend[None, :] <= last_tile[:, None] * tm).astype(I32), axis=1), N_EXPERTS - 1)

    slot = jnp.arange(tm, dtype=I32)
    spare = n_tiles * tm + slot % SC_CHUNK
    pad_idx = jnp.where(slot[None, :] < (padded - counts)[:, None], (gstart + counts)[:, None] + slot[None, :],
                        spare[None, :]).reshape(-1)

    xs = _sc_dispatch(hp, dest, pad_idx, n_tiles * tm + SC_CHUNK)
    ys = _experts(tile_e.astype(I32), n_valid.reshape(1).astype(I32), xs, *moe_weights)
    return _sc_gather(ys, dest.reshape(T, TOP_K).T.reshape(-1))


def kernel(x, mem, norm_mix, w_in, b_ml_gates, conv_ml, ml_head_norm, b_fx_gate, norm_mem, w_mem_kv, w_branch,
           w_out, norm_moe, w_router, b_router, w_exp_in, b_exp_in, w_exp_out, b_exp_out, norm_final):
    B, S, D = x.shape
    M = mem.shape[1]
    depth = norm_mix.shape[0]
    assert depth == 1, "the combine kernel fuses the final norm, so exactly one layer is supported"
    assert D == D_MODEL and S % ML_BLOCK == 0 and S % FX_T == 0 and S % CA_TQ == 0
    out = _layer(x.reshape(B * S, D), mem.reshape(B * M, D), B, S, M, norm_mix[0], w_in[0], b_ml_gates[0],
                 conv_ml[0], ml_head_norm[0], b_fx_gate[0], norm_mem[0], w_mem_kv[0], w_branch[0], w_out[0],
                 norm_moe[0], w_router[0], b_router[0], w_exp_in[0], b_exp_in[0], w_exp_out[0], b_exp_out[0],
                 norm_final)
    return out.reshape(B, S, D)
```

```python
import functools

import jax
import jax.numpy as jnp
from jax import lax
from jax.experimental import pallas as pl
from jax.experimental.pallas import tpu as pltpu
from jax.experimental.pallas import tpu_sc as plsc

F32 = jnp.float32
BF16 = jnp.bfloat16
I32 = jnp.int32

D_MODEL = 1024
N_MEM_HEADS = 4
ML_HEADS = 4
ML_DQK = 128
ML_DV = 256
ML_CONV = 4
FX_HEADS = 8
FX_DH = 128
CA_HEADS = 4
CA_DH = 256
N_EXPERTS = 32
TOP_K = 4
D_FF = D_MODEL
SWIGLU_LIMIT = 7.0
SWIGLU_ALPHA = 1.702
EPS = 1e-5
LANES = 128
HALF = D_MODEL // 2
HI_MASK = -65536

COL_MLQK, COL_MLV, COL_MLO, COL_FXQ, COL_FXK, COL_FXV, COL_CAQ, COL_GATE0 = 0, 1, 2, 3, 4, 5, 6, 7
N_BIG = 10 * D_MODEL

VMEM_LIMIT = 56 * 1024 * 1024


def _cparams(sem):
    return pltpu.CompilerParams(dimension_semantics=sem, vmem_limit_bytes=VMEM_LIMIT)


def _rms(x, g):
    return x * lax.rsqrt(jnp.mean(x * x, axis=-1, keepdims=True) + EPS) * g


def _log_sigmoid(x):
    return jnp.minimum(x, 0.0) - jnp.log1p(jnp.exp(-jnp.abs(x)))


def _pack_rows(y):
    bits = lax.bitcast_convert_type(y.astype(BF16).astype(F32), I32)
    return lax.shift_right_logical(bits[:, :HALF], 16) | (bits[:, HALF:] & HI_MASK)


def _unpack_rows(w):
    lo = lax.bitcast_convert_type(lax.shift_left(w, 16), F32)
    hi = lax.bitcast_convert_type(w & HI_MASK, F32)
    return lo, hi


def _inproj_body(x_ref, g_ref, w_ref, ws_ref, o_ref, os_ref, h_ref):
    @pl.when(pl.program_id(1) == 0)
    def _():
        hb = _rms(x_ref[...], g_ref[...]).astype(BF16)
        h_ref[...] = hb
        os_ref[...] = jnp.dot(hb, ws_ref[...], preferred_element_type=F32)

    o_ref[...] = jnp.dot(h_ref[...], w_ref[...], preferred_element_type=F32).astype(BF16)


def _inproj(x2d, g, w_big, w_small):
    T = x2d.shape[0]
    tm = min(1024, T)
    tn = 2048
    return pl.pallas_call(
        _inproj_body,
        grid=(T // tm, N_BIG // tn),
        in_specs=[
            pl.BlockSpec((tm, D_MODEL), lambda i, j: (i, 0)),
            pl.BlockSpec((1, D_MODEL), lambda i, j: (0, 0)),
            pl.BlockSpec((D_MODEL, tn), lambda i, j: (0, j)),
            pl.BlockSpec((D_MODEL, LANES), lambda i, j: (0, 0)),
        ],
        out_specs=[
            pl.BlockSpec((tm, tn), lambda i, j: (i, j)),
            pl.BlockSpec((tm, LANES), lambda i, j: (i, 0)),
        ],
        out_shape=[
            jax.ShapeDtypeStruct((T, N_BIG), BF16),
            jax.ShapeDtypeStruct((T, LANES), F32),
        ],
        scratch_shapes=[pltpu.VMEM((tm, D_MODEL), BF16)],
        compiler_params=_cparams(("parallel", "arbitrary")),
        name="inproj",
    )(x2d, g, w_big, w_small)


ML_BLOCK = 512
ML_MB = 1
ML_CHUNK = 128
CONV_PAD = 8


def _mlstm_body(qk_ref, v_ref, o_ref, g_ref, cw_ref, bg_ref, hn_ref, y_ref, xbuf, c_st, n_st, m_st):
    L = ML_CHUNK

    @pl.when(pl.program_id(1) == 0)
    def _():
        xbuf[:, 0:CONV_PAD, :] = jnp.zeros((ML_MB, CONV_PAD, D_MODEL), F32)
        c_st[...] = jnp.zeros_like(c_st)
        n_st[...] = jnp.zeros_like(n_st)
        m_st[...] = jnp.zeros_like(m_st)

    for bb in range(ML_MB):
        xbuf[bb, CONV_PAD:CONV_PAD + ML_BLOCK, :] = qk_ref[bb].astype(F32)
    cw = cw_ref[...]
    row = lax.broadcasted_iota(I32, (L, L), 0)
    col = lax.broadcasted_iota(I32, (L, L), 1)
    tri = (row >= col).astype(F32)
    causal_t = col >= row
    bg = bg_ref[...]
    scale = ML_DQK ** -0.5
    nt_dims = (((1,), (1,)), ((), ()))

    def chunk(bb, c):
        r0 = c * L
        conv = cw[0:1, :] * xbuf[bb, r0 + CONV_PAD - 3:r0 + CONV_PAD - 3 + L, :]
        for j in range(1, ML_CONV):
            s0 = r0 + CONV_PAD - 3 + j
            conv = conv + cw[j:j + 1, :] * xbuf[bb, s0:s0 + L, :]
        act = conv * jax.nn.sigmoid(conv)

        gates = g_ref[bb, r0:r0 + L, :] + bg
        cum = jnp.dot(tri, _log_sigmoid(gates), precision=lax.Precision.HIGHEST,
                      preferred_element_type=F32)
        gates_t = gates.T
        cum_t = cum.T
        for h in range(ML_HEADS):
            b_row = cum_t[ML_HEADS + h:ML_HEADS + h + 1, :]
            i_row = gates_t[h:h + 1, :]
            a_col = gates[:, h:h + 1] - cum[:, ML_HEADS + h:ML_HEADS + h + 1]
            st = bb * ML_HEADS + h
            m_prev = m_st[st]
            dm = jnp.where(causal_t, a_col + b_row, -jnp.inf)
            m_inter = b_row + m_prev
            m_t = jnp.maximum(jnp.max(dm, axis=0, keepdims=True), m_inter)
            w_intra = jnp.exp(dm - m_t)
            w_inter = jnp.exp(m_inter - m_t)

            qb = (act[:, h * ML_DQK:(h + 1) * ML_DQK] * scale).astype(BF16)
            kb = act[:, (ML_HEADS + h) * ML_DQK:(ML_HEADS + h + 1) * ML_DQK].astype(BF16)
            v_t = v_ref[bb, r0:r0 + L, h * ML_DV:(h + 1) * ML_DV].astype(F32).T
            p_t = lax.dot_general(kb, qb, nt_dims, preferred_element_type=F32) * w_intra
            c_old = c_st[st]
            n_old = n_st[st]
            num = jnp.dot(v_t.astype(BF16), p_t.astype(BF16), preferred_element_type=F32) + w_inter * (
                lax.dot_general(c_old.astype(BF16), qb, nt_dims, preferred_element_type=F32))
            qn = lax.dot_general(jnp.broadcast_to(n_old, (8, ML_DQK)).astype(BF16), qb, nt_dims,
                                 preferred_element_type=F32)[0:1, :]
            den = jnp.sum(p_t, axis=0, keepdims=True) + w_inter * qn
            hv = num / jnp.maximum(jnp.abs(den), jnp.exp(-m_t))

            m_new = m_t[:, L - 1:L]
            b_last = b_row[:, L - 1:L]
            wk = jnp.exp(b_last - b_row + i_row - m_new)
            decay = jnp.exp(b_last + m_prev - m_new)
            c_st[st] = decay * c_old + jnp.dot((v_t * wk).astype(BF16), kb, preferred_element_type=F32)
            n_st[st] = decay * n_old + jnp.dot(jnp.broadcast_to(wk, (8, L)).astype(BF16), kb,
                                               preferred_element_type=F32)[0:1, :]
            m_st[st] = m_new

            hn = (hv * lax.rsqrt(jnp.mean(hv * hv, axis=0, keepdims=True) + EPS)).T
            og = o_ref[bb, r0:r0 + L, h * ML_DV:(h + 1) * ML_DV].astype(F32)
            y_ref[bb, r0:r0 + L, h * ML_DV:(h + 1) * ML_DV] = (
                hn * hn_ref[:, h * ML_DV:(h + 1) * ML_DV] * jax.nn.sigmoid(og)).astype(BF16)

    for c in range(ML_BLOCK // L):
        for bb in range(ML_MB):
            chunk(bb, c)

    xbuf[:, 0:CONV_PAD, :] = xbuf[:, ML_BLOCK:ML_BLOCK + CONV_PAD, :]


def _mlstm(proj, small, conv_w, b_gates, head_norm, B, S):
    T = B * S
    ns = S // ML_BLOCK
    assert B % ML_MB == 0
    proj3 = proj.reshape(B, S, N_BIG)
    blk = lambda col: pl.BlockSpec((ML_MB, ML_BLOCK, D_MODEL), lambda b, s: (b, s, col))
    out = pl.pallas_call(
        _mlstm_body,
        grid=(B // ML_MB, ns),
        in_specs=[
            blk(COL_MLQK),
            blk(COL_MLV),
            blk(COL_MLO),
            pl.BlockSpec((ML_MB, ML_BLOCK, LANES), lambda b, s: (b, s, 0)),
            pl.BlockSpec((ML_CONV, D_MODEL), lambda b, s: (0, 0)),
            pl.BlockSpec((1, LANES), lambda b, s: (0, 0)),
            pl.BlockSpec((1, D_MODEL), lambda b, s: (0, 0)),
        ],
        out_specs=blk(0),
        out_shape=jax.ShapeDtypeStruct((B, S, D_MODEL), BF16),
        scratch_shapes=[
            pltpu.VMEM((ML_MB, ML_BLOCK + CONV_PAD, D_MODEL), F32),
            pltpu.VMEM((ML_MB * ML_HEADS, ML_DV, ML_DQK), F32),
            pltpu.VMEM((ML_MB * ML_HEADS, 1, ML_DQK), F32),
            pltpu.VMEM((ML_MB * ML_HEADS, 1, 1), F32),
        ],
        compiler_params=_cparams(("parallel", "arbitrary")),
        name="mlstm",
    )(proj3, proj3, proj3, small.reshape(B, S, LANES), conv_w, b_gates, head_norm)
    return out.reshape(T, D_MODEL)


FX_T = 512
FX_HP = 2
FX_VR = FX_DH + 16
LOG2E = 1.4426950408889634
N_PIECES = 3


def _fox_gate_body(g_ref, b_ref, o_ref):
    S = g_ref.shape[0]
    row = lax.broadcasted_iota(I32, (FX_T, FX_T), 0)
    col = lax.broadcasted_iota(I32, (FX_T, FX_T), 1)
    tri = (row >= col).astype(F32)
    carry = jnp.zeros((1, LANES), F32)
    for blk in range(S // FX_T):
        rows = slice(blk * FX_T, (blk + 1) * FX_T)
        lf = _log_sigmoid(g_ref[rows, :] + b_ref[...])
        cum = jnp.dot(tri, lf, precision=lax.Precision.HIGHEST, preferred_element_type=F32) + carry
        carry = cum[FX_T - 1:FX_T, :]
        o_ref[rows, :] = cum * (-LOG2E)


def _fox_gate(small, b_fx, B, S):
    return pl.pallas_call(
        _fox_gate_body,
        grid=(B,),
        in_specs=[
            pl.BlockSpec((S, LANES), lambda b: (b, 0)),
            pl.BlockSpec((1, LANES), lambda b: (0, 0)),
        ],
        out_specs=pl.BlockSpec((S, LANES), lambda b: (b, 0)),
        out_shape=jax.ShapeDtypeStruct((B * S, LANES), F32),
        compiler_params=_cparams(("parallel",)),
        name="fox_gate",
    )(small, b_fx)


def _fox_attn_body(q_ref, k_ref, v_ref, c_ref, o_ref, kx_ref, vt_ref, m_ref, acc_ref, s_ref):
    S = k_ref.shape[0]
    nq = S // FX_T

    c = c_ref[...]
    hi = c.astype(BF16)
    r1 = c - hi.astype(F32)
    mid = r1.astype(BF16)
    lo = (r1 - mid.astype(F32)).astype(BF16)
    sel_row = lax.broadcasted_iota(I32, (LANES, LANES), 0)
    sel_col = lax.broadcasted_iota(I32, (LANES, LANES), 1)
    ones_rows = (lax.broadcasted_iota(I32, (FX_VR - FX_DH, FX_T), 0) == 0).astype(BF16)
    head_slices = [slice(hh * FX_DH, (hh + 1) * FX_DH) for hh in range(FX_HP)]
    for hh, sl in enumerate(head_slices):
        lane = 2 * ML_HEADS + pl.program_id(1) * FX_HP + hh
        pieces = None
        for p, part in enumerate((hi, mid, lo)):
            pick = jnp.logical_and(sel_row == lane, sel_col == p).astype(BF16)
            t = jnp.dot(part, pick, preferred_element_type=F32)
            pieces = t if pieces is None else pieces + t
        kx_ref[hh, :, 0:FX_DH] = k_ref[:, sl]
        kx_ref[hh, :, FX_DH:2 * FX_DH] = pieces.astype(BF16)
        for j in range(nq):
            vt = v_ref[j * FX_T:(j + 1) * FX_T, sl].astype(F32).T.astype(BF16)
            vt_ref[hh, j] = jnp.concatenate([vt, ones_rows], axis=0)

    piece_rows = (lax.broadcasted_iota(I32, (FX_DH, FX_T), 0) < N_PIECES).astype(BF16)

    def start(i):
        q_x = []
        for sl in head_slices:
            q_t = (q_ref[i * FX_T:(i + 1) * FX_T, sl].astype(F32) * (FX_DH ** -0.5 * LOG2E)).T.astype(BF16)
            q_x.append(jnp.concatenate([q_t, piece_rows], axis=0))
        m_ref[i % 2] = jnp.full(m_ref.shape[1:], -jnp.inf, F32)
        acc_ref[i % 2] = jnp.zeros(acc_ref.shape[1:], F32)
        return q_x

    def key_rows(j):
        return pl.ds(j * FX_T, FX_T) if isinstance(j, int) else pl.ds(pl.multiple_of(j * FX_T, FX_T), FX_T)

    def scores(q_x, j, slot):
        for hh in range(FX_HP):
            s_ref[slot, hh] = jnp.dot(kx_ref[hh, key_rows(j), :], q_x[hh], preferred_element_type=F32)

    def consume(par, j, slot, masked):
        for hh in range(FX_HP):
            s = s_ref[slot, hh]
            if masked:
                key = lax.broadcasted_iota(I32, (FX_T, FX_T), 0)
                qry = lax.broadcasted_iota(I32, (FX_T, FX_T), 1)
                s = jnp.where(qry >= key, s, -jnp.inf)
            m_old = m_ref[par, hh]
            m_new = jnp.maximum(m_old, jnp.max(s, axis=0, keepdims=True))
            p = jnp.exp2(s - m_new).astype(BF16)
            acc_ref[par, hh] = jnp.exp2(m_old - m_new) * acc_ref[par, hh] + jnp.dot(
                vt_ref[hh, j], p, preferred_element_type=F32)
            m_ref[par, hh] = m_new

    def finish(i, slot):
        consume(i % 2, i, slot, True)
        for hh, sl in enumerate(head_slices):
            acc = acc_ref[i % 2, hh]
            o_ref[i * FX_T:(i + 1) * FX_T, sl] = (acc[0:FX_DH, :] / acc[FX_DH:FX_DH + 1, :]).T.astype(BF16)

    diag_slot = 0
    for i in range(nq):
        q_x = start(i)
        first = 0 if i == 0 else 1 - diag_slot
        scores(q_x, 0, first)
        if i > 0:
            finish(i - 1, diag_slot)

        def pair(jj, carry, q_x=q_x, first=first, par=i % 2):
            j = 2 * jj
            scores(q_x, j + 1, 1 - first)
            consume(par, j, first, False)
            scores(q_x, j + 2, first)
            consume(par, j + 1, 1 - first, False)
            return carry

        if i >= 2:
            lax.fori_loop(0, i // 2, pair, 0)
        if i % 2 == 1:
            scores(q_x, i, 1 - first)
            consume(i % 2, i - 1, first, False)
            diag_slot = 1 - first
        else:
            diag_slot = first
    finish(nq - 1, diag_slot)


def _fox_attn(proj, c_neg, B, S):
    T = B * S
    nq = S // FX_T
    wide = FX_HP * FX_DH
    cq = COL_FXQ * (D_MODEL // wide)
    ck = COL_FXK * (D_MODEL // wide)
    cv = COL_FXV * (D_MODEL // wide)
    proj3 = proj.reshape(B, S, N_BIG)
    out = pl.pallas_call(
        _fox_attn_body,
        grid=(B, FX_HEADS // FX_HP),
        in_specs=[
            pl.BlockSpec((None, S, wide), lambda b, h: (b, 0, cq + h)),
            pl.BlockSpec((None, S, wide), lambda b, h: (b, 0, ck + h)),
            pl.BlockSpec((None, S, wide), lambda b, h: (b, 0, cv + h)),
            pl.BlockSpec((None, S, LANES), lambda b, h: (b, 0, 0)),
        ],
        out_specs=pl.BlockSpec((None, S, wide), lambda b, h: (b, 0, h)),
        out_shape=jax.ShapeDtypeStruct((B, S, D_MODEL), BF16),
        scratch_shapes=[
            pltpu.VMEM((FX_HP, S, 2 * FX_DH), BF16),
            pltpu.VMEM((FX_HP, nq, FX_VR, FX_T), BF16),
            pltpu.VMEM((2, FX_HP, 1, FX_T), F32),
            pltpu.VMEM((2, FX_HP, FX_VR, FX_T), F32),
            pltpu.VMEM((2, FX_HP, FX_T, FX_T), F32),
        ],
        compiler_params=_cparams(("parallel", "parallel")),
        name="fox_attn",
    )(proj3, proj3, proj3, c_neg.reshape(B, S, LANES))
    return out.reshape(T, D_MODEL)


def _memkv_body(x_ref, g_ref, w_ref, o_ref):
    hb = _rms(x_ref[...], g_ref[...]).astype(BF16)
    o_ref[...] = jnp.dot(hb, w_ref[...], preferred_element_type=F32).astype(BF16)


def _memkv(mem2d, g, w_kv):
    R = mem2d.shape[0]
    tm = min(512, R)
    N = w_kv.shape[1]
    return pl.pallas_call(
        _memkv_body,
        grid=(R // tm,),
        in_specs=[
            pl.BlockSpec((tm, D_MODEL), lambda i: (i, 0)),
            pl.BlockSpec((1, D_MODEL), lambda i: (0, 0)),
            pl.BlockSpec((D_MODEL, N), lambda i: (0, 0)),
        ],
        out_specs=pl.BlockSpec((tm, N), lambda i: (i, 0)),
        out_shape=jax.ShapeDtypeStruct((R, N), BF16),
        compiler_params=_cparams(("parallel",)),
        name="memkv",
    )(mem2d, g, w_kv)


CA_TQ = 512


def _memattn_body(q_ref, k_ref, v_ref, o_ref):
    scale = CA_DH ** -0.5
    for h in range(CA_HEADS):
        sl = slice(h * CA_DH, (h + 1) * CA_DH)
        s = lax.dot_general(q_ref[:, sl], k_ref[:, sl], (((1,), (1,)), ((), ())),
                            preferred_element_type=F32) * scale
        p = jnp.exp(s - jnp.max(s, axis=-1, keepdims=True))
        l = jnp.sum(p, axis=-1, keepdims=True)
        o = jnp.dot(p.astype(BF16), v_ref[:, sl], preferred_element_type=F32) / l
        o_ref[:, sl] = o.astype(BF16)


def _memattn(proj, kv, B, S, M):
    T = B * S
    nq = S // CA_TQ
    kv3 = kv.reshape(B, M, 2 * D_MODEL)
    return pl.pallas_call(
        _memattn_body,
        grid=(B, nq),
        in_specs=[
            pl.BlockSpec((CA_TQ, D_MODEL), lambda b, i: (b * nq + i, COL_CAQ)),
            pl.BlockSpec((None, M, D_MODEL), lambda b, i: (b, 0, 0)),
            pl.BlockSpec((None, M, D_MODEL), lambda b, i: (b, 0, 1)),
        ],
        out_specs=pl.BlockSpec((CA_TQ, D_MODEL), lambda b, i: (b * nq + i, 0)),
        out_shape=jax.ShapeDtypeStruct((T, D_MODEL), BF16),
        compiler_params=_cparams(("parallel", "arbitrary")),
        name="memattn",
    )(proj, kv3, kv3)


MERGE_TM = 512
MOE_PARTS = 2


def _merge_body(y0_ref, y1_ref, y2_ref, g0_ref, g1_ref, g2_ref, x_ref, wb_ref, wo_ref, gn_ref, wr_ref, br_ref,
                o_ref, hp_ref, ri_ref, rw_ref, cnt_ref, carry_ref):
    merged = None
    for n, (y_ref, g_ref) in enumerate(((y0_ref, g0_ref), (y1_ref, g1_ref), (y2_ref, g2_ref))):
        p = jnp.dot(y_ref[...], wb_ref[n], preferred_element_type=F32)
        t = jax.nn.sigmoid(g_ref[...].astype(F32)) * p
        merged = t if merged is None else merged + t
    x2 = x_ref[...] + jnp.dot(merged.astype(BF16), wo_ref[...], preferred_element_type=F32)
    o_ref[...] = x2
    _route(x2, gn_ref, wr_ref, br_ref, hp_ref, ri_ref, rw_ref, cnt_ref, carry_ref)


def _merge(y_ml, y_fx, y_ca, proj, x2d, w_branch, w_out, g_moe, w_router, b_router, part):
    T = x2d.shape[0] // MOE_PARTS
    tm = MERGE_TM
    off = part * (T // tm)
    src = lambda i: (off + i, 0)
    row = lambda i: (i, 0)
    const = lambda i: (0, 0)
    return pl.pallas_call(
        _merge_body,
        grid=(T // tm,),
        in_specs=[
            pl.BlockSpec((tm, D_MODEL), src),
            pl.BlockSpec((tm, D_MODEL), src),
            pl.BlockSpec((tm, D_MODEL), src),
            pl.BlockSpec((tm, D_MODEL), lambda i: (off + i, COL_GATE0)),
            pl.BlockSpec((tm, D_MODEL), lambda i: (off + i, COL_GATE0 + 1)),
            pl.BlockSpec((tm, D_MODEL), lambda i: (off + i, COL_GATE0 + 2)),
            pl.BlockSpec((tm, D_MODEL), src),
            pl.BlockSpec((3, D_MODEL, D_MODEL), lambda i: (0, 0, 0)),
            pl.BlockSpec((D_MODEL, D_MODEL), const),
            pl.BlockSpec((1, D_MODEL), const),
            pl.BlockSpec((D_MODEL, LANES), const),
            pl.BlockSpec((1, LANES), const),
        ],
        out_specs=[
            pl.BlockSpec((tm, D_MODEL), row),
            pl.BlockSpec((tm, HALF), row),
            pl.BlockSpec((tm, LANES), row),
            pl.BlockSpec((tm, LANES), row),
            pl.BlockSpec((1, LANES), const),
        ],
        out_shape=[
            jax.ShapeDtypeStruct((T, D_MODEL), F32),
            jax.ShapeDtypeStruct((T, HALF), I32),
            jax.ShapeDtypeStruct((T, LANES), I32),
            jax.ShapeDtypeStruct((T, LANES), F32),
            jax.ShapeDtypeStruct((1, LANES), F32),
        ],
        scratch_shapes=[pltpu.VMEM((1, LANES), F32)],
        compiler_params=_cparams(("arbitrary",)),
        name="merge_router",
    )(y_ml, y_fx, y_ca, proj, proj, proj, x2d, w_branch, w_out, g_moe, w_router, b_router)


def _route(x2, g_ref, wr_ref, br_ref, hp_ref, ri_ref, rw_ref, cnt_ref, carry_ref):
    tm = MERGE_TM

    @pl.when(pl.program_id(0) == 0)
    def _():
        carry_ref[...] = jnp.zeros_like(carry_ref)

    h = _rms(x2, g_ref[...])
    hp_ref[...] = _pack_rows(h)
    logits = jnp.dot(h.astype(BF16), wr_ref[...], preferred_element_type=F32) + br_ref[...]
    lane = lax.broadcasted_iota(I32, (tm, LANES), 1)
    lane_f = lane.astype(F32)

    work = logits
    onehot_sum = jnp.zeros((tm, LANES), F32)
    vals, sels, idxs = [], [], []
    for _ in range(TOP_K):
        mx = jnp.max(work, axis=-1, keepdims=True)
        idx = jnp.min(jnp.where(work == mx, lane_f, float(LANES)), axis=-1, keepdims=True)
        sel = lane_f == idx
        onehot_sum = onehot_sum + sel.astype(F32)
        work = jnp.where(sel, -jnp.inf, work)
        vals.append(mx)
        sels.append(sel)
        idxs.append(idx)
    exps = [jnp.exp(v - vals[0]) for v in vals]
    total = exps[0] + exps[1] + exps[2] + exps[3]

    row = lax.broadcasted_iota(I32, (tm, tm), 0)
    col = lax.broadcasted_iota(I32, (tm, tm), 1)
    strict = (row > col).astype(BF16)
    before = jnp.dot(strict, onehot_sum.astype(BF16), preferred_element_type=F32) + carry_ref[...]
    carry_ref[...] = carry_ref[...] + jnp.sum(onehot_sum, axis=0, keepdims=True)
    cnt_ref[...] = carry_ref[...]

    ri = jnp.zeros((tm, LANES), I32)
    rw = jnp.zeros((tm, LANES), F32)
    for k in range(TOP_K):
        rank = jnp.sum(jnp.where(sels[k], before, 0.0), axis=-1, keepdims=True)
        ri = jnp.where(lane == k, idxs[k].astype(I32), ri)
        ri = jnp.where(lane == TOP_K + k, rank.astype(I32), ri)
        rw = jnp.where(lane == k, exps[k] / total, rw)
    ri_ref[...] = ri
    rw_ref[...] = rw


EXPERT_TM = 512
SC_CORES = 2
SC_SUBCORES = 16
SC_WORKERS = SC_CORES * SC_SUBCORES
SC_CHUNK = 64
PAD_SLOTS = N_EXPERTS * EXPERT_TM


def _sc_mesh():
    return plsc.VectorSubcoreMesh(core_axis_name="c", subcore_axis_name="s")


def _sc_worker():
    return lax.axis_index("s") * SC_CORES + lax.axis_index("c")


def _sc_dispatch(hp, dest, pad_idx, n_rows):
    T = hp.shape[0]
    per_w = T // SC_WORKERS
    n_ch = per_w // SC_CHUNK
    n_pc = PAD_SLOTS // (SC_WORKERS * SC_CHUNK)
    assert per_w % SC_CHUNK == 0 and n_ch >= 2 and n_ch % 2 == 0
    idx = dest.reshape(SC_WORKERS, n_ch, SC_CHUNK, TOP_K).transpose(0, 1, 3, 2)
    idx = idx.reshape(SC_WORKERS, n_ch * TOP_K, SC_CHUNK)
    pidx = pad_idx.reshape(SC_WORKERS, n_pc, SC_CHUNK)
    zeros = jnp.zeros((SC_CHUNK, HALF), I32)

    @functools.partial(
        pl.kernel, mesh=_sc_mesh(),
        out_type=jax.ShapeDtypeStruct((n_rows, HALF), I32),
        scratch_types=[
            pltpu.VMEM((n_ch * TOP_K, SC_CHUNK), I32),
            pltpu.VMEM((n_pc, SC_CHUNK), I32),
            pltpu.VMEM((2, SC_CHUNK, HALF), I32),
            pltpu.SemaphoreType.DMA((2,)),
            pltpu.SemaphoreType.DMA((2,)),
        ],
        name="sc_dispatch",
    )
    def k(hp_hbm, idx_hbm, pidx_hbm, zeros_hbm, xs_hbm, idx_v, pidx_v, rows_v, lsem, ssem):
        wid = _sc_worker()
        base = wid * per_w
        pltpu.sync_copy(idx_hbm.at[wid], idx_v)
        pltpu.sync_copy(pidx_hbm.at[wid], pidx_v)

        pltpu.sync_copy(zeros_hbm, rows_v.at[0])
        for p in range(n_pc):
            pltpu.make_async_copy(rows_v.at[0], xs_hbm.at[pidx_v.at[p]], ssem.at[0]).start()
        for p in range(n_pc):
            pltpu.make_async_copy(rows_v.at[0], xs_hbm.at[pidx_v.at[p]], ssem.at[0]).wait()

        def load(i, slot):
            return pltpu.make_async_copy(hp_hbm.at[pl.ds(base + i * SC_CHUNK, SC_CHUNK)], rows_v.at[slot],
                                         lsem.at[slot])

        def scatter(i, kk, slot):
            return pltpu.make_async_copy(rows_v.at[slot], xs_hbm.at[idx_v.at[i * TOP_K + kk]], ssem.at[slot])

        load(0, 0).start()

        def body(i2, carry):
            for slot in range(2):
                i = i2 * 2 + slot
                nxt = 1 - slot

                @pl.when(i + 1 < n_ch)
                def _():
                    @pl.when(i >= 1)
                    def _():
                        for kk in range(TOP_K):
                            scatter(i - 1, kk, nxt).wait()
                    load(i + 1, nxt).start()

                load(i, slot).wait()
                for kk in range(TOP_K):
                    scatter(i, kk, slot).start()
            return carry

        lax.fori_loop(0, n_ch // 2, body, 0)
        for kk in range(TOP_K):
            scatter(n_ch - 2, kk, 0).wait()
            scatter(n_ch - 1, kk, 1).wait()

    return k(hp, idx, pidx, zeros)


def _sc_gather(table, idx):
    n = idx.shape[0]
    per_w = n // SC_WORKERS
    n_ch = per_w // SC_CHUNK
    assert per_w % SC_CHUNK == 0 and n_ch >= 2 and n_ch % 2 == 0

    @functools.partial(
        pl.kernel, mesh=_sc_mesh(),
        out_type=jax.ShapeDtypeStruct((n, HALF), I32),
        scratch_types=[
            pltpu.VMEM((n_ch, SC_CHUNK), I32),
            pltpu.VMEM((2, SC_CHUNK, HALF), I32),
            pltpu.SemaphoreType.DMA((2,)),
            pltpu.SemaphoreType.DMA((2,)),
        ],
        name="sc_gather",
    )
    def k(table_hbm, idx_hbm, out_hbm, idx_v, rows_v, gsem, wsem):
        wid = _sc_worker()
        base = wid * per_w
        pltpu.sync_copy(idx_hbm.at[wid], idx_v)

        def gather(i, slot):
            return pltpu.make_async_copy(table_hbm.at[idx_v.at[i]], rows_v.at[slot], gsem.at[slot])

        def writeback(i, slot):
            return pltpu.make_async_copy(rows_v.at[slot], out_hbm.at[pl.ds(base + i * SC_CHUNK, SC_CHUNK)],
                                         wsem.at[slot])

        gather(0, 0).start()

        def body(i2, carry):
            for slot in range(2):
                i = i2 * 2 + slot
                nxt = 1 - slot

                @pl.when(i + 1 < n_ch)
                def _():
                    @pl.when(i >= 1)
                    def _():
                        writeback(i - 1, nxt).wait()
                    gather(i + 1, nxt).start()

                gather(i, slot).wait()
                writeback(i, slot).start()
            return carry

        lax.fori_loop(0, n_ch // 2, body, 0)
        writeback(n_ch - 2, 0).wait()
        writeback(n_ch - 1, 1).wait()

    return k(table, idx.reshape(SC_WORKERS, n_ch, SC_CHUNK))


FF_CHUNK = 512


def _expert_body(te_ref, nv_ref, x_ref, w1f_ref, b1_ref, w2f_ref, b2_ref, y_ref, w1_ref, w2_ref):
    i = pl.program_id(0)

    @pl.when(jnp.logical_or(i == 0, te_ref[i] != te_ref[jnp.maximum(i - 1, 0)]))
    def _():
        w1_ref[...] = w1f_ref[...].astype(BF16)
        w2_ref[...] = w2f_ref[...].astype(BF16)

    @pl.when(i < nv_ref[0])
    def _():
        lo, hi = _unpack_rows(x_ref[...])
        xlo = lo.astype(BF16)
        xhi = hi.astype(BF16)
        acc = jnp.zeros((EXPERT_TM, D_MODEL), F32) + b2_ref[...]
        for c in range(D_FF // FF_CHUNK):
            def up(off):
                cs = slice(off + c * FF_CHUNK, off + (c + 1) * FF_CHUNK)
                return (jnp.dot(xlo, w1_ref[0:HALF, cs], preferred_element_type=F32)
                        + jnp.dot(xhi, w1_ref[HALF:D_MODEL, cs], preferred_element_type=F32)
                        + b1_ref[:, cs])
            g = jnp.minimum(up(0), SWIGLU_LIMIT)
            lin = jnp.clip(up(D_FF), -SWIGLU_LIMIT, SWIGLU_LIMIT)
            a = g * jax.nn.sigmoid(SWIGLU_ALPHA * g) * (lin + 1.0)
            acc = acc + jnp.dot(a.astype(BF16), w2_ref[c * FF_CHUNK:(c + 1) * FF_CHUNK, :],
                                preferred_element_type=F32)
        y_ref[...] = _pack_rows(acc)


def _experts(tile_expert, n_valid, xs, w1, b1, w2, b2):
    n_rows = xs.shape[0]
    tm = EXPERT_TM
    n_tiles = n_rows // tm
    row = lambda i, te, nv: (jnp.minimum(i, nv[0] - 1), 0)
    grid_spec = pltpu.PrefetchScalarGridSpec(
        num_scalar_prefetch=2,
        grid=(n_tiles,),
        in_specs=[
            pl.BlockSpec((tm, HALF), row),
            pl.BlockSpec((None, D_MODEL, 2 * D_FF), lambda i, te, nv: (te[i], 0, 0)),
            pl.BlockSpec((None, 1, 2 * D_FF), lambda i, te, nv: (te[i], 0, 0)),
            pl.BlockSpec((None, D_FF, D_MODEL), lambda i, te, nv: (te[i], 0, 0)),
            pl.BlockSpec((None, 1, D_MODEL), lambda i, te, nv: (te[i], 0, 0)),
        ],
        out_specs=pl.BlockSpec((tm, HALF), row),
        scratch_shapes=[pltpu.VMEM((D_MODEL, 2 * D_FF), BF16), pltpu.VMEM((D_FF, D_MODEL), BF16)],
    )
    return pl.pallas_call(
        _expert_body,
        grid_spec=grid_spec,
        out_shape=jax.ShapeDtypeStruct((n_rows, HALF), I32),
        compiler_params=_cparams(("arbitrary",)),
        name="experts",
    )(tile_expert, n_valid, xs, w1, b1, w2, b2)


COMBINE_TM = 512


def _combine_body(y0_ref, y1_ref, y2_ref, y3_ref, rw_ref, x_ref, g_ref, *rest):
    o_ref = rest[-1]
    acc = x_ref[...]
    rw = rw_ref[...]
    for k, y_ref in enumerate((y0_ref, y1_ref, y2_ref, y3_ref)):
        lo, hi = _unpack_rows(y_ref[...])
        acc = acc + rw[:, k:k + 1] * jnp.concatenate([lo, hi], axis=-1)
    o_ref[...] = _rms(acc, g_ref[...])


def _combine(yg, rw, x2, g, part, out_prev):
    T = x2.shape[0]
    tm = COMBINE_TM
    nt = T // tm
    in_specs = [
        pl.BlockSpec((tm, HALF), lambda i: (i, 0)),
        pl.BlockSpec((tm, HALF), lambda i: (nt + i, 0)),
        pl.BlockSpec((tm, HALF), lambda i: (2 * nt + i, 0)),
        pl.BlockSpec((tm, HALF), lambda i: (3 * nt + i, 0)),
        pl.BlockSpec((tm, LANES), lambda i: (i, 0)),
        pl.BlockSpec((tm, D_MODEL), lambda i: (i, 0)),
        pl.BlockSpec((1, D_MODEL), lambda i: (0, 0)),
    ]
    args = [yg, yg, yg, yg, rw, x2, g]
    aliases = {}
    if out_prev is not None:
        in_specs.append(pl.BlockSpec(memory_space=pl.ANY))
        args.append(out_prev)
        aliases = {len(args) - 1: 0}
    return pl.pallas_call(
        _combine_body,
        grid=(nt,),
        in_specs=in_specs,
        out_specs=pl.BlockSpec((tm, D_MODEL), lambda i: (part * nt + i, 0)),
        out_shape=jax.ShapeDtypeStruct((T * MOE_PARTS, D_MODEL), F32),
        input_output_aliases=aliases,
        compiler_params=_cparams(("parallel",)),
        name="combine",
    )(*args)


def _pad_lanes(v, fill=0.0):
    v = v.reshape(1, -1).astype(F32)
    return jnp.pad(v, ((0, 0), (0, LANES - v.shape[1])), constant_values=fill)


def _layer(x2d, mem2d, B, S, M, norm_mix, w_in, b_ml_gates, conv_ml, ml_head_norm, b_fx_gate, norm_mem,
           w_mem_kv, w_branch, w_out, norm_moe, w_router, b_router, w_exp_in, b_exp_in, w_exp_out,
           b_exp_out, norm_out):
    T = B * S
    w16 = w_in.astype(BF16)
    w_big = jnp.concatenate([w16[:, 0:2048], w16[:, 2056:3080], w16[:, 3080:6152], w16[:, 6160:7184],
                             w16[:, 7184:10256]], axis=1)
    w_small = jnp.concatenate([w16[:, 2048:2056], w16[:, 6152:6160]], axis=1)
    w_small = jnp.pad(w_small, ((0, 0), (0, LANES - w_small.shape[1])))
    row = lambda v: v.reshape(1, -1).astype(F32)

    proj, small = _inproj(x2d, row(norm_mix), w_big, w_small)

    y_ml = _mlstm(proj, small, conv_ml.astype(F32), _pad_lanes(b_ml_gates), row(ml_head_norm), B, S)

    b_fx = jnp.pad(b_fx_gate.reshape(1, -1).astype(F32), ((0, 0), (2 * ML_HEADS, LANES - 2 * ML_HEADS - FX_HEADS)))
    y_fx = _fox_attn(proj, _fox_gate(small, b_fx, B, S), B, S)

    kv = _memkv(mem2d, row(norm_mem), w_mem_kv.astype(BF16))
    y_ca = _memattn(proj, kv, B, S, M)

    w_r = jnp.pad(w_router, ((0, 0), (0, LANES - N_EXPERTS))).astype(BF16)
    moe_weights = (w_exp_in.astype(F32), b_exp_in.reshape(N_EXPERTS, 1, -1).astype(F32), w_exp_out.astype(F32),
                   b_exp_out.reshape(N_EXPERTS, 1, -1).astype(F32))
    routed = [_merge(y_ml, y_fx, y_ca, proj, x2d, w_branch.astype(BF16), w_out.astype(BF16), row(norm_moe), w_r,
                     _pad_lanes(b_router, fill=-1e30), part) for part in range(MOE_PARTS)]
    out = None
    for part, (x2, hp, ri, rw, cnt) in enumerate(routed):
        yg = _moe_rows(hp, ri, cnt, moe_weights)
        out = _combine(yg, rw, x2, row(norm_out), part, out)
    return out


def _moe_rows(hp, ri, cnt, moe_weights):
    T = hp.shape[0]
    tm = EXPERT_TM
    n_tiles = (T * TOP_K) // tm + N_EXPERTS
    counts = cnt[0, :N_EXPERTS].astype(I32)
    padded = ((counts + tm - 1) // tm) * tm
    gend = jnp.cumsum(padded)
    gstart = gend - padded
    expert_ids = jnp.arange(N_EXPERTS, dtype=I32)
    start_of = jnp.sum(jnp.where(ri[:, 0:TOP_K, None] == expert_ids, gstart, 0), axis=-1)
    dest = (start_of + ri[:, TOP_K:2 * TOP_K]).reshape(-1)
    n_valid = gend[-1] // tm
    tile_ids = jnp.arange(n_tiles, dtype=I32)
    last_tile = jnp.minimum(tile_ids, n_valid - 1)
    tile_e = jnp.minimum(jnp.sum((gend[None, :] <= last_tile[:, None] * tm).astype(I32), axis=1), N_EXPERTS - 1)

    slot = jnp.arange(tm, dtype=I32)
    spare = n_tiles * tm + slot % SC_CHUNK
    pad_idx = jnp.where(slot[None, :] < (padded - counts)[:, None], (gstart + counts)[:, None] + slot[None, :],
                        spare[None, :]).reshape(-1)

    xs = _sc_dispatch(hp, dest, pad_idx, n_tiles * tm + SC_CHUNK)
    ys = _experts(tile_e.astype(I32), n_valid.reshape(1).astype(I32), xs, *moe_weights)
    return _sc_gather(ys, dest.reshape(T, TOP_K).T.reshape(-1))


def kernel(x, mem, norm_mix, w_in, b_ml_gates, conv_ml, ml_head_norm, b_fx_gate, norm_mem, w_mem_kv, w_branch,
           w_out, norm_moe, w_router, b_router, w_exp_in, b_exp_in, w_exp_out, b_exp_out, norm_final):
    B, S, D = x.shape
    M = mem.shape[1]
    depth = norm_mix.shape[0]
    assert depth == 1, "the combine kernel fuses the final norm, so exactly one layer is supported"
    assert D == D_MODEL and S % ML_BLOCK == 0 and S % FX_T == 0 and S % CA_TQ == 0
    out = _layer(x.reshape(B * S, D), mem.reshape(B * M, D), B, S, M, norm_mix[0], w_in[0], b_ml_gates[0],
                 conv_ml[0], ml_head_norm[0], b_fx_gate[0], norm_mem[0], w_mem_kv[0], w_branch[0], w_out[0],
                 norm_moe[0], w_router[0], b_router[0], w_exp_in[0], b_exp_in[0], w_exp_out[0], b_exp_out[0],
                 norm_final)
    return out.reshape(B, S, D)
```

```python
import functools

import jax
import jax.numpy as jnp
from jax import lax
from jax.experimental import pallas as pl
from jax.experimental.pallas import tpu as pltpu
from jax.experimental.pallas import tpu_sc as plsc

F32 = jnp.float32
BF16 = jnp.bfloat16
I32 = jnp.int32

D_MODEL = 1024
N_MEM_HEADS = 4
ML_HEADS = 4
ML_DQK = 128
ML_DV = 256
ML_CONV = 4
FX_HEADS = 8
FX_DH = 128
CA_HEADS = 4
CA_DH = 256
N_EXPERTS = 32
TOP_K = 4
D_FF = D_MODEL
SWIGLU_LIMIT = 7.0
SWIGLU_ALPHA = 1.702
EPS = 1e-5
LANES = 128
HALF = D_MODEL // 2
HI_MASK = -65536

COL_MLQK, COL_MLV, COL_MLO, COL_FXQ, COL_FXK, COL_FXV, COL_CAQ, COL_GATE0 = 0, 1, 2, 3, 4, 5, 6, 7
N_BIG = 10 * D_MODEL

VMEM_LIMIT = 56 * 1024 * 1024


def _cparams(sem):
    return pltpu.CompilerParams(dimension_semantics=sem, vmem_limit_bytes=VMEM_LIMIT)


def _rms(x, g):
    return x * lax.rsqrt(jnp.mean(x * x, axis=-1, keepdims=True) + EPS) * g


def _log_sigmoid(x):
    return jnp.minimum(x, 0.0) - jnp.log1p(jnp.exp(-jnp.abs(x)))


def _pack_rows(y):
    bits = lax.bitcast_convert_type(y.astype(BF16).astype(F32), I32)
    return lax.shift_right_logical(bits[:, :HALF], 16) | (bits[:, HALF:] & HI_MASK)


def _unpack_rows(w):
    lo = lax.bitcast_convert_type(lax.shift_left(w, 16), F32)
    hi = lax.bitcast_convert_type(w & HI_MASK, F32)
    return lo, hi


def _inproj_body(x_ref, g_ref, w_ref, ws_ref, o_ref, os_ref, h_ref):
    @pl.when(pl.program_id(1) == 0)
    def _():
        hb = _rms(x_ref[...], g_ref[...]).astype(BF16)
        h_ref[...] = hb
        os_ref[...] = jnp.dot(hb, ws_ref[...], preferred_element_type=F32)

    o_ref[...] = jnp.dot(h_ref[...], w_ref[...], preferred_element_type=F32).astype(BF16)


def _inproj(x2d, g, w_big, w_small):
    T = x2d.shape[0]
    tm = min(1024, T)
    tn = 2048
    return pl.pallas_call(
        _inproj_body,
        grid=(T // tm, N_BIG // tn),
        in_specs=[
            pl.BlockSpec((tm, D_MODEL), lambda i, j: (i, 0)),
            pl.BlockSpec((1, D_MODEL), lambda i, j: (0, 0)),
            pl.BlockSpec((D_MODEL, tn), lambda i, j: (0, j)),
            pl.BlockSpec((D_MODEL, LANES), lambda i, j: (0, 0)),
        ],
        out_specs=[
            pl.BlockSpec((tm, tn), lambda i, j: (i, j)),
            pl.BlockSpec((tm, LANES), lambda i, j: (i, 0)),
        ],
        out_shape=[
            jax.ShapeDtypeStruct((T, N_BIG), BF16),
            jax.ShapeDtypeStruct((T, LANES), F32),
        ],
        scratch_shapes=[pltpu.VMEM((tm, D_MODEL), BF16)],
        compiler_params=_cparams(("parallel", "arbitrary")),
        name="inproj",
    )(x2d, g, w_big, w_small)


ML_BLOCK = 512
ML_MB = 1
ML_CHUNK = 128
CONV_PAD = 8


def _mlstm_body(qk_ref, v_ref, o_ref, g_ref, cw_ref, bg_ref, hn_ref, y_ref, xbuf, c_st, n_st, m_st):
    L = ML_CHUNK

    @pl.when(pl.program_id(1) == 0)
    def _():
        xbuf[:, 0:CONV_PAD, :] = jnp.zeros((ML_MB, CONV_PAD, D_MODEL), F32)
        c_st[...] = jnp.zeros_like(c_st)
        n_st[...] = jnp.zeros_like(n_st)
        m_st[...] = jnp.zeros_like(m_st)

    for bb in range(ML_MB):
        xbuf[bb, CONV_PAD:CONV_PAD + ML_BLOCK, :] = qk_ref[bb].astype(F32)
    cw = cw_ref[...]
    row = lax.broadcasted_iota(I32, (L, L), 0)
    col = lax.broadcasted_iota(I32, (L, L), 1)
    tri = (row >= col).astype(BF16)
    causal_t = col >= row
    bg = bg_ref[...]
    scale = ML_DQK ** -0.5
    nt_dims = (((1,), (1,)), ((), ()))

    def chunk(bb, c):
        r0 = c * L
        conv = cw[0:1, :] * xbuf[bb, r0 + CONV_PAD - 3:r0 + CONV_PAD - 3 + L, :]
        for j in range(1, ML_CONV):
            s0 = r0 + CONV_PAD - 3 + j
            conv = conv + cw[j:j + 1, :] * xbuf[bb, s0:s0 + L, :]
        act = conv * jax.nn.sigmoid(conv)

        gates = g_ref[bb, r0:r0 + L, :] + bg
        lf = _log_sigmoid(gates)
        cum = jnp.zeros((L, LANES), F32)
        for _ in range(3):
            piece = lf.astype(BF16)
            cum = cum + jnp.dot(tri, piece, preferred_element_type=F32)
            lf = lf - piece.astype(F32)
        gates_t = gates.T
        cum_t = cum.T
        for h in range(ML_HEADS):
            b_row = cum_t[ML_HEADS + h:ML_HEADS + h + 1, :]
            i_row = gates_t[h:h + 1, :]
            a_col = gates[:, h:h + 1] - cum[:, ML_HEADS + h:ML_HEADS + h + 1]
            st = bb * ML_HEADS + h
            m_prev = m_st[st]
            dm = jnp.where(causal_t, a_col + b_row, -jnp.inf)
            m_inter = b_row + m_prev
            m_t = jnp.maximum(jnp.max(dm, axis=0, keepdims=True), m_inter)
            w_intra = jnp.exp(dm - m_t)
            w_inter = jnp.exp(m_inter - m_t)

            qb = (act[:, h * ML_DQK:(h + 1) * ML_DQK] * scale).astype(BF16)
            kb = act[:, (ML_HEADS + h) * ML_DQK:(ML_HEADS + h + 1) * ML_DQK].astype(BF16)
            v_t = v_ref[bb, r0:r0 + L, h * ML_DV:(h + 1) * ML_DV].astype(F32).T
            p_t = lax.dot_general(kb, qb, nt_dims, preferred_element_type=F32) * w_intra
            c_old = c_st[st]
            n_old = n_st[st]
            num = jnp.dot(v_t.astype(BF16), p_t.astype(BF16), preferred_element_type=F32) + w_inter * (
                lax.dot_general(c_old.astype(BF16), qb, nt_dims, preferred_element_type=F32))
            qn = lax.dot_general(jnp.broadcast_to(n_old, (8, ML_DQK)).astype(BF16), qb, nt_dims,
                                 preferred_element_type=F32)[0:1, :]
            den = jnp.sum(p_t, axis=0, keepdims=True) + w_inter * qn
            hv = num / jnp.maximum(jnp.abs(den), jnp.exp(-m_t))

            m_new = m_t[:, L - 1:L]
            b_last = b_row[:, L - 1:L]
            wk = jnp.exp(b_last - b_row + i_row - m_new)
            decay = jnp.exp(b_last + m_prev - m_new)
            c_st[st] = decay * c_old + jnp.dot((v_t * wk).astype(BF16), kb, preferred_element_type=F32)
            n_st[st] = decay * n_old + jnp.dot(jnp.broadcast_to(wk, (8, L)).astype(BF16), kb,
                                               preferred_element_type=F32)[0:1, :]
            m_st[st] = m_new

            hn = (hv * lax.rsqrt(jnp.mean(hv * hv, axis=0, keepdims=True) + EPS)).T
            og = o_ref[bb, r0:r0 + L, h * ML_DV:(h + 1) * ML_DV].astype(F32)
            y_ref[bb, r0:r0 + L, h * ML_DV:(h + 1) * ML_DV] = (
                hn * hn_ref[:, h * ML_DV:(h + 1) * ML_DV] * jax.nn.sigmoid(og)).astype(BF16)

    for c in range(ML_BLOCK // L):
        for bb in range(ML_MB):
            chunk(bb, c)

    xbuf[:, 0:CONV_PAD, :] = xbuf[:, ML_BLOCK:ML_BLOCK + CONV_PAD, :]


def _mlstm(proj, small, conv_w, b_gates, head_norm, B, S):
    T = B * S
    ns = S // ML_BLOCK
    assert B % ML_MB == 0
    proj3 = proj.reshape(B, S, N_BIG)
    blk = lambda col: pl.BlockSpec((ML_MB, ML_BLOCK, D_MODEL), lambda b, s: (b, s, col))
    out = pl.pallas_call(
        _mlstm_body,
        grid=(B // ML_MB, ns),
        in_specs=[
            blk(COL_MLQK),
            blk(COL_MLV),
            blk(COL_MLO),
            pl.BlockSpec((ML_MB, ML_BLOCK, LANES), lambda b, s: (b, s, 0)),
            pl.BlockSpec((ML_CONV, D_MODEL), lambda b, s: (0, 0)),
            pl.BlockSpec((1, LANES), lambda b, s: (0, 0)),
            pl.BlockSpec((1, D_MODEL), lambda b, s: (0, 0)),
        ],
        out_specs=blk(0),
        out_shape=jax.ShapeDtypeStruct((B, S, D_MODEL), BF16),
        scratch_shapes=[
            pltpu.VMEM((ML_MB, ML_BLOCK + CONV_PAD, D_MODEL), F32),
            pltpu.VMEM((ML_MB * ML_HEADS, ML_DV, ML_DQK), F32),
            pltpu.VMEM((ML_MB * ML_HEADS, 1, ML_DQK), F32),
            pltpu.VMEM((ML_MB * ML_HEADS, 1, 1), F32),
        ],
        compiler_params=_cparams(("parallel", "arbitrary")),
        name="mlstm",
    )(proj3, proj3, proj3, small.reshape(B, S, LANES), conv_w, b_gates, head_norm)
    return out.reshape(T, D_MODEL)


FX_T = 512
FX_HP = 2
FX_VR = FX_DH + 16
LOG2E = 1.4426950408889634
N_PIECES = 3
FX_GATE_T = 128


def _fox_gate_body(g_ref, b_ref, o_ref):
    S = g_ref.shape[0]
    row = lax.broadcasted_iota(I32, (FX_GATE_T, FX_GATE_T), 0)
    col = lax.broadcasted_iota(I32, (FX_GATE_T, FX_GATE_T), 1)
    tri = (row >= col).astype(BF16)
    carry = jnp.zeros((1, LANES), F32)
    for blk in range(S // FX_GATE_T):
        rows = slice(blk * FX_GATE_T, (blk + 1) * FX_GATE_T)
        lf = _log_sigmoid(g_ref[rows, :] + b_ref[...])
        cum = carry
        for _ in range(N_PIECES):
            piece = lf.astype(BF16)
            cum = cum + jnp.dot(tri, piece, preferred_element_type=F32)
            lf = lf - piece.astype(F32)
        carry = cum[FX_GATE_T - 1:FX_GATE_T, :]
        o_ref[rows, :] = cum * (-LOG2E)


def _fox_gate(small, b_fx, B, S):
    return pl.pallas_call(
        _fox_gate_body,
        grid=(B,),
        in_specs=[
            pl.BlockSpec((S, LANES), lambda b: (b, 0)),
            pl.BlockSpec((1, LANES), lambda b: (0, 0)),
        ],
        out_specs=pl.BlockSpec((S, LANES), lambda b: (b, 0)),
        out_shape=jax.ShapeDtypeStruct((B * S, LANES), F32),
        compiler_params=_cparams(("parallel",)),
        name="fox_gate",
    )(small, b_fx)


def _fox_attn_body(q_ref, k_ref, v_ref, c_ref, o_ref, kx_ref, vt_ref, m_ref, acc_ref, s_ref):
    S = k_ref.shape[0]
    nq = S // FX_T

    c = c_ref[...]
    hi = c.astype(BF16)
    r1 = c - hi.astype(F32)
    mid = r1.astype(BF16)
    lo = (r1 - mid.astype(F32)).astype(BF16)
    sel_row = lax.broadcasted_iota(I32, (LANES, LANES), 0)
    sel_col = lax.broadcasted_iota(I32, (LANES, LANES), 1)
    ones_rows = (lax.broadcasted_iota(I32, (FX_VR - FX_DH, FX_T), 0) == 0).astype(BF16)
    head_slices = [slice(hh * FX_DH, (hh + 1) * FX_DH) for hh in range(FX_HP)]
    for hh, sl in enumerate(head_slices):
        lane = 2 * ML_HEADS + pl.program_id(1) * FX_HP + hh
        pieces = None
        for p, part in enumerate((hi, mid, lo)):
            pick = jnp.logical_and(sel_row == lane, sel_col == p).astype(BF16)
            t = jnp.dot(part, pick, preferred_element_type=F32)
            pieces = t if pieces is None else pieces + t
        kx_ref[hh, :, 0:FX_DH] = k_ref[:, sl]
        kx_ref[hh, :, FX_DH:2 * FX_DH] = pieces.astype(BF16)
        for j in range(nq):
            vt = v_ref[j * FX_T:(j + 1) * FX_T, sl].astype(F32).T.astype(BF16)
            vt_ref[hh, j] = jnp.concatenate([vt, ones_rows], axis=0)

    piece_rows = (lax.broadcasted_iota(I32, (FX_DH, FX_T), 0) < N_PIECES).astype(BF16)

    def start(i):
        q_x = []
        for sl in head_slices:
            q_t = (q_ref[i * FX_T:(i + 1) * FX_T, sl].astype(F32) * (FX_DH ** -0.5 * LOG2E)).T.astype(BF16)
            q_x.append(jnp.concatenate([q_t, piece_rows], axis=0))
        m_ref[i % 2] = jnp.full(m_ref.shape[1:], -jnp.inf, F32)
        acc_ref[i % 2] = jnp.zeros(acc_ref.shape[1:], F32)
        return q_x

    def key_rows(j):
        return pl.ds(j * FX_T, FX_T) if isinstance(j, int) else pl.ds(pl.multiple_of(j * FX_T, FX_T), FX_T)

    def scores(q_x, j, slot):
        for hh in range(FX_HP):
            s_ref[slot, hh] = jnp.dot(kx_ref[hh, key_rows(j), :], q_x[hh], preferred_element_type=F32)

    def consume(par, j, slot, masked):
        for hh in range(FX_HP):
            s = s_ref[slot, hh]
            if masked:
                key = lax.broadcasted_iota(I32, (FX_T, FX_T), 0)
                qry = lax.broadcasted_iota(I32, (FX_T, FX_T), 1)
                s = jnp.where(qry >= key, s, -jnp.inf)
            m_old = m_ref[par, hh]
            m_new = jnp.maximum(m_old, jnp.max(s, axis=0, keepdims=True))
            p = jnp.exp2(s - m_new).astype(BF16)
            acc_ref[par, hh] = jnp.exp2(m_old - m_new) * acc_ref[par, hh] + jnp.dot(
                vt_ref[hh, j], p, preferred_element_type=F32)
            m_ref[par, hh] = m_new

    def finish(i, slot):
        consume(i % 2, i, slot, True)
        for hh, sl in enumerate(head_slices):
            acc = acc_ref[i % 2, hh]
            o_ref[i * FX_T:(i + 1) * FX_T, sl] = (acc[0:FX_DH, :] / acc[FX_DH:FX_DH + 1, :]).T.astype(BF16)

    diag_slot = 0
    for i in range(nq):
        q_x = start(i)
        first = 0 if i == 0 else 1 - diag_slot
        scores(q_x, 0, first)
        if i > 0:
            finish(i - 1, diag_slot)

        def pair(jj, carry, q_x=q_x, first=first, par=i % 2):
            j = 2 * jj
            scores(q_x, j + 1, 1 - first)
            consume(par, j, first, False)
            scores(q_x, j + 2, first)
            consume(par, j + 1, 1 - first, False)
            return carry

        if i >= 2:
            lax.fori_loop(0, i // 2, pair, 0)
        if i % 2 == 1:
            scores(q_x, i, 1 - first)
            consume(i % 2, i - 1, first, False)
            diag_slot = 1 - first
        else:
            diag_slot = first
    finish(nq - 1, diag_slot)


def _fox_attn(proj, c_neg, B, S):
    T = B * S
    nq = S // FX_T
    wide = FX_HP * FX_DH
    cq = COL_FXQ * (D_MODEL // wide)
    ck = COL_FXK * (D_MODEL // wide)
    cv = COL_FXV * (D_MODEL // wide)
    proj3 = proj.reshape(B, S, N_BIG)
    out = pl.pallas_call(
        _fox_attn_body,
        grid=(B, FX_HEADS // FX_HP),
        in_specs=[
            pl.BlockSpec((None, S, wide), lambda b, h: (b, 0, cq + h)),
            pl.BlockSpec((None, S, wide), lambda b, h: (b, 0, ck + h)),
            pl.BlockSpec((None, S, wide), lambda b, h: (b, 0, cv + h)),
            pl.BlockSpec((None, S, LANES), lambda b, h: (b, 0, 0)),
        ],
        out_specs=pl.BlockSpec((None, S, wide), lambda b, h: (b, 0, h)),
        out_shape=jax.ShapeDtypeStruct((B, S, D_MODEL), BF16),
        scratch_shapes=[
            pltpu.VMEM((FX_HP, S, 2 * FX_DH), BF16),
            pltpu.VMEM((FX_HP, nq, FX_VR, FX_T), BF16),
            pltpu.VMEM((2, FX_HP, 1, FX_T), F32),
            pltpu.VMEM((2, FX_HP, FX_VR, FX_T), F32),
            pltpu.VMEM((2, FX_HP, FX_T, FX_T), F32),
        ],
        compiler_params=_cparams(("parallel", "parallel")),
        name="fox_attn",
    )(proj3, proj3, proj3, c_neg.reshape(B, S, LANES))
    return out.reshape(T, D_MODEL)


def _memkv_body(x_ref, g_ref, w_ref, o_ref):
    hb = _rms(x_ref[...], g_ref[...]).astype(BF16)
    o_ref[...] = jnp.dot(hb, w_ref[...], preferred_element_type=F32).astype(BF16)


def _memkv(mem2d, g, w_kv):
    R = mem2d.shape[0]
    tm = min(512, R)
    N = w_kv.shape[1]
    return pl.pallas_call(
        _memkv_body,
        grid=(R // tm,),
        in_specs=[
            pl.BlockSpec((tm, D_MODEL), lambda i: (i, 0)),
            pl.BlockSpec((1, D_MODEL), lambda i: (0, 0)),
            pl.BlockSpec((D_MODEL, N), lambda i: (0, 0)),
        ],
        out_specs=pl.BlockSpec((tm, N), lambda i: (i, 0)),
        out_shape=jax.ShapeDtypeStruct((R, N), BF16),
        compiler_params=_cparams(("parallel",)),
        name="memkv",
    )(mem2d, g, w_kv)


CA_TQ = 512


def _memattn_body(q_ref, k_ref, v_ref, o_ref):
    scale = CA_DH ** -0.5
    for h in range(CA_HEADS):
        sl = slice(h * CA_DH, (h + 1) * CA_DH)
        s = lax.dot_general(q_ref[:, sl], k_ref[:, sl], (((1,), (1,)), ((), ())),
                            preferred_element_type=F32) * scale
        p = jnp.exp(s - jnp.max(s, axis=-1, keepdims=True))
        l = jnp.sum(p, axis=-1, keepdims=True)
        o = jnp.dot(p.astype(BF16), v_ref[:, sl], preferred_element_type=F32) / l
        o_ref[:, sl] = o.astype(BF16)


def _memattn(proj, kv, B, S, M):
    T = B * S
    nq = S // CA_TQ
    kv3 = kv.reshape(B, M, 2 * D_MODEL)
    return pl.pallas_call(
        _memattn_body,
        grid=(B, nq),
        in_specs=[
            pl.BlockSpec((CA_TQ, D_MODEL), lambda b, i: (b * nq + i, COL_CAQ)),
            pl.BlockSpec((None, M, D_MODEL), lambda b, i: (b, 0, 0)),
            pl.BlockSpec((None, M, D_MODEL), lambda b, i: (b, 0, 1)),
        ],
        out_specs=pl.BlockSpec((CA_TQ, D_MODEL), lambda b, i: (b * nq + i, 0)),
        out_shape=jax.ShapeDtypeStruct((T, D_MODEL), BF16),
        compiler_params=_cparams(("parallel", "arbitrary")),
        name="memattn",
    )(proj, kv3, kv3)


MERGE_TM = 512
MOE_PARTS = 2


def _merge_body(y0_ref, y1_ref, y2_ref, g0_ref, g1_ref, g2_ref, x_ref, wb_ref, wo_ref, gn_ref, wr_ref, br_ref,
                o_ref, hp_ref, ri_ref, rw_ref, cnt_ref, carry_ref):
    merged = None
    for n, (y_ref, g_ref) in enumerate(((y0_ref, g0_ref), (y1_ref, g1_ref), (y2_ref, g2_ref))):
        p = jnp.dot(y_ref[...], wb_ref[n], preferred_element_type=F32)
        t = jax.nn.sigmoid(g_ref[...].astype(F32)) * p
        merged = t if merged is None else merged + t
    x2 = x_ref[...] + jnp.dot(merged.astype(BF16), wo_ref[...], preferred_element_type=F32)
    o_ref[...] = x2
    _route(x2, gn_ref, wr_ref, br_ref, hp_ref, ri_ref, rw_ref, cnt_ref, carry_ref)


def _merge(y_ml, y_fx, y_ca, proj, x2d, w_branch, w_out, g_moe, w_router, b_router, part):
    T = x2d.shape[0] // MOE_PARTS
    tm = MERGE_TM
    off = part * (T // tm)
    src = lambda i: (off + i, 0)
    row = lambda i: (i, 0)
    const = lambda i: (0, 0)
    return pl.pallas_call(
        _merge_body,
        grid=(T // tm,),
        in_specs=[
            pl.BlockSpec((tm, D_MODEL), src),
            pl.BlockSpec((tm, D_MODEL), src),
            pl.BlockSpec((tm, D_MODEL), src),
            pl.BlockSpec((tm, D_MODEL), lambda i: (off + i, COL_GATE0)),
            pl.BlockSpec((tm, D_MODEL), lambda i: (off + i, COL_GATE0 + 1)),
            pl.BlockSpec((tm, D_MODEL), lambda i: (off + i, COL_GATE0 + 2)),
            pl.BlockSpec((tm, D_MODEL), src),
            pl.BlockSpec((3, D_MODEL, D_MODEL), lambda i: (0, 0, 0)),
            pl.BlockSpec((D_MODEL, D_MODEL), const),
            pl.BlockSpec((1, D_MODEL), const),
            pl.BlockSpec((N_EXPERTS, D_MODEL), const),
            pl.BlockSpec((N_EXPERTS, 1), const),
        ],
        out_specs=[
            pl.BlockSpec((tm, D_MODEL), row),
            pl.BlockSpec((tm, HALF), row),
            pl.BlockSpec((2 * TOP_K, tm), lambda i: (0, i)),
            pl.BlockSpec((2 * TOP_K, tm), lambda i: (0, i)),
            pl.BlockSpec((N_EXPERTS, 1), const),
        ],
        out_shape=[
            jax.ShapeDtypeStruct((T, D_MODEL), F32),
            jax.ShapeDtypeStruct((T, HALF), I32),
            jax.ShapeDtypeStruct((2 * TOP_K, T), I32),
            jax.ShapeDtypeStruct((2 * TOP_K, T), F32),
            jax.ShapeDtypeStruct((N_EXPERTS, 1), F32),
        ],
        scratch_shapes=[pltpu.VMEM((N_EXPERTS, 1), F32)],
        compiler_params=_cparams(("arbitrary",)),
        name="merge_router",
    )(y_ml, y_fx, y_ca, proj, proj, proj, x2d, w_branch, w_out, g_moe, w_router, b_router)


def _route(x2, g_ref, wr_ref, br_ref, hp_ref, ri_ref, rw_ref, cnt_ref, carry_ref):
    tm = MERGE_TM

    @pl.when(pl.program_id(0) == 0)
    def _():
        carry_ref[...] = jnp.zeros_like(carry_ref)

    h = _rms(x2, g_ref[...])
    hp_ref[...] = _pack_rows(h)
    logits = lax.dot_general(wr_ref[...], h.astype(BF16), (((1,), (1,)), ((), ())),
                             preferred_element_type=F32) + br_ref[...]
    eid = lax.broadcasted_iota(I32, (N_EXPERTS, tm), 0).astype(F32)

    work = logits
    onehot_sum = jnp.zeros((N_EXPERTS, tm), F32)
    vals, sels, idxs = [], [], []
    for _ in range(TOP_K):
        mx = jnp.max(work, axis=0, keepdims=True)
        idx = jnp.min(jnp.where(work == mx, eid, float(N_EXPERTS)), axis=0, keepdims=True)
        sel = eid == idx
        onehot_sum = onehot_sum + sel.astype(F32)
        work = jnp.where(sel, -jnp.inf, work)
        vals.append(mx)
        sels.append(sel)
        idxs.append(idx)
    exps = [jnp.exp(v - vals[0]) for v in vals]
    total = exps[0] + exps[1] + exps[2] + exps[3]

    earlier = (lax.broadcasted_iota(I32, (tm, tm), 0) < lax.broadcasted_iota(I32, (tm, tm), 1)).astype(BF16)
    before = jnp.dot(onehot_sum.astype(BF16), earlier, preferred_element_type=F32) + carry_ref[...]
    carry_ref[...] = carry_ref[...] + jnp.sum(onehot_sum, axis=1, keepdims=True)
    cnt_ref[...] = carry_ref[...]

    out_row = lax.broadcasted_iota(I32, (2 * TOP_K, tm), 0)
    ri = jnp.zeros((2 * TOP_K, tm), I32)
    rw = jnp.zeros((2 * TOP_K, tm), F32)
    for k in range(TOP_K):
        rank = jnp.sum(jnp.where(sels[k], before, 0.0), axis=0, keepdims=True)
        ri = jnp.where(out_row == k, idxs[k].astype(I32), ri)
        ri = jnp.where(out_row == TOP_K + k, rank.astype(I32), ri)
        rw = jnp.where(out_row == k, exps[k] / total, rw)
    ri_ref[...] = ri
    rw_ref[...] = rw


EXPERT_TM = 512
SC_CORES = 2
SC_SUBCORES = 16
SC_WORKERS = SC_CORES * SC_SUBCORES
SC_CHUNK = 64
PAD_SLOTS = N_EXPERTS * EXPERT_TM


def _sc_mesh():
    return plsc.VectorSubcoreMesh(core_axis_name="c", subcore_axis_name="s")


def _sc_worker():
    return lax.axis_index("s") * SC_CORES + lax.axis_index("c")


def _sc_dispatch(hp, dest, pad_idx, n_rows):
    T = hp.shape[0]
    per_w = T // SC_WORKERS
    n_ch = per_w // SC_CHUNK
    n_pc = PAD_SLOTS // (SC_WORKERS * SC_CHUNK)
    assert per_w % SC_CHUNK == 0 and n_ch >= 2 and n_ch % 2 == 0
    idx = dest.reshape(TOP_K, SC_WORKERS, n_ch, SC_CHUNK).transpose(1, 2, 0, 3)
    idx = idx.reshape(SC_WORKERS, n_ch * TOP_K, SC_CHUNK)
    pidx = pad_idx.reshape(SC_WORKERS, n_pc, SC_CHUNK)
    zeros = jnp.zeros((SC_CHUNK, HALF), I32)

    @functools.partial(
        pl.kernel, mesh=_sc_mesh(),
        out_type=jax.ShapeDtypeStruct((n_rows, HALF), I32),
        scratch_types=[
            pltpu.VMEM((n_ch * TOP_K, SC_CHUNK), I32),
            pltpu.VMEM((n_pc, SC_CHUNK), I32),
            pltpu.VMEM((2, SC_CHUNK, HALF), I32),
            pltpu.SemaphoreType.DMA((2,)),
            pltpu.SemaphoreType.DMA((2,)),
        ],
        name="sc_dispatch",
    )
    def k(hp_hbm, idx_hbm, pidx_hbm, zeros_hbm, xs_hbm, idx_v, pidx_v, rows_v, lsem, ssem):
        wid = _sc_worker()
        base = wid * per_w
        pltpu.sync_copy(idx_hbm.at[wid], idx_v)
        pltpu.sync_copy(pidx_hbm.at[wid], pidx_v)

        pltpu.sync_copy(zeros_hbm, rows_v.at[0])
        for p in range(n_pc):
            pltpu.make_async_copy(rows_v.at[0], xs_hbm.at[pidx_v.at[p]], ssem.at[0]).start()
        for p in range(n_pc):
            pltpu.make_async_copy(rows_v.at[0], xs_hbm.at[pidx_v.at[p]], ssem.at[0]).wait()

        def load(i, slot):
            return pltpu.make_async_copy(hp_hbm.at[pl.ds(base + i * SC_CHUNK, SC_CHUNK)], rows_v.at[slot],
                                         lsem.at[slot])

        def scatter(i, kk, slot):
            return pltpu.make_async_copy(rows_v.at[slot], xs_hbm.at[idx_v.at[i * TOP_K + kk]], ssem.at[slot])

        load(0, 0).start()

        def body(i2, carry):
            for slot in range(2):
                i = i2 * 2 + slot
                nxt = 1 - slot

                @pl.when(i + 1 < n_ch)
                def _():
                    @pl.when(i >= 1)
                    def _():
                        for kk in range(TOP_K):
                            scatter(i - 1, kk, nxt).wait()
                    load(i + 1, nxt).start()

                load(i, slot).wait()
                for kk in range(TOP_K):
                    scatter(i, kk, slot).start()
            return carry

        lax.fori_loop(0, n_ch // 2, body, 0)
        for kk in range(TOP_K):
            scatter(n_ch - 2, kk, 0).wait()
            scatter(n_ch - 1, kk, 1).wait()

    return k(hp, idx, pidx, zeros)


def _sc_gather(table, idx):
    n = idx.shape[0]
    per_w = n // SC_WORKERS
    n_ch = per_w // SC_CHUNK
    assert per_w % SC_CHUNK == 0 and n_ch >= 2 and n_ch % 2 == 0

    @functools.partial(
        pl.kernel, mesh=_sc_mesh(),
        out_type=jax.ShapeDtypeStruct((n, HALF), I32),
        scratch_types=[
            pltpu.VMEM((n_ch, SC_CHUNK), I32),
            pltpu.VMEM((2, SC_CHUNK, HALF), I32),
            pltpu.SemaphoreType.DMA((2,)),
            pltpu.SemaphoreType.DMA((2,)),
        ],
        name="sc_gather",
    )
    def k(table_hbm, idx_hbm, out_hbm, idx_v, rows_v, gsem, wsem):
        wid = _sc_worker()
        base = wid * per_w
        pltpu.sync_copy(idx_hbm.at[wid], idx_v)

        def gather(i, slot):
            return pltpu.make_async_copy(table_hbm.at[idx_v.at[i]], rows_v.at[slot], gsem.at[slot])

        def writeback(i, slot):
            return pltpu.make_async_copy(rows_v.at[slot], out_hbm.at[pl.ds(base + i * SC_CHUNK, SC_CHUNK)],
                                         wsem.at[slot])

        gather(0, 0).start()

        def body(i2, carry):
            for slot in range(2):
                i = i2 * 2 + slot
                nxt = 1 - slot

                @pl.when(i + 1 < n_ch)
                def _():
                    @pl.when(i >= 1)
                    def _():
                        writeback(i - 1, nxt).wait()
                    gather(i + 1, nxt).start()

                gather(i, slot).wait()
                writeback(i, slot).start()
            return carry

        lax.fori_loop(0, n_ch // 2, body, 0)
        writeback(n_ch - 2, 0).wait()
        writeback(n_ch - 1, 1).wait()

    return k(table, idx.reshape(SC_WORKERS, n_ch, SC_CHUNK))


FF_CHUNK = 512


def _expert_body(te_ref, nv_ref, x_ref, w1f_ref, b1_ref, w2f_ref, b2_ref, y_ref, w1_ref, w2_ref):
    i = pl.program_id(0)

    @pl.when(jnp.logical_or(i == 0, te_ref[i] != te_ref[jnp.maximum(i - 1, 0)]))
    def _():
        w1_ref[...] = w1f_ref[...].astype(BF16)
        w2_ref[...] = w2f_ref[...].astype(BF16)

    @pl.when(i < nv_ref[0])
    def _():
        lo, hi = _unpack_rows(x_ref[...])
        xlo = lo.astype(BF16)
        xhi = hi.astype(BF16)
        acc = jnp.zeros((EXPERT_TM, D_MODEL), F32) + b2_ref[...]
        for c in range(D_FF // FF_CHUNK):
            def up(off):
                cs = slice(off + c * FF_CHUNK, off + (c + 1) * FF_CHUNK)
                return (jnp.dot(xlo, w1_ref[0:HALF, cs], preferred_element_type=F32)
                        + jnp.dot(xhi, w1_ref[HALF:D_MODEL, cs], preferred_element_type=F32)
                        + b1_ref[:, cs])
            g = jnp.minimum(up(0), SWIGLU_LIMIT)
            lin = jnp.clip(up(D_FF), -SWIGLU_LIMIT, SWIGLU_LIMIT)
            a = g * jax.nn.sigmoid(SWIGLU_ALPHA * g) * (lin + 1.0)
            acc = acc + jnp.dot(a.astype(BF16), w2_ref[c * FF_CHUNK:(c + 1) * FF_CHUNK, :],
                                preferred_element_type=F32)
        y_ref[...] = _pack_rows(acc)


def _experts(tile_expert, n_valid, xs, w1, b1, w2, b2):
    n_rows = xs.shape[0]
    tm = EXPERT_TM
    n_tiles = n_rows // tm
    row = lambda i, te, nv: (jnp.minimum(i, nv[0] - 1), 0)
    grid_spec = pltpu.PrefetchScalarGridSpec(
        num_scalar_prefetch=2,
        grid=(n_tiles,),
        in_specs=[
            pl.BlockSpec((tm, HALF), row),
            pl.BlockSpec((None, D_MODEL, 2 * D_FF), lambda i, te, nv: (te[i], 0, 0)),
            pl.BlockSpec((None, 1, 2 * D_FF), lambda i, te, nv: (te[i], 0, 0)),
            pl.BlockSpec((None, D_FF, D_MODEL), lambda i, te, nv: (te[i], 0, 0)),
            pl.BlockSpec((None, 1, D_MODEL), lambda i, te, nv: (te[i], 0, 0)),
        ],
        out_specs=pl.BlockSpec((tm, HALF), row),
        scratch_shapes=[pltpu.VMEM((D_MODEL, 2 * D_FF), BF16), pltpu.VMEM((D_FF, D_MODEL), BF16)],
    )
    return pl.pallas_call(
        _expert_body,
        grid_spec=grid_spec,
        out_shape=jax.ShapeDtypeStruct((n_rows, HALF), I32),
        compiler_params=_cparams(("arbitrary",)),
        name="experts",
    )(tile_expert, n_valid, xs, w1, b1, w2, b2)


COMBINE_TM = 512


def _combine_body(y0_ref, y1_ref, y2_ref, y3_ref, rw_ref, x_ref, g_ref, *rest):
    o_ref = rest[-1]
    acc = x_ref[...]
    rw = jnp.concatenate([rw_ref[...], jnp.zeros((LANES - 2 * TOP_K, COMBINE_TM), F32)], axis=0).T
    for k, y_ref in enumerate((y0_ref, y1_ref, y2_ref, y3_ref)):
        lo, hi = _unpack_rows(y_ref[...])
        acc = acc + rw[:, k:k + 1] * jnp.concatenate([lo, hi], axis=-1)
    o_ref[...] = _rms(acc, g_ref[...])


def _combine(yg, rw, x2, g, part, out_prev):
    T = x2.shape[0]
    tm = COMBINE_TM
    nt = T // tm
    in_specs = [
        pl.BlockSpec((tm, HALF), lambda i: (i, 0)),
        pl.BlockSpec((tm, HALF), lambda i: (nt + i, 0)),
        pl.BlockSpec((tm, HALF), lambda i: (2 * nt + i, 0)),
        pl.BlockSpec((tm, HALF), lambda i: (3 * nt + i, 0)),
        pl.BlockSpec((2 * TOP_K, tm), lambda i: (0, i)),
        pl.BlockSpec((tm, D_MODEL), lambda i: (i, 0)),
        pl.BlockSpec((1, D_MODEL), lambda i: (0, 0)),
    ]
    args = [yg, yg, yg, yg, rw, x2, g]
    aliases = {}
    if out_prev is not None:
        in_specs.append(pl.BlockSpec(memory_space=pl.ANY))
        args.append(out_prev)
        aliases = {len(args) - 1: 0}
    return pl.pallas_call(
        _combine_body,
        grid=(nt,),
        in_specs=in_specs,
        out_specs=pl.BlockSpec((tm, D_MODEL), lambda i: (part * nt + i, 0)),
        out_shape=jax.ShapeDtypeStruct((T * MOE_PARTS, D_MODEL), F32),
        input_output_aliases=aliases,
        compiler_params=_cparams(("parallel",)),
        name="combine",
    )(*args)


def _pad_lanes(v, fill=0.0):
    v = v.reshape(1, -1).astype(F32)
    return jnp.pad(v, ((0, 0), (0, LANES - v.shape[1])), constant_values=fill)


def _layer(x2d, mem2d, B, S, M, norm_mix, w_in, b_ml_gates, conv_ml, ml_head_norm, b_fx_gate, norm_mem,
           w_mem_kv, w_branch, w_out, norm_moe, w_router, b_router, w_exp_in, b_exp_in, w_exp_out,
           b_exp_out, norm_out):
    T = B * S
    w16 = w_in.astype(BF16)
    w_big = jnp.concatenate([w16[:, 0:2048], w16[:, 2056:3080], w16[:, 3080:6152], w16[:, 6160:7184],
                             w16[:, 7184:10256]], axis=1)
    w_small = jnp.concatenate([w16[:, 2048:2056], w16[:, 6152:6160]], axis=1)
    w_small = jnp.pad(w_small, ((0, 0), (0, LANES - w_small.shape[1])))
    row = lambda v: v.reshape(1, -1).astype(F32)

    proj, small = _inproj(x2d, row(norm_mix), w_big, w_small)

    y_ml = _mlstm(proj, small, conv_ml.astype(F32), _pad_lanes(b_ml_gates), row(ml_head_norm), B, S)

    b_fx = jnp.pad(b_fx_gate.reshape(1, -1).astype(F32), ((0, 0), (2 * ML_HEADS, LANES - 2 * ML_HEADS - FX_HEADS)))
    y_fx = _fox_attn(proj, _fox_gate(small, b_fx, B, S), B, S)

    kv = _memkv(mem2d, row(norm_mem), w_mem_kv.astype(BF16))
    y_ca = _memattn(proj, kv, B, S, M)

    w_r = w_router.T.astype(BF16)
    moe_weights = (w_exp_in.astype(F32), b_exp_in.reshape(N_EXPERTS, 1, -1).astype(F32), w_exp_out.astype(F32),
                   b_exp_out.reshape(N_EXPERTS, 1, -1).astype(F32))
    routed = [_merge(y_ml, y_fx, y_ca, proj, x2d, w_branch.astype(BF16), w_out.astype(BF16), row(norm_moe), w_r,
                     b_router.reshape(N_EXPERTS, 1).astype(F32), part) for part in range(MOE_PARTS)]
    out = None
    for part, (x2, hp, ri, rw, cnt) in enumerate(routed):
        yg = _moe_rows(hp, ri, cnt, moe_weights)
        out = _combine(yg, rw, x2, row(norm_out), part, out)
    return out


def _moe_rows(hp, ri, cnt, moe_weights):
    T = hp.shape[0]
    tm = EXPERT_TM
    n_tiles = (T * TOP_K) // tm + N_EXPERTS
    counts = cnt[:, 0].astype(I32)
    padded = ((counts + tm - 1) // tm) * tm
    gend = jnp.cumsum(padded)
    gstart = gend - padded
    expert_ids = jnp.arange(N_EXPERTS, dtype=I32)
    start_of = jnp.sum(jnp.where(ri[0:TOP_K, :, None] == expert_ids, gstart, 0), axis=-1)
    dest = start_of + ri[TOP_K:2 * TOP_K, :]
    n_valid = gend[-1] // tm
    tile_ids = jnp.arange(n_tiles, dtype=I32)
    last_tile = jnp.minimum(tile_ids, n_valid - 1)
    tile_e = jnp.minimum(jnp.sum((gend[None, :] <= last_tile[:, None] * tm).astype(I32), axis=1), N_EXPERTS - 1)

    slot = jnp.arange(tm, dtype=I32)
    spare = n_tiles * tm + slot % SC_CHUNK
    pad_idx = jnp.where(slot[None, :] < (padded - counts)[:, None], (gstart + counts)[:, None] + slot[None, :],
                        spare[None, :]).reshape(-1)

    xs = _sc_dispatch(hp, dest, pad_idx, n_tiles * tm + SC_CHUNK)
    ys = _experts(tile_e.astype(I32), n_valid.reshape(1).astype(I32), xs, *moe_weights)
    return _sc_gather(ys, dest.reshape(-1))


def kernel(x, mem, norm_mix, w_in, b_ml_gates, conv_ml, ml_head_norm, b_fx_gate, norm_mem, w_mem_kv, w_branch,
           w_out, norm_moe, w_router, b_router, w_exp_in, b_exp_in, w_exp_out, b_exp_out, norm_final):
    B, S, D = x.shape
    M = mem.shape[1]
    depth = norm_mix.shape[0]
    assert depth == 1, "the combine kernel fuses the final norm, so exactly one layer is supported"
    assert D == D_MODEL and S % ML_BLOCK == 0 and S % FX_T == 0 and S % CA_TQ == 0
    out = _layer(x.reshape(B * S, D), mem.reshape(B * M, D), B, S, M, norm_mix[0], w_in[0], b_ml_gates[0],
                 conv_ml[0], ml_head_norm[0], b_fx_gate[0], norm_mem[0], w_mem_kv[0], w_branch[0], w_out[0],
                 norm_moe[0], w_router[0], b_router[0], w_exp_in[0], b_exp_in[0], w_exp_out[0], b_exp_out[0],
                 norm_final)
    return out.reshape(B, S, D)
```

```python
import functools

import jax
import jax.numpy as jnp
from jax import lax
from jax.experimental import pallas as pl
from jax.experimental.pallas import tpu as pltpu
from jax.experimental.pallas import tpu_sc as plsc

F32 = jnp.float32
BF16 = jnp.bfloat16
I32 = jnp.int32

D_MODEL = 1024
N_MEM_HEADS = 4
ML_HEADS = 4
ML_DQK = 128
ML_DV = 256
ML_CONV = 4
FX_HEADS = 8
FX_DH = 128
CA_HEADS = 4
CA_DH = 256
N_EXPERTS = 32
TOP_K = 4
D_FF = D_MODEL
SWIGLU_LIMIT = 7.0
SWIGLU_ALPHA = 1.702
EPS = 1e-5
LANES = 128
HALF = D_MODEL // 2
HI_MASK = -65536

COL_MLQK, COL_MLV, COL_MLO, COL_FXQ, COL_FXK, COL_FXV, COL_CAQ, COL_GATE0 = 0, 1, 2, 3, 4, 5, 6, 7
N_BIG = 10 * D_MODEL

VMEM_LIMIT = 56 * 1024 * 1024


def _cparams(sem):
    return pltpu.CompilerParams(dimension_semantics=sem, vmem_limit_bytes=VMEM_LIMIT)


def _rms(x, g):
    return x * lax.rsqrt(jnp.mean(x * x, axis=-1, keepdims=True) + EPS) * g


def _log_sigmoid(x):
    return jnp.minimum(x, 0.0) - jnp.log1p(jnp.exp(-jnp.abs(x)))


def _pack_rows(y):
    bits = lax.bitcast_convert_type(y.astype(BF16).astype(F32), I32)
    return lax.shift_right_logical(bits[:, :HALF], 16) | (bits[:, HALF:] & HI_MASK)


def _unpack_rows(w):
    lo = lax.bitcast_convert_type(lax.shift_left(w, 16), F32)
    hi = lax.bitcast_convert_type(w & HI_MASK, F32)
    return lo, hi


def _inproj_body(x_ref, g_ref, w_ref, ws_ref, o_ref, os_ref, h_ref):
    @pl.when(pl.program_id(1) == 0)
    def _():
        hb = _rms(x_ref[...], g_ref[...]).astype(BF16)
        h_ref[...] = hb
        os_ref[...] = jnp.dot(hb, ws_ref[...], preferred_element_type=F32)

    o_ref[...] = jnp.dot(h_ref[...], w_ref[...], preferred_element_type=F32).astype(BF16)


def _inproj(x2d, g, w_big, w_small):
    T = x2d.shape[0]
    tm = min(1024, T)
    tn = 2048
    return pl.pallas_call(
        _inproj_body,
        grid=(T // tm, N_BIG // tn),
        in_specs=[
            pl.BlockSpec((tm, D_MODEL), lambda i, j: (i, 0)),
            pl.BlockSpec((1, D_MODEL), lambda i, j: (0, 0)),
            pl.BlockSpec((D_MODEL, tn), lambda i, j: (0, j)),
            pl.BlockSpec((D_MODEL, LANES), lambda i, j: (0, 0)),
        ],
        out_specs=[
            pl.BlockSpec((tm, tn), lambda i, j: (i, j)),
            pl.BlockSpec((tm, LANES), lambda i, j: (i, 0)),
        ],
        out_shape=[
            jax.ShapeDtypeStruct((T, N_BIG), BF16),
            jax.ShapeDtypeStruct((T, LANES), F32),
        ],
        scratch_shapes=[pltpu.VMEM((tm, D_MODEL), BF16)],
        compiler_params=_cparams(("parallel", "arbitrary")),
        name="inproj",
    )(x2d, g, w_big, w_small)


ML_BLOCK = 512
ML_MB = 1
ML_CHUNK = 128
CONV_PAD = 8


def _mlstm_body(qk_ref, v_ref, o_ref, g_ref, cw_ref, bg_ref, hn_ref, y_ref, xbuf, c_st, n_st, m_st):
    L = ML_CHUNK

    @pl.when(pl.program_id(1) == 0)
    def _():
        xbuf[:, 0:CONV_PAD, :] = jnp.zeros((ML_MB, CONV_PAD, D_MODEL), F32)
        c_st[...] = jnp.zeros_like(c_st)
        n_st[...] = jnp.zeros_like(n_st)
        m_st[...] = jnp.zeros_like(m_st)

    for bb in range(ML_MB):
        xbuf[bb, CONV_PAD:CONV_PAD + ML_BLOCK, :] = qk_ref[bb].astype(F32)
    cw = cw_ref[...]
    row = lax.broadcasted_iota(I32, (L, L), 0)
    col = lax.broadcasted_iota(I32, (L, L), 1)
    tri = (row >= col).astype(BF16)
    causal_t = col >= row
    bg = bg_ref[...]
    scale = ML_DQK ** -0.5
    nt_dims = (((1,), (1,)), ((), ()))

    def chunk(bb, c):
        r0 = c * L
        conv = cw[0:1, :] * xbuf[bb, r0 + CONV_PAD - 3:r0 + CONV_PAD - 3 + L, :]
        for j in range(1, ML_CONV):
            s0 = r0 + CONV_PAD - 3 + j
            conv = conv + cw[j:j + 1, :] * xbuf[bb, s0:s0 + L, :]
        act = conv * jax.nn.sigmoid(conv)

        gates = g_ref[bb, r0:r0 + L, :] + bg
        lf = _log_sigmoid(gates)
        cum = jnp.zeros((L, LANES), F32)
        for _ in range(3):
            piece = lf.astype(BF16)
            cum = cum + jnp.dot(tri, piece, preferred_element_type=F32)
            lf = lf - piece.astype(F32)
        gates_t = gates.T
        cum_t = cum.T
        for h in range(ML_HEADS):
            b_row = cum_t[ML_HEADS + h:ML_HEADS + h + 1, :]
            i_row = gates_t[h:h + 1, :]
            a_col = gates[:, h:h + 1] - cum[:, ML_HEADS + h:ML_HEADS + h + 1]
            st = bb * ML_HEADS + h
            m_prev = m_st[st]
            dm = jnp.where(causal_t, a_col + b_row, -jnp.inf)
            m_inter = b_row + m_prev
            m_t = jnp.maximum(jnp.max(dm, axis=0, keepdims=True), m_inter)
            w_intra = jnp.exp(dm - m_t)
            w_inter = jnp.exp(m_inter - m_t)

            qb = (act[:, h * ML_DQK:(h + 1) * ML_DQK] * scale).astype(BF16)
            kb = act[:, (ML_HEADS + h) * ML_DQK:(ML_HEADS + h + 1) * ML_DQK].astype(BF16)
            v_t = v_ref[bb, r0:r0 + L, h * ML_DV:(h + 1) * ML_DV].astype(F32).T
            p_t = lax.dot_general(kb, qb, nt_dims, preferred_element_type=F32) * w_intra
            c_old = c_st[st]
            n_old = n_st[st]
            num = jnp.dot(v_t.astype(BF16), p_t.astype(BF16), preferred_element_type=F32) + w_inter * (
                lax.dot_general(c_old.astype(BF16), qb, nt_dims, preferred_element_type=F32))
            qn = lax.dot_general(jnp.broadcast_to(n_old, (8, ML_DQK)).astype(BF16), qb, nt_dims,
                                 preferred_element_type=F32)[0:1, :]
            den = jnp.sum(p_t, axis=0, keepdims=True) + w_inter * qn
            hv = num / jnp.maximum(jnp.abs(den), jnp.exp(-m_t))

            m_new = m_t[:, L - 1:L]
            b_last = b_row[:, L - 1:L]
            wk = jnp.exp(b_last - b_row + i_row - m_new)
            decay = jnp.exp(b_last + m_prev - m_new)
            c_st[st] = decay * c_old + jnp.dot((v_t * wk).astype(BF16), kb, preferred_element_type=F32)
            n_st[st] = decay * n_old + jnp.dot(jnp.broadcast_to(wk, (8, L)).astype(BF16), kb,
                                               preferred_element_type=F32)[0:1, :]
            m_st[st] = m_new

            hn = (hv * lax.rsqrt(jnp.mean(hv * hv, axis=0, keepdims=True) + EPS)).T
            og = o_ref[bb, r0:r0 + L, h * ML_DV:(h + 1) * ML_DV].astype(F32)
            y_ref[bb, r0:r0 + L, h * ML_DV:(h + 1) * ML_DV] = (
                hn * hn_ref[:, h * ML_DV:(h + 1) * ML_DV] * jax.nn.sigmoid(og)).astype(BF16)

    for c in range(ML_BLOCK // L):
        for bb in range(ML_MB):
            chunk(bb, c)

    xbuf[:, 0:CONV_PAD, :] = xbuf[:, ML_BLOCK:ML_BLOCK + CONV_PAD, :]


def _mlstm(proj, small, conv_w, b_gates, head_norm, B, S):
    T = B * S
    ns = S // ML_BLOCK
    assert B % ML_MB == 0
    proj3 = proj.reshape(B, S, N_BIG)
    blk = lambda col: pl.BlockSpec((ML_MB, ML_BLOCK, D_MODEL), lambda b, s: (b, s, col))
    out = pl.pallas_call(
        _mlstm_body,
        grid=(B // ML_MB, ns),
        in_specs=[
            blk(COL_MLQK),
            blk(COL_MLV),
            blk(COL_MLO),
            pl.BlockSpec((ML_MB, ML_BLOCK, LANES), lambda b, s: (b, s, 0)),
            pl.BlockSpec((ML_CONV, D_MODEL), lambda b, s: (0, 0)),
            pl.BlockSpec((1, LANES), lambda b, s: (0, 0)),
            pl.BlockSpec((1, D_MODEL), lambda b, s: (0, 0)),
        ],
        out_specs=blk(0),
        out_shape=jax.ShapeDtypeStruct((B, S, D_MODEL), BF16),
        scratch_shapes=[
            pltpu.VMEM((ML_MB, ML_BLOCK + CONV_PAD, D_MODEL), F32),
            pltpu.VMEM((ML_MB * ML_HEADS, ML_DV, ML_DQK), F32),
            pltpu.VMEM((ML_MB * ML_HEADS, 1, ML_DQK), F32),
            pltpu.VMEM((ML_MB * ML_HEADS, 1, 1), F32),
        ],
        compiler_params=_cparams(("parallel", "arbitrary")),
        name="mlstm",
    )(proj3, proj3, proj3, small.reshape(B, S, LANES), conv_w, b_gates, head_norm)
    return out.reshape(T, D_MODEL)


FX_T = 512
FX_HP = 2
FX_VR = FX_DH + 16
LOG2E = 1.4426950408889634
N_PIECES = 3
FX_GATE_T = 128


def _fox_gate_body(g_ref, b_ref, o_ref):
    S = g_ref.shape[0]
    row = lax.broadcasted_iota(I32, (FX_GATE_T, FX_GATE_T), 0)
    col = lax.broadcasted_iota(I32, (FX_GATE_T, FX_GATE_T), 1)
    tri = (row >= col).astype(BF16)
    carry = jnp.zeros((1, LANES), F32)
    for blk in range(S // FX_GATE_T):
        rows = slice(blk * FX_GATE_T, (blk + 1) * FX_GATE_T)
        lf = _log_sigmoid(g_ref[rows, :] + b_ref[...])
        cum = carry
        for _ in range(N_PIECES):
            piece = lf.astype(BF16)
            cum = cum + jnp.dot(tri, piece, preferred_element_type=F32)
            lf = lf - piece.astype(F32)
        carry = cum[FX_GATE_T - 1:FX_GATE_T, :]
        o_ref[rows, :] = cum * (-LOG2E)


def _fox_gate(small, b_fx, B, S):
    return pl.pallas_call(
        _fox_gate_body,
        grid=(B,),
        in_specs=[
            pl.BlockSpec((S, LANES), lambda b: (b, 0)),
            pl.BlockSpec((1, LANES), lambda b: (0, 0)),
        ],
        out_specs=pl.BlockSpec((S, LANES), lambda b: (b, 0)),
        out_shape=jax.ShapeDtypeStruct((B * S, LANES), F32),
        compiler_params=_cparams(("parallel",)),
        name="fox_gate",
    )(small, b_fx)


def _fox_attn_body(q_ref, k_ref, v_ref, c_ref, o_ref, kx_ref, vt_ref, m_ref, acc_ref, s_ref):
    S = k_ref.shape[0]
    nq = S // FX_T

    c = c_ref[...]
    hi = c.astype(BF16)
    r1 = c - hi.astype(F32)
    mid = r1.astype(BF16)
    lo = (r1 - mid.astype(F32)).astype(BF16)
    sel_row = lax.broadcasted_iota(I32, (LANES, LANES), 0)
    sel_col = lax.broadcasted_iota(I32, (LANES, LANES), 1)
    ones_rows = (lax.broadcasted_iota(I32, (FX_VR - FX_DH, FX_T), 0) == 0).astype(BF16)
    head_slices = [slice(hh * FX_DH, (hh + 1) * FX_DH) for hh in range(FX_HP)]
    for hh, sl in enumerate(head_slices):
        lane = 2 * ML_HEADS + pl.program_id(1) * FX_HP + hh
        pieces = None
        for p, part in enumerate((hi, mid, lo)):
            pick = jnp.logical_and(sel_row == lane, sel_col == p).astype(BF16)
            t = jnp.dot(part, pick, preferred_element_type=F32)
            pieces = t if pieces is None else pieces + t
        kx_ref[hh, :, 0:FX_DH] = k_ref[:, sl]
        kx_ref[hh, :, FX_DH:2 * FX_DH] = pieces.astype(BF16)
        for j in range(nq):
            vt = v_ref[j * FX_T:(j + 1) * FX_T, sl].astype(F32).T.astype(BF16)
            vt_ref[hh, j] = jnp.concatenate([vt, ones_rows], axis=0)

    piece_rows = (lax.broadcasted_iota(I32, (FX_DH, FX_T), 0) < N_PIECES).astype(BF16)

    def start(i):
        q_x = []
        for sl in head_slices:
            q_t = (q_ref[i * FX_T:(i + 1) * FX_T, sl].astype(F32) * (FX_DH ** -0.5 * LOG2E)).T.astype(BF16)
            q_x.append(jnp.concatenate([q_t, piece_rows], axis=0))
        m_ref[i % 2] = jnp.full(m_ref.shape[1:], -jnp.inf, F32)
        acc_ref[i % 2] = jnp.zeros(acc_ref.shape[1:], F32)
        return q_x

    def key_rows(j):
        return pl.ds(j * FX_T, FX_T) if isinstance(j, int) else pl.ds(pl.multiple_of(j * FX_T, FX_T), FX_T)

    def scores(q_x, j, slot):
        for hh in range(FX_HP):
            s_ref[slot, hh] = jnp.dot(kx_ref[hh, key_rows(j), :], q_x[hh], preferred_element_type=F32)

    def consume(par, j, slot, masked):
        for hh in range(FX_HP):
            s = s_ref[slot, hh]
            if masked:
                key = lax.broadcasted_iota(I32, (FX_T, FX_T), 0)
                qry = lax.broadcasted_iota(I32, (FX_T, FX_T), 1)
                s = jnp.where(qry >= key, s, -jnp.inf)
            m_old = m_ref[par, hh]
            m_new = jnp.maximum(m_old, jnp.max(s, axis=0, keepdims=True))
            p = jnp.exp2(s - m_new).astype(BF16)
            acc_ref[par, hh] = jnp.exp2(m_old - m_new) * acc_ref[par, hh] + jnp.dot(
                vt_ref[hh, j], p, preferred_element_type=F32)
            m_ref[par, hh] = m_new

    def finish(i, slot):
        consume(i % 2, i, slot, True)
        for hh, sl in enumerate(head_slices):
            acc = acc_ref[i % 2, hh]
            o_ref[i * FX_T:(i + 1) * FX_T, sl] = (acc[0:FX_DH, :] / acc[FX_DH:FX_DH + 1, :]).T.astype(BF16)

    diag_slot = 0
    for i in range(nq):
        q_x = start(i)
        first = 0 if i == 0 else 1 - diag_slot
        scores(q_x, 0, first)
        if i > 0:
            finish(i - 1, diag_slot)

        def pair(jj, carry, q_x=q_x, first=first, par=i % 2):
            j = 2 * jj
            scores(q_x, j + 1, 1 - first)
            consume(par, j, first, False)
            scores(q_x, j + 2, first)
            consume(par, j + 1, 1 - first, False)
            return carry

        if i >= 2:
            lax.fori_loop(0, i // 2, pair, 0)
        if i % 2 == 1:
            scores(q_x, i, 1 - first)
            consume(i % 2, i - 1, first, False)
            diag_slot = 1 - first
        else:
            diag_slot = first
    finish(nq - 1, diag_slot)


def _fox_attn(proj, c_neg, B, S):
    T = B * S
    nq = S // FX_T
    wide = FX_HP * FX_DH
    cq = COL_FXQ * (D_MODEL // wide)
    ck = COL_FXK * (D_MODEL // wide)
    cv = COL_FXV * (D_MODEL // wide)
    proj3 = proj.reshape(B, S, N_BIG)
    out = pl.pallas_call(
        _fox_attn_body,
        grid=(B, FX_HEADS // FX_HP),
        in_specs=[
            pl.BlockSpec((None, S, wide), lambda b, h: (b, 0, cq + h)),
            pl.BlockSpec((None, S, wide), lambda b, h: (b, 0, ck + h)),
            pl.BlockSpec((None, S, wide), lambda b, h: (b, 0, cv + h)),
            pl.BlockSpec((None, S, LANES), lambda b, h: (b, 0, 0)),
        ],
        out_specs=pl.BlockSpec((None, S, wide), lambda b, h: (b, 0, h)),
        out_shape=jax.ShapeDtypeStruct((B, S, D_MODEL), BF16),
        scratch_shapes=[
            pltpu.VMEM((FX_HP, S, 2 * FX_DH), BF16),
            pltpu.VMEM((FX_HP, nq, FX_VR, FX_T), BF16),
            pltpu.VMEM((2, FX_HP, 1, FX_T), F32),
            pltpu.VMEM((2, FX_HP, FX_VR, FX_T), F32),
            pltpu.VMEM((2, FX_HP, FX_T, FX_T), F32),
        ],
        compiler_params=_cparams(("parallel", "parallel")),
        name="fox_attn",
    )(proj3, proj3, proj3, c_neg.reshape(B, S, LANES))
    return out.reshape(T, D_MODEL)


def _memkv_body(x_ref, g_ref, w_ref, o_ref):
    hb = _rms(x_ref[...], g_ref[...]).astype(BF16)
    o_ref[...] = jnp.dot(hb, w_ref[...], preferred_element_type=F32).astype(BF16)


def _memkv(mem2d, g, w_kv):
    R = mem2d.shape[0]
    tm = min(512, R)
    N = w_kv.shape[1]
    return pl.pallas_call(
        _memkv_body,
        grid=(R // tm,),
        in_specs=[
            pl.BlockSpec((tm, D_MODEL), lambda i: (i, 0)),
            pl.BlockSpec((1, D_MODEL), lambda i: (0, 0)),
            pl.BlockSpec((D_MODEL, N), lambda i: (0, 0)),
        ],
        out_specs=pl.BlockSpec((tm, N), lambda i: (i, 0)),
        out_shape=jax.ShapeDtypeStruct((R, N), BF16),
        compiler_params=_cparams(("parallel",)),
        name="memkv",
    )(mem2d, g, w_kv)


CA_TQ = 512


def _memattn_body(q_ref, k_ref, v_ref, o_ref):
    scale = CA_DH ** -0.5
    for h in range(CA_HEADS):
        sl = slice(h * CA_DH, (h + 1) * CA_DH)
        s = lax.dot_general(q_ref[:, sl], k_ref[:, sl], (((1,), (1,)), ((), ())),
                            preferred_element_type=F32) * scale
        p = jnp.exp(s - jnp.max(s, axis=-1, keepdims=True))
        l = jnp.sum(p, axis=-1, keepdims=True)
        o = jnp.dot(p.astype(BF16), v_ref[:, sl], preferred_element_type=F32) / l
        o_ref[:, sl] = o.astype(BF16)


def _memattn(proj, kv, B, S, M):
    T = B * S
    nq = S // CA_TQ
    kv3 = kv.reshape(B, M, 2 * D_MODEL)
    return pl.pallas_call(
        _memattn_body,
        grid=(B, nq),
        in_specs=[
            pl.BlockSpec((CA_TQ, D_MODEL), lambda b, i: (b * nq + i, COL_CAQ)),
            pl.BlockSpec((None, M, D_MODEL), lambda b, i: (b, 0, 0)),
            pl.BlockSpec((None, M, D_MODEL), lambda b, i: (b, 0, 1)),
        ],
        out_specs=pl.BlockSpec((CA_TQ, D_MODEL), lambda b, i: (b * nq + i, 0)),
        out_shape=jax.ShapeDtypeStruct((T, D_MODEL), BF16),
        compiler_params=_cparams(("parallel", "arbitrary")),
        name="memattn",
    )(proj, kv3, kv3)


MERGE_TM = 512
MOE_PARTS = 2


def _merge_body(y0_ref, y1_ref, y2_ref, g0_ref, g1_ref, g2_ref, x_ref, wb_ref, wo_ref, gn_ref, wr_ref, br_ref,
                *rest):
    o_ref, hp_ref, ri_ref, rw_ref, cnt_ref, carry_ref = rest[-6:]
    merged = None
    for n, (y_ref, g_ref) in enumerate(((y0_ref, g0_ref), (y1_ref, g1_ref), (y2_ref, g2_ref))):
        p = jnp.dot(y_ref[...], wb_ref[n], preferred_element_type=F32)
        t = jax.nn.sigmoid(g_ref[...].astype(F32)) * p
        merged = t if merged is None else merged + t
    x2 = x_ref[...] + jnp.dot(merged.astype(BF16), wo_ref[...], preferred_element_type=F32)
    o_ref[...] = x2
    _route(x2, gn_ref, wr_ref, br_ref, hp_ref, ri_ref, rw_ref, cnt_ref, carry_ref)


def _merge(y_ml, y_fx, y_ca, proj, x2d, w_branch, w_out, g_moe, w_router, b_router, part, after):
    T = x2d.shape[0] // MOE_PARTS
    tm = MERGE_TM
    off = part * (T // tm)
    src = lambda i: (off + i, 0)
    row = lambda i: (i, 0)
    const = lambda i: (0, 0)
    extra_specs, extra_args = ([], []) if after is None else ([pl.BlockSpec(memory_space=pl.ANY)], [after])
    return pl.pallas_call(
        _merge_body,
        grid=(T // tm,),
        in_specs=[
            pl.BlockSpec((tm, D_MODEL), src),
            pl.BlockSpec((tm, D_MODEL), src),
            pl.BlockSpec((tm, D_MODEL), src),
            pl.BlockSpec((tm, D_MODEL), lambda i: (off + i, COL_GATE0)),
            pl.BlockSpec((tm, D_MODEL), lambda i: (off + i, COL_GATE0 + 1)),
            pl.BlockSpec((tm, D_MODEL), lambda i: (off + i, COL_GATE0 + 2)),
            pl.BlockSpec((tm, D_MODEL), src),
            pl.BlockSpec((3, D_MODEL, D_MODEL), lambda i: (0, 0, 0)),
            pl.BlockSpec((D_MODEL, D_MODEL), const),
            pl.BlockSpec((1, D_MODEL), const),
            pl.BlockSpec((N_EXPERTS, D_MODEL), const),
            pl.BlockSpec((N_EXPERTS, 1), const),
        ] + extra_specs,
        out_specs=[
            pl.BlockSpec((tm, D_MODEL), row),
            pl.BlockSpec((tm, HALF), row),
            pl.BlockSpec((2 * TOP_K, tm), lambda i: (0, i)),
            pl.BlockSpec((2 * TOP_K, tm), lambda i: (0, i)),
            pl.BlockSpec((N_EXPERTS, 1), const),
        ],
        out_shape=[
            jax.ShapeDtypeStruct((T, D_MODEL), F32),
            jax.ShapeDtypeStruct((T, HALF), I32),
            jax.ShapeDtypeStruct((2 * TOP_K, T), I32),
            jax.ShapeDtypeStruct((2 * TOP_K, T), F32),
            jax.ShapeDtypeStruct((N_EXPERTS, 1), F32),
        ],
        scratch_shapes=[pltpu.VMEM((N_EXPERTS, 1), F32)],
        compiler_params=_cparams(("arbitrary",)),
        name="merge_router",
    )(y_ml, y_fx, y_ca, proj, proj, proj, x2d, w_branch, w_out, g_moe, w_router, b_router, *extra_args)


def _route(x2, g_ref, wr_ref, br_ref, hp_ref, ri_ref, rw_ref, cnt_ref, carry_ref):
    tm = MERGE_TM

    @pl.when(pl.program_id(0) == 0)
    def _():
        carry_ref[...] = jnp.zeros_like(carry_ref)

    h = _rms(x2, g_ref[...])
    hp_ref[...] = _pack_rows(h)
    logits = lax.dot_general(wr_ref[...], h.astype(BF16), (((1,), (1,)), ((), ())),
                             preferred_element_type=F32) + br_ref[...]
    eid = lax.broadcasted_iota(I32, (N_EXPERTS, tm), 0).astype(F32)

    work = logits
    onehot_sum = jnp.zeros((N_EXPERTS, tm), F32)
    vals, sels, idxs = [], [], []
    for _ in range(TOP_K):
        mx = jnp.max(work, axis=0, keepdims=True)
        idx = jnp.min(jnp.where(work == mx, eid, float(N_EXPERTS)), axis=0, keepdims=True)
        sel = eid == idx
        onehot_sum = onehot_sum + sel.astype(F32)
        work = jnp.where(sel, -jnp.inf, work)
        vals.append(mx)
        sels.append(sel)
        idxs.append(idx)
    exps = [jnp.exp(v - vals[0]) for v in vals]
    total = exps[0] + exps[1] + exps[2] + exps[3]

    earlier = (lax.broadcasted_iota(I32, (tm, tm), 0) < lax.broadcasted_iota(I32, (tm, tm), 1)).astype(BF16)
    before = jnp.dot(onehot_sum.astype(BF16), earlier, preferred_element_type=F32) + carry_ref[...]
    carry_ref[...] = carry_ref[...] + jnp.sum(onehot_sum, axis=1, keepdims=True)
    cnt_ref[...] = carry_ref[...]

    out_row = lax.broadcasted_iota(I32, (2 * TOP_K, tm), 0)
    ri = jnp.zeros((2 * TOP_K, tm), I32)
    rw = jnp.zeros((2 * TOP_K, tm), F32)
    for k in range(TOP_K):
        rank = jnp.sum(jnp.where(sels[k], before, 0.0), axis=0, keepdims=True)
        ri = jnp.where(out_row == k, idxs[k].astype(I32), ri)
        ri = jnp.where(out_row == TOP_K + k, rank.astype(I32), ri)
        rw = jnp.where(out_row == k, exps[k] / total, rw)
    ri_ref[...] = ri
    rw_ref[...] = rw


EXPERT_TM = 512
SC_CORES = 2
SC_SUBCORES = 16
SC_WORKERS = SC_CORES * SC_SUBCORES
SC_CHUNK = 64
PAD_SLOTS = N_EXPERTS * EXPERT_TM


def _sc_mesh():
    return plsc.VectorSubcoreMesh(core_axis_name="c", subcore_axis_name="s")


def _sc_worker():
    return lax.axis_index("s") * SC_CORES + lax.axis_index("c")


def _sc_dispatch(hp, dest, pad_idx, n_rows):
    T = hp.shape[0]
    per_w = T // SC_WORKERS
    n_ch = per_w // SC_CHUNK
    n_pc = PAD_SLOTS // (SC_WORKERS * SC_CHUNK)
    assert per_w % SC_CHUNK == 0 and n_ch >= 2 and n_ch % 2 == 0
    idx = dest.reshape(TOP_K, SC_WORKERS, n_ch, SC_CHUNK).transpose(1, 2, 0, 3)
    idx = idx.reshape(SC_WORKERS, n_ch * TOP_K, SC_CHUNK)
    pidx = pad_idx.reshape(SC_WORKERS, n_pc, SC_CHUNK)
    zeros = jnp.zeros((SC_CHUNK, HALF), I32)

    @functools.partial(
        pl.kernel, mesh=_sc_mesh(),
        out_type=jax.ShapeDtypeStruct((n_rows, HALF), I32),
        scratch_types=[
            pltpu.VMEM((n_ch * TOP_K, SC_CHUNK), I32),
            pltpu.VMEM((n_pc, SC_CHUNK), I32),
            pltpu.VMEM((2, SC_CHUNK, HALF), I32),
            pltpu.SemaphoreType.DMA((2,)),
            pltpu.SemaphoreType.DMA((2,)),
        ],
        name="sc_dispatch",
    )
    def k(hp_hbm, idx_hbm, pidx_hbm, zeros_hbm, xs_hbm, idx_v, pidx_v, rows_v, lsem, ssem):
        wid = _sc_worker()
        base = wid * per_w
        pltpu.sync_copy(idx_hbm.at[wid], idx_v)
        pltpu.sync_copy(pidx_hbm.at[wid], pidx_v)

        pltpu.sync_copy(zeros_hbm, rows_v.at[0])
        for p in range(n_pc):
            pltpu.make_async_copy(rows_v.at[0], xs_hbm.at[pidx_v.at[p]], ssem.at[0]).start()
        for p in range(n_pc):
            pltpu.make_async_copy(rows_v.at[0], xs_hbm.at[pidx_v.at[p]], ssem.at[0]).wait()

        def load(i, slot):
            return pltpu.make_async_copy(hp_hbm.at[pl.ds(base + i * SC_CHUNK, SC_CHUNK)], rows_v.at[slot],
                                         lsem.at[slot])

        def scatter(i, kk, slot):
            return pltpu.make_async_copy(rows_v.at[slot], xs_hbm.at[idx_v.at[i * TOP_K + kk]], ssem.at[slot])

        load(0, 0).start()

        def body(i2, carry):
            for slot in range(2):
                i = i2 * 2 + slot
                nxt = 1 - slot

                @pl.when(i + 1 < n_ch)
                def _():
                    @pl.when(i >= 1)
                    def _():
                        for kk in range(TOP_K):
                            scatter(i - 1, kk, nxt).wait()
                    load(i + 1, nxt).start()

                load(i, slot).wait()
                for kk in range(TOP_K):
                    scatter(i, kk, slot).start()
            return carry

        lax.fori_loop(0, n_ch // 2, body, 0)
        for kk in range(TOP_K):
            scatter(n_ch - 2, kk, 0).wait()
            scatter(n_ch - 1, kk, 1).wait()

    return k(hp, idx, pidx, zeros)


def _sc_gather(table, idx):
    n = idx.shape[0]
    per_w = n // SC_WORKERS
    n_ch = per_w // SC_CHUNK
    assert per_w % SC_CHUNK == 0 and n_ch >= 2 and n_ch % 2 == 0

    @functools.partial(
        pl.kernel, mesh=_sc_mesh(),
        out_type=jax.ShapeDtypeStruct((n, HALF), I32),
        scratch_types=[
            pltpu.VMEM((n_ch, SC_CHUNK), I32),
            pltpu.VMEM((2, SC_CHUNK, HALF), I32),
            pltpu.SemaphoreType.DMA((2,)),
            pltpu.SemaphoreType.DMA((2,)),
        ],
        name="sc_gather",
    )
    def k(table_hbm, idx_hbm, out_hbm, idx_v, rows_v, gsem, wsem):
        wid = _sc_worker()
        base = wid * per_w
        pltpu.sync_copy(idx_hbm.at[wid], idx_v)

        def gather(i, slot):
            return pltpu.make_async_copy(table_hbm.at[idx_v.at[i]], rows_v.at[slot], gsem.at[slot])

        def writeback(i, slot):
            return pltpu.make_async_copy(rows_v.at[slot], out_hbm.at[pl.ds(base + i * SC_CHUNK, SC_CHUNK)],
                                         wsem.at[slot])

        gather(0, 0).start()

        def body(i2, carry):
            for slot in range(2):
                i = i2 * 2 + slot
                nxt = 1 - slot

                @pl.when(i + 1 < n_ch)
                def _():
                    @pl.when(i >= 1)
                    def _():
                        writeback(i - 1, nxt).wait()
                    gather(i + 1, nxt).start()

                gather(i, slot).wait()
                writeback(i, slot).start()
            return carry

        lax.fori_loop(0, n_ch // 2, body, 0)
        writeback(n_ch - 2, 0).wait()
        writeback(n_ch - 1, 1).wait()

    return k(table, idx.reshape(SC_WORKERS, n_ch, SC_CHUNK))


FF_CHUNK = 512


def _expert_body(te_ref, nv_ref, x_ref, w1f_ref, b1_ref, w2f_ref, b2_ref, y_ref, w1_ref, w2_ref):
    i = pl.program_id(0)

    @pl.when(jnp.logical_or(i == 0, te_ref[i] != te_ref[jnp.maximum(i - 1, 0)]))
    def _():
        w1_ref[...] = w1f_ref[...].astype(BF16)
        w2_ref[...] = w2f_ref[...].astype(BF16)

    @pl.when(i < nv_ref[0])
    def _():
        lo, hi = _unpack_rows(x_ref[...])
        xlo = lo.astype(BF16)
        xhi = hi.astype(BF16)
        acc = jnp.zeros((EXPERT_TM, D_MODEL), F32) + b2_ref[...]
        for c in range(D_FF // FF_CHUNK):
            def up(off):
                cs = slice(off + c * FF_CHUNK, off + (c + 1) * FF_CHUNK)
                return (jnp.dot(xlo, w1_ref[0:HALF, cs], preferred_element_type=F32)
                        + jnp.dot(xhi, w1_ref[HALF:D_MODEL, cs], preferred_element_type=F32)
                        + b1_ref[:, cs])
            g = jnp.minimum(up(0), SWIGLU_LIMIT)
            lin = jnp.clip(up(D_FF), -SWIGLU_LIMIT, SWIGLU_LIMIT)
            a = g * jax.nn.sigmoid(SWIGLU_ALPHA * g) * (lin + 1.0)
            acc = acc + jnp.dot(a.astype(BF16), w2_ref[c * FF_CHUNK:(c + 1) * FF_CHUNK, :],
                                preferred_element_type=F32)
        y_ref[...] = _pack_rows(acc)


def _experts(tile_expert, n_valid, xs, w1, b1, w2, b2):
    n_rows = xs.shape[0]
    tm = EXPERT_TM
    n_tiles = n_rows // tm
    row = lambda i, te, nv: (jnp.minimum(i, nv[0] - 1), 0)
    grid_spec = pltpu.PrefetchScalarGridSpec(
        num_scalar_prefetch=2,
        grid=(n_tiles,),
        in_specs=[
            pl.BlockSpec((tm, HALF), row),
            pl.BlockSpec((None, D_MODEL, 2 * D_FF), lambda i, te, nv: (te[i], 0, 0)),
            pl.BlockSpec((None, 1, 2 * D_FF), lambda i, te, nv: (te[i], 0, 0)),
            pl.BlockSpec((None, D_FF, D_MODEL), lambda i, te, nv: (te[i], 0, 0)),
            pl.BlockSpec((None, 1, D_MODEL), lambda i, te, nv: (te[i], 0, 0)),
        ],
        out_specs=pl.BlockSpec((tm, HALF), row),
        scratch_shapes=[pltpu.VMEM((D_MODEL, 2 * D_FF), BF16), pltpu.VMEM((D_FF, D_MODEL), BF16)],
    )
    return pl.pallas_call(
        _expert_body,
        grid_spec=grid_spec,
        out_shape=jax.ShapeDtypeStruct((n_rows, HALF), I32),
        compiler_params=_cparams(("arbitrary",)),
        name="experts",
    )(tile_expert, n_valid, xs, w1, b1, w2, b2)


COMBINE_TM = 512


def _combine_body(y0_ref, y1_ref, y2_ref, y3_ref, rw_ref, x_ref, g_ref, *rest):
    o_ref = rest[-1]
    acc = x_ref[...]
    rw = jnp.concatenate([rw_ref[...], jnp.zeros((LANES - 2 * TOP_K, COMBINE_TM), F32)], axis=0).T
    for k, y_ref in enumerate((y0_ref, y1_ref, y2_ref, y3_ref)):
        lo, hi = _unpack_rows(y_ref[...])
        acc = acc + rw[:, k:k + 1] * jnp.concatenate([lo, hi], axis=-1)
    o_ref[...] = _rms(acc, g_ref[...])


def _combine(yg, rw, x2, g, part, out_prev):
    T = x2.shape[0]
    tm = COMBINE_TM
    nt = T // tm
    in_specs = [
        pl.BlockSpec((tm, HALF), lambda i: (i, 0)),
        pl.BlockSpec((tm, HALF), lambda i: (nt + i, 0)),
        pl.BlockSpec((tm, HALF), lambda i: (2 * nt + i, 0)),
        pl.BlockSpec((tm, HALF), lambda i: (3 * nt + i, 0)),
        pl.BlockSpec((2 * TOP_K, tm), lambda i: (0, i)),
        pl.BlockSpec((tm, D_MODEL), lambda i: (i, 0)),
        pl.BlockSpec((1, D_MODEL), lambda i: (0, 0)),
    ]
    args = [yg, yg, yg, yg, rw, x2, g]
    aliases = {}
    if out_prev is not None:
        in_specs.append(pl.BlockSpec(memory_space=pl.ANY))
        args.append(out_prev)
        aliases = {len(args) - 1: 0}
    return pl.pallas_call(
        _combine_body,
        grid=(nt,),
        in_specs=in_specs,
        out_specs=pl.BlockSpec((tm, D_MODEL), lambda i: (part * nt + i, 0)),
        out_shape=jax.ShapeDtypeStruct((T * MOE_PARTS, D_MODEL), F32),
        input_output_aliases=aliases,
        compiler_params=_cparams(("parallel",)),
        name="combine",
    )(*args)


def _pad_lanes(v):
    v = v.reshape(1, -1).astype(F32)
    return jnp.pad(v, ((0, 0), (0, LANES - v.shape[1])))


def _layer(x2d, mem2d, B, S, M, norm_mix, w_in, b_ml_gates, conv_ml, ml_head_norm, b_fx_gate, norm_mem,
           w_mem_kv, w_branch, w_out, norm_moe, w_router, b_router, w_exp_in, b_exp_in, w_exp_out,
           b_exp_out, norm_out):
    T = B * S
    w16 = w_in.astype(BF16)
    w_big = jnp.concatenate([w16[:, 0:2048], w16[:, 2056:3080], w16[:, 3080:6152], w16[:, 6160:7184],
                             w16[:, 7184:10256]], axis=1)
    w_small = jnp.concatenate([w16[:, 2048:2056], w16[:, 6152:6160]], axis=1)
    w_small = jnp.pad(w_small, ((0, 0), (0, LANES - w_small.shape[1])))
    row = lambda v: v.reshape(1, -1).astype(F32)

    proj, small = _inproj(x2d, row(norm_mix), w_big, w_small)

    y_ml = _mlstm(proj, small, conv_ml.astype(F32), _pad_lanes(b_ml_gates), row(ml_head_norm), B, S)

    b_fx = jnp.pad(b_fx_gate.reshape(1, -1).astype(F32), ((0, 0), (2 * ML_HEADS, LANES - 2 * ML_HEADS - FX_HEADS)))
    y_fx = _fox_attn(proj, _fox_gate(small, b_fx, B, S), B, S)

    kv = _memkv(mem2d, row(norm_mem), w_mem_kv.astype(BF16))
    y_ca = _memattn(proj, kv, B, S, M)

    w_r = w_router.T.astype(BF16)
    moe_weights = (w_exp_in.astype(F32), b_exp_in.reshape(N_EXPERTS, 1, -1).astype(F32), w_exp_out.astype(F32),
                   b_exp_out.reshape(N_EXPERTS, 1, -1).astype(F32))
    routed, plan = [], None
    for part in range(MOE_PARTS):
        x2, hp, ri, rw, cnt = _merge(y_ml, y_fx, y_ca, proj, x2d, w_branch.astype(BF16), w_out.astype(BF16),
                                     row(norm_moe), w_r, b_router.reshape(N_EXPERTS, 1).astype(F32), part,
                                     None if plan is None else plan[0])
        plan = _moe_plan(ri, cnt)
        routed.append((x2, hp, rw, plan))
    out = None
    for part, (x2, hp, rw, plan) in enumerate(routed):
        out = _combine(_moe_rows(hp, plan, moe_weights), rw, x2, row(norm_out), part, out)
    return out


def _moe_plan(ri, cnt):
    T = ri.shape[1]
    tm = EXPERT_TM
    n_tiles = (T * TOP_K) // tm + N_EXPERTS
    counts = cnt[:, 0].astype(I32)
    padded = ((counts + tm - 1) // tm) * tm
    gend = jnp.cumsum(padded)
    gstart = gend - padded
    expert_ids = jnp.arange(N_EXPERTS, dtype=I32)
    start_of = jnp.sum(jnp.where(ri[0:TOP_K, :, None] == expert_ids, gstart, 0), axis=-1)
    dest = start_of + ri[TOP_K:2 * TOP_K, :]
    n_valid = gend[-1] // tm
    tile_ids = jnp.arange(n_tiles, dtype=I32)
    last_tile = jnp.minimum(tile_ids, n_valid - 1)
    tile_e = jnp.minimum(jnp.sum((gend[None, :] <= last_tile[:, None] * tm).astype(I32), axis=1), N_EXPERTS - 1)

    slot = jnp.arange(tm, dtype=I32)
    spare = n_tiles * tm + slot % SC_CHUNK
    pad_idx = jnp.where(slot[None, :] < (padded - counts)[:, None], (gstart + counts)[:, None] + slot[None, :],
                        spare[None, :]).reshape(-1)

    return dest, pad_idx, tile_e.astype(I32), n_valid.reshape(1).astype(I32), n_tiles * tm + SC_CHUNK


def _moe_rows(hp, plan, moe_weights):
    dest, pad_idx, tile_e, n_valid, n_rows = plan
    xs = _sc_dispatch(hp, dest, pad_idx, n_rows)
    ys = _experts(tile_e, n_valid, xs, *moe_weights)
    return _sc_gather(ys, dest.reshape(-1))


def kernel(x, mem, norm_mix, w_in, b_ml_gates, conv_ml, ml_head_norm, b_fx_gate, norm_mem, w_mem_kv, w_branch,
           w_out, norm_moe, w_router, b_router, w_exp_in, b_exp_in, w_exp_out, b_exp_out, norm_final):
    B, S, D = x.shape
    M = mem.shape[1]
    depth = norm_mix.shape[0]
    assert depth == 1, "the combine kernel fuses the final norm, so exactly one layer is supported"
    assert D == D_MODEL and S % ML_BLOCK == 0 and S % FX_T == 0 and S % CA_TQ == 0
    out = _layer(x.reshape(B * S, D), mem.reshape(B * M, D), B, S, M, norm_mix[0], w_in[0], b_ml_gates[0],
                 conv_ml[0], ml_head_norm[0], b_fx_gate[0], norm_mem[0], w_mem_kv[0], w_branch[0], w_out[0],
                 norm_moe[0], w_router[0], b_router[0], w_exp_in[0], b_exp_in[0], w_exp_out[0], b_exp_out[0],
                 norm_final)
    return out.reshape(B, S, D)
```

```python
import functools

import jax
import jax.numpy as jnp
from jax import lax
from jax.experimental import pallas as pl
from jax.experimental.pallas import tpu as pltpu
from jax.experimental.pallas import tpu_sc as plsc

F32 = jnp.float32
BF16 = jnp.bfloat16
I32 = jnp.int32

D_MODEL = 1024
N_MEM_HEADS = 4
ML_HEADS = 4
ML_DQK = 128
ML_DV = 256
ML_CONV = 4
FX_HEADS = 8
FX_DH = 128
CA_HEADS = 4
CA_DH = 256
N_EXPERTS = 32
TOP_K = 4
D_FF = D_MODEL
SWIGLU_LIMIT = 7.0
SWIGLU_ALPHA = 1.702
EPS = 1e-5
LANES = 128
HALF = D_MODEL // 2
HI_MASK = -65536

COL_MLQK, COL_MLV, COL_MLO, COL_FXQ, COL_FXK, COL_FXV, COL_CAQ, COL_GATE0 = 0, 1, 2, 3, 4, 5, 6, 7
N_BIG = 10 * D_MODEL

VMEM_LIMIT = 56 * 1024 * 1024


def _cparams(sem):
    return pltpu.CompilerParams(dimension_semantics=sem, vmem_limit_bytes=VMEM_LIMIT)


def _rms(x, g):
    return x * lax.rsqrt(jnp.mean(x * x, axis=-1, keepdims=True) + EPS) * g


def _log_sigmoid(x):
    return jnp.minimum(x, 0.0) - jnp.log1p(jnp.exp(-jnp.abs(x)))


def _pack_rows(y):
    bits = lax.bitcast_convert_type(y.astype(BF16).astype(F32), I32)
    return lax.shift_right_logical(bits[:, :HALF], 16) | (bits[:, HALF:] & HI_MASK)


def _unpack_rows(w):
    lo = lax.bitcast_convert_type(lax.shift_left(w, 16), F32)
    hi = lax.bitcast_convert_type(w & HI_MASK, F32)
    return lo, hi


def _inproj_body(x_ref, g_ref, w_ref, ws_ref, o_ref, os_ref, h_ref):
    @pl.when(pl.program_id(1) == 0)
    def _():
        hb = _rms(x_ref[...], g_ref[...]).astype(BF16)
        h_ref[...] = hb
        os_ref[...] = jnp.dot(hb, ws_ref[...], preferred_element_type=F32)

    o_ref[...] = jnp.dot(h_ref[...], w_ref[...], preferred_element_type=F32).astype(BF16)


def _inproj(x2d, g, w_big, w_small):
    T = x2d.shape[0]
    tm = min(1024, T)
    tn = 2048
    return pl.pallas_call(
        _inproj_body,
        grid=(T // tm, N_BIG // tn),
        in_specs=[
            pl.BlockSpec((tm, D_MODEL), lambda i, j: (i, 0)),
            pl.BlockSpec((1, D_MODEL), lambda i, j: (0, 0)),
            pl.BlockSpec((D_MODEL, tn), lambda i, j: (0, j)),
            pl.BlockSpec((D_MODEL, LANES), lambda i, j: (0, 0)),
        ],
        out_specs=[
            pl.BlockSpec((tm, tn), lambda i, j: (i, j)),
            pl.BlockSpec((tm, LANES), lambda i, j: (i, 0)),
        ],
        out_shape=[
            jax.ShapeDtypeStruct((T, N_BIG), BF16),
            jax.ShapeDtypeStruct((T, LANES), F32),
        ],
        scratch_shapes=[pltpu.VMEM((tm, D_MODEL), BF16)],
        compiler_params=_cparams(("parallel", "arbitrary")),
        name="inproj",
    )(x2d, g, w_big, w_small)


ML_BLOCK = 512
ML_MB = 1
ML_CHUNK = 128
CONV_PAD = 8


def _mlstm_body(qk_ref, v_ref, o_ref, g_ref, cw_ref, bg_ref, hn_ref, y_ref, xbuf, c_st, n_st, m_st):
    L = ML_CHUNK

    @pl.when(pl.program_id(1) == 0)
    def _():
        xbuf[:, 0:CONV_PAD, :] = jnp.zeros((ML_MB, CONV_PAD, D_MODEL), F32)
        c_st[...] = jnp.zeros_like(c_st)
        n_st[...] = jnp.zeros_like(n_st)
        m_st[...] = jnp.zeros_like(m_st)

    for bb in range(ML_MB):
        xbuf[bb, CONV_PAD:CONV_PAD + ML_BLOCK, :] = qk_ref[bb].astype(F32)
    cw = cw_ref[...]
    row = lax.broadcasted_iota(I32, (L, L), 0)
    col = lax.broadcasted_iota(I32, (L, L), 1)
    tri = (row >= col).astype(BF16)
    causal_t = col >= row
    bg = bg_ref[...]
    scale = ML_DQK ** -0.5
    nt_dims = (((1,), (1,)), ((), ()))

    def chunk(bb, c):
        r0 = c * L
        conv = cw[0:1, :] * xbuf[bb, r0 + CONV_PAD - 3:r0 + CONV_PAD - 3 + L, :]
        for j in range(1, ML_CONV):
            s0 = r0 + CONV_PAD - 3 + j
            conv = conv + cw[j:j + 1, :] * xbuf[bb, s0:s0 + L, :]
        act = conv * jax.nn.sigmoid(conv)

        gates = g_ref[bb, r0:r0 + L, :] + bg
        lf = _log_sigmoid(gates)
        cum = jnp.zeros((L, LANES), F32)
        for _ in range(3):
            piece = lf.astype(BF16)
            cum = cum + jnp.dot(tri, piece, preferred_element_type=F32)
            lf = lf - piece.astype(F32)
        gates_t = gates.T
        cum_t = cum.T
        for h in range(ML_HEADS):
            b_row = cum_t[ML_HEADS + h:ML_HEADS + h + 1, :]
            i_row = gates_t[h:h + 1, :]
            a_col = gates[:, h:h + 1] - cum[:, ML_HEADS + h:ML_HEADS + h + 1]
            st = bb * ML_HEADS + h
            m_prev = m_st[st]
            dm = jnp.where(causal_t, a_col + b_row, -jnp.inf)
            m_inter = b_row + m_prev
            m_t = jnp.maximum(jnp.max(dm, axis=0, keepdims=True), m_inter)
            w_intra = jnp.exp(dm - m_t)
            w_inter = jnp.exp(m_inter - m_t)

            qb = (act[:, h * ML_DQK:(h + 1) * ML_DQK] * scale).astype(BF16)
            kb = act[:, (ML_HEADS + h) * ML_DQK:(ML_HEADS + h + 1) * ML_DQK].astype(BF16)
            v_t = v_ref[bb, r0:r0 + L, h * ML_DV:(h + 1) * ML_DV].astype(F32).T
            p_t = lax.dot_general(kb, qb, nt_dims, preferred_element_type=F32) * w_intra
            c_old = c_st[st]
            n_old = n_st[st]
            num = jnp.dot(v_t.astype(BF16), p_t.astype(BF16), preferred_element_type=F32) + w_inter * (
                lax.dot_general(c_old.astype(BF16), qb, nt_dims, preferred_element_type=F32))
            qn = lax.dot_general(jnp.broadcast_to(n_old, (8, ML_DQK)).astype(BF16), qb, nt_dims,
                                 preferred_element_type=F32)[0:1, :]
            den = jnp.sum(p_t, axis=0, keepdims=True) + w_inter * qn
            hv = num / jnp.maximum(jnp.abs(den), jnp.exp(-m_t))

            m_new = m_t[:, L - 1:L]
            b_last = b_row[:, L - 1:L]
            wk = jnp.exp(b_last - b_row + i_row - m_new)
            decay = jnp.exp(b_last + m_prev - m_new)
            c_st[st] = decay * c_old + jnp.dot((v_t * wk).astype(BF16), kb, preferred_element_type=F32)
            n_st[st] = decay * n_old + jnp.dot(jnp.broadcast_to(wk, (8, L)).astype(BF16), kb,
                                               preferred_element_type=F32)[0:1, :]
            m_st[st] = m_new

            hn = (hv * lax.rsqrt(jnp.mean(hv * hv, axis=0, keepdims=True) + EPS)).T
            og = o_ref[bb, r0:r0 + L, h * ML_DV:(h + 1) * ML_DV].astype(F32)
            y_ref[bb, r0:r0 + L, h * ML_DV:(h + 1) * ML_DV] = (
                hn * hn_ref[:, h * ML_DV:(h + 1) * ML_DV] * jax.nn.sigmoid(og)).astype(BF16)

    for c in range(ML_BLOCK // L):
        for bb in range(ML_MB):
            chunk(bb, c)

    xbuf[:, 0:CONV_PAD, :] = xbuf[:, ML_BLOCK:ML_BLOCK + CONV_PAD, :]


def _mlstm(proj, small, conv_w, b_gates, head_norm, B, S):
    T = B * S
    ns = S // ML_BLOCK
    assert B % ML_MB == 0
    proj3 = proj.reshape(B, S, N_BIG)
    blk = lambda col: pl.BlockSpec((ML_MB, ML_BLOCK, D_MODEL), lambda b, s: (b, s, col))
    out = pl.pallas_call(
        _mlstm_body,
        grid=(B // ML_MB, ns),
        in_specs=[
            blk(COL_MLQK),
            blk(COL_MLV),
            blk(COL_MLO),
            pl.BlockSpec((ML_MB, ML_BLOCK, LANES), lambda b, s: (b, s, 0)),
            pl.BlockSpec((ML_CONV, D_MODEL), lambda b, s: (0, 0)),
            pl.BlockSpec((1, LANES), lambda b, s: (0, 0)),
            pl.BlockSpec((1, D_MODEL), lambda b, s: (0, 0)),
        ],
        out_specs=blk(0),
        out_shape=jax.ShapeDtypeStruct((B, S, D_MODEL), BF16),
        scratch_shapes=[
            pltpu.VMEM((ML_MB, ML_BLOCK + CONV_PAD, D_MODEL), F32),
            pltpu.VMEM((ML_MB * ML_HEADS, ML_DV, ML_DQK), F32),
            pltpu.VMEM((ML_MB * ML_HEADS, 1, ML_DQK), F32),
            pltpu.VMEM((ML_MB * ML_HEADS, 1, 1), F32),
        ],
        compiler_params=_cparams(("parallel", "arbitrary")),
        name="mlstm",
    )(proj3, proj3, proj3, small.reshape(B, S, LANES), conv_w, b_gates, head_norm)
    return out.reshape(T, D_MODEL)


FX_T = 512
FX_HP = 2
FX_VR = FX_DH + 16
LOG2E = 1.4426950408889634
N_PIECES = 3
FX_GATE_T = 128


def _fox_gate_body(g_ref, b_ref, o_ref):
    S = g_ref.shape[0]
    row = lax.broadcasted_iota(I32, (FX_GATE_T, FX_GATE_T), 0)
    col = lax.broadcasted_iota(I32, (FX_GATE_T, FX_GATE_T), 1)
    tri = (row >= col).astype(BF16)
    carry = jnp.zeros((1, LANES), F32)
    for blk in range(S // FX_GATE_T):
        rows = slice(blk * FX_GATE_T, (blk + 1) * FX_GATE_T)
        lf = _log_sigmoid(g_ref[rows, :] + b_ref[...])
        cum = carry
        for _ in range(N_PIECES):
            piece = lf.astype(BF16)
            cum = cum + jnp.dot(tri, piece, preferred_element_type=F32)
            lf = lf - piece.astype(F32)
        carry = cum[FX_GATE_T - 1:FX_GATE_T, :]
        o_ref[rows, :] = cum * (-LOG2E)


def _fox_gate(small, b_fx, B, S):
    return pl.pallas_call(
        _fox_gate_body,
        grid=(B,),
        in_specs=[
            pl.BlockSpec((S, LANES), lambda b: (b, 0)),
            pl.BlockSpec((1, LANES), lambda b: (0, 0)),
        ],
        out_specs=pl.BlockSpec((S, LANES), lambda b: (b, 0)),
        out_shape=jax.ShapeDtypeStruct((B * S, LANES), F32),
        compiler_params=_cparams(("parallel",)),
        name="fox_gate",
    )(small, b_fx)


def _fox_attn_body(q_ref, k_ref, v_ref, c_ref, o_ref, kx_ref, vt_ref, m_ref, acc_ref, s_ref):
    S = k_ref.shape[0]
    nq = S // FX_T

    c = c_ref[...]
    hi = c.astype(BF16)
    r1 = c - hi.astype(F32)
    mid = r1.astype(BF16)
    lo = (r1 - mid.astype(F32)).astype(BF16)
    sel_row = lax.broadcasted_iota(I32, (LANES, LANES), 0)
    sel_col = lax.broadcasted_iota(I32, (LANES, LANES), 1)
    ones_rows = (lax.broadcasted_iota(I32, (FX_VR - FX_DH, FX_T), 0) == 0).astype(BF16)
    head_slices = [slice(hh * FX_DH, (hh + 1) * FX_DH) for hh in range(FX_HP)]
    for hh, sl in enumerate(head_slices):
        lane = 2 * ML_HEADS + pl.program_id(1) * FX_HP + hh
        pieces = None
        for p, part in enumerate((hi, mid, lo)):
            pick = jnp.logical_and(sel_row == lane, sel_col == p).astype(BF16)
            t = jnp.dot(part, pick, preferred_element_type=F32)
            pieces = t if pieces is None else pieces + t
        kx_ref[hh, :, 0:FX_DH] = k_ref[:, sl]
        kx_ref[hh, :, FX_DH:2 * FX_DH] = pieces.astype(BF16)
        for j in range(nq):
            vt = v_ref[j * FX_T:(j + 1) * FX_T, sl].astype(F32).T.astype(BF16)
            vt_ref[hh, j] = jnp.concatenate([vt, ones_rows], axis=0)

    piece_rows = (lax.broadcasted_iota(I32, (FX_DH, FX_T), 0) < N_PIECES).astype(BF16)

    def start(i):
        q_x = []
        for sl in head_slices:
            q_t = (q_ref[i * FX_T:(i + 1) * FX_T, sl].astype(F32) * (FX_DH ** -0.5 * LOG2E)).T.astype(BF16)
            q_x.append(jnp.concatenate([q_t, piece_rows], axis=0))
        m_ref[i % 2] = jnp.full(m_ref.shape[1:], -jnp.inf, F32)
        acc_ref[i % 2] = jnp.zeros(acc_ref.shape[1:], F32)
        return q_x

    def key_rows(j):
        return pl.ds(j * FX_T, FX_T) if isinstance(j, int) else pl.ds(pl.multiple_of(j * FX_T, FX_T), FX_T)

    def scores(q_x, j, slot):
        for hh in range(FX_HP):
            s_ref[slot, hh] = jnp.dot(kx_ref[hh, key_rows(j), :], q_x[hh], preferred_element_type=F32)

    def consume(par, j, slot, masked):
        for hh in range(FX_HP):
            s = s_ref[slot, hh]
            if masked:
                key = lax.broadcasted_iota(I32, (FX_T, FX_T), 0)
                qry = lax.broadcasted_iota(I32, (FX_T, FX_T), 1)
                s = jnp.where(qry >= key, s, -jnp.inf)
            m_old = m_ref[par, hh]
            m_new = jnp.maximum(m_old, jnp.max(s, axis=0, keepdims=True))
            p = jnp.exp2(s - m_new).astype(BF16)
            acc_ref[par, hh] = jnp.exp2(m_old - m_new) * acc_ref[par, hh] + jnp.dot(
                vt_ref[hh, j], p, preferred_element_type=F32)
            m_ref[par, hh] = m_new

    def finish(i, slot):
        consume(i % 2, i, slot, True)
        for hh, sl in enumerate(head_slices):
            acc = acc_ref[i % 2, hh]
            o_ref[i * FX_T:(i + 1) * FX_T, sl] = (acc[0:FX_DH, :] / acc[FX_DH:FX_DH + 1, :]).T.astype(BF16)

    diag_slot = 0
    for i in range(nq):
        q_x = start(i)
        first = 0 if i == 0 else 1 - diag_slot
        scores(q_x, 0, first)
        if i > 0:
            finish(i - 1, diag_slot)

        def pair(jj, carry, q_x=q_x, first=first, par=i % 2):
            j = 2 * jj
            scores(q_x, j + 1, 1 - first)
            consume(par, j, first, False)
            scores(q_x, j + 2, first)
            consume(par, j + 1, 1 - first, False)
            return carry

        if i >= 2:
            lax.fori_loop(0, i // 2, pair, 0)
        if i % 2 == 1:
            scores(q_x, i, 1 - first)
            consume(i % 2, i - 1, first, False)
            diag_slot = 1 - first
        else:
            diag_slot = first
    finish(nq - 1, diag_slot)


def _fox_attn(proj, c_neg, B, S):
    T = B * S
    nq = S // FX_T
    wide = FX_HP * FX_DH
    cq = COL_FXQ * (D_MODEL // wide)
    ck = COL_FXK * (D_MODEL // wide)
    cv = COL_FXV * (D_MODEL // wide)
    proj3 = proj.reshape(B, S, N_BIG)
    out = pl.pallas_call(
        _fox_attn_body,
        grid=(B, FX_HEADS // FX_HP),
        in_specs=[
            pl.BlockSpec((None, S, wide), lambda b, h: (b, 0, cq + h)),
            pl.BlockSpec((None, S, wide), lambda b, h: (b, 0, ck + h)),
            pl.BlockSpec((None, S, wide), lambda b, h: (b, 0, cv + h)),
            pl.BlockSpec((None, S, LANES), lambda b, h: (b, 0, 0)),
        ],
        out_specs=pl.BlockSpec((None, S, wide), lambda b, h: (b, 0, h)),
        out_shape=jax.ShapeDtypeStruct((B, S, D_MODEL), BF16),
        scratch_shapes=[
            pltpu.VMEM((FX_HP, S, 2 * FX_DH), BF16),
            pltpu.VMEM((FX_HP, nq, FX_VR, FX_T), BF16),
            pltpu.VMEM((2, FX_HP, 1, FX_T), F32),
            pltpu.VMEM((2, FX_HP, FX_VR, FX_T), F32),
            pltpu.VMEM((2, FX_HP, FX_T, FX_T), F32),
        ],
        compiler_params=_cparams(("parallel", "parallel")),
        name="fox_attn",
    )(proj3, proj3, proj3, c_neg.reshape(B, S, LANES))
    return out.reshape(T, D_MODEL)


def _memkv_body(x_ref, g_ref, w_ref, o_ref):
    hb = _rms(x_ref[...], g_ref[...]).astype(BF16)
    o_ref[...] = jnp.dot(hb, w_ref[...], preferred_element_type=F32).astype(BF16)


def _memkv(mem2d, g, w_kv):
    R = mem2d.shape[0]
    tm = min(512, R)
    N = w_kv.shape[1]
    return pl.pallas_call(
        _memkv_body,
        grid=(R // tm,),
        in_specs=[
            pl.BlockSpec((tm, D_MODEL), lambda i: (i, 0)),
            pl.BlockSpec((1, D_MODEL), lambda i: (0, 0)),
            pl.BlockSpec((D_MODEL, N), lambda i: (0, 0)),
        ],
        out_specs=pl.BlockSpec((tm, N), lambda i: (i, 0)),
        out_shape=jax.ShapeDtypeStruct((R, N), BF16),
        compiler_params=_cparams(("parallel",)),
        name="memkv",
    )(mem2d, g, w_kv)


CA_TQ = 512


def _memattn_body(q_ref, k_ref, v_ref, o_ref):
    scale = CA_DH ** -0.5
    for h in range(CA_HEADS):
        sl = slice(h * CA_DH, (h + 1) * CA_DH)
        s = lax.dot_general(q_ref[:, sl], k_ref[:, sl], (((1,), (1,)), ((), ())),
                            preferred_element_type=F32) * scale
        p = jnp.exp(s - jnp.max(s, axis=-1, keepdims=True))
        l = jnp.sum(p, axis=-1, keepdims=True)
        o = jnp.dot(p.astype(BF16), v_ref[:, sl], preferred_element_type=F32) / l
        o_ref[:, sl] = o.astype(BF16)


def _memattn(proj, kv, B, S, M):
    T = B * S
    nq = S // CA_TQ
    kv3 = kv.reshape(B, M, 2 * D_MODEL)
    return pl.pallas_call(
        _memattn_body,
        grid=(B, nq),
        in_specs=[
            pl.BlockSpec((CA_TQ, D_MODEL), lambda b, i: (b * nq + i, COL_CAQ)),
            pl.BlockSpec((None, M, D_MODEL), lambda b, i: (b, 0, 0)),
            pl.BlockSpec((None, M, D_MODEL), lambda b, i: (b, 0, 1)),
        ],
        out_specs=pl.BlockSpec((CA_TQ, D_MODEL), lambda b, i: (b * nq + i, 0)),
        out_shape=jax.ShapeDtypeStruct((T, D_MODEL), BF16),
        compiler_params=_cparams(("parallel", "arbitrary")),
        name="memattn",
    )(proj, kv3, kv3)


MERGE_TM = 512
MOE_PARTS = 2


def _merge_body(y0_ref, y1_ref, y2_ref, g0_ref, g1_ref, g2_ref, x_ref, wb_ref, wo_ref, gn_ref, wr_ref, br_ref,
                *rest):
    o_ref, hp_ref, ri_ref, rw_ref, cnt_ref, carry_ref = rest[-6:]
    merged = None
    for n, (y_ref, g_ref) in enumerate(((y0_ref, g0_ref), (y1_ref, g1_ref), (y2_ref, g2_ref))):
        p = jnp.dot(y_ref[...], wb_ref[n], preferred_element_type=F32)
        t = jax.nn.sigmoid(g_ref[...].astype(F32)) * p
        merged = t if merged is None else merged + t
    x2 = x_ref[...] + jnp.dot(merged.astype(BF16), wo_ref[...], preferred_element_type=F32)
    o_ref[...] = x2
    _route(x2, gn_ref, wr_ref, br_ref, hp_ref, ri_ref, rw_ref, cnt_ref, carry_ref)


def _merge(y_ml, y_fx, y_ca, proj, x2d, w_branch, w_out, g_moe, w_router, b_router, part, after):
    T = x2d.shape[0] // MOE_PARTS
    tm = MERGE_TM
    off = part * (T // tm)
    src = lambda i: (off + i, 0)
    row = lambda i: (i, 0)
    const = lambda i: (0, 0)
    extra_specs, extra_args = ([], []) if after is None else ([pl.BlockSpec(memory_space=pl.ANY)], [after])
    return pl.pallas_call(
        _merge_body,
        grid=(T // tm,),
        in_specs=[
            pl.BlockSpec((tm, D_MODEL), src),
            pl.BlockSpec((tm, D_MODEL), src),
            pl.BlockSpec((tm, D_MODEL), src),
            pl.BlockSpec((tm, D_MODEL), lambda i: (off + i, COL_GATE0)),
            pl.BlockSpec((tm, D_MODEL), lambda i: (off + i, COL_GATE0 + 1)),
            pl.BlockSpec((tm, D_MODEL), lambda i: (off + i, COL_GATE0 + 2)),
            pl.BlockSpec((tm, D_MODEL), src),
            pl.BlockSpec((3, D_MODEL, D_MODEL), lambda i: (0, 0, 0)),
            pl.BlockSpec((D_MODEL, D_MODEL), const),
            pl.BlockSpec((1, D_MODEL), const),
            pl.BlockSpec((N_EXPERTS, D_MODEL), const),
            pl.BlockSpec((N_EXPERTS, 1), const),
        ] + extra_specs,
        out_specs=[
            pl.BlockSpec((tm, D_MODEL), row),
            pl.BlockSpec((tm, HALF), row),
            pl.BlockSpec((2 * TOP_K, tm), lambda i: (0, i)),
            pl.BlockSpec((2 * TOP_K, tm), lambda i: (0, i)),
            pl.BlockSpec((N_EXPERTS, 1), const),
        ],
        out_shape=[
            jax.ShapeDtypeStruct((T, D_MODEL), F32),
            jax.ShapeDtypeStruct((T, HALF), I32),
            jax.ShapeDtypeStruct((2 * TOP_K, T), I32),
            jax.ShapeDtypeStruct((2 * TOP_K, T), F32),
            jax.ShapeDtypeStruct((N_EXPERTS, 1), F32),
        ],
        scratch_shapes=[pltpu.VMEM((N_EXPERTS, 1), F32)],
        compiler_params=_cparams(("arbitrary",)),
        name="merge_router",
    )(y_ml, y_fx, y_ca, proj, proj, proj, x2d, w_branch, w_out, g_moe, w_router, b_router, *extra_args)


def _route(x2, g_ref, wr_ref, br_ref, hp_ref, ri_ref, rw_ref, cnt_ref, carry_ref):
    tm = MERGE_TM

    @pl.when(pl.program_id(0) == 0)
    def _():
        carry_ref[...] = jnp.zeros_like(carry_ref)

    h = _rms(x2, g_ref[...])
    hp_ref[...] = _pack_rows(h)
    logits = lax.dot_general(wr_ref[...], h.astype(BF16), (((1,), (1,)), ((), ())),
                             preferred_element_type=F32) + br_ref[...]
    eid = lax.broadcasted_iota(I32, (N_EXPERTS, tm), 0).astype(F32)

    work = logits
    onehot_sum = jnp.zeros((N_EXPERTS, tm), F32)
    vals, sels, idxs = [], [], []
    for _ in range(TOP_K):
        mx = jnp.max(work, axis=0, keepdims=True)
        idx = jnp.min(jnp.where(work == mx, eid, float(N_EXPERTS)), axis=0, keepdims=True)
        sel = eid == idx
        onehot_sum = onehot_sum + sel.astype(F32)
        work = jnp.where(sel, -jnp.inf, work)
        vals.append(mx)
        sels.append(sel)
        idxs.append(idx)
    exps = [jnp.exp(v - vals[0]) for v in vals]
    total = exps[0] + exps[1] + exps[2] + exps[3]

    earlier = (lax.broadcasted_iota(I32, (tm, tm), 0) < lax.broadcasted_iota(I32, (tm, tm), 1)).astype(BF16)
    before = jnp.dot(onehot_sum.astype(BF16), earlier, preferred_element_type=F32) + carry_ref[...]
    carry_ref[...] = carry_ref[...] + jnp.sum(onehot_sum, axis=1, keepdims=True)
    cnt_ref[...] = carry_ref[...]

    out_row = lax.broadcasted_iota(I32, (2 * TOP_K, tm), 0)
    ri = jnp.zeros((2 * TOP_K, tm), I32)
    rw = jnp.zeros((2 * TOP_K, tm), F32)
    for k in range(TOP_K):
        rank = jnp.sum(jnp.where(sels[k], before, 0.0), axis=0, keepdims=True)
        ri = jnp.where(out_row == k, idxs[k].astype(I32), ri)
        ri = jnp.where(out_row == TOP_K + k, rank.astype(I32), ri)
        rw = jnp.where(out_row == k, exps[k] / total, rw)
    ri_ref[...] = ri
    rw_ref[...] = rw


EXPERT_TM = 512
SC_CORES = 2
SC_SUBCORES = 16
SC_WORKERS = SC_CORES * SC_SUBCORES
SC_CHUNK = 64
PAD_SLOTS = N_EXPERTS * EXPERT_TM


def _sc_mesh():
    return plsc.VectorSubcoreMesh(core_axis_name="c", subcore_axis_name="s")


def _sc_worker():
    return lax.axis_index("s") * SC_CORES + lax.axis_index("c")


def _scatter_indices(dest):
    T = dest.shape[1]
    n_ch = T // (SC_WORKERS * SC_CHUNK)
    idx = dest.reshape(TOP_K, SC_WORKERS, n_ch, SC_CHUNK).transpose(1, 2, 0, 3)
    return idx.reshape(SC_WORKERS, n_ch * TOP_K, SC_CHUNK)


def _sc_dispatch(hp, idx, pad_idx, n_rows):
    T = hp.shape[0]
    per_w = T // SC_WORKERS
    n_ch = per_w // SC_CHUNK
    n_pc = PAD_SLOTS // (SC_WORKERS * SC_CHUNK)
    assert per_w % SC_CHUNK == 0 and n_ch >= 2 and n_ch % 2 == 0
    pidx = pad_idx.reshape(SC_WORKERS, n_pc, SC_CHUNK)
    zeros = jnp.zeros((SC_CHUNK, HALF), I32)

    @functools.partial(
        pl.kernel, mesh=_sc_mesh(),
        out_type=jax.ShapeDtypeStruct((n_rows, HALF), I32),
        scratch_types=[
            pltpu.VMEM((n_ch * TOP_K, SC_CHUNK), I32),
            pltpu.VMEM((n_pc, SC_CHUNK), I32),
            pltpu.VMEM((2, SC_CHUNK, HALF), I32),
            pltpu.SemaphoreType.DMA((2,)),
            pltpu.SemaphoreType.DMA((2,)),
        ],
        name="sc_dispatch",
    )
    def k(hp_hbm, idx_hbm, pidx_hbm, zeros_hbm, xs_hbm, idx_v, pidx_v, rows_v, lsem, ssem):
        wid = _sc_worker()
        base = wid * per_w
        pltpu.sync_copy(idx_hbm.at[wid], idx_v)
        pltpu.sync_copy(pidx_hbm.at[wid], pidx_v)

        pltpu.sync_copy(zeros_hbm, rows_v.at[0])
        for p in range(n_pc):
            pltpu.make_async_copy(rows_v.at[0], xs_hbm.at[pidx_v.at[p]], ssem.at[0]).start()
        for p in range(n_pc):
            pltpu.make_async_copy(rows_v.at[0], xs_hbm.at[pidx_v.at[p]], ssem.at[0]).wait()

        def load(i, slot):
            return pltpu.make_async_copy(hp_hbm.at[pl.ds(base + i * SC_CHUNK, SC_CHUNK)], rows_v.at[slot],
                                         lsem.at[slot])

        def scatter(i, kk, slot):
            return pltpu.make_async_copy(rows_v.at[slot], xs_hbm.at[idx_v.at[i * TOP_K + kk]], ssem.at[slot])

        load(0, 0).start()

        def body(i2, carry):
            for slot in range(2):
                i = i2 * 2 + slot
                nxt = 1 - slot

                @pl.when(i + 1 < n_ch)
                def _():
                    @pl.when(i >= 1)
                    def _():
                        for kk in range(TOP_K):
                            scatter(i - 1, kk, nxt).wait()
                    load(i + 1, nxt).start()

                load(i, slot).wait()
                for kk in range(TOP_K):
                    scatter(i, kk, slot).start()
            return carry

        lax.fori_loop(0, n_ch // 2, body, 0)
        for kk in range(TOP_K):
            scatter(n_ch - 2, kk, 0).wait()
            scatter(n_ch - 1, kk, 1).wait()

    return k(hp, idx, pidx, zeros)


def _sc_gather(table, idx):
    n = idx.shape[0]
    per_w = n // SC_WORKERS
    n_ch = per_w // SC_CHUNK
    assert per_w % SC_CHUNK == 0 and n_ch >= 2 and n_ch % 2 == 0

    @functools.partial(
        pl.kernel, mesh=_sc_mesh(),
        out_type=jax.ShapeDtypeStruct((n, HALF), I32),
        scratch_types=[
            pltpu.VMEM((n_ch, SC_CHUNK), I32),
            pltpu.VMEM((2, SC_CHUNK, HALF), I32),
            pltpu.SemaphoreType.DMA((2,)),
            pltpu.SemaphoreType.DMA((2,)),
        ],
        name="sc_gather",
    )
    def k(table_hbm, idx_hbm, out_hbm, idx_v, rows_v, gsem, wsem):
        wid = _sc_worker()
        base = wid * per_w
        pltpu.sync_copy(idx_hbm.at[wid], idx_v)

        def gather(i, slot):
            return pltpu.make_async_copy(table_hbm.at[idx_v.at[i]], rows_v.at[slot], gsem.at[slot])

        def writeback(i, slot):
            return pltpu.make_async_copy(rows_v.at[slot], out_hbm.at[pl.ds(base + i * SC_CHUNK, SC_CHUNK)],
                                         wsem.at[slot])

        gather(0, 0).start()

        def body(i2, carry):
            for slot in range(2):
                i = i2 * 2 + slot
                nxt = 1 - slot

                @pl.when(i + 1 < n_ch)
                def _():
                    @pl.when(i >= 1)
                    def _():
                        writeback(i - 1, nxt).wait()
                    gather(i + 1, nxt).start()

                gather(i, slot).wait()
                writeback(i, slot).start()
            return carry

        lax.fori_loop(0, n_ch // 2, body, 0)
        writeback(n_ch - 2, 0).wait()
        writeback(n_ch - 1, 1).wait()

    return k(table, idx.reshape(SC_WORKERS, n_ch, SC_CHUNK))


FF_CHUNK = 512


def _expert_body(te_ref, nv_ref, x_ref, w1f_ref, b1_ref, w2f_ref, b2_ref, y_ref, w1_ref, w2_ref):
    i = pl.program_id(0)

    @pl.when(jnp.logical_or(i == 0, te_ref[i] != te_ref[jnp.maximum(i - 1, 0)]))
    def _():
        w1_ref[...] = w1f_ref[...].astype(BF16)
        w2_ref[...] = w2f_ref[...].astype(BF16)

    @pl.when(i < nv_ref[0])
    def _():
        lo, hi = _unpack_rows(x_ref[...])
        xlo = lo.astype(BF16)
        xhi = hi.astype(BF16)
        acc = jnp.zeros((EXPERT_TM, D_MODEL), F32) + b2_ref[...]
        for c in range(D_FF // FF_CHUNK):
            def up(off):
                cs = slice(off + c * FF_CHUNK, off + (c + 1) * FF_CHUNK)
                return (jnp.dot(xlo, w1_ref[0:HALF, cs], preferred_element_type=F32)
                        + jnp.dot(xhi, w1_ref[HALF:D_MODEL, cs], preferred_element_type=F32)
                        + b1_ref[:, cs])
            g = jnp.minimum(up(0), SWIGLU_LIMIT)
            lin = jnp.clip(up(D_FF), -SWIGLU_LIMIT, SWIGLU_LIMIT)
            a = g * jax.nn.sigmoid(SWIGLU_ALPHA * g) * (lin + 1.0)
            acc = acc + jnp.dot(a.astype(BF16), w2_ref[c * FF_CHUNK:(c + 1) * FF_CHUNK, :],
                                preferred_element_type=F32)
        y_ref[...] = _pack_rows(acc)


def _experts(tile_expert, n_valid, xs, w1, b1, w2, b2):
    n_rows = xs.shape[0]
    tm = EXPERT_TM
    n_tiles = n_rows // tm
    row = lambda i, te, nv: (jnp.minimum(i, nv[0] - 1), 0)
    grid_spec = pltpu.PrefetchScalarGridSpec(
        num_scalar_prefetch=2,
        grid=(n_tiles,),
        in_specs=[
            pl.BlockSpec((tm, HALF), row),
            pl.BlockSpec((None, D_MODEL, 2 * D_FF), lambda i, te, nv: (te[i], 0, 0)),
            pl.BlockSpec((None, 1, 2 * D_FF), lambda i, te, nv: (te[i], 0, 0)),
            pl.BlockSpec((None, D_FF, D_MODEL), lambda i, te, nv: (te[i], 0, 0)),
            pl.BlockSpec((None, 1, D_MODEL), lambda i, te, nv: (te[i], 0, 0)),
        ],
        out_specs=pl.BlockSpec((tm, HALF), row),
        scratch_shapes=[pltpu.VMEM((D_MODEL, 2 * D_FF), BF16), pltpu.VMEM((D_FF, D_MODEL), BF16)],
    )
    return pl.pallas_call(
        _expert_body,
        grid_spec=grid_spec,
        out_shape=jax.ShapeDtypeStruct((n_rows, HALF), I32),
        compiler_params=_cparams(("arbitrary",)),
        name="experts",
    )(tile_expert, n_valid, xs, w1, b1, w2, b2)


COMBINE_TM = 512


def _combine_body(y0_ref, y1_ref, y2_ref, y3_ref, rw_ref, x_ref, g_ref, *rest):
    o_ref = rest[-1]
    acc = x_ref[...]
    rw = jnp.concatenate([rw_ref[...], jnp.zeros((LANES - 2 * TOP_K, COMBINE_TM), F32)], axis=0).T
    for k, y_ref in enumerate((y0_ref, y1_ref, y2_ref, y3_ref)):
        lo, hi = _unpack_rows(y_ref[...])
        acc = acc + rw[:, k:k + 1] * jnp.concatenate([lo, hi], axis=-1)
    o_ref[...] = _rms(acc, g_ref[...])


def _combine(yg, rw, x2, g, part, out_prev):
    T = x2.shape[0]
    tm = COMBINE_TM
    nt = T // tm
    in_specs = [
        pl.BlockSpec((tm, HALF), lambda i: (i, 0)),
        pl.BlockSpec((tm, HALF), lambda i: (nt + i, 0)),
        pl.BlockSpec((tm, HALF), lambda i: (2 * nt + i, 0)),
        pl.BlockSpec((tm, HALF), lambda i: (3 * nt + i, 0)),
        pl.BlockSpec((2 * TOP_K, tm), lambda i: (0, i)),
        pl.BlockSpec((tm, D_MODEL), lambda i: (i, 0)),
        pl.BlockSpec((1, D_MODEL), lambda i: (0, 0)),
    ]
    args = [yg, yg, yg, yg, rw, x2, g]
    aliases = {}
    if out_prev is not None:
        in_specs.append(pl.BlockSpec(memory_space=pl.ANY))
        args.append(out_prev)
        aliases = {len(args) - 1: 0}
    return pl.pallas_call(
        _combine_body,
        grid=(nt,),
        in_specs=in_specs,
        out_specs=pl.BlockSpec((tm, D_MODEL), lambda i: (part * nt + i, 0)),
        out_shape=jax.ShapeDtypeStruct((T * MOE_PARTS, D_MODEL), F32),
        input_output_aliases=aliases,
        compiler_params=_cparams(("parallel",)),
        name="combine",
    )(*args)


def _pad_lanes(v):
    v = v.reshape(1, -1).astype(F32)
    return jnp.pad(v, ((0, 0), (0, LANES - v.shape[1])))


def _layer(x2d, mem2d, B, S, M, norm_mix, w_in, b_ml_gates, conv_ml, ml_head_norm, b_fx_gate, norm_mem,
           w_mem_kv, w_branch, w_out, norm_moe, w_router, b_router, w_exp_in, b_exp_in, w_exp_out,
           b_exp_out, norm_out):
    T = B * S
    w16 = w_in.astype(BF16)
    w_big = jnp.concatenate([w16[:, 0:2048], w16[:, 2056:3080], w16[:, 3080:6152], w16[:, 6160:7184],
                             w16[:, 7184:10256]], axis=1)
    w_small = jnp.concatenate([w16[:, 2048:2056], w16[:, 6152:6160]], axis=1)
    w_small = jnp.pad(w_small, ((0, 0), (0, LANES - w_small.shape[1])))
    row = lambda v: v.reshape(1, -1).astype(F32)

    proj, small = _inproj(x2d, row(norm_mix), w_big, w_small)

    y_ml = _mlstm(proj, small, conv_ml.astype(F32), _pad_lanes(b_ml_gates), row(ml_head_norm), B, S)

    b_fx = jnp.pad(b_fx_gate.reshape(1, -1).astype(F32), ((0, 0), (2 * ML_HEADS, LANES - 2 * ML_HEADS - FX_HEADS)))
    y_fx = _fox_attn(proj, _fox_gate(small, b_fx, B, S), B, S)

    kv = _memkv(mem2d, row(norm_mem), w_mem_kv.astype(BF16))
    y_ca = _memattn(proj, kv, B, S, M)

    w_r = w_router.T.astype(BF16)
    moe_weights = (w_exp_in.astype(F32), b_exp_in.reshape(N_EXPERTS, 1, -1).astype(F32), w_exp_out.astype(F32),
                   b_exp_out.reshape(N_EXPERTS, 1, -1).astype(F32))
    routed, plan = [], None
    for part in range(MOE_PARTS):
        x2, hp, ri, rw, cnt = _merge(y_ml, y_fx, y_ca, proj, x2d, w_branch.astype(BF16), w_out.astype(BF16),
                                     row(norm_moe), w_r, b_router.reshape(N_EXPERTS, 1).astype(F32), part,
                                     None if plan is None else plan[0])
        plan = _moe_plan(ri, cnt)
        routed.append((x2, hp, rw, plan))
    out = None
    for part, (x2, hp, rw, plan) in enumerate(routed):
        out = _combine(_moe_rows(hp, plan, moe_weights), rw, x2, row(norm_out), part, out)
    return out


def _moe_plan(ri, cnt):
    T = ri.shape[1]
    tm = EXPERT_TM
    n_tiles = (T * TOP_K) // tm + N_EXPERTS
    counts = cnt[:, 0].astype(I32)
    padded = ((counts + tm - 1) // tm) * tm
    gend = jnp.cumsum(padded)
    gstart = gend - padded
    expert_ids = jnp.arange(N_EXPERTS, dtype=I32)
    start_of = jnp.sum(jnp.where(ri[0:TOP_K, :, None] == expert_ids, gstart, 0), axis=-1)
    dest = start_of + ri[TOP_K:2 * TOP_K, :]
    n_valid = gend[-1] // tm
    tile_ids = jnp.arange(n_tiles, dtype=I32)
    last_tile = jnp.minimum(tile_ids, n_valid - 1)
    tile_e = jnp.minimum(jnp.sum((gend[None, :] <= last_tile[:, None] * tm).astype(I32), axis=1), N_EXPERTS - 1)

    slot = jnp.arange(tm, dtype=I32)
    spare = n_tiles * tm + slot % SC_CHUNK
    pad_idx = jnp.where(slot[None, :] < (padded - counts)[:, None], (gstart + counts)[:, None] + slot[None, :],
                        spare[None, :]).reshape(-1)

    return (_scatter_indices(dest), dest, pad_idx, tile_e.astype(I32), n_valid.reshape(1).astype(I32),
            n_tiles * tm + SC_CHUNK)


def _moe_rows(hp, plan, moe_weights):
    scatter_idx, dest, pad_idx, tile_e, n_valid, n_rows = plan
    xs = _sc_dispatch(hp, scatter_idx, pad_idx, n_rows)
    ys = _experts(tile_e, n_valid, xs, *moe_weights)
    return _sc_gather(ys, dest.reshape(-1))


def kernel(x, mem, norm_mix, w_in, b_ml_gates, conv_ml, ml_head_norm, b_fx_gate, norm_mem, w_mem_kv, w_branch,
           w_out, norm_moe, w_router, b_router, w_exp_in, b_exp_in, w_exp_out, b_exp_out, norm_final):
    B, S, D = x.shape
    M = mem.shape[1]
    depth = norm_mix.shape[0]
    assert depth == 1, "the combine kernel fuses the final norm, so exactly one layer is supported"
    assert D == D_MODEL and S % ML_BLOCK == 0 and S % FX_T == 0 and S % CA_TQ == 0
    out = _layer(x.reshape(B * S, D), mem.reshape(B * M, D), B, S, M, norm_mix[0], w_in[0], b_ml_gates[0],
                 conv_ml[0], ml_head_norm[0], b_fx_gate[0], norm_mem[0], w_mem_kv[0], w_branch[0], w_out[0],
                 norm_moe[0], w_router[0], b_router[0], w_exp_in[0], b_exp_in[0], w_exp_out[0], b_exp_out[0],
                 norm_final)
    return out.reshape(B, S, D)
```

```python
import functools

import jax
import jax.numpy as jnp
from jax import lax
from jax.experimental import pallas as pl
from jax.experimental.pallas import tpu as pltpu
from jax.experimental.pallas import tpu_sc as plsc

F32 = jnp.float32
BF16 = jnp.bfloat16
I32 = jnp.int32

D_MODEL = 1024
N_MEM_HEADS = 4
ML_HEADS = 4
ML_DQK = 128
ML_DV = 256
ML_CONV = 4
FX_HEADS = 8
FX_DH = 128
CA_HEADS = 4
CA_DH = 256
N_EXPERTS = 32
TOP_K = 4
D_FF = D_MODEL
SWIGLU_LIMIT = 7.0
SWIGLU_ALPHA = 1.702
EPS = 1e-5
LANES = 128
HALF = D_MODEL // 2
HI_MASK = -65536

COL_MLQK, COL_MLV, COL_MLO, COL_FXQ, COL_FXK, COL_FXV, COL_CAQ, COL_GATE0 = 0, 1, 2, 3, 4, 5, 6, 7
N_BIG = 10 * D_MODEL

VMEM_LIMIT = 56 * 1024 * 1024


def _cparams(sem):
    return pltpu.CompilerParams(dimension_semantics=sem, vmem_limit_bytes=VMEM_LIMIT)


def _rms(x, g):
    return x * lax.rsqrt(jnp.mean(x * x, axis=-1, keepdims=True) + EPS) * g


def _log_sigmoid(x):
    return jnp.minimum(x, 0.0) - jnp.log1p(jnp.exp(-jnp.abs(x)))


def _pack_rows(y):
    bits = lax.bitcast_convert_type(y.astype(BF16).astype(F32), I32)
    return lax.shift_right_logical(bits[:, :HALF], 16) | (bits[:, HALF:] & HI_MASK)


def _unpack_rows(w):
    lo = lax.bitcast_convert_type(lax.shift_left(w, 16), F32)
    hi = lax.bitcast_convert_type(w & HI_MASK, F32)
    return lo, hi


def _inproj_body(x_ref, g_ref, w_ref, ws_ref, o_ref, os_ref, h_ref):
    @pl.when(pl.program_id(1) == 0)
    def _():
        hb = _rms(x_ref[...], g_ref[...]).astype(BF16)
        h_ref[...] = hb
        os_ref[...] = jnp.dot(hb, ws_ref[...], preferred_element_type=F32)

    o_ref[...] = jnp.dot(h_ref[...], w_ref[...], preferred_element_type=F32).astype(BF16)


def _inproj(x2d, g, w_big, w_small):
    T = x2d.shape[0]
    tm = min(1024, T)
    tn = 2048
    return pl.pallas_call(
        _inproj_body,
        grid=(T // tm, N_BIG // tn),
        in_specs=[
            pl.BlockSpec((tm, D_MODEL), lambda i, j: (i, 0)),
            pl.BlockSpec((1, D_MODEL), lambda i, j: (0, 0)),
            pl.BlockSpec((D_MODEL, tn), lambda i, j: (0, j)),
            pl.BlockSpec((D_MODEL, LANES), lambda i, j: (0, 0)),
        ],
        out_specs=[
            pl.BlockSpec((tm, tn), lambda i, j: (i, j)),
            pl.BlockSpec((tm, LANES), lambda i, j: (i, 0)),
        ],
        out_shape=[
            jax.ShapeDtypeStruct((T, N_BIG), BF16),
            jax.ShapeDtypeStruct((T, LANES), F32),
        ],
        scratch_shapes=[pltpu.VMEM((tm, D_MODEL), BF16)],
        compiler_params=_cparams(("parallel", "arbitrary")),
        name="inproj",
    )(x2d, g, w_big, w_small)


ML_BLOCK = 512
ML_MB = 1
ML_CHUNK = 128
CONV_PAD = 8


def _mlstm_body(qk_ref, v_ref, o_ref, g_ref, cw_ref, bg_ref, hn_ref, y_ref, xbuf, c_st, n_st, m_st):
    L = ML_CHUNK

    @pl.when(pl.program_id(1) == 0)
    def _():
        xbuf[:, 0:CONV_PAD, :] = jnp.zeros((ML_MB, CONV_PAD, D_MODEL), F32)
        c_st[...] = jnp.zeros_like(c_st)
        n_st[...] = jnp.zeros_like(n_st)
        m_st[...] = jnp.zeros_like(m_st)

    for bb in range(ML_MB):
        xbuf[bb, CONV_PAD:CONV_PAD + ML_BLOCK, :] = qk_ref[bb].astype(F32)
    cw = cw_ref[...]
    row = lax.broadcasted_iota(I32, (L, L), 0)
    col = lax.broadcasted_iota(I32, (L, L), 1)
    tri = (row >= col).astype(BF16)
    causal_t = col >= row
    bg = bg_ref[...]
    scale = ML_DQK ** -0.5
    nt_dims = (((1,), (1,)), ((), ()))

    def chunk(bb, c):
        r0 = c * L
        conv = cw[0:1, :] * xbuf[bb, r0 + CONV_PAD - 3:r0 + CONV_PAD - 3 + L, :]
        for j in range(1, ML_CONV):
            s0 = r0 + CONV_PAD - 3 + j
            conv = conv + cw[j:j + 1, :] * xbuf[bb, s0:s0 + L, :]
        act = conv * jax.nn.sigmoid(conv)

        gates = g_ref[bb, r0:r0 + L, :] + bg
        lf = _log_sigmoid(gates)
        cum = jnp.zeros((L, LANES), F32)
        for _ in range(3):
            piece = lf.astype(BF16)
            cum = cum + jnp.dot(tri, piece, preferred_element_type=F32)
            lf = lf - piece.astype(F32)
        gates_t = gates.T
        cum_t = cum.T
        for h in range(ML_HEADS):
            b_row = cum_t[ML_HEADS + h:ML_HEADS + h + 1, :]
            i_row = gates_t[h:h + 1, :]
            a_col = gates[:, h:h + 1] - cum[:, ML_HEADS + h:ML_HEADS + h + 1]
            st = bb * ML_HEADS + h
            m_prev = m_st[st]
            dm = jnp.where(causal_t, a_col + b_row, -jnp.inf)
            m_inter = b_row + m_prev
            m_t = jnp.maximum(jnp.max(dm, axis=0, keepdims=True), m_inter)
            w_intra = jnp.exp(dm - m_t)
            w_inter = jnp.exp(m_inter - m_t)

            qb = (act[:, h * ML_DQK:(h + 1) * ML_DQK] * scale).astype(BF16)
            kb = act[:, (ML_HEADS + h) * ML_DQK:(ML_HEADS + h + 1) * ML_DQK].astype(BF16)
            v_t = v_ref[bb, r0:r0 + L, h * ML_DV:(h + 1) * ML_DV].astype(F32).T
            p_t = lax.dot_general(kb, qb, nt_dims, preferred_element_type=F32) * w_intra
            c_old = c_st[st]
            n_old = n_st[st]
            num = jnp.dot(v_t.astype(BF16), p_t.astype(BF16), preferred_element_type=F32) + w_inter * (
                lax.dot_general(c_old.astype(BF16), qb, nt_dims, preferred_element_type=F32))
            qn = lax.dot_general(jnp.broadcast_to(n_old, (8, ML_DQK)).astype(BF16), qb, nt_dims,
                                 preferred_element_type=F32)[0:1, :]
            den = jnp.sum(p_t, axis=0, keepdims=True) + w_inter * qn
            hv = num / jnp.maximum(jnp.abs(den), jnp.exp(-m_t))

            m_new = m_t[:, L - 1:L]
            b_last = b_row[:, L - 1:L]
            wk = jnp.exp(b_last - b_row + i_row - m_new)
            decay = jnp.exp(b_last + m_prev - m_new)
            c_st[st] = decay * c_old + jnp.dot((v_t * wk).astype(BF16), kb, preferred_element_type=F32)
            n_st[st] = decay * n_old + jnp.dot(jnp.broadcast_to(wk, (8, L)).astype(BF16), kb,
                                               preferred_element_type=F32)[0:1, :]
            m_st[st] = m_new

            hn = (hv * lax.rsqrt(jnp.mean(hv * hv, axis=0, keepdims=True) + EPS)).T
            og = o_ref[bb, r0:r0 + L, h * ML_DV:(h + 1) * ML_DV].astype(F32)
            y_ref[bb, r0:r0 + L, h * ML_DV:(h + 1) * ML_DV] = (
                hn * hn_ref[:, h * ML_DV:(h + 1) * ML_DV] * jax.nn.sigmoid(og)).astype(BF16)

    for c in range(ML_BLOCK // L):
        for bb in range(ML_MB):
            chunk(bb, c)

    xbuf[:, 0:CONV_PAD, :] = xbuf[:, ML_BLOCK:ML_BLOCK + CONV_PAD, :]


def _mlstm(proj, small, conv_w, b_gates, head_norm, B, S):
    T = B * S
    ns = S // ML_BLOCK
    assert B % ML_MB == 0
    proj3 = proj.reshape(B, S, N_BIG)
    blk = lambda col: pl.BlockSpec((ML_MB, ML_BLOCK, D_MODEL), lambda b, s: (b, s, col))
    out = pl.pallas_call(
        _mlstm_body,
        grid=(B // ML_MB, ns),
        in_specs=[
            blk(COL_MLQK),
            blk(COL_MLV),
            blk(COL_MLO),
            pl.BlockSpec((ML_MB, ML_BLOCK, LANES), lambda b, s: (b, s, 0)),
            pl.BlockSpec((ML_CONV, D_MODEL), lambda b, s: (0, 0)),
            pl.BlockSpec((1, LANES), lambda b, s: (0, 0)),
            pl.BlockSpec((1, D_MODEL), lambda b, s: (0, 0)),
        ],
        out_specs=blk(0),
        out_shape=jax.ShapeDtypeStruct((B, S, D_MODEL), BF16),
        scratch_shapes=[
            pltpu.VMEM((ML_MB, ML_BLOCK + CONV_PAD, D_MODEL), F32),
            pltpu.VMEM((ML_MB * ML_HEADS, ML_DV, ML_DQK), F32),
            pltpu.VMEM((ML_MB * ML_HEADS, 1, ML_DQK), F32),
            pltpu.VMEM((ML_MB * ML_HEADS, 1, 1), F32),
        ],
        compiler_params=_cparams(("parallel", "arbitrary")),
        name="mlstm",
    )(proj3, proj3, proj3, small.reshape(B, S, LANES), conv_w, b_gates, head_norm)
    return out.reshape(T, D_MODEL)


FX_T = 512
FX_HP = 2
FX_VR = FX_DH + 16
LOG2E = 1.4426950408889634
N_PIECES = 3
FX_GATE_T = 128


def _fox_gate_body(g_ref, b_ref, o_ref):
    S = g_ref.shape[0]
    row = lax.broadcasted_iota(I32, (FX_GATE_T, FX_GATE_T), 0)
    col = lax.broadcasted_iota(I32, (FX_GATE_T, FX_GATE_T), 1)
    tri = (row >= col).astype(BF16)
    carry = jnp.zeros((1, LANES), F32)
    for blk in range(S // FX_GATE_T):
        rows = slice(blk * FX_GATE_T, (blk + 1) * FX_GATE_T)
        lf = _log_sigmoid(g_ref[rows, :] + b_ref[...])
        cum = carry
        for _ in range(N_PIECES):
            piece = lf.astype(BF16)
            cum = cum + jnp.dot(tri, piece, preferred_element_type=F32)
            lf = lf - piece.astype(F32)
        carry = cum[FX_GATE_T - 1:FX_GATE_T, :]
        o_ref[rows, :] = cum * (-LOG2E)


def _fox_gate(small, b_fx, B, S):
    return pl.pallas_call(
        _fox_gate_body,
        grid=(B,),
        in_specs=[
            pl.BlockSpec((S, LANES), lambda b: (b, 0)),
            pl.BlockSpec((1, LANES), lambda b: (0, 0)),
        ],
        out_specs=pl.BlockSpec((S, LANES), lambda b: (b, 0)),
        out_shape=jax.ShapeDtypeStruct((B * S, LANES), F32),
        compiler_params=_cparams(("parallel",)),
        name="fox_gate",
    )(small, b_fx)


def _fox_attn_body(q_ref, k_ref, v_ref, c_ref, o_ref, kx_ref, vt_ref, m_ref, acc_ref, s_ref):
    S = k_ref.shape[0]
    nq = S // FX_T

    c = c_ref[...]
    hi = c.astype(BF16)
    r1 = c - hi.astype(F32)
    mid = r1.astype(BF16)
    lo = (r1 - mid.astype(F32)).astype(BF16)
    sel_row = lax.broadcasted_iota(I32, (LANES, LANES), 0)
    sel_col = lax.broadcasted_iota(I32, (LANES, LANES), 1)
    ones_rows = (lax.broadcasted_iota(I32, (FX_VR - FX_DH, FX_T), 0) == 0).astype(BF16)
    head_slices = [slice(hh * FX_DH, (hh + 1) * FX_DH) for hh in range(FX_HP)]
    for hh, sl in enumerate(head_slices):
        lane = 2 * ML_HEADS + pl.program_id(1) * FX_HP + hh
        pieces = None
        for p, part in enumerate((hi, mid, lo)):
            pick = jnp.logical_and(sel_row == lane, sel_col == p).astype(BF16)
            t = jnp.dot(part, pick, preferred_element_type=F32)
            pieces = t if pieces is None else pieces + t
        kx_ref[hh, :, 0:FX_DH] = k_ref[:, sl]
        kx_ref[hh, :, FX_DH:2 * FX_DH] = pieces.astype(BF16)
        for j in range(nq):
            vt = v_ref[j * FX_T:(j + 1) * FX_T, sl].astype(F32).T.astype(BF16)
            vt_ref[hh, j] = jnp.concatenate([vt, ones_rows], axis=0)

    piece_rows = (lax.broadcasted_iota(I32, (FX_DH, FX_T), 0) < N_PIECES).astype(BF16)

    def start(i):
        q_x = []
        for sl in head_slices:
            q_t = (q_ref[i * FX_T:(i + 1) * FX_T, sl].astype(F32) * (FX_DH ** -0.5 * LOG2E)).T.astype(BF16)
            q_x.append(jnp.concatenate([q_t, piece_rows], axis=0))
        m_ref[i % 2] = jnp.full(m_ref.shape[1:], -jnp.inf, F32)
        acc_ref[i % 2] = jnp.zeros(acc_ref.shape[1:], F32)
        return q_x

    def key_rows(j):
        return pl.ds(j * FX_T, FX_T) if isinstance(j, int) else pl.ds(pl.multiple_of(j * FX_T, FX_T), FX_T)

    def scores(q_x, j, slot):
        for hh in range(FX_HP):
            s_ref[slot, hh] = jnp.dot(kx_ref[hh, key_rows(j), :], q_x[hh], preferred_element_type=F32)

    def consume(par, j, slot, masked):
        for hh in range(FX_HP):
            s = s_ref[slot, hh]
            if masked:
                key = lax.broadcasted_iota(I32, (FX_T, FX_T), 0)
                qry = lax.broadcasted_iota(I32, (FX_T, FX_T), 1)
                s = jnp.where(qry >= key, s, -jnp.inf)
            m_old = m_ref[par, hh]
            m_new = jnp.maximum(m_old, jnp.max(s, axis=0, keepdims=True))
            p = jnp.exp2(s - m_new).astype(BF16)
            acc_ref[par, hh] = jnp.exp2(m_old - m_new) * acc_ref[par, hh] + jnp.dot(
                vt_ref[hh, j], p, preferred_element_type=F32)
            m_ref[par, hh] = m_new

    def finish(i, slot):
        consume(i % 2, i, slot, True)
        for hh, sl in enumerate(head_slices):
            acc = acc_ref[i % 2, hh]
            o_ref[i * FX_T:(i + 1) * FX_T, sl] = (acc[0:FX_DH, :] / acc[FX_DH:FX_DH + 1, :]).T.astype(BF16)

    diag_slot = 0
    for i in range(nq):
        q_x = start(i)
        first = 0 if i == 0 else 1 - diag_slot
        scores(q_x, 0, first)
        if i > 0:
            finish(i - 1, diag_slot)

        def pair(jj, carry, q_x=q_x, first=first, par=i % 2):
            j = 2 * jj
            scores(q_x, j + 1, 1 - first)
            consume(par, j, first, False)
            scores(q_x, j + 2, first)
            consume(par, j + 1, 1 - first, False)
            return carry

        if i >= 2:
            lax.fori_loop(0, i // 2, pair, 0)
        if i % 2 == 1:
            scores(q_x, i, 1 - first)
            consume(i % 2, i - 1, first, False)
            diag_slot = 1 - first
        else:
            diag_slot = first
    finish(nq - 1, diag_slot)


def _fox_attn(proj, c_neg, B, S):
    T = B * S
    nq = S // FX_T
    wide = FX_HP * FX_DH
    cq = COL_FXQ * (D_MODEL // wide)
    ck = COL_FXK * (D_MODEL // wide)
    cv = COL_FXV * (D_MODEL // wide)
    proj3 = proj.reshape(B, S, N_BIG)
    out = pl.pallas_call(
        _fox_attn_body,
        grid=(B, FX_HEADS // FX_HP),
        in_specs=[
            pl.BlockSpec((None, S, wide), lambda b, h: (b, 0, cq + h)),
            pl.BlockSpec((None, S, wide), lambda b, h: (b, 0, ck + h)),
            pl.BlockSpec((None, S, wide), lambda b, h: (b, 0, cv + h)),
            pl.BlockSpec((None, S, LANES), lambda b, h: (b, 0, 0)),
        ],
        out_specs=pl.BlockSpec((None, S, wide), lambda b, h: (b, 0, h)),
        out_shape=jax.ShapeDtypeStruct((B, S, D_MODEL), BF16),
        scratch_shapes=[
            pltpu.VMEM((FX_HP, S, 2 * FX_DH), BF16),
            pltpu.VMEM((FX_HP, nq, FX_VR, FX_T), BF16),
            pltpu.VMEM((2, FX_HP, 1, FX_T), F32),
            pltpu.VMEM((2, FX_HP, FX_VR, FX_T), F32),
            pltpu.VMEM((2, FX_HP, FX_T, FX_T), F32),
        ],
        compiler_params=_cparams(("parallel", "parallel")),
        name="fox_attn",
    )(proj3, proj3, proj3, c_neg.reshape(B, S, LANES))
    return out.reshape(T, D_MODEL)


def _memkv_body(x_ref, g_ref, w_ref, o_ref):
    hb = _rms(x_ref[...], g_ref[...]).astype(BF16)
    o_ref[...] = jnp.dot(hb, w_ref[...], preferred_element_type=F32).astype(BF16)


def _memkv(mem2d, g, w_kv):
    R = mem2d.shape[0]
    tm = min(512, R)
    N = w_kv.shape[1]
    return pl.pallas_call(
        _memkv_body,
        grid=(R // tm,),
        in_specs=[
            pl.BlockSpec((tm, D_MODEL), lambda i: (i, 0)),
            pl.BlockSpec((1, D_MODEL), lambda i: (0, 0)),
            pl.BlockSpec((D_MODEL, N), lambda i: (0, 0)),
        ],
        out_specs=pl.BlockSpec((tm, N), lambda i: (i, 0)),
        out_shape=jax.ShapeDtypeStruct((R, N), BF16),
        compiler_params=_cparams(("parallel",)),
        name="memkv",
    )(mem2d, g, w_kv)


CA_TQ = 512


def _memattn_body(q_ref, k_ref, v_ref, o_ref):
    scale = CA_DH ** -0.5
    for h in range(CA_HEADS):
        sl = slice(h * CA_DH, (h + 1) * CA_DH)
        s = lax.dot_general(q_ref[:, sl], k_ref[:, sl], (((1,), (1,)), ((), ())),
                            preferred_element_type=F32) * scale
        p = jnp.exp(s - jnp.max(s, axis=-1, keepdims=True))
        l = jnp.sum(p, axis=-1, keepdims=True)
        o = jnp.dot(p.astype(BF16), v_ref[:, sl], preferred_element_type=F32) / l
        o_ref[:, sl] = o.astype(BF16)


def _memattn(proj, kv, B, S, M):
    T = B * S
    nq = S // CA_TQ
    kv3 = kv.reshape(B, M, 2 * D_MODEL)
    return pl.pallas_call(
        _memattn_body,
        grid=(B, nq),
        in_specs=[
            pl.BlockSpec((CA_TQ, D_MODEL), lambda b, i: (b * nq + i, COL_CAQ)),
            pl.BlockSpec((None, M, D_MODEL), lambda b, i: (b, 0, 0)),
            pl.BlockSpec((None, M, D_MODEL), lambda b, i: (b, 0, 1)),
        ],
        out_specs=pl.BlockSpec((CA_TQ, D_MODEL), lambda b, i: (b * nq + i, 0)),
        out_shape=jax.ShapeDtypeStruct((T, D_MODEL), BF16),
        compiler_params=_cparams(("parallel", "arbitrary")),
        name="memattn",
    )(proj, kv3, kv3)


MERGE_TM = 512
MOE_PARTS = 1


def _merge_body(y0_ref, y1_ref, y2_ref, g0_ref, g1_ref, g2_ref, x_ref, wb_ref, wo_ref, gn_ref, wr_ref, br_ref,
                o_ref, hp_ref, ri_ref, rw_ref, cnt_ref, carry_ref):
    merged = None
    for n, (y_ref, g_ref) in enumerate(((y0_ref, g0_ref), (y1_ref, g1_ref), (y2_ref, g2_ref))):
        p = jnp.dot(y_ref[...], wb_ref[n], preferred_element_type=F32)
        t = jax.nn.sigmoid(g_ref[...].astype(F32)) * p
        merged = t if merged is None else merged + t
    x2 = x_ref[...] + jnp.dot(merged.astype(BF16), wo_ref[...], preferred_element_type=F32)
    o_ref[...] = x2
    _route(x2, gn_ref, wr_ref, br_ref, hp_ref, ri_ref, rw_ref, cnt_ref, carry_ref)


def _merge(y_ml, y_fx, y_ca, proj, x2d, w_branch, w_out, g_moe, w_router, b_router, part):
    T = x2d.shape[0] // MOE_PARTS
    tm = MERGE_TM
    off = part * (T // tm)
    src = lambda i: (off + i, 0)
    row = lambda i: (i, 0)
    const = lambda i: (0, 0)
    return pl.pallas_call(
        _merge_body,
        grid=(T // tm,),
        in_specs=[
            pl.BlockSpec((tm, D_MODEL), src),
            pl.BlockSpec((tm, D_MODEL), src),
            pl.BlockSpec((tm, D_MODEL), src),
            pl.BlockSpec((tm, D_MODEL), lambda i: (off + i, COL_GATE0)),
            pl.BlockSpec((tm, D_MODEL), lambda i: (off + i, COL_GATE0 + 1)),
            pl.BlockSpec((tm, D_MODEL), lambda i: (off + i, COL_GATE0 + 2)),
            pl.BlockSpec((tm, D_MODEL), src),
            pl.BlockSpec((3, D_MODEL, D_MODEL), lambda i: (0, 0, 0)),
            pl.BlockSpec((D_MODEL, D_MODEL), const),
            pl.BlockSpec((1, D_MODEL), const),
            pl.BlockSpec((N_EXPERTS, D_MODEL), const),
            pl.BlockSpec((N_EXPERTS, 1), const),
        ],
        out_specs=[
            pl.BlockSpec((tm, D_MODEL), row),
            pl.BlockSpec((tm, HALF), row),
            pl.BlockSpec((2 * TOP_K, tm), lambda i: (0, i)),
            pl.BlockSpec((2 * TOP_K, tm), lambda i: (0, i)),
            pl.BlockSpec((N_EXPERTS, 1), const),
        ],
        out_shape=[
            jax.ShapeDtypeStruct((T, D_MODEL), F32),
            jax.ShapeDtypeStruct((T, HALF), I32),
            jax.ShapeDtypeStruct((2 * TOP_K, T), I32),
            jax.ShapeDtypeStruct((2 * TOP_K, T), F32),
            jax.ShapeDtypeStruct((N_EXPERTS, 1), F32),
        ],
        scratch_shapes=[pltpu.VMEM((N_EXPERTS, 1), F32)],
        compiler_params=_cparams(("arbitrary",)),
        name="merge_router",
    )(y_ml, y_fx, y_ca, proj, proj, proj, x2d, w_branch, w_out, g_moe, w_router, b_router)


def _route(x2, g_ref, wr_ref, br_ref, hp_ref, ri_ref, rw_ref, cnt_ref, carry_ref):
    tm = MERGE_TM

    @pl.when(pl.program_id(0) == 0)
    def _():
        carry_ref[...] = jnp.zeros_like(carry_ref)

    h = _rms(x2, g_ref[...])
    hp_ref[...] = _pack_rows(h)
    logits = lax.dot_general(wr_ref[...], h.astype(BF16), (((1,), (1,)), ((), ())),
                             preferred_element_type=F32) + br_ref[...]
    eid = lax.broadcasted_iota(I32, (N_EXPERTS, tm), 0).astype(F32)

    work = logits
    onehot_sum = jnp.zeros((N_EXPERTS, tm), F32)
    vals, sels, idxs = [], [], []
    for _ in range(TOP_K):
        mx = jnp.max(work, axis=0, keepdims=True)
        idx = jnp.min(jnp.where(work == mx, eid, float(N_EXPERTS)), axis=0, keepdims=True)
        sel = eid == idx
        onehot_sum = onehot_sum + sel.astype(F32)
        work = jnp.where(sel, -jnp.inf, work)
        vals.append(mx)
        sels.append(sel)
        idxs.append(idx)
    exps = [jnp.exp(v - vals[0]) for v in vals]
    total = exps[0] + exps[1] + exps[2] + exps[3]

    earlier = (lax.broadcasted_iota(I32, (tm, tm), 0) < lax.broadcasted_iota(I32, (tm, tm), 1)).astype(BF16)
    before = jnp.dot(onehot_sum.astype(BF16), earlier, preferred_element_type=F32) + carry_ref[...]
    carry_ref[...] = carry_ref[...] + jnp.sum(onehot_sum, axis=1, keepdims=True)
    cnt_ref[...] = carry_ref[...]

    out_row = lax.broadcasted_iota(I32, (2 * TOP_K, tm), 0)
    ri = jnp.zeros((2 * TOP_K, tm), I32)
    rw = jnp.zeros((2 * TOP_K, tm), F32)
    for k in range(TOP_K):
        rank = jnp.sum(jnp.where(sels[k], before, 0.0), axis=0, keepdims=True)
        ri = jnp.where(out_row == k, idxs[k].astype(I32), ri)
        ri = jnp.where(out_row == TOP_K + k, rank.astype(I32), ri)
        rw = jnp.where(out_row == k, exps[k] / total, rw)
    ri_ref[...] = ri
    rw_ref[...] = rw


EXPERT_TM = 512
SC_CORES = 2
SC_SUBCORES = 16
SC_WORKERS = SC_CORES * SC_SUBCORES
SC_CHUNK = 64
PAD_SLOTS = N_EXPERTS * EXPERT_TM


def _sc_mesh():
    return plsc.VectorSubcoreMesh(core_axis_name="c", subcore_axis_name="s")


def _sc_worker():
    return lax.axis_index("s") * SC_CORES + lax.axis_index("c")


def _scatter_indices(dest):
    T = dest.shape[1]
    n_ch = T // (SC_WORKERS * SC_CHUNK)
    idx = dest.reshape(TOP_K, SC_WORKERS, n_ch, SC_CHUNK).transpose(1, 2, 0, 3)
    return idx.reshape(SC_WORKERS, n_ch * TOP_K, SC_CHUNK)


def _sc_dispatch(hp, idx, pad_idx, n_rows):
    T = hp.shape[0]
    per_w = T // SC_WORKERS
    n_ch = per_w // SC_CHUNK
    n_pc = PAD_SLOTS // (SC_WORKERS * SC_CHUNK)
    assert per_w % SC_CHUNK == 0 and n_ch >= 2 and n_ch % 2 == 0
    pidx = pad_idx.reshape(SC_WORKERS, n_pc, SC_CHUNK)
    zeros = jnp.zeros((SC_CHUNK, HALF), I32)

    @functools.partial(
        pl.kernel, mesh=_sc_mesh(),
        out_type=jax.ShapeDtypeStruct((n_rows, HALF), I32),
        scratch_types=[
            pltpu.VMEM((n_ch * TOP_K, SC_CHUNK), I32),
            pltpu.VMEM((n_pc, SC_CHUNK), I32),
            pltpu.VMEM((2, SC_CHUNK, HALF), I32),
            pltpu.SemaphoreType.DMA((2,)),
            pltpu.SemaphoreType.DMA((2,)),
        ],
        name="sc_dispatch",
    )
    def k(hp_hbm, idx_hbm, pidx_hbm, zeros_hbm, xs_hbm, idx_v, pidx_v, rows_v, lsem, ssem):
        wid = _sc_worker()
        base = wid * per_w
        pltpu.sync_copy(idx_hbm.at[wid], idx_v)
        pltpu.sync_copy(pidx_hbm.at[wid], pidx_v)

        pltpu.sync_copy(zeros_hbm, rows_v.at[0])
        for p in range(n_pc):
            pltpu.make_async_copy(rows_v.at[0], xs_hbm.at[pidx_v.at[p]], ssem.at[0]).start()
        for p in range(n_pc):
            pltpu.make_async_copy(rows_v.at[0], xs_hbm.at[pidx_v.at[p]], ssem.at[0]).wait()

        def load(i, slot):
            return pltpu.make_async_copy(hp_hbm.at[pl.ds(base + i * SC_CHUNK, SC_CHUNK)], rows_v.at[slot],
                                         lsem.at[slot])

        def scatter(i, kk, slot):
            return pltpu.make_async_copy(rows_v.at[slot], xs_hbm.at[idx_v.at[i * TOP_K + kk]], ssem.at[slot])

        load(0, 0).start()

        def body(i2, carry):
            for slot in range(2):
                i = i2 * 2 + slot
                nxt = 1 - slot

                @pl.when(i + 1 < n_ch)
                def _():
                    @pl.when(i >= 1)
                    def _():
                        for kk in range(TOP_K):
                            scatter(i - 1, kk, nxt).wait()
                    load(i + 1, nxt).start()

                load(i, slot).wait()
                for kk in range(TOP_K):
                    scatter(i, kk, slot).start()
            return carry

        lax.fori_loop(0, n_ch // 2, body, 0)
        for kk in range(TOP_K):
            scatter(n_ch - 2, kk, 0).wait()
            scatter(n_ch - 1, kk, 1).wait()

    return k(hp, idx, pidx, zeros)


def _sc_gather(table, idx):
    n = idx.shape[0]
    per_w = n // SC_WORKERS
    n_ch = per_w // SC_CHUNK
    assert per_w % SC_CHUNK == 0 and n_ch >= 2 and n_ch % 2 == 0

    @functools.partial(
        pl.kernel, mesh=_sc_mesh(),
        out_type=jax.ShapeDtypeStruct((n, HALF), I32),
        scratch_types=[
            pltpu.VMEM((n_ch, SC_CHUNK), I32),
            pltpu.VMEM((2, SC_CHUNK, HALF), I32),
            pltpu.SemaphoreType.DMA((2,)),
            pltpu.SemaphoreType.DMA((2,)),
        ],
        name="sc_gather",
    )
    def k(table_hbm, idx_hbm, out_hbm, idx_v, rows_v, gsem, wsem):
        wid = _sc_worker()
        base = wid * per_w
        pltpu.sync_copy(idx_hbm.at[wid], idx_v)

        def gather(i, slot):
            return pltpu.make_async_copy(table_hbm.at[idx_v.at[i]], rows_v.at[slot], gsem.at[slot])

        def writeback(i, slot):
            return pltpu.make_async_copy(rows_v.at[slot], out_hbm.at[pl.ds(base + i * SC_CHUNK, SC_CHUNK)],
                                         wsem.at[slot])

        gather(0, 0).start()

        def body(i2, carry):
            for slot in range(2):
                i = i2 * 2 + slot
                nxt = 1 - slot

                @pl.when(i + 1 < n_ch)
                def _():
                    @pl.when(i >= 1)
                    def _():
                        writeback(i - 1, nxt).wait()
                    gather(i + 1, nxt).start()

                gather(i, slot).wait()
                writeback(i, slot).start()
            return carry

        lax.fori_loop(0, n_ch // 2, body, 0)
        writeback(n_ch - 2, 0).wait()
        writeback(n_ch - 1, 1).wait()

    return k(table, idx.reshape(SC_WORKERS, n_ch, SC_CHUNK))


FF_CHUNK = 512


def _expert_body(te_ref, nv_ref, x_ref, w1f_ref, b1_ref, w2f_ref, b2_ref, y_ref, w1_ref, w2_ref):
    i = pl.program_id(0)

    @pl.when(jnp.logical_or(i == 0, te_ref[i] != te_ref[jnp.maximum(i - 1, 0)]))
    def _():
        w1_ref[...] = w1f_ref[...].astype(BF16)
        w2_ref[...] = w2f_ref[...].astype(BF16)

    @pl.when(i < nv_ref[0])
    def _():
        lo, hi = _unpack_rows(x_ref[...])
        xlo = lo.astype(BF16)
        xhi = hi.astype(BF16)
        acc = jnp.zeros((EXPERT_TM, D_MODEL), F32) + b2_ref[...]
        for c in range(D_FF // FF_CHUNK):
            def up(off):
                cs = slice(off + c * FF_CHUNK, off + (c + 1) * FF_CHUNK)
                return (jnp.dot(xlo, w1_ref[0:HALF, cs], preferred_element_type=F32)
                        + jnp.dot(xhi, w1_ref[HALF:D_MODEL, cs], preferred_element_type=F32)
                        + b1_ref[:, cs])
            g = jnp.minimum(up(0), SWIGLU_LIMIT)
            lin = jnp.clip(up(D_FF), -SWIGLU_LIMIT, SWIGLU_LIMIT)
            a = g * jax.nn.sigmoid(SWIGLU_ALPHA * g) * (lin + 1.0)
            acc = acc + jnp.dot(a.astype(BF16), w2_ref[c * FF_CHUNK:(c + 1) * FF_CHUNK, :],
                                preferred_element_type=F32)
        y_ref[...] = _pack_rows(acc)


def _experts(tile_expert, n_valid, xs, w1, b1, w2, b2):
    n_rows = xs.shape[0]
    tm = EXPERT_TM
    n_tiles = n_rows // tm
    row = lambda i, te, nv: (jnp.minimum(i, nv[0] - 1), 0)
    grid_spec = pltpu.PrefetchScalarGridSpec(
        num_scalar_prefetch=2,
        grid=(n_tiles,),
        in_specs=[
            pl.BlockSpec((tm, HALF), row),
            pl.BlockSpec((None, D_MODEL, 2 * D_FF), lambda i, te, nv: (te[i], 0, 0)),
            pl.BlockSpec((None, 1, 2 * D_FF), lambda i, te, nv: (te[i], 0, 0)),
            pl.BlockSpec((None, D_FF, D_MODEL), lambda i, te, nv: (te[i], 0, 0)),
            pl.BlockSpec((None, 1, D_MODEL), lambda i, te, nv: (te[i], 0, 0)),
        ],
        out_specs=pl.BlockSpec((tm, HALF), row),
        scratch_shapes=[pltpu.VMEM((D_MODEL, 2 * D_FF), BF16), pltpu.VMEM((D_FF, D_MODEL), BF16)],
    )
    return pl.pallas_call(
        _expert_body,
        grid_spec=grid_spec,
        out_shape=jax.ShapeDtypeStruct((n_rows, HALF), I32),
        compiler_params=_cparams(("arbitrary",)),
        name="experts",
    )(tile_expert, n_valid, xs, w1, b1, w2, b2)


COMBINE_TM = 512


def _combine_body(y0_ref, y1_ref, y2_ref, y3_ref, rw_ref, x_ref, g_ref, *rest):
    o_ref = rest[-1]
    acc = x_ref[...]
    rw = jnp.concatenate([rw_ref[...], jnp.zeros((LANES - 2 * TOP_K, COMBINE_TM), F32)], axis=0).T
    for k, y_ref in enumerate((y0_ref, y1_ref, y2_ref, y3_ref)):
        lo, hi = _unpack_rows(y_ref[...])
        acc = acc + rw[:, k:k + 1] * jnp.concatenate([lo, hi], axis=-1)
    o_ref[...] = _rms(acc, g_ref[...])


def _combine(yg, rw, x2, g, part, out_prev):
    T = x2.shape[0]
    tm = COMBINE_TM
    nt = T // tm
    in_specs = [
        pl.BlockSpec((tm, HALF), lambda i: (i, 0)),
        pl.BlockSpec((tm, HALF), lambda i: (nt + i, 0)),
        pl.BlockSpec((tm, HALF), lambda i: (2 * nt + i, 0)),
        pl.BlockSpec((tm, HALF), lambda i: (3 * nt + i, 0)),
        pl.BlockSpec((2 * TOP_K, tm), lambda i: (0, i)),
        pl.BlockSpec((tm, D_MODEL), lambda i: (i, 0)),
        pl.BlockSpec((1, D_MODEL), lambda i: (0, 0)),
    ]
    args = [yg, yg, yg, yg, rw, x2, g]
    aliases = {}
    if out_prev is not None:
        in_specs.append(pl.BlockSpec(memory_space=pl.ANY))
        args.append(out_prev)
        aliases = {len(args) - 1: 0}
    return pl.pallas_call(
        _combine_body,
        grid=(nt,),
        in_specs=in_specs,
        out_specs=pl.BlockSpec((tm, D_MODEL), lambda i: (part * nt + i, 0)),
        out_shape=jax.ShapeDtypeStruct((T * MOE_PARTS, D_MODEL), F32),
        input_output_aliases=aliases,
        compiler_params=_cparams(("parallel",)),
        name="combine",
    )(*args)


def _pad_lanes(v):
    v = v.reshape(1, -1).astype(F32)
    return jnp.pad(v, ((0, 0), (0, LANES - v.shape[1])))


def _layer(x2d, mem2d, B, S, M, norm_mix, w_in, b_ml_gates, conv_ml, ml_head_norm, b_fx_gate, norm_mem,
           w_mem_kv, w_branch, w_out, norm_moe, w_router, b_router, w_exp_in, b_exp_in, w_exp_out,
           b_exp_out, norm_out):
    T = B * S
    w16 = w_in.astype(BF16)
    w_big = jnp.concatenate([w16[:, 0:2048], w16[:, 2056:3080], w16[:, 3080:6152], w16[:, 6160:7184],
                             w16[:, 7184:10256]], axis=1)
    w_small = jnp.concatenate([w16[:, 2048:2056], w16[:, 6152:6160]], axis=1)
    w_small = jnp.pad(w_small, ((0, 0), (0, LANES - w_small.shape[1])))
    row = lambda v: v.reshape(1, -1).astype(F32)

    proj, small = _inproj(x2d, row(norm_mix), w_big, w_small)

    y_ml = _mlstm(proj, small, conv_ml.astype(F32), _pad_lanes(b_ml_gates), row(ml_head_norm), B, S)

    b_fx = jnp.pad(b_fx_gate.reshape(1, -1).astype(F32), ((0, 0), (2 * ML_HEADS, LANES - 2 * ML_HEADS - FX_HEADS)))
    y_fx = _fox_attn(proj, _fox_gate(small, b_fx, B, S), B, S)

    kv = _memkv(mem2d, row(norm_mem), w_mem_kv.astype(BF16))
    y_ca = _memattn(proj, kv, B, S, M)

    w_r = w_router.T.astype(BF16)
    moe_weights = (w_exp_in.astype(F32), b_exp_in.reshape(N_EXPERTS, 1, -1).astype(F32), w_exp_out.astype(F32),
                   b_exp_out.reshape(N_EXPERTS, 1, -1).astype(F32))
    routed = [_merge(y_ml, y_fx, y_ca, proj, x2d, w_branch.astype(BF16), w_out.astype(BF16), row(norm_moe), w_r,
                     b_router.reshape(N_EXPERTS, 1).astype(F32), part) for part in range(MOE_PARTS)]
    out = None
    for part, (x2, hp, ri, rw, cnt) in enumerate(routed):
        out = _combine(_moe_rows(hp, _moe_plan(ri, cnt), moe_weights), rw, x2, row(norm_out), part, out)
    return out


def _moe_plan(ri, cnt):
    T = ri.shape[1]
    tm = EXPERT_TM
    n_tiles = (T * TOP_K) // tm + N_EXPERTS
    counts = cnt[:, 0].astype(I32)
    padded = ((counts + tm - 1) // tm) * tm
    gend = jnp.cumsum(padded)
    gstart = gend - padded
    expert_ids = jnp.arange(N_EXPERTS, dtype=I32)
    start_of = jnp.sum(jnp.where(ri[0:TOP_K, :, None] == expert_ids, gstart, 0), axis=-1)
    dest = start_of + ri[TOP_K:2 * TOP_K, :]
    n_valid = gend[-1] // tm
    tile_ids = jnp.arange(n_tiles, dtype=I32)
    last_tile = jnp.minimum(tile_ids, n_valid - 1)
    tile_e = jnp.minimum(jnp.sum((gend[None, :] <= last_tile[:, None] * tm).astype(I32), axis=1), N_EXPERTS - 1)

    slot = jnp.arange(tm, dtype=I32)
    spare = n_tiles * tm + slot % SC_CHUNK
    pad_idx = jnp.where(slot[None, :] < (padded - counts)[:, None], (gstart + counts)[:, None] + slot[None, :],
                        spare[None, :]).reshape(-1)

    return (_scatter_indices(dest), dest, pad_idx, tile_e.astype(I32), n_valid.reshape(1).astype(I32),
            n_tiles * tm + SC_CHUNK)


def _moe_rows(hp, plan, moe_weights):
    scatter_idx, dest, pad_idx, tile_e, n_valid, n_rows = plan
    xs = _sc_dispatch(hp, scatter_idx, pad_idx, n_rows)
    ys = _experts(tile_e, n_valid, xs, *moe_weights)
    return _sc_gather(ys, dest.reshape(-1))


def kernel(x, mem, norm_mix, w_in, b_ml_gates, conv_ml, ml_head_norm, b_fx_gate, norm_mem, w_mem_kv, w_branch,
           w_out, norm_moe, w_router, b_router, w_exp_in, b_exp_in, w_exp_out, b_exp_out, norm_final):
    B, S, D = x.shape
    M = mem.shape[1]
    depth = norm_mix.shape[0]
    assert depth == 1, "the combine kernel fuses the final norm, so exactly one layer is supported"
    assert D == D_MODEL and S % ML_BLOCK == 0 and S % FX_T == 0 and S % CA_TQ == 0
    out = _layer(x.reshape(B * S, D), mem.reshape(B * M, D), B, S, M, norm_mix[0], w_in[0], b_ml_gates[0],
                 conv_ml[0], ml_head_norm[0], b_fx_gate[0], norm_mem[0], w_mem_kv[0], w_branch[0], w_out[0],
                 norm_moe[0], w_router[0], b_router[0], w_exp_in[0], b_exp_in[0], w_exp_out[0], b_exp_out[0],
                 norm_final)
    return out.reshape(B, S, D)
```

```python
import functools

import jax
import jax.numpy as jnp
from jax import lax
from jax.experimental import pallas as pl
from jax.experimental.pallas import tpu as pltpu
from jax.experimental.pallas import tpu_sc as plsc

F32 = jnp.float32
BF16 = jnp.bfloat16
I32 = jnp.int32

D_MODEL = 1024
N_MEM_HEADS = 4
ML_HEADS = 4
ML_DQK = 128
ML_DV = 256
ML_CONV = 4
FX_HEADS = 8
FX_DH = 128
CA_HEADS = 4
CA_DH = 256
N_EXPERTS = 32
TOP_K = 4
D_FF = D_MODEL
SWIGLU_LIMIT = 7.0
SWIGLU_ALPHA = 1.702
EPS = 1e-5
LANES = 128
HALF = D_MODEL // 2
HI_MASK = -65536

COL_MLQK, COL_MLV, COL_MLO, COL_FXQ, COL_FXK, COL_FXV, COL_CAQ, COL_GATE0 = 0, 1, 2, 3, 4, 5, 6, 7
N_BIG = 10 * D_MODEL

VMEM_LIMIT = 56 * 1024 * 1024


def _cparams(sem):
    return pltpu.CompilerParams(dimension_semantics=sem, vmem_limit_bytes=VMEM_LIMIT)


def _rms(x, g):
    return x * lax.rsqrt(jnp.mean(x * x, axis=-1, keepdims=True) + EPS) * g


def _log_sigmoid(x):
    return jnp.minimum(x, 0.0) - jnp.log1p(jnp.exp(-jnp.abs(x)))


def _pack_rows(y):
    bits = lax.bitcast_convert_type(y.astype(BF16).astype(F32), I32)
    return lax.shift_right_logical(bits[:, :HALF], 16) | (bits[:, HALF:] & HI_MASK)


def _unpack_rows(w):
    lo = lax.bitcast_convert_type(lax.shift_left(w, 16), F32)
    hi = lax.bitcast_convert_type(w & HI_MASK, F32)
    return lo, hi


def _inproj_body(x_ref, g_ref, w_ref, ws_ref, o_ref, os_ref, h_ref):
    @pl.when(pl.program_id(1) == 0)
    def _():
        hb = _rms(x_ref[...], g_ref[...]).astype(BF16)
        h_ref[...] = hb
        os_ref[...] = jnp.dot(hb, ws_ref[...], preferred_element_type=F32)

    o_ref[...] = jnp.dot(h_ref[...], w_ref[...], preferred_element_type=F32).astype(BF16)


def _inproj(x2d, g, w_big, w_small):
    T = x2d.shape[0]
    tm = min(1024, T)
    tn = 2048
    return pl.pallas_call(
        _inproj_body,
        grid=(T // tm, N_BIG // tn),
        in_specs=[
            pl.BlockSpec((tm, D_MODEL), lambda i, j: (i, 0)),
            pl.BlockSpec((1, D_MODEL), lambda i, j: (0, 0)),
            pl.BlockSpec((D_MODEL, tn), lambda i, j: (0, j)),
            pl.BlockSpec((D_MODEL, LANES), lambda i, j: (0, 0)),
        ],
        out_specs=[
            pl.BlockSpec((tm, tn), lambda i, j: (i, j)),
            pl.BlockSpec((tm, LANES), lambda i, j: (i, 0)),
        ],
        out_shape=[
            jax.ShapeDtypeStruct((T, N_BIG), BF16),
            jax.ShapeDtypeStruct((T, LANES), F32),
        ],
        scratch_shapes=[pltpu.VMEM((tm, D_MODEL), BF16)],
        compiler_params=_cparams(("parallel", "arbitrary")),
        name="inproj",
    )(x2d, g, w_big, w_small)


ML_BLOCK = 512
ML_MB = 1
ML_CHUNK = 128
CONV_PAD = 8


def _mlstm_body(qk_ref, v_ref, o_ref, g_ref, cw_ref, bg_ref, hn_ref, y_ref, xbuf, c_st, n_st, m_st):
    L = ML_CHUNK

    @pl.when(pl.program_id(1) == 0)
    def _():
        xbuf[:, 0:CONV_PAD, :] = jnp.zeros((ML_MB, CONV_PAD, D_MODEL), F32)
        c_st[...] = jnp.zeros_like(c_st)
        n_st[...] = jnp.zeros_like(n_st)
        m_st[...] = jnp.zeros_like(m_st)

    for bb in range(ML_MB):
        xbuf[bb, CONV_PAD:CONV_PAD + ML_BLOCK, :] = qk_ref[bb].astype(F32)
    cw = cw_ref[...]
    row = lax.broadcasted_iota(I32, (L, L), 0)
    col = lax.broadcasted_iota(I32, (L, L), 1)
    tri = (row >= col).astype(BF16)
    causal_t = col >= row
    bg = bg_ref[...]
    scale = ML_DQK ** -0.5
    nt_dims = (((1,), (1,)), ((), ()))

    def chunk(bb, c):
        r0 = c * L
        conv = cw[0:1, :] * xbuf[bb, r0 + CONV_PAD - 3:r0 + CONV_PAD - 3 + L, :]
        for j in range(1, ML_CONV):
            s0 = r0 + CONV_PAD - 3 + j
            conv = conv + cw[j:j + 1, :] * xbuf[bb, s0:s0 + L, :]
        act = conv * jax.nn.sigmoid(conv)

        gates = g_ref[bb, r0:r0 + L, :] + bg
        lf = _log_sigmoid(gates)
        cum = jnp.zeros((L, LANES), F32)
        for _ in range(3):
            piece = lf.astype(BF16)
            cum = cum + jnp.dot(tri, piece, preferred_element_type=F32)
            lf = lf - piece.astype(F32)
        gates_t = gates.T
        cum_t = cum.T
        for h in range(ML_HEADS):
            b_row = cum_t[ML_HEADS + h:ML_HEADS + h + 1, :]
            i_row = gates_t[h:h + 1, :]
            a_col = gates[:, h:h + 1] - cum[:, ML_HEADS + h:ML_HEADS + h + 1]
            st = bb * ML_HEADS + h
            m_prev = m_st[st]
            dm = jnp.where(causal_t, a_col + b_row, -jnp.inf)
            m_inter = b_row + m_prev
            m_t = jnp.maximum(jnp.max(dm, axis=0, keepdims=True), m_inter)
            w_intra = jnp.exp(dm - m_t)
            w_inter = jnp.exp(m_inter - m_t)

            qb = (act[:, h * ML_DQK:(h + 1) * ML_DQK] * scale).astype(BF16)
            kb = act[:, (ML_HEADS + h) * ML_DQK:(ML_HEADS + h + 1) * ML_DQK].astype(BF16)
            v_t = v_ref[bb, r0:r0 + L, h * ML_DV:(h + 1) * ML_DV].astype(F32).T
            p_t = lax.dot_general(kb, qb, nt_dims, preferred_element_type=F32) * w_intra
            c_old = c_st[st]
            n_old = n_st[st]
            num = jnp.dot(v_t.astype(BF16), p_t.astype(BF16), preferred_element_type=F32) + w_inter * (
                lax.dot_general(c_old.astype(BF16), qb, nt_dims, preferred_element_type=F32))
            qn = lax.dot_general(jnp.broadcast_to(n_old, (8, ML_DQK)).astype(BF16), qb, nt_dims,
                                 preferred_element_type=F32)[0:1, :]
            den = jnp.sum(p_t, axis=0, keepdims=True) + w_inter * qn
            hv = num / jnp.maximum(jnp.abs(den), jnp.exp(-m_t))

            m_new = m_t[:, L - 1:L]
            b_last = b_row[:, L - 1:L]
            wk = jnp.exp(b_last - b_row + i_row - m_new)
            decay = jnp.exp(b_last + m_prev - m_new)
            c_st[st] = decay * c_old + jnp.dot((v_t * wk).astype(BF16), kb, preferred_element_type=F32)
            n_st[st] = decay * n_old + jnp.dot(jnp.broadcast_to(wk, (8, L)).astype(BF16), kb,
                                               preferred_element_type=F32)[0:1, :]
            m_st[st] = m_new

            hn = (hv * lax.rsqrt(jnp.mean(hv * hv, axis=0, keepdims=True) + EPS)).T
            og = o_ref[bb, r0:r0 + L, h * ML_DV:(h + 1) * ML_DV].astype(F32)
            y_ref[bb, r0:r0 + L, h * ML_DV:(h + 1) * ML_DV] = (
                hn * hn_ref[:, h * ML_DV:(h + 1) * ML_DV] * jax.nn.sigmoid(og)).astype(BF16)

    for c in range(ML_BLOCK // L):
        for bb in range(ML_MB):
            chunk(bb, c)

    xbuf[:, 0:CONV_PAD, :] = xbuf[:, ML_BLOCK:ML_BLOCK + CONV_PAD, :]


def _mlstm(proj, small, conv_w, b_gates, head_norm, B, S):
    T = B * S
    ns = S // ML_BLOCK
    assert B % ML_MB == 0
    proj3 = proj.reshape(B, S, N_BIG)
    blk = lambda col: pl.BlockSpec((ML_MB, ML_BLOCK, D_MODEL), lambda b, s: (b, s, col))
    out = pl.pallas_call(
        _mlstm_body,
        grid=(B // ML_MB, ns),
        in_specs=[
            blk(COL_MLQK),
            blk(COL_MLV),
            blk(COL_MLO),
            pl.BlockSpec((ML_MB, ML_BLOCK, LANES), lambda b, s: (b, s, 0)),
            pl.BlockSpec((ML_CONV, D_MODEL), lambda b, s: (0, 0)),
            pl.BlockSpec((1, LANES), lambda b, s: (0, 0)),
            pl.BlockSpec((1, D_MODEL), lambda b, s: (0, 0)),
        ],
        out_specs=blk(0),
        out_shape=jax.ShapeDtypeStruct((B, S, D_MODEL), BF16),
        scratch_shapes=[
            pltpu.VMEM((ML_MB, ML_BLOCK + CONV_PAD, D_MODEL), F32),
            pltpu.VMEM((ML_MB * ML_HEADS, ML_DV, ML_DQK), F32),
            pltpu.VMEM((ML_MB * ML_HEADS, 1, ML_DQK), F32),
            pltpu.VMEM((ML_MB * ML_HEADS, 1, 1), F32),
        ],
        compiler_params=_cparams(("parallel", "arbitrary")),
        name="mlstm",
    )(proj3, proj3, proj3, small.reshape(B, S, LANES), conv_w, b_gates, head_norm)
    return out.reshape(T, D_MODEL)


FX_T = 512
FX_HP = 2
FX_VR = FX_DH + 16
LOG2E = 1.4426950408889634
N_PIECES = 3
FX_GATE_T = 128


def _fox_gate_body(g_ref, b_ref, o_ref):
    S = g_ref.shape[0]
    row = lax.broadcasted_iota(I32, (FX_GATE_T, FX_GATE_T), 0)
    col = lax.broadcasted_iota(I32, (FX_GATE_T, FX_GATE_T), 1)
    tri = (row >= col).astype(BF16)
    carry = jnp.zeros((1, LANES), F32)
    for blk in range(S // FX_GATE_T):
        rows = slice(blk * FX_GATE_T, (blk + 1) * FX_GATE_T)
        lf = _log_sigmoid(g_ref[rows, :] + b_ref[...])
        cum = carry
        for _ in range(N_PIECES):
            piece = lf.astype(BF16)
            cum = cum + jnp.dot(tri, piece, preferred_element_type=F32)
            lf = lf - piece.astype(F32)
        carry = cum[FX_GATE_T - 1:FX_GATE_T, :]
        o_ref[rows, :] = cum * (-LOG2E)


def _fox_gate(small, b_fx, B, S):
    return pl.pallas_call(
        _fox_gate_body,
        grid=(B,),
        in_specs=[
            pl.BlockSpec((S, LANES), lambda b: (b, 0)),
            pl.BlockSpec((1, LANES), lambda b: (0, 0)),
        ],
        out_specs=pl.BlockSpec((S, LANES), lambda b: (b, 0)),
        out_shape=jax.ShapeDtypeStruct((B * S, LANES), F32),
        compiler_params=_cparams(("parallel",)),
        name="fox_gate",
    )(small, b_fx)


def _fox_attn_body(q_ref, k_ref, v_ref, c_ref, o_ref, kx_ref, vt_ref, m_ref, acc_ref, s_ref):
    S = k_ref.shape[0]
    nq = S // FX_T

    c = c_ref[...]
    hi = c.astype(BF16)
    r1 = c - hi.astype(F32)
    mid = r1.astype(BF16)
    lo = (r1 - mid.astype(F32)).astype(BF16)
    sel_row = lax.broadcasted_iota(I32, (LANES, LANES), 0)
    sel_col = lax.broadcasted_iota(I32, (LANES, LANES), 1)
    ones_rows = (lax.broadcasted_iota(I32, (FX_VR - FX_DH, FX_T), 0) == 0).astype(BF16)
    head_slices = [slice(hh * FX_DH, (hh + 1) * FX_DH) for hh in range(FX_HP)]
    for hh, sl in enumerate(head_slices):
        lane = 2 * ML_HEADS + pl.program_id(1) * FX_HP + hh
        pieces = None
        for p, part in enumerate((hi, mid, lo)):
            pick = jnp.logical_and(sel_row == lane, sel_col == p).astype(BF16)
            t = jnp.dot(part, pick, preferred_element_type=F32)
            pieces = t if pieces is None else pieces + t
        kx_ref[hh, :, 0:FX_DH] = k_ref[:, sl]
        kx_ref[hh, :, FX_DH:2 * FX_DH] = pieces.astype(BF16)
        for j in range(nq):
            vt = v_ref[j * FX_T:(j + 1) * FX_T, sl].astype(F32).T.astype(BF16)
            vt_ref[hh, j] = jnp.concatenate([vt, ones_rows], axis=0)

    piece_rows = (lax.broadcasted_iota(I32, (FX_DH, FX_T), 0) < N_PIECES).astype(BF16)

    def start(i):
        q_x = []
        for sl in head_slices:
            q_t = (q_ref[i * FX_T:(i + 1) * FX_T, sl].astype(F32) * (FX_DH ** -0.5 * LOG2E)).T.astype(BF16)
            q_x.append(jnp.concatenate([q_t, piece_rows], axis=0))
        m_ref[i % 2] = jnp.full(m_ref.shape[1:], -jnp.inf, F32)
        acc_ref[i % 2] = jnp.zeros(acc_ref.shape[1:], F32)
        return q_x

    def key_rows(j):
        return pl.ds(j * FX_T, FX_T) if isinstance(j, int) else pl.ds(pl.multiple_of(j * FX_T, FX_T), FX_T)

    def scores(q_x, j, slot):
        for hh in range(FX_HP):
            s_ref[slot, hh] = jnp.dot(kx_ref[hh, key_rows(j), :], q_x[hh], preferred_element_type=F32)

    def consume(par, j, slot, masked):
        for hh in range(FX_HP):
            s = s_ref[slot, hh]
            if masked:
                key = lax.broadcasted_iota(I32, (FX_T, FX_T), 0)
                qry = lax.broadcasted_iota(I32, (FX_T, FX_T), 1)
                s = jnp.where(qry >= key, s, -jnp.inf)
            m_old = m_ref[par, hh]
            m_new = jnp.maximum(m_old, jnp.max(s, axis=0, keepdims=True))
            p = jnp.exp2(s - m_new).astype(BF16)
            acc_ref[par, hh] = jnp.exp2(m_old - m_new) * acc_ref[par, hh] + jnp.dot(
                vt_ref[hh, j], p, preferred_element_type=F32)
            m_ref[par, hh] = m_new

    def finish(i, slot):
        consume(i % 2, i, slot, True)
        for hh, sl in enumerate(head_slices):
            acc = acc_ref[i % 2, hh]
            o_ref[i * FX_T:(i + 1) * FX_T, sl] = (acc[0:FX_DH, :] / acc[FX_DH:FX_DH + 1, :]).T.astype(BF16)

    diag_slot = 0
    for i in range(nq):
        q_x = start(i)
        first = 0 if i == 0 else 1 - diag_slot
        scores(q_x, 0, first)
        if i > 0:
            finish(i - 1, diag_slot)

        def pair(jj, carry, q_x=q_x, first=first, par=i % 2):
            j = 2 * jj
            scores(q_x, j + 1, 1 - first)
            consume(par, j, first, False)
            scores(q_x, j + 2, first)
            consume(par, j + 1, 1 - first, False)
            return carry

        if i >= 2:
            lax.fori_loop(0, i // 2, pair, 0)
        if i % 2 == 1:
            scores(q_x, i, 1 - first)
            consume(i % 2, i - 1, first, False)
            diag_slot = 1 - first
        else:
            diag_slot = first
    finish(nq - 1, diag_slot)


def _fox_attn(proj, c_neg, B, S):
    T = B * S
    nq = S // FX_T
    wide = FX_HP * FX_DH
    cq = COL_FXQ * (D_MODEL // wide)
    ck = COL_FXK * (D_MODEL // wide)
    cv = COL_FXV * (D_MODEL // wide)
    proj3 = proj.reshape(B, S, N_BIG)
    out = pl.pallas_call(
        _fox_attn_body,
        grid=(B, FX_HEADS // FX_HP),
        in_specs=[
            pl.BlockSpec((None, S, wide), lambda b, h: (b, 0, cq + h)),
            pl.BlockSpec((None, S, wide), lambda b, h: (b, 0, ck + h)),
            pl.BlockSpec((None, S, wide), lambda b, h: (b, 0, cv + h)),
            pl.BlockSpec((None, S, LANES), lambda b, h: (b, 0, 0)),
        ],
        out_specs=pl.BlockSpec((None, S, wide), lambda b, h: (b, 0, h)),
        out_shape=jax.ShapeDtypeStruct((B, S, D_MODEL), BF16),
        scratch_shapes=[
            pltpu.VMEM((FX_HP, S, 2 * FX_DH), BF16),
            pltpu.VMEM((FX_HP, nq, FX_VR, FX_T), BF16),
            pltpu.VMEM((2, FX_HP, 1, FX_T), F32),
            pltpu.VMEM((2, FX_HP, FX_VR, FX_T), F32),
            pltpu.VMEM((2, FX_HP, FX_T, FX_T), F32),
        ],
        compiler_params=_cparams(("parallel", "parallel")),
        name="fox_attn",
    )(proj3, proj3, proj3, c_neg.reshape(B, S, LANES))
    return out.reshape(T, D_MODEL)


def _memkv_body(x_ref, g_ref, w_ref, o_ref):
    hb = _rms(x_ref[...], g_ref[...]).astype(BF16)
    o_ref[...] = jnp.dot(hb, w_ref[...], preferred_element_type=F32).astype(BF16)


def _memkv(mem2d, g, w_kv):
    R = mem2d.shape[0]
    tm = min(512, R)
    N = w_kv.shape[1]
    return pl.pallas_call(
        _memkv_body,
        grid=(R // tm,),
        in_specs=[
            pl.BlockSpec((tm, D_MODEL), lambda i: (i, 0)),
            pl.BlockSpec((1, D_MODEL), lambda i: (0, 0)),
            pl.BlockSpec((D_MODEL, N), lambda i: (0, 0)),
        ],
        out_specs=pl.BlockSpec((tm, N), lambda i: (i, 0)),
        out_shape=jax.ShapeDtypeStruct((R, N), BF16),
        compiler_params=_cparams(("parallel",)),
        name="memkv",
    )(mem2d, g, w_kv)


CA_TQ = 512


def _memattn_body(q_ref, k_ref, v_ref, o_ref):
    scale = CA_DH ** -0.5
    for h in range(CA_HEADS):
        sl = slice(h * CA_DH, (h + 1) * CA_DH)
        s = lax.dot_general(q_ref[:, sl], k_ref[:, sl], (((1,), (1,)), ((), ())),
                            preferred_element_type=F32) * scale
        p = jnp.exp(s - jnp.max(s, axis=-1, keepdims=True))
        l = jnp.sum(p, axis=-1, keepdims=True)
        o = jnp.dot(p.astype(BF16), v_ref[:, sl], preferred_element_type=F32) / l
        o_ref[:, sl] = o.astype(BF16)


def _memattn(proj, kv, B, S, M):
    T = B * S
    nq = S // CA_TQ
    kv3 = kv.reshape(B, M, 2 * D_MODEL)
    return pl.pallas_call(
        _memattn_body,
        grid=(B, nq),
        in_specs=[
            pl.BlockSpec((CA_TQ, D_MODEL), lambda b, i: (b * nq + i, COL_CAQ)),
            pl.BlockSpec((None, M, D_MODEL), lambda b, i: (b, 0, 0)),
            pl.BlockSpec((None, M, D_MODEL), lambda b, i: (b, 0, 1)),
        ],
        out_specs=pl.BlockSpec((CA_TQ, D_MODEL), lambda b, i: (b * nq + i, 0)),
        out_shape=jax.ShapeDtypeStruct((T, D_MODEL), BF16),
        compiler_params=_cparams(("parallel", "arbitrary")),
        name="memattn",
    )(proj, kv3, kv3)


MERGE_TM = 512
MOE_PARTS = 2


def _merge_body(y0_ref, y1_ref, y2_ref, g0_ref, g1_ref, g2_ref, x_ref, wb_ref, wo_ref, gn_ref, wr_ref, br_ref,
                o_ref, hp_ref, ri_ref, rw_ref, cnt_ref, carry_ref):
    merged = None
    for n, (y_ref, g_ref) in enumerate(((y0_ref, g0_ref), (y1_ref, g1_ref), (y2_ref, g2_ref))):
        p = jnp.dot(y_ref[...], wb_ref[n], preferred_element_type=F32)
        t = jax.nn.sigmoid(g_ref[...].astype(F32)) * p
        merged = t if merged is None else merged + t
    x2 = x_ref[...] + jnp.dot(merged.astype(BF16), wo_ref[...], preferred_element_type=F32)
    o_ref[...] = x2
    _route(x2, gn_ref, wr_ref, br_ref, hp_ref, ri_ref, rw_ref, cnt_ref, carry_ref)


def _merge(y_ml, y_fx, y_ca, proj, x2d, w_branch, w_out, g_moe, w_router, b_router, part):
    T = x2d.shape[0] // MOE_PARTS
    tm = MERGE_TM
    off = part * (T // tm)
    src = lambda i: (off + i, 0)
    row = lambda i: (i, 0)
    const = lambda i: (0, 0)
    return pl.pallas_call(
        _merge_body,
        grid=(T // tm,),
        in_specs=[
            pl.BlockSpec((tm, D_MODEL), src),
            pl.BlockSpec((tm, D_MODEL), src),
            pl.BlockSpec((tm, D_MODEL), src),
            pl.BlockSpec((tm, D_MODEL), lambda i: (off + i, COL_GATE0)),
            pl.BlockSpec((tm, D_MODEL), lambda i: (off + i, COL_GATE0 + 1)),
            pl.BlockSpec((tm, D_MODEL), lambda i: (off + i, COL_GATE0 + 2)),
            pl.BlockSpec((tm, D_MODEL), src),
            pl.BlockSpec((3, D_MODEL, D_MODEL), lambda i: (0, 0, 0)),
            pl.BlockSpec((D_MODEL, D_MODEL), const),
            pl.BlockSpec((1, D_MODEL), const),
            pl.BlockSpec((N_EXPERTS, D_MODEL), const),
            pl.BlockSpec((N_EXPERTS, 1), const),
        ],
        out_specs=[
            pl.BlockSpec((tm, D_MODEL), row),
            pl.BlockSpec((tm, HALF), row),
            pl.BlockSpec((2 * TOP_K, tm), lambda i: (0, i)),
            pl.BlockSpec((2 * TOP_K, tm), lambda i: (0, i)),
            pl.BlockSpec((N_EXPERTS, 1), const),
        ],
        out_shape=[
            jax.ShapeDtypeStruct((T, D_MODEL), F32),
            jax.ShapeDtypeStruct((T, HALF), I32),
            jax.ShapeDtypeStruct((2 * TOP_K, T), I32),
            jax.ShapeDtypeStruct((2 * TOP_K, T), F32),
            jax.ShapeDtypeStruct((N_EXPERTS, 1), F32),
        ],
        scratch_shapes=[pltpu.VMEM((N_EXPERTS, 1), F32)],
        compiler_params=_cparams(("arbitrary",)),
        name="merge_router",
    )(y_ml, y_fx, y_ca, proj, proj, proj, x2d, w_branch, w_out, g_moe, w_router, b_router)


def _route(x2, g_ref, wr_ref, br_ref, hp_ref, ri_ref, rw_ref, cnt_ref, carry_ref):
    tm = MERGE_TM

    @pl.when(pl.program_id(0) == 0)
    def _():
        carry_ref[...] = jnp.zeros_like(carry_ref)

    h = _rms(x2, g_ref[...])
    hp_ref[...] = _pack_rows(h)
    logits = lax.dot_general(wr_ref[...], h.astype(BF16), (((1,), (1,)), ((), ())),
                             preferred_element_type=F32) + br_ref[...]
    eid = lax.broadcasted_iota(I32, (N_EXPERTS, tm), 0).astype(F32)

    work = logits
    onehot_sum = jnp.zeros((N_EXPERTS, tm), F32)
    vals, sels, idxs = [], [], []
    for _ in range(TOP_K):
        mx = jnp.max(work, axis=0, keepdims=True)
        idx = jnp.min(jnp.where(work == mx, eid, float(N_EXPERTS)), axis=0, keepdims=True)
        sel = eid == idx
        onehot_sum = onehot_sum + sel.astype(F32)
        work = jnp.where(sel, -jnp.inf, work)
        vals.append(mx)
        sels.append(sel)
        idxs.append(idx)
    exps = [jnp.exp(v - vals[0]) for v in vals]
    total = exps[0] + exps[1] + exps[2] + exps[3]

    earlier = (lax.broadcasted_iota(I32, (tm, tm), 0) < lax.broadcasted_iota(I32, (tm, tm), 1)).astype(BF16)
    before = jnp.dot(onehot_sum.astype(BF16), earlier, preferred_element_type=F32) + carry_ref[...]
    carry_ref[...] = carry_ref[...] + jnp.sum(onehot_sum, axis=1, keepdims=True)
    cnt_ref[...] = carry_ref[...]

    out_row = lax.broadcasted_iota(I32, (2 * TOP_K, tm), 0)
    ri = jnp.zeros((2 * TOP_K, tm), I32)
    rw = jnp.zeros((2 * TOP_K, tm), F32)
    for k in range(TOP_K):
        rank = jnp.sum(jnp.where(sels[k], before, 0.0), axis=0, keepdims=True)
        ri = jnp.where(out_row == k, idxs[k].astype(I32), ri)
        ri = jnp.where(out_row == TOP_K + k, rank.astype(I32), ri)
        rw = jnp.where(out_row == k, exps[k] / total, rw)
    ri_ref[...] = ri
    rw_ref[...] = rw


EXPERT_TM = 512
SC_CORES = 2
SC_SUBCORES = 16
SC_WORKERS = SC_CORES * SC_SUBCORES
SC_CHUNK = 64
PAD_SLOTS = N_EXPERTS * EXPERT_TM


def _sc_mesh():
    return plsc.VectorSubcoreMesh(core_axis_name="c", subcore_axis_name="s")


def _sc_worker():
    return lax.axis_index("s") * SC_CORES + lax.axis_index("c")


def _scatter_indices(dest):
    T = dest.shape[1]
    n_ch = T // (SC_WORKERS * SC_CHUNK)
    idx = dest.reshape(TOP_K, SC_WORKERS, n_ch, SC_CHUNK).transpose(1, 2, 0, 3)
    return idx.reshape(SC_WORKERS, n_ch * TOP_K, SC_CHUNK)


def _sc_dispatch(hp, idx, pad_idx, n_rows):
    T = hp.shape[0]
    per_w = T // SC_WORKERS
    n_ch = per_w // SC_CHUNK
    n_pc = PAD_SLOTS // (SC_WORKERS * SC_CHUNK)
    assert per_w % SC_CHUNK == 0 and n_ch >= 2 and n_ch % 2 == 0
    pidx = pad_idx.reshape(SC_WORKERS, n_pc, SC_CHUNK)
    zeros = jnp.zeros((SC_CHUNK, HALF), I32)

    @functools.partial(
        pl.kernel, mesh=_sc_mesh(),
        out_type=jax.ShapeDtypeStruct((n_rows, HALF), I32),
        scratch_types=[
            pltpu.VMEM((n_ch * TOP_K, SC_CHUNK), I32),
            pltpu.VMEM((n_pc, SC_CHUNK), I32),
            pltpu.VMEM((2, SC_CHUNK, HALF), I32),
            pltpu.SemaphoreType.DMA((2,)),
            pltpu.SemaphoreType.DMA((2,)),
        ],
        name="sc_dispatch",
    )
    def k(hp_hbm, idx_hbm, pidx_hbm, zeros_hbm, xs_hbm, idx_v, pidx_v, rows_v, lsem, ssem):
        wid = _sc_worker()
        base = wid * per_w
        pltpu.sync_copy(idx_hbm.at[wid], idx_v)
        pltpu.sync_copy(pidx_hbm.at[wid], pidx_v)

        pltpu.sync_copy(zeros_hbm, rows_v.at[0])
        for p in range(n_pc):
            pltpu.make_async_copy(rows_v.at[0], xs_hbm.at[pidx_v.at[p]], ssem.at[0]).start()
        for p in range(n_pc):
            pltpu.make_async_copy(rows_v.at[0], xs_hbm.at[pidx_v.at[p]], ssem.at[0]).wait()

        def load(i, slot):
            return pltpu.make_async_copy(hp_hbm.at[pl.ds(base + i * SC_CHUNK, SC_CHUNK)], rows_v.at[slot],
                                         lsem.at[slot])

        def scatter(i, kk, slot):
            return pltpu.make_async_copy(rows_v.at[slot], xs_hbm.at[idx_v.at[i * TOP_K + kk]], ssem.at[slot])

        load(0, 0).start()

        def body(i2, carry):
            for slot in range(2):
                i = i2 * 2 + slot
                nxt = 1 - slot

                @pl.when(i + 1 < n_ch)
                def _():
                    @pl.when(i >= 1)
                    def _():
                        for kk in range(TOP_K):
                            scatter(i - 1, kk, nxt).wait()
                    load(i + 1, nxt).start()

                load(i, slot).wait()
                for kk in range(TOP_K):
                    scatter(i, kk, slot).start()
            return carry

        lax.fori_loop(0, n_ch // 2, body, 0)
        for kk in range(TOP_K):
            scatter(n_ch - 2, kk, 0).wait()
            scatter(n_ch - 1, kk, 1).wait()

    return k(hp, idx, pidx, zeros)


def _sc_gather(table, idx):
    n = idx.shape[0]
    per_w = n // SC_WORKERS
    n_ch = per_w // SC_CHUNK
    assert per_w % SC_CHUNK == 0 and n_ch >= 2 and n_ch % 2 == 0

    @functools.partial(
        pl.kernel, mesh=_sc_mesh(),
        out_type=jax.ShapeDtypeStruct((n, HALF), I32),
        scratch_types=[
            pltpu.VMEM((n_ch, SC_CHUNK), I32),
            pltpu.VMEM((2, SC_CHUNK, HALF), I32),
            pltpu.SemaphoreType.DMA((2,)),
            pltpu.SemaphoreType.DMA((2,)),
        ],
        name="sc_gather",
    )
    def k(table_hbm, idx_hbm, out_hbm, idx_v, rows_v, gsem, wsem):
        wid = _sc_worker()
        base = wid * per_w
        pltpu.sync_copy(idx_hbm.at[wid], idx_v)

        def gather(i, slot):
            return pltpu.make_async_copy(table_hbm.at[idx_v.at[i]], rows_v.at[slot], gsem.at[slot])

        def writeback(i, slot):
            return pltpu.make_async_copy(rows_v.at[slot], out_hbm.at[pl.ds(base + i * SC_CHUNK, SC_CHUNK)],
                                         wsem.at[slot])

        gather(0, 0).start()

        def body(i2, carry):
            for slot in range(2):
                i = i2 * 2 + slot
                nxt = 1 - slot

                @pl.when(i + 1 < n_ch)
                def _():
                    @pl.when(i >= 1)
                    def _():
                        writeback(i - 1, nxt).wait()
                    gather(i + 1, nxt).start()

                gather(i, slot).wait()
                writeback(i, slot).start()
            return carry

        lax.fori_loop(0, n_ch // 2, body, 0)
        writeback(n_ch - 2, 0).wait()
        writeback(n_ch - 1, 1).wait()

    return k(table, idx.reshape(SC_WORKERS, n_ch, SC_CHUNK))


FF_CHUNK = 512


def _expert_body(te_ref, nv_ref, x_ref, w1f_ref, b1_ref, w2f_ref, b2_ref, y_ref, w1_ref, w2_ref):
    i = pl.program_id(0)

    @pl.when(jnp.logical_or(i == 0, te_ref[i] != te_ref[jnp.maximum(i - 1, 0)]))
    def _():
        w1_ref[...] = w1f_ref[...].astype(BF16)
        w2_ref[...] = w2f_ref[...].astype(BF16)

    @pl.when(i < nv_ref[0])
    def _():
        lo, hi = _unpack_rows(x_ref[...])
        xlo = lo.astype(BF16)
        xhi = hi.astype(BF16)
        acc = jnp.zeros((EXPERT_TM, D_MODEL), F32) + b2_ref[...]
        for c in range(D_FF // FF_CHUNK):
            def up(off):
                cs = slice(off + c * FF_CHUNK, off + (c + 1) * FF_CHUNK)
                return (jnp.dot(xlo, w1_ref[0:HALF, cs], preferred_element_type=F32)
                        + jnp.dot(xhi, w1_ref[HALF:D_MODEL, cs], preferred_element_type=F32)
                        + b1_ref[:, cs])
            g = jnp.minimum(up(0), SWIGLU_LIMIT)
            lin = jnp.clip(up(D_FF), -SWIGLU_LIMIT, SWIGLU_LIMIT)
            a = g * jax.nn.sigmoid(SWIGLU_ALPHA * g) * (lin + 1.0)
            acc = acc + jnp.dot(a.astype(BF16), w2_ref[c * FF_CHUNK:(c + 1) * FF_CHUNK, :],
                                preferred_element_type=F32)
        y_ref[...] = _pack_rows(acc)


def _experts(tile_expert, n_valid, xs, w1, b1, w2, b2):
    n_rows = xs.shape[0]
    tm = EXPERT_TM
    n_tiles = n_rows // tm
    row = lambda i, te, nv: (jnp.minimum(i, nv[0] - 1), 0)
    grid_spec = pltpu.PrefetchScalarGridSpec(
        num_scalar_prefetch=2,
        grid=(n_tiles,),
        in_specs=[
            pl.BlockSpec((tm, HALF), row),
            pl.BlockSpec((None, D_MODEL, 2 * D_FF), lambda i, te, nv: (te[i], 0, 0)),
            pl.BlockSpec((None, 1, 2 * D_FF), lambda i, te, nv: (te[i], 0, 0)),
            pl.BlockSpec((None, D_FF, D_MODEL), lambda i, te, nv: (te[i], 0, 0)),
            pl.BlockSpec((None, 1, D_MODEL), lambda i, te, nv: (te[i], 0, 0)),
        ],
        out_specs=pl.BlockSpec((tm, HALF), row),
        scratch_shapes=[pltpu.VMEM((D_MODEL, 2 * D_FF), BF16), pltpu.VMEM((D_FF, D_MODEL), BF16)],
    )
    return pl.pallas_call(
        _expert_body,
        grid_spec=grid_spec,
        out_shape=jax.ShapeDtypeStruct((n_rows, HALF), I32),
        compiler_params=_cparams(("arbitrary",)),
        name="experts",
    )(tile_expert, n_valid, xs, w1, b1, w2, b2)


COMBINE_TM = 512


def _combine_body(y0_ref, y1_ref, y2_ref, y3_ref, rw_ref, x_ref, g_ref, *rest):
    o_ref = rest[-1]
    acc = x_ref[...]
    rw = jnp.concatenate([rw_ref[...], jnp.zeros((LANES - 2 * TOP_K, COMBINE_TM), F32)], axis=0).T
    for k, y_ref in enumerate((y0_ref, y1_ref, y2_ref, y3_ref)):
        lo, hi = _unpack_rows(y_ref[...])
        acc = acc + rw[:, k:k + 1] * jnp.concatenate([lo, hi], axis=-1)
    o_ref[...] = _rms(acc, g_ref[...])


def _combine(yg, rw, x2, g, part, out_prev):
    T = x2.shape[0]
    tm = COMBINE_TM
    nt = T // tm
    in_specs = [
        pl.BlockSpec((tm, HALF), lambda i: (i, 0)),
        pl.BlockSpec((tm, HALF), lambda i: (nt + i, 0)),
        pl.BlockSpec((tm, HALF), lambda i: (2 * nt + i, 0)),
        pl.BlockSpec((tm, HALF), lambda i: (3 * nt + i, 0)),
        pl.BlockSpec((2 * TOP_K, tm), lambda i: (0, i)),
        pl.BlockSpec((tm, D_MODEL), lambda i: (i, 0)),
        pl.BlockSpec((1, D_MODEL), lambda i: (0, 0)),
    ]
    args = [yg, yg, yg, yg, rw, x2, g]
    aliases = {}
    if out_prev is not None:
        in_specs.append(pl.BlockSpec(memory_space=pl.ANY))
        args.append(out_prev)
        aliases = {len(args) - 1: 0}
    return pl.pallas_call(
        _combine_body,
        grid=(nt,),
        in_specs=in_specs,
        out_specs=pl.BlockSpec((tm, D_MODEL), lambda i: (part * nt + i, 0)),
        out_shape=jax.ShapeDtypeStruct((T * MOE_PARTS, D_MODEL), F32),
        input_output_aliases=aliases,
        compiler_params=_cparams(("parallel",)),
        name="combine",
    )(*args)


def _pad_lanes(v):
    v = v.reshape(1, -1).astype(F32)
    return jnp.pad(v, ((0, 0), (0, LANES - v.shape[1])))


def _layer(x2d, mem2d, B, S, M, norm_mix, w_in, b_ml_gates, conv_ml, ml_head_norm, b_fx_gate, norm_mem,
           w_mem_kv, w_branch, w_out, norm_moe, w_router, b_router, w_exp_in, b_exp_in, w_exp_out,
           b_exp_out, norm_out):
    T = B * S
    w16 = w_in.astype(BF16)
    w_big = jnp.concatenate([w16[:, 0:2048], w16[:, 2056:3080], w16[:, 3080:6152], w16[:, 6160:7184],
                             w16[:, 7184:10256]], axis=1)
    w_small = jnp.concatenate([w16[:, 2048:2056], w16[:, 6152:6160]], axis=1)
    w_small = jnp.pad(w_small, ((0, 0), (0, LANES - w_small.shape[1])))
    row = lambda v: v.reshape(1, -1).astype(F32)

    proj, small = _inproj(x2d, row(norm_mix), w_big, w_small)

    y_ml = _mlstm(proj, small, conv_ml.astype(F32), _pad_lanes(b_ml_gates), row(ml_head_norm), B, S)

    b_fx = jnp.pad(b_fx_gate.reshape(1, -1).astype(F32), ((0, 0), (2 * ML_HEADS, LANES - 2 * ML_HEADS - FX_HEADS)))
    y_fx = _fox_attn(proj, _fox_gate(small, b_fx, B, S), B, S)

    kv = _memkv(mem2d, row(norm_mem), w_mem_kv.astype(BF16))
    y_ca = _memattn(proj, kv, B, S, M)

    w_r = w_router.T.astype(BF16)
    moe_weights = (w_exp_in.astype(F32), b_exp_in.reshape(N_EXPERTS, 1, -1).astype(F32), w_exp_out.astype(F32),
                   b_exp_out.reshape(N_EXPERTS, 1, -1).astype(F32))
    routed = [_merge(y_ml, y_fx, y_ca, proj, x2d, w_branch.astype(BF16), w_out.astype(BF16), row(norm_moe), w_r,
                     b_router.reshape(N_EXPERTS, 1).astype(F32), part) for part in range(MOE_PARTS)]
    out = None
    for part, (x2, hp, ri, rw, cnt) in enumerate(routed):
        out = _combine(_moe_rows(hp, _moe_plan(ri, cnt), moe_weights), rw, x2, row(norm_out), part, out)
    return out


def _moe_plan(ri, cnt):
    T = ri.shape[1]
    tm = EXPERT_TM
    n_tiles = (T * TOP_K) // tm + N_EXPERTS
    counts = cnt[:, 0].astype(I32)
    padded = ((counts + tm - 1) // tm) * tm
    gend = jnp.cumsum(padded)
    gstart = gend - padded
    expert_ids = jnp.arange(N_EXPERTS, dtype=I32)
    start_of = jnp.sum(jnp.where(ri[0:TOP_K, :, None] == expert_ids, gstart, 0), axis=-1)
    dest = start_of + ri[TOP_K:2 * TOP_K, :]
    n_valid = gend[-1] // tm
    tile_ids = jnp.arange(n_tiles, dtype=I32)
    last_tile = jnp.minimum(tile_ids, n_valid - 1)
    tile_e = jnp.minimum(jnp.sum((gend[None, :] <= last_tile[:, None] * tm).astype(I32), axis=1), N_EXPERTS - 1)

    slot = jnp.arange(tm, dtype=I32)
    spare = n_tiles * tm + slot % SC_CHUNK
    pad_idx = jnp.where(slot[None, :] < (padded - counts)[:, None], (gstart + counts)[:, None] + slot[None, :],
                        spare[None, :]).reshape(-1)

    return (_scatter_indices(dest), dest, pad_idx, tile_e.astype(I32), n_valid.reshape(1).astype(I32),
            n_tiles * tm + SC_CHUNK)


def _moe_rows(hp, plan, moe_weights):
    scatter_idx, dest, pad_idx, tile_e, n_valid, n_rows = plan
    xs = _sc_dispatch(hp, scatter_idx, pad_idx, n_rows)
    ys = _experts(tile_e, n_valid, xs, *moe_weights)
    return _sc_gather(ys, dest.reshape(-1))


def kernel(x, mem, norm_mix, w_in, b_ml_gates, conv_ml, ml_head_norm, b_fx_gate, norm_mem, w_mem_kv, w_branch,
           w_out, norm_moe, w_router, b_router, w_exp_in, b_exp_in, w_exp_out, b_exp_out, norm_final):
    B, S, D = x.shape
    M = mem.shape[1]
    depth = norm_mix.shape[0]
    assert depth == 1, "the combine kernel fuses the final norm, so exactly one layer is supported"
    assert D == D_MODEL and S % ML_BLOCK == 0 and S % FX_T == 0 and S % CA_TQ == 0
    out = _layer(x.reshape(B * S, D), mem.reshape(B * M, D), B, S, M, norm_mix[0], w_in[0], b_ml_gates[0],
                 conv_ml[0], ml_head_norm[0], b_fx_gate[0], norm_mem[0], w_mem_kv[0], w_branch[0], w_out[0],
                 norm_moe[0], w_router[0], b_router[0], w_exp_in[0], b_exp_in[0], w_exp_out[0], b_exp_out[0],
                 norm_final)
    return out.reshape(B, S, D)
```

```python
import functools

import jax
import jax.numpy as jnp
from jax import lax
from jax.experimental import pallas as pl
from jax.experimental.pallas import tpu as pltpu
from jax.experimental.pallas import tpu_sc as plsc

F32 = jnp.float32
BF16 = jnp.bfloat16
I32 = jnp.int32

D_MODEL = 1024
N_MEM_HEADS = 4
ML_HEADS = 4
ML_DQK = 128
ML_DV = 256
ML_CONV = 4
FX_HEADS = 8
FX_DH = 128
CA_HEADS = 4
CA_DH = 256
N_EXPERTS = 32
TOP_K = 4
D_FF = D_MODEL
SWIGLU_LIMIT = 7.0
SWIGLU_ALPHA = 1.702
EPS = 1e-5
LANES = 128
HALF = D_MODEL // 2
HI_MASK = -65536

COL_MLQK, COL_MLV, COL_MLO, COL_FXQ, COL_FXK, COL_FXV, COL_CAQ, COL_GATE0 = 0, 1, 2, 3, 4, 5, 6, 7
N_BIG = 10 * D_MODEL

VMEM_LIMIT = 56 * 1024 * 1024


def _cparams(sem):
    return pltpu.CompilerParams(dimension_semantics=sem, vmem_limit_bytes=VMEM_LIMIT)


def _rms(x, g):
    return x * lax.rsqrt(jnp.mean(x * x, axis=-1, keepdims=True) + EPS) * g


def _log_sigmoid(x):
    return jnp.minimum(x, 0.0) - jnp.log1p(jnp.exp(-jnp.abs(x)))


def _pack_rows(y):
    bits = lax.bitcast_convert_type(y.astype(BF16).astype(F32), I32)
    return lax.shift_right_logical(bits[:, :HALF], 16) | (bits[:, HALF:] & HI_MASK)


def _unpack_rows(w):
    lo = lax.bitcast_convert_type(lax.shift_left(w, 16), F32)
    hi = lax.bitcast_convert_type(w & HI_MASK, F32)
    return lo, hi


def _inproj_body(x_ref, g_ref, w_ref, ws_ref, o_ref, os_ref, h_ref):
    @pl.when(pl.program_id(1) == 0)
    def _():
        hb = _rms(x_ref[...], g_ref[...]).astype(BF16)
        h_ref[...] = hb
        os_ref[...] = jnp.dot(hb, ws_ref[...], preferred_element_type=F32)

    o_ref[...] = jnp.dot(h_ref[...], w_ref[...], preferred_element_type=F32).astype(BF16)


def _inproj(x2d, g, w_big, w_small):
    T = x2d.shape[0]
    tm = min(1024, T)
    tn = 2048
    return pl.pallas_call(
        _inproj_body,
        grid=(T // tm, N_BIG // tn),
        in_specs=[
            pl.BlockSpec((tm, D_MODEL), lambda i, j: (i, 0)),
            pl.BlockSpec((1, D_MODEL), lambda i, j: (0, 0)),
            pl.BlockSpec((D_MODEL, tn), lambda i, j: (0, j)),
            pl.BlockSpec((D_MODEL, LANES), lambda i, j: (0, 0)),
        ],
        out_specs=[
            pl.BlockSpec((tm, tn), lambda i, j: (i, j)),
            pl.BlockSpec((tm, LANES), lambda i, j: (i, 0)),
        ],
        out_shape=[
            jax.ShapeDtypeStruct((T, N_BIG), BF16),
            jax.ShapeDtypeStruct((T, LANES), F32),
        ],
        scratch_shapes=[pltpu.VMEM((tm, D_MODEL), BF16)],
        compiler_params=_cparams(("parallel", "arbitrary")),
        name="inproj",
    )(x2d, g, w_big, w_small)


ML_BLOCK = 512
ML_MB = 1
ML_CHUNK = 128
CONV_PAD = 8


def _mlstm_body(qk_ref, v_ref, o_ref, g_ref, cw_ref, bg_ref, hn_ref, y_ref, xbuf, c_st, n_st, m_st):
    L = ML_CHUNK

    @pl.when(pl.program_id(1) == 0)
    def _():
        xbuf[:, 0:CONV_PAD, :] = jnp.zeros((ML_MB, CONV_PAD, D_MODEL), F32)
        c_st[...] = jnp.zeros_like(c_st)
        n_st[...] = jnp.zeros_like(n_st)
        m_st[...] = jnp.zeros_like(m_st)

    for bb in range(ML_MB):
        xbuf[bb, CONV_PAD:CONV_PAD + ML_BLOCK, :] = qk_ref[bb].astype(F32)
    cw = cw_ref[...]
    row = lax.broadcasted_iota(I32, (L, L), 0)
    col = lax.broadcasted_iota(I32, (L, L), 1)
    tri = (row >= col).astype(BF16)
    causal_t = col >= row
    bg = bg_ref[...]
    scale = ML_DQK ** -0.5
    nt_dims = (((1,), (1,)), ((), ()))

    def chunk(bb, c):
        r0 = c * L
        conv = cw[0:1, :] * xbuf[bb, r0 + CONV_PAD - 3:r0 + CONV_PAD - 3 + L, :]
        for j in range(1, ML_CONV):
            s0 = r0 + CONV_PAD - 3 + j
            conv = conv + cw[j:j + 1, :] * xbuf[bb, s0:s0 + L, :]
        act = conv * jax.nn.sigmoid(conv)

        gates = g_ref[bb, r0:r0 + L, :] + bg
        lf = _log_sigmoid(gates)
        cum = jnp.zeros((L, LANES), F32)
        for _ in range(3):
            piece = lf.astype(BF16)
            cum = cum + jnp.dot(tri, piece, preferred_element_type=F32)
            lf = lf - piece.astype(F32)
        gates_t = gates.T
        cum_t = cum.T
        for h in range(ML_HEADS):
            b_row = cum_t[ML_HEADS + h:ML_HEADS + h + 1, :]
            i_row = gates_t[h:h + 1, :]
            a_col = gates[:, h:h + 1] - cum[:, ML_HEADS + h:ML_HEADS + h + 1]
            st = bb * ML_HEADS + h
            m_prev = m_st[st]
            dm = jnp.where(causal_t, a_col + b_row, -jnp.inf)
            m_inter = b_row + m_prev
            m_t = jnp.maximum(jnp.max(dm, axis=0, keepdims=True), m_inter)
            w_intra = jnp.exp(dm - m_t)
            w_inter = jnp.exp(m_inter - m_t)

            qb = (act[:, h * ML_DQK:(h + 1) * ML_DQK] * scale).astype(BF16)
            kb = act[:, (ML_HEADS + h) * ML_DQK:(ML_HEADS + h + 1) * ML_DQK].astype(BF16)
            v_t = v_ref[bb, r0:r0 + L, h * ML_DV:(h + 1) * ML_DV].astype(F32).T
            p_t = lax.dot_general(kb, qb, nt_dims, preferred_element_type=F32) * w_intra
            c_old = c_st[st]
            n_old = n_st[st]
            num = jnp.dot(v_t.astype(BF16), p_t.astype(BF16), preferred_element_type=F32) + w_inter * (
                lax.dot_general(c_old.astype(BF16), qb, nt_dims, preferred_element_type=F32))
            qn = lax.dot_general(jnp.broadcast_to(n_old, (8, ML_DQK)).astype(BF16), qb, nt_dims,
                                 preferred_element_type=F32)[0:1, :]
            den = jnp.sum(p_t, axis=0, keepdims=True) + w_inter * qn
            hv = num / jnp.maximum(jnp.abs(den), jnp.exp(-m_t))

            m_new = m_t[:, L - 1:L]
            b_last = b_row[:, L - 1:L]
            wk = jnp.exp(b_last - b_row + i_row - m_new)
            decay = jnp.exp(b_last + m_prev - m_new)
            c_st[st] = decay * c_old + jnp.dot((v_t * wk).astype(BF16), kb, preferred_element_type=F32)
            n_st[st] = decay * n_old + jnp.dot(jnp.broadcast_to(wk, (8, L)).astype(BF16), kb,
                                               preferred_element_type=F32)[0:1, :]
            m_st[st] = m_new

            hn = (hv * lax.rsqrt(jnp.mean(hv * hv, axis=0, keepdims=True) + EPS)).T
            og = o_ref[bb, r0:r0 + L, h * ML_DV:(h + 1) * ML_DV].astype(F32)
            y_ref[bb, r0:r0 + L, h * ML_DV:(h + 1) * ML_DV] = (
                hn * hn_ref[:, h * ML_DV:(h + 1) * ML_DV] * jax.nn.sigmoid(og)).astype(BF16)

    for c in range(ML_BLOCK // L):
        for bb in range(ML_MB):
            chunk(bb, c)

    xbuf[:, 0:CONV_PAD, :] = xbuf[:, ML_BLOCK:ML_BLOCK + CONV_PAD, :]


def _mlstm(proj, small, conv_w, b_gates, head_norm, B, S):
    T = B * S
    ns = S // ML_BLOCK
    assert B % ML_MB == 0
    proj3 = proj.reshape(B, S, N_BIG)
    blk = lambda col: pl.BlockSpec((ML_MB, ML_BLOCK, D_MODEL), lambda b, s: (b, s, col))
    out = pl.pallas_call(
        _mlstm_body,
        grid=(B // ML_MB, ns),
        in_specs=[
            blk(COL_MLQK),
            blk(COL_MLV),
            blk(COL_MLO),
            pl.BlockSpec((ML_MB, ML_BLOCK, LANES), lambda b, s: (b, s, 0)),
            pl.BlockSpec((ML_CONV, D_MODEL), lambda b, s: (0, 0)),
            pl.BlockSpec((1, LANES), lambda b, s: (0, 0)),
            pl.BlockSpec((1, D_MODEL), lambda b, s: (0, 0)),
        ],
        out_specs=blk(0),
        out_shape=jax.ShapeDtypeStruct((B, S, D_MODEL), BF16),
        scratch_shapes=[
            pltpu.VMEM((ML_MB, ML_BLOCK + CONV_PAD, D_MODEL), F32),
            pltpu.VMEM((ML_MB * ML_HEADS, ML_DV, ML_DQK), F32),
            pltpu.VMEM((ML_MB * ML_HEADS, 1, ML_DQK), F32),
            pltpu.VMEM((ML_MB * ML_HEADS, 1, 1), F32),
        ],
        compiler_params=_cparams(("parallel", "arbitrary")),
        name="mlstm",
    )(proj3, proj3, proj3, small.reshape(B, S, LANES), conv_w, b_gates, head_norm)
    return out.reshape(T, D_MODEL)


FX_T = 512
FX_HP = 2
FX_VR = FX_DH + 16
LOG2E = 1.4426950408889634
N_PIECES = 3
FX_GATE_T = 128


def _fox_gate_body(g_ref, b_ref, o_ref):
    S = g_ref.shape[0]
    row = lax.broadcasted_iota(I32, (FX_GATE_T, FX_GATE_T), 0)
    col = lax.broadcasted_iota(I32, (FX_GATE_T, FX_GATE_T), 1)
    tri = (row >= col).astype(BF16)
    carry = jnp.zeros((1, LANES), F32)
    for blk in range(S // FX_GATE_T):
        rows = slice(blk * FX_GATE_T, (blk + 1) * FX_GATE_T)
        lf = _log_sigmoid(g_ref[rows, :] + b_ref[...])
        cum = carry
        for _ in range(N_PIECES):
            piece = lf.astype(BF16)
            cum = cum + jnp.dot(tri, piece, preferred_element_type=F32)
            lf = lf - piece.astype(F32)
        carry = cum[FX_GATE_T - 1:FX_GATE_T, :]
        o_ref[rows, :] = cum * (-LOG2E)


def _fox_gate(small, b_fx, B, S):
    return pl.pallas_call(
        _fox_gate_body,
        grid=(B,),
        in_specs=[
            pl.BlockSpec((S, LANES), lambda b: (b, 0)),
            pl.BlockSpec((1, LANES), lambda b: (0, 0)),
        ],
        out_specs=pl.BlockSpec((S, LANES), lambda b: (b, 0)),
        out_shape=jax.ShapeDtypeStruct((B * S, LANES), F32),
        compiler_params=_cparams(("parallel",)),
        name="fox_gate",
    )(small, b_fx)


def _fox_attn_body(q_ref, k_ref, v_ref, c_ref, o_ref, kx_ref, vt_ref, m_ref, acc_ref, s_ref):
    S = k_ref.shape[0]
    nq = S // FX_T

    c = c_ref[...]
    hi = c.astype(BF16)
    r1 = c - hi.astype(F32)
    mid = r1.astype(BF16)
    lo = (r1 - mid.astype(F32)).astype(BF16)
    sel_row = lax.broadcasted_iota(I32, (LANES, LANES), 0)
    sel_col = lax.broadcasted_iota(I32, (LANES, LANES), 1)
    ones_rows = (lax.broadcasted_iota(I32, (FX_VR - FX_DH, FX_T), 0) == 0).astype(BF16)
    head_slices = [slice(hh * FX_DH, (hh + 1) * FX_DH) for hh in range(FX_HP)]
    for hh, sl in enumerate(head_slices):
        lane = 2 * ML_HEADS + pl.program_id(1) * FX_HP + hh
        pieces = None
        for p, part in enumerate((hi, mid, lo)):
            pick = jnp.logical_and(sel_row == lane, sel_col == p).astype(BF16)
            t = jnp.dot(part, pick, preferred_element_type=F32)
            pieces = t if pieces is None else pieces + t
        kx_ref[hh, :, 0:FX_DH] = k_ref[:, sl]
        kx_ref[hh, :, FX_DH:2 * FX_DH] = pieces.astype(BF16)
        for j in range(nq):
            vt = v_ref[j * FX_T:(j + 1) * FX_T, sl].astype(F32).T.astype(BF16)
            vt_ref[hh, j] = jnp.concatenate([vt, ones_rows], axis=0)

    piece_rows = (lax.broadcasted_iota(I32, (FX_DH, FX_T), 0) < N_PIECES).astype(BF16)

    def start(i):
        q_x = []
        for sl in head_slices:
            q_t = (q_ref[i * FX_T:(i + 1) * FX_T, sl].astype(F32) * (FX_DH ** -0.5 * LOG2E)).T.astype(BF16)
            q_x.append(jnp.concatenate([q_t, piece_rows], axis=0))
        m_ref[i % 2] = jnp.full(m_ref.shape[1:], -jnp.inf, F32)
        acc_ref[i % 2] = jnp.zeros(acc_ref.shape[1:], F32)
        return q_x

    def key_rows(j):
        return pl.ds(j * FX_T, FX_T) if isinstance(j, int) else pl.ds(pl.multiple_of(j * FX_T, FX_T), FX_T)

    def scores(q_x, j, slot):
        for hh in range(FX_HP):
            s_ref[slot, hh] = jnp.dot(kx_ref[hh, key_rows(j), :], q_x[hh], preferred_element_type=F32)

    def consume(par, j, slot, masked):
        for hh in range(FX_HP):
            s = s_ref[slot, hh]
            if masked:
                key = lax.broadcasted_iota(I32, (FX_T, FX_T), 0)
                qry = lax.broadcasted_iota(I32, (FX_T, FX_T), 1)
                s = jnp.where(qry >= key, s, -jnp.inf)
            m_old = m_ref[par, hh]
            m_new = jnp.maximum(m_old, jnp.max(s, axis=0, keepdims=True))
            p = jnp.exp2(s - m_new).astype(BF16)
            acc_ref[par, hh] = jnp.exp2(m_old - m_new) * acc_ref[par, hh] + jnp.dot(
                vt_ref[hh, j], p, preferred_element_type=F32)
            m_ref[par, hh] = m_new

    def finish(i, slot):
        consume(i % 2, i, slot, True)
        for hh, sl in enumerate(head_slices):
            acc = acc_ref[i % 2, hh]
            o_ref[i * FX_T:(i + 1) * FX_T, sl] = (acc[0:FX_DH, :] / acc[FX_DH:FX_DH + 1, :]).T.astype(BF16)

    diag_slot = 0
    for i in range(nq):
        q_x = start(i)
        first = 0 if i == 0 else 1 - diag_slot
        scores(q_x, 0, first)
        if i > 0:
            finish(i - 1, diag_slot)

        def pair(jj, carry, q_x=q_x, first=first, par=i % 2):
            j = 2 * jj
            scores(q_x, j + 1, 1 - first)
            consume(par, j, first, False)
            scores(q_x, j + 2, first)
            consume(par, j + 1, 1 - first, False)
            return carry

        if i >= 2:
            lax.fori_loop(0, i // 2, pair, 0)
        if i % 2 == 1:
            scores(q_x, i, 1 - first)
            consume(i % 2, i - 1, first, False)
            diag_slot = 1 - first
        else:
            diag_slot = first
    finish(nq - 1, diag_slot)


def _fox_attn(proj, c_neg, B, S):
    T = B * S
    nq = S // FX_T
    wide = FX_HP * FX_DH
    cq = COL_FXQ * (D_MODEL // wide)
    ck = COL_FXK * (D_MODEL // wide)
    cv = COL_FXV * (D_MODEL // wide)
    proj3 = proj.reshape(B, S, N_BIG)
    out = pl.pallas_call(
        _fox_attn_body,
        grid=(B, FX_HEADS // FX_HP),
        in_specs=[
            pl.BlockSpec((None, S, wide), lambda b, h: (b, 0, cq + h)),
            pl.BlockSpec((None, S, wide), lambda b, h: (b, 0, ck + h)),
            pl.BlockSpec((None, S, wide), lambda b, h: (b, 0, cv + h)),
            pl.BlockSpec((None, S, LANES), lambda b, h: (b, 0, 0)),
        ],
        out_specs=pl.BlockSpec((None, S, wide), lambda b, h: (b, 0, h)),
        out_shape=jax.ShapeDtypeStruct((B, S, D_MODEL), BF16),
        scratch_shapes=[
            pltpu.VMEM((FX_HP, S, 2 * FX_DH), BF16),
            pltpu.VMEM((FX_HP, nq, FX_VR, FX_T), BF16),
            pltpu.VMEM((2, FX_HP, 1, FX_T), F32),
            pltpu.VMEM((2, FX_HP, FX_VR, FX_T), F32),
            pltpu.VMEM((2, FX_HP, FX_T, FX_T), F32),
        ],
        compiler_params=_cparams(("parallel", "parallel")),
        name="fox_attn",
    )(proj3, proj3, proj3, c_neg.reshape(B, S, LANES))
    return out.reshape(T, D_MODEL)


def _memkv_body(x_ref, g_ref, w_ref, o_ref):
    hb = _rms(x_ref[...], g_ref[...]).astype(BF16)
    o_ref[...] = jnp.dot(hb, w_ref[...], preferred_element_type=F32).astype(BF16)


def _memkv(mem2d, g, w_kv):
    R = mem2d.shape[0]
    tm = min(512, R)
    N = w_kv.shape[1]
    return pl.pallas_call(
        _memkv_body,
        grid=(R // tm,),
        in_specs=[
            pl.BlockSpec((tm, D_MODEL), lambda i: (i, 0)),
            pl.BlockSpec((1, D_MODEL), lambda i: (0, 0)),
            pl.BlockSpec((D_MODEL, N), lambda i: (0, 0)),
        ],
        out_specs=pl.BlockSpec((tm, N), lambda i: (i, 0)),
        out_shape=jax.ShapeDtypeStruct((R, N), BF16),
        compiler_params=_cparams(("parallel",)),
        name="memkv",
    )(mem2d, g, w_kv)


CA_TQ = 512


def _memattn_body(q_ref, k_ref, v_ref, o_ref):
    scale = CA_DH ** -0.5
    for h in range(CA_HEADS):
        sl = slice(h * CA_DH, (h + 1) * CA_DH)
        s = lax.dot_general(q_ref[:, sl], k_ref[:, sl], (((1,), (1,)), ((), ())),
                            preferred_element_type=F32) * scale
        p = jnp.exp(s - jnp.max(s, axis=-1, keepdims=True))
        l = jnp.sum(p, axis=-1, keepdims=True)
        o = jnp.dot(p.astype(BF16), v_ref[:, sl], preferred_element_type=F32) / l
        o_ref[:, sl] = o.astype(BF16)


def _memattn(proj, kv, B, S, M):
    T = B * S
    nq = S // CA_TQ
    kv3 = kv.reshape(B, M, 2 * D_MODEL)
    return pl.pallas_call(
        _memattn_body,
        grid=(B, nq),
        in_specs=[
            pl.BlockSpec((CA_TQ, D_MODEL), lambda b, i: (b * nq + i, COL_CAQ)),
            pl.BlockSpec((None, M, D_MODEL), lambda b, i: (b, 0, 0)),
            pl.BlockSpec((None, M, D_MODEL), lambda b, i: (b, 0, 1)),
        ],
        out_specs=pl.BlockSpec((CA_TQ, D_MODEL), lambda b, i: (b * nq + i, 0)),
        out_shape=jax.ShapeDtypeStruct((T, D_MODEL), BF16),
        compiler_params=_cparams(("parallel", "arbitrary")),
        name="memattn",
    )(proj, kv3, kv3)


MERGE_TM = 512
MOE_PARTS = 2


def _merge_body(y0_ref, y1_ref, y2_ref, g0_ref, g1_ref, g2_ref, x_ref, wb_ref, wo_ref, gn_ref, wr_ref, br_ref,
                o_ref, hp_ref, ri_ref, rw_ref, cnt_ref, carry_ref):
    merged = None
    for n, (y_ref, g_ref) in enumerate(((y0_ref, g0_ref), (y1_ref, g1_ref), (y2_ref, g2_ref))):
        p = jnp.dot(y_ref[...], wb_ref[n], preferred_element_type=F32)
        t = jax.nn.sigmoid(g_ref[...].astype(F32)) * p
        merged = t if merged is None else merged + t
    x2 = x_ref[...] + jnp.dot(merged.astype(BF16), wo_ref[...], preferred_element_type=F32)
    o_ref[...] = x2
    _route(x2, gn_ref, wr_ref, br_ref, hp_ref, ri_ref, rw_ref, cnt_ref, carry_ref)


def _merge(y_ml, y_fx, y_ca, proj, x2d, w_branch, w_out, g_moe, w_router, b_router, part):
    T = x2d.shape[0] // MOE_PARTS
    tm = MERGE_TM
    off = part * (T // tm)
    src = lambda i: (off + i, 0)
    row = lambda i: (i, 0)
    const = lambda i: (0, 0)
    return pl.pallas_call(
        _merge_body,
        grid=(T // tm,),
        in_specs=[
            pl.BlockSpec((tm, D_MODEL), src),
            pl.BlockSpec((tm, D_MODEL), src),
            pl.BlockSpec((tm, D_MODEL), src),
            pl.BlockSpec((tm, D_MODEL), lambda i: (off + i, COL_GATE0)),
            pl.BlockSpec((tm, D_MODEL), lambda i: (off + i, COL_GATE0 + 1)),
            pl.BlockSpec((tm, D_MODEL), lambda i: (off + i, COL_GATE0 + 2)),
            pl.BlockSpec((tm, D_MODEL), src),
            pl.BlockSpec((3, D_MODEL, D_MODEL), lambda i: (0, 0, 0)),
            pl.BlockSpec((D_MODEL, D_MODEL), const),
            pl.BlockSpec((1, D_MODEL), const),
            pl.BlockSpec((N_EXPERTS, D_MODEL), const),
            pl.BlockSpec((N_EXPERTS, 1), const),
        ],
        out_specs=[
            pl.BlockSpec((tm, D_MODEL), row),
            pl.BlockSpec((tm, HALF), row),
            pl.BlockSpec((2 * TOP_K, tm), lambda i: (0, i)),
            pl.BlockSpec((2 * TOP_K, tm), lambda i: (0, i)),
            pl.BlockSpec((N_EXPERTS, 1), const),
        ],
        out_shape=[
            jax.ShapeDtypeStruct((T, D_MODEL), F32),
            jax.ShapeDtypeStruct((T, HALF), I32),
            jax.ShapeDtypeStruct((2 * TOP_K, T), I32),
            jax.ShapeDtypeStruct((2 * TOP_K, T), F32),
            jax.ShapeDtypeStruct((N_EXPERTS, 1), F32),
        ],
        scratch_shapes=[pltpu.VMEM((N_EXPERTS, 1), F32)],
        compiler_params=_cparams(("arbitrary",)),
        name="merge_router",
    )(y_ml, y_fx, y_ca, proj, proj, proj, x2d, w_branch, w_out, g_moe, w_router, b_router)


def _route(x2, g_ref, wr_ref, br_ref, hp_ref, ri_ref, rw_ref, cnt_ref, carry_ref):
    tm = MERGE_TM

    @pl.when(pl.program_id(0) == 0)
    def _():
        carry_ref[...] = jnp.zeros_like(carry_ref)

    h = _rms(x2, g_ref[...])
    hp_ref[...] = _pack_rows(h)
    logits = lax.dot_general(wr_ref[...], h.astype(BF16), (((1,), (1,)), ((), ())),
                             preferred_element_type=F32) + br_ref[...]
    eid = lax.broadcasted_iota(I32, (N_EXPERTS, tm), 0).astype(F32)

    work = logits
    onehot_sum = jnp.zeros((N_EXPERTS, tm), F32)
    vals, sels, idxs = [], [], []
    for _ in range(TOP_K):
        mx = jnp.max(work, axis=0, keepdims=True)
        idx = jnp.min(jnp.where(work == mx, eid, float(N_EXPERTS)), axis=0, keepdims=True)
        sel = eid == idx
        onehot_sum = onehot_sum + sel.astype(F32)
        work = jnp.where(sel, -jnp.inf, work)
        vals.append(mx)
        sels.append(sel)
        idxs.append(idx)
    exps = [jnp.exp(v - vals[0]) for v in vals]
    total = exps[0] + exps[1] + exps[2] + exps[3]

    earlier = (lax.broadcasted_iota(I32, (tm, tm), 0) < lax.broadcasted_iota(I32, (tm, tm), 1)).astype(BF16)
    before = jnp.dot(onehot_sum.astype(BF16), earlier, preferred_element_type=F32) + carry_ref[...]
    carry_ref[...] = carry_ref[...] + jnp.sum(onehot_sum, axis=1, keepdims=True)
    cnt_ref[...] = carry_ref[...]

    out_row = lax.broadcasted_iota(I32, (2 * TOP_K, tm), 0)
    ri = jnp.zeros((2 * TOP_K, tm), I32)
    rw = jnp.zeros((2 * TOP_K, tm), F32)
    for k in range(TOP_K):
        rank = jnp.sum(jnp.where(sels[k], before, 0.0), axis=0, keepdims=True)
        ri = jnp.where(out_row == k, idxs[k].astype(I32), ri)
        ri = jnp.where(out_row == TOP_K + k, rank.astype(I32), ri)
        rw = jnp.where(out_row == k, exps[k] / total, rw)
    ri_ref[...] = ri
    rw_ref[...] = rw


EXPERT_TM = 512
SC_CORES = 2
SC_SUBCORES = 16
SC_WORKERS = SC_CORES * SC_SUBCORES
SC_CHUNK = 64
PAD_SLOTS = N_EXPERTS * EXPERT_TM


def _sc_mesh():
    return plsc.VectorSubcoreMesh(core_axis_name="c", subcore_axis_name="s")


def _sc_worker():
    return lax.axis_index("s") * SC_CORES + lax.axis_index("c")


def _scatter_indices(dest):
    T = dest.shape[1]
    n_ch = T // (SC_WORKERS * SC_CHUNK)
    idx = dest.reshape(TOP_K, SC_WORKERS, n_ch, SC_CHUNK).transpose(1, 2, 0, 3)
    return idx.reshape(SC_WORKERS, n_ch * TOP_K, SC_CHUNK)


def _sc_dispatch(hp, idx, pad_idx, n_rows):
    T = hp.shape[0]
    per_w = T // SC_WORKERS
    n_ch = per_w // SC_CHUNK
    n_pc = PAD_SLOTS // (SC_WORKERS * SC_CHUNK)
    assert per_w % SC_CHUNK == 0 and n_ch >= 2 and n_ch % 2 == 0
    pidx = pad_idx.reshape(SC_WORKERS, n_pc, SC_CHUNK)
    zeros = jnp.zeros((SC_CHUNK, HALF), I32)

    @functools.partial(
        pl.kernel, mesh=_sc_mesh(),
        out_type=jax.ShapeDtypeStruct((n_rows, HALF), I32),
        scratch_types=[
            pltpu.VMEM((n_ch * TOP_K, SC_CHUNK), I32),
            pltpu.VMEM((n_pc, SC_CHUNK), I32),
            pltpu.VMEM((2, SC_CHUNK, HALF), I32),
            pltpu.SemaphoreType.DMA((2,)),
            pltpu.SemaphoreType.DMA((2,)),
        ],
        name="sc_dispatch",
    )
    def k(hp_hbm, idx_hbm, pidx_hbm, zeros_hbm, xs_hbm, idx_v, pidx_v, rows_v, lsem, ssem):
        wid = _sc_worker()
        base = wid * per_w
        pltpu.sync_copy(idx_hbm.at[wid], idx_v)
        pltpu.sync_copy(pidx_hbm.at[wid], pidx_v)

        pltpu.sync_copy(zeros_hbm, rows_v.at[0])
        for p in range(n_pc):
            pltpu.make_async_copy(rows_v.at[0], xs_hbm.at[pidx_v.at[p]], ssem.at[0]).start()
        for p in range(n_pc):
            pltpu.make_async_copy(rows_v.at[0], xs_hbm.at[pidx_v.at[p]], ssem.at[0]).wait()

        def load(i, slot):
            return pltpu.make_async_copy(hp_hbm.at[pl.ds(base + i * SC_CHUNK, SC_CHUNK)], rows_v.at[slot],
                                         lsem.at[slot])

        def scatter(i, kk, slot):
            return pltpu.make_async_copy(rows_v.at[slot], xs_hbm.at[idx_v.at[i * TOP_K + kk]], ssem.at[slot])

        load(0, 0).start()

        def body(i2, carry):
            for slot in range(2):
                i = i2 * 2 + slot
                nxt = 1 - slot

                @pl.when(i + 1 < n_ch)
                def _():
                    @pl.when(i >= 1)
                    def _():
                        for kk in range(TOP_K):
                            scatter(i - 1, kk, nxt).wait()
                    load(i + 1, nxt).start()

                load(i, slot).wait()
                for kk in range(TOP_K):
                    scatter(i, kk, slot).start()
            return carry

        lax.fori_loop(0, n_ch // 2, body, 0)
        for kk in range(TOP_K):
            scatter(n_ch - 2, kk, 0).wait()
            scatter(n_ch - 1, kk, 1).wait()

    return k(hp, idx, pidx, zeros)


def _sc_gather(table, idx):
    n = idx.shape[0]
    per_w = n // SC_WORKERS
    n_ch = per_w // SC_CHUNK
    assert per_w % SC_CHUNK == 0 and n_ch >= 2 and n_ch % 2 == 0

    @functools.partial(
        pl.kernel, mesh=_sc_mesh(),
        out_type=jax.ShapeDtypeStruct((n, HALF), I32),
        scratch_types=[
            pltpu.VMEM((n_ch, SC_CHUNK), I32),
            pltpu.VMEM((2, SC_CHUNK, HALF), I32),
            pltpu.SemaphoreType.DMA((2,)),
            pltpu.SemaphoreType.DMA((2,)),
        ],
        name="sc_gather",
    )
    def k(table_hbm, idx_hbm, out_hbm, idx_v, rows_v, gsem, wsem):
        wid = _sc_worker()
        base = wid * per_w
        pltpu.sync_copy(idx_hbm.at[wid], idx_v)

        def gather(i, slot):
            return pltpu.make_async_copy(table_hbm.at[idx_v.at[i]], rows_v.at[slot], gsem.at[slot])

        def writeback(i, slot):
            return pltpu.make_async_copy(rows_v.at[slot], out_hbm.at[pl.ds(base + i * SC_CHUNK, SC_CHUNK)],
                                         wsem.at[slot])

        gather(0, 0).start()

        def body(i2, carry):
            for slot in range(2):
                i = i2 * 2 + slot
                nxt = 1 - slot

                @pl.when(i + 1 < n_ch)
                def _():
                    @pl.when(i >= 1)
                    def _():
                        writeback(i - 1, nxt).wait()
                    gather(i + 1, nxt).start()

                gather(i, slot).wait()
                writeback(i, slot).start()
            return carry

        lax.fori_loop(0, n_ch // 2, body, 0)
        writeback(n_ch - 2, 0).wait()
        writeback(n_ch - 1, 1).wait()

    return k(table, idx.reshape(SC_WORKERS, n_ch, SC_CHUNK))


FF_CHUNK = 512


def _expert_body(te_ref, nv_ref, x_ref, w1f_ref, b1_ref, w2f_ref, b2_ref, y_ref, w1_ref, w2_ref):
    i = pl.program_id(0)

    @pl.when(jnp.logical_or(i == 0, te_ref[i] != te_ref[jnp.maximum(i - 1, 0)]))
    def _():
        w1_ref[...] = w1f_ref[...].astype(BF16)
        w2_ref[...] = w2f_ref[...].astype(BF16)

    @pl.when(i < nv_ref[0])
    def _():
        lo, hi = _unpack_rows(x_ref[...])
        xlo = lo.astype(BF16)
        xhi = hi.astype(BF16)
        acc = jnp.zeros((EXPERT_TM, D_MODEL), F32) + b2_ref[...]
        for c in range(D_FF // FF_CHUNK):
            def up(off):
                cs = slice(off + c * FF_CHUNK, off + (c + 1) * FF_CHUNK)
                return (jnp.dot(xlo, w1_ref[0:HALF, cs], preferred_element_type=F32)
                        + jnp.dot(xhi, w1_ref[HALF:D_MODEL, cs], preferred_element_type=F32)
                        + b1_ref[:, cs])
            g = jnp.minimum(up(0), SWIGLU_LIMIT)
            lin = jnp.clip(up(D_FF), -SWIGLU_LIMIT, SWIGLU_LIMIT)
            a = g * jax.nn.sigmoid(SWIGLU_ALPHA * g) * (lin + 1.0)
            acc = acc + jnp.dot(a.astype(BF16), w2_ref[c * FF_CHUNK:(c + 1) * FF_CHUNK, :],
                                preferred_element_type=F32)
        y_ref[...] = _pack_rows(acc)


def _experts(tile_expert, n_valid, xs, w1, b1, w2, b2):
    n_rows = xs.shape[0]
    tm = EXPERT_TM
    n_tiles = n_rows // tm
    row = lambda i, te, nv: (jnp.minimum(i, nv[0] - 1), 0)
    grid_spec = pltpu.PrefetchScalarGridSpec(
        num_scalar_prefetch=2,
        grid=(n_tiles,),
        in_specs=[
            pl.BlockSpec((tm, HALF), row),
            pl.BlockSpec((None, D_MODEL, 2 * D_FF), lambda i, te, nv: (te[i], 0, 0)),
            pl.BlockSpec((None, 1, 2 * D_FF), lambda i, te, nv: (te[i], 0, 0)),
            pl.BlockSpec((None, D_FF, D_MODEL), lambda i, te, nv: (te[i], 0, 0)),
            pl.BlockSpec((None, 1, D_MODEL), lambda i, te, nv: (te[i], 0, 0)),
        ],
        out_specs=pl.BlockSpec((tm, HALF), row),
        scratch_shapes=[pltpu.VMEM((D_MODEL, 2 * D_FF), BF16), pltpu.VMEM((D_FF, D_MODEL), BF16)],
    )
    return pl.pallas_call(
        _expert_body,
        grid_spec=grid_spec,
        out_shape=jax.ShapeDtypeStruct((n_rows, HALF), I32),
        compiler_params=_cparams(("arbitrary",)),
        name="experts",
    )(tile_expert, n_valid, xs, w1, b1, w2, b2)


COMBINE_TM = 512


def _combine_body(y0_ref, y1_ref, y2_ref, y3_ref, rw_ref, x_ref, g_ref, *rest):
    o_ref = rest[-1]
    acc = x_ref[...]
    rw = jnp.concatenate([rw_ref[...], jnp.zeros((LANES - 2 * TOP_K, COMBINE_TM), F32)], axis=0).T
    for k, y_ref in enumerate((y0_ref, y1_ref, y2_ref, y3_ref)):
        lo, hi = _unpack_rows(y_ref[...])
        acc = acc + rw[:, k:k + 1] * jnp.concatenate([lo, hi], axis=-1)
    o_ref[...] = _rms(acc, g_ref[...])


def _combine(yg, rw, x2, g, part, out_prev):
    T = x2.shape[0]
    tm = COMBINE_TM
    nt = T // tm
    in_specs = [
        pl.BlockSpec((tm, HALF), lambda i: (i, 0)),
        pl.BlockSpec((tm, HALF), lambda i: (nt + i, 0)),
        pl.BlockSpec((tm, HALF), lambda i: (2 * nt + i, 0)),
        pl.BlockSpec((tm, HALF), lambda i: (3 * nt + i, 0)),
        pl.BlockSpec((2 * TOP_K, tm), lambda i: (0, i)),
        pl.BlockSpec((tm, D_MODEL), lambda i: (i, 0)),
        pl.BlockSpec((1, D_MODEL), lambda i: (0, 0)),
    ]
    args = [yg, yg, yg, yg, rw, x2, g]
    aliases = {}
    if out_prev is not None:
        in_specs.append(pl.BlockSpec(memory_space=pl.ANY))
        args.append(out_prev)
        aliases = {len(args) - 1: 0}
    return pl.pallas_call(
        _combine_body,
        grid=(nt,),
        in_specs=in_specs,
        out_specs=pl.BlockSpec((tm, D_MODEL), lambda i: (part * nt + i, 0)),
        out_shape=jax.ShapeDtypeStruct((T * MOE_PARTS, D_MODEL), F32),
        input_output_aliases=aliases,
        compiler_params=_cparams(("parallel",)),
        name="combine",
    )(*args)


def _pad_lanes(v):
    v = v.reshape(1, -1).astype(F32)
    return jnp.pad(v, ((0, 0), (0, LANES - v.shape[1])))


def _layer(x2d, mem2d, B, S, M, norm_mix, w_in, b_ml_gates, conv_ml, ml_head_norm, b_fx_gate, norm_mem,
           w_mem_kv, w_branch, w_out, norm_moe, w_router, b_router, w_exp_in, b_exp_in, w_exp_out,
           b_exp_out, norm_out):
    T = B * S
    w16 = w_in.astype(BF16)
    w_big = jnp.concatenate([w16[:, 0:2048], w16[:, 2056:3080], w16[:, 3080:6152], w16[:, 6160:7184],
                             w16[:, 7184:10256]], axis=1)
    w_small = jnp.concatenate([w16[:, 2048:2056], w16[:, 6152:6160]], axis=1)
    w_small = jnp.pad(w_small, ((0, 0), (0, LANES - w_small.shape[1])))
    row = lambda v: v.reshape(1, -1).astype(F32)

    proj, small = _inproj(x2d, row(norm_mix), w_big, w_small)

    y_ml = _mlstm(proj, small, conv_ml.astype(F32), _pad_lanes(b_ml_gates), row(ml_head_norm), B, S)

    b_fx = jnp.pad(b_fx_gate.reshape(1, -1).astype(F32), ((0, 0), (2 * ML_HEADS, LANES - 2 * ML_HEADS - FX_HEADS)))
    y_fx = _fox_attn(proj, _fox_gate(small, b_fx, B, S), B, S)

    kv = _memkv(mem2d, row(norm_mem), w_mem_kv.astype(BF16))
    y_ca = _memattn(proj, kv, B, S, M)

    w_r = w_router.T.astype(BF16)
    moe_weights = (w_exp_in.astype(F32), b_exp_in.reshape(N_EXPERTS, 1, -1).astype(F32), w_exp_out.astype(F32),
                   b_exp_out.reshape(N_EXPERTS, 1, -1).astype(F32))
    staged = []
    for part in range(MOE_PARTS):
        x2, hp, ri, rw, cnt = _merge(y_ml, y_fx, y_ca, proj, x2d, w_branch.astype(BF16), w_out.astype(BF16),
                                     row(norm_moe), w_r, b_router.reshape(N_EXPERTS, 1).astype(F32), part)
        scatter_idx, dest, pad_idx, tile_e, n_valid, n_rows = _moe_plan(ri, cnt)
        staged.append((x2, rw, dest, tile_e, n_valid, _sc_dispatch(hp, scatter_idx, pad_idx, n_rows)))
    gathered = [_sc_gather(_experts(tile_e, n_valid, xs, *moe_weights), dest.reshape(-1))
                for _, _, dest, tile_e, n_valid, xs in staged]
    out = None
    for part, ((x2, rw, *_), yg) in enumerate(zip(staged, gathered)):
        out = _combine(yg, rw, x2, row(norm_out), part, out)
    return out


def _moe_plan(ri, cnt):
    T = ri.shape[1]
    tm = EXPERT_TM
    n_tiles = (T * TOP_K) // tm + N_EXPERTS
    counts = cnt[:, 0].astype(I32)
    padded = ((counts + tm - 1) // tm) * tm
    gend = jnp.cumsum(padded)
    gstart = gend - padded
    expert_ids = jnp.arange(N_EXPERTS, dtype=I32)
    start_of = jnp.sum(jnp.where(ri[0:TOP_K, :, None] == expert_ids, gstart, 0), axis=-1)
    dest = start_of + ri[TOP_K:2 * TOP_K, :]
    n_valid = gend[-1] // tm
    tile_ids = jnp.arange(n_tiles, dtype=I32)
    last_tile = jnp.minimum(tile_ids, n_valid - 1)
    tile_e = jnp.minimum(jnp.sum((gend[None, :] <= last_tile[:, None] * tm).astype(I32), axis=1), N_EXPERTS - 1)

    slot = jnp.arange(tm, dtype=I32)
    spare = n_tiles * tm + slot % SC_CHUNK
    pad_idx = jnp.where(slot[None, :] < (padded - counts)[:, None], (gstart + counts)[:, None] + slot[None, :],
                        spare[None, :]).reshape(-1)

    return (_scatter_indices(dest), dest, pad_idx, tile_e.astype(I32), n_valid.reshape(1).astype(I32),
            n_tiles * tm + SC_CHUNK)


def kernel(x, mem, norm_mix, w_in, b_ml_gates, conv_ml, ml_head_norm, b_fx_gate, norm_mem, w_mem_kv, w_branch,
           w_out, norm_moe, w_router, b_router, w_exp_in, b_exp_in, w_exp_out, b_exp_out, norm_final):
    B, S, D = x.shape
    M = mem.shape[1]
    depth = norm_mix.shape[0]
    assert depth == 1, "the combine kernel fuses the final norm, so exactly one layer is supported"
    assert D == D_MODEL and S % ML_BLOCK == 0 and S % FX_T == 0 and S % CA_TQ == 0
    out = _layer(x.reshape(B * S, D), mem.reshape(B * M, D), B, S, M, norm_mix[0], w_in[0], b_ml_gates[0],
                 conv_ml[0], ml_head_norm[0], b_fx_gate[0], norm_mem[0], w_mem_kv[0], w_branch[0], w_out[0],
                 norm_moe[0], w_router[0], b_router[0], w_exp_in[0], b_exp_in[0], w_exp_out[0], b_exp_out[0],
                 norm_final)
    return out.reshape(B, S, D)
```

```python
import functools

import jax
import jax.numpy as jnp
from jax import lax
from jax.experimental import pallas as pl
from jax.experimental.pallas import tpu as pltpu
from jax.experimental.pallas import tpu_sc as plsc

F32 = jnp.float32
BF16 = jnp.bfloat16
I32 = jnp.int32

D_MODEL = 1024
N_MEM_HEADS = 4
ML_HEADS = 4
ML_DQK = 128
ML_DV = 256
ML_CONV = 4
FX_HEADS = 8
FX_DH = 128
CA_HEADS = 4
CA_DH = 256
N_EXPERTS = 32
TOP_K = 4
D_FF = D_MODEL
SWIGLU_LIMIT = 7.0
SWIGLU_ALPHA = 1.702
EPS = 1e-5
LANES = 128
HALF = D_MODEL // 2
HI_MASK = -65536

COL_MLQK, COL_MLV, COL_MLO, COL_FXQ, COL_FXK, COL_FXV, COL_CAQ, COL_GATE0 = 0, 1, 2, 3, 4, 5, 6, 7
N_BIG = 10 * D_MODEL

VMEM_LIMIT = 56 * 1024 * 1024


def _cparams(sem):
    return pltpu.CompilerParams(dimension_semantics=sem, vmem_limit_bytes=VMEM_LIMIT)


def _rms(x, g):
    return x * lax.rsqrt(jnp.mean(x * x, axis=-1, keepdims=True) + EPS) * g


def _log_sigmoid(x):
    return jnp.minimum(x, 0.0) - jnp.log1p(jnp.exp(-jnp.abs(x)))


def _pack_rows(y):
    bits = lax.bitcast_convert_type(y.astype(BF16).astype(F32), I32)
    return lax.shift_right_logical(bits[:, :HALF], 16) | (bits[:, HALF:] & HI_MASK)


def _unpack_rows(w):
    lo = lax.bitcast_convert_type(lax.shift_left(w, 16), F32)
    hi = lax.bitcast_convert_type(w & HI_MASK, F32)
    return lo, hi


def _inproj_body(x_ref, g_ref, w_ref, ws_ref, o_ref, os_ref, h_ref):
    @pl.when(pl.program_id(1) == 0)
    def _():
        hb = _rms(x_ref[...], g_ref[...]).astype(BF16)
        h_ref[...] = hb
        os_ref[...] = jnp.dot(hb, ws_ref[...], preferred_element_type=F32)

    o_ref[...] = jnp.dot(h_ref[...], w_ref[...], preferred_element_type=F32).astype(BF16)


def _inproj(x2d, g, w_big, w_small):
    T = x2d.shape[0]
    tm = min(1024, T)
    tn = 2560
    return pl.pallas_call(
        _inproj_body,
        grid=(T // tm, N_BIG // tn),
        in_specs=[
            pl.BlockSpec((tm, D_MODEL), lambda i, j: (i, 0)),
            pl.BlockSpec((1, D_MODEL), lambda i, j: (0, 0)),
            pl.BlockSpec((D_MODEL, tn), lambda i, j: (0, j)),
            pl.BlockSpec((D_MODEL, LANES), lambda i, j: (0, 0)),
        ],
        out_specs=[
            pl.BlockSpec((tm, tn), lambda i, j: (i, j)),
            pl.BlockSpec((tm, LANES), lambda i, j: (i, 0)),
        ],
        out_shape=[
            jax.ShapeDtypeStruct((T, N_BIG), BF16),
            jax.ShapeDtypeStruct((T, LANES), F32),
        ],
        scratch_shapes=[pltpu.VMEM((tm, D_MODEL), BF16)],
        compiler_params=_cparams(("parallel", "arbitrary")),
        name="inproj",
    )(x2d, g, w_big, w_small)


ML_BLOCK = 1024
ML_MB = 1
ML_CHUNK = 128
CONV_PAD = 8


def _mlstm_body(qk_ref, v_ref, o_ref, g_ref, cw_ref, bg_ref, hn_ref, y_ref, xbuf, c_st, n_st, m_st):
    L = ML_CHUNK

    @pl.when(pl.program_id(1) == 0)
    def _():
        xbuf[:, 0:CONV_PAD, :] = jnp.zeros((ML_MB, CONV_PAD, D_MODEL), F32)
        c_st[...] = jnp.zeros_like(c_st)
        n_st[...] = jnp.zeros_like(n_st)
        m_st[...] = jnp.zeros_like(m_st)

    for bb in range(ML_MB):
        xbuf[bb, CONV_PAD:CONV_PAD + ML_BLOCK, :] = qk_ref[bb].astype(F32)
    cw = cw_ref[...]
    row = lax.broadcasted_iota(I32, (L, L), 0)
    col = lax.broadcasted_iota(I32, (L, L), 1)
    tri = (row >= col).astype(BF16)
    causal_t = col >= row
    bg = bg_ref[...]
    scale = ML_DQK ** -0.5
    nt_dims = (((1,), (1,)), ((), ()))

    def chunk(bb, c):
        r0 = c * L
        conv = cw[0:1, :] * xbuf[bb, r0 + CONV_PAD - 3:r0 + CONV_PAD - 3 + L, :]
        for j in range(1, ML_CONV):
            s0 = r0 + CONV_PAD - 3 + j
            conv = conv + cw[j:j + 1, :] * xbuf[bb, s0:s0 + L, :]
        act = conv * jax.nn.sigmoid(conv)

        gates = g_ref[bb, r0:r0 + L, :] + bg
        lf = _log_sigmoid(gates)
        cum = jnp.zeros((L, LANES), F32)
        for _ in range(3):
            piece = lf.astype(BF16)
            cum = cum + jnp.dot(tri, piece, preferred_element_type=F32)
            lf = lf - piece.astype(F32)
        gates_t = gates.T
        cum_t = cum.T
        for h in range(ML_HEADS):
            b_row = cum_t[ML_HEADS + h:ML_HEADS + h + 1, :]
            i_row = gates_t[h:h + 1, :]
            a_col = gates[:, h:h + 1] - cum[:, ML_HEADS + h:ML_HEADS + h + 1]
            st = bb * ML_HEADS + h
            m_prev = m_st[st]
            dm = jnp.where(causal_t, a_col + b_row, -jnp.inf)
            m_inter = b_row + m_prev
            m_t = jnp.maximum(jnp.max(dm, axis=0, keepdims=True), m_inter)
            w_intra = jnp.exp(dm - m_t)
            w_inter = jnp.exp(m_inter - m_t)

            qb = (act[:, h * ML_DQK:(h + 1) * ML_DQK] * scale).astype(BF16)
            kb = act[:, (ML_HEADS + h) * ML_DQK:(ML_HEADS + h + 1) * ML_DQK].astype(BF16)
            v_t = v_ref[bb, r0:r0 + L, h * ML_DV:(h + 1) * ML_DV].astype(F32).T
            p_t = lax.dot_general(kb, qb, nt_dims, preferred_element_type=F32) * w_intra
            c_old = c_st[st]
            n_old = n_st[st]
            num = jnp.dot(v_t.astype(BF16), p_t.astype(BF16), preferred_element_type=F32) + w_inter * (
                lax.dot_general(c_old.astype(BF16), qb, nt_dims, preferred_element_type=F32))
            qn = lax.dot_general(jnp.broadcast_to(n_old, (8, ML_DQK)).astype(BF16), qb, nt_dims,
                                 preferred_element_type=F32)[0:1, :]
            den = jnp.sum(p_t, axis=0, keepdims=True) + w_inter * qn
            hv = num / jnp.maximum(jnp.abs(den), jnp.exp(-m_t))

            m_new = m_t[:, L - 1:L]
            b_last = b_row[:, L - 1:L]
            wk = jnp.exp(b_last - b_row + i_row - m_new)
            decay = jnp.exp(b_last + m_prev - m_new)
            c_st[st] = decay * c_old + jnp.dot((v_t * wk).astype(BF16), kb, preferred_element_type=F32)
            n_st[st] = decay * n_old + jnp.dot(jnp.broadcast_to(wk, (8, L)).astype(BF16), kb,
                                               preferred_element_type=F32)[0:1, :]
            m_st[st] = m_new

            hn = (hv * lax.rsqrt(jnp.mean(hv * hv, axis=0, keepdims=True) + EPS)).T
            og = o_ref[bb, r0:r0 + L, h * ML_DV:(h + 1) * ML_DV].astype(F32)
            y_ref[bb, r0:r0 + L, h * ML_DV:(h + 1) * ML_DV] = (
                hn * hn_ref[:, h * ML_DV:(h + 1) * ML_DV] * jax.nn.sigmoid(og)).astype(BF16)

    for c in range(ML_BLOCK // L):
        for bb in range(ML_MB):
            chunk(bb, c)

    xbuf[:, 0:CONV_PAD, :] = xbuf[:, ML_BLOCK:ML_BLOCK + CONV_PAD, :]


def _mlstm(proj, small, conv_w, b_gates, head_norm, B, S):
    T = B * S
    ns = S // ML_BLOCK
    assert B % ML_MB == 0
    proj3 = proj.reshape(B, S, N_BIG)
    blk = lambda col: pl.BlockSpec((ML_MB, ML_BLOCK, D_MODEL), lambda b, s: (b, s, col))
    out = pl.pallas_call(
        _mlstm_body,
        grid=(B // ML_MB, ns),
        in_specs=[
            blk(COL_MLQK),
            blk(COL_MLV),
            blk(COL_MLO),
            pl.BlockSpec((ML_MB, ML_BLOCK, LANES), lambda b, s: (b, s, 0)),
            pl.BlockSpec((ML_CONV, D_MODEL), lambda b, s: (0, 0)),
            pl.BlockSpec((1, LANES), lambda b, s: (0, 0)),
            pl.BlockSpec((1, D_MODEL), lambda b, s: (0, 0)),
        ],
        out_specs=blk(0),
        out_shape=jax.ShapeDtypeStruct((B, S, D_MODEL), BF16),
        scratch_shapes=[
            pltpu.VMEM((ML_MB, ML_BLOCK + CONV_PAD, D_MODEL), F32),
            pltpu.VMEM((ML_MB * ML_HEADS, ML_DV, ML_DQK), F32),
            pltpu.VMEM((ML_MB * ML_HEADS, 1, ML_DQK), F32),
            pltpu.VMEM((ML_MB * ML_HEADS, 1, 1), F32),
        ],
        compiler_params=_cparams(("parallel", "arbitrary")),
        name="mlstm",
    )(proj3, proj3, proj3, small.reshape(B, S, LANES), conv_w, b_gates, head_norm)
    return out.reshape(T, D_MODEL)


FX_T = 512
FX_HP = 2
FX_VR = FX_DH + 16
LOG2E = 1.4426950408889634
N_PIECES = 3
FX_GATE_T = 128


def _fox_gate_body(g_ref, b_ref, o_ref):
    S = g_ref.shape[0]
    row = lax.broadcasted_iota(I32, (FX_GATE_T, FX_GATE_T), 0)
    col = lax.broadcasted_iota(I32, (FX_GATE_T, FX_GATE_T), 1)
    tri = (row >= col).astype(BF16)
    carry = jnp.zeros((1, LANES), F32)
    for blk in range(S // FX_GATE_T):
        rows = slice(blk * FX_GATE_T, (blk + 1) * FX_GATE_T)
        lf = _log_sigmoid(g_ref[rows, :] + b_ref[...])
        cum = carry
        for _ in range(N_PIECES):
            piece = lf.astype(BF16)
            cum = cum + jnp.dot(tri, piece, preferred_element_type=F32)
            lf = lf - piece.astype(F32)
        carry = cum[FX_GATE_T - 1:FX_GATE_T, :]
        o_ref[rows, :] = cum * (-LOG2E)


def _fox_gate(small, b_fx, B, S):
    return pl.pallas_call(
        _fox_gate_body,
        grid=(B,),
        in_specs=[
            pl.BlockSpec((S, LANES), lambda b: (b, 0)),
            pl.BlockSpec((1, LANES), lambda b: (0, 0)),
        ],
        out_specs=pl.BlockSpec((S, LANES), lambda b: (b, 0)),
        out_shape=jax.ShapeDtypeStruct((B * S, LANES), F32),
        compiler_params=_cparams(("parallel",)),
        name="fox_gate",
    )(small, b_fx)


def _fox_attn_body(q_ref, k_ref, v_ref, c_ref, o_ref, kx_ref, vt_ref, m_ref, acc_ref, s_ref):
    S = k_ref.shape[0]
    nq = S // FX_T

    c = c_ref[...]
    hi = c.astype(BF16)
    r1 = c - hi.astype(F32)
    mid = r1.astype(BF16)
    lo = (r1 - mid.astype(F32)).astype(BF16)
    sel_row = lax.broadcasted_iota(I32, (LANES, LANES), 0)
    sel_col = lax.broadcasted_iota(I32, (LANES, LANES), 1)
    ones_rows = (lax.broadcasted_iota(I32, (FX_VR - FX_DH, FX_T), 0) == 0).astype(BF16)
    head_slices = [slice(hh * FX_DH, (hh + 1) * FX_DH) for hh in range(FX_HP)]
    for hh, sl in enumerate(head_slices):
        lane = 2 * ML_HEADS + pl.program_id(1) * FX_HP + hh
        pieces = None
        for p, part in enumerate((hi, mid, lo)):
            pick = jnp.logical_and(sel_row == lane, sel_col == p).astype(BF16)
            t = jnp.dot(part, pick, preferred_element_type=F32)
            pieces = t if pieces is None else pieces + t
        kx_ref[hh, :, 0:FX_DH] = k_ref[:, sl]
        kx_ref[hh, :, FX_DH:2 * FX_DH] = pieces.astype(BF16)
        for j in range(nq):
            vt = v_ref[j * FX_T:(j + 1) * FX_T, sl].astype(F32).T.astype(BF16)
            vt_ref[hh, j] = jnp.concatenate([vt, ones_rows], axis=0)

    piece_rows = (lax.broadcasted_iota(I32, (FX_DH, FX_T), 0) < N_PIECES).astype(BF16)

    def start(i):
        q_x = []
        for sl in head_slices:
            q_t = (q_ref[i * FX_T:(i + 1) * FX_T, sl].astype(F32) * (FX_DH ** -0.5 * LOG2E)).T.astype(BF16)
            q_x.append(jnp.concatenate([q_t, piece_rows], axis=0))
        m_ref[i % 2] = jnp.full(m_ref.shape[1:], -jnp.inf, F32)
        acc_ref[i % 2] = jnp.zeros(acc_ref.shape[1:], F32)
        return q_x

    def key_rows(j):
        return pl.ds(j * FX_T, FX_T) if isinstance(j, int) else pl.ds(pl.multiple_of(j * FX_T, FX_T), FX_T)

    def scores(q_x, j, slot):
        for hh in range(FX_HP):
            s_ref[slot, hh] = jnp.dot(kx_ref[hh, key_rows(j), :], q_x[hh], preferred_element_type=F32)

    def consume(par, j, slot, masked):
        for hh in range(FX_HP):
            s = s_ref[slot, hh]
            if masked:
                key = lax.broadcasted_iota(I32, (FX_T, FX_T), 0)
                qry = lax.broadcasted_iota(I32, (FX_T, FX_T), 1)
                s = jnp.where(qry >= key, s, -jnp.inf)
            m_old = m_ref[par, hh]
            m_new = jnp.maximum(m_old, jnp.max(s, axis=0, keepdims=True))
            p = jnp.exp2(s - m_new).astype(BF16)
            acc_ref[par, hh] = jnp.exp2(m_old - m_new) * acc_ref[par, hh] + jnp.dot(
                vt_ref[hh, j], p, preferred_element_type=F32)
            m_ref[par, hh] = m_new

    def finish(i, slot):
        consume(i % 2, i, slot, True)
        for hh, sl in enumerate(head_slices):
            acc = acc_ref[i % 2, hh]
            o_ref[i * FX_T:(i + 1) * FX_T, sl] = (acc[0:FX_DH, :] / acc[FX_DH:FX_DH + 1, :]).T.astype(BF16)

    diag_slot = 0
    for i in range(nq):
        q_x = start(i)
        first = 0 if i == 0 else 1 - diag_slot
        scores(q_x, 0, first)
        if i > 0:
            finish(i - 1, diag_slot)

        def pair(jj, carry, q_x=q_x, first=first, par=i % 2):
            j = 2 * jj
            scores(q_x, j + 1, 1 - first)
            consume(par, j, first, False)
            scores(q_x, j + 2, first)
            consume(par, j + 1, 1 - first, False)
            return carry

        if i >= 2:
            lax.fori_loop(0, i // 2, pair, 0)
        if i % 2 == 1:
            scores(q_x, i, 1 - first)
            consume(i % 2, i - 1, first, False)
            diag_slot = 1 - first
        else:
            diag_slot = first
    finish(nq - 1, diag_slot)


def _fox_attn(proj, c_neg, B, S):
    T = B * S
    nq = S // FX_T
    wide = FX_HP * FX_DH
    cq = COL_FXQ * (D_MODEL // wide)
    ck = COL_FXK * (D_MODEL // wide)
    cv = COL_FXV * (D_MODEL // wide)
    proj3 = proj.reshape(B, S, N_BIG)
    out = pl.pallas_call(
        _fox_attn_body,
        grid=(B, FX_HEADS // FX_HP),
        in_specs=[
            pl.BlockSpec((None, S, wide), lambda b, h: (b, 0, cq + h)),
            pl.BlockSpec((None, S, wide), lambda b, h: (b, 0, ck + h)),
            pl.BlockSpec((None, S, wide), lambda b, h: (b, 0, cv + h)),
            pl.BlockSpec((None, S, LANES), lambda b, h: (b, 0, 0)),
        ],
        out_specs=pl.BlockSpec((None, S, wide), lambda b, h: (b, 0, h)),
        out_shape=jax.ShapeDtypeStruct((B, S, D_MODEL), BF16),
        scratch_shapes=[
            pltpu.VMEM((FX_HP, S, 2 * FX_DH), BF16),
            pltpu.VMEM((FX_HP, nq, FX_VR, FX_T), BF16),
            pltpu.VMEM((2, FX_HP, 1, FX_T), F32),
            pltpu.VMEM((2, FX_HP, FX_VR, FX_T), F32),
            pltpu.VMEM((2, FX_HP, FX_T, FX_T), F32),
        ],
        compiler_params=_cparams(("parallel", "parallel")),
        name="fox_attn",
    )(proj3, proj3, proj3, c_neg.reshape(B, S, LANES))
    return out.reshape(T, D_MODEL)


def _memkv_body(x_ref, g_ref, w_ref, o_ref):
    hb = _rms(x_ref[...], g_ref[...]).astype(BF16)
    o_ref[...] = jnp.dot(hb, w_ref[...], preferred_element_type=F32).astype(BF16)


def _memkv(mem2d, g, w_kv):
    R = mem2d.shape[0]
    tm = min(512, R)
    N = w_kv.shape[1]
    return pl.pallas_call(
        _memkv_body,
        grid=(R // tm,),
        in_specs=[
            pl.BlockSpec((tm, D_MODEL), lambda i: (i, 0)),
            pl.BlockSpec((1, D_MODEL), lambda i: (0, 0)),
            pl.BlockSpec((D_MODEL, N), lambda i: (0, 0)),
        ],
        out_specs=pl.BlockSpec((tm, N), lambda i: (i, 0)),
        out_shape=jax.ShapeDtypeStruct((R, N), BF16),
        compiler_params=_cparams(("parallel",)),
        name="memkv",
    )(mem2d, g, w_kv)


CA_TQ = 1024


def _memattn_body(q_ref, k_ref, v_ref, o_ref):
    scale = CA_DH ** -0.5
    for h in range(CA_HEADS):
        sl = slice(h * CA_DH, (h + 1) * CA_DH)
        s = lax.dot_general(q_ref[:, sl], k_ref[:, sl], (((1,), (1,)), ((), ())),
                            preferred_element_type=F32) * scale
        p = jnp.exp(s - jnp.max(s, axis=-1, keepdims=True))
        l = jnp.sum(p, axis=-1, keepdims=True)
        o = jnp.dot(p.astype(BF16), v_ref[:, sl], preferred_element_type=F32) / l
        o_ref[:, sl] = o.astype(BF16)


def _memattn(proj, kv, B, S, M):
    T = B * S
    nq = S // CA_TQ
    kv3 = kv.reshape(B, M, 2 * D_MODEL)
    return pl.pallas_call(
        _memattn_body,
        grid=(B, nq),
        in_specs=[
            pl.BlockSpec((CA_TQ, D_MODEL), lambda b, i: (b * nq + i, COL_CAQ)),
            pl.BlockSpec((None, M, D_MODEL), lambda b, i: (b, 0, 0)),
            pl.BlockSpec((None, M, D_MODEL), lambda b, i: (b, 0, 1)),
        ],
        out_specs=pl.BlockSpec((CA_TQ, D_MODEL), lambda b, i: (b * nq + i, 0)),
        out_shape=jax.ShapeDtypeStruct((T, D_MODEL), BF16),
        compiler_params=_cparams(("parallel", "arbitrary")),
        name="memattn",
    )(proj, kv3, kv3)


MERGE_TM = 512
MOE_PARTS = 2


def _merge_body(y0_ref, y1_ref, y2_ref, g0_ref, g1_ref, g2_ref, x_ref, wb_ref, wo_ref, gn_ref, wr_ref, br_ref,
                o_ref, hp_ref, ri_ref, rw_ref, cnt_ref, carry_ref):
    merged = None
    for n, (y_ref, g_ref) in enumerate(((y0_ref, g0_ref), (y1_ref, g1_ref), (y2_ref, g2_ref))):
        p = jnp.dot(y_ref[...], wb_ref[n], preferred_element_type=F32)
        t = jax.nn.sigmoid(g_ref[...].astype(F32)) * p
        merged = t if merged is None else merged + t
    x2 = x_ref[...] + jnp.dot(merged.astype(BF16), wo_ref[...], preferred_element_type=F32)
    o_ref[...] = x2
    _route(x2, gn_ref, wr_ref, br_ref, hp_ref, ri_ref, rw_ref, cnt_ref, carry_ref)


def _merge(y_ml, y_fx, y_ca, proj, x2d, w_branch, w_out, g_moe, w_router, b_router, part):
    T = x2d.shape[0] // MOE_PARTS
    tm = MERGE_TM
    off = part * (T // tm)
    src = lambda i: (off + i, 0)
    row = lambda i: (i, 0)
    const = lambda i: (0, 0)
    return pl.pallas_call(
        _merge_body,
        grid=(T // tm,),
        in_specs=[
            pl.BlockSpec((tm, D_MODEL), src),
            pl.BlockSpec((tm, D_MODEL), src),
            pl.BlockSpec((tm, D_MODEL), src),
            pl.BlockSpec((tm, D_MODEL), lambda i: (off + i, COL_GATE0)),
            pl.BlockSpec((tm, D_MODEL), lambda i: (off + i, COL_GATE0 + 1)),
            pl.BlockSpec((tm, D_MODEL), lambda i: (off + i, COL_GATE0 + 2)),
            pl.BlockSpec((tm, D_MODEL), src),
            pl.BlockSpec((3, D_MODEL, D_MODEL), lambda i: (0, 0, 0)),
            pl.BlockSpec((D_MODEL, D_MODEL), const),
            pl.BlockSpec((1, D_MODEL), const),
            pl.BlockSpec((N_EXPERTS, D_MODEL), const),
            pl.BlockSpec((N_EXPERTS, 1), const),
        ],
        out_specs=[
            pl.BlockSpec((tm, D_MODEL), row),
            pl.BlockSpec((tm, HALF), row),
            pl.BlockSpec((2 * TOP_K, tm), lambda i: (0, i)),
            pl.BlockSpec((2 * TOP_K, tm), lambda i: (0, i)),
            pl.BlockSpec((N_EXPERTS, 1), const),
        ],
        out_shape=[
            jax.ShapeDtypeStruct((T, D_MODEL), F32),
            jax.ShapeDtypeStruct((T, HALF), I32),
            jax.ShapeDtypeStruct((2 * TOP_K, T), I32),
            jax.ShapeDtypeStruct((2 * TOP_K, T), F32),
            jax.ShapeDtypeStruct((N_EXPERTS, 1), F32),
        ],
        scratch_shapes=[pltpu.VMEM((N_EXPERTS, 1), F32)],
        compiler_params=_cparams(("arbitrary",)),
        name="merge_router",
    )(y_ml, y_fx, y_ca, proj, proj, proj, x2d, w_branch, w_out, g_moe, w_router, b_router)


def _route(x2, g_ref, wr_ref, br_ref, hp_ref, ri_ref, rw_ref, cnt_ref, carry_ref):
    tm = MERGE_TM

    @pl.when(pl.program_id(0) == 0)
    def _():
        carry_ref[...] = jnp.zeros_like(carry_ref)

    h = _rms(x2, g_ref[...])
    hp_ref[...] = _pack_rows(h)
    logits = lax.dot_general(wr_ref[...], h.astype(BF16), (((1,), (1,)), ((), ())),
                             preferred_element_type=F32) + br_ref[...]
    eid = lax.broadcasted_iota(I32, (N_EXPERTS, tm), 0).astype(F32)

    work = logits
    onehot_sum = jnp.zeros((N_EXPERTS, tm), F32)
    vals, sels, idxs = [], [], []
    for _ in range(TOP_K):
        mx = jnp.max(work, axis=0, keepdims=True)
        idx = jnp.min(jnp.where(work == mx, eid, float(N_EXPERTS)), axis=0, keepdims=True)
        sel = eid == idx
        onehot_sum = onehot_sum + sel.astype(F32)
        work = jnp.where(sel, -jnp.inf, work)
        vals.append(mx)
        sels.append(sel)
        idxs.append(idx)
    exps = [jnp.exp(v - vals[0]) for v in vals]
    total = exps[0] + exps[1] + exps[2] + exps[3]

    earlier = (lax.broadcasted_iota(I32, (tm, tm), 0) < lax.broadcasted_iota(I32, (tm, tm), 1)).astype(BF16)
    before = jnp.dot(onehot_sum.astype(BF16), earlier, preferred_element_type=F32) + carry_ref[...]
    carry_ref[...] = carry_ref[...] + jnp.sum(onehot_sum, axis=1, keepdims=True)
    cnt_ref[...] = carry_ref[...]

    out_row = lax.broadcasted_iota(I32, (2 * TOP_K, tm), 0)
    ri = jnp.zeros((2 * TOP_K, tm), I32)
    rw = jnp.zeros((2 * TOP_K, tm), F32)
    for k in range(TOP_K):
        rank = jnp.sum(jnp.where(sels[k], before, 0.0), axis=0, keepdims=True)
        ri = jnp.where(out_row == k, idxs[k].astype(I32), ri)
        ri = jnp.where(out_row == TOP_K + k, rank.astype(I32), ri)
        rw = jnp.where(out_row == k, exps[k] / total, rw)
    ri_ref[...] = ri
    rw_ref[...] = rw


EXPERT_TM = 512
SC_CORES = 2
SC_SUBCORES = 16
SC_WORKERS = SC_CORES * SC_SUBCORES
SC_CHUNK = 64
PAD_SLOTS = N_EXPERTS * EXPERT_TM


def _sc_mesh():
    return plsc.VectorSubcoreMesh(core_axis_name="c", subcore_axis_name="s")


def _sc_worker():
    return lax.axis_index("s") * SC_CORES + lax.axis_index("c")


def _scatter_indices(dest):
    T = dest.shape[1]
    n_ch = T // (SC_WORKERS * SC_CHUNK)
    idx = dest.reshape(TOP_K, SC_WORKERS, n_ch, SC_CHUNK).transpose(1, 2, 0, 3)
    return idx.reshape(SC_WORKERS, n_ch * TOP_K, SC_CHUNK)


def _sc_dispatch(hp, idx, pad_idx, n_rows):
    T = hp.shape[0]
    per_w = T // SC_WORKERS
    n_ch = per_w // SC_CHUNK
    n_pc = PAD_SLOTS // (SC_WORKERS * SC_CHUNK)
    assert per_w % SC_CHUNK == 0 and n_ch >= 2 and n_ch % 2 == 0
    pidx = pad_idx.reshape(SC_WORKERS, n_pc, SC_CHUNK)
    zeros = jnp.zeros((SC_CHUNK, HALF), I32)

    @functools.partial(
        pl.kernel, mesh=_sc_mesh(),
        out_type=jax.ShapeDtypeStruct((n_rows, HALF), I32),
        scratch_types=[
            pltpu.VMEM((n_ch * TOP_K, SC_CHUNK), I32),
            pltpu.VMEM((n_pc, SC_CHUNK), I32),
            pltpu.VMEM((2, SC_CHUNK, HALF), I32),
            pltpu.SemaphoreType.DMA((2,)),
            pltpu.SemaphoreType.DMA((2,)),
        ],
        name="sc_dispatch",
    )
    def k(hp_hbm, idx_hbm, pidx_hbm, zeros_hbm, xs_hbm, idx_v, pidx_v, rows_v, lsem, ssem):
        wid = _sc_worker()
        base = wid * per_w
        pltpu.sync_copy(idx_hbm.at[wid], idx_v)
        pltpu.sync_copy(pidx_hbm.at[wid], pidx_v)

        pltpu.sync_copy(zeros_hbm, rows_v.at[0])
        for p in range(n_pc):
            pltpu.make_async_copy(rows_v.at[0], xs_hbm.at[pidx_v.at[p]], ssem.at[0]).start()
        for p in range(n_pc):
            pltpu.make_async_copy(rows_v.at[0], xs_hbm.at[pidx_v.at[p]], ssem.at[0]).wait()

        def load(i, slot):
            return pltpu.make_async_copy(hp_hbm.at[pl.ds(base + i * SC_CHUNK, SC_CHUNK)], rows_v.at[slot],
                                         lsem.at[slot])

        def scatter(i, kk, slot):
            return pltpu.make_async_copy(rows_v.at[slot], xs_hbm.at[idx_v.at[i * TOP_K + kk]], ssem.at[slot])

        load(0, 0).start()

        def body(i2, carry):
            for slot in range(2):
                i = i2 * 2 + slot
                nxt = 1 - slot

                @pl.when(i + 1 < n_ch)
                def _():
                    @pl.when(i >= 1)
                    def _():
                        for kk in range(TOP_K):
                            scatter(i - 1, kk, nxt).wait()
                    load(i + 1, nxt).start()

                load(i, slot).wait()
                for kk in range(TOP_K):
                    scatter(i, kk, slot).start()
            return carry

        lax.fori_loop(0, n_ch // 2, body, 0)
        for kk in range(TOP_K):
            scatter(n_ch - 2, kk, 0).wait()
            scatter(n_ch - 1, kk, 1).wait()

    return k(hp, idx, pidx, zeros)


def _sc_gather(table, idx):
    n = idx.shape[0]
    per_w = n // SC_WORKERS
    n_ch = per_w // SC_CHUNK
    assert per_w % SC_CHUNK == 0 and n_ch >= 2 and n_ch % 2 == 0

    @functools.partial(
        pl.kernel, mesh=_sc_mesh(),
        out_type=jax.ShapeDtypeStruct((n, HALF), I32),
        scratch_types=[
            pltpu.VMEM((n_ch, SC_CHUNK), I32),
            pltpu.VMEM((2, SC_CHUNK, HALF), I32),
            pltpu.SemaphoreType.DMA((2,)),
            pltpu.SemaphoreType.DMA((2,)),
        ],
        name="sc_gather",
    )
    def k(table_hbm, idx_hbm, out_hbm, idx_v, rows_v, gsem, wsem):
        wid = _sc_worker()
        base = wid * per_w
        pltpu.sync_copy(idx_hbm.at[wid], idx_v)

        def gather(i, slot):
            return pltpu.make_async_copy(table_hbm.at[idx_v.at[i]], rows_v.at[slot], gsem.at[slot])

        def writeback(i, slot):
            return pltpu.make_async_copy(rows_v.at[slot], out_hbm.at[pl.ds(base + i * SC_CHUNK, SC_CHUNK)],
                                         wsem.at[slot])

        gather(0, 0).start()

        def body(i2, carry):
            for slot in range(2):
                i = i2 * 2 + slot
                nxt = 1 - slot

                @pl.when(i + 1 < n_ch)
                def _():
                    @pl.when(i >= 1)
                    def _():
                        writeback(i - 1, nxt).wait()
                    gather(i + 1, nxt).start()

                gather(i, slot).wait()
                writeback(i, slot).start()
            return carry

        lax.fori_loop(0, n_ch // 2, body, 0)
        writeback(n_ch - 2, 0).wait()
        writeback(n_ch - 1, 1).wait()

    return k(table, idx.reshape(SC_WORKERS, n_ch, SC_CHUNK))


FF_CHUNK = 512


def _expert_body(te_ref, nv_ref, x_ref, w1f_ref, b1_ref, w2f_ref, b2_ref, y_ref, w1_ref, w2_ref):
    i = pl.program_id(0)

    @pl.when(jnp.logical_or(i == 0, te_ref[i] != te_ref[jnp.maximum(i - 1, 0)]))
    def _():
        w1_ref[...] = w1f_ref[...].astype(BF16)
        w2_ref[...] = w2f_ref[...].astype(BF16)

    @pl.when(i < nv_ref[0])
    def _():
        lo, hi = _unpack_rows(x_ref[...])
        xlo = lo.astype(BF16)
        xhi = hi.astype(BF16)
        acc = jnp.zeros((EXPERT_TM, D_MODEL), F32) + b2_ref[...]
        for c in range(D_FF // FF_CHUNK):
            def up(off):
                cs = slice(off + c * FF_CHUNK, off + (c + 1) * FF_CHUNK)
                return (jnp.dot(xlo, w1_ref[0:HALF, cs], preferred_element_type=F32)
                        + jnp.dot(xhi, w1_ref[HALF:D_MODEL, cs], preferred_element_type=F32)
                        + b1_ref[:, cs])
            g = jnp.minimum(up(0), SWIGLU_LIMIT)
            lin = jnp.clip(up(D_FF), -SWIGLU_LIMIT, SWIGLU_LIMIT)
            a = g * jax.nn.sigmoid(SWIGLU_ALPHA * g) * (lin + 1.0)
            acc = acc + jnp.dot(a.astype(BF16), w2_ref[c * FF_CHUNK:(c + 1) * FF_CHUNK, :],
                                preferred_element_type=F32)
        y_ref[...] = _pack_rows(acc)


def _experts(tile_expert, n_valid, xs, w1, b1, w2, b2):
    n_rows = xs.shape[0]
    tm = EXPERT_TM
    n_tiles = n_rows // tm
    row = lambda i, te, nv: (jnp.minimum(i, nv[0] - 1), 0)
    grid_spec = pltpu.PrefetchScalarGridSpec(
        num_scalar_prefetch=2,
        grid=(n_tiles,),
        in_specs=[
            pl.BlockSpec((tm, HALF), row),
            pl.BlockSpec((None, D_MODEL, 2 * D_FF), lambda i, te, nv: (te[i], 0, 0)),
            pl.BlockSpec((None, 1, 2 * D_FF), lambda i, te, nv: (te[i], 0, 0)),
            pl.BlockSpec((None, D_FF, D_MODEL), lambda i, te, nv: (te[i], 0, 0)),
            pl.BlockSpec((None, 1, D_MODEL), lambda i, te, nv: (te[i], 0, 0)),
        ],
        out_specs=pl.BlockSpec((tm, HALF), row),
        scratch_shapes=[pltpu.VMEM((D_MODEL, 2 * D_FF), BF16), pltpu.VMEM((D_FF, D_MODEL), BF16)],
    )
    return pl.pallas_call(
        _expert_body,
        grid_spec=grid_spec,
        out_shape=jax.ShapeDtypeStruct((n_rows, HALF), I32),
        compiler_params=_cparams(("arbitrary",)),
        name="experts",
    )(tile_expert, n_valid, xs, w1, b1, w2, b2)


COMBINE_TM = 1024


def _combine_body(y0_ref, y1_ref, y2_ref, y3_ref, rw_ref, x_ref, g_ref, *rest):
    o_ref = rest[-1]
    acc = x_ref[...]
    rw = jnp.concatenate([rw_ref[...], jnp.zeros((LANES - 2 * TOP_K, COMBINE_TM), F32)], axis=0).T
    for k, y_ref in enumerate((y0_ref, y1_ref, y2_ref, y3_ref)):
        lo, hi = _unpack_rows(y_ref[...])
        acc = acc + rw[:, k:k + 1] * jnp.concatenate([lo, hi], axis=-1)
    o_ref[...] = _rms(acc, g_ref[...])


def _combine(yg, rw, x2, g, part, out_prev):
    T = x2.shape[0]
    tm = COMBINE_TM
    nt = T // tm
    in_specs = [
        pl.BlockSpec((tm, HALF), lambda i: (i, 0)),
        pl.BlockSpec((tm, HALF), lambda i: (nt + i, 0)),
        pl.BlockSpec((tm, HALF), lambda i: (2 * nt + i, 0)),
        pl.BlockSpec((tm, HALF), lambda i: (3 * nt + i, 0)),
        pl.BlockSpec((2 * TOP_K, tm), lambda i: (0, i)),
        pl.BlockSpec((tm, D_MODEL), lambda i: (i, 0)),
        pl.BlockSpec((1, D_MODEL), lambda i: (0, 0)),
    ]
    args = [yg, yg, yg, yg, rw, x2, g]
    aliases = {}
    if out_prev is not None:
        in_specs.append(pl.BlockSpec(memory_space=pl.ANY))
        args.append(out_prev)
        aliases = {len(args) - 1: 0}
    return pl.pallas_call(
        _combine_body,
        grid=(nt,),
        in_specs=in_specs,
        out_specs=pl.BlockSpec((tm, D_MODEL), lambda i: (part * nt + i, 0)),
        out_shape=jax.ShapeDtypeStruct((T * MOE_PARTS, D_MODEL), F32),
        input_output_aliases=aliases,
        compiler_params=_cparams(("parallel",)),
        name="combine",
    )(*args)


def _pad_lanes(v):
    v = v.reshape(1, -1).astype(F32)
    return jnp.pad(v, ((0, 0), (0, LANES - v.shape[1])))


def _layer(x2d, mem2d, B, S, M, norm_mix, w_in, b_ml_gates, conv_ml, ml_head_norm, b_fx_gate, norm_mem,
           w_mem_kv, w_branch, w_out, norm_moe, w_router, b_router, w_exp_in, b_exp_in, w_exp_out,
           b_exp_out, norm_out):
    T = B * S
    w16 = w_in.astype(BF16)
    w_big = jnp.concatenate([w16[:, 0:2048], w16[:, 2056:3080], w16[:, 3080:6152], w16[:, 6160:7184],
                             w16[:, 7184:10256]], axis=1)
    w_small = jnp.concatenate([w16[:, 2048:2056], w16[:, 6152:6160]], axis=1)
    w_small = jnp.pad(w_small, ((0, 0), (0, LANES - w_small.shape[1])))
    row = lambda v: v.reshape(1, -1).astype(F32)

    proj, small = _inproj(x2d, row(norm_mix), w_big, w_small)

    y_ml = _mlstm(proj, small, conv_ml.astype(F32), _pad_lanes(b_ml_gates), row(ml_head_norm), B, S)

    b_fx = jnp.pad(b_fx_gate.reshape(1, -1).astype(F32), ((0, 0), (2 * ML_HEADS, LANES - 2 * ML_HEADS - FX_HEADS)))
    y_fx = _fox_attn(proj, _fox_gate(small, b_fx, B, S), B, S)

    kv = _memkv(mem2d, row(norm_mem), w_mem_kv.astype(BF16))
    y_ca = _memattn(proj, kv, B, S, M)

    w_r = w_router.T.astype(BF16)
    moe_weights = (w_exp_in.astype(F32), b_exp_in.reshape(N_EXPERTS, 1, -1).astype(F32), w_exp_out.astype(F32),
                   b_exp_out.reshape(N_EXPERTS, 1, -1).astype(F32))
    staged = []
    for part in range(MOE_PARTS):
        x2, hp, ri, rw, cnt = _merge(y_ml, y_fx, y_ca, proj, x2d, w_branch.astype(BF16), w_out.astype(BF16),
                                     row(norm_moe), w_r, b_router.reshape(N_EXPERTS, 1).astype(F32), part)
        scatter_idx, dest, pad_idx, tile_e, n_valid, n_rows = _moe_plan(ri, cnt)
        staged.append((x2, rw, dest, tile_e, n_valid, _sc_dispatch(hp, scatter_idx, pad_idx, n_rows)))
    gathered = [_sc_gather(_experts(tile_e, n_valid, xs, *moe_weights), dest.reshape(-1))
                for _, _, dest, tile_e, n_valid, xs in staged]
    out = None
    for part, ((x2, rw, *_), yg) in enumerate(zip(staged, gathered)):
        out = _combine(yg, rw, x2, row(norm_out), part, out)
    return out


def _moe_plan(ri, cnt):
    T = ri.shape[1]
    tm = EXPERT_TM
    n_tiles = (T * TOP_K) // tm + N_EXPERTS
    counts = cnt[:, 0].astype(I32)
    padded = ((counts + tm - 1) // tm) * tm
    gend = jnp.cumsum(padded)
    gstart = gend - padded
    expert_ids = jnp.arange(N_EXPERTS, dtype=I32)
    start_of = jnp.sum(jnp.where(ri[0:TOP_K, :, None] == expert_ids, gstart, 0), axis=-1)
    dest = start_of + ri[TOP_K:2 * TOP_K, :]
    n_valid = gend[-1] // tm
    tile_ids = jnp.arange(n_tiles, dtype=I32)
    last_tile = jnp.minimum(tile_ids, n_valid - 1)
    tile_e = jnp.minimum(jnp.sum((gend[None, :] <= last_tile[:, None] * tm).astype(I32), axis=1), N_EXPERTS - 1)

    slot = jnp.arange(tm, dtype=I32)
    spare = n_tiles * tm + slot % SC_CHUNK
    pad_idx = jnp.where(slot[None, :] < (padded - counts)[:, None], (gstart + counts)[:, None] + slot[None, :],
                        spare[None, :]).reshape(-1)

    return (_scatter_indices(dest), dest, pad_idx, tile_e.astype(I32), n_valid.reshape(1).astype(I32),
            n_tiles * tm + SC_CHUNK)


def kernel(x, mem, norm_mix, w_in, b_ml_gates, conv_ml, ml_head_norm, b_fx_gate, norm_mem, w_mem_kv, w_branch,
           w_out, norm_moe, w_router, b_router, w_exp_in, b_exp_in, w_exp_out, b_exp_out, norm_final):
    B, S, D = x.shape
    M = mem.shape[1]
    depth = norm_mix.shape[0]
    assert depth == 1, "the combine kernel fuses the final norm, so exactly one layer is supported"
    assert D == D_MODEL and S % ML_BLOCK == 0 and S % FX_T == 0 and S % CA_TQ == 0
    out = _layer(x.reshape(B * S, D), mem.reshape(B * M, D), B, S, M, norm_mix[0], w_in[0], b_ml_gates[0],
                 conv_ml[0], ml_head_norm[0], b_fx_gate[0], norm_mem[0], w_mem_kv[0], w_branch[0], w_out[0],
                 norm_moe[0], w_router[0], b_router[0], w_exp_in[0], b_exp_in[0], w_exp_out[0], b_exp_out[0],
                 norm_final)
    return out.reshape(B, S, D)
```

```python
import functools

import jax
import jax.numpy as jnp
from jax import lax
from jax.experimental import pallas as pl
from jax.experimental.pallas import tpu as pltpu
from jax.experimental.pallas import tpu_sc as plsc

F32 = jnp.float32
BF16 = jnp.bfloat16
I32 = jnp.int32

D_MODEL = 1024
N_MEM_HEADS = 4
ML_HEADS = 4
ML_DQK = 128
ML_DV = 256
ML_CONV = 4
FX_HEADS = 8
FX_DH = 128
CA_HEADS = 4
CA_DH = 256
N_EXPERTS = 32
TOP_K = 4
D_FF = D_MODEL
SWIGLU_LIMIT = 7.0
SWIGLU_ALPHA = 1.702
EPS = 1e-5
LANES = 128
HALF = D_MODEL // 2
HI_MASK = -65536

COL_MLQK, COL_MLV, COL_MLO, COL_FXQ, COL_FXK, COL_FXV, COL_CAQ, COL_GATE0 = 0, 1, 2, 3, 4, 5, 6, 7
N_BIG = 10 * D_MODEL

VMEM_LIMIT = 56 * 1024 * 1024


def _cparams(sem):
    return pltpu.CompilerParams(dimension_semantics=sem, vmem_limit_bytes=VMEM_LIMIT)


def _rms(x, g):
    return x * lax.rsqrt(jnp.mean(x * x, axis=-1, keepdims=True) + EPS) * g


def _log_sigmoid(x):
    return jnp.minimum(x, 0.0) - jnp.log1p(jnp.exp(-jnp.abs(x)))


def _pack_rows(y):
    bits = lax.bitcast_convert_type(y.astype(BF16).astype(F32), I32)
    return lax.shift_right_logical(bits[:, :HALF], 16) | (bits[:, HALF:] & HI_MASK)


def _unpack_rows(w):
    lo = lax.bitcast_convert_type(lax.shift_left(w, 16), F32)
    hi = lax.bitcast_convert_type(w & HI_MASK, F32)
    return lo, hi


def _inproj_body(x_ref, g_ref, w_ref, ws_ref, o_ref, os_ref, h_ref):
    @pl.when(pl.program_id(1) == 0)
    def _():
        hb = _rms(x_ref[...], g_ref[...]).astype(BF16)
        h_ref[...] = hb
        os_ref[...] = jnp.dot(hb, ws_ref[...], preferred_element_type=F32)

    o_ref[...] = jnp.dot(h_ref[...], w_ref[...], preferred_element_type=F32).astype(BF16)


def _inproj(x2d, g, w_big, w_small):
    T = x2d.shape[0]
    tm = min(1024, T)
    tn = 2560
    return pl.pallas_call(
        _inproj_body,
        grid=(T // tm, N_BIG // tn),
        in_specs=[
            pl.BlockSpec((tm, D_MODEL), lambda i, j: (i, 0)),
            pl.BlockSpec((1, D_MODEL), lambda i, j: (0, 0)),
            pl.BlockSpec((D_MODEL, tn), lambda i, j: (0, j)),
            pl.BlockSpec((D_MODEL, LANES), lambda i, j: (0, 0)),
        ],
        out_specs=[
            pl.BlockSpec((tm, tn), lambda i, j: (i, j)),
            pl.BlockSpec((tm, LANES), lambda i, j: (i, 0)),
        ],
        out_shape=[
            jax.ShapeDtypeStruct((T, N_BIG), BF16),
            jax.ShapeDtypeStruct((T, LANES), F32),
        ],
        scratch_shapes=[pltpu.VMEM((tm, D_MODEL), BF16)],
        compiler_params=_cparams(("parallel", "arbitrary")),
        name="inproj",
    )(x2d, g, w_big, w_small)


ML_BLOCK = 1024
ML_MB = 1
ML_CHUNK = 128
CONV_PAD = 8


def _mlstm_body(qk_ref, v_ref, o_ref, g_ref, cw_ref, bg_ref, hn_ref, y_ref, xbuf, c_st, n_st, m_st):
    L = ML_CHUNK

    @pl.when(pl.program_id(1) == 0)
    def _():
        xbuf[:, 0:CONV_PAD, :] = jnp.zeros((ML_MB, CONV_PAD, D_MODEL), F32)
        c_st[...] = jnp.zeros_like(c_st)
        n_st[...] = jnp.zeros_like(n_st)
        m_st[...] = jnp.zeros_like(m_st)

    for bb in range(ML_MB):
        xbuf[bb, CONV_PAD:CONV_PAD + ML_BLOCK, :] = qk_ref[bb].astype(F32)
    cw = cw_ref[...]
    row = lax.broadcasted_iota(I32, (L, L), 0)
    col = lax.broadcasted_iota(I32, (L, L), 1)
    tri = (row >= col).astype(BF16)
    causal_t = col >= row
    bg = bg_ref[...]
    scale = ML_DQK ** -0.5
    nt_dims = (((1,), (1,)), ((), ()))

    def chunk(bb, c):
        r0 = c * L
        conv = cw[0:1, :] * xbuf[bb, r0 + CONV_PAD - 3:r0 + CONV_PAD - 3 + L, :]
        for j in range(1, ML_CONV):
            s0 = r0 + CONV_PAD - 3 + j
            conv = conv + cw[j:j + 1, :] * xbuf[bb, s0:s0 + L, :]
        act = conv * jax.nn.sigmoid(conv)

        gates = g_ref[bb, r0:r0 + L, :] + bg
        lf = _log_sigmoid(gates)
        cum = jnp.zeros((L, LANES), F32)
        for _ in range(3):
            piece = lf.astype(BF16)
            cum = cum + jnp.dot(tri, piece, preferred_element_type=F32)
            lf = lf - piece.astype(F32)
        gates_t = gates.T
        cum_t = cum.T
        for h in range(ML_HEADS):
            b_row = cum_t[ML_HEADS + h:ML_HEADS + h + 1, :]
            i_row = gates_t[h:h + 1, :]
            a_col = gates[:, h:h + 1] - cum[:, ML_HEADS + h:ML_HEADS + h + 1]
            st = bb * ML_HEADS + h
            m_prev = m_st[st]
            dm = jnp.where(causal_t, a_col + b_row, -jnp.inf)
            m_inter = b_row + m_prev
            m_t = jnp.maximum(jnp.max(dm, axis=0, keepdims=True), m_inter)
            w_intra = jnp.exp(dm - m_t)
            w_inter = jnp.exp(m_inter - m_t)

            qb = (act[:, h * ML_DQK:(h + 1) * ML_DQK] * scale).astype(BF16)
            kb = act[:, (ML_HEADS + h) * ML_DQK:(ML_HEADS + h + 1) * ML_DQK].astype(BF16)
            v_t = v_ref[bb, r0:r0 + L, h * ML_DV:(h + 1) * ML_DV].astype(F32).T
            p_t = lax.dot_general(kb, qb, nt_dims, preferred_element_type=F32) * w_intra
            c_old = c_st[st]
            n_old = n_st[st]
            num = jnp.dot(v_t.astype(BF16), p_t.astype(BF16), preferred_element_type=F32) + w_inter * (
                lax.dot_general(c_old.astype(BF16), qb, nt_dims, preferred_element_type=F32))
            qn = lax.dot_general(jnp.broadcast_to(n_old, (8, ML_DQK)).astype(BF16), qb, nt_dims,
                                 preferred_element_type=F32)[0:1, :]
            den = jnp.sum(p_t, axis=0, keepdims=True) + w_inter * qn
            hv = num / jnp.maximum(jnp.abs(den), jnp.exp(-m_t))

            m_new = m_t[:, L - 1:L]
            b_last = b_row[:, L - 1:L]
            wk = jnp.exp(b_last - b_row + i_row - m_new)
            decay = jnp.exp(b_last + m_prev - m_new)
            c_st[st] = decay * c_old + jnp.dot((v_t * wk).astype(BF16), kb, preferred_element_type=F32)
            n_st[st] = decay * n_old + jnp.dot(jnp.broadcast_to(wk, (8, L)).astype(BF16), kb,
                                               preferred_element_type=F32)[0:1, :]
            m_st[st] = m_new

            hn = (hv * lax.rsqrt(jnp.mean(hv * hv, axis=0, keepdims=True) + EPS)).T
            og = o_ref[bb, r0:r0 + L, h * ML_DV:(h + 1) * ML_DV].astype(F32)
            y_ref[bb, r0:r0 + L, h * ML_DV:(h + 1) * ML_DV] = (
                hn * hn_ref[:, h * ML_DV:(h + 1) * ML_DV] * jax.nn.sigmoid(og)).astype(BF16)

    for c in range(ML_BLOCK // L):
        for bb in range(ML_MB):
            chunk(bb, c)

    xbuf[:, 0:CONV_PAD, :] = xbuf[:, ML_BLOCK:ML_BLOCK + CONV_PAD, :]


def _mlstm(proj, small, conv_w, b_gates, head_norm, B, S):
    T = B * S
    ns = S // ML_BLOCK
    assert B % ML_MB == 0
    proj3 = proj.reshape(B, S, N_BIG)
    blk = lambda col: pl.BlockSpec((ML_MB, ML_BLOCK, D_MODEL), lambda b, s: (b, s, col))
    out = pl.pallas_call(
        _mlstm_body,
        grid=(B // ML_MB, ns),
        in_specs=[
            blk(COL_MLQK),
            blk(COL_MLV),
            blk(COL_MLO),
            pl.BlockSpec((ML_MB, ML_BLOCK, LANES), lambda b, s: (b, s, 0)),
            pl.BlockSpec((ML_CONV, D_MODEL), lambda b, s: (0, 0)),
            pl.BlockSpec((1, LANES), lambda b, s: (0, 0)),
            pl.BlockSpec((1, D_MODEL), lambda b, s: (0, 0)),
        ],
        out_specs=blk(0),
        out_shape=jax.ShapeDtypeStruct((B, S, D_MODEL), BF16),
        scratch_shapes=[
            pltpu.VMEM((ML_MB, ML_BLOCK + CONV_PAD, D_MODEL), F32),
            pltpu.VMEM((ML_MB * ML_HEADS, ML_DV, ML_DQK), F32),
            pltpu.VMEM((ML_MB * ML_HEADS, 1, ML_DQK), F32),
            pltpu.VMEM((ML_MB * ML_HEADS, 1, 1), F32),
        ],
        compiler_params=_cparams(("parallel", "arbitrary")),
        name="mlstm",
    )(proj3, proj3, proj3, small.reshape(B, S, LANES), conv_w, b_gates, head_norm)
    return out.reshape(T, D_MODEL)


FX_T = 512
FX_HP = 2
FX_VR = FX_DH + 16
LOG2E = 1.4426950408889634
N_PIECES = 3
FX_GATE_T = 128


def _fox_gate_body(g_ref, b_ref, o_ref):
    S = g_ref.shape[0]
    row = lax.broadcasted_iota(I32, (FX_GATE_T, FX_GATE_T), 0)
    col = lax.broadcasted_iota(I32, (FX_GATE_T, FX_GATE_T), 1)
    tri = (row >= col).astype(BF16)
    carry = jnp.zeros((1, LANES), F32)
    for blk in range(S // FX_GATE_T):
        rows = slice(blk * FX_GATE_T, (blk + 1) * FX_GATE_T)
        lf = _log_sigmoid(g_ref[rows, :] + b_ref[...])
        cum = carry
        for _ in range(N_PIECES):
            piece = lf.astype(BF16)
            cum = cum + jnp.dot(tri, piece, preferred_element_type=F32)
            lf = lf - piece.astype(F32)
        carry = cum[FX_GATE_T - 1:FX_GATE_T, :]
        o_ref[rows, :] = cum * (-LOG2E)


def _fox_gate(small, b_fx, B, S):
    return pl.pallas_call(
        _fox_gate_body,
        grid=(B,),
        in_specs=[
            pl.BlockSpec((S, LANES), lambda b: (b, 0)),
            pl.BlockSpec((1, LANES), lambda b: (0, 0)),
        ],
        out_specs=pl.BlockSpec((S, LANES), lambda b: (b, 0)),
        out_shape=jax.ShapeDtypeStruct((B * S, LANES), F32),
        compiler_params=_cparams(("parallel",)),
        name="fox_gate",
    )(small, b_fx)


def _fox_attn_body(q_ref, k_ref, v_ref, c_ref, o_ref, kx_ref, vt_ref, m_ref, acc_ref, s_ref):
    S = k_ref.shape[0]
    nq = S // FX_T

    c = c_ref[...]
    hi = c.astype(BF16)
    r1 = c - hi.astype(F32)
    mid = r1.astype(BF16)
    lo = (r1 - mid.astype(F32)).astype(BF16)
    sel_row = lax.broadcasted_iota(I32, (LANES, LANES), 0)
    sel_col = lax.broadcasted_iota(I32, (LANES, LANES), 1)
    ones_rows = (lax.broadcasted_iota(I32, (FX_VR - FX_DH, FX_T), 0) == 0).astype(BF16)
    head_slices = [slice(hh * FX_DH, (hh + 1) * FX_DH) for hh in range(FX_HP)]
    for hh, sl in enumerate(head_slices):
        lane = 2 * ML_HEADS + pl.program_id(1) * FX_HP + hh
        pieces = None
        for p, part in enumerate((hi, mid, lo)):
            pick = jnp.logical_and(sel_row == lane, sel_col == p).astype(BF16)
            t = jnp.dot(part, pick, preferred_element_type=F32)
            pieces = t if pieces is None else pieces + t
        kx_ref[hh, :, 0:FX_DH] = k_ref[:, sl]
        kx_ref[hh, :, FX_DH:2 * FX_DH] = pieces.astype(BF16)
        for j in range(nq):
            vt = v_ref[j * FX_T:(j + 1) * FX_T, sl].astype(F32).T.astype(BF16)
            vt_ref[hh, j] = jnp.concatenate([vt, ones_rows], axis=0)

    piece_rows = (lax.broadcasted_iota(I32, (FX_DH, FX_T), 0) < N_PIECES).astype(BF16)

    def start(i):
        q_x = []
        for sl in head_slices:
            q_t = (q_ref[i * FX_T:(i + 1) * FX_T, sl].astype(F32) * (FX_DH ** -0.5 * LOG2E)).T.astype(BF16)
            q_x.append(jnp.concatenate([q_t, piece_rows], axis=0))
        m_ref[i % 2] = jnp.full(m_ref.shape[1:], -jnp.inf, F32)
        acc_ref[i % 2] = jnp.zeros(acc_ref.shape[1:], F32)
        return q_x

    def key_rows(j):
        return pl.ds(j * FX_T, FX_T) if isinstance(j, int) else pl.ds(pl.multiple_of(j * FX_T, FX_T), FX_T)

    def scores(q_x, j, slot):
        for hh in range(FX_HP):
            s_ref[slot, hh] = jnp.dot(kx_ref[hh, key_rows(j), :], q_x[hh], preferred_element_type=F32)

    def consume(par, j, slot, masked):
        for hh in range(FX_HP):
            s = s_ref[slot, hh]
            if masked:
                key = lax.broadcasted_iota(I32, (FX_T, FX_T), 0)
                qry = lax.broadcasted_iota(I32, (FX_T, FX_T), 1)
                s = jnp.where(qry >= key, s, -jnp.inf)
            m_old = m_ref[par, hh]
            m_new = jnp.maximum(m_old, jnp.max(s, axis=0, keepdims=True))
            p = jnp.exp2(s - m_new).astype(BF16)
            acc_ref[par, hh] = jnp.exp2(m_old - m_new) * acc_ref[par, hh] + jnp.dot(
                vt_ref[hh, j], p, preferred_element_type=F32)
            m_ref[par, hh] = m_new

    def finish(i, slot):
        consume(i % 2, i, slot, True)
        for hh, sl in enumerate(head_slices):
            acc = acc_ref[i % 2, hh]
            o_ref[i * FX_T:(i + 1) * FX_T, sl] = (acc[0:FX_DH, :] / acc[FX_DH:FX_DH + 1, :]).T.astype(BF16)

    diag_slot = 0
    for i in range(nq):
        q_x = start(i)
        first = 0 if i == 0 else 1 - diag_slot
        scores(q_x, 0, first)
        if i > 0:
            finish(i - 1, diag_slot)

        def pair(jj, carry, q_x=q_x, first=first, par=i % 2):
            j = 2 * jj
            scores(q_x, j + 1, 1 - first)
            consume(par, j, first, False)
            scores(q_x, j + 2, first)
            consume(par, j + 1, 1 - first, False)
            return carry

        if i >= 2:
            lax.fori_loop(0, i // 2, pair, 0)
        if i % 2 == 1:
            scores(q_x, i, 1 - first)
            consume(i % 2, i - 1, first, False)
            diag_slot = 1 - first
        else:
            diag_slot = first
    finish(nq - 1, diag_slot)


def _fox_attn(proj, c_neg, B, S):
    T = B * S
    nq = S // FX_T
    wide = FX_HP * FX_DH
    cq = COL_FXQ * (D_MODEL // wide)
    ck = COL_FXK * (D_MODEL // wide)
    cv = COL_FXV * (D_MODEL // wide)
    proj3 = proj.reshape(B, S, N_BIG)
    out = pl.pallas_call(
        _fox_attn_body,
        grid=(B, FX_HEADS // FX_HP),
        in_specs=[
            pl.BlockSpec((None, S, wide), lambda b, h: (b, 0, cq + h)),
            pl.BlockSpec((None, S, wide), lambda b, h: (b, 0, ck + h)),
            pl.BlockSpec((None, S, wide), lambda b, h: (b, 0, cv + h)),
            pl.BlockSpec((None, S, LANES), lambda b, h: (b, 0, 0)),
        ],
        out_specs=pl.BlockSpec((None, S, wide), lambda b, h: (b, 0, h)),
        out_shape=jax.ShapeDtypeStruct((B, S, D_MODEL), BF16),
        scratch_shapes=[
            pltpu.VMEM((FX_HP, S, 2 * FX_DH), BF16),
            pltpu.VMEM((FX_HP, nq, FX_VR, FX_T), BF16),
            pltpu.VMEM((2, FX_HP, 1, FX_T), F32),
            pltpu.VMEM((2, FX_HP, FX_VR, FX_T), F32),
            pltpu.VMEM((2, FX_HP, FX_T, FX_T), F32),
        ],
        compiler_params=_cparams(("parallel", "parallel")),
        name="fox_attn",
    )(proj3, proj3, proj3, c_neg.reshape(B, S, LANES))
    return out.reshape(T, D_MODEL)


def _memkv_body(x_ref, g_ref, w_ref, o_ref):
    hb = _rms(x_ref[...], g_ref[...]).astype(BF16)
    o_ref[...] = jnp.dot(hb, w_ref[...], preferred_element_type=F32).astype(BF16)


def _memkv(mem2d, g, w_kv):
    R = mem2d.shape[0]
    tm = min(512, R)
    N = w_kv.shape[1]
    return pl.pallas_call(
        _memkv_body,
        grid=(R // tm,),
        in_specs=[
            pl.BlockSpec((tm, D_MODEL), lambda i: (i, 0)),
            pl.BlockSpec((1, D_MODEL), lambda i: (0, 0)),
            pl.BlockSpec((D_MODEL, N), lambda i: (0, 0)),
        ],
        out_specs=pl.BlockSpec((tm, N), lambda i: (i, 0)),
        out_shape=jax.ShapeDtypeStruct((R, N), BF16),
        compiler_params=_cparams(("parallel",)),
        name="memkv",
    )(mem2d, g, w_kv)


CA_TQ = 2048


def _memattn_body(q_ref, k_ref, v_ref, o_ref):
    scale = CA_DH ** -0.5
    for h in range(CA_HEADS):
        sl = slice(h * CA_DH, (h + 1) * CA_DH)
        s = lax.dot_general(q_ref[:, sl], k_ref[:, sl], (((1,), (1,)), ((), ())),
                            preferred_element_type=F32) * scale
        p = jnp.exp(s - jnp.max(s, axis=-1, keepdims=True))
        l = jnp.sum(p, axis=-1, keepdims=True)
        o = jnp.dot(p.astype(BF16), v_ref[:, sl], preferred_element_type=F32) / l
        o_ref[:, sl] = o.astype(BF16)


def _memattn(proj, kv, B, S, M):
    T = B * S
    nq = S // CA_TQ
    kv3 = kv.reshape(B, M, 2 * D_MODEL)
    return pl.pallas_call(
        _memattn_body,
        grid=(B, nq),
        in_specs=[
            pl.BlockSpec((CA_TQ, D_MODEL), lambda b, i: (b * nq + i, COL_CAQ)),
            pl.BlockSpec((None, M, D_MODEL), lambda b, i: (b, 0, 0)),
            pl.BlockSpec((None, M, D_MODEL), lambda b, i: (b, 0, 1)),
        ],
        out_specs=pl.BlockSpec((CA_TQ, D_MODEL), lambda b, i: (b * nq + i, 0)),
        out_shape=jax.ShapeDtypeStruct((T, D_MODEL), BF16),
        compiler_params=_cparams(("parallel", "arbitrary")),
        name="memattn",
    )(proj, kv3, kv3)


MERGE_TM = 512
MOE_PARTS = 2


def _merge_body(y0_ref, y1_ref, y2_ref, g0_ref, g1_ref, g2_ref, x_ref, wb_ref, wo_ref, gn_ref, wr_ref, br_ref,
                o_ref, hp_ref, ri_ref, rw_ref, cnt_ref, carry_ref):
    merged = None
    for n, (y_ref, g_ref) in enumerate(((y0_ref, g0_ref), (y1_ref, g1_ref), (y2_ref, g2_ref))):
        p = jnp.dot(y_ref[...], wb_ref[n], preferred_element_type=F32)
        t = jax.nn.sigmoid(g_ref[...].astype(F32)) * p
        merged = t if merged is None else merged + t
    x2 = x_ref[...] + jnp.dot(merged.astype(BF16), wo_ref[...], preferred_element_type=F32)
    o_ref[...] = x2
    _route(x2, gn_ref, wr_ref, br_ref, hp_ref, ri_ref, rw_ref, cnt_ref, carry_ref)


def _merge(y_ml, y_fx, y_ca, proj, x2d, w_branch, w_out, g_moe, w_router, b_router, part):
    T = x2d.shape[0] // MOE_PARTS
    tm = MERGE_TM
    off = part * (T // tm)
    src = lambda i: (off + i, 0)
    row = lambda i: (i, 0)
    const = lambda i: (0, 0)
    return pl.pallas_call(
        _merge_body,
        grid=(T // tm,),
        in_specs=[
            pl.BlockSpec((tm, D_MODEL), src),
            pl.BlockSpec((tm, D_MODEL), src),
            pl.BlockSpec((tm, D_MODEL), src),
            pl.BlockSpec((tm, D_MODEL), lambda i: (off + i, COL_GATE0)),
            pl.BlockSpec((tm, D_MODEL), lambda i: (off + i, COL_GATE0 + 1)),
            pl.BlockSpec((tm, D_MODEL), lambda i: (off + i, COL_GATE0 + 2)),
            pl.BlockSpec((tm, D_MODEL), src),
            pl.BlockSpec((3, D_MODEL, D_MODEL), lambda i: (0, 0, 0)),
            pl.BlockSpec((D_MODEL, D_MODEL), const),
            pl.BlockSpec((1, D_MODEL), const),
            pl.BlockSpec((N_EXPERTS, D_MODEL), const),
            pl.BlockSpec((N_EXPERTS, 1), const),
        ],
        out_specs=[
            pl.BlockSpec((tm, D_MODEL), row),
            pl.BlockSpec((tm, HALF), row),
            pl.BlockSpec((2 * TOP_K, tm), lambda i: (0, i)),
            pl.BlockSpec((2 * TOP_K, tm), lambda i: (0, i)),
            pl.BlockSpec((N_EXPERTS, 1), const),
        ],
        out_shape=[
            jax.ShapeDtypeStruct((T, D_MODEL), F32),
            jax.ShapeDtypeStruct((T, HALF), I32),
            jax.ShapeDtypeStruct((2 * TOP_K, T), I32),
            jax.ShapeDtypeStruct((2 * TOP_K, T), F32),
            jax.ShapeDtypeStruct((N_EXPERTS, 1), F32),
        ],
        scratch_shapes=[pltpu.VMEM((N_EXPERTS, 1), F32)],
        compiler_params=_cparams(("arbitrary",)),
        name="merge_router",
    )(y_ml, y_fx, y_ca, proj, proj, proj, x2d, w_branch, w_out, g_moe, w_router, b_router)


def _route(x2, g_ref, wr_ref, br_ref, hp_ref, ri_ref, rw_ref, cnt_ref, carry_ref):
    tm = MERGE_TM

    @pl.when(pl.program_id(0) == 0)
    def _():
        carry_ref[...] = jnp.zeros_like(carry_ref)

    h = _rms(x2, g_ref[...])
    hp_ref[...] = _pack_rows(h)
    logits = lax.dot_general(wr_ref[...], h.astype(BF16), (((1,), (1,)), ((), ())),
                             preferred_element_type=F32) + br_ref[...]
    eid = lax.broadcasted_iota(I32, (N_EXPERTS, tm), 0).astype(F32)

    work = logits
    onehot_sum = jnp.zeros((N_EXPERTS, tm), F32)
    vals, sels, idxs = [], [], []
    for _ in range(TOP_K):
        mx = jnp.max(work, axis=0, keepdims=True)
        idx = jnp.min(jnp.where(work == mx, eid, float(N_EXPERTS)), axis=0, keepdims=True)
        sel = eid == idx
        onehot_sum = onehot_sum + sel.astype(F32)
        work = jnp.where(sel, -jnp.inf, work)
        vals.append(mx)
        sels.append(sel)
        idxs.append(idx)
    exps = [jnp.exp(v - vals[0]) for v in vals]
    total = exps[0] + exps[1] + exps[2] + exps[3]

    earlier = (lax.broadcasted_iota(I32, (tm, tm), 0) < lax.broadcasted_iota(I32, (tm, tm), 1)).astype(BF16)
    before = jnp.dot(onehot_sum.astype(BF16), earlier, preferred_element_type=F32) + carry_ref[...]
    carry_ref[...] = carry_ref[...] + jnp.sum(onehot_sum, axis=1, keepdims=True)
    cnt_ref[...] = carry_ref[...]

    out_row = lax.broadcasted_iota(I32, (2 * TOP_K, tm), 0)
    ri = jnp.zeros((2 * TOP_K, tm), I32)
    rw = jnp.zeros((2 * TOP_K, tm), F32)
    for k in range(TOP_K):
        rank = jnp.sum(jnp.where(sels[k], before, 0.0), axis=0, keepdims=True)
        ri = jnp.where(out_row == k, idxs[k].astype(I32), ri)
        ri = jnp.where(out_row == TOP_K + k, rank.astype(I32), ri)
        rw = jnp.where(out_row == k, exps[k] / total, rw)
    ri_ref[...] = ri
    rw_ref[...] = rw


EXPERT_TM = 512
SC_CORES = 2
SC_SUBCORES = 16
SC_WORKERS = SC_CORES * SC_SUBCORES
SC_CHUNK = 64
PAD_SLOTS = N_EXPERTS * EXPERT_TM


def _sc_mesh():
    return plsc.VectorSubcoreMesh(core_axis_name="c", subcore_axis_name="s")


def _sc_worker():
    return lax.axis_index("s") * SC_CORES + lax.axis_index("c")


def _scatter_indices(dest):
    T = dest.shape[1]
    n_ch = T // (SC_WORKERS * SC_CHUNK)
    idx = dest.reshape(TOP_K, SC_WORKERS, n_ch, SC_CHUNK).transpose(1, 2, 0, 3)
    return idx.reshape(SC_WORKERS, n_ch * TOP_K, SC_CHUNK)


def _sc_dispatch(hp, idx, pad_idx, n_rows):
    T = hp.shape[0]
    per_w = T // SC_WORKERS
    n_ch = per_w // SC_CHUNK
    n_pc = PAD_SLOTS // (SC_WORKERS * SC_CHUNK)
    assert per_w % SC_CHUNK == 0 and n_ch >= 2 and n_ch % 2 == 0
    pidx = pad_idx.reshape(SC_WORKERS, n_pc, SC_CHUNK)
    zeros = jnp.zeros((SC_CHUNK, HALF), I32)

    @functools.partial(
        pl.kernel, mesh=_sc_mesh(),
        out_type=jax.ShapeDtypeStruct((n_rows, HALF), I32),
        scratch_types=[
            pltpu.VMEM((n_ch * TOP_K, SC_CHUNK), I32),
            pltpu.VMEM((n_pc, SC_CHUNK), I32),
            pltpu.VMEM((2, SC_CHUNK, HALF), I32),
            pltpu.SemaphoreType.DMA((2,)),
            pltpu.SemaphoreType.DMA((2,)),
        ],
        name="sc_dispatch",
    )
    def k(hp_hbm, idx_hbm, pidx_hbm, zeros_hbm, xs_hbm, idx_v, pidx_v, rows_v, lsem, ssem):
        wid = _sc_worker()
        base = wid * per_w
        pltpu.sync_copy(idx_hbm.at[wid], idx_v)
        pltpu.sync_copy(pidx_hbm.at[wid], pidx_v)

        pltpu.sync_copy(zeros_hbm, rows_v.at[0])
        for p in range(n_pc):
            pltpu.make_async_copy(rows_v.at[0], xs_hbm.at[pidx_v.at[p]], ssem.at[0]).start()
        for p in range(n_pc):
            pltpu.make_async_copy(rows_v.at[0], xs_hbm.at[pidx_v.at[p]], ssem.at[0]).wait()

        def load(i, slot):
            return pltpu.make_async_copy(hp_hbm.at[pl.ds(base + i * SC_CHUNK, SC_CHUNK)], rows_v.at[slot],
                                         lsem.at[slot])

        def scatter(i, kk, slot):
            return pltpu.make_async_copy(rows_v.at[slot], xs_hbm.at[idx_v.at[i * TOP_K + kk]], ssem.at[slot])

        load(0, 0).start()

        def body(i2, carry):
            for slot in range(2):
                i = i2 * 2 + slot
                nxt = 1 - slot

                @pl.when(i + 1 < n_ch)
                def _():
                    @pl.when(i >= 1)
                    def _():
                        for kk in range(TOP_K):
                            scatter(i - 1, kk, nxt).wait()
                    load(i + 1, nxt).start()

                load(i, slot).wait()
                for kk in range(TOP_K):
                    scatter(i, kk, slot).start()
            return carry

        lax.fori_loop(0, n_ch // 2, body, 0)
        for kk in range(TOP_K):
            scatter(n_ch - 2, kk, 0).wait()
            scatter(n_ch - 1, kk, 1).wait()

    return k(hp, idx, pidx, zeros)


def _sc_gather(table, idx):
    n = idx.shape[0]
    per_w = n // SC_WORKERS
    n_ch = per_w // SC_CHUNK
    assert per_w % SC_CHUNK == 0 and n_ch >= 2 and n_ch % 2 == 0

    @functools.partial(
        pl.kernel, mesh=_sc_mesh(),
        out_type=jax.ShapeDtypeStruct((n, HALF), I32),
        scratch_types=[
            pltpu.VMEM((n_ch, SC_CHUNK), I32),
            pltpu.VMEM((2, SC_CHUNK, HALF), I32),
            pltpu.SemaphoreType.DMA((2,)),
            pltpu.SemaphoreType.DMA((2,)),
        ],
        name="sc_gather",
    )
    def k(table_hbm, idx_hbm, out_hbm, idx_v, rows_v, gsem, wsem):
        wid = _sc_worker()
        base = wid * per_w
        pltpu.sync_copy(idx_hbm.at[wid], idx_v)

        def gather(i, slot):
            return pltpu.make_async_copy(table_hbm.at[idx_v.at[i]], rows_v.at[slot], gsem.at[slot])

        def writeback(i, slot):
            return pltpu.make_async_copy(rows_v.at[slot], out_hbm.at[pl.ds(base + i * SC_CHUNK, SC_CHUNK)],
                                         wsem.at[slot])

        gather(0, 0).start()

        def body(i2, carry):
            for slot in range(2):
                i = i2 * 2 + slot
                nxt = 1 - slot

                @pl.when(i + 1 < n_ch)
                def _():
                    @pl.when(i >= 1)
                    def _():
                        writeback(i - 1, nxt).wait()
                    gather(i + 1, nxt).start()

                gather(i, slot).wait()
                writeback(i, slot).start()
            return carry

        lax.fori_loop(0, n_ch // 2, body, 0)
        writeback(n_ch - 2, 0).wait()
        writeback(n_ch - 1, 1).wait()

    return k(table, idx.reshape(SC_WORKERS, n_ch, SC_CHUNK))


FF_CHUNK = 1024


def _expert_body(te_ref, nv_ref, x_ref, w1f_ref, b1_ref, w2f_ref, b2_ref, y_ref, w1_ref, w2_ref):
    i = pl.program_id(0)

    @pl.when(jnp.logical_or(i == 0, te_ref[i] != te_ref[jnp.maximum(i - 1, 0)]))
    def _():
        w1_ref[...] = w1f_ref[...].astype(BF16)
        w2_ref[...] = w2f_ref[...].astype(BF16)

    @pl.when(i < nv_ref[0])
    def _():
        lo, hi = _unpack_rows(x_ref[...])
        xb = jnp.concatenate([lo.astype(BF16), hi.astype(BF16)], axis=-1)
        acc = jnp.zeros((EXPERT_TM, D_MODEL), F32) + b2_ref[...]
        for c in range(D_FF // FF_CHUNK):
            def up(off):
                cs = slice(off + c * FF_CHUNK, off + (c + 1) * FF_CHUNK)
                return jnp.dot(xb, w1_ref[:, cs], preferred_element_type=F32) + b1_ref[:, cs]
            g = jnp.minimum(up(0), SWIGLU_LIMIT)
            lin = jnp.clip(up(D_FF), -SWIGLU_LIMIT, SWIGLU_LIMIT)
            a = g * jax.nn.sigmoid(SWIGLU_ALPHA * g) * (lin + 1.0)
            acc = acc + jnp.dot(a.astype(BF16), w2_ref[c * FF_CHUNK:(c + 1) * FF_CHUNK, :],
                                preferred_element_type=F32)
        y_ref[...] = _pack_rows(acc)


def _experts(tile_expert, n_valid, xs, w1, b1, w2, b2):
    n_rows = xs.shape[0]
    tm = EXPERT_TM
    n_tiles = n_rows // tm
    row = lambda i, te, nv: (jnp.minimum(i, nv[0] - 1), 0)
    grid_spec = pltpu.PrefetchScalarGridSpec(
        num_scalar_prefetch=2,
        grid=(n_tiles,),
        in_specs=[
            pl.BlockSpec((tm, HALF), row),
            pl.BlockSpec((None, D_MODEL, 2 * D_FF), lambda i, te, nv: (te[i], 0, 0)),
            pl.BlockSpec((None, 1, 2 * D_FF), lambda i, te, nv: (te[i], 0, 0)),
            pl.BlockSpec((None, D_FF, D_MODEL), lambda i, te, nv: (te[i], 0, 0)),
            pl.BlockSpec((None, 1, D_MODEL), lambda i, te, nv: (te[i], 0, 0)),
        ],
        out_specs=pl.BlockSpec((tm, HALF), row),
        scratch_shapes=[pltpu.VMEM((D_MODEL, 2 * D_FF), BF16), pltpu.VMEM((D_FF, D_MODEL), BF16)],
    )
    return pl.pallas_call(
        _expert_body,
        grid_spec=grid_spec,
        out_shape=jax.ShapeDtypeStruct((n_rows, HALF), I32),
        compiler_params=_cparams(("arbitrary",)),
        name="experts",
    )(tile_expert, n_valid, xs, w1, b1, w2, b2)


COMBINE_TM = 1024


def _combine_body(y0_ref, y1_ref, y2_ref, y3_ref, rw_ref, x_ref, g_ref, *rest):
    o_ref = rest[-1]
    acc = x_ref[...]
    rw = jnp.concatenate([rw_ref[...], jnp.zeros((LANES - 2 * TOP_K, COMBINE_TM), F32)], axis=0).T
    for k, y_ref in enumerate((y0_ref, y1_ref, y2_ref, y3_ref)):
        lo, hi = _unpack_rows(y_ref[...])
        acc = acc + rw[:, k:k + 1] * jnp.concatenate([lo, hi], axis=-1)
    o_ref[...] = _rms(acc, g_ref[...])


def _combine(yg, rw, x2, g, part, out_prev):
    T = x2.shape[0]
    tm = COMBINE_TM
    nt = T // tm
    in_specs = [
        pl.BlockSpec((tm, HALF), lambda i: (i, 0)),
        pl.BlockSpec((tm, HALF), lambda i: (nt + i, 0)),
        pl.BlockSpec((tm, HALF), lambda i: (2 * nt + i, 0)),
        pl.BlockSpec((tm, HALF), lambda i: (3 * nt + i, 0)),
        pl.BlockSpec((2 * TOP_K, tm), lambda i: (0, i)),
        pl.BlockSpec((tm, D_MODEL), lambda i: (i, 0)),
        pl.BlockSpec((1, D_MODEL), lambda i: (0, 0)),
    ]
    args = [yg, yg, yg, yg, rw, x2, g]
    aliases = {}
    if out_prev is not None:
        in_specs.append(pl.BlockSpec(memory_space=pl.ANY))
        args.append(out_prev)
        aliases = {len(args) - 1: 0}
    return pl.pallas_call(
        _combine_body,
        grid=(nt,),
        in_specs=in_specs,
        out_specs=pl.BlockSpec((tm, D_MODEL), lambda i: (part * nt + i, 0)),
        out_shape=jax.ShapeDtypeStruct((T * MOE_PARTS, D_MODEL), F32),
        input_output_aliases=aliases,
        compiler_params=_cparams(("parallel",)),
        name="combine",
    )(*args)


def _pad_lanes(v):
    v = v.reshape(1, -1).astype(F32)
    return jnp.pad(v, ((0, 0), (0, LANES - v.shape[1])))


def _layer(x2d, mem2d, B, S, M, norm_mix, w_in, b_ml_gates, conv_ml, ml_head_norm, b_fx_gate, norm_mem,
           w_mem_kv, w_branch, w_out, norm_moe, w_router, b_router, w_exp_in, b_exp_in, w_exp_out,
           b_exp_out, norm_out):
    T = B * S
    w16 = w_in.astype(BF16)
    w_big = jnp.concatenate([w16[:, 0:2048], w16[:, 2056:3080], w16[:, 3080:6152], w16[:, 6160:7184],
                             w16[:, 7184:10256]], axis=1)
    w_small = jnp.concatenate([w16[:, 2048:2056], w16[:, 6152:6160]], axis=1)
    w_small = jnp.pad(w_small, ((0, 0), (0, LANES - w_small.shape[1])))
    row = lambda v: v.reshape(1, -1).astype(F32)

    proj, small = _inproj(x2d, row(norm_mix), w_big, w_small)

    y_ml = _mlstm(proj, small, conv_ml.astype(F32), _pad_lanes(b_ml_gates), row(ml_head_norm), B, S)

    b_fx = jnp.pad(b_fx_gate.reshape(1, -1).astype(F32), ((0, 0), (2 * ML_HEADS, LANES - 2 * ML_HEADS - FX_HEADS)))
    y_fx = _fox_attn(proj, _fox_gate(small, b_fx, B, S), B, S)

    kv = _memkv(mem2d, row(norm_mem), w_mem_kv.astype(BF16))
    y_ca = _memattn(proj, kv, B, S, M)

    w_r = w_router.T.astype(BF16)
    moe_weights = (w_exp_in.astype(F32), b_exp_in.reshape(N_EXPERTS, 1, -1).astype(F32), w_exp_out.astype(F32),
                   b_exp_out.reshape(N_EXPERTS, 1, -1).astype(F32))
    staged = []
    for part in range(MOE_PARTS):
        x2, hp, ri, rw, cnt = _merge(y_ml, y_fx, y_ca, proj, x2d, w_branch.astype(BF16), w_out.astype(BF16),
                                     row(norm_moe), w_r, b_router.reshape(N_EXPERTS, 1).astype(F32), part)
        scatter_idx, dest, pad_idx, tile_e, n_valid, n_rows = _moe_plan(ri, cnt)
        staged.append((x2, rw, dest, tile_e, n_valid, _sc_dispatch(hp, scatter_idx, pad_idx, n_rows)))
    gathered = [_sc_gather(_experts(tile_e, n_valid, xs, *moe_weights), dest.reshape(-1))
                for _, _, dest, tile_e, n_valid, xs in staged]
    out = None
    for part, ((x2, rw, *_), yg) in enumerate(zip(staged, gathered)):
        out = _combine(yg, rw, x2, row(norm_out), part, out)
    return out


def _moe_plan(ri, cnt):
    T = ri.shape[1]
    tm = EXPERT_TM
    n_tiles = (T * TOP_K) // tm + N_EXPERTS
    counts = cnt[:, 0].astype(I32)
    padded = ((counts + tm - 1) // tm) * tm
    gend = jnp.cumsum(padded)
    gstart = gend - padded
    expert_ids = jnp.arange(N_EXPERTS, dtype=I32)
    start_of = jnp.sum(jnp.where(ri[0:TOP_K, :, None] == expert_ids, gstart, 0), axis=-1)
    dest = start_of + ri[TOP_K:2 * TOP_K, :]
    n_valid = gend[-1] // tm
    tile_ids = jnp.arange(n_tiles, dtype=I32)
    last_tile = jnp.minimum(tile_ids, n_valid - 1)
    tile_e = jnp.minimum(jnp.sum((gend[None, :] <= last_tile[:, None] * tm).astype(I32), axis=1), N_EXPERTS - 1)

    slot = jnp.arange(tm, dtype=I32)
    spare = n_tiles * tm + slot % SC_CHUNK
    pad_idx = jnp.where(slot[None, :] < (padded - counts)[:, None], (gstart + counts)[:, None] + slot[None, :],
                        spare[None, :]).reshape(-1)

    return (_scatter_indices(dest), dest, pad_idx, tile_e.astype(I32), n_valid.reshape(1).astype(I32),
            n_tiles * tm + SC_CHUNK)


def kernel(x, mem, norm_mix, w_in, b_ml_gates, conv_ml, ml_head_norm, b_fx_gate, norm_mem, w_mem_kv, w_branch,
           w_out, norm_moe, w_router, b_router, w_exp_in, b_exp_in, w_exp_out, b_exp_out, norm_final):
    B, S, D = x.shape
    M = mem.shape[1]
    depth = norm_mix.shape[0]
    assert depth == 1, "the combine kernel fuses the final norm, so exactly one layer is supported"
    assert D == D_MODEL and S % ML_BLOCK == 0 and S % FX_T == 0 and S % CA_TQ == 0
    out = _layer(x.reshape(B * S, D), mem.reshape(B * M, D), B, S, M, norm_mix[0], w_in[0], b_ml_gates[0],
                 conv_ml[0], ml_head_norm[0], b_fx_gate[0], norm_mem[0], w_mem_kv[0], w_branch[0], w_out[0],
                 norm_moe[0], w_router[0], b_router[0], w_exp_in[0], b_exp_in[0], w_exp_out[0], b_exp_out[0],
                 norm_final)
    return out.reshape(B, S, D)
```

```python
import functools

import jax
import jax.numpy as jnp
from jax import lax
from jax.experimental import pallas as pl
from jax.experimental.pallas import tpu as pltpu
from jax.experimental.pallas import tpu_sc as plsc

F32 = jnp.float32
BF16 = jnp.bfloat16
I32 = jnp.int32

D_MODEL = 1024
ML_HEADS = 4
ML_DQK = 128
ML_DV = 256
ML_CONV = 4
FX_HEADS = 8
FX_DH = 128
CA_HEADS = 4
CA_DH = 256
N_EXPERTS = 32
TOP_K = 4
D_FF = D_MODEL
SWIGLU_LIMIT = 7.0
SWIGLU_ALPHA = 1.702
EPS = 1e-5
LANES = 128
HALF = D_MODEL // 2
HI_MASK = -65536

COL_MLQK, COL_MLV, COL_MLO, COL_FXQ, COL_FXK, COL_FXV, COL_CAQ, COL_GATE0 = 0, 1, 2, 3, 4, 5, 6, 7
N_BIG = 10 * D_MODEL

VMEM_LIMIT = 56 * 1024 * 1024


def _cparams(sem):
    return pltpu.CompilerParams(dimension_semantics=sem, vmem_limit_bytes=VMEM_LIMIT)


def _rms(x, g):
    return x * lax.rsqrt(jnp.mean(x * x, axis=-1, keepdims=True) + EPS) * g


def _log_sigmoid(x):
    return jnp.minimum(x, 0.0) - jnp.log1p(jnp.exp(-jnp.abs(x)))


def _pack_rows(y):
    bits = lax.bitcast_convert_type(y.astype(BF16).astype(F32), I32)
    return lax.shift_right_logical(bits[:, :HALF], 16) | (bits[:, HALF:] & HI_MASK)


def _unpack_rows(w):
    lo = lax.bitcast_convert_type(lax.shift_left(w, 16), F32)
    hi = lax.bitcast_convert_type(w & HI_MASK, F32)
    return lo, hi


def _inproj_body(x_ref, g_ref, w_ref, ws_ref, o_ref, os_ref, h_ref):
    @pl.when(pl.program_id(1) == 0)
    def _():
        hb = _rms(x_ref[...], g_ref[...]).astype(BF16)
        h_ref[...] = hb
        os_ref[...] = jnp.dot(hb, ws_ref[...], preferred_element_type=F32)

    o_ref[...] = jnp.dot(h_ref[...], w_ref[...], preferred_element_type=F32).astype(BF16)


def _inproj(x2d, g, w_big, w_small):
    T = x2d.shape[0]
    tm = min(1024, T)
    tn = 2560
    return pl.pallas_call(
        _inproj_body,
        grid=(T // tm, N_BIG // tn),
        in_specs=[
            pl.BlockSpec((tm, D_MODEL), lambda i, j: (i, 0)),
            pl.BlockSpec((1, D_MODEL), lambda i, j: (0, 0)),
            pl.BlockSpec((D_MODEL, tn), lambda i, j: (0, j)),
            pl.BlockSpec((D_MODEL, LANES), lambda i, j: (0, 0)),
        ],
        out_specs=[
            pl.BlockSpec((tm, tn), lambda i, j: (i, j)),
            pl.BlockSpec((tm, LANES), lambda i, j: (i, 0)),
        ],
        out_shape=[
            jax.ShapeDtypeStruct((T, N_BIG), BF16),
            jax.ShapeDtypeStruct((T, LANES), F32),
        ],
        scratch_shapes=[pltpu.VMEM((tm, D_MODEL), BF16)],
        compiler_params=_cparams(("parallel", "arbitrary")),
        name="inproj",
    )(x2d, g, w_big, w_small)


ML_BLOCK = 1024
ML_MB = 1
ML_CHUNK = 128
CONV_PAD = 8


def _mlstm_body(qk_ref, v_ref, o_ref, g_ref, cw_ref, bg_ref, hn_ref, y_ref, xbuf, c_st, n_st, m_st):
    L = ML_CHUNK

    @pl.when(pl.program_id(1) == 0)
    def _():
        xbuf[:, 0:CONV_PAD, :] = jnp.zeros((ML_MB, CONV_PAD, D_MODEL), F32)
        c_st[...] = jnp.zeros_like(c_st)
        n_st[...] = jnp.zeros_like(n_st)
        m_st[...] = jnp.zeros_like(m_st)

    for bb in range(ML_MB):
        xbuf[bb, CONV_PAD:CONV_PAD + ML_BLOCK, :] = qk_ref[bb].astype(F32)
    cw = cw_ref[...]
    row = lax.broadcasted_iota(I32, (L, L), 0)
    col = lax.broadcasted_iota(I32, (L, L), 1)
    tri = (row >= col).astype(BF16)
    causal_t = col >= row
    bg = bg_ref[...]
    scale = ML_DQK ** -0.5
    nt_dims = (((1,), (1,)), ((), ()))

    def chunk(bb, c):
        r0 = c * L
        conv = cw[0:1, :] * xbuf[bb, r0 + CONV_PAD - 3:r0 + CONV_PAD - 3 + L, :]
        for j in range(1, ML_CONV):
            s0 = r0 + CONV_PAD - 3 + j
            conv = conv + cw[j:j + 1, :] * xbuf[bb, s0:s0 + L, :]
        act = conv * jax.nn.sigmoid(conv)

        gates = g_ref[bb, r0:r0 + L, :] + bg
        lf = _log_sigmoid(gates)
        cum = jnp.zeros((L, LANES), F32)
        for _ in range(3):
            piece = lf.astype(BF16)
            cum = cum + jnp.dot(tri, piece, preferred_element_type=F32)
            lf = lf - piece.astype(F32)
        gates_t = gates.T
        cum_t = cum.T
        for h in range(ML_HEADS):
            b_row = cum_t[ML_HEADS + h:ML_HEADS + h + 1, :]
            i_row = gates_t[h:h + 1, :]
            a_col = gates[:, h:h + 1] - cum[:, ML_HEADS + h:ML_HEADS + h + 1]
            st = bb * ML_HEADS + h
            m_prev = m_st[st]
            dm = jnp.where(causal_t, a_col + b_row, -jnp.inf)
            m_inter = b_row + m_prev
            m_t = jnp.maximum(jnp.max(dm, axis=0, keepdims=True), m_inter)
            w_intra = jnp.exp(dm - m_t)
            w_inter = jnp.exp(m_inter - m_t)

            qb = (act[:, h * ML_DQK:(h + 1) * ML_DQK] * scale).astype(BF16)
            kb = act[:, (ML_HEADS + h) * ML_DQK:(ML_HEADS + h + 1) * ML_DQK].astype(BF16)
            v_t = v_ref[bb, r0:r0 + L, h * ML_DV:(h + 1) * ML_DV].astype(F32).T
            p_t = lax.dot_general(kb, qb, nt_dims, preferred_element_type=F32) * w_intra
            c_old = c_st[st]
            n_old = n_st[st]
            num = jnp.dot(v_t.astype(BF16), p_t.astype(BF16), preferred_element_type=F32) + w_inter * (
                lax.dot_general(c_old.astype(BF16), qb, nt_dims, preferred_element_type=F32))
            qn = lax.dot_general(jnp.broadcast_to(n_old, (8, ML_DQK)).astype(BF16), qb, nt_dims,
                                 preferred_element_type=F32)[0:1, :]
            den = jnp.sum(p_t, axis=0, keepdims=True) + w_inter * qn
            hv = num / jnp.maximum(jnp.abs(den), jnp.exp(-m_t))

            m_new = m_t[:, L - 1:L]
            b_last = b_row[:, L - 1:L]
            wk = jnp.exp(b_last - b_row + i_row - m_new)
            decay = jnp.exp(b_last + m_prev - m_new)
            c_st[st] = decay * c_old + jnp.dot((v_t * wk).astype(BF16), kb, preferred_element_type=F32)
            n_st[st] = decay * n_old + jnp.dot(jnp.broadcast_to(wk, (8, L)).astype(BF16), kb,
                                               preferred_element_type=F32)[0:1, :]
            m_st[st] = m_new

            hn = (hv * lax.rsqrt(jnp.mean(hv * hv, axis=0, keepdims=True) + EPS)).T
            og = o_ref[bb, r0:r0 + L, h * ML_DV:(h + 1) * ML_DV].astype(F32)
            y_ref[bb, r0:r0 + L, h * ML_DV:(h + 1) * ML_DV] = (
                hn * hn_ref[:, h * ML_DV:(h + 1) * ML_DV] * jax.nn.sigmoid(og)).astype(BF16)

    for c in range(ML_BLOCK // L):
        for bb in range(ML_MB):
            chunk(bb, c)

    xbuf[:, 0:CONV_PAD, :] = xbuf[:, ML_BLOCK:ML_BLOCK + CONV_PAD, :]


def _mlstm(proj, small, conv_w, b_gates, head_norm, B, S):
    T = B * S
    ns = S // ML_BLOCK
    assert B % ML_MB == 0
    proj3 = proj.reshape(B, S, N_BIG)
    blk = lambda col: pl.BlockSpec((ML_MB, ML_BLOCK, D_MODEL), lambda b, s: (b, s, col))
    out = pl.pallas_call(
        _mlstm_body,
        grid=(B // ML_MB, ns),
        in_specs=[
            blk(COL_MLQK),
            blk(COL_MLV),
            blk(COL_MLO),
            pl.BlockSpec((ML_MB, ML_BLOCK, LANES), lambda b, s: (b, s, 0)),
            pl.BlockSpec((ML_CONV, D_MODEL), lambda b, s: (0, 0)),
            pl.BlockSpec((1, LANES), lambda b, s: (0, 0)),
            pl.BlockSpec((1, D_MODEL), lambda b, s: (0, 0)),
        ],
        out_specs=blk(0),
        out_shape=jax.ShapeDtypeStruct((B, S, D_MODEL), BF16),
        scratch_shapes=[
            pltpu.VMEM((ML_MB, ML_BLOCK + CONV_PAD, D_MODEL), F32),
            pltpu.VMEM((ML_MB * ML_HEADS, ML_DV, ML_DQK), F32),
            pltpu.VMEM((ML_MB * ML_HEADS, 1, ML_DQK), F32),
            pltpu.VMEM((ML_MB * ML_HEADS, 1, 1), F32),
        ],
        compiler_params=_cparams(("parallel", "arbitrary")),
        name="mlstm",
    )(proj3, proj3, proj3, small.reshape(B, S, LANES), conv_w, b_gates, head_norm)
    return out.reshape(T, D_MODEL)


FX_T = 512
FX_HP = 2
FX_VR = FX_DH + 16
LOG2E = 1.4426950408889634
N_PIECES = 3
FX_GATE_T = 128


def _fox_gate_body(g_ref, b_ref, o_ref):
    S = g_ref.shape[0]
    row = lax.broadcasted_iota(I32, (FX_GATE_T, FX_GATE_T), 0)
    col = lax.broadcasted_iota(I32, (FX_GATE_T, FX_GATE_T), 1)
    tri = (row >= col).astype(BF16)
    carry = jnp.zeros((1, LANES), F32)
    for blk in range(S // FX_GATE_T):
        rows = slice(blk * FX_GATE_T, (blk + 1) * FX_GATE_T)
        lf = _log_sigmoid(g_ref[rows, :] + b_ref[...])
        cum = carry
        for _ in range(N_PIECES):
            piece = lf.astype(BF16)
            cum = cum + jnp.dot(tri, piece, preferred_element_type=F32)
            lf = lf - piece.astype(F32)
        carry = cum[FX_GATE_T - 1:FX_GATE_T, :]
        o_ref[rows, :] = cum * (-LOG2E)


def _fox_gate(small, b_fx, B, S):
    return pl.pallas_call(
        _fox_gate_body,
        grid=(B,),
        in_specs=[
            pl.BlockSpec((S, LANES), lambda b: (b, 0)),
            pl.BlockSpec((1, LANES), lambda b: (0, 0)),
        ],
        out_specs=pl.BlockSpec((S, LANES), lambda b: (b, 0)),
        out_shape=jax.ShapeDtypeStruct((B * S, LANES), F32),
        compiler_params=_cparams(("parallel",)),
        name="fox_gate",
    )(small, b_fx)


def _fox_attn_body(q_ref, k_ref, v_ref, c_ref, o_ref, kx_ref, vt_ref, m_ref, acc_ref, s_ref):
    S = k_ref.shape[0]
    nq = S // FX_T

    c = c_ref[...]
    hi = c.astype(BF16)
    r1 = c - hi.astype(F32)
    mid = r1.astype(BF16)
    lo = (r1 - mid.astype(F32)).astype(BF16)
    sel_row = lax.broadcasted_iota(I32, (LANES, LANES), 0)
    sel_col = lax.broadcasted_iota(I32, (LANES, LANES), 1)
    ones_rows = (lax.broadcasted_iota(I32, (FX_VR - FX_DH, FX_T), 0) == 0).astype(BF16)
    head_slices = [slice(hh * FX_DH, (hh + 1) * FX_DH) for hh in range(FX_HP)]
    for hh, sl in enumerate(head_slices):
        lane = 2 * ML_HEADS + pl.program_id(1) * FX_HP + hh
        pieces = None
        for p, part in enumerate((hi, mid, lo)):
            pick = jnp.logical_and(sel_row == lane, sel_col == p).astype(BF16)
            t = jnp.dot(part, pick, preferred_element_type=F32)
            pieces = t if pieces is None else pieces + t
        kx_ref[hh, :, 0:FX_DH] = k_ref[:, sl]
        kx_ref[hh, :, FX_DH:2 * FX_DH] = pieces.astype(BF16)
        for j in range(nq):
            vt = v_ref[j * FX_T:(j + 1) * FX_T, sl].astype(F32).T.astype(BF16)
            vt_ref[hh, j] = jnp.concatenate([vt, ones_rows], axis=0)

    piece_rows = (lax.broadcasted_iota(I32, (FX_DH, FX_T), 0) < N_PIECES).astype(BF16)

    def start(i):
        q_x = []
        for sl in head_slices:
            q_t = (q_ref[i * FX_T:(i + 1) * FX_T, sl].astype(F32) * (FX_DH ** -0.5 * LOG2E)).T.astype(BF16)
            q_x.append(jnp.concatenate([q_t, piece_rows], axis=0))
        m_ref[i % 2] = jnp.full(m_ref.shape[1:], -jnp.inf, F32)
        acc_ref[i % 2] = jnp.zeros(acc_ref.shape[1:], F32)
        return q_x

    def key_rows(j):
        return pl.ds(j * FX_T, FX_T) if isinstance(j, int) else pl.ds(pl.multiple_of(j * FX_T, FX_T), FX_T)

    def scores(q_x, j, slot):
        for hh in range(FX_HP):
            s_ref[slot, hh] = jnp.dot(kx_ref[hh, key_rows(j), :], q_x[hh], preferred_element_type=F32)

    def consume(par, j, slot, masked):
        for hh in range(FX_HP):
            s = s_ref[slot, hh]
            if masked:
                key = lax.broadcasted_iota(I32, (FX_T, FX_T), 0)
                qry = lax.broadcasted_iota(I32, (FX_T, FX_T), 1)
                s = jnp.where(qry >= key, s, -jnp.inf)
            m_old = m_ref[par, hh]
            m_new = jnp.maximum(m_old, jnp.max(s, axis=0, keepdims=True))
            p = jnp.exp2(s - m_new).astype(BF16)
            acc_ref[par, hh] = jnp.exp2(m_old - m_new) * acc_ref[par, hh] + jnp.dot(
                vt_ref[hh, j], p, preferred_element_type=F32)
            m_ref[par, hh] = m_new

    def finish(i, slot):
        consume(i % 2, i, slot, True)
        for hh, sl in enumerate(head_slices):
            acc = acc_ref[i % 2, hh]
            o_ref[i * FX_T:(i + 1) * FX_T, sl] = (acc[0:FX_DH, :] / acc[FX_DH:FX_DH + 1, :]).T.astype(BF16)

    diag_slot = 0
    for i in range(nq):
        q_x = start(i)
        first = 0 if i == 0 else 1 - diag_slot
        scores(q_x, 0, first)
        if i > 0:
            finish(i - 1, diag_slot)

        def pair(jj, carry, q_x=q_x, first=first, par=i % 2):
            j = 2 * jj
            scores(q_x, j + 1, 1 - first)
            consume(par, j, first, False)
            scores(q_x, j + 2, first)
            consume(par, j + 1, 1 - first, False)
            return carry

        if i >= 2:
            lax.fori_loop(0, i // 2, pair, 0)
        if i % 2 == 1:
            scores(q_x, i, 1 - first)
            consume(i % 2, i - 1, first, False)
            diag_slot = 1 - first
        else:
            diag_slot = first
    finish(nq - 1, diag_slot)


def _fox_attn(proj, c_neg, B, S):
    T = B * S
    nq = S // FX_T
    wide = FX_HP * FX_DH
    cq = COL_FXQ * (D_MODEL // wide)
    ck = COL_FXK * (D_MODEL // wide)
    cv = COL_FXV * (D_MODEL // wide)
    proj3 = proj.reshape(B, S, N_BIG)
    out = pl.pallas_call(
        _fox_attn_body,
        grid=(B, FX_HEADS // FX_HP),
        in_specs=[
            pl.BlockSpec((None, S, wide), lambda b, h: (b, 0, cq + h)),
            pl.BlockSpec((None, S, wide), lambda b, h: (b, 0, ck + h)),
            pl.BlockSpec((None, S, wide), lambda b, h: (b, 0, cv + h)),
            pl.BlockSpec((None, S, LANES), lambda b, h: (b, 0, 0)),
        ],
        out_specs=pl.BlockSpec((None, S, wide), lambda b, h: (b, 0, h)),
        out_shape=jax.ShapeDtypeStruct((B, S, D_MODEL), BF16),
        scratch_shapes=[
            pltpu.VMEM((FX_HP, S, 2 * FX_DH), BF16),
            pltpu.VMEM((FX_HP, nq, FX_VR, FX_T), BF16),
            pltpu.VMEM((2, FX_HP, 1, FX_T), F32),
            pltpu.VMEM((2, FX_HP, FX_VR, FX_T), F32),
            pltpu.VMEM((2, FX_HP, FX_T, FX_T), F32),
        ],
        compiler_params=_cparams(("parallel", "parallel")),
        name="fox_attn",
    )(proj3, proj3, proj3, c_neg.reshape(B, S, LANES))
    return out.reshape(T, D_MODEL)


def _memkv_body(x_ref, g_ref, w_ref, o_ref):
    hb = _rms(x_ref[...], g_ref[...]).astype(BF16)
    o_ref[...] = jnp.dot(hb, w_ref[...], preferred_element_type=F32).astype(BF16)


def _memkv(mem2d, g, w_kv):
    R = mem2d.shape[0]
    tm = min(512, R)
    N = w_kv.shape[1]
    return pl.pallas_call(
        _memkv_body,
        grid=(R // tm,),
        in_specs=[
            pl.BlockSpec((tm, D_MODEL), lambda i: (i, 0)),
            pl.BlockSpec((1, D_MODEL), lambda i: (0, 0)),
            pl.BlockSpec((D_MODEL, N), lambda i: (0, 0)),
        ],
        out_specs=pl.BlockSpec((tm, N), lambda i: (i, 0)),
        out_shape=jax.ShapeDtypeStruct((R, N), BF16),
        compiler_params=_cparams(("parallel",)),
        name="memkv",
    )(mem2d, g, w_kv)


CA_TQ = 2048


def _memattn_body(q_ref, k_ref, v_ref, o_ref):
    scale = CA_DH ** -0.5
    for h in range(CA_HEADS):
        sl = slice(h * CA_DH, (h + 1) * CA_DH)
        s = lax.dot_general(q_ref[:, sl], k_ref[:, sl], (((1,), (1,)), ((), ())),
                            preferred_element_type=F32) * scale
        p = jnp.exp(s - jnp.max(s, axis=-1, keepdims=True))
        l = jnp.sum(p, axis=-1, keepdims=True)
        o = jnp.dot(p.astype(BF16), v_ref[:, sl], preferred_element_type=F32) / l
        o_ref[:, sl] = o.astype(BF16)


def _memattn(proj, kv, B, S, M):
    T = B * S
    nq = S // CA_TQ
    kv3 = kv.reshape(B, M, 2 * D_MODEL)
    return pl.pallas_call(
        _memattn_body,
        grid=(B, nq),
        in_specs=[
            pl.BlockSpec((CA_TQ, D_MODEL), lambda b, i: (b * nq + i, COL_CAQ)),
            pl.BlockSpec((None, M, D_MODEL), lambda b, i: (b, 0, 0)),
            pl.BlockSpec((None, M, D_MODEL), lambda b, i: (b, 0, 1)),
        ],
        out_specs=pl.BlockSpec((CA_TQ, D_MODEL), lambda b, i: (b * nq + i, 0)),
        out_shape=jax.ShapeDtypeStruct((T, D_MODEL), BF16),
        compiler_params=_cparams(("parallel", "arbitrary")),
        name="memattn",
    )(proj, kv3, kv3)


MERGE_TM = 512
MOE_PARTS = 2


def _merge_body(y0_ref, y1_ref, y2_ref, g0_ref, g1_ref, g2_ref, x_ref, wb_ref, wo_ref, gn_ref, wr_ref, br_ref,
                o_ref, hp_ref, ri_ref, rw_ref, cnt_ref, carry_ref):
    merged = None
    for n, (y_ref, g_ref) in enumerate(((y0_ref, g0_ref), (y1_ref, g1_ref), (y2_ref, g2_ref))):
        p = jnp.dot(y_ref[...], wb_ref[n], preferred_element_type=F32)
        t = jax.nn.sigmoid(g_ref[...].astype(F32)) * p
        merged = t if merged is None else merged + t
    x2 = x_ref[...] + jnp.dot(merged.astype(BF16), wo_ref[...], preferred_element_type=F32)
    o_ref[...] = x2
    _route(x2, gn_ref, wr_ref, br_ref, hp_ref, ri_ref, rw_ref, cnt_ref, carry_ref)


def _merge(y_ml, y_fx, y_ca, proj, x2d, w_branch, w_out, g_moe, w_router, b_router, part):
    T = x2d.shape[0] // MOE_PARTS
    tm = MERGE_TM
    off = part * (T // tm)
    src = lambda i: (off + i, 0)
    row = lambda i: (i, 0)
    const = lambda i: (0, 0)
    return pl.pallas_call(
        _merge_body,
        grid=(T // tm,),
        in_specs=[
            pl.BlockSpec((tm, D_MODEL), src),
            pl.BlockSpec((tm, D_MODEL), src),
            pl.BlockSpec((tm, D_MODEL), src),
            pl.BlockSpec((tm, D_MODEL), lambda i: (off + i, COL_GATE0)),
            pl.BlockSpec((tm, D_MODEL), lambda i: (off + i, COL_GATE0 + 1)),
            pl.BlockSpec((tm, D_MODEL), lambda i: (off + i, COL_GATE0 + 2)),
            pl.BlockSpec((tm, D_MODEL), src),
            pl.BlockSpec((3, D_MODEL, D_MODEL), lambda i: (0, 0, 0)),
            pl.BlockSpec((D_MODEL, D_MODEL), const),
            pl.BlockSpec((1, D_MODEL), const),
            pl.BlockSpec((N_EXPERTS, D_MODEL), const),
            pl.BlockSpec((N_EXPERTS, 1), const),
        ],
        out_specs=[
            pl.BlockSpec((tm, D_MODEL), row),
            pl.BlockSpec((tm, HALF), row),
            pl.BlockSpec((2 * TOP_K, tm), lambda i: (0, i)),
            pl.BlockSpec((2 * TOP_K, tm), lambda i: (0, i)),
            pl.BlockSpec((N_EXPERTS, 1), const),
        ],
        out_shape=[
            jax.ShapeDtypeStruct((T, D_MODEL), F32),
            jax.ShapeDtypeStruct((T, HALF), I32),
            jax.ShapeDtypeStruct((2 * TOP_K, T), I32),
            jax.ShapeDtypeStruct((2 * TOP_K, T), F32),
            jax.ShapeDtypeStruct((N_EXPERTS, 1), F32),
        ],
        scratch_shapes=[pltpu.VMEM((N_EXPERTS, 1), F32)],
        compiler_params=_cparams(("arbitrary",)),
        name="merge_router",
    )(y_ml, y_fx, y_ca, proj, proj, proj, x2d, w_branch, w_out, g_moe, w_router, b_router)


def _route(x2, g_ref, wr_ref, br_ref, hp_ref, ri_ref, rw_ref, cnt_ref, carry_ref):
    tm = MERGE_TM

    @pl.when(pl.program_id(0) == 0)
    def _():
        carry_ref[...] = jnp.zeros_like(carry_ref)

    h = _rms(x2, g_ref[...])
    hp_ref[...] = _pack_rows(h)
    logits = lax.dot_general(wr_ref[...], h.astype(BF16), (((1,), (1,)), ((), ())),
                             preferred_element_type=F32) + br_ref[...]
    eid = lax.broadcasted_iota(I32, (N_EXPERTS, tm), 0).astype(F32)

    work = logits
    onehot_sum = jnp.zeros((N_EXPERTS, tm), F32)
    vals, sels, idxs = [], [], []
    for _ in range(TOP_K):
        mx = jnp.max(work, axis=0, keepdims=True)
        idx = jnp.min(jnp.where(work == mx, eid, float(N_EXPERTS)), axis=0, keepdims=True)
        sel = eid == idx
        onehot_sum = onehot_sum + sel.astype(F32)
        work = jnp.where(sel, -jnp.inf, work)
        vals.append(mx)
        sels.append(sel)
        idxs.append(idx)
    exps = [jnp.exp(v - vals[0]) for v in vals]
    total = exps[0] + exps[1] + exps[2] + exps[3]

    earlier = (lax.broadcasted_iota(I32, (tm, tm), 0) < lax.broadcasted_iota(I32, (tm, tm), 1)).astype(BF16)
    before = jnp.dot(onehot_sum.astype(BF16), earlier, preferred_element_type=F32) + carry_ref[...]
    carry_ref[...] = carry_ref[...] + jnp.sum(onehot_sum, axis=1, keepdims=True)
    cnt_ref[...] = carry_ref[...]

    out_row = lax.broadcasted_iota(I32, (2 * TOP_K, tm), 0)
    ri = jnp.zeros((2 * TOP_K, tm), I32)
    rw = jnp.zeros((2 * TOP_K, tm), F32)
    for k in range(TOP_K):
        rank = jnp.sum(jnp.where(sels[k], before, 0.0), axis=0, keepdims=True)
        ri = jnp.where(out_row == k, idxs[k].astype(I32), ri)
        ri = jnp.where(out_row == TOP_K + k, rank.astype(I32), ri)
        rw = jnp.where(out_row == k, exps[k] / total, rw)
    ri_ref[...] = ri
    rw_ref[...] = rw


EXPERT_TM = 512
SC_CORES = 2
SC_SUBCORES = 16
SC_WORKERS = SC_CORES * SC_SUBCORES
SC_CHUNK = 64
PAD_SLOTS = N_EXPERTS * EXPERT_TM


def _sc_mesh():
    return plsc.VectorSubcoreMesh(core_axis_name="c", subcore_axis_name="s")


def _sc_worker():
    return lax.axis_index("s") * SC_CORES + lax.axis_index("c")


def _scatter_indices(dest):
    T = dest.shape[1]
    n_ch = T // (SC_WORKERS * SC_CHUNK)
    idx = dest.reshape(TOP_K, SC_WORKERS, n_ch, SC_CHUNK).transpose(1, 2, 0, 3)
    return idx.reshape(SC_WORKERS, n_ch * TOP_K, SC_CHUNK)


def _sc_dispatch(hp, idx, pad_idx, n_rows):
    T = hp.shape[0]
    per_w = T // SC_WORKERS
    n_ch = per_w // SC_CHUNK
    n_pc = PAD_SLOTS // (SC_WORKERS * SC_CHUNK)
    assert per_w % SC_CHUNK == 0 and n_ch >= 2 and n_ch % 2 == 0
    pidx = pad_idx.reshape(SC_WORKERS, n_pc, SC_CHUNK)
    zeros = jnp.zeros((SC_CHUNK, HALF), I32)

    @functools.partial(
        pl.kernel, mesh=_sc_mesh(),
        out_type=jax.ShapeDtypeStruct((n_rows, HALF), I32),
        scratch_types=[
            pltpu.VMEM((n_ch * TOP_K, SC_CHUNK), I32),
            pltpu.VMEM((n_pc, SC_CHUNK), I32),
            pltpu.VMEM((2, SC_CHUNK, HALF), I32),
            pltpu.SemaphoreType.DMA((2,)),
            pltpu.SemaphoreType.DMA((2,)),
        ],
        name="sc_dispatch",
    )
    def k(hp_hbm, idx_hbm, pidx_hbm, zeros_hbm, xs_hbm, idx_v, pidx_v, rows_v, lsem, ssem):
        wid = _sc_worker()
        base = wid * per_w
        pltpu.sync_copy(idx_hbm.at[wid], idx_v)
        pltpu.sync_copy(pidx_hbm.at[wid], pidx_v)

        pltpu.sync_copy(zeros_hbm, rows_v.at[0])
        for p in range(n_pc):
            pltpu.make_async_copy(rows_v.at[0], xs_hbm.at[pidx_v.at[p]], ssem.at[0]).start()
        for p in range(n_pc):
            pltpu.make_async_copy(rows_v.at[0], xs_hbm.at[pidx_v.at[p]], ssem.at[0]).wait()

        def load(i, slot):
            return pltpu.make_async_copy(hp_hbm.at[pl.ds(base + i * SC_CHUNK, SC_CHUNK)], rows_v.at[slot],
                                         lsem.at[slot])

        def scatter(i, kk, slot):
            return pltpu.make_async_copy(rows_v.at[slot], xs_hbm.at[idx_v.at[i * TOP_K + kk]], ssem.at[slot])

        load(0, 0).start()

        def body(i2, carry):
            for slot in range(2):
                i = i2 * 2 + slot
                nxt = 1 - slot

                @pl.when(i + 1 < n_ch)
                def _():
                    @pl.when(i >= 1)
                    def _():
                        for kk in range(TOP_K):
                            scatter(i - 1, kk, nxt).wait()
                    load(i + 1, nxt).start()

                load(i, slot).wait()
                for kk in range(TOP_K):
                    scatter(i, kk, slot).start()
            return carry

        lax.fori_loop(0, n_ch // 2, body, 0)
        for kk in range(TOP_K):
            scatter(n_ch - 2, kk, 0).wait()
            scatter(n_ch - 1, kk, 1).wait()

    return k(hp, idx, pidx, zeros)


def _sc_gather(table, idx):
    n = idx.shape[0]
    per_w = n // SC_WORKERS
    n_ch = per_w // SC_CHUNK
    assert per_w % SC_CHUNK == 0 and n_ch >= 2 and n_ch % 2 == 0

    @functools.partial(
        pl.kernel, mesh=_sc_mesh(),
        out_type=jax.ShapeDtypeStruct((n, HALF), I32),
        scratch_types=[
            pltpu.VMEM((n_ch, SC_CHUNK), I32),
            pltpu.VMEM((2, SC_CHUNK, HALF), I32),
            pltpu.SemaphoreType.DMA((2,)),
            pltpu.SemaphoreType.DMA((2,)),
        ],
        name="sc_gather",
    )
    def k(table_hbm, idx_hbm, out_hbm, idx_v, rows_v, gsem, wsem):
        wid = _sc_worker()
        base = wid * per_w
        pltpu.sync_copy(idx_hbm.at[wid], idx_v)

        def gather(i, slot):
            return pltpu.make_async_copy(table_hbm.at[idx_v.at[i]], rows_v.at[slot], gsem.at[slot])

        def writeback(i, slot):
            return pltpu.make_async_copy(rows_v.at[slot], out_hbm.at[pl.ds(base + i * SC_CHUNK, SC_CHUNK)],
                                         wsem.at[slot])

        gather(0, 0).start()

        def body(i2, carry):
            for slot in range(2):
                i = i2 * 2 + slot
                nxt = 1 - slot

                @pl.when(i + 1 < n_ch)
                def _():
                    @pl.when(i >= 1)
                    def _():
                        writeback(i - 1, nxt).wait()
                    gather(i + 1, nxt).start()

                gather(i, slot).wait()
                writeback(i, slot).start()
            return carry

        lax.fori_loop(0, n_ch // 2, body, 0)
        writeback(n_ch - 2, 0).wait()
        writeback(n_ch - 1, 1).wait()

    return k(table, idx.reshape(SC_WORKERS, n_ch, SC_CHUNK))


FF_CHUNK = 1024


def _expert_body(te_ref, nv_ref, x_ref, w1f_ref, b1_ref, w2f_ref, b2_ref, y_ref, w1_ref, w2_ref):
    i = pl.program_id(0)

    @pl.when(jnp.logical_or(i == 0, te_ref[i] != te_ref[jnp.maximum(i - 1, 0)]))
    def _():
        w1_ref[...] = w1f_ref[...].astype(BF16)
        w2_ref[...] = w2f_ref[...].astype(BF16)

    @pl.when(i < nv_ref[0])
    def _():
        lo, hi = _unpack_rows(x_ref[...])
        xb = jnp.concatenate([lo.astype(BF16), hi.astype(BF16)], axis=-1)
        acc = jnp.zeros((EXPERT_TM, D_MODEL), F32) + b2_ref[...]
        for c in range(D_FF // FF_CHUNK):
            def up(off):
                cs = slice(off + c * FF_CHUNK, off + (c + 1) * FF_CHUNK)
                return jnp.dot(xb, w1_ref[:, cs], preferred_element_type=F32) + b1_ref[:, cs]
            g = jnp.minimum(up(0), SWIGLU_LIMIT)
            lin = jnp.clip(up(D_FF), -SWIGLU_LIMIT, SWIGLU_LIMIT)
            a = g * jax.nn.sigmoid(SWIGLU_ALPHA * g) * (lin + 1.0)
            acc = acc + jnp.dot(a.astype(BF16), w2_ref[c * FF_CHUNK:(c + 1) * FF_CHUNK, :],
                                preferred_element_type=F32)
        y_ref[...] = _pack_rows(acc)


def _experts(tile_expert, n_valid, xs, w1, b1, w2, b2):
    n_rows = xs.shape[0]
    tm = EXPERT_TM
    n_tiles = n_rows // tm
    row = lambda i, te, nv: (jnp.minimum(i, nv[0] - 1), 0)
    grid_spec = pltpu.PrefetchScalarGridSpec(
        num_scalar_prefetch=2,
        grid=(n_tiles,),
        in_specs=[
            pl.BlockSpec((tm, HALF), row),
            pl.BlockSpec((None, D_MODEL, 2 * D_FF), lambda i, te, nv: (te[i], 0, 0)),
            pl.BlockSpec((None, 1, 2 * D_FF), lambda i, te, nv: (te[i], 0, 0)),
            pl.BlockSpec((None, D_FF, D_MODEL), lambda i, te, nv: (te[i], 0, 0)),
            pl.BlockSpec((None, 1, D_MODEL), lambda i, te, nv: (te[i], 0, 0)),
        ],
        out_specs=pl.BlockSpec((tm, HALF), row),
        scratch_shapes=[pltpu.VMEM((D_MODEL, 2 * D_FF), BF16), pltpu.VMEM((D_FF, D_MODEL), BF16)],
    )
    return pl.pallas_call(
        _expert_body,
        grid_spec=grid_spec,
        out_shape=jax.ShapeDtypeStruct((n_rows, HALF), I32),
        compiler_params=_cparams(("arbitrary",)),
        name="experts",
    )(tile_expert, n_valid, xs, w1, b1, w2, b2)


COMBINE_TM = 1024


def _combine_body(y0_ref, y1_ref, y2_ref, y3_ref, rw_ref, x_ref, g_ref, *rest):
    o_ref = rest[-1]
    acc = x_ref[...]
    rw = jnp.concatenate([rw_ref[...], jnp.zeros((LANES - 2 * TOP_K, COMBINE_TM), F32)], axis=0).T
    for k, y_ref in enumerate((y0_ref, y1_ref, y2_ref, y3_ref)):
        lo, hi = _unpack_rows(y_ref[...])
        acc = acc + rw[:, k:k + 1] * jnp.concatenate([lo, hi], axis=-1)
    o_ref[...] = _rms(acc, g_ref[...])


def _combine(yg, rw, x2, g, part, out_prev):
    T = x2.shape[0]
    tm = COMBINE_TM
    nt = T // tm
    in_specs = [
        pl.BlockSpec((tm, HALF), lambda i: (i, 0)),
        pl.BlockSpec((tm, HALF), lambda i: (nt + i, 0)),
        pl.BlockSpec((tm, HALF), lambda i: (2 * nt + i, 0)),
        pl.BlockSpec((tm, HALF), lambda i: (3 * nt + i, 0)),
        pl.BlockSpec((2 * TOP_K, tm), lambda i: (0, i)),
        pl.BlockSpec((tm, D_MODEL), lambda i: (i, 0)),
        pl.BlockSpec((1, D_MODEL), lambda i: (0, 0)),
    ]
    args = [yg, yg, yg, yg, rw, x2, g]
    aliases = {}
    if out_prev is not None:
        in_specs.append(pl.BlockSpec(memory_space=pl.ANY))
        args.append(out_prev)
        aliases = {len(args) - 1: 0}
    return pl.pallas_call(
        _combine_body,
        grid=(nt,),
        in_specs=in_specs,
        out_specs=pl.BlockSpec((tm, D_MODEL), lambda i: (part * nt + i, 0)),
        out_shape=jax.ShapeDtypeStruct((T * MOE_PARTS, D_MODEL), F32),
        input_output_aliases=aliases,
        compiler_params=_cparams(("parallel",)),
        name="combine",
    )(*args)


def _pad_lanes(v):
    v = v.reshape(1, -1).astype(F32)
    return jnp.pad(v, ((0, 0), (0, LANES - v.shape[1])))


def _layer(x2d, mem2d, B, S, M, norm_mix, w_in, b_ml_gates, conv_ml, ml_head_norm, b_fx_gate, norm_mem,
           w_mem_kv, w_branch, w_out, norm_moe, w_router, b_router, w_exp_in, b_exp_in, w_exp_out,
           b_exp_out, norm_out):
    T = B * S
    w16 = w_in.astype(BF16)
    w_big = jnp.concatenate([w16[:, 0:2048], w16[:, 2056:3080], w16[:, 3080:6152], w16[:, 6160:7184],
                             w16[:, 7184:10256]], axis=1)
    w_small = jnp.concatenate([w16[:, 2048:2056], w16[:, 6152:6160]], axis=1)
    w_small = jnp.pad(w_small, ((0, 0), (0, LANES - w_small.shape[1])))
    row = lambda v: v.reshape(1, -1).astype(F32)

    proj, small = _inproj(x2d, row(norm_mix), w_big, w_small)

    y_ml = _mlstm(proj, small, conv_ml.astype(F32), _pad_lanes(b_ml_gates), row(ml_head_norm), B, S)

    b_fx = jnp.pad(b_fx_gate.reshape(1, -1).astype(F32), ((0, 0), (2 * ML_HEADS, LANES - 2 * ML_HEADS - FX_HEADS)))
    y_fx = _fox_attn(proj, _fox_gate(small, b_fx, B, S), B, S)

    kv = _memkv(mem2d, row(norm_mem), w_mem_kv.astype(BF16))
    y_ca = _memattn(proj, kv, B, S, M)

    w_r = w_router.T.astype(BF16)
    moe_weights = (w_exp_in.astype(F32), b_exp_in.reshape(N_EXPERTS, 1, -1).astype(F32), w_exp_out.astype(F32),
                   b_exp_out.reshape(N_EXPERTS, 1, -1).astype(F32))
    staged = []
    for part in range(MOE_PARTS):
        x2, hp, ri, rw, cnt = _merge(y_ml, y_fx, y_ca, proj, x2d, w_branch.astype(BF16), w_out.astype(BF16),
                                     row(norm_moe), w_r, b_router.reshape(N_EXPERTS, 1).astype(F32), part)
        scatter_idx, dest, pad_idx, tile_e, n_valid, n_rows = _moe_plan(ri, cnt)
        staged.append((x2, rw, dest, tile_e, n_valid, _sc_dispatch(hp, scatter_idx, pad_idx, n_rows)))
    gathered = [_sc_gather(_experts(tile_e, n_valid, xs, *moe_weights), dest.reshape(-1))
                for _, _, dest, tile_e, n_valid, xs in staged]
    out = None
    for part, ((x2, rw, *_), yg) in enumerate(zip(staged, gathered)):
        out = _combine(yg, rw, x2, row(norm_out), part, out)
    return out


def _moe_plan(ri, cnt):
    T = ri.shape[1]
    tm = EXPERT_TM
    n_tiles = (T * TOP_K) // tm + N_EXPERTS
    counts = cnt[:, 0].astype(I32)
    padded = ((counts + tm - 1) // tm) * tm
    gend = jnp.cumsum(padded)
    gstart = gend - padded
    expert_ids = jnp.arange(N_EXPERTS, dtype=I32)
    start_of = jnp.sum(jnp.where(ri[0:TOP_K, :, None] == expert_ids, gstart, 0), axis=-1)
    dest = start_of + ri[TOP_K:2 * TOP_K, :]
    n_valid = gend[-1] // tm
    tile_ids = jnp.arange(n_tiles, dtype=I32)
    last_tile = jnp.minimum(tile_ids, n_valid - 1)
    tile_e = jnp.minimum(jnp.sum((gend[None, :] <= last_tile[:, None] * tm).astype(I32), axis=1), N_EXPERTS - 1)

    slot = jnp.arange(tm, dtype=I32)
    spare = n_tiles * tm + slot % SC_CHUNK
    pad_idx = jnp.where(slot[None, :] < (padded - counts)[:, None], (gstart + counts)[:, None] + slot[None, :],
                        spare[None, :]).reshape(-1)

    return (_scatter_indices(dest), dest, pad_idx, tile_e.astype(I32), n_valid.reshape(1).astype(I32),
            n_tiles * tm + SC_CHUNK)


def kernel(x, mem, norm_mix, w_in, b_ml_gates, conv_ml, ml_head_norm, b_fx_gate, norm_mem, w_mem_kv, w_branch,
           w_out, norm_moe, w_router, b_router, w_exp_in, b_exp_in, w_exp_out, b_exp_out, norm_final):
    B, S, D = x.shape
    M = mem.shape[1]
    depth = norm_mix.shape[0]
    assert depth == 1, "the combine kernel fuses the final norm, so exactly one layer is supported"
    assert D == D_MODEL and S % ML_BLOCK == 0 and S % FX_T == 0 and S % CA_TQ == 0
    out = _layer(x.reshape(B * S, D), mem.reshape(B * M, D), B, S, M, norm_mix[0], w_in[0], b_ml_gates[0],
                 conv_ml[0], ml_head_norm[0], b_fx_gate[0], norm_mem[0], w_mem_kv[0], w_branch[0], w_out[0],
                 norm_moe[0], w_router[0], b_router[0], w_exp_in[0], b_exp_in[0], w_exp_out[0], b_exp_out[0],
                 norm_final)
    return out.reshape(B, S, D)
```

```python
import functools

import jax
import jax.numpy as jnp
from jax import lax
from jax.experimental import pallas as pl
from jax.experimental.pallas import tpu as pltpu
from jax.experimental.pallas import tpu_sc as plsc

F32 = jnp.float32
BF16 = jnp.bfloat16
I32 = jnp.int32

D_MODEL = 1024
ML_HEADS = 4
ML_DQK = 128
ML_DV = 256
ML_CONV = 4
FX_HEADS = 8
FX_DH = 128
CA_HEADS = 4
CA_DH = 256
N_EXPERTS = 32
TOP_K = 4
D_FF = D_MODEL
SWIGLU_LIMIT = 7.0
SWIGLU_ALPHA = 1.702
EPS = 1e-5
LANES = 128
HALF = D_MODEL // 2
HI_MASK = -65536

COL_MLQK, COL_MLV, COL_MLO, COL_FXQ, COL_FXK, COL_FXV, COL_CAQ, COL_GATE0 = 0, 1, 2, 3, 4, 5, 6, 7
N_BIG = 10 * D_MODEL

VMEM_LIMIT = 56 * 1024 * 1024


def _cparams(sem):
    return pltpu.CompilerParams(dimension_semantics=sem, vmem_limit_bytes=VMEM_LIMIT)


def _rms(x, g):
    return x * lax.rsqrt(jnp.mean(x * x, axis=-1, keepdims=True) + EPS) * g


def _log_sigmoid(x):
    return jnp.minimum(x, 0.0) - jnp.log1p(jnp.exp(-jnp.abs(x)))


def _pack_rows(y):
    bits = lax.bitcast_convert_type(y.astype(BF16).astype(F32), I32)
    return lax.shift_right_logical(bits[:, :HALF], 16) | (bits[:, HALF:] & HI_MASK)


def _unpack_rows(w):
    lo = lax.bitcast_convert_type(lax.shift_left(w, 16), F32)
    hi = lax.bitcast_convert_type(w & HI_MASK, F32)
    return lo, hi


def _inproj_body(x_ref, g_ref, w_ref, ws_ref, o_ref, os_ref, h_ref):
    @pl.when(pl.program_id(1) == 0)
    def _():
        hb = _rms(x_ref[...], g_ref[...]).astype(BF16)
        h_ref[...] = hb
        os_ref[...] = jnp.dot(hb, ws_ref[...], preferred_element_type=F32)

    o_ref[...] = jnp.dot(h_ref[...], w_ref[...], preferred_element_type=F32).astype(BF16)


def _inproj(x2d, g, w_big, w_small):
    T = x2d.shape[0]
    tm = min(1024, T)
    tn = 2560
    return pl.pallas_call(
        _inproj_body,
        grid=(T // tm, N_BIG // tn),
        in_specs=[
            pl.BlockSpec((tm, D_MODEL), lambda i, j: (i, 0)),
            pl.BlockSpec((1, D_MODEL), lambda i, j: (0, 0)),
            pl.BlockSpec((D_MODEL, tn), lambda i, j: (0, j)),
            pl.BlockSpec((D_MODEL, LANES), lambda i, j: (0, 0)),
        ],
        out_specs=[
            pl.BlockSpec((tm, tn), lambda i, j: (i, j)),
            pl.BlockSpec((tm, LANES), lambda i, j: (i, 0)),
        ],
        out_shape=[
            jax.ShapeDtypeStruct((T, N_BIG), BF16),
            jax.ShapeDtypeStruct((T, LANES), F32),
        ],
        scratch_shapes=[pltpu.VMEM((tm, D_MODEL), BF16)],
        compiler_params=_cparams(("parallel", "arbitrary")),
        name="inproj",
    )(x2d, g, w_big, w_small)


ML_BLOCK = 1024
ML_MB = 1
ML_CHUNK = 128
CONV_PAD = 8


def _mlstm_body(qk_ref, v_ref, o_ref, g_ref, cw_ref, bg_ref, hn_ref, y_ref, xbuf, c_st, n_st, m_st):
    L = ML_CHUNK

    @pl.when(pl.program_id(1) == 0)
    def _():
        xbuf[:, 0:CONV_PAD, :] = jnp.zeros((ML_MB, CONV_PAD, D_MODEL), F32)
        c_st[...] = jnp.zeros_like(c_st)
        n_st[...] = jnp.zeros_like(n_st)
        m_st[...] = jnp.zeros_like(m_st)

    for bb in range(ML_MB):
        xbuf[bb, CONV_PAD:CONV_PAD + ML_BLOCK, :] = qk_ref[bb].astype(F32)
    cw = cw_ref[...]
    row = lax.broadcasted_iota(I32, (L, L), 0)
    col = lax.broadcasted_iota(I32, (L, L), 1)
    tri = (row >= col).astype(BF16)
    causal_t = col >= row
    bg = bg_ref[...]
    scale = ML_DQK ** -0.5
    nt_dims = (((1,), (1,)), ((), ()))

    def chunk(bb, c):
        r0 = c * L
        conv = cw[0:1, :] * xbuf[bb, r0 + CONV_PAD - 3:r0 + CONV_PAD - 3 + L, :]
        for j in range(1, ML_CONV):
            s0 = r0 + CONV_PAD - 3 + j
            conv = conv + cw[j:j + 1, :] * xbuf[bb, s0:s0 + L, :]
        act = conv * jax.nn.sigmoid(conv)

        gates = g_ref[bb, r0:r0 + L, :] + bg
        lf = _log_sigmoid(gates)
        cum = jnp.zeros((L, LANES), F32)
        for _ in range(3):
            piece = lf.astype(BF16)
            cum = cum + jnp.dot(tri, piece, preferred_element_type=F32)
            lf = lf - piece.astype(F32)
        gates_t = gates.T
        cum_t = cum.T
        for h in range(ML_HEADS):
            b_row = cum_t[ML_HEADS + h:ML_HEADS + h + 1, :]
            i_row = gates_t[h:h + 1, :]
            a_col = gates[:, h:h + 1] - cum[:, ML_HEADS + h:ML_HEADS + h + 1]
            st = bb * ML_HEADS + h
            m_prev = m_st[st]
            dm = jnp.where(causal_t, a_col + b_row, -jnp.inf)
            m_inter = b_row + m_prev
            m_t = jnp.maximum(jnp.max(dm, axis=0, keepdims=True), m_inter)
            w_intra = jnp.exp(dm - m_t)
            w_inter = jnp.exp(m_inter - m_t)

            qb = (act[:, h * ML_DQK:(h + 1) * ML_DQK] * scale).astype(BF16)
            kb = act[:, (ML_HEADS + h) * ML_DQK:(ML_HEADS + h + 1) * ML_DQK].astype(BF16)
            v_t = v_ref[bb, r0:r0 + L, h * ML_DV:(h + 1) * ML_DV].astype(F32).T
            p_t = lax.dot_general(kb, qb, nt_dims, preferred_element_type=F32) * w_intra
            c_old = c_st[st]
            n_old = n_st[st]
            num = jnp.dot(v_t.astype(BF16), p_t.astype(BF16), preferred_element_type=F32) + w_inter * (
                lax.dot_general(c_old.astype(BF16), qb, nt_dims, preferred_element_type=F32))
            qn = lax.dot_general(jnp.broadcast_to(n_old, (8, ML_DQK)).astype(BF16), qb, nt_dims,
                                 preferred_element_type=F32)[0:1, :]
            den = jnp.sum(p_t, axis=0, keepdims=True) + w_inter * qn
            hv = num / jnp.maximum(jnp.abs(den), jnp.exp(-m_t))

            m_new = m_t[:, L - 1:L]
            b_last = b_row[:, L - 1:L]
            wk = jnp.exp(b_last - b_row + i_row - m_new)
            decay = jnp.exp(b_last + m_prev - m_new)
            c_st[st] = decay * c_old + jnp.dot((v_t * wk).astype(BF16), kb, preferred_element_type=F32)
            n_st[st] = decay * n_old + jnp.dot(jnp.broadcast_to(wk, (8, L)).astype(BF16), kb,
                                               preferred_element_type=F32)[0:1, :]
            m_st[st] = m_new

            hn = (hv * lax.rsqrt(jnp.mean(hv * hv, axis=0, keepdims=True) + EPS)).T
            og = o_ref[bb, r0:r0 + L, h * ML_DV:(h + 1) * ML_DV].astype(F32)
            y_ref[bb, r0:r0 + L, h * ML_DV:(h + 1) * ML_DV] = (
                hn * hn_ref[:, h * ML_DV:(h + 1) * ML_DV] * jax.nn.sigmoid(og)).astype(BF16)

    for c in range(ML_BLOCK // L):
        for bb in range(ML_MB):
            chunk(bb, c)

    xbuf[:, 0:CONV_PAD, :] = xbuf[:, ML_BLOCK:ML_BLOCK + CONV_PAD, :]


def _mlstm(proj, small, conv_w, b_gates, head_norm, B, S):
    T = B * S
    ns = S // ML_BLOCK
    assert B % ML_MB == 0
    proj3 = proj.reshape(B, S, N_BIG)
    blk = lambda col: pl.BlockSpec((ML_MB, ML_BLOCK, D_MODEL), lambda b, s: (b, s, col))
    out = pl.pallas_call(
        _mlstm_body,
        grid=(B // ML_MB, ns),
        in_specs=[
            blk(COL_MLQK),
            blk(COL_MLV),
            blk(COL_MLO),
            pl.BlockSpec((ML_MB, ML_BLOCK, LANES), lambda b, s: (b, s, 0)),
            pl.BlockSpec((ML_CONV, D_MODEL), lambda b, s: (0, 0)),
            pl.BlockSpec((1, LANES), lambda b, s: (0, 0)),
            pl.BlockSpec((1, D_MODEL), lambda b, s: (0, 0)),
        ],
        out_specs=blk(0),
        out_shape=jax.ShapeDtypeStruct((B, S, D_MODEL), BF16),
        scratch_shapes=[
            pltpu.VMEM((ML_MB, ML_BLOCK + CONV_PAD, D_MODEL), F32),
            pltpu.VMEM((ML_MB * ML_HEADS, ML_DV, ML_DQK), F32),
            pltpu.VMEM((ML_MB * ML_HEADS, 1, ML_DQK), F32),
            pltpu.VMEM((ML_MB * ML_HEADS, 1, 1), F32),
        ],
        compiler_params=_cparams(("parallel", "arbitrary")),
        name="mlstm",
    )(proj3, proj3, proj3, small.reshape(B, S, LANES), conv_w, b_gates, head_norm)
    return out.reshape(T, D_MODEL)


FX_T = 512
FX_HP = 2
FX_VR = FX_DH + 16
LOG2E = 1.4426950408889634
N_PIECES = 3
FX_GATE_T = 128


def _fox_gate_body(g_ref, b_ref, o_ref):
    S = g_ref.shape[0]
    row = lax.broadcasted_iota(I32, (FX_GATE_T, FX_GATE_T), 0)
    col = lax.broadcasted_iota(I32, (FX_GATE_T, FX_GATE_T), 1)
    tri = (row >= col).astype(BF16)
    carry = jnp.zeros((1, LANES), F32)
    for blk in range(S // FX_GATE_T):
        rows = slice(blk * FX_GATE_T, (blk + 1) * FX_GATE_T)
        lf = _log_sigmoid(g_ref[rows, :] + b_ref[...])
        cum = carry
        for _ in range(N_PIECES):
            piece = lf.astype(BF16)
            cum = cum + jnp.dot(tri, piece, preferred_element_type=F32)
            lf = lf - piece.astype(F32)
        carry = cum[FX_GATE_T - 1:FX_GATE_T, :]
        o_ref[rows, :] = cum * (-LOG2E)


def _fox_gate(small, b_fx, B, S):
    return pl.pallas_call(
        _fox_gate_body,
        grid=(B,),
        in_specs=[
            pl.BlockSpec((S, LANES), lambda b: (b, 0)),
            pl.BlockSpec((1, LANES), lambda b: (0, 0)),
        ],
        out_specs=pl.BlockSpec((S, LANES), lambda b: (b, 0)),
        out_shape=jax.ShapeDtypeStruct((B * S, LANES), F32),
        compiler_params=_cparams(("parallel",)),
        name="fox_gate",
    )(small, b_fx)


def _fox_attn_body(q_ref, k_ref, v_ref, c_ref, o_ref, kx_ref, vt_ref, m_ref, acc_ref, s_ref):
    S = k_ref.shape[0]
    nq = S // FX_T

    c = c_ref[...]
    hi = c.astype(BF16)
    r1 = c - hi.astype(F32)
    mid = r1.astype(BF16)
    lo = (r1 - mid.astype(F32)).astype(BF16)
    sel_row = lax.broadcasted_iota(I32, (LANES, LANES), 0)
    sel_col = lax.broadcasted_iota(I32, (LANES, LANES), 1)
    ones_rows = (lax.broadcasted_iota(I32, (FX_VR - FX_DH, FX_T), 0) == 0).astype(BF16)
    head_slices = [slice(hh * FX_DH, (hh + 1) * FX_DH) for hh in range(FX_HP)]
    for hh, sl in enumerate(head_slices):
        lane = 2 * ML_HEADS + pl.program_id(1) * FX_HP + hh
        pieces = None
        for p, part in enumerate((hi, mid, lo)):
            pick = jnp.logical_and(sel_row == lane, sel_col == p).astype(BF16)
            t = jnp.dot(part, pick, preferred_element_type=F32)
            pieces = t if pieces is None else pieces + t
        kx_ref[hh, :, 0:FX_DH] = k_ref[:, sl]
        kx_ref[hh, :, FX_DH:2 * FX_DH] = pieces.astype(BF16)
        for j in range(nq):
            vt = v_ref[j * FX_T:(j + 1) * FX_T, sl].astype(F32).T.astype(BF16)
            vt_ref[hh, j] = jnp.concatenate([vt, ones_rows], axis=0)

    piece_rows = (lax.broadcasted_iota(I32, (FX_DH, FX_T), 0) < N_PIECES).astype(BF16)

    def start(i):
        q_x = []
        for sl in head_slices:
            q_t = (q_ref[i * FX_T:(i + 1) * FX_T, sl].astype(F32) * (FX_DH ** -0.5 * LOG2E)).T.astype(BF16)
            q_x.append(jnp.concatenate([q_t, piece_rows], axis=0))
        m_ref[i % 2] = jnp.full(m_ref.shape[1:], -jnp.inf, F32)
        acc_ref[i % 2] = jnp.zeros(acc_ref.shape[1:], F32)
        return q_x

    def key_rows(j):
        return pl.ds(j * FX_T, FX_T) if isinstance(j, int) else pl.ds(pl.multiple_of(j * FX_T, FX_T), FX_T)

    def scores(q_x, j, slot):
        for hh in range(FX_HP):
            s_ref[slot, hh] = jnp.dot(kx_ref[hh, key_rows(j), :], q_x[hh], preferred_element_type=F32)

    def consume(par, j, slot, masked):
        for hh in range(FX_HP):
            s = s_ref[slot, hh]
            if masked:
                key = lax.broadcasted_iota(I32, (FX_T, FX_T), 0)
                qry = lax.broadcasted_iota(I32, (FX_T, FX_T), 1)
                s = jnp.where(qry >= key, s, -jnp.inf)
            m_old = m_ref[par, hh]
            m_new = jnp.maximum(m_old, jnp.max(s, axis=0, keepdims=True))
            p = jnp.exp2(s - m_new).astype(BF16)
            acc_ref[par, hh] = jnp.exp2(m_old - m_new) * acc_ref[par, hh] + jnp.dot(
                vt_ref[hh, j], p, preferred_element_type=F32)
            m_ref[par, hh] = m_new

    def finish(i, slot):
        consume(i % 2, i, slot, True)
        for hh, sl in enumerate(head_slices):
            acc = acc_ref[i % 2, hh]
            o_ref[i * FX_T:(i + 1) * FX_T, sl] = (acc[0:FX_DH, :] / acc[FX_DH:FX_DH + 1, :]).T.astype(BF16)

    diag_slot = 0
    for i in range(nq):
        q_x = start(i)
        first = 0 if i == 0 else 1 - diag_slot
        scores(q_x, 0, first)
        if i > 0:
            finish(i - 1, diag_slot)

        def pair(jj, carry, q_x=q_x, first=first, par=i % 2):
            j = 2 * jj
            scores(q_x, j + 1, 1 - first)
            consume(par, j, first, False)
            scores(q_x, j + 2, first)
            consume(par, j + 1, 1 - first, False)
            return carry

        if i >= 2:
            lax.fori_loop(0, i // 2, pair, 0)
        if i % 2 == 1:
            scores(q_x, i, 1 - first)
            consume(i % 2, i - 1, first, False)
            diag_slot = 1 - first
        else:
            diag_slot = first
    finish(nq - 1, diag_slot)


def _fox_attn(proj, c_neg, B, S):
    T = B * S
    nq = S // FX_T
    wide = FX_HP * FX_DH
    cq = COL_FXQ * (D_MODEL // wide)
    ck = COL_FXK * (D_MODEL // wide)
    cv = COL_FXV * (D_MODEL // wide)
    proj3 = proj.reshape(B, S, N_BIG)
    out = pl.pallas_call(
        _fox_attn_body,
        grid=(B, FX_HEADS // FX_HP),
        in_specs=[
            pl.BlockSpec((None, S, wide), lambda b, h: (b, 0, cq + h)),
            pl.BlockSpec((None, S, wide), lambda b, h: (b, 0, ck + h)),
            pl.BlockSpec((None, S, wide), lambda b, h: (b, 0, cv + h)),
            pl.BlockSpec((None, S, LANES), lambda b, h: (b, 0, 0)),
        ],
        out_specs=pl.BlockSpec((None, S, wide), lambda b, h: (b, 0, h)),
        out_shape=jax.ShapeDtypeStruct((B, S, D_MODEL), BF16),
        scratch_shapes=[
            pltpu.VMEM((FX_HP, S, 2 * FX_DH), BF16),
            pltpu.VMEM((FX_HP, nq, FX_VR, FX_T), BF16),
            pltpu.VMEM((2, FX_HP, 1, FX_T), F32),
            pltpu.VMEM((2, FX_HP, FX_VR, FX_T), F32),
            pltpu.VMEM((2, FX_HP, FX_T, FX_T), F32),
        ],
        compiler_params=_cparams(("parallel", "parallel")),
        name="fox_attn",
    )(proj3, proj3, proj3, c_neg.reshape(B, S, LANES))
    return out.reshape(T, D_MODEL)


def _memkv_body(x_ref, g_ref, w_ref, o_ref):
    hb = _rms(x_ref[...], g_ref[...]).astype(BF16)
    o_ref[...] = jnp.dot(hb, w_ref[...], preferred_element_type=F32).astype(BF16)


def _memkv(mem2d, g, w_kv):
    R = mem2d.shape[0]
    tm = min(512, R)
    N = w_kv.shape[1]
    return pl.pallas_call(
        _memkv_body,
        grid=(R // tm,),
        in_specs=[
            pl.BlockSpec((tm, D_MODEL), lambda i: (i, 0)),
            pl.BlockSpec((1, D_MODEL), lambda i: (0, 0)),
            pl.BlockSpec((D_MODEL, N), lambda i: (0, 0)),
        ],
        out_specs=pl.BlockSpec((tm, N), lambda i: (i, 0)),
        out_shape=jax.ShapeDtypeStruct((R, N), BF16),
        compiler_params=_cparams(("parallel",)),
        name="memkv",
    )(mem2d, g, w_kv)


CA_TQ = 2048


def _memattn_body(q_ref, k_ref, v_ref, o_ref):
    scale = CA_DH ** -0.5
    for h in range(CA_HEADS):
        sl = slice(h * CA_DH, (h + 1) * CA_DH)
        s = lax.dot_general(q_ref[:, sl], k_ref[:, sl], (((1,), (1,)), ((), ())),
                            preferred_element_type=F32) * scale
        p = jnp.exp(s - jnp.max(s, axis=-1, keepdims=True))
        l = jnp.sum(p, axis=-1, keepdims=True)
        o = jnp.dot(p.astype(BF16), v_ref[:, sl], preferred_element_type=F32) / l
        o_ref[:, sl] = o.astype(BF16)


def _memattn(proj, kv, B, S, M):
    T = B * S
    nq = S // CA_TQ
    kv3 = kv.reshape(B, M, 2 * D_MODEL)
    return pl.pallas_call(
        _memattn_body,
        grid=(B, nq),
        in_specs=[
            pl.BlockSpec((CA_TQ, D_MODEL), lambda b, i: (b * nq + i, COL_CAQ)),
            pl.BlockSpec((None, M, D_MODEL), lambda b, i: (b, 0, 0)),
            pl.BlockSpec((None, M, D_MODEL), lambda b, i: (b, 0, 1)),
        ],
        out_specs=pl.BlockSpec((CA_TQ, D_MODEL), lambda b, i: (b * nq + i, 0)),
        out_shape=jax.ShapeDtypeStruct((T, D_MODEL), BF16),
        compiler_params=_cparams(("parallel", "arbitrary")),
        name="memattn",
    )(proj, kv3, kv3)


MERGE_TM = 512
MOE_PARTS = 2


def _merge_body(y0_ref, y1_ref, y2_ref, g0_ref, g1_ref, g2_ref, x_ref, wb_ref, wo_ref, gn_ref, wr_ref, br_ref,
                o_ref, hp_ref, ri_ref, rw_ref, cnt_ref, carry_ref):
    merged = None
    for n, (y_ref, g_ref) in enumerate(((y0_ref, g0_ref), (y1_ref, g1_ref), (y2_ref, g2_ref))):
        p = jnp.dot(y_ref[...], wb_ref[n], preferred_element_type=F32)
        t = jax.nn.sigmoid(g_ref[...].astype(F32)) * p
        merged = t if merged is None else merged + t
    x2 = x_ref[...] + jnp.dot(merged.astype(BF16), wo_ref[...], preferred_element_type=F32)
    o_ref[...] = x2
    _route(x2, gn_ref, wr_ref, br_ref, hp_ref, ri_ref, rw_ref, cnt_ref, carry_ref)


def _merge(y_ml, y_fx, y_ca, proj, x2d, w_branch, w_out, g_moe, w_router, b_router, part):
    T = x2d.shape[0] // MOE_PARTS
    tm = MERGE_TM
    off = part * (T // tm)
    src = lambda i: (off + i, 0)
    row = lambda i: (i, 0)
    const = lambda i: (0, 0)
    return pl.pallas_call(
        _merge_body,
        grid=(T // tm,),
        in_specs=[
            pl.BlockSpec((tm, D_MODEL), src),
            pl.BlockSpec((tm, D_MODEL), src),
            pl.BlockSpec((tm, D_MODEL), src),
            pl.BlockSpec((tm, D_MODEL), lambda i: (off + i, COL_GATE0)),
            pl.BlockSpec((tm, D_MODEL), lambda i: (off + i, COL_GATE0 + 1)),
            pl.BlockSpec((tm, D_MODEL), lambda i: (off + i, COL_GATE0 + 2)),
            pl.BlockSpec((tm, D_MODEL), src),
            pl.BlockSpec((3, D_MODEL, D_MODEL), lambda i: (0, 0, 0)),
            pl.BlockSpec((D_MODEL, D_MODEL), const),
            pl.BlockSpec((1, D_MODEL), const),
            pl.BlockSpec((N_EXPERTS, D_MODEL), const),
            pl.BlockSpec((N_EXPERTS, 1), const),
        ],
        out_specs=[
            pl.BlockSpec((tm, D_MODEL), row),
            pl.BlockSpec((tm, HALF), row),
            pl.BlockSpec((2 * TOP_K, tm), lambda i: (0, i)),
            pl.BlockSpec((2 * TOP_K, tm), lambda i: (0, i)),
            pl.BlockSpec((N_EXPERTS, 1), const),
        ],
        out_shape=[
            jax.ShapeDtypeStruct((T, D_MODEL), F32),
            jax.ShapeDtypeStruct((T, HALF), I32),
            jax.ShapeDtypeStruct((2 * TOP_K, T), I32),
            jax.ShapeDtypeStruct((2 * TOP_K, T), F32),
            jax.ShapeDtypeStruct((N_EXPERTS, 1), F32),
        ],
        scratch_shapes=[pltpu.VMEM((N_EXPERTS, 1), F32)],
        compiler_params=_cparams(("arbitrary",)),
        name="merge_router",
    )(y_ml, y_fx, y_ca, proj, proj, proj, x2d, w_branch, w_out, g_moe, w_router, b_router)


def _route(x2, g_ref, wr_ref, br_ref, hp_ref, ri_ref, rw_ref, cnt_ref, carry_ref):
    tm = MERGE_TM

    @pl.when(pl.program_id(0) == 0)
    def _():
        carry_ref[...] = jnp.zeros_like(carry_ref)

    h = _rms(x2, g_ref[...])
    hp_ref[...] = _pack_rows(h)
    logits = lax.dot_general(wr_ref[...], h.astype(BF16), (((1,), (1,)), ((), ())),
                             preferred_element_type=F32) + br_ref[...]
    eid = lax.broadcasted_iota(I32, (N_EXPERTS, tm), 0).astype(F32)

    work = logits
    onehot_sum = jnp.zeros((N_EXPERTS, tm), F32)
    vals, sels, idxs = [], [], []
    for _ in range(TOP_K):
        mx = jnp.max(work, axis=0, keepdims=True)
        idx = jnp.min(jnp.where(work == mx, eid, float(N_EXPERTS)), axis=0, keepdims=True)
        sel = eid == idx
        onehot_sum = onehot_sum + sel.astype(F32)
        work = jnp.where(sel, -jnp.inf, work)
        vals.append(mx)
        sels.append(sel)
        idxs.append(idx)
    exps = [jnp.exp(v - vals[0]) for v in vals]
    total = exps[0] + exps[1] + exps[2] + exps[3]

    earlier = (lax.broadcasted_iota(I32, (tm, tm), 0) < lax.broadcasted_iota(I32, (tm, tm), 1)).astype(BF16)
    before = jnp.dot(onehot_sum.astype(BF16), earlier, preferred_element_type=F32) + carry_ref[...]
    carry_ref[...] = carry_ref[...] + jnp.sum(onehot_sum, axis=1, keepdims=True)
    cnt_ref[...] = carry_ref[...]

    out_row = lax.broadcasted_iota(I32, (2 * TOP_K, tm), 0)
    ri = jnp.zeros((2 * TOP_K, tm), I32)
    rw = jnp.zeros((2 * TOP_K, tm), F32)
    for k in range(TOP_K):
        rank = jnp.sum(jnp.where(sels[k], before, 0.0), axis=0, keepdims=True)
        ri = jnp.where(out_row == k, idxs[k].astype(I32), ri)
        ri = jnp.where(out_row == TOP_K + k, rank.astype(I32), ri)
        rw = jnp.where(out_row == k, exps[k] / total, rw)
    ri_ref[...] = ri
    rw_ref[...] = rw


EXPERT_TM = 512
SC_CORES = 2
SC_SUBCORES = 16
SC_WORKERS = SC_CORES * SC_SUBCORES
SC_CHUNK = 64
PAD_SLOTS = N_EXPERTS * EXPERT_TM


def _sc_mesh():
    return plsc.VectorSubcoreMesh(core_axis_name="c", subcore_axis_name="s")


def _sc_worker():
    return lax.axis_index("s") * SC_CORES + lax.axis_index("c")


def _scatter_indices(dest):
    T = dest.shape[1]
    n_ch = T // (SC_WORKERS * SC_CHUNK)
    idx = dest.reshape(TOP_K, SC_WORKERS, n_ch, SC_CHUNK).transpose(1, 2, 0, 3)
    return idx.reshape(SC_WORKERS, n_ch * TOP_K, SC_CHUNK)


def _sc_dispatch(hp, idx, pad_idx, n_rows):
    T = hp.shape[0]
    per_w = T // SC_WORKERS
    n_ch = per_w // SC_CHUNK
    n_pc = PAD_SLOTS // (SC_WORKERS * SC_CHUNK)
    assert per_w % SC_CHUNK == 0 and n_ch >= 2 and n_ch % 2 == 0
    pidx = pad_idx.reshape(SC_WORKERS, n_pc, SC_CHUNK)
    zeros = jnp.zeros((SC_CHUNK, HALF), I32)

    @functools.partial(
        pl.kernel, mesh=_sc_mesh(),
        out_type=jax.ShapeDtypeStruct((n_rows, HALF), I32),
        scratch_types=[
            pltpu.VMEM((n_ch * TOP_K, SC_CHUNK), I32),
            pltpu.VMEM((n_pc, SC_CHUNK), I32),
            pltpu.VMEM((2, SC_CHUNK, HALF), I32),
            pltpu.SemaphoreType.DMA((2,)),
            pltpu.SemaphoreType.DMA((2,)),
        ],
        name="sc_dispatch",
    )
    def k(hp_hbm, idx_hbm, pidx_hbm, zeros_hbm, xs_hbm, idx_v, pidx_v, rows_v, lsem, ssem):
        wid = _sc_worker()
        base = wid * per_w
        pltpu.sync_copy(idx_hbm.at[wid], idx_v)
        pltpu.sync_copy(pidx_hbm.at[wid], pidx_v)

        pltpu.sync_copy(zeros_hbm, rows_v.at[0])
        for p in range(n_pc):
            pltpu.make_async_copy(rows_v.at[0], xs_hbm.at[pidx_v.at[p]], ssem.at[0]).start()
        for p in range(n_pc):
            pltpu.make_async_copy(rows_v.at[0], xs_hbm.at[pidx_v.at[p]], ssem.at[0]).wait()

        def load(i, slot):
            return pltpu.make_async_copy(hp_hbm.at[pl.ds(base + i * SC_CHUNK, SC_CHUNK)], rows_v.at[slot],
                                         lsem.at[slot])

        def scatter(i, kk, slot):
            return pltpu.make_async_copy(rows_v.at[slot], xs_hbm.at[idx_v.at[i * TOP_K + kk]], ssem.at[slot])

        load(0, 0).start()

        def body(i2, carry):
            for slot in range(2):
                i = i2 * 2 + slot
                nxt = 1 - slot

                @pl.when(i + 1 < n_ch)
                def _():
                    @pl.when(i >= 1)
                    def _():
                        for kk in range(TOP_K):
                            scatter(i - 1, kk, nxt).wait()
                    load(i + 1, nxt).start()

                load(i, slot).wait()
                for kk in range(TOP_K):
                    scatter(i, kk, slot).start()
            return carry

        lax.fori_loop(0, n_ch // 2, body, 0)
        for kk in range(TOP_K):
            scatter(n_ch - 2, kk, 0).wait()
            scatter(n_ch - 1, kk, 1).wait()

    return k(hp, idx, pidx, zeros)


def _sc_gather(table, idx):
    n = idx.shape[0]
    per_w = n // SC_WORKERS
    n_ch = per_w // SC_CHUNK
    assert per_w % SC_CHUNK == 0 and n_ch >= 2 and n_ch % 2 == 0

    @functools.partial(
        pl.kernel, mesh=_sc_mesh(),
        out_type=jax.ShapeDtypeStruct((n, HALF), I32),
        scratch_types=[
            pltpu.VMEM((n_ch, SC_CHUNK), I32),
            pltpu.VMEM((2, SC_CHUNK, HALF), I32),
            pltpu.SemaphoreType.DMA((2,)),
            pltpu.SemaphoreType.DMA((2,)),
        ],
        name="sc_gather",
    )
    def k(table_hbm, idx_hbm, out_hbm, idx_v, rows_v, gsem, wsem):
        wid = _sc_worker()
        base = wid * per_w
        pltpu.sync_copy(idx_hbm.at[wid], idx_v)

        def gather(i, slot):
            return pltpu.make_async_copy(table_hbm.at[idx_v.at[i]], rows_v.at[slot], gsem.at[slot])

        def writeback(i, slot):
            return pltpu.make_async_copy(rows_v.at[slot], out_hbm.at[pl.ds(base + i * SC_CHUNK, SC_CHUNK)],
                                         wsem.at[slot])

        gather(0, 0).start()

        def body(i2, carry):
            for slot in range(2):
                i = i2 * 2 + slot
                nxt = 1 - slot

                @pl.when(i + 1 < n_ch)
                def _():
                    @pl.when(i >= 1)
                    def _():
                        writeback(i - 1, nxt).wait()
                    gather(i + 1, nxt).start()

                gather(i, slot).wait()
                writeback(i, slot).start()
            return carry

        lax.fori_loop(0, n_ch // 2, body, 0)
        writeback(n_ch - 2, 0).wait()
        writeback(n_ch - 1, 1).wait()

    return k(table, idx.reshape(SC_WORKERS, n_ch, SC_CHUNK))


FF_CHUNK = 1024


def _cast_body(w1f_ref, w2f_ref, after_ref, w1_ref, w2_ref):
    del after_ref
    w1_ref[...] = w1f_ref[...].astype(BF16)
    w2_ref[...] = w2f_ref[...].astype(BF16)


def _cast_expert_weights(w1, w2, after):
    return pl.pallas_call(
        _cast_body,
        grid=(N_EXPERTS,),
        in_specs=[
            pl.BlockSpec((None, D_MODEL, 2 * D_FF), lambda e: (e, 0, 0)),
            pl.BlockSpec((None, D_FF, D_MODEL), lambda e: (e, 0, 0)),
            pl.BlockSpec(memory_space=pl.ANY),
        ],
        out_specs=[
            pl.BlockSpec((None, D_MODEL, 2 * D_FF), lambda e: (e, 0, 0)),
            pl.BlockSpec((None, D_FF, D_MODEL), lambda e: (e, 0, 0)),
        ],
        out_shape=[jax.ShapeDtypeStruct(w1.shape, BF16), jax.ShapeDtypeStruct(w2.shape, BF16)],
        compiler_params=_cparams(("parallel",)),
        name="cast_experts",
    )(w1, w2, after)


def _expert_body(te_ref, nv_ref, x_ref, w1_ref, b1_ref, w2_ref, b2_ref, y_ref):
    del te_ref
    i = pl.program_id(0)

    @pl.when(i < nv_ref[0])
    def _():
        lo, hi = _unpack_rows(x_ref[...])
        xb = jnp.concatenate([lo.astype(BF16), hi.astype(BF16)], axis=-1)
        acc = jnp.zeros((EXPERT_TM, D_MODEL), F32) + b2_ref[...]
        for c in range(D_FF // FF_CHUNK):
            def up(off):
                cs = slice(off + c * FF_CHUNK, off + (c + 1) * FF_CHUNK)
                return jnp.dot(xb, w1_ref[:, cs], preferred_element_type=F32) + b1_ref[:, cs]
            g = jnp.minimum(up(0), SWIGLU_LIMIT)
            lin = jnp.clip(up(D_FF), -SWIGLU_LIMIT, SWIGLU_LIMIT)
            a = g * jax.nn.sigmoid(SWIGLU_ALPHA * g) * (lin + 1.0)
            acc = acc + jnp.dot(a.astype(BF16), w2_ref[c * FF_CHUNK:(c + 1) * FF_CHUNK, :],
                                preferred_element_type=F32)
        y_ref[...] = _pack_rows(acc)


def _experts(tile_expert, n_valid, xs, w1, b1, w2, b2):
    n_rows = xs.shape[0]
    tm = EXPERT_TM
    n_tiles = n_rows // tm
    row = lambda i, te, nv: (jnp.minimum(i, nv[0] - 1), 0)
    grid_spec = pltpu.PrefetchScalarGridSpec(
        num_scalar_prefetch=2,
        grid=(n_tiles,),
        in_specs=[
            pl.BlockSpec((tm, HALF), row),
            pl.BlockSpec((None, D_MODEL, 2 * D_FF), lambda i, te, nv: (te[i], 0, 0)),
            pl.BlockSpec((None, 1, 2 * D_FF), lambda i, te, nv: (te[i], 0, 0)),
            pl.BlockSpec((None, D_FF, D_MODEL), lambda i, te, nv: (te[i], 0, 0)),
            pl.BlockSpec((None, 1, D_MODEL), lambda i, te, nv: (te[i], 0, 0)),
        ],
        out_specs=pl.BlockSpec((tm, HALF), row),
    )
    return pl.pallas_call(
        _expert_body,
        grid_spec=grid_spec,
        out_shape=jax.ShapeDtypeStruct((n_rows, HALF), I32),
        compiler_params=_cparams(("arbitrary",)),
        name="experts",
    )(tile_expert, n_valid, xs, w1, b1, w2, b2)


COMBINE_TM = 1024


def _combine_body(y0_ref, y1_ref, y2_ref, y3_ref, rw_ref, x_ref, g_ref, *rest):
    o_ref = rest[-1]
    acc = x_ref[...]
    rw = jnp.concatenate([rw_ref[...], jnp.zeros((LANES - 2 * TOP_K, COMBINE_TM), F32)], axis=0).T
    for k, y_ref in enumerate((y0_ref, y1_ref, y2_ref, y3_ref)):
        lo, hi = _unpack_rows(y_ref[...])
        acc = acc + rw[:, k:k + 1] * jnp.concatenate([lo, hi], axis=-1)
    o_ref[...] = _rms(acc, g_ref[...])


def _combine(yg, rw, x2, g, part, out_prev):
    T = x2.shape[0]
    tm = COMBINE_TM
    nt = T // tm
    in_specs = [
        pl.BlockSpec((tm, HALF), lambda i: (i, 0)),
        pl.BlockSpec((tm, HALF), lambda i: (nt + i, 0)),
        pl.BlockSpec((tm, HALF), lambda i: (2 * nt + i, 0)),
        pl.BlockSpec((tm, HALF), lambda i: (3 * nt + i, 0)),
        pl.BlockSpec((2 * TOP_K, tm), lambda i: (0, i)),
        pl.BlockSpec((tm, D_MODEL), lambda i: (i, 0)),
        pl.BlockSpec((1, D_MODEL), lambda i: (0, 0)),
    ]
    args = [yg, yg, yg, yg, rw, x2, g]
    aliases = {}
    if out_prev is not None:
        in_specs.append(pl.BlockSpec(memory_space=pl.ANY))
        args.append(out_prev)
        aliases = {len(args) - 1: 0}
    return pl.pallas_call(
        _combine_body,
        grid=(nt,),
        in_specs=in_specs,
        out_specs=pl.BlockSpec((tm, D_MODEL), lambda i: (part * nt + i, 0)),
        out_shape=jax.ShapeDtypeStruct((T * MOE_PARTS, D_MODEL), F32),
        input_output_aliases=aliases,
        compiler_params=_cparams(("parallel",)),
        name="combine",
    )(*args)


def _pad_lanes(v):
    v = v.reshape(1, -1).astype(F32)
    return jnp.pad(v, ((0, 0), (0, LANES - v.shape[1])))


def _layer(x2d, mem2d, B, S, M, norm_mix, w_in, b_ml_gates, conv_ml, ml_head_norm, b_fx_gate, norm_mem,
           w_mem_kv, w_branch, w_out, norm_moe, w_router, b_router, w_exp_in, b_exp_in, w_exp_out,
           b_exp_out, norm_out):
    T = B * S
    w16 = w_in.astype(BF16)
    w_big = jnp.concatenate([w16[:, 0:2048], w16[:, 2056:3080], w16[:, 3080:6152], w16[:, 6160:7184],
                             w16[:, 7184:10256]], axis=1)
    w_small = jnp.concatenate([w16[:, 2048:2056], w16[:, 6152:6160]], axis=1)
    w_small = jnp.pad(w_small, ((0, 0), (0, LANES - w_small.shape[1])))
    row = lambda v: v.reshape(1, -1).astype(F32)

    proj, small = _inproj(x2d, row(norm_mix), w_big, w_small)

    y_ml = _mlstm(proj, small, conv_ml.astype(F32), _pad_lanes(b_ml_gates), row(ml_head_norm), B, S)

    b_fx = jnp.pad(b_fx_gate.reshape(1, -1).astype(F32), ((0, 0), (2 * ML_HEADS, LANES - 2 * ML_HEADS - FX_HEADS)))
    y_fx = _fox_attn(proj, _fox_gate(small, b_fx, B, S), B, S)

    kv = _memkv(mem2d, row(norm_mem), w_mem_kv.astype(BF16))
    y_ca = _memattn(proj, kv, B, S, M)

    w_r = w_router.T.astype(BF16)
    moe_weights = (w_exp_in.astype(F32), b_exp_in.reshape(N_EXPERTS, 1, -1).astype(F32), w_exp_out.astype(F32),
                   b_exp_out.reshape(N_EXPERTS, 1, -1).astype(F32))
    staged = []
    for part in range(MOE_PARTS):
        x2, hp, ri, rw, cnt = _merge(y_ml, y_fx, y_ca, proj, x2d, w_branch.astype(BF16), w_out.astype(BF16),
                                     row(norm_moe), w_r, b_router.reshape(N_EXPERTS, 1).astype(F32), part)
        scatter_idx, dest, pad_idx, tile_e, n_valid, n_rows = _moe_plan(ri, cnt)
        staged.append((x2, rw, dest, tile_e, n_valid, _sc_dispatch(hp, scatter_idx, pad_idx, n_rows)))
    w1_b, w2_b = _cast_expert_weights(moe_weights[0], moe_weights[2], staged[0][0])
    moe_weights = (w1_b, moe_weights[1], w2_b, moe_weights[3])
    gathered = [_sc_gather(_experts(tile_e, n_valid, xs, *moe_weights), dest.reshape(-1))
                for _, _, dest, tile_e, n_valid, xs in staged]
    out = None
    for part, ((x2, rw, *_), yg) in enumerate(zip(staged, gathered)):
        out = _combine(yg, rw, x2, row(norm_out), part, out)
    return out


def _moe_plan(ri, cnt):
    T = ri.shape[1]
    tm = EXPERT_TM
    n_tiles = (T * TOP_K) // tm + N_EXPERTS
    counts = cnt[:, 0].astype(I32)
    padded = ((counts + tm - 1) // tm) * tm
    gend = jnp.cumsum(padded)
    gstart = gend - padded
    expert_ids = jnp.arange(N_EXPERTS, dtype=I32)
    start_of = jnp.sum(jnp.where(ri[0:TOP_K, :, None] == expert_ids, gstart, 0), axis=-1)
    dest = start_of + ri[TOP_K:2 * TOP_K, :]
    n_valid = gend[-1] // tm
    tile_ids = jnp.arange(n_tiles, dtype=I32)
    last_tile = jnp.minimum(tile_ids, n_valid - 1)
    tile_e = jnp.minimum(jnp.sum((gend[None, :] <= last_tile[:, None] * tm).astype(I32), axis=1), N_EXPERTS - 1)

    slot = jnp.arange(tm, dtype=I32)
    spare = n_tiles * tm + slot % SC_CHUNK
    pad_idx = jnp.where(slot[None, :] < (padded - counts)[:, None], (gstart + counts)[:, None] + slot[None, :],
                        spare[None, :]).reshape(-1)

    return (_scatter_indices(dest), dest, pad_idx, tile_e.astype(I32), n_valid.reshape(1).astype(I32),
            n_tiles * tm + SC_CHUNK)


def kernel(x, mem, norm_mix, w_in, b_ml_gates, conv_ml, ml_head_norm, b_fx_gate, norm_mem, w_mem_kv, w_branch,
           w_out, norm_moe, w_router, b_router, w_exp_in, b_exp_in, w_exp_out, b_exp_out, norm_final):
    B, S, D = x.shape
    M = mem.shape[1]
    depth = norm_mix.shape[0]
    assert depth == 1, "the combine kernel fuses the final norm, so exactly one layer is supported"
    assert D == D_MODEL and S % ML_BLOCK == 0 and S % FX_T == 0 and S % CA_TQ == 0
    out = _layer(x.reshape(B * S, D), mem.reshape(B * M, D), B, S, M, norm_mix[0], w_in[0], b_ml_gates[0],
                 conv_ml[0], ml_head_norm[0], b_fx_gate[0], norm_mem[0], w_mem_kv[0], w_branch[0], w_out[0],
                 norm_moe[0], w_router[0], b_router[0], w_exp_in[0], b_exp_in[0], w_exp_out[0], b_exp_out[0],
                 norm_final)
    return out.reshape(B, S, D)
```

```python
import functools

import jax
import jax.numpy as jnp
from jax import lax
from jax.experimental import pallas as pl
from jax.experimental.pallas import tpu as pltpu
from jax.experimental.pallas import tpu_sc as plsc

F32 = jnp.float32
BF16 = jnp.bfloat16
I32 = jnp.int32

D_MODEL = 1024
ML_HEADS = 4
ML_DQK = 128
ML_DV = 256
ML_CONV = 4
FX_HEADS = 8
FX_DH = 128
CA_HEADS = 4
CA_DH = 256
N_EXPERTS = 32
TOP_K = 4
D_FF = D_MODEL
SWIGLU_LIMIT = 7.0
SWIGLU_ALPHA = 1.702
EPS = 1e-5
LANES = 128
HALF = D_MODEL // 2
HI_MASK = -65536

COL_MLQK, COL_MLV, COL_MLO, COL_FXQ, COL_FXK, COL_FXV, COL_CAQ, COL_GATE0 = 0, 1, 2, 3, 4, 5, 6, 7
N_BIG = 10 * D_MODEL

VMEM_LIMIT = 56 * 1024 * 1024


def _cparams(sem):
    return pltpu.CompilerParams(dimension_semantics=sem, vmem_limit_bytes=VMEM_LIMIT)


def _rms(x, g):
    return x * lax.rsqrt(jnp.mean(x * x, axis=-1, keepdims=True) + EPS) * g


def _log_sigmoid(x):
    return jnp.minimum(x, 0.0) - jnp.log1p(jnp.exp(-jnp.abs(x)))


def _pack_rows(y):
    bits = lax.bitcast_convert_type(y.astype(BF16).astype(F32), I32)
    return lax.shift_right_logical(bits[:, :HALF], 16) | (bits[:, HALF:] & HI_MASK)


def _unpack_rows(w):
    lo = lax.bitcast_convert_type(lax.shift_left(w, 16), F32)
    hi = lax.bitcast_convert_type(w & HI_MASK, F32)
    return lo, hi


def _inproj_body(x_ref, g_ref, w_ref, ws_ref, o_ref, os_ref, h_ref):
    @pl.when(pl.program_id(1) == 0)
    def _():
        hb = _rms(x_ref[...], g_ref[...]).astype(BF16)
        h_ref[...] = hb
        os_ref[...] = jnp.dot(hb, ws_ref[...], preferred_element_type=F32)

    o_ref[...] = jnp.dot(h_ref[...], w_ref[...], preferred_element_type=F32).astype(BF16)


def _inproj(x2d, g, w_big, w_small):
    T = x2d.shape[0]
    tm = min(1024, T)
    tn = 2560
    return pl.pallas_call(
        _inproj_body,
        grid=(T // tm, N_BIG // tn),
        in_specs=[
            pl.BlockSpec((tm, D_MODEL), lambda i, j: (i, 0)),
            pl.BlockSpec((1, D_MODEL), lambda i, j: (0, 0)),
            pl.BlockSpec((D_MODEL, tn), lambda i, j: (0, j)),
            pl.BlockSpec((D_MODEL, LANES), lambda i, j: (0, 0)),
        ],
        out_specs=[
            pl.BlockSpec((tm, tn), lambda i, j: (i, j)),
            pl.BlockSpec((tm, LANES), lambda i, j: (i, 0)),
        ],
        out_shape=[
            jax.ShapeDtypeStruct((T, N_BIG), BF16),
            jax.ShapeDtypeStruct((T, LANES), F32),
        ],
        scratch_shapes=[pltpu.VMEM((tm, D_MODEL), BF16)],
        compiler_params=_cparams(("parallel", "arbitrary")),
        name="inproj",
    )(x2d, g, w_big, w_small)


ML_BLOCK = 1024
ML_MB = 1
ML_CHUNK = 128
CONV_PAD = 8


def _mlstm_body(qk_ref, v_ref, o_ref, g_ref, cw_ref, bg_ref, hn_ref, y_ref, xbuf, c_st, n_st, m_st):
    L = ML_CHUNK

    @pl.when(pl.program_id(1) == 0)
    def _():
        xbuf[:, 0:CONV_PAD, :] = jnp.zeros((ML_MB, CONV_PAD, D_MODEL), F32)
        c_st[...] = jnp.zeros_like(c_st)
        n_st[...] = jnp.zeros_like(n_st)
        m_st[...] = jnp.zeros_like(m_st)

    for bb in range(ML_MB):
        xbuf[bb, CONV_PAD:CONV_PAD + ML_BLOCK, :] = qk_ref[bb].astype(F32)
    cw = cw_ref[...]
    row = lax.broadcasted_iota(I32, (L, L), 0)
    col = lax.broadcasted_iota(I32, (L, L), 1)
    tri = (row >= col).astype(BF16)
    causal_t = col >= row
    bg = bg_ref[...]
    scale = ML_DQK ** -0.5
    nt_dims = (((1,), (1,)), ((), ()))

    def chunk(bb, c):
        r0 = c * L
        conv = cw[0:1, :] * xbuf[bb, r0 + CONV_PAD - 3:r0 + CONV_PAD - 3 + L, :]
        for j in range(1, ML_CONV):
            s0 = r0 + CONV_PAD - 3 + j
            conv = conv + cw[j:j + 1, :] * xbuf[bb, s0:s0 + L, :]
        act = conv * jax.nn.sigmoid(conv)

        gates = g_ref[bb, r0:r0 + L, :] + bg
        lf = _log_sigmoid(gates)
        cum = jnp.zeros((L, LANES), F32)
        for _ in range(3):
            piece = lf.astype(BF16)
            cum = cum + jnp.dot(tri, piece, preferred_element_type=F32)
            lf = lf - piece.astype(F32)
        gates_t = gates.T
        cum_t = cum.T
        for h in range(ML_HEADS):
            b_row = cum_t[ML_HEADS + h:ML_HEADS + h + 1, :]
            i_row = gates_t[h:h + 1, :]
            a_col = gates[:, h:h + 1] - cum[:, ML_HEADS + h:ML_HEADS + h + 1]
            st = bb * ML_HEADS + h
            m_prev = m_st[st]
            dm = jnp.where(causal_t, a_col + b_row, -jnp.inf)
            m_inter = b_row + m_prev
            m_t = jnp.maximum(jnp.max(dm, axis=0, keepdims=True), m_inter)
            w_intra = jnp.exp(dm - m_t)
            w_inter = jnp.exp(m_inter - m_t)

            qb = (act[:, h * ML_DQK:(h + 1) * ML_DQK] * scale).astype(BF16)
            kb = act[:, (ML_HEADS + h) * ML_DQK:(ML_HEADS + h + 1) * ML_DQK].astype(BF16)
            v_t = v_ref[bb, r0:r0 + L, h * ML_DV:(h + 1) * ML_DV].astype(F32).T
            p_t = lax.dot_general(kb, qb, nt_dims, preferred_element_type=F32) * w_intra
            c_old = c_st[st]
            n_old = n_st[st]
            num = jnp.dot(v_t.astype(BF16), p_t.astype(BF16), preferred_element_type=F32) + w_inter * (
                lax.dot_general(c_old.astype(BF16), qb, nt_dims, preferred_element_type=F32))
            qn = lax.dot_general(jnp.broadcast_to(n_old, (8, ML_DQK)).astype(BF16), qb, nt_dims,
                                 preferred_element_type=F32)[0:1, :]
            den = jnp.sum(p_t, axis=0, keepdims=True) + w_inter * qn
            hv = num / jnp.maximum(jnp.abs(den), jnp.exp(-m_t))

            m_new = m_t[:, L - 1:L]
            b_last = b_row[:, L - 1:L]
            wk = jnp.exp(b_last - b_row + i_row - m_new)
            decay = jnp.exp(b_last + m_prev - m_new)
            c_st[st] = decay * c_old + jnp.dot((v_t * wk).astype(BF16), kb, preferred_element_type=F32)
            n_st[st] = decay * n_old + jnp.dot(jnp.broadcast_to(wk, (8, L)).astype(BF16), kb,
                                               preferred_element_type=F32)[0:1, :]
            m_st[st] = m_new

            hn = (hv * lax.rsqrt(jnp.mean(hv * hv, axis=0, keepdims=True) + EPS)).T
            og = o_ref[bb, r0:r0 + L, h * ML_DV:(h + 1) * ML_DV].astype(F32)
            y_ref[bb, r0:r0 + L, h * ML_DV:(h + 1) * ML_DV] = (
                hn * hn_ref[:, h * ML_DV:(h + 1) * ML_DV] * jax.nn.sigmoid(og)).astype(BF16)

    for c in range(ML_BLOCK // L):
        for bb in range(ML_MB):
            chunk(bb, c)

    xbuf[:, 0:CONV_PAD, :] = xbuf[:, ML_BLOCK:ML_BLOCK + CONV_PAD, :]


def _mlstm(proj, small, conv_w, b_gates, head_norm, B, S, b0, nb):
    ns = S // ML_BLOCK
    assert nb % ML_MB == 0 and b0 % ML_MB == 0
    proj3 = proj.reshape(B, S, N_BIG)
    blk = lambda col: pl.BlockSpec((ML_MB, ML_BLOCK, D_MODEL), lambda b, s: (b0 // ML_MB + b, s, col))
    out = pl.pallas_call(
        _mlstm_body,
        grid=(nb // ML_MB, ns),
        in_specs=[
            blk(COL_MLQK),
            blk(COL_MLV),
            blk(COL_MLO),
            pl.BlockSpec((ML_MB, ML_BLOCK, LANES), lambda b, s: (b0 // ML_MB + b, s, 0)),
            pl.BlockSpec((ML_CONV, D_MODEL), lambda b, s: (0, 0)),
            pl.BlockSpec((1, LANES), lambda b, s: (0, 0)),
            pl.BlockSpec((1, D_MODEL), lambda b, s: (0, 0)),
        ],
        out_specs=pl.BlockSpec((ML_MB, ML_BLOCK, D_MODEL), lambda b, s: (b, s, 0)),
        out_shape=jax.ShapeDtypeStruct((nb, S, D_MODEL), BF16),
        scratch_shapes=[
            pltpu.VMEM((ML_MB, ML_BLOCK + CONV_PAD, D_MODEL), F32),
            pltpu.VMEM((ML_MB * ML_HEADS, ML_DV, ML_DQK), F32),
            pltpu.VMEM((ML_MB * ML_HEADS, 1, ML_DQK), F32),
            pltpu.VMEM((ML_MB * ML_HEADS, 1, 1), F32),
        ],
        compiler_params=_cparams(("parallel", "arbitrary")),
        name="mlstm",
    )(proj3, proj3, proj3, small.reshape(B, S, LANES), conv_w, b_gates, head_norm)
    return out.reshape(nb * S, D_MODEL)


FX_T = 512
FX_HP = 2
FX_VR = FX_DH + 16
LOG2E = 1.4426950408889634
N_PIECES = 3
FX_GATE_T = 128


def _fox_gate_body(g_ref, b_ref, o_ref):
    S = g_ref.shape[0]
    row = lax.broadcasted_iota(I32, (FX_GATE_T, FX_GATE_T), 0)
    col = lax.broadcasted_iota(I32, (FX_GATE_T, FX_GATE_T), 1)
    tri = (row >= col).astype(BF16)
    carry = jnp.zeros((1, LANES), F32)
    for blk in range(S // FX_GATE_T):
        rows = slice(blk * FX_GATE_T, (blk + 1) * FX_GATE_T)
        lf = _log_sigmoid(g_ref[rows, :] + b_ref[...])
        cum = carry
        for _ in range(N_PIECES):
            piece = lf.astype(BF16)
            cum = cum + jnp.dot(tri, piece, preferred_element_type=F32)
            lf = lf - piece.astype(F32)
        carry = cum[FX_GATE_T - 1:FX_GATE_T, :]
        o_ref[rows, :] = cum * (-LOG2E)


def _fox_gate(small, b_fx, S, b0, nb):
    return pl.pallas_call(
        _fox_gate_body,
        grid=(nb,),
        in_specs=[
            pl.BlockSpec((S, LANES), lambda b: (b0 + b, 0)),
            pl.BlockSpec((1, LANES), lambda b: (0, 0)),
        ],
        out_specs=pl.BlockSpec((S, LANES), lambda b: (b, 0)),
        out_shape=jax.ShapeDtypeStruct((nb * S, LANES), F32),
        compiler_params=_cparams(("parallel",)),
        name="fox_gate",
    )(small, b_fx)


def _fox_attn_body(q_ref, k_ref, v_ref, c_ref, o_ref, kx_ref, vt_ref, m_ref, acc_ref, s_ref):
    S = k_ref.shape[0]
    nq = S // FX_T

    c = c_ref[...]
    hi = c.astype(BF16)
    r1 = c - hi.astype(F32)
    mid = r1.astype(BF16)
    lo = (r1 - mid.astype(F32)).astype(BF16)
    sel_row = lax.broadcasted_iota(I32, (LANES, LANES), 0)
    sel_col = lax.broadcasted_iota(I32, (LANES, LANES), 1)
    ones_rows = (lax.broadcasted_iota(I32, (FX_VR - FX_DH, FX_T), 0) == 0).astype(BF16)
    head_slices = [slice(hh * FX_DH, (hh + 1) * FX_DH) for hh in range(FX_HP)]
    for hh, sl in enumerate(head_slices):
        lane = 2 * ML_HEADS + pl.program_id(1) * FX_HP + hh
        pieces = None
        for p, part in enumerate((hi, mid, lo)):
            pick = jnp.logical_and(sel_row == lane, sel_col == p).astype(BF16)
            t = jnp.dot(part, pick, preferred_element_type=F32)
            pieces = t if pieces is None else pieces + t
        kx_ref[hh, :, 0:FX_DH] = k_ref[:, sl]
        kx_ref[hh, :, FX_DH:2 * FX_DH] = pieces.astype(BF16)
        for j in range(nq):
            vt = v_ref[j * FX_T:(j + 1) * FX_T, sl].astype(F32).T.astype(BF16)
            vt_ref[hh, j] = jnp.concatenate([vt, ones_rows], axis=0)

    piece_rows = (lax.broadcasted_iota(I32, (FX_DH, FX_T), 0) < N_PIECES).astype(BF16)

    def start(i):
        q_x = []
        for sl in head_slices:
            q_t = (q_ref[i * FX_T:(i + 1) * FX_T, sl].astype(F32) * (FX_DH ** -0.5 * LOG2E)).T.astype(BF16)
            q_x.append(jnp.concatenate([q_t, piece_rows], axis=0))
        m_ref[i % 2] = jnp.full(m_ref.shape[1:], -jnp.inf, F32)
        acc_ref[i % 2] = jnp.zeros(acc_ref.shape[1:], F32)
        return q_x

    def key_rows(j):
        return pl.ds(j * FX_T, FX_T) if isinstance(j, int) else pl.ds(pl.multiple_of(j * FX_T, FX_T), FX_T)

    def scores(q_x, j, slot):
        for hh in range(FX_HP):
            s_ref[slot, hh] = jnp.dot(kx_ref[hh, key_rows(j), :], q_x[hh], preferred_element_type=F32)

    def consume(par, j, slot, masked):
        for hh in range(FX_HP):
            s = s_ref[slot, hh]
            if masked:
                key = lax.broadcasted_iota(I32, (FX_T, FX_T), 0)
                qry = lax.broadcasted_iota(I32, (FX_T, FX_T), 1)
                s = jnp.where(qry >= key, s, -jnp.inf)
            m_old = m_ref[par, hh]
            m_new = jnp.maximum(m_old, jnp.max(s, axis=0, keepdims=True))
            p = jnp.exp2(s - m_new).astype(BF16)
            acc_ref[par, hh] = jnp.exp2(m_old - m_new) * acc_ref[par, hh] + jnp.dot(
                vt_ref[hh, j], p, preferred_element_type=F32)
            m_ref[par, hh] = m_new

    def finish(i, slot):
        consume(i % 2, i, slot, True)
        for hh, sl in enumerate(head_slices):
            acc = acc_ref[i % 2, hh]
            o_ref[i * FX_T:(i + 1) * FX_T, sl] = (acc[0:FX_DH, :] / acc[FX_DH:FX_DH + 1, :]).T.astype(BF16)

    diag_slot = 0
    for i in range(nq):
        q_x = start(i)
        first = 0 if i == 0 else 1 - diag_slot
        scores(q_x, 0, first)
        if i > 0:
            finish(i - 1, diag_slot)

        def pair(jj, carry, q_x=q_x, first=first, par=i % 2):
            j = 2 * jj
            scores(q_x, j + 1, 1 - first)
            consume(par, j, first, False)
            scores(q_x, j + 2, first)
            consume(par, j + 1, 1 - first, False)
            return carry

        if i >= 2:
            lax.fori_loop(0, i // 2, pair, 0)
        if i % 2 == 1:
            scores(q_x, i, 1 - first)
            consume(i % 2, i - 1, first, False)
            diag_slot = 1 - first
        else:
            diag_slot = first
    finish(nq - 1, diag_slot)


def _fox_attn(proj, c_neg, B, S, b0, nb):
    nq = S // FX_T
    wide = FX_HP * FX_DH
    cq = COL_FXQ * (D_MODEL // wide)
    ck = COL_FXK * (D_MODEL // wide)
    cv = COL_FXV * (D_MODEL // wide)
    proj3 = proj.reshape(B, S, N_BIG)
    out = pl.pallas_call(
        _fox_attn_body,
        grid=(nb, FX_HEADS // FX_HP),
        in_specs=[
            pl.BlockSpec((None, S, wide), lambda b, h: (b0 + b, 0, cq + h)),
            pl.BlockSpec((None, S, wide), lambda b, h: (b0 + b, 0, ck + h)),
            pl.BlockSpec((None, S, wide), lambda b, h: (b0 + b, 0, cv + h)),
            pl.BlockSpec((None, S, LANES), lambda b, h: (b, 0, 0)),
        ],
        out_specs=pl.BlockSpec((None, S, wide), lambda b, h: (b, 0, h)),
        out_shape=jax.ShapeDtypeStruct((nb, S, D_MODEL), BF16),
        scratch_shapes=[
            pltpu.VMEM((FX_HP, S, 2 * FX_DH), BF16),
            pltpu.VMEM((FX_HP, nq, FX_VR, FX_T), BF16),
            pltpu.VMEM((2, FX_HP, 1, FX_T), F32),
            pltpu.VMEM((2, FX_HP, FX_VR, FX_T), F32),
            pltpu.VMEM((2, FX_HP, FX_T, FX_T), F32),
        ],
        compiler_params=_cparams(("parallel", "parallel")),
        name="fox_attn",
    )(proj3, proj3, proj3, c_neg.reshape(nb, S, LANES))
    return out.reshape(nb * S, D_MODEL)


def _memkv_body(x_ref, g_ref, w_ref, o_ref):
    hb = _rms(x_ref[...], g_ref[...]).astype(BF16)
    o_ref[...] = jnp.dot(hb, w_ref[...], preferred_element_type=F32).astype(BF16)


def _memkv(mem2d, g, w_kv):
    R = mem2d.shape[0]
    tm = min(512, R)
    N = w_kv.shape[1]
    return pl.pallas_call(
        _memkv_body,
        grid=(R // tm,),
        in_specs=[
            pl.BlockSpec((tm, D_MODEL), lambda i: (i, 0)),
            pl.BlockSpec((1, D_MODEL), lambda i: (0, 0)),
            pl.BlockSpec((D_MODEL, N), lambda i: (0, 0)),
        ],
        out_specs=pl.BlockSpec((tm, N), lambda i: (i, 0)),
        out_shape=jax.ShapeDtypeStruct((R, N), BF16),
        compiler_params=_cparams(("parallel",)),
        name="memkv",
    )(mem2d, g, w_kv)


CA_TQ = 2048


def _memattn_body(q_ref, k_ref, v_ref, o_ref):
    scale = CA_DH ** -0.5
    for h in range(CA_HEADS):
        sl = slice(h * CA_DH, (h + 1) * CA_DH)
        s = lax.dot_general(q_ref[:, sl], k_ref[:, sl], (((1,), (1,)), ((), ())),
                            preferred_element_type=F32) * scale
        p = jnp.exp(s - jnp.max(s, axis=-1, keepdims=True))
        l = jnp.sum(p, axis=-1, keepdims=True)
        o = jnp.dot(p.astype(BF16), v_ref[:, sl], preferred_element_type=F32) / l
        o_ref[:, sl] = o.astype(BF16)


def _memattn(proj, kv, B, S, M, b0, nb):
    nq = S // CA_TQ
    kv3 = kv.reshape(B, M, 2 * D_MODEL)
    return pl.pallas_call(
        _memattn_body,
        grid=(nb, nq),
        in_specs=[
            pl.BlockSpec((CA_TQ, D_MODEL), lambda b, i: ((b0 + b) * nq + i, COL_CAQ)),
            pl.BlockSpec((None, M, D_MODEL), lambda b, i: (b0 + b, 0, 0)),
            pl.BlockSpec((None, M, D_MODEL), lambda b, i: (b0 + b, 0, 1)),
        ],
        out_specs=pl.BlockSpec((CA_TQ, D_MODEL), lambda b, i: (b * nq + i, 0)),
        out_shape=jax.ShapeDtypeStruct((nb * S, D_MODEL), BF16),
        compiler_params=_cparams(("parallel", "arbitrary")),
        name="memattn",
    )(proj, kv3, kv3)


MERGE_TM = 512
MOE_PARTS = 2


def _merge_body(y0_ref, y1_ref, y2_ref, g0_ref, g1_ref, g2_ref, x_ref, wb_ref, wo_ref, gn_ref, wr_ref, br_ref,
                o_ref, hp_ref, ri_ref, rw_ref, cnt_ref, carry_ref):
    merged = None
    for n, (y_ref, g_ref) in enumerate(((y0_ref, g0_ref), (y1_ref, g1_ref), (y2_ref, g2_ref))):
        p = jnp.dot(y_ref[...], wb_ref[n], preferred_element_type=F32)
        t = jax.nn.sigmoid(g_ref[...].astype(F32)) * p
        merged = t if merged is None else merged + t
    x2 = x_ref[...] + jnp.dot(merged.astype(BF16), wo_ref[...], preferred_element_type=F32)
    o_ref[...] = x2
    _route(x2, gn_ref, wr_ref, br_ref, hp_ref, ri_ref, rw_ref, cnt_ref, carry_ref)


def _merge(y_ml, y_fx, y_ca, proj, x2d, w_branch, w_out, g_moe, w_router, b_router, part):
    T = x2d.shape[0] // MOE_PARTS
    tm = MERGE_TM
    off = part * (T // tm)
    src = lambda i: (off + i, 0)
    row = lambda i: (i, 0)
    const = lambda i: (0, 0)
    return pl.pallas_call(
        _merge_body,
        grid=(T // tm,),
        in_specs=[
            pl.BlockSpec((tm, D_MODEL), row),
            pl.BlockSpec((tm, D_MODEL), row),
            pl.BlockSpec((tm, D_MODEL), row),
            pl.BlockSpec((tm, D_MODEL), lambda i: (off + i, COL_GATE0)),
            pl.BlockSpec((tm, D_MODEL), lambda i: (off + i, COL_GATE0 + 1)),
            pl.BlockSpec((tm, D_MODEL), lambda i: (off + i, COL_GATE0 + 2)),
            pl.BlockSpec((tm, D_MODEL), src),
            pl.BlockSpec((3, D_MODEL, D_MODEL), lambda i: (0, 0, 0)),
            pl.BlockSpec((D_MODEL, D_MODEL), const),
            pl.BlockSpec((1, D_MODEL), const),
            pl.BlockSpec((N_EXPERTS, D_MODEL), const),
            pl.BlockSpec((N_EXPERTS, 1), const),
        ],
        out_specs=[
            pl.BlockSpec((tm, D_MODEL), row),
            pl.BlockSpec((tm, HALF), row),
            pl.BlockSpec((2 * TOP_K, tm), lambda i: (0, i)),
            pl.BlockSpec((2 * TOP_K, tm), lambda i: (0, i)),
            pl.BlockSpec((N_EXPERTS, 1), const),
        ],
        out_shape=[
            jax.ShapeDtypeStruct((T, D_MODEL), F32),
            jax.ShapeDtypeStruct((T, HALF), I32),
            jax.ShapeDtypeStruct((2 * TOP_K, T), I32),
            jax.ShapeDtypeStruct((2 * TOP_K, T), F32),
            jax.ShapeDtypeStruct((N_EXPERTS, 1), F32),
        ],
        scratch_shapes=[pltpu.VMEM((N_EXPERTS, 1), F32)],
        compiler_params=_cparams(("arbitrary",)),
        name="merge_router",
    )(y_ml, y_fx, y_ca, proj, proj, proj, x2d, w_branch, w_out, g_moe, w_router, b_router)


def _route(x2, g_ref, wr_ref, br_ref, hp_ref, ri_ref, rw_ref, cnt_ref, carry_ref):
    tm = MERGE_TM

    @pl.when(pl.program_id(0) == 0)
    def _():
        carry_ref[...] = jnp.zeros_like(carry_ref)

    h = _rms(x2, g_ref[...])
    hp_ref[...] = _pack_rows(h)
    logits = lax.dot_general(wr_ref[...], h.astype(BF16), (((1,), (1,)), ((), ())),
                             preferred_element_type=F32) + br_ref[...]
    eid = lax.broadcasted_iota(I32, (N_EXPERTS, tm), 0).astype(F32)

    work = logits
    onehot_sum = jnp.zeros((N_EXPERTS, tm), F32)
    vals, sels, idxs = [], [], []
    for _ in range(TOP_K):
        mx = jnp.max(work, axis=0, keepdims=True)
        idx = jnp.min(jnp.where(work == mx, eid, float(N_EXPERTS)), axis=0, keepdims=True)
        sel = eid == idx
        onehot_sum = onehot_sum + sel.astype(F32)
        work = jnp.where(sel, -jnp.inf, work)
        vals.append(mx)
        sels.append(sel)
        idxs.append(idx)
    exps = [jnp.exp(v - vals[0]) for v in vals]
    total = exps[0] + exps[1] + exps[2] + exps[3]

    earlier = (lax.broadcasted_iota(I32, (tm, tm), 0) < lax.broadcasted_iota(I32, (tm, tm), 1)).astype(BF16)
    before = jnp.dot(onehot_sum.astype(BF16), earlier, preferred_element_type=F32) + carry_ref[...]
    carry_ref[...] = carry_ref[...] + jnp.sum(onehot_sum, axis=1, keepdims=True)
    cnt_ref[...] = carry_ref[...]

    out_row = lax.broadcasted_iota(I32, (2 * TOP_K, tm), 0)
    ri = jnp.zeros((2 * TOP_K, tm), I32)
    rw = jnp.zeros((2 * TOP_K, tm), F32)
    for k in range(TOP_K):
        rank = jnp.sum(jnp.where(sels[k], before, 0.0), axis=0, keepdims=True)
        ri = jnp.where(out_row == k, idxs[k].astype(I32), ri)
        ri = jnp.where(out_row == TOP_K + k, rank.astype(I32), ri)
        rw = jnp.where(out_row == k, exps[k] / total, rw)
    ri_ref[...] = ri
    rw_ref[...] = rw


EXPERT_TM = 512
SC_CORES = 2
SC_SUBCORES = 16
SC_WORKERS = SC_CORES * SC_SUBCORES
SC_CHUNK = 64
PAD_SLOTS = N_EXPERTS * EXPERT_TM


def _sc_mesh():
    return plsc.VectorSubcoreMesh(core_axis_name="c", subcore_axis_name="s")


def _sc_worker():
    return lax.axis_index("s") * SC_CORES + lax.axis_index("c")


def _scatter_indices(dest):
    T = dest.shape[1]
    n_ch = T // (SC_WORKERS * SC_CHUNK)
    idx = dest.reshape(TOP_K, SC_WORKERS, n_ch, SC_CHUNK).transpose(1, 2, 0, 3)
    return idx.reshape(SC_WORKERS, n_ch * TOP_K, SC_CHUNK)


def _sc_dispatch(hp, idx, pad_idx, n_rows):
    T = hp.shape[0]
    per_w = T // SC_WORKERS
    n_ch = per_w // SC_CHUNK
    n_pc = PAD_SLOTS // (SC_WORKERS * SC_CHUNK)
    assert per_w % SC_CHUNK == 0 and n_ch >= 2 and n_ch % 2 == 0
    pidx = pad_idx.reshape(SC_WORKERS, n_pc, SC_CHUNK)
    zeros = jnp.zeros((SC_CHUNK, HALF), I32)

    @functools.partial(
        pl.kernel, mesh=_sc_mesh(),
        out_type=jax.ShapeDtypeStruct((n_rows, HALF), I32),
        scratch_types=[
            pltpu.VMEM((n_ch * TOP_K, SC_CHUNK), I32),
            pltpu.VMEM((n_pc, SC_CHUNK), I32),
            pltpu.VMEM((2, SC_CHUNK, HALF), I32),
            pltpu.SemaphoreType.DMA((2,)),
            pltpu.SemaphoreType.DMA((2,)),
        ],
        name="sc_dispatch",
    )
    def k(hp_hbm, idx_hbm, pidx_hbm, zeros_hbm, xs_hbm, idx_v, pidx_v, rows_v, lsem, ssem):
        wid = _sc_worker()
        base = wid * per_w
        pltpu.sync_copy(idx_hbm.at[wid], idx_v)
        pltpu.sync_copy(pidx_hbm.at[wid], pidx_v)

        pltpu.sync_copy(zeros_hbm, rows_v.at[0])
        for p in range(n_pc):
            pltpu.make_async_copy(rows_v.at[0], xs_hbm.at[pidx_v.at[p]], ssem.at[0]).start()
        for p in range(n_pc):
            pltpu.make_async_copy(rows_v.at[0], xs_hbm.at[pidx_v.at[p]], ssem.at[0]).wait()

        def load(i, slot):
            return pltpu.make_async_copy(hp_hbm.at[pl.ds(base + i * SC_CHUNK, SC_CHUNK)], rows_v.at[slot],
                                         lsem.at[slot])

        def scatter(i, kk, slot):
            return pltpu.make_async_copy(rows_v.at[slot], xs_hbm.at[idx_v.at[i * TOP_K + kk]], ssem.at[slot])

        load(0, 0).start()

        def body(i2, carry):
            for slot in range(2):
                i = i2 * 2 + slot
                nxt = 1 - slot

                @pl.when(i + 1 < n_ch)
                def _():
                    @pl.when(i >= 1)
                    def _():
                        for kk in range(TOP_K):
                            scatter(i - 1, kk, nxt).wait()
                    load(i + 1, nxt).start()

                load(i, slot).wait()
                for kk in range(TOP_K):
                    scatter(i, kk, slot).start()
            return carry

        lax.fori_loop(0, n_ch // 2, body, 0)
        for kk in range(TOP_K):
            scatter(n_ch - 2, kk, 0).wait()
            scatter(n_ch - 1, kk, 1).wait()

    return k(hp, idx, pidx, zeros)


def _sc_gather(table, idx):
    n = idx.shape[0]
    per_w = n // SC_WORKERS
    n_ch = per_w // SC_CHUNK
    assert per_w % SC_CHUNK == 0 and n_ch >= 2 and n_ch % 2 == 0

    @functools.partial(
        pl.kernel, mesh=_sc_mesh(),
        out_type=jax.ShapeDtypeStruct((n, HALF), I32),
        scratch_types=[
            pltpu.VMEM((n_ch, SC_CHUNK), I32),
            pltpu.VMEM((2, SC_CHUNK, HALF), I32),
            pltpu.SemaphoreType.DMA((2,)),
            pltpu.SemaphoreType.DMA((2,)),
        ],
        name="sc_gather",
    )
    def k(table_hbm, idx_hbm, out_hbm, idx_v, rows_v, gsem, wsem):
        wid = _sc_worker()
        base = wid * per_w
        pltpu.sync_copy(idx_hbm.at[wid], idx_v)

        def gather(i, slot):
            return pltpu.make_async_copy(table_hbm.at[idx_v.at[i]], rows_v.at[slot], gsem.at[slot])

        def writeback(i, slot):
            return pltpu.make_async_copy(rows_v.at[slot], out_hbm.at[pl.ds(base + i * SC_CHUNK, SC_CHUNK)],
                                         wsem.at[slot])

        gather(0, 0).start()

        def body(i2, carry):
            for slot in range(2):
                i = i2 * 2 + slot
                nxt = 1 - slot

                @pl.when(i + 1 < n_ch)
                def _():
                    @pl.when(i >= 1)
                    def _():
                        writeback(i - 1, nxt).wait()
                    gather(i + 1, nxt).start()

                gather(i, slot).wait()
                writeback(i, slot).start()
            return carry

        lax.fori_loop(0, n_ch // 2, body, 0)
        writeback(n_ch - 2, 0).wait()
        writeback(n_ch - 1, 1).wait()

    return k(table, idx.reshape(SC_WORKERS, n_ch, SC_CHUNK))


FF_CHUNK = 1024


def _expert_body(te_ref, nv_ref, x_ref, w1f_ref, b1_ref, w2f_ref, b2_ref, y_ref, w1_ref, w2_ref):
    i = pl.program_id(0)

    @pl.when(jnp.logical_or(i == 0, te_ref[i] != te_ref[jnp.maximum(i - 1, 0)]))
    def _():
        w1_ref[...] = w1f_ref[...].astype(BF16)
        w2_ref[...] = w2f_ref[...].astype(BF16)

    @pl.when(i < nv_ref[0])
    def _():
        lo, hi = _unpack_rows(x_ref[...])
        xb = jnp.concatenate([lo.astype(BF16), hi.astype(BF16)], axis=-1)
        acc = jnp.zeros((EXPERT_TM, D_MODEL), F32) + b2_ref[...]
        for c in range(D_FF // FF_CHUNK):
            def up(off):
                cs = slice(off + c * FF_CHUNK, off + (c + 1) * FF_CHUNK)
                return jnp.dot(xb, w1_ref[:, cs], preferred_element_type=F32) + b1_ref[:, cs]
            g = jnp.minimum(up(0), SWIGLU_LIMIT)
            lin = jnp.clip(up(D_FF), -SWIGLU_LIMIT, SWIGLU_LIMIT)
            a = g * jax.nn.sigmoid(SWIGLU_ALPHA * g) * (lin + 1.0)
            acc = acc + jnp.dot(a.astype(BF16), w2_ref[c * FF_CHUNK:(c + 1) * FF_CHUNK, :],
                                preferred_element_type=F32)
        y_ref[...] = _pack_rows(acc)


def _experts(tile_expert, n_valid, xs, w1, b1, w2, b2):
    n_rows = xs.shape[0]
    tm = EXPERT_TM
    n_tiles = n_rows // tm
    row = lambda i, te, nv: (jnp.minimum(i, nv[0] - 1), 0)
    grid_spec = pltpu.PrefetchScalarGridSpec(
        num_scalar_prefetch=2,
        grid=(n_tiles,),
        in_specs=[
            pl.BlockSpec((tm, HALF), row),
            pl.BlockSpec((None, D_MODEL, 2 * D_FF), lambda i, te, nv: (te[i], 0, 0)),
            pl.BlockSpec((None, 1, 2 * D_FF), lambda i, te, nv: (te[i], 0, 0)),
            pl.BlockSpec((None, D_FF, D_MODEL), lambda i, te, nv: (te[i], 0, 0)),
            pl.BlockSpec((None, 1, D_MODEL), lambda i, te, nv: (te[i], 0, 0)),
        ],
        out_specs=pl.BlockSpec((tm, HALF), row),
        scratch_shapes=[pltpu.VMEM((D_MODEL, 2 * D_FF), BF16), pltpu.VMEM((D_FF, D_MODEL), BF16)],
    )
    return pl.pallas_call(
        _expert_body,
        grid_spec=grid_spec,
        out_shape=jax.ShapeDtypeStruct((n_rows, HALF), I32),
        compiler_params=_cparams(("arbitrary",)),
        name="experts",
    )(tile_expert, n_valid, xs, w1, b1, w2, b2)


COMBINE_TM = 1024


def _combine_body(y0_ref, y1_ref, y2_ref, y3_ref, rw_ref, x_ref, g_ref, *rest):
    o_ref = rest[-1]
    acc = x_ref[...]
    rw = jnp.concatenate([rw_ref[...], jnp.zeros((LANES - 2 * TOP_K, COMBINE_TM), F32)], axis=0).T
    for k, y_ref in enumerate((y0_ref, y1_ref, y2_ref, y3_ref)):
        lo, hi = _unpack_rows(y_ref[...])
        acc = acc + rw[:, k:k + 1] * jnp.concatenate([lo, hi], axis=-1)
    o_ref[...] = _rms(acc, g_ref[...])


def _combine(yg, rw, x2, g, part, out_prev):
    T = x2.shape[0]
    tm = COMBINE_TM
    nt = T // tm
    in_specs = [
        pl.BlockSpec((tm, HALF), lambda i: (i, 0)),
        pl.BlockSpec((tm, HALF), lambda i: (nt + i, 0)),
        pl.BlockSpec((tm, HALF), lambda i: (2 * nt + i, 0)),
        pl.BlockSpec((tm, HALF), lambda i: (3 * nt + i, 0)),
        pl.BlockSpec((2 * TOP_K, tm), lambda i: (0, i)),
        pl.BlockSpec((tm, D_MODEL), lambda i: (i, 0)),
        pl.BlockSpec((1, D_MODEL), lambda i: (0, 0)),
    ]
    args = [yg, yg, yg, yg, rw, x2, g]
    aliases = {}
    if out_prev is not None:
        in_specs.append(pl.BlockSpec(memory_space=pl.ANY))
        args.append(out_prev)
        aliases = {len(args) - 1: 0}
    return pl.pallas_call(
        _combine_body,
        grid=(nt,),
        in_specs=in_specs,
        out_specs=pl.BlockSpec((tm, D_MODEL), lambda i: (part * nt + i, 0)),
        out_shape=jax.ShapeDtypeStruct((T * MOE_PARTS, D_MODEL), F32),
        input_output_aliases=aliases,
        compiler_params=_cparams(("parallel",)),
        name="combine",
    )(*args)


def _pad_lanes(v):
    v = v.reshape(1, -1).astype(F32)
    return jnp.pad(v, ((0, 0), (0, LANES - v.shape[1])))


def _layer(x2d, mem2d, B, S, M, norm_mix, w_in, b_ml_gates, conv_ml, ml_head_norm, b_fx_gate, norm_mem,
           w_mem_kv, w_branch, w_out, norm_moe, w_router, b_router, w_exp_in, b_exp_in, w_exp_out,
           b_exp_out, norm_out):
    T = B * S
    w16 = w_in.astype(BF16)
    w_big = jnp.concatenate([w16[:, 0:2048], w16[:, 2056:3080], w16[:, 3080:6152], w16[:, 6160:7184],
                             w16[:, 7184:10256]], axis=1)
    w_small = jnp.concatenate([w16[:, 2048:2056], w16[:, 6152:6160]], axis=1)
    w_small = jnp.pad(w_small, ((0, 0), (0, LANES - w_small.shape[1])))
    row = lambda v: v.reshape(1, -1).astype(F32)

    proj, small = _inproj(x2d, row(norm_mix), w_big, w_small)

    b_fx = jnp.pad(b_fx_gate.reshape(1, -1).astype(F32), ((0, 0), (2 * ML_HEADS, LANES - 2 * ML_HEADS - FX_HEADS)))
    kv = _memkv(mem2d, row(norm_mem), w_mem_kv.astype(BF16))

    def mixers(b0, nb):
        y_ml = _mlstm(proj, small, conv_ml.astype(F32), _pad_lanes(b_ml_gates), row(ml_head_norm), B, S, b0, nb)
        y_fx = _fox_attn(proj, _fox_gate(small, b_fx, S, b0, nb), B, S, b0, nb)
        y_ca = _memattn(proj, kv, B, S, M, b0, nb)
        return y_ml, y_fx, y_ca

    w_r = w_router.T.astype(BF16)
    moe_weights = (w_exp_in.astype(F32), b_exp_in.reshape(N_EXPERTS, 1, -1).astype(F32), w_exp_out.astype(F32),
                   b_exp_out.reshape(N_EXPERTS, 1, -1).astype(F32))
    staged = []
    nb = B // MOE_PARTS
    for part in range(MOE_PARTS):
        y_ml, y_fx, y_ca = mixers(part * nb, nb)
        x2, hp, ri, rw, cnt = _merge(y_ml, y_fx, y_ca, proj, x2d, w_branch.astype(BF16), w_out.astype(BF16),
                                     row(norm_moe), w_r, b_router.reshape(N_EXPERTS, 1).astype(F32), part)
        scatter_idx, dest, pad_idx, tile_e, n_valid, n_rows = _moe_plan(ri, cnt)
        staged.append((x2, rw, dest, tile_e, n_valid, _sc_dispatch(hp, scatter_idx, pad_idx, n_rows)))
    gathered = [_sc_gather(_experts(tile_e, n_valid, xs, *moe_weights), dest.reshape(-1))
                for _, _, dest, tile_e, n_valid, xs in staged]
    out = None
    for part, ((x2, rw, *_), yg) in enumerate(zip(staged, gathered)):
        out = _combine(yg, rw, x2, row(norm_out), part, out)
    return out


def _moe_plan(ri, cnt):
    T = ri.shape[1]
    tm = EXPERT_TM
    n_tiles = (T * TOP_K) // tm + N_EXPERTS
    counts = cnt[:, 0].astype(I32)
    padded = ((counts + tm - 1) // tm) * tm
    gend = jnp.cumsum(padded)
    gstart = gend - padded
    expert_ids = jnp.arange(N_EXPERTS, dtype=I32)
    start_of = jnp.sum(jnp.where(ri[0:TOP_K, :, None] == expert_ids, gstart, 0), axis=-1)
    dest = start_of + ri[TOP_K:2 * TOP_K, :]
    n_valid = gend[-1] // tm
    tile_ids = jnp.arange(n_tiles, dtype=I32)
    last_tile = jnp.minimum(tile_ids, n_valid - 1)
    tile_e = jnp.minimum(jnp.sum((gend[None, :] <= last_tile[:, None] * tm).astype(I32), axis=1), N_EXPERTS - 1)

    slot = jnp.arange(tm, dtype=I32)
    spare = n_tiles * tm + slot % SC_CHUNK
    pad_idx = jnp.where(slot[None, :] < (padded - counts)[:, None], (gstart + counts)[:, None] + slot[None, :],
                        spare[None, :]).reshape(-1)

    return (_scatter_indices(dest), dest, pad_idx, tile_e.astype(I32), n_valid.reshape(1).astype(I32),
            n_tiles * tm + SC_CHUNK)


def kernel(x, mem, norm_mix, w_in, b_ml_gates, conv_ml, ml_head_norm, b_fx_gate, norm_mem, w_mem_kv, w_branch,
           w_out, norm_moe, w_router, b_router, w_exp_in, b_exp_in, w_exp_out, b_exp_out, norm_final):
    B, S, D = x.shape
    M = mem.shape[1]
    depth = norm_mix.shape[0]
    assert depth == 1, "the combine kernel fuses the final norm, so exactly one layer is supported"
    assert D == D_MODEL and S % ML_BLOCK == 0 and S % FX_T == 0 and S % CA_TQ == 0 and B % MOE_PARTS == 0
    out = _layer(x.reshape(B * S, D), mem.reshape(B * M, D), B, S, M, norm_mix[0], w_in[0], b_ml_gates[0],
                 conv_ml[0], ml_head_norm[0], b_fx_gate[0], norm_mem[0], w_mem_kv[0], w_branch[0], w_out[0],
                 norm_moe[0], w_router[0], b_router[0], w_exp_in[0], b_exp_in[0], w_exp_out[0], b_exp_out[0],
                 norm_final)
    return out.reshape(B, S, D)
```

```python
import functools

import jax
import jax.numpy as jnp
from jax import lax
from jax.experimental import pallas as pl
from jax.experimental.pallas import tpu as pltpu
from jax.experimental.pallas import tpu_sc as plsc

F32 = jnp.float32
BF16 = jnp.bfloat16
I32 = jnp.int32

D_MODEL = 1024
ML_HEADS = 4
ML_DQK = 128
ML_DV = 256
ML_CONV = 4
FX_HEADS = 8
FX_DH = 128
CA_HEADS = 4
CA_DH = 256
N_EXPERTS = 32
TOP_K = 4
D_FF = D_MODEL
SWIGLU_LIMIT = 7.0
SWIGLU_ALPHA = 1.702
EPS = 1e-5
LANES = 128
HALF = D_MODEL // 2
HI_MASK = -65536

COL_MLQK, COL_MLV, COL_MLO, COL_FXQ, COL_FXK, COL_FXV, COL_CAQ, COL_GATE0 = 0, 1, 2, 3, 4, 5, 6, 7
N_BIG = 10 * D_MODEL

VMEM_LIMIT = 56 * 1024 * 1024


def _cparams(sem):
    return pltpu.CompilerParams(dimension_semantics=sem, vmem_limit_bytes=VMEM_LIMIT)


def _rms(x, g):
    return x * lax.rsqrt(jnp.mean(x * x, axis=-1, keepdims=True) + EPS) * g


def _log_sigmoid(x):
    return jnp.minimum(x, 0.0) - jnp.log1p(jnp.exp(-jnp.abs(x)))


def _pack_rows(y):
    bits = lax.bitcast_convert_type(y.astype(BF16).astype(F32), I32)
    return lax.shift_right_logical(bits[:, :HALF], 16) | (bits[:, HALF:] & HI_MASK)


def _unpack_rows(w):
    lo = lax.bitcast_convert_type(lax.shift_left(w, 16), F32)
    hi = lax.bitcast_convert_type(w & HI_MASK, F32)
    return lo, hi


def _inproj_body(x_ref, g_ref, w_ref, ws_ref, o_ref, os_ref, h_ref):
    @pl.when(pl.program_id(1) == 0)
    def _():
        hb = _rms(x_ref[...], g_ref[...]).astype(BF16)
        h_ref[...] = hb
        os_ref[...] = jnp.dot(hb, ws_ref[...], preferred_element_type=F32)

    o_ref[...] = jnp.dot(h_ref[...], w_ref[...], preferred_element_type=F32).astype(BF16)


def _inproj(x2d, g, w_big, w_small):
    T = x2d.shape[0]
    tm = min(1024, T)
    tn = 2560
    return pl.pallas_call(
        _inproj_body,
        grid=(T // tm, N_BIG // tn),
        in_specs=[
            pl.BlockSpec((tm, D_MODEL), lambda i, j: (i, 0)),
            pl.BlockSpec((1, D_MODEL), lambda i, j: (0, 0)),
            pl.BlockSpec((D_MODEL, tn), lambda i, j: (0, j)),
            pl.BlockSpec((D_MODEL, LANES), lambda i, j: (0, 0)),
        ],
        out_specs=[
            pl.BlockSpec((tm, tn), lambda i, j: (i, j)),
            pl.BlockSpec((tm, LANES), lambda i, j: (i, 0)),
        ],
        out_shape=[
            jax.ShapeDtypeStruct((T, N_BIG), BF16),
            jax.ShapeDtypeStruct((T, LANES), F32),
        ],
        scratch_shapes=[pltpu.VMEM((tm, D_MODEL), BF16)],
        compiler_params=_cparams(("parallel", "arbitrary")),
        name="inproj",
    )(x2d, g, w_big, w_small)


ML_BLOCK = 1024
ML_MB = 1
ML_CHUNK = 128
CONV_PAD = 8


def _mlstm_body(qk_ref, v_ref, o_ref, g_ref, cw_ref, bg_ref, hn_ref, y_ref, xbuf, c_st, n_st, m_st):
    L = ML_CHUNK

    @pl.when(pl.program_id(1) == 0)
    def _():
        xbuf[:, 0:CONV_PAD, :] = jnp.zeros((ML_MB, CONV_PAD, D_MODEL), F32)
        c_st[...] = jnp.zeros_like(c_st)
        n_st[...] = jnp.zeros_like(n_st)
        m_st[...] = jnp.zeros_like(m_st)

    for bb in range(ML_MB):
        xbuf[bb, CONV_PAD:CONV_PAD + ML_BLOCK, :] = qk_ref[bb].astype(F32)
    cw = cw_ref[...]
    row = lax.broadcasted_iota(I32, (L, L), 0)
    col = lax.broadcasted_iota(I32, (L, L), 1)
    tri = (row >= col).astype(BF16)
    causal_t = col >= row
    bg = bg_ref[...]
    scale = ML_DQK ** -0.5
    nt_dims = (((1,), (1,)), ((), ()))

    def chunk(bb, c):
        r0 = c * L
        conv = cw[0:1, :] * xbuf[bb, r0 + CONV_PAD - 3:r0 + CONV_PAD - 3 + L, :]
        for j in range(1, ML_CONV):
            s0 = r0 + CONV_PAD - 3 + j
            conv = conv + cw[j:j + 1, :] * xbuf[bb, s0:s0 + L, :]
        act = conv * jax.nn.sigmoid(conv)

        gates = g_ref[bb, r0:r0 + L, :] + bg
        lf = _log_sigmoid(gates)
        cum = jnp.zeros((L, LANES), F32)
        for _ in range(3):
            piece = lf.astype(BF16)
            cum = cum + jnp.dot(tri, piece, preferred_element_type=F32)
            lf = lf - piece.astype(F32)
        gates_t = gates.T
        cum_t = cum.T
        for h in range(ML_HEADS):
            b_row = cum_t[ML_HEADS + h:ML_HEADS + h + 1, :]
            i_row = gates_t[h:h + 1, :]
            a_col = gates[:, h:h + 1] - cum[:, ML_HEADS + h:ML_HEADS + h + 1]
            st = bb * ML_HEADS + h
            m_prev = m_st[st]
            dm = jnp.where(causal_t, a_col + b_row, -jnp.inf)
            m_inter = b_row + m_prev
            m_t = jnp.maximum(jnp.max(dm, axis=0, keepdims=True), m_inter)
            w_intra = jnp.exp(dm - m_t)
            w_inter = jnp.exp(m_inter - m_t)

            qb = (act[:, h * ML_DQK:(h + 1) * ML_DQK] * scale).astype(BF16)
            kb = act[:, (ML_HEADS + h) * ML_DQK:(ML_HEADS + h + 1) * ML_DQK].astype(BF16)
            v_t = v_ref[bb, r0:r0 + L, h * ML_DV:(h + 1) * ML_DV].astype(F32).T
            p_t = lax.dot_general(kb, qb, nt_dims, preferred_element_type=F32) * w_intra
            c_old = c_st[st]
            n_old = n_st[st]
            num = jnp.dot(v_t.astype(BF16), p_t.astype(BF16), preferred_element_type=F32) + w_inter * (
                lax.dot_general(c_old.astype(BF16), qb, nt_dims, preferred_element_type=F32))
            qn = lax.dot_general(jnp.broadcast_to(n_old, (8, ML_DQK)).astype(BF16), qb, nt_dims,
                                 preferred_element_type=F32)[0:1, :]
            den = jnp.sum(p_t, axis=0, keepdims=True) + w_inter * qn
            hv = num / jnp.maximum(jnp.abs(den), jnp.exp(-m_t))

            m_new = m_t[:, L - 1:L]
            b_last = b_row[:, L - 1:L]
            wk = jnp.exp(b_last - b_row + i_row - m_new)
            decay = jnp.exp(b_last + m_prev - m_new)
            c_st[st] = decay * c_old + jnp.dot((v_t * wk).astype(BF16), kb, preferred_element_type=F32)
            n_st[st] = decay * n_old + jnp.dot(jnp.broadcast_to(wk, (8, L)).astype(BF16), kb,
                                               preferred_element_type=F32)[0:1, :]
            m_st[st] = m_new

            hn = (hv * lax.rsqrt(jnp.mean(hv * hv, axis=0, keepdims=True) + EPS)).T
            og = o_ref[bb, r0:r0 + L, h * ML_DV:(h + 1) * ML_DV].astype(F32)
            y_ref[bb, r0:r0 + L, h * ML_DV:(h + 1) * ML_DV] = (
                hn * hn_ref[:, h * ML_DV:(h + 1) * ML_DV] * jax.nn.sigmoid(og)).astype(BF16)

    for c in range(ML_BLOCK // L):
        for bb in range(ML_MB):
            chunk(bb, c)

    xbuf[:, 0:CONV_PAD, :] = xbuf[:, ML_BLOCK:ML_BLOCK + CONV_PAD, :]


def _mlstm(proj, small, conv_w, b_gates, head_norm, B, S):
    T = B * S
    ns = S // ML_BLOCK
    assert B % ML_MB == 0
    proj3 = proj.reshape(B, S, N_BIG)
    blk = lambda col: pl.BlockSpec((ML_MB, ML_BLOCK, D_MODEL), lambda b, s: (b, s, col))
    out = pl.pallas_call(
        _mlstm_body,
        grid=(B // ML_MB, ns),
        in_specs=[
            blk(COL_MLQK),
            blk(COL_MLV),
            blk(COL_MLO),
            pl.BlockSpec((ML_MB, ML_BLOCK, LANES), lambda b, s: (b, s, 0)),
            pl.BlockSpec((ML_CONV, D_MODEL), lambda b, s: (0, 0)),
            pl.BlockSpec((1, LANES), lambda b, s: (0, 0)),
            pl.BlockSpec((1, D_MODEL), lambda b, s: (0, 0)),
        ],
        out_specs=blk(0),
        out_shape=jax.ShapeDtypeStruct((B, S, D_MODEL), BF16),
        scratch_shapes=[
            pltpu.VMEM((ML_MB, ML_BLOCK + CONV_PAD, D_MODEL), F32),
            pltpu.VMEM((ML_MB * ML_HEADS, ML_DV, ML_DQK), F32),
            pltpu.VMEM((ML_MB * ML_HEADS, 1, ML_DQK), F32),
            pltpu.VMEM((ML_MB * ML_HEADS, 1, 1), F32),
        ],
        compiler_params=_cparams(("parallel", "arbitrary")),
        name="mlstm",
    )(proj3, proj3, proj3, small.reshape(B, S, LANES), conv_w, b_gates, head_norm)
    return out.reshape(T, D_MODEL)


FX_T = 512
FX_HP = 2
FX_VR = FX_DH + 16
LOG2E = 1.4426950408889634
N_PIECES = 3
FX_GATE_T = 128


def _fox_gate_body(g_ref, b_ref, o_ref):
    S = g_ref.shape[0]
    row = lax.broadcasted_iota(I32, (FX_GATE_T, FX_GATE_T), 0)
    col = lax.broadcasted_iota(I32, (FX_GATE_T, FX_GATE_T), 1)
    tri = (row >= col).astype(BF16)
    carry = jnp.zeros((1, LANES), F32)
    for blk in range(S // FX_GATE_T):
        rows = slice(blk * FX_GATE_T, (blk + 1) * FX_GATE_T)
        lf = _log_sigmoid(g_ref[rows, :] + b_ref[...])
        cum = carry
        for _ in range(N_PIECES):
            piece = lf.astype(BF16)
            cum = cum + jnp.dot(tri, piece, preferred_element_type=F32)
            lf = lf - piece.astype(F32)
        carry = cum[FX_GATE_T - 1:FX_GATE_T, :]
        o_ref[rows, :] = cum * (-LOG2E)


def _fox_gate(small, b_fx, B, S):
    return pl.pallas_call(
        _fox_gate_body,
        grid=(B,),
        in_specs=[
            pl.BlockSpec((S, LANES), lambda b: (b, 0)),
            pl.BlockSpec((1, LANES), lambda b: (0, 0)),
        ],
        out_specs=pl.BlockSpec((S, LANES), lambda b: (b, 0)),
        out_shape=jax.ShapeDtypeStruct((B * S, LANES), F32),
        compiler_params=_cparams(("parallel",)),
        name="fox_gate",
    )(small, b_fx)


def _fox_attn_body(q_ref, k_ref, v_ref, c_ref, o_ref, kx_ref, vt_ref, m_ref, acc_ref, s_ref):
    S = k_ref.shape[0]
    nq = S // FX_T

    c = c_ref[...]
    hi = c.astype(BF16)
    r1 = c - hi.astype(F32)
    mid = r1.astype(BF16)
    lo = (r1 - mid.astype(F32)).astype(BF16)
    sel_row = lax.broadcasted_iota(I32, (LANES, LANES), 0)
    sel_col = lax.broadcasted_iota(I32, (LANES, LANES), 1)
    ones_rows = (lax.broadcasted_iota(I32, (FX_VR - FX_DH, FX_T), 0) == 0).astype(BF16)
    head_slices = [slice(hh * FX_DH, (hh + 1) * FX_DH) for hh in range(FX_HP)]
    for hh, sl in enumerate(head_slices):
        lane = 2 * ML_HEADS + pl.program_id(1) * FX_HP + hh
        pieces = None
        for p, part in enumerate((hi, mid, lo)):
            pick = jnp.logical_and(sel_row == lane, sel_col == p).astype(BF16)
            t = jnp.dot(part, pick, preferred_element_type=F32)
            pieces = t if pieces is None else pieces + t
        kx_ref[hh, :, 0:FX_DH] = k_ref[:, sl]
        kx_ref[hh, :, FX_DH:2 * FX_DH] = pieces.astype(BF16)
        for j in range(nq):
            vt = v_ref[j * FX_T:(j + 1) * FX_T, sl].astype(F32).T.astype(BF16)
            vt_ref[hh, j] = jnp.concatenate([vt, ones_rows], axis=0)

    piece_rows = (lax.broadcasted_iota(I32, (FX_DH, FX_T), 0) < N_PIECES).astype(BF16)

    def start(i):
        q_x = []
        for sl in head_slices:
            q_t = (q_ref[i * FX_T:(i + 1) * FX_T, sl].astype(F32) * (FX_DH ** -0.5 * LOG2E)).T.astype(BF16)
            q_x.append(jnp.concatenate([q_t, piece_rows], axis=0))
        m_ref[i % 2] = jnp.full(m_ref.shape[1:], -jnp.inf, F32)
        acc_ref[i % 2] = jnp.zeros(acc_ref.shape[1:], F32)
        return q_x

    def key_rows(j):
        return pl.ds(j * FX_T, FX_T) if isinstance(j, int) else pl.ds(pl.multiple_of(j * FX_T, FX_T), FX_T)

    def scores(q_x, j, slot):
        for hh in range(FX_HP):
            s_ref[slot, hh] = jnp.dot(kx_ref[hh, key_rows(j), :], q_x[hh], preferred_element_type=F32)

    def consume(par, j, slot, masked):
        for hh in range(FX_HP):
            s = s_ref[slot, hh]
            if masked:
                key = lax.broadcasted_iota(I32, (FX_T, FX_T), 0)
                qry = lax.broadcasted_iota(I32, (FX_T, FX_T), 1)
                s = jnp.where(qry >= key, s, -jnp.inf)
            m_old = m_ref[par, hh]
            m_new = jnp.maximum(m_old, jnp.max(s, axis=0, keepdims=True))
            p = jnp.exp2(s - m_new).astype(BF16)
            acc_ref[par, hh] = jnp.exp2(m_old - m_new) * acc_ref[par, hh] + jnp.dot(
                vt_ref[hh, j], p, preferred_element_type=F32)
            m_ref[par, hh] = m_new

    def finish(i, slot):
        consume(i % 2, i, slot, True)
        for hh, sl in enumerate(head_slices):
            acc = acc_ref[i % 2, hh]
            o_ref[i * FX_T:(i + 1) * FX_T, sl] = (acc[0:FX_DH, :] / acc[FX_DH:FX_DH + 1, :]).T.astype(BF16)

    diag_slot = 0
    for i in range(nq):
        q_x = start(i)
        first = 0 if i == 0 else 1 - diag_slot
        scores(q_x, 0, first)
        if i > 0:
            finish(i - 1, diag_slot)

        def pair(jj, carry, q_x=q_x, first=first, par=i % 2):
            j = 2 * jj
            scores(q_x, j + 1, 1 - first)
            consume(par, j, first, False)
            scores(q_x, j + 2, first)
            consume(par, j + 1, 1 - first, False)
            return carry

        if i >= 2:
            lax.fori_loop(0, i // 2, pair, 0)
        if i % 2 == 1:
            scores(q_x, i, 1 - first)
            consume(i % 2, i - 1, first, False)
            diag_slot = 1 - first
        else:
            diag_slot = first
    finish(nq - 1, diag_slot)


def _fox_attn(proj, c_neg, B, S):
    T = B * S
    nq = S // FX_T
    wide = FX_HP * FX_DH
    cq = COL_FXQ * (D_MODEL // wide)
    ck = COL_FXK * (D_MODEL // wide)
    cv = COL_FXV * (D_MODEL // wide)
    proj3 = proj.reshape(B, S, N_BIG)
    out = pl.pallas_call(
        _fox_attn_body,
        grid=(B, FX_HEADS // FX_HP),
        in_specs=[
            pl.BlockSpec((None, S, wide), lambda b, h: (b, 0, cq + h)),
            pl.BlockSpec((None, S, wide), lambda b, h: (b, 0, ck + h)),
            pl.BlockSpec((None, S, wide), lambda b, h: (b, 0, cv + h)),
            pl.BlockSpec((None, S, LANES), lambda b, h: (b, 0, 0)),
        ],
        out_specs=pl.BlockSpec((None, S, wide), lambda b, h: (b, 0, h)),
        out_shape=jax.ShapeDtypeStruct((B, S, D_MODEL), BF16),
        scratch_shapes=[
            pltpu.VMEM((FX_HP, S, 2 * FX_DH), BF16),
            pltpu.VMEM((FX_HP, nq, FX_VR, FX_T), BF16),
            pltpu.VMEM((2, FX_HP, 1, FX_T), F32),
            pltpu.VMEM((2, FX_HP, FX_VR, FX_T), F32),
            pltpu.VMEM((2, FX_HP, FX_T, FX_T), F32),
        ],
        compiler_params=_cparams(("parallel", "parallel")),
        name="fox_attn",
    )(proj3, proj3, proj3, c_neg.reshape(B, S, LANES))
    return out.reshape(T, D_MODEL)


def _memkv_body(x_ref, g_ref, w_ref, o_ref):
    hb = _rms(x_ref[...], g_ref[...]).astype(BF16)
    o_ref[...] = jnp.dot(hb, w_ref[...], preferred_element_type=F32).astype(BF16)


def _memkv(mem2d, g, w_kv):
    R = mem2d.shape[0]
    tm = min(512, R)
    N = w_kv.shape[1]
    return pl.pallas_call(
        _memkv_body,
        grid=(R // tm,),
        in_specs=[
            pl.BlockSpec((tm, D_MODEL), lambda i: (i, 0)),
            pl.BlockSpec((1, D_MODEL), lambda i: (0, 0)),
            pl.BlockSpec((D_MODEL, N), lambda i: (0, 0)),
        ],
        out_specs=pl.BlockSpec((tm, N), lambda i: (i, 0)),
        out_shape=jax.ShapeDtypeStruct((R, N), BF16),
        compiler_params=_cparams(("parallel",)),
        name="memkv",
    )(mem2d, g, w_kv)


CA_TQ = 2048


def _memattn_body(q_ref, k_ref, v_ref, o_ref):
    scale = CA_DH ** -0.5
    for h in range(CA_HEADS):
        sl = slice(h * CA_DH, (h + 1) * CA_DH)
        s = lax.dot_general(q_ref[:, sl], k_ref[:, sl], (((1,), (1,)), ((), ())),
                            preferred_element_type=F32) * scale
        p = jnp.exp(s - jnp.max(s, axis=-1, keepdims=True))
        l = jnp.sum(p, axis=-1, keepdims=True)
        o = jnp.dot(p.astype(BF16), v_ref[:, sl], preferred_element_type=F32) / l
        o_ref[:, sl] = o.astype(BF16)


def _memattn(proj, kv, B, S, M):
    T = B * S
    nq = S // CA_TQ
    kv3 = kv.reshape(B, M, 2 * D_MODEL)
    return pl.pallas_call(
        _memattn_body,
        grid=(B, nq),
        in_specs=[
            pl.BlockSpec((CA_TQ, D_MODEL), lambda b, i: (b * nq + i, COL_CAQ)),
            pl.BlockSpec((None, M, D_MODEL), lambda b, i: (b, 0, 0)),
            pl.BlockSpec((None, M, D_MODEL), lambda b, i: (b, 0, 1)),
        ],
        out_specs=pl.BlockSpec((CA_TQ, D_MODEL), lambda b, i: (b * nq + i, 0)),
        out_shape=jax.ShapeDtypeStruct((T, D_MODEL), BF16),
        compiler_params=_cparams(("parallel", "arbitrary")),
        name="memattn",
    )(proj, kv3, kv3)


MERGE_TM = 512
MOE_PARTS = 2


def _merge_body(y0_ref, y1_ref, y2_ref, g0_ref, g1_ref, g2_ref, x_ref, wb_ref, wo_ref, gn_ref, wr_ref, br_ref,
                o_ref, hp_ref, ri_ref, rw_ref, cnt_ref, carry_ref):
    merged = None
    for n, (y_ref, g_ref) in enumerate(((y0_ref, g0_ref), (y1_ref, g1_ref), (y2_ref, g2_ref))):
        p = jnp.dot(y_ref[...], wb_ref[n], preferred_element_type=F32)
        t = jax.nn.sigmoid(g_ref[...].astype(F32)) * p
        merged = t if merged is None else merged + t
    x2 = x_ref[...] + jnp.dot(merged.astype(BF16), wo_ref[...], preferred_element_type=F32)
    o_ref[...] = x2
    _route(x2, gn_ref, wr_ref, br_ref, hp_ref, ri_ref, rw_ref, cnt_ref, carry_ref)


def _merge(y_ml, y_fx, y_ca, proj, x2d, w_branch, w_out, g_moe, w_router, b_router, part):
    T = x2d.shape[0] // MOE_PARTS
    tm = MERGE_TM
    off = part * (T // tm)
    src = lambda i: (off + i, 0)
    row = lambda i: (i, 0)
    const = lambda i: (0, 0)
    return pl.pallas_call(
        _merge_body,
        grid=(T // tm,),
        in_specs=[
            pl.BlockSpec((tm, D_MODEL), src),
            pl.BlockSpec((tm, D_MODEL), src),
            pl.BlockSpec((tm, D_MODEL), src),
            pl.BlockSpec((tm, D_MODEL), lambda i: (off + i, COL_GATE0)),
            pl.BlockSpec((tm, D_MODEL), lambda i: (off + i, COL_GATE0 + 1)),
            pl.BlockSpec((tm, D_MODEL), lambda i: (off + i, COL_GATE0 + 2)),
            pl.BlockSpec((tm, D_MODEL), src),
            pl.BlockSpec((3, D_MODEL, D_MODEL), lambda i: (0, 0, 0)),
            pl.BlockSpec((D_MODEL, D_MODEL), const),
            pl.BlockSpec((1, D_MODEL), const),
            pl.BlockSpec((N_EXPERTS, D_MODEL), const),
            pl.BlockSpec((N_EXPERTS, 1), const),
        ],
        out_specs=[
            pl.BlockSpec((tm, D_MODEL), row),
            pl.BlockSpec((tm, HALF), row),
            pl.BlockSpec((2 * TOP_K, tm), lambda i: (0, i)),
            pl.BlockSpec((2 * TOP_K, tm), lambda i: (0, i)),
            pl.BlockSpec((N_EXPERTS, 1), const),
        ],
        out_shape=[
            jax.ShapeDtypeStruct((T, D_MODEL), F32),
            jax.ShapeDtypeStruct((T, HALF), I32),
            jax.ShapeDtypeStruct((2 * TOP_K, T), I32),
            jax.ShapeDtypeStruct((2 * TOP_K, T), F32),
            jax.ShapeDtypeStruct((N_EXPERTS, 1), F32),
        ],
        scratch_shapes=[pltpu.VMEM((N_EXPERTS, 1), F32)],
        compiler_params=_cparams(("arbitrary",)),
        name="merge_router",
    )(y_ml, y_fx, y_ca, proj, proj, proj, x2d, w_branch, w_out, g_moe, w_router, b_router)


def _route(x2, g_ref, wr_ref, br_ref, hp_ref, ri_ref, rw_ref, cnt_ref, carry_ref):
    tm = MERGE_TM

    @pl.when(pl.program_id(0) == 0)
    def _():
        carry_ref[...] = jnp.zeros_like(carry_ref)

    h = _rms(x2, g_ref[...])
    hp_ref[...] = _pack_rows(h)
    logits = lax.dot_general(wr_ref[...], h.astype(BF16), (((1,), (1,)), ((), ())),
                             preferred_element_type=F32) + br_ref[...]
    eid = lax.broadcasted_iota(I32, (N_EXPERTS, tm), 0).astype(F32)

    work = logits
    onehot_sum = jnp.zeros((N_EXPERTS, tm), F32)
    vals, sels, idxs = [], [], []
    for _ in range(TOP_K):
        mx = jnp.max(work, axis=0, keepdims=True)
        idx = jnp.min(jnp.where(work == mx, eid, float(N_EXPERTS)), axis=0, keepdims=True)
        sel = eid == idx
        onehot_sum = onehot_sum + sel.astype(F32)
        work = jnp.where(sel, -jnp.inf, work)
        vals.append(mx)
        sels.append(sel)
        idxs.append(idx)
    exps = [jnp.exp(v - vals[0]) for v in vals]
    total = exps[0] + exps[1] + exps[2] + exps[3]

    earlier = (lax.broadcasted_iota(I32, (tm, tm), 0) < lax.broadcasted_iota(I32, (tm, tm), 1)).astype(BF16)
    before = jnp.dot(onehot_sum.astype(BF16), earlier, preferred_element_type=F32) + carry_ref[...]
    carry_ref[...] = carry_ref[...] + jnp.sum(onehot_sum, axis=1, keepdims=True)
    cnt_ref[...] = carry_ref[...]

    out_row = lax.broadcasted_iota(I32, (2 * TOP_K, tm), 0)
    ri = jnp.zeros((2 * TOP_K, tm), I32)
    rw = jnp.zeros((2 * TOP_K, tm), F32)
    for k in range(TOP_K):
        rank = jnp.sum(jnp.where(sels[k], before, 0.0), axis=0, keepdims=True)
        ri = jnp.where(out_row == k, idxs[k].astype(I32), ri)
        ri = jnp.where(out_row == TOP_K + k, rank.astype(I32), ri)
        rw = jnp.where(out_row == k, exps[k] / total, rw)
    ri_ref[...] = ri
    rw_ref[...] = rw


EXPERT_TM = 512
SC_CORES = 2
SC_SUBCORES = 16
SC_WORKERS = SC_CORES * SC_SUBCORES
SC_CHUNK = 64
PAD_SLOTS = N_EXPERTS * EXPERT_TM


def _sc_mesh():
    return plsc.VectorSubcoreMesh(core_axis_name="c", subcore_axis_name="s")


def _sc_worker():
    return lax.axis_index("s") * SC_CORES + lax.axis_index("c")


def _scatter_indices(dest):
    T = dest.shape[1]
    n_ch = T // (SC_WORKERS * SC_CHUNK)
    idx = dest.reshape(TOP_K, SC_WORKERS, n_ch, SC_CHUNK).transpose(1, 2, 0, 3)
    return idx.reshape(SC_WORKERS, n_ch * TOP_K, SC_CHUNK)


def _sc_dispatch(hp, idx, pad_idx, n_rows, after):
    T = hp.shape[0]
    per_w = T // SC_WORKERS
    n_ch = per_w // SC_CHUNK
    n_pc = PAD_SLOTS // (SC_WORKERS * SC_CHUNK)
    assert per_w % SC_CHUNK == 0 and n_ch >= 2 and n_ch % 2 == 0
    pidx = pad_idx.reshape(SC_WORKERS, n_pc, SC_CHUNK)
    zeros = jnp.zeros((SC_CHUNK, HALF), I32)

    @functools.partial(
        pl.kernel, mesh=_sc_mesh(),
        out_type=jax.ShapeDtypeStruct((n_rows, HALF), I32),
        scratch_types=[
            pltpu.VMEM((n_ch * TOP_K, SC_CHUNK), I32),
            pltpu.VMEM((n_pc, SC_CHUNK), I32),
            pltpu.VMEM((2, SC_CHUNK, HALF), I32),
            pltpu.SemaphoreType.DMA((2,)),
            pltpu.SemaphoreType.DMA((2,)),
        ],
        name="sc_dispatch",
    )
    def k(hp_hbm, idx_hbm, pidx_hbm, zeros_hbm, *rest):
        xs_hbm, idx_v, pidx_v, rows_v, lsem, ssem = rest[-6:]
        wid = _sc_worker()
        base = wid * per_w
        pltpu.sync_copy(idx_hbm.at[wid], idx_v)
        pltpu.sync_copy(pidx_hbm.at[wid], pidx_v)

        pltpu.sync_copy(zeros_hbm, rows_v.at[0])
        for p in range(n_pc):
            pltpu.make_async_copy(rows_v.at[0], xs_hbm.at[pidx_v.at[p]], ssem.at[0]).start()
        for p in range(n_pc):
            pltpu.make_async_copy(rows_v.at[0], xs_hbm.at[pidx_v.at[p]], ssem.at[0]).wait()

        def load(i, slot):
            return pltpu.make_async_copy(hp_hbm.at[pl.ds(base + i * SC_CHUNK, SC_CHUNK)], rows_v.at[slot],
                                         lsem.at[slot])

        def scatter(i, kk, slot):
            return pltpu.make_async_copy(rows_v.at[slot], xs_hbm.at[idx_v.at[i * TOP_K + kk]], ssem.at[slot])

        load(0, 0).start()

        def body(i2, carry):
            for slot in range(2):
                i = i2 * 2 + slot
                nxt = 1 - slot

                @pl.when(i + 1 < n_ch)
                def _():
                    @pl.when(i >= 1)
                    def _():
                        for kk in range(TOP_K):
                            scatter(i - 1, kk, nxt).wait()
                    load(i + 1, nxt).start()

                load(i, slot).wait()
                for kk in range(TOP_K):
                    scatter(i, kk, slot).start()
            return carry

        lax.fori_loop(0, n_ch // 2, body, 0)
        for kk in range(TOP_K):
            scatter(n_ch - 2, kk, 0).wait()
            scatter(n_ch - 1, kk, 1).wait()

    return k(hp, idx, pidx, zeros, *(() if after is None else (after,)))


def _sc_gather(table, idx):
    n = idx.shape[0]
    per_w = n // SC_WORKERS
    n_ch = per_w // SC_CHUNK
    assert per_w % SC_CHUNK == 0 and n_ch >= 2 and n_ch % 2 == 0

    @functools.partial(
        pl.kernel, mesh=_sc_mesh(),
        out_type=jax.ShapeDtypeStruct((n, HALF), I32),
        scratch_types=[
            pltpu.VMEM((n_ch, SC_CHUNK), I32),
            pltpu.VMEM((2, SC_CHUNK, HALF), I32),
            pltpu.SemaphoreType.DMA((2,)),
            pltpu.SemaphoreType.DMA((2,)),
        ],
        name="sc_gather",
    )
    def k(table_hbm, idx_hbm, out_hbm, idx_v, rows_v, gsem, wsem):
        wid = _sc_worker()
        base = wid * per_w
        pltpu.sync_copy(idx_hbm.at[wid], idx_v)

        def gather(i, slot):
            return pltpu.make_async_copy(table_hbm.at[idx_v.at[i]], rows_v.at[slot], gsem.at[slot])

        def writeback(i, slot):
            return pltpu.make_async_copy(rows_v.at[slot], out_hbm.at[pl.ds(base + i * SC_CHUNK, SC_CHUNK)],
                                         wsem.at[slot])

        gather(0, 0).start()

        def body(i2, carry):
            for slot in range(2):
                i = i2 * 2 + slot
                nxt = 1 - slot

                @pl.when(i + 1 < n_ch)
                def _():
                    @pl.when(i >= 1)
                    def _():
                        writeback(i - 1, nxt).wait()
                    gather(i + 1, nxt).start()

                gather(i, slot).wait()
                writeback(i, slot).start()
            return carry

        lax.fori_loop(0, n_ch // 2, body, 0)
        writeback(n_ch - 2, 0).wait()
        writeback(n_ch - 1, 1).wait()

    return k(table, idx.reshape(SC_WORKERS, n_ch, SC_CHUNK))


FF_CHUNK = 1024


def _expert_body(te_ref, nv_ref, x_ref, w1f_ref, b1_ref, w2f_ref, b2_ref, y_ref, w1_ref, w2_ref):
    i = pl.program_id(0)

    @pl.when(jnp.logical_or(i == 0, te_ref[i] != te_ref[jnp.maximum(i - 1, 0)]))
    def _():
        w1_ref[...] = w1f_ref[...].astype(BF16)
        w2_ref[...] = w2f_ref[...].astype(BF16)

    @pl.when(i < nv_ref[0])
    def _():
        lo, hi = _unpack_rows(x_ref[...])
        xb = jnp.concatenate([lo.astype(BF16), hi.astype(BF16)], axis=-1)
        acc = jnp.zeros((EXPERT_TM, D_MODEL), F32) + b2_ref[...]
        for c in range(D_FF // FF_CHUNK):
            def up(off):
                cs = slice(off + c * FF_CHUNK, off + (c + 1) * FF_CHUNK)
                return jnp.dot(xb, w1_ref[:, cs], preferred_element_type=F32) + b1_ref[:, cs]
            g = jnp.minimum(up(0), SWIGLU_LIMIT)
            lin = jnp.clip(up(D_FF), -SWIGLU_LIMIT, SWIGLU_LIMIT)
            a = g * jax.nn.sigmoid(SWIGLU_ALPHA * g) * (lin + 1.0)
            acc = acc + jnp.dot(a.astype(BF16), w2_ref[c * FF_CHUNK:(c + 1) * FF_CHUNK, :],
                                preferred_element_type=F32)
        y_ref[...] = _pack_rows(acc)


def _experts(tile_expert, n_valid, xs, w1, b1, w2, b2):
    n_rows = xs.shape[0]
    tm = EXPERT_TM
    n_tiles = n_rows // tm
    row = lambda i, te, nv: (jnp.minimum(i, nv[0] - 1), 0)
    grid_spec = pltpu.PrefetchScalarGridSpec(
        num_scalar_prefetch=2,
        grid=(n_tiles,),
        in_specs=[
            pl.BlockSpec((tm, HALF), row),
            pl.BlockSpec((None, D_MODEL, 2 * D_FF), lambda i, te, nv: (te[i], 0, 0)),
            pl.BlockSpec((None, 1, 2 * D_FF), lambda i, te, nv: (te[i], 0, 0)),
            pl.BlockSpec((None, D_FF, D_MODEL), lambda i, te, nv: (te[i], 0, 0)),
            pl.BlockSpec((None, 1, D_MODEL), lambda i, te, nv: (te[i], 0, 0)),
        ],
        out_specs=pl.BlockSpec((tm, HALF), row),
        scratch_shapes=[pltpu.VMEM((D_MODEL, 2 * D_FF), BF16), pltpu.VMEM((D_FF, D_MODEL), BF16)],
    )
    return pl.pallas_call(
        _expert_body,
        grid_spec=grid_spec,
        out_shape=jax.ShapeDtypeStruct((n_rows, HALF), I32),
        compiler_params=_cparams(("arbitrary",)),
        name="experts",
    )(tile_expert, n_valid, xs, w1, b1, w2, b2)


COMBINE_TM = 1024


def _combine_body(y0_ref, y1_ref, y2_ref, y3_ref, rw_ref, x_ref, g_ref, *rest):
    o_ref = rest[-1]
    acc = x_ref[...]
    rw = jnp.concatenate([rw_ref[...], jnp.zeros((LANES - 2 * TOP_K, COMBINE_TM), F32)], axis=0).T
    for k, y_ref in enumerate((y0_ref, y1_ref, y2_ref, y3_ref)):
        lo, hi = _unpack_rows(y_ref[...])
        acc = acc + rw[:, k:k + 1] * jnp.concatenate([lo, hi], axis=-1)
    o_ref[...] = _rms(acc, g_ref[...])


def _combine(yg, rw, x2, g, part, out_prev):
    T = x2.shape[0]
    tm = COMBINE_TM
    nt = T // tm
    in_specs = [
        pl.BlockSpec((tm, HALF), lambda i: (i, 0)),
        pl.BlockSpec((tm, HALF), lambda i: (nt + i, 0)),
        pl.BlockSpec((tm, HALF), lambda i: (2 * nt + i, 0)),
        pl.BlockSpec((tm, HALF), lambda i: (3 * nt + i, 0)),
        pl.BlockSpec((2 * TOP_K, tm), lambda i: (0, i)),
        pl.BlockSpec((tm, D_MODEL), lambda i: (i, 0)),
        pl.BlockSpec((1, D_MODEL), lambda i: (0, 0)),
    ]
    args = [yg, yg, yg, yg, rw, x2, g]
    aliases = {}
    if out_prev is not None:
        in_specs.append(pl.BlockSpec(memory_space=pl.ANY))
        args.append(out_prev)
        aliases = {len(args) - 1: 0}
    return pl.pallas_call(
        _combine_body,
        grid=(nt,),
        in_specs=in_specs,
        out_specs=pl.BlockSpec((tm, D_MODEL), lambda i: (part * nt + i, 0)),
        out_shape=jax.ShapeDtypeStruct((T * MOE_PARTS, D_MODEL), F32),
        input_output_aliases=aliases,
        compiler_params=_cparams(("parallel",)),
        name="combine",
    )(*args)


def _pad_lanes(v):
    v = v.reshape(1, -1).astype(F32)
    return jnp.pad(v, ((0, 0), (0, LANES - v.shape[1])))


def _layer(x2d, mem2d, B, S, M, norm_mix, w_in, b_ml_gates, conv_ml, ml_head_norm, b_fx_gate, norm_mem,
           w_mem_kv, w_branch, w_out, norm_moe, w_router, b_router, w_exp_in, b_exp_in, w_exp_out,
           b_exp_out, norm_out):
    T = B * S
    w16 = w_in.astype(BF16)
    w_big = jnp.concatenate([w16[:, 0:2048], w16[:, 2056:3080], w16[:, 3080:6152], w16[:, 6160:7184],
                             w16[:, 7184:10256]], axis=1)
    w_small = jnp.concatenate([w16[:, 2048:2056], w16[:, 6152:6160]], axis=1)
    w_small = jnp.pad(w_small, ((0, 0), (0, LANES - w_small.shape[1])))
    row = lambda v: v.reshape(1, -1).astype(F32)

    proj, small = _inproj(x2d, row(norm_mix), w_big, w_small)

    y_ml = _mlstm(proj, small, conv_ml.astype(F32), _pad_lanes(b_ml_gates), row(ml_head_norm), B, S)

    b_fx = jnp.pad(b_fx_gate.reshape(1, -1).astype(F32), ((0, 0), (2 * ML_HEADS, LANES - 2 * ML_HEADS - FX_HEADS)))
    y_fx = _fox_attn(proj, _fox_gate(small, b_fx, B, S), B, S)

    kv = _memkv(mem2d, row(norm_mem), w_mem_kv.astype(BF16))
    y_ca = _memattn(proj, kv, B, S, M)

    w_r = w_router.T.astype(BF16)
    moe_weights = (w_exp_in.astype(F32), b_exp_in.reshape(N_EXPERTS, 1, -1).astype(F32), w_exp_out.astype(F32),
                   b_exp_out.reshape(N_EXPERTS, 1, -1).astype(F32))
    staged = []
    for part in range(MOE_PARTS):
        x2, hp, ri, rw, cnt = _merge(y_ml, y_fx, y_ca, proj, x2d, w_branch.astype(BF16), w_out.astype(BF16),
                                     row(norm_moe), w_r, b_router.reshape(N_EXPERTS, 1).astype(F32), part)
        scatter_idx, dest, pad_idx, tile_e, n_valid, n_rows = _moe_plan(ri, cnt)
        xs = _sc_dispatch(hp, scatter_idx, pad_idx, n_rows, staged[-1][-1] if staged else None)
        staged.append((x2, rw, dest, tile_e, n_valid, xs))
    gathered = [_sc_gather(_experts(tile_e, n_valid, xs, *moe_weights), dest.reshape(-1))
                for _, _, dest, tile_e, n_valid, xs in staged]
    out = None
    for part, ((x2, rw, *_), yg) in enumerate(zip(staged, gathered)):
        out = _combine(yg, rw, x2, row(norm_out), part, out)
    return out


def _moe_plan(ri, cnt):
    T = ri.shape[1]
    tm = EXPERT_TM
    n_tiles = (T * TOP_K) // tm + N_EXPERTS
    counts = cnt[:, 0].astype(I32)
    padded = ((counts + tm - 1) // tm) * tm
    gend = jnp.cumsum(padded)
    gstart = gend - padded
    expert_ids = jnp.arange(N_EXPERTS, dtype=I32)
    start_of = jnp.sum(jnp.where(ri[0:TOP_K, :, None] == expert_ids, gstart, 0), axis=-1)
    dest = start_of + ri[TOP_K:2 * TOP_K, :]
    n_valid = gend[-1] // tm
    tile_ids = jnp.arange(n_tiles, dtype=I32)
    last_tile = jnp.minimum(tile_ids, n_valid - 1)
    tile_e = jnp.minimum(jnp.sum((gend[None, :] <= last_tile[:, None] * tm).astype(I32), axis=1), N_EXPERTS - 1)

    slot = jnp.arange(tm, dtype=I32)
    spare = n_tiles * tm + slot % SC_CHUNK
    pad_idx = jnp.where(slot[None, :] < (padded - counts)[:, None], (gstart + counts)[:, None] + slot[None, :],
                        spare[None, :]).reshape(-1)

    return (_scatter_indices(dest), dest, pad_idx, tile_e.astype(I32), n_valid.reshape(1).astype(I32),
            n_tiles * tm + SC_CHUNK)


def kernel(x, mem, norm_mix, w_in, b_ml_gates, conv_ml, ml_head_norm, b_fx_gate, norm_mem, w_mem_kv, w_branch,
           w_out, norm_moe, w_router, b_router, w_exp_in, b_exp_in, w_exp_out, b_exp_out, norm_final):
    B, S, D = x.shape
    M = mem.shape[1]
    depth = norm_mix.shape[0]
    assert depth == 1, "the combine kernel fuses the final norm, so exactly one layer is supported"
    assert D == D_MODEL and S % ML_BLOCK == 0 and S % FX_T == 0 and S % CA_TQ == 0
    out = _layer(x.reshape(B * S, D), mem.reshape(B * M, D), B, S, M, norm_mix[0], w_in[0], b_ml_gates[0],
                 conv_ml[0], ml_head_norm[0], b_fx_gate[0], norm_mem[0], w_mem_kv[0], w_branch[0], w_out[0],
                 norm_moe[0], w_router[0], b_router[0], w_exp_in[0], b_exp_in[0], w_exp_out[0], b_exp_out[0],
                 norm_final)
    return out.reshape(B, S, D)
```

```python
import functools

import jax
import jax.numpy as jnp
from jax import lax
from jax.experimental import pallas as pl
from jax.experimental.pallas import tpu as pltpu
from jax.experimental.pallas import tpu_sc as plsc

F32 = jnp.float32
BF16 = jnp.bfloat16
I32 = jnp.int32

D_MODEL = 1024
ML_HEADS = 4
ML_DQK = 128
ML_DV = 256
ML_CONV = 4
FX_HEADS = 8
FX_DH = 128
CA_HEADS = 4
CA_DH = 256
N_EXPERTS = 32
TOP_K = 4
D_FF = D_MODEL
SWIGLU_LIMIT = 7.0
SWIGLU_ALPHA = 1.702
EPS = 1e-5
LANES = 128
HALF = D_MODEL // 2
HI_MASK = -65536

COL_MLQK, COL_MLV, COL_MLO, COL_FXQ, COL_FXK, COL_FXV, COL_CAQ, COL_GATE0 = 0, 1, 2, 3, 4, 5, 6, 7
N_BIG = 10 * D_MODEL

VMEM_LIMIT = 56 * 1024 * 1024


def _cparams(sem):
    return pltpu.CompilerParams(dimension_semantics=sem, vmem_limit_bytes=VMEM_LIMIT)


def _rms(x, g):
    return x * lax.rsqrt(jnp.mean(x * x, axis=-1, keepdims=True) + EPS) * g


def _log_sigmoid(x):
    return jnp.minimum(x, 0.0) - jnp.log1p(jnp.exp(-jnp.abs(x)))


def _pack_rows(y):
    bits = lax.bitcast_convert_type(y.astype(BF16).astype(F32), I32)
    return lax.shift_right_logical(bits[:, :HALF], 16) | (bits[:, HALF:] & HI_MASK)


def _unpack_rows(w):
    lo = lax.bitcast_convert_type(lax.shift_left(w, 16), F32)
    hi = lax.bitcast_convert_type(w & HI_MASK, F32)
    return lo, hi


def _inproj_body(x_ref, g_ref, w_ref, ws_ref, o_ref, os_ref, h_ref):
    @pl.when(pl.program_id(1) == 0)
    def _():
        hb = _rms(x_ref[...], g_ref[...]).astype(BF16)
        h_ref[...] = hb
        os_ref[...] = jnp.dot(hb, ws_ref[...], preferred_element_type=F32)

    o_ref[...] = jnp.dot(h_ref[...], w_ref[...], preferred_element_type=F32).astype(BF16)


def _inproj(x2d, g, w_big, w_small):
    T = x2d.shape[0]
    tm = min(1024, T)
    tn = 2560
    return pl.pallas_call(
        _inproj_body,
        grid=(T // tm, N_BIG // tn),
        in_specs=[
            pl.BlockSpec((tm, D_MODEL), lambda i, j: (i, 0)),
            pl.BlockSpec((1, D_MODEL), lambda i, j: (0, 0)),
            pl.BlockSpec((D_MODEL, tn), lambda i, j: (0, j)),
            pl.BlockSpec((D_MODEL, LANES), lambda i, j: (0, 0)),
        ],
        out_specs=[
            pl.BlockSpec((tm, tn), lambda i, j: (i, j)),
            pl.BlockSpec((tm, LANES), lambda i, j: (i, 0)),
        ],
        out_shape=[
            jax.ShapeDtypeStruct((T, N_BIG), BF16),
            jax.ShapeDtypeStruct((T, LANES), F32),
        ],
        scratch_shapes=[pltpu.VMEM((tm, D_MODEL), BF16)],
        compiler_params=_cparams(("parallel", "arbitrary")),
        name="inproj",
    )(x2d, g, w_big, w_small)


ML_BLOCK = 1024
ML_MB = 1
ML_CHUNK = 128
CONV_PAD = 8


def _mlstm_body(qk_ref, v_ref, o_ref, g_ref, cw_ref, bg_ref, hn_ref, y_ref, xbuf, c_st, n_st, m_st):
    L = ML_CHUNK

    @pl.when(pl.program_id(1) == 0)
    def _():
        xbuf[:, 0:CONV_PAD, :] = jnp.zeros((ML_MB, CONV_PAD, D_MODEL), F32)
        c_st[...] = jnp.zeros_like(c_st)
        n_st[...] = jnp.zeros_like(n_st)
        m_st[...] = jnp.zeros_like(m_st)

    for bb in range(ML_MB):
        xbuf[bb, CONV_PAD:CONV_PAD + ML_BLOCK, :] = qk_ref[bb].astype(F32)
    cw = cw_ref[...]
    row = lax.broadcasted_iota(I32, (L, L), 0)
    col = lax.broadcasted_iota(I32, (L, L), 1)
    tri = (row >= col).astype(BF16)
    causal_t = col >= row
    bg = bg_ref[...]
    scale = ML_DQK ** -0.5
    nt_dims = (((1,), (1,)), ((), ()))

    def chunk(bb, c):
        r0 = c * L
        conv = cw[0:1, :] * xbuf[bb, r0 + CONV_PAD - 3:r0 + CONV_PAD - 3 + L, :]
        for j in range(1, ML_CONV):
            s0 = r0 + CONV_PAD - 3 + j
            conv = conv + cw[j:j + 1, :] * xbuf[bb, s0:s0 + L, :]
        act = conv * jax.nn.sigmoid(conv)

        gates = g_ref[bb, r0:r0 + L, :] + bg
        lf = _log_sigmoid(gates)
        cum = jnp.zeros((L, LANES), F32)
        for _ in range(3):
            piece = lf.astype(BF16)
            cum = cum + jnp.dot(tri, piece, preferred_element_type=F32)
            lf = lf - piece.astype(F32)
        gates_t = gates.T
        cum_t = cum.T
        for h in range(ML_HEADS):
            b_row = cum_t[ML_HEADS + h:ML_HEADS + h + 1, :]
            i_row = gates_t[h:h + 1, :]
            a_col = gates[:, h:h + 1] - cum[:, ML_HEADS + h:ML_HEADS + h + 1]
            st = bb * ML_HEADS + h
            m_prev = m_st[st]
            dm = jnp.where(causal_t, a_col + b_row, -jnp.inf)
            m_inter = b_row + m_prev
            m_t = jnp.maximum(jnp.max(dm, axis=0, keepdims=True), m_inter)
            w_intra = jnp.exp(dm - m_t)
            w_inter = jnp.exp(m_inter - m_t)

            qb = (act[:, h * ML_DQK:(h + 1) * ML_DQK] * scale).astype(BF16)
            kb = act[:, (ML_HEADS + h) * ML_DQK:(ML_HEADS + h + 1) * ML_DQK].astype(BF16)
            v_t = v_ref[bb, r0:r0 + L, h * ML_DV:(h + 1) * ML_DV].astype(F32).T
            p_t = lax.dot_general(kb, qb, nt_dims, preferred_element_type=F32) * w_intra
            c_old = c_st[st]
            n_old = n_st[st]
            num = jnp.dot(v_t.astype(BF16), p_t.astype(BF16), preferred_element_type=F32) + w_inter * (
                lax.dot_general(c_old.astype(BF16), qb, nt_dims, preferred_element_type=F32))
            qn = lax.dot_general(jnp.broadcast_to(n_old, (8, ML_DQK)).astype(BF16), qb, nt_dims,
                                 preferred_element_type=F32)[0:1, :]
            den = jnp.sum(p_t, axis=0, keepdims=True) + w_inter * qn
            hv = num / jnp.maximum(jnp.abs(den), jnp.exp(-m_t))

            m_new = m_t[:, L - 1:L]
            b_last = b_row[:, L - 1:L]
            wk = jnp.exp(b_last - b_row + i_row - m_new)
            decay = jnp.exp(b_last + m_prev - m_new)
            c_st[st] = decay * c_old + jnp.dot((v_t * wk).astype(BF16), kb, preferred_element_type=F32)
            n_st[st] = decay * n_old + jnp.dot(jnp.broadcast_to(wk, (8, L)).astype(BF16), kb,
                                               preferred_element_type=F32)[0:1, :]
            m_st[st] = m_new

            hn = (hv * lax.rsqrt(jnp.mean(hv * hv, axis=0, keepdims=True) + EPS)).T
            og = o_ref[bb, r0:r0 + L, h * ML_DV:(h + 1) * ML_DV].astype(F32)
            y_ref[bb, r0:r0 + L, h * ML_DV:(h + 1) * ML_DV] = (
                hn * hn_ref[:, h * ML_DV:(h + 1) * ML_DV] * jax.nn.sigmoid(og)).astype(BF16)

    for c in range(ML_BLOCK // L):
        for bb in range(ML_MB):
            chunk(bb, c)

    xbuf[:, 0:CONV_PAD, :] = xbuf[:, ML_BLOCK:ML_BLOCK + CONV_PAD, :]


def _mlstm(proj, small, conv_w, b_gates, head_norm, B, S):
    T = B * S
    ns = S // ML_BLOCK
    assert B % ML_MB == 0
    proj3 = proj.reshape(B, S, N_BIG)
    blk = lambda col: pl.BlockSpec((ML_MB, ML_BLOCK, D_MODEL), lambda b, s: (b, s, col))
    out = pl.pallas_call(
        _mlstm_body,
        grid=(B // ML_MB, ns),
        in_specs=[
            blk(COL_MLQK),
            blk(COL_MLV),
            blk(COL_MLO),
            pl.BlockSpec((ML_MB, ML_BLOCK, LANES), lambda b, s: (b, s, 0)),
            pl.BlockSpec((ML_CONV, D_MODEL), lambda b, s: (0, 0)),
            pl.BlockSpec((1, LANES), lambda b, s: (0, 0)),
            pl.BlockSpec((1, D_MODEL), lambda b, s: (0, 0)),
        ],
        out_specs=blk(0),
        out_shape=jax.ShapeDtypeStruct((B, S, D_MODEL), BF16),
        scratch_shapes=[
            pltpu.VMEM((ML_MB, ML_BLOCK + CONV_PAD, D_MODEL), F32),
            pltpu.VMEM((ML_MB * ML_HEADS, ML_DV, ML_DQK), F32),
            pltpu.VMEM((ML_MB * ML_HEADS, 1, ML_DQK), F32),
            pltpu.VMEM((ML_MB * ML_HEADS, 1, 1), F32),
        ],
        compiler_params=_cparams(("parallel", "arbitrary")),
        name="mlstm",
    )(proj3, proj3, proj3, small.reshape(B, S, LANES), conv_w, b_gates, head_norm)
    return out.reshape(T, D_MODEL)


FX_T = 512
FX_HP = 2
FX_VR = FX_DH + 16
LOG2E = 1.4426950408889634
N_PIECES = 3
FX_GATE_T = 128


def _fox_gate_body(g_ref, b_ref, o_ref):
    S = g_ref.shape[0]
    row = lax.broadcasted_iota(I32, (FX_GATE_T, FX_GATE_T), 0)
    col = lax.broadcasted_iota(I32, (FX_GATE_T, FX_GATE_T), 1)
    tri = (row >= col).astype(BF16)
    carry = jnp.zeros((1, LANES), F32)
    for blk in range(S // FX_GATE_T):
        rows = slice(blk * FX_GATE_T, (blk + 1) * FX_GATE_T)
        lf = _log_sigmoid(g_ref[rows, :] + b_ref[...])
        cum = carry
        for _ in range(N_PIECES):
            piece = lf.astype(BF16)
            cum = cum + jnp.dot(tri, piece, preferred_element_type=F32)
            lf = lf - piece.astype(F32)
        carry = cum[FX_GATE_T - 1:FX_GATE_T, :]
        o_ref[rows, :] = cum * (-LOG2E)


def _fox_gate(small, b_fx, B, S):
    return pl.pallas_call(
        _fox_gate_body,
        grid=(B,),
        in_specs=[
            pl.BlockSpec((S, LANES), lambda b: (b, 0)),
            pl.BlockSpec((1, LANES), lambda b: (0, 0)),
        ],
        out_specs=pl.BlockSpec((S, LANES), lambda b: (b, 0)),
        out_shape=jax.ShapeDtypeStruct((B * S, LANES), F32),
        compiler_params=_cparams(("parallel",)),
        name="fox_gate",
    )(small, b_fx)


def _fox_attn_body(q_ref, k_ref, v_ref, c_ref, o_ref, kx_ref, vt_ref, m_ref, acc_ref, s_ref):
    S = k_ref.shape[0]
    nq = S // FX_T

    c = c_ref[...]
    hi = c.astype(BF16)
    r1 = c - hi.astype(F32)
    mid = r1.astype(BF16)
    lo = (r1 - mid.astype(F32)).astype(BF16)
    sel_row = lax.broadcasted_iota(I32, (LANES, LANES), 0)
    sel_col = lax.broadcasted_iota(I32, (LANES, LANES), 1)
    ones_rows = (lax.broadcasted_iota(I32, (FX_VR - FX_DH, FX_T), 0) == 0).astype(BF16)
    head_slices = [slice(hh * FX_DH, (hh + 1) * FX_DH) for hh in range(FX_HP)]
    for hh, sl in enumerate(head_slices):
        lane = 2 * ML_HEADS + pl.program_id(1) * FX_HP + hh
        pieces = None
        for p, part in enumerate((hi, mid, lo)):
            pick = jnp.logical_and(sel_row == lane, sel_col == p).astype(BF16)
            t = jnp.dot(part, pick, preferred_element_type=F32)
            pieces = t if pieces is None else pieces + t
        kx_ref[hh, :, 0:FX_DH] = k_ref[:, sl]
        kx_ref[hh, :, FX_DH:2 * FX_DH] = pieces.astype(BF16)
        for j in range(nq):
            vt = v_ref[j * FX_T:(j + 1) * FX_T, sl].astype(F32).T.astype(BF16)
            vt_ref[hh, j] = jnp.concatenate([vt, ones_rows], axis=0)

    piece_rows = (lax.broadcasted_iota(I32, (FX_DH, FX_T), 0) < N_PIECES).astype(BF16)

    def start(i):
        q_x = []
        for sl in head_slices:
            q_t = (q_ref[i * FX_T:(i + 1) * FX_T, sl].astype(F32) * (FX_DH ** -0.5 * LOG2E)).T.astype(BF16)
            q_x.append(jnp.concatenate([q_t, piece_rows], axis=0))
        m_ref[i % 2] = jnp.full(m_ref.shape[1:], -jnp.inf, F32)
        acc_ref[i % 2] = jnp.zeros(acc_ref.shape[1:], F32)
        return q_x

    def key_rows(j):
        return pl.ds(j * FX_T, FX_T) if isinstance(j, int) else pl.ds(pl.multiple_of(j * FX_T, FX_T), FX_T)

    def scores(q_x, j, slot):
        for hh in range(FX_HP):
            s_ref[slot, hh] = jnp.dot(kx_ref[hh, key_rows(j), :], q_x[hh], preferred_element_type=F32)

    def consume(par, j, slot, masked):
        for hh in range(FX_HP):
            s = s_ref[slot, hh]
            if masked:
                key = lax.broadcasted_iota(I32, (FX_T, FX_T), 0)
                qry = lax.broadcasted_iota(I32, (FX_T, FX_T), 1)
                s = jnp.where(qry >= key, s, -jnp.inf)
            m_old = m_ref[par, hh]
            m_new = jnp.maximum(m_old, jnp.max(s, axis=0, keepdims=True))
            p = jnp.exp2(s - m_new).astype(BF16)
            acc_ref[par, hh] = jnp.exp2(m_old - m_new) * acc_ref[par, hh] + jnp.dot(
                vt_ref[hh, j], p, preferred_element_type=F32)
            m_ref[par, hh] = m_new

    def finish(i, slot):
        consume(i % 2, i, slot, True)
        for hh, sl in enumerate(head_slices):
            acc = acc_ref[i % 2, hh]
            o_ref[i * FX_T:(i + 1) * FX_T, sl] = (acc[0:FX_DH, :] / acc[FX_DH:FX_DH + 1, :]).T.astype(BF16)

    diag_slot = 0
    for i in range(nq):
        q_x = start(i)
        first = 0 if i == 0 else 1 - diag_slot
        scores(q_x, 0, first)
        if i > 0:
            finish(i - 1, diag_slot)

        def pair(jj, carry, q_x=q_x, first=first, par=i % 2):
            j = 2 * jj
            scores(q_x, j + 1, 1 - first)
            consume(par, j, first, False)
            scores(q_x, j + 2, first)
            consume(par, j + 1, 1 - first, False)
            return carry

        if i >= 2:
            lax.fori_loop(0, i // 2, pair, 0)
        if i % 2 == 1:
            scores(q_x, i, 1 - first)
            consume(i % 2, i - 1, first, False)
            diag_slot = 1 - first
        else:
            diag_slot = first
    finish(nq - 1, diag_slot)


def _fox_attn(proj, c_neg, B, S):
    T = B * S
    nq = S // FX_T
    wide = FX_HP * FX_DH
    cq = COL_FXQ * (D_MODEL // wide)
    ck = COL_FXK * (D_MODEL // wide)
    cv = COL_FXV * (D_MODEL // wide)
    proj3 = proj.reshape(B, S, N_BIG)
    out = pl.pallas_call(
        _fox_attn_body,
        grid=(B, FX_HEADS // FX_HP),
        in_specs=[
            pl.BlockSpec((None, S, wide), lambda b, h: (b, 0, cq + h)),
            pl.BlockSpec((None, S, wide), lambda b, h: (b, 0, ck + h)),
            pl.BlockSpec((None, S, wide), lambda b, h: (b, 0, cv + h)),
            pl.BlockSpec((None, S, LANES), lambda b, h: (b, 0, 0)),
        ],
        out_specs=pl.BlockSpec((None, S, wide), lambda b, h: (b, 0, h)),
        out_shape=jax.ShapeDtypeStruct((B, S, D_MODEL), BF16),
        scratch_shapes=[
            pltpu.VMEM((FX_HP, S, 2 * FX_DH), BF16),
            pltpu.VMEM((FX_HP, nq, FX_VR, FX_T), BF16),
            pltpu.VMEM((2, FX_HP, 1, FX_T), F32),
            pltpu.VMEM((2, FX_HP, FX_VR, FX_T), F32),
            pltpu.VMEM((2, FX_HP, FX_T, FX_T), F32),
        ],
        compiler_params=_cparams(("parallel", "parallel")),
        name="fox_attn",
    )(proj3, proj3, proj3, c_neg.reshape(B, S, LANES))
    return out.reshape(T, D_MODEL)


def _memkv_body(x_ref, g_ref, w_ref, o_ref):
    hb = _rms(x_ref[...], g_ref[...]).astype(BF16)
    o_ref[...] = jnp.dot(hb, w_ref[...], preferred_element_type=F32).astype(BF16)


def _memkv(mem2d, g, w_kv):
    R = mem2d.shape[0]
    tm = min(512, R)
    N = w_kv.shape[1]
    return pl.pallas_call(
        _memkv_body,
        grid=(R // tm,),
        in_specs=[
            pl.BlockSpec((tm, D_MODEL), lambda i: (i, 0)),
            pl.BlockSpec((1, D_MODEL), lambda i: (0, 0)),
            pl.BlockSpec((D_MODEL, N), lambda i: (0, 0)),
        ],
        out_specs=pl.BlockSpec((tm, N), lambda i: (i, 0)),
        out_shape=jax.ShapeDtypeStruct((R, N), BF16),
        compiler_params=_cparams(("parallel",)),
        name="memkv",
    )(mem2d, g, w_kv)


CA_TQ = 2048


def _memattn_body(q_ref, k_ref, v_ref, o_ref):
    scale = CA_DH ** -0.5
    for h in range(CA_HEADS):
        sl = slice(h * CA_DH, (h + 1) * CA_DH)
        s = lax.dot_general(q_ref[:, sl], k_ref[:, sl], (((1,), (1,)), ((), ())),
                            preferred_element_type=F32) * scale
        p = jnp.exp(s - jnp.max(s, axis=-1, keepdims=True))
        l = jnp.sum(p, axis=-1, keepdims=True)
        o = jnp.dot(p.astype(BF16), v_ref[:, sl], preferred_element_type=F32) / l
        o_ref[:, sl] = o.astype(BF16)


def _memattn(proj, kv, B, S, M):
    T = B * S
    nq = S // CA_TQ
    kv3 = kv.reshape(B, M, 2 * D_MODEL)
    return pl.pallas_call(
        _memattn_body,
        grid=(B, nq),
        in_specs=[
            pl.BlockSpec((CA_TQ, D_MODEL), lambda b, i: (b * nq + i, COL_CAQ)),
            pl.BlockSpec((None, M, D_MODEL), lambda b, i: (b, 0, 0)),
            pl.BlockSpec((None, M, D_MODEL), lambda b, i: (b, 0, 1)),
        ],
        out_specs=pl.BlockSpec((CA_TQ, D_MODEL), lambda b, i: (b * nq + i, 0)),
        out_shape=jax.ShapeDtypeStruct((T, D_MODEL), BF16),
        compiler_params=_cparams(("parallel", "arbitrary")),
        name="memattn",
    )(proj, kv3, kv3)


MERGE_TM = 512
MOE_PARTS = 2


def _merge_body(y0_ref, y1_ref, y2_ref, g0_ref, g1_ref, g2_ref, x_ref, wb_ref, wo_ref, gn_ref, wr_ref, br_ref,
                o_ref, hp_ref, ri_ref, rw_ref, cnt_ref, carry_ref):
    merged = None
    for n, (y_ref, g_ref) in enumerate(((y0_ref, g0_ref), (y1_ref, g1_ref), (y2_ref, g2_ref))):
        p = jnp.dot(y_ref[...], wb_ref[n], preferred_element_type=F32)
        t = jax.nn.sigmoid(g_ref[...].astype(F32)) * p
        merged = t if merged is None else merged + t
    x2 = x_ref[...] + jnp.dot(merged.astype(BF16), wo_ref[...], preferred_element_type=F32)
    o_ref[...] = x2
    _route(x2, gn_ref, wr_ref, br_ref, hp_ref, ri_ref, rw_ref, cnt_ref, carry_ref)


def _merge(y_ml, y_fx, y_ca, proj, x2d, w_branch, w_out, g_moe, w_router, b_router, part):
    T = x2d.shape[0] // MOE_PARTS
    tm = MERGE_TM
    off = part * (T // tm)
    src = lambda i: (off + i, 0)
    row = lambda i: (i, 0)
    const = lambda i: (0, 0)
    return pl.pallas_call(
        _merge_body,
        grid=(T // tm,),
        in_specs=[
            pl.BlockSpec((tm, D_MODEL), src),
            pl.BlockSpec((tm, D_MODEL), src),
            pl.BlockSpec((tm, D_MODEL), src),
            pl.BlockSpec((tm, D_MODEL), lambda i: (off + i, COL_GATE0)),
            pl.BlockSpec((tm, D_MODEL), lambda i: (off + i, COL_GATE0 + 1)),
            pl.BlockSpec((tm, D_MODEL), lambda i: (off + i, COL_GATE0 + 2)),
            pl.BlockSpec((tm, D_MODEL), src),
            pl.BlockSpec((3, D_MODEL, D_MODEL), lambda i: (0, 0, 0)),
            pl.BlockSpec((D_MODEL, D_MODEL), const),
            pl.BlockSpec((1, D_MODEL), const),
            pl.BlockSpec((N_EXPERTS, D_MODEL), const),
            pl.BlockSpec((N_EXPERTS, 1), const),
        ],
        out_specs=[
            pl.BlockSpec((tm, D_MODEL), row),
            pl.BlockSpec((tm, HALF), row),
            pl.BlockSpec((2 * TOP_K, tm), lambda i: (0, i)),
            pl.BlockSpec((2 * TOP_K, tm), lambda i: (0, i)),
            pl.BlockSpec((N_EXPERTS, 1), const),
        ],
        out_shape=[
            jax.ShapeDtypeStruct((T, D_MODEL), F32),
            jax.ShapeDtypeStruct((T, HALF), I32),
            jax.ShapeDtypeStruct((2 * TOP_K, T), I32),
            jax.ShapeDtypeStruct((2 * TOP_K, T), F32),
            jax.ShapeDtypeStruct((N_EXPERTS, 1), F32),
        ],
        scratch_shapes=[pltpu.VMEM((N_EXPERTS, 1), F32)],
        compiler_params=_cparams(("arbitrary",)),
        name="merge_router",
    )(y_ml, y_fx, y_ca, proj, proj, proj, x2d, w_branch, w_out, g_moe, w_router, b_router)


def _route(x2, g_ref, wr_ref, br_ref, hp_ref, ri_ref, rw_ref, cnt_ref, carry_ref):
    tm = MERGE_TM

    @pl.when(pl.program_id(0) == 0)
    def _():
        carry_ref[...] = jnp.zeros_like(carry_ref)

    h = _rms(x2, g_ref[...])
    hp_ref[...] = _pack_rows(h)
    logits = lax.dot_general(wr_ref[...], h.astype(BF16), (((1,), (1,)), ((), ())),
                             preferred_element_type=F32) + br_ref[...]
    eid = lax.broadcasted_iota(I32, (N_EXPERTS, tm), 0).astype(F32)

    work = logits
    onehot_sum = jnp.zeros((N_EXPERTS, tm), F32)
    vals, sels, idxs = [], [], []
    for _ in range(TOP_K):
        mx = jnp.max(work, axis=0, keepdims=True)
        idx = jnp.min(jnp.where(work == mx, eid, float(N_EXPERTS)), axis=0, keepdims=True)
        sel = eid == idx
        onehot_sum = onehot_sum + sel.astype(F32)
        work = jnp.where(sel, -jnp.inf, work)
        vals.append(mx)
        sels.append(sel)
        idxs.append(idx)
    exps = [jnp.exp(v - vals[0]) for v in vals]
    total = exps[0] + exps[1] + exps[2] + exps[3]

    earlier = (lax.broadcasted_iota(I32, (tm, tm), 0) < lax.broadcasted_iota(I32, (tm, tm), 1)).astype(BF16)
    before = jnp.dot(onehot_sum.astype(BF16), earlier, preferred_element_type=F32) + carry_ref[...]
    carry_ref[...] = carry_ref[...] + jnp.sum(onehot_sum, axis=1, keepdims=True)
    cnt_ref[...] = carry_ref[...]

    out_row = lax.broadcasted_iota(I32, (2 * TOP_K, tm), 0)
    ri = jnp.zeros((2 * TOP_K, tm), I32)
    rw = jnp.zeros((2 * TOP_K, tm), F32)
    for k in range(TOP_K):
        rank = jnp.sum(jnp.where(sels[k], before, 0.0), axis=0, keepdims=True)
        ri = jnp.where(out_row == k, idxs[k].astype(I32), ri)
        ri = jnp.where(out_row == TOP_K + k, rank.astype(I32), ri)
        rw = jnp.where(out_row == k, exps[k] / total, rw)
    ri_ref[...] = ri
    rw_ref[...] = rw


EXPERT_TM = 512
SC_CORES = 2
SC_SUBCORES = 16
SC_WORKERS = SC_CORES * SC_SUBCORES
SC_CHUNK = 64
PAD_SLOTS = N_EXPERTS * EXPERT_TM


def _sc_mesh():
    return plsc.VectorSubcoreMesh(core_axis_name="c", subcore_axis_name="s")


def _sc_worker():
    return lax.axis_index("s") * SC_CORES + lax.axis_index("c")


def _scatter_indices(dest):
    T = dest.shape[1]
    n_ch = T // (SC_WORKERS * SC_CHUNK)
    idx = dest.reshape(TOP_K, SC_WORKERS, n_ch, SC_CHUNK).transpose(1, 2, 0, 3)
    return idx.reshape(SC_WORKERS, n_ch * TOP_K, SC_CHUNK)


def _sc_dispatch(hp, idx, pad_idx, n_rows):
    T = hp.shape[0]
    per_w = T // SC_WORKERS
    n_ch = per_w // SC_CHUNK
    n_pc = PAD_SLOTS // (SC_WORKERS * SC_CHUNK)
    assert per_w % SC_CHUNK == 0 and n_ch >= 2 and n_ch % 2 == 0
    pidx = pad_idx.reshape(SC_WORKERS, n_pc, SC_CHUNK)
    zeros = jnp.zeros((SC_CHUNK, HALF), I32)

    @functools.partial(
        pl.kernel, mesh=_sc_mesh(),
        out_type=jax.ShapeDtypeStruct((n_rows, HALF), I32),
        scratch_types=[
            pltpu.VMEM((n_ch * TOP_K, SC_CHUNK), I32),
            pltpu.VMEM((n_pc, SC_CHUNK), I32),
            pltpu.VMEM((2, SC_CHUNK, HALF), I32),
            pltpu.SemaphoreType.DMA((2,)),
            pltpu.SemaphoreType.DMA((2,)),
        ],
        name="sc_dispatch",
    )
    def k(hp_hbm, idx_hbm, pidx_hbm, zeros_hbm, xs_hbm, idx_v, pidx_v, rows_v, lsem, ssem):
        wid = _sc_worker()
        base = wid * per_w
        pltpu.sync_copy(idx_hbm.at[wid], idx_v)
        pltpu.sync_copy(pidx_hbm.at[wid], pidx_v)

        pltpu.sync_copy(zeros_hbm, rows_v.at[0])
        for p in range(n_pc):
            pltpu.make_async_copy(rows_v.at[0], xs_hbm.at[pidx_v.at[p]], ssem.at[0]).start()
        for p in range(n_pc):
            pltpu.make_async_copy(rows_v.at[0], xs_hbm.at[pidx_v.at[p]], ssem.at[0]).wait()

        def load(i, slot):
            return pltpu.make_async_copy(hp_hbm.at[pl.ds(base + i * SC_CHUNK, SC_CHUNK)], rows_v.at[slot],
                                         lsem.at[slot])

        def scatter(i, kk, slot):
            return pltpu.make_async_copy(rows_v.at[slot], xs_hbm.at[idx_v.at[i * TOP_K + kk]], ssem.at[slot])

        load(0, 0).start()

        def body(i2, carry):
            for slot in range(2):
                i = i2 * 2 + slot
                nxt = 1 - slot

                @pl.when(i + 1 < n_ch)
                def _():
                    @pl.when(i >= 1)
                    def _():
                        for kk in range(TOP_K):
                            scatter(i - 1, kk, nxt).wait()
                    load(i + 1, nxt).start()

                load(i, slot).wait()
                for kk in range(TOP_K):
                    scatter(i, kk, slot).start()
            return carry

        lax.fori_loop(0, n_ch // 2, body, 0)
        for kk in range(TOP_K):
            scatter(n_ch - 2, kk, 0).wait()
            scatter(n_ch - 1, kk, 1).wait()

    return k(hp, idx, pidx, zeros)


def _sc_gather(table, idx):
    n = idx.shape[0]
    per_w = n // SC_WORKERS
    n_ch = per_w // SC_CHUNK
    assert per_w % SC_CHUNK == 0 and n_ch >= 2 and n_ch % 2 == 0

    @functools.partial(
        pl.kernel, mesh=_sc_mesh(),
        out_type=jax.ShapeDtypeStruct((n, HALF), I32),
        scratch_types=[
            pltpu.VMEM((n_ch, SC_CHUNK), I32),
            pltpu.VMEM((2, SC_CHUNK, HALF), I32),
            pltpu.SemaphoreType.DMA((2,)),
            pltpu.SemaphoreType.DMA((2,)),
        ],
        name="sc_gather",
    )
    def k(table_hbm, idx_hbm, out_hbm, idx_v, rows_v, gsem, wsem):
        wid = _sc_worker()
        base = wid * per_w
        pltpu.sync_copy(idx_hbm.at[wid], idx_v)

        def gather(i, slot):
            return pltpu.make_async_copy(table_hbm.at[idx_v.at[i]], rows_v.at[slot], gsem.at[slot])

        def writeback(i, slot):
            return pltpu.make_async_copy(rows_v.at[slot], out_hbm.at[pl.ds(base + i * SC_CHUNK, SC_CHUNK)],
                                         wsem.at[slot])

        gather(0, 0).start()

        def body(i2, carry):
            for slot in range(2):
                i = i2 * 2 + slot
                nxt = 1 - slot

                @pl.when(i + 1 < n_ch)
                def _():
                    @pl.when(i >= 1)
                    def _():
                        writeback(i - 1, nxt).wait()
                    gather(i + 1, nxt).start()

                gather(i, slot).wait()
                writeback(i, slot).start()
            return carry

        lax.fori_loop(0, n_ch // 2, body, 0)
        writeback(n_ch - 2, 0).wait()
        writeback(n_ch - 1, 1).wait()

    return k(table, idx.reshape(SC_WORKERS, n_ch, SC_CHUNK))


FF_CHUNK = 1024


def _expert_body(te_ref, nv_ref, nx_ref, x_ref, w1_hbm, b1_ref, w2_hbm, b2_ref, y_ref,
                 w1_stage, w2_stage, w1_ref, w2_ref, sem):
    i = pl.program_id(0)
    valid = i < nv_ref[0]
    first = jnp.logical_or(i == 0, te_ref[i] != te_ref[jnp.maximum(i - 1, 0)])

    def fetch(e):
        return (pltpu.make_async_copy(w1_hbm.at[e], w1_stage, sem.at[0]),
                pltpu.make_async_copy(w2_hbm.at[e], w2_stage, sem.at[1]))

    @pl.when(i == 0)
    def _():
        for copy in fetch(te_ref[0]):
            copy.start()

    @pl.when(jnp.logical_and(valid, first))
    def _():
        for copy in fetch(te_ref[i]):
            copy.wait()
        w1_ref[...] = w1_stage[...].astype(BF16)
        w2_ref[...] = w2_stage[...].astype(BF16)

        @pl.when(nx_ref[i] < N_EXPERTS)
        def _():
            for copy in fetch(nx_ref[i]):
                copy.start()

    @pl.when(valid)
    def _():
        lo, hi = _unpack_rows(x_ref[...])
        xb = jnp.concatenate([lo.astype(BF16), hi.astype(BF16)], axis=-1)
        acc = jnp.zeros((EXPERT_TM, D_MODEL), F32) + b2_ref[...]
        for c in range(D_FF // FF_CHUNK):
            def up(off):
                cs = slice(off + c * FF_CHUNK, off + (c + 1) * FF_CHUNK)
                return jnp.dot(xb, w1_ref[:, cs], preferred_element_type=F32) + b1_ref[:, cs]
            g = jnp.minimum(up(0), SWIGLU_LIMIT)
            lin = jnp.clip(up(D_FF), -SWIGLU_LIMIT, SWIGLU_LIMIT)
            a = g * jax.nn.sigmoid(SWIGLU_ALPHA * g) * (lin + 1.0)
            acc = acc + jnp.dot(a.astype(BF16), w2_ref[c * FF_CHUNK:(c + 1) * FF_CHUNK, :],
                                preferred_element_type=F32)
        y_ref[...] = _pack_rows(acc)


def _experts(tile_expert, n_valid, next_expert, xs, w1, b1, w2, b2):
    n_rows = xs.shape[0]
    tm = EXPERT_TM
    n_tiles = n_rows // tm
    row = lambda i, te, nv, nx: (jnp.minimum(i, nv[0] - 1), 0)
    grid_spec = pltpu.PrefetchScalarGridSpec(
        num_scalar_prefetch=3,
        grid=(n_tiles,),
        in_specs=[
            pl.BlockSpec((tm, HALF), row),
            pl.BlockSpec(memory_space=pl.ANY),
            pl.BlockSpec((None, 1, 2 * D_FF), lambda i, te, nv, nx: (te[i], 0, 0)),
            pl.BlockSpec(memory_space=pl.ANY),
            pl.BlockSpec((None, 1, D_MODEL), lambda i, te, nv, nx: (te[i], 0, 0)),
        ],
        out_specs=pl.BlockSpec((tm, HALF), row),
        scratch_shapes=[pltpu.VMEM((D_MODEL, 2 * D_FF), F32), pltpu.VMEM((D_FF, D_MODEL), F32),
                        pltpu.VMEM((D_MODEL, 2 * D_FF), BF16), pltpu.VMEM((D_FF, D_MODEL), BF16),
                        pltpu.SemaphoreType.DMA((2,))],
    )
    return pl.pallas_call(
        _expert_body,
        grid_spec=grid_spec,
        out_shape=jax.ShapeDtypeStruct((n_rows, HALF), I32),
        compiler_params=_cparams(("arbitrary",)),
        name="experts",
    )(tile_expert, n_valid, next_expert, xs, w1, b1, w2, b2)


COMBINE_TM = 1024


def _combine_body(y0_ref, y1_ref, y2_ref, y3_ref, rw_ref, x_ref, g_ref, *rest):
    o_ref = rest[-1]
    acc = x_ref[...]
    rw = jnp.concatenate([rw_ref[...], jnp.zeros((LANES - 2 * TOP_K, COMBINE_TM), F32)], axis=0).T
    for k, y_ref in enumerate((y0_ref, y1_ref, y2_ref, y3_ref)):
        lo, hi = _unpack_rows(y_ref[...])
        acc = acc + rw[:, k:k + 1] * jnp.concatenate([lo, hi], axis=-1)
    o_ref[...] = _rms(acc, g_ref[...])


def _combine(yg, rw, x2, g, part, out_prev):
    T = x2.shape[0]
    tm = COMBINE_TM
    nt = T // tm
    in_specs = [
        pl.BlockSpec((tm, HALF), lambda i: (i, 0)),
        pl.BlockSpec((tm, HALF), lambda i: (nt + i, 0)),
        pl.BlockSpec((tm, HALF), lambda i: (2 * nt + i, 0)),
        pl.BlockSpec((tm, HALF), lambda i: (3 * nt + i, 0)),
        pl.BlockSpec((2 * TOP_K, tm), lambda i: (0, i)),
        pl.BlockSpec((tm, D_MODEL), lambda i: (i, 0)),
        pl.BlockSpec((1, D_MODEL), lambda i: (0, 0)),
    ]
    args = [yg, yg, yg, yg, rw, x2, g]
    aliases = {}
    if out_prev is not None:
        in_specs.append(pl.BlockSpec(memory_space=pl.ANY))
        args.append(out_prev)
        aliases = {len(args) - 1: 0}
    return pl.pallas_call(
        _combine_body,
        grid=(nt,),
        in_specs=in_specs,
        out_specs=pl.BlockSpec((tm, D_MODEL), lambda i: (part * nt + i, 0)),
        out_shape=jax.ShapeDtypeStruct((T * MOE_PARTS, D_MODEL), F32),
        input_output_aliases=aliases,
        compiler_params=_cparams(("parallel",)),
        name="combine",
    )(*args)


def _pad_lanes(v):
    v = v.reshape(1, -1).astype(F32)
    return jnp.pad(v, ((0, 0), (0, LANES - v.shape[1])))


def _layer(x2d, mem2d, B, S, M, norm_mix, w_in, b_ml_gates, conv_ml, ml_head_norm, b_fx_gate, norm_mem,
           w_mem_kv, w_branch, w_out, norm_moe, w_router, b_router, w_exp_in, b_exp_in, w_exp_out,
           b_exp_out, norm_out):
    T = B * S
    w16 = w_in.astype(BF16)
    w_big = jnp.concatenate([w16[:, 0:2048], w16[:, 2056:3080], w16[:, 3080:6152], w16[:, 6160:7184],
                             w16[:, 7184:10256]], axis=1)
    w_small = jnp.concatenate([w16[:, 2048:2056], w16[:, 6152:6160]], axis=1)
    w_small = jnp.pad(w_small, ((0, 0), (0, LANES - w_small.shape[1])))
    row = lambda v: v.reshape(1, -1).astype(F32)

    proj, small = _inproj(x2d, row(norm_mix), w_big, w_small)

    y_ml = _mlstm(proj, small, conv_ml.astype(F32), _pad_lanes(b_ml_gates), row(ml_head_norm), B, S)

    b_fx = jnp.pad(b_fx_gate.reshape(1, -1).astype(F32), ((0, 0), (2 * ML_HEADS, LANES - 2 * ML_HEADS - FX_HEADS)))
    y_fx = _fox_attn(proj, _fox_gate(small, b_fx, B, S), B, S)

    kv = _memkv(mem2d, row(norm_mem), w_mem_kv.astype(BF16))
    y_ca = _memattn(proj, kv, B, S, M)

    w_r = w_router.T.astype(BF16)
    moe_weights = (w_exp_in.astype(F32), b_exp_in.reshape(N_EXPERTS, 1, -1).astype(F32), w_exp_out.astype(F32),
                   b_exp_out.reshape(N_EXPERTS, 1, -1).astype(F32))
    staged = []
    for part in range(MOE_PARTS):
        x2, hp, ri, rw, cnt = _merge(y_ml, y_fx, y_ca, proj, x2d, w_branch.astype(BF16), w_out.astype(BF16),
                                     row(norm_moe), w_r, b_router.reshape(N_EXPERTS, 1).astype(F32), part)
        scatter_idx, dest, pad_idx, tiles, n_rows = _moe_plan(ri, cnt)
        staged.append((x2, rw, dest, tiles, _sc_dispatch(hp, scatter_idx, pad_idx, n_rows)))
    gathered = [_sc_gather(_experts(tile_e, n_valid, next_e, xs, *moe_weights), dest.reshape(-1))
                for _, _, dest, (tile_e, n_valid, next_e), xs in staged]
    out = None
    for part, ((x2, rw, *_), yg) in enumerate(zip(staged, gathered)):
        out = _combine(yg, rw, x2, row(norm_out), part, out)
    return out


def _moe_plan(ri, cnt):
    T = ri.shape[1]
    tm = EXPERT_TM
    n_tiles = (T * TOP_K) // tm + N_EXPERTS
    counts = cnt[:, 0].astype(I32)
    padded = ((counts + tm - 1) // tm) * tm
    gend = jnp.cumsum(padded)
    gstart = gend - padded
    expert_ids = jnp.arange(N_EXPERTS, dtype=I32)
    start_of = jnp.sum(jnp.where(ri[0:TOP_K, :, None] == expert_ids, gstart, 0), axis=-1)
    dest = start_of + ri[TOP_K:2 * TOP_K, :]
    n_valid = gend[-1] // tm
    tile_ids = jnp.arange(n_tiles, dtype=I32)
    last_tile = jnp.minimum(tile_ids, n_valid - 1)
    tile_e = jnp.minimum(jnp.sum((gend[None, :] <= last_tile[:, None] * tm).astype(I32), axis=1), N_EXPERTS - 1)

    slot = jnp.arange(tm, dtype=I32)
    spare = n_tiles * tm + slot % SC_CHUNK
    pad_idx = jnp.where(slot[None, :] < (padded - counts)[:, None], (gstart + counts)[:, None] + slot[None, :],
                        spare[None, :]).reshape(-1)

    later = jnp.logical_and(expert_ids[None, :] > expert_ids[:, None], (padded > 0)[None, :])
    next_group = jnp.min(jnp.where(later, expert_ids[None, :], N_EXPERTS), axis=1)
    tiles = (tile_e.astype(I32), n_valid.reshape(1).astype(I32), next_group[tile_e].astype(I32))
    return _scatter_indices(dest), dest, pad_idx, tiles, n_tiles * tm + SC_CHUNK


def kernel(x, mem, norm_mix, w_in, b_ml_gates, conv_ml, ml_head_norm, b_fx_gate, norm_mem, w_mem_kv, w_branch,
           w_out, norm_moe, w_router, b_router, w_exp_in, b_exp_in, w_exp_out, b_exp_out, norm_final):
    B, S, D = x.shape
    M = mem.shape[1]
    depth = norm_mix.shape[0]
    assert depth == 1, "the combine kernel fuses the final norm, so exactly one layer is supported"
    assert D == D_MODEL and S % ML_BLOCK == 0 and S % FX_T == 0 and S % CA_TQ == 0
    out = _layer(x.reshape(B * S, D), mem.reshape(B * M, D), B, S, M, norm_mix[0], w_in[0], b_ml_gates[0],
                 conv_ml[0], ml_head_norm[0], b_fx_gate[0], norm_mem[0], w_mem_kv[0], w_branch[0], w_out[0],
                 norm_moe[0], w_router[0], b_router[0], w_exp_in[0], b_exp_in[0], w_exp_out[0], b_exp_out[0],
                 norm_final)
    return out.reshape(B, S, D)
```

```python
import functools

import jax
import jax.numpy as jnp
from jax import lax
from jax.experimental import pallas as pl
from jax.experimental.pallas import tpu as pltpu
from jax.experimental.pallas import tpu_sc as plsc

F32 = jnp.float32
BF16 = jnp.bfloat16
I32 = jnp.int32

D_MODEL = 1024
ML_HEADS = 4
ML_DQK = 128
ML_DV = 256
ML_CONV = 4
FX_HEADS = 8
FX_DH = 128
CA_HEADS = 4
CA_DH = 256
N_EXPERTS = 32
TOP_K = 4
D_FF = D_MODEL
SWIGLU_LIMIT = 7.0
SWIGLU_ALPHA = 1.702
EPS = 1e-5
LANES = 128
HALF = D_MODEL // 2
HI_MASK = -65536

COL_MLQK, COL_MLV, COL_MLO, COL_FXQ, COL_FXK, COL_FXV, COL_CAQ, COL_GATE0 = 0, 1, 2, 3, 4, 5, 6, 7
N_BIG = 10 * D_MODEL

VMEM_LIMIT = 56 * 1024 * 1024


def _cparams(sem):
    return pltpu.CompilerParams(dimension_semantics=sem, vmem_limit_bytes=VMEM_LIMIT)


def _rms(x, g):
    return x * lax.rsqrt(jnp.mean(x * x, axis=-1, keepdims=True) + EPS) * g


def _log_sigmoid(x):
    return jnp.minimum(x, 0.0) - jnp.log1p(jnp.exp(-jnp.abs(x)))


def _pack_rows(y):
    bits = lax.bitcast_convert_type(y.astype(BF16).astype(F32), I32)
    return lax.shift_right_logical(bits[:, :HALF], 16) | (bits[:, HALF:] & HI_MASK)


def _unpack_rows(w):
    lo = lax.bitcast_convert_type(lax.shift_left(w, 16), F32)
    hi = lax.bitcast_convert_type(w & HI_MASK, F32)
    return lo, hi


def _inproj_body(x_ref, g_ref, w_ref, ws_ref, o_ref, os_ref, h_ref):
    @pl.when(pl.program_id(1) == 0)
    def _():
        hb = _rms(x_ref[...], g_ref[...]).astype(BF16)
        h_ref[...] = hb
        os_ref[...] = jnp.dot(hb, ws_ref[...], preferred_element_type=F32)

    o_ref[...] = jnp.dot(h_ref[...], w_ref[...], preferred_element_type=F32).astype(BF16)


def _inproj(x2d, g, w_big, w_small):
    T = x2d.shape[0]
    tm = min(1024, T)
    tn = 2560
    return pl.pallas_call(
        _inproj_body,
        grid=(T // tm, N_BIG // tn),
        in_specs=[
            pl.BlockSpec((tm, D_MODEL), lambda i, j: (i, 0)),
            pl.BlockSpec((1, D_MODEL), lambda i, j: (0, 0)),
            pl.BlockSpec((D_MODEL, tn), lambda i, j: (0, j)),
            pl.BlockSpec((D_MODEL, LANES), lambda i, j: (0, 0)),
        ],
        out_specs=[
            pl.BlockSpec((tm, tn), lambda i, j: (i, j)),
            pl.BlockSpec((tm, LANES), lambda i, j: (i, 0)),
        ],
        out_shape=[
            jax.ShapeDtypeStruct((T, N_BIG), BF16),
            jax.ShapeDtypeStruct((T, LANES), F32),
        ],
        scratch_shapes=[pltpu.VMEM((tm, D_MODEL), BF16)],
        compiler_params=_cparams(("parallel", "arbitrary")),
        name="inproj",
    )(x2d, g, w_big, w_small)


ML_BLOCK = 1024
ML_MB = 1
ML_CHUNK = 128
CONV_PAD = 8


def _mlstm_body(qk_ref, v_ref, o_ref, g_ref, cw_ref, bg_ref, hn_ref, y_ref, xbuf, c_st, n_st, m_st):
    L = ML_CHUNK

    @pl.when(pl.program_id(1) == 0)
    def _():
        xbuf[:, 0:CONV_PAD, :] = jnp.zeros((ML_MB, CONV_PAD, D_MODEL), F32)
        c_st[...] = jnp.zeros_like(c_st)
        n_st[...] = jnp.zeros_like(n_st)
        m_st[...] = jnp.zeros_like(m_st)

    for bb in range(ML_MB):
        xbuf[bb, CONV_PAD:CONV_PAD + ML_BLOCK, :] = qk_ref[bb].astype(F32)
    cw = cw_ref[...]
    row = lax.broadcasted_iota(I32, (L, L), 0)
    col = lax.broadcasted_iota(I32, (L, L), 1)
    tri = (row >= col).astype(BF16)
    causal_t = col >= row
    bg = bg_ref[...]
    scale = ML_DQK ** -0.5
    nt_dims = (((1,), (1,)), ((), ()))

    def chunk(bb, c):
        r0 = c * L
        conv = cw[0:1, :] * xbuf[bb, r0 + CONV_PAD - 3:r0 + CONV_PAD - 3 + L, :]
        for j in range(1, ML_CONV):
            s0 = r0 + CONV_PAD - 3 + j
            conv = conv + cw[j:j + 1, :] * xbuf[bb, s0:s0 + L, :]
        act = conv * jax.nn.sigmoid(conv)

        gates = g_ref[bb, r0:r0 + L, :] + bg
        lf = _log_sigmoid(gates)
        cum = jnp.zeros((L, LANES), F32)
        for _ in range(3):
            piece = lf.astype(BF16)
            cum = cum + jnp.dot(tri, piece, preferred_element_type=F32)
            lf = lf - piece.astype(F32)
        gates_t = gates.T
        cum_t = cum.T
        for h in range(ML_HEADS):
            b_row = cum_t[ML_HEADS + h:ML_HEADS + h + 1, :]
            i_row = gates_t[h:h + 1, :]
            a_col = gates[:, h:h + 1] - cum[:, ML_HEADS + h:ML_HEADS + h + 1]
            st = bb * ML_HEADS + h
            m_prev = m_st[st]
            dm = jnp.where(causal_t, a_col + b_row, -jnp.inf)
            m_inter = b_row + m_prev
            m_t = jnp.maximum(jnp.max(dm, axis=0, keepdims=True), m_inter)
            w_intra = jnp.exp(dm - m_t)
            w_inter = jnp.exp(m_inter - m_t)

            qb = (act[:, h * ML_DQK:(h + 1) * ML_DQK] * scale).astype(BF16)
            kb = act[:, (ML_HEADS + h) * ML_DQK:(ML_HEADS + h + 1) * ML_DQK].astype(BF16)
            v_t = v_ref[bb, r0:r0 + L, h * ML_DV:(h + 1) * ML_DV].astype(F32).T
            p_t = lax.dot_general(kb, qb, nt_dims, preferred_element_type=F32) * w_intra
            c_old = c_st[st]
            n_old = n_st[st]
            num = jnp.dot(v_t.astype(BF16), p_t.astype(BF16), preferred_element_type=F32) + w_inter * (
                lax.dot_general(c_old.astype(BF16), qb, nt_dims, preferred_element_type=F32))
            qn = lax.dot_general(jnp.broadcast_to(n_old, (8, ML_DQK)).astype(BF16), qb, nt_dims,
                                 preferred_element_type=F32)[0:1, :]
            den = jnp.sum(p_t, axis=0, keepdims=True) + w_inter * qn
            hv = num / jnp.maximum(jnp.abs(den), jnp.exp(-m_t))

            m_new = m_t[:, L - 1:L]
            b_last = b_row[:, L - 1:L]
            wk = jnp.exp(b_last - b_row + i_row - m_new)
            decay = jnp.exp(b_last + m_prev - m_new)
            c_st[st] = decay * c_old + jnp.dot((v_t * wk).astype(BF16), kb, preferred_element_type=F32)
            n_st[st] = decay * n_old + jnp.dot(jnp.broadcast_to(wk, (8, L)).astype(BF16), kb,
                                               preferred_element_type=F32)[0:1, :]
            m_st[st] = m_new

            hn = (hv * lax.rsqrt(jnp.mean(hv * hv, axis=0, keepdims=True) + EPS)).T
            og = o_ref[bb, r0:r0 + L, h * ML_DV:(h + 1) * ML_DV].astype(F32)
            y_ref[bb, r0:r0 + L, h * ML_DV:(h + 1) * ML_DV] = (
                hn * hn_ref[:, h * ML_DV:(h + 1) * ML_DV] * jax.nn.sigmoid(og)).astype(BF16)

    for c in range(ML_BLOCK // L):
        for bb in range(ML_MB):
            chunk(bb, c)

    xbuf[:, 0:CONV_PAD, :] = xbuf[:, ML_BLOCK:ML_BLOCK + CONV_PAD, :]


def _mlstm(proj, small, conv_w, b_gates, head_norm, B, S):
    T = B * S
    ns = S // ML_BLOCK
    assert B % ML_MB == 0
    proj3 = proj.reshape(B, S, N_BIG)
    blk = lambda col: pl.BlockSpec((ML_MB, ML_BLOCK, D_MODEL), lambda b, s: (b, s, col))
    out = pl.pallas_call(
        _mlstm_body,
        grid=(B // ML_MB, ns),
        in_specs=[
            blk(COL_MLQK),
            blk(COL_MLV),
            blk(COL_MLO),
            pl.BlockSpec((ML_MB, ML_BLOCK, LANES), lambda b, s: (b, s, 0)),
            pl.BlockSpec((ML_CONV, D_MODEL), lambda b, s: (0, 0)),
            pl.BlockSpec((1, LANES), lambda b, s: (0, 0)),
            pl.BlockSpec((1, D_MODEL), lambda b, s: (0, 0)),
        ],
        out_specs=blk(0),
        out_shape=jax.ShapeDtypeStruct((B, S, D_MODEL), BF16),
        scratch_shapes=[
            pltpu.VMEM((ML_MB, ML_BLOCK + CONV_PAD, D_MODEL), F32),
            pltpu.VMEM((ML_MB * ML_HEADS, ML_DV, ML_DQK), F32),
            pltpu.VMEM((ML_MB * ML_HEADS, 1, ML_DQK), F32),
            pltpu.VMEM((ML_MB * ML_HEADS, 1, 1), F32),
        ],
        compiler_params=_cparams(("parallel", "arbitrary")),
        name="mlstm",
    )(proj3, proj3, proj3, small.reshape(B, S, LANES), conv_w, b_gates, head_norm)
    return out.reshape(T, D_MODEL)


FX_T = 512
FX_HP = 2
FX_VR = FX_DH + 16
LOG2E = 1.4426950408889634
N_PIECES = 3
FX_GATE_T = 128


def _fox_gate_body(g_ref, b_ref, o_ref):
    S = g_ref.shape[0]
    row = lax.broadcasted_iota(I32, (FX_GATE_T, FX_GATE_T), 0)
    col = lax.broadcasted_iota(I32, (FX_GATE_T, FX_GATE_T), 1)
    tri = (row >= col).astype(BF16)
    carry = jnp.zeros((1, LANES), F32)
    for blk in range(S // FX_GATE_T):
        rows = slice(blk * FX_GATE_T, (blk + 1) * FX_GATE_T)
        lf = _log_sigmoid(g_ref[rows, :] + b_ref[...])
        cum = carry
        for _ in range(N_PIECES):
            piece = lf.astype(BF16)
            cum = cum + jnp.dot(tri, piece, preferred_element_type=F32)
            lf = lf - piece.astype(F32)
        carry = cum[FX_GATE_T - 1:FX_GATE_T, :]
        o_ref[rows, :] = cum * (-LOG2E)


def _fox_gate(small, b_fx, B, S):
    return pl.pallas_call(
        _fox_gate_body,
        grid=(B,),
        in_specs=[
            pl.BlockSpec((S, LANES), lambda b: (b, 0)),
            pl.BlockSpec((1, LANES), lambda b: (0, 0)),
        ],
        out_specs=pl.BlockSpec((S, LANES), lambda b: (b, 0)),
        out_shape=jax.ShapeDtypeStruct((B * S, LANES), F32),
        compiler_params=_cparams(("parallel",)),
        name="fox_gate",
    )(small, b_fx)


def _fox_attn_body(q_ref, k_ref, v_ref, c_ref, o_ref, kx_ref, vt_ref, m_ref, acc_ref, s_ref):
    S = k_ref.shape[0]
    nq = S // FX_T

    c = c_ref[...]
    hi = c.astype(BF16)
    r1 = c - hi.astype(F32)
    mid = r1.astype(BF16)
    lo = (r1 - mid.astype(F32)).astype(BF16)
    sel_row = lax.broadcasted_iota(I32, (LANES, LANES), 0)
    sel_col = lax.broadcasted_iota(I32, (LANES, LANES), 1)
    ones_rows = (lax.broadcasted_iota(I32, (FX_VR - FX_DH, FX_T), 0) == 0).astype(BF16)
    head_slices = [slice(hh * FX_DH, (hh + 1) * FX_DH) for hh in range(FX_HP)]
    for hh, sl in enumerate(head_slices):
        lane = 2 * ML_HEADS + pl.program_id(1) * FX_HP + hh
        pieces = None
        for p, part in enumerate((hi, mid, lo)):
            pick = jnp.logical_and(sel_row == lane, sel_col == p).astype(BF16)
            t = jnp.dot(part, pick, preferred_element_type=F32)
            pieces = t if pieces is None else pieces + t
        kx_ref[hh, :, 0:FX_DH] = k_ref[:, sl]
        kx_ref[hh, :, FX_DH:2 * FX_DH] = pieces.astype(BF16)
        for j in range(nq):
            vt = v_ref[j * FX_T:(j + 1) * FX_T, sl].astype(F32).T.astype(BF16)
            vt_ref[hh, j] = jnp.concatenate([vt, ones_rows], axis=0)

    piece_rows = (lax.broadcasted_iota(I32, (FX_DH, FX_T), 0) < N_PIECES).astype(BF16)

    def start(i):
        q_x = []
        for sl in head_slices:
            q_t = (q_ref[i * FX_T:(i + 1) * FX_T, sl].astype(F32) * (FX_DH ** -0.5 * LOG2E)).T.astype(BF16)
            q_x.append(jnp.concatenate([q_t, piece_rows], axis=0))
        m_ref[i % 2] = jnp.full(m_ref.shape[1:], -jnp.inf, F32)
        acc_ref[i % 2] = jnp.zeros(acc_ref.shape[1:], F32)
        return q_x

    def key_rows(j):
        return pl.ds(j * FX_T, FX_T) if isinstance(j, int) else pl.ds(pl.multiple_of(j * FX_T, FX_T), FX_T)

    def scores(q_x, j, slot):
        for hh in range(FX_HP):
            s_ref[slot, hh] = jnp.dot(kx_ref[hh, key_rows(j), :], q_x[hh], preferred_element_type=F32)

    def consume(par, j, slot, masked):
        for hh in range(FX_HP):
            s = s_ref[slot, hh]
            if masked:
                key = lax.broadcasted_iota(I32, (FX_T, FX_T), 0)
                qry = lax.broadcasted_iota(I32, (FX_T, FX_T), 1)
                s = jnp.where(qry >= key, s, -jnp.inf)
            m_old = m_ref[par, hh]
            m_new = jnp.maximum(m_old, jnp.max(s, axis=0, keepdims=True))
            p = jnp.exp2(s - m_new).astype(BF16)
            acc_ref[par, hh] = jnp.exp2(m_old - m_new) * acc_ref[par, hh] + jnp.dot(
                vt_ref[hh, j], p, preferred_element_type=F32)
            m_ref[par, hh] = m_new

    def finish(i, slot):
        consume(i % 2, i, slot, True)
        for hh, sl in enumerate(head_slices):
            acc = acc_ref[i % 2, hh]
            o_ref[i * FX_T:(i + 1) * FX_T, sl] = (acc[0:FX_DH, :] / acc[FX_DH:FX_DH + 1, :]).T.astype(BF16)

    diag_slot = 0
    for i in range(nq):
        q_x = start(i)
        first = 0 if i == 0 else 1 - diag_slot
        scores(q_x, 0, first)
        if i > 0:
            finish(i - 1, diag_slot)

        def pair(jj, carry, q_x=q_x, first=first, par=i % 2):
            j = 2 * jj
            scores(q_x, j + 1, 1 - first)
            consume(par, j, first, False)
            scores(q_x, j + 2, first)
            consume(par, j + 1, 1 - first, False)
            return carry

        if i >= 2:
            lax.fori_loop(0, i // 2, pair, 0)
        if i % 2 == 1:
            scores(q_x, i, 1 - first)
            consume(i % 2, i - 1, first, False)
            diag_slot = 1 - first
        else:
            diag_slot = first
    finish(nq - 1, diag_slot)


def _fox_attn(proj, c_neg, B, S):
    T = B * S
    nq = S // FX_T
    wide = FX_HP * FX_DH
    cq = COL_FXQ * (D_MODEL // wide)
    ck = COL_FXK * (D_MODEL // wide)
    cv = COL_FXV * (D_MODEL // wide)
    proj3 = proj.reshape(B, S, N_BIG)
    out = pl.pallas_call(
        _fox_attn_body,
        grid=(B, FX_HEADS // FX_HP),
        in_specs=[
            pl.BlockSpec((None, S, wide), lambda b, h: (b, 0, cq + h)),
            pl.BlockSpec((None, S, wide), lambda b, h: (b, 0, ck + h)),
            pl.BlockSpec((None, S, wide), lambda b, h: (b, 0, cv + h)),
            pl.BlockSpec((None, S, LANES), lambda b, h: (b, 0, 0)),
        ],
        out_specs=pl.BlockSpec((None, S, wide), lambda b, h: (b, 0, h)),
        out_shape=jax.ShapeDtypeStruct((B, S, D_MODEL), BF16),
        scratch_shapes=[
            pltpu.VMEM((FX_HP, S, 2 * FX_DH), BF16),
            pltpu.VMEM((FX_HP, nq, FX_VR, FX_T), BF16),
            pltpu.VMEM((2, FX_HP, 1, FX_T), F32),
            pltpu.VMEM((2, FX_HP, FX_VR, FX_T), F32),
            pltpu.VMEM((2, FX_HP, FX_T, FX_T), F32),
        ],
        compiler_params=_cparams(("parallel", "parallel")),
        name="fox_attn",
    )(proj3, proj3, proj3, c_neg.reshape(B, S, LANES))
    return out.reshape(T, D_MODEL)


def _memkv_body(x_ref, g_ref, w_ref, o_ref):
    hb = _rms(x_ref[...], g_ref[...]).astype(BF16)
    o_ref[...] = jnp.dot(hb, w_ref[...], preferred_element_type=F32).astype(BF16)


def _memkv(mem2d, g, w_kv):
    R = mem2d.shape[0]
    tm = min(512, R)
    N = w_kv.shape[1]
    return pl.pallas_call(
        _memkv_body,
        grid=(R // tm,),
        in_specs=[
            pl.BlockSpec((tm, D_MODEL), lambda i: (i, 0)),
            pl.BlockSpec((1, D_MODEL), lambda i: (0, 0)),
            pl.BlockSpec((D_MODEL, N), lambda i: (0, 0)),
        ],
        out_specs=pl.BlockSpec((tm, N), lambda i: (i, 0)),
        out_shape=jax.ShapeDtypeStruct((R, N), BF16),
        compiler_params=_cparams(("parallel",)),
        name="memkv",
    )(mem2d, g, w_kv)


CA_TQ = 2048


def _memattn_body(q_ref, k_ref, v_ref, o_ref):
    scale = CA_DH ** -0.5
    for h in range(CA_HEADS):
        sl = slice(h * CA_DH, (h + 1) * CA_DH)
        s = lax.dot_general(q_ref[:, sl], k_ref[:, sl], (((1,), (1,)), ((), ())),
                            preferred_element_type=F32) * scale
        p = jnp.exp(s - jnp.max(s, axis=-1, keepdims=True))
        l = jnp.sum(p, axis=-1, keepdims=True)
        o = jnp.dot(p.astype(BF16), v_ref[:, sl], preferred_element_type=F32) / l
        o_ref[:, sl] = o.astype(BF16)


def _memattn(proj, kv, B, S, M):
    T = B * S
    nq = S // CA_TQ
    kv3 = kv.reshape(B, M, 2 * D_MODEL)
    return pl.pallas_call(
        _memattn_body,
        grid=(B, nq),
        in_specs=[
            pl.BlockSpec((CA_TQ, D_MODEL), lambda b, i: (b * nq + i, COL_CAQ)),
            pl.BlockSpec((None, M, D_MODEL), lambda b, i: (b, 0, 0)),
            pl.BlockSpec((None, M, D_MODEL), lambda b, i: (b, 0, 1)),
        ],
        out_specs=pl.BlockSpec((CA_TQ, D_MODEL), lambda b, i: (b * nq + i, 0)),
        out_shape=jax.ShapeDtypeStruct((T, D_MODEL), BF16),
        compiler_params=_cparams(("parallel", "arbitrary")),
        name="memattn",
    )(proj, kv3, kv3)


MERGE_TM = 1024
MERGE_VMEM_LIMIT = 62 * 1024 * 1024
MOE_PARTS = 2


def _merge_body(y0_ref, y1_ref, y2_ref, g0_ref, g1_ref, g2_ref, x_ref, wb_ref, wo_ref, gn_ref, wr_ref, br_ref,
                o_ref, hp_ref, ri_ref, rw_ref, cnt_ref, carry_ref):
    merged = None
    for n, (y_ref, g_ref) in enumerate(((y0_ref, g0_ref), (y1_ref, g1_ref), (y2_ref, g2_ref))):
        p = jnp.dot(y_ref[...], wb_ref[n], preferred_element_type=F32)
        t = jax.nn.sigmoid(g_ref[...].astype(F32)) * p
        merged = t if merged is None else merged + t
    x2 = x_ref[...] + jnp.dot(merged.astype(BF16), wo_ref[...], preferred_element_type=F32)
    o_ref[...] = x2
    _route(x2, gn_ref, wr_ref, br_ref, hp_ref, ri_ref, rw_ref, cnt_ref, carry_ref)


def _merge(y_ml, y_fx, y_ca, proj, x2d, w_branch, w_out, g_moe, w_router, b_router, part):
    T = x2d.shape[0] // MOE_PARTS
    tm = MERGE_TM
    off = part * (T // tm)
    src = lambda i: (off + i, 0)
    row = lambda i: (i, 0)
    const = lambda i: (0, 0)
    return pl.pallas_call(
        _merge_body,
        grid=(T // tm,),
        in_specs=[
            pl.BlockSpec((tm, D_MODEL), src),
            pl.BlockSpec((tm, D_MODEL), src),
            pl.BlockSpec((tm, D_MODEL), src),
            pl.BlockSpec((tm, D_MODEL), lambda i: (off + i, COL_GATE0)),
            pl.BlockSpec((tm, D_MODEL), lambda i: (off + i, COL_GATE0 + 1)),
            pl.BlockSpec((tm, D_MODEL), lambda i: (off + i, COL_GATE0 + 2)),
            pl.BlockSpec((tm, D_MODEL), src),
            pl.BlockSpec((3, D_MODEL, D_MODEL), lambda i: (0, 0, 0), pipeline_mode=pl.Buffered(1)),
            pl.BlockSpec((D_MODEL, D_MODEL), const, pipeline_mode=pl.Buffered(1)),
            pl.BlockSpec((1, D_MODEL), const),
            pl.BlockSpec((N_EXPERTS, D_MODEL), const),
            pl.BlockSpec((N_EXPERTS, 1), const),
        ],
        out_specs=[
            pl.BlockSpec((tm, D_MODEL), row),
            pl.BlockSpec((tm, HALF), row),
            pl.BlockSpec((2 * TOP_K, tm), lambda i: (0, i)),
            pl.BlockSpec((2 * TOP_K, tm), lambda i: (0, i)),
            pl.BlockSpec((N_EXPERTS, 1), const),
        ],
        out_shape=[
            jax.ShapeDtypeStruct((T, D_MODEL), F32),
            jax.ShapeDtypeStruct((T, HALF), I32),
            jax.ShapeDtypeStruct((2 * TOP_K, T), I32),
            jax.ShapeDtypeStruct((2 * TOP_K, T), F32),
            jax.ShapeDtypeStruct((N_EXPERTS, 1), F32),
        ],
        scratch_shapes=[pltpu.VMEM((N_EXPERTS, 1), F32)],
        compiler_params=pltpu.CompilerParams(dimension_semantics=("arbitrary",),
                                             vmem_limit_bytes=MERGE_VMEM_LIMIT),
        name="merge_router",
    )(y_ml, y_fx, y_ca, proj, proj, proj, x2d, w_branch, w_out, g_moe, w_router, b_router)


def _route(x2, g_ref, wr_ref, br_ref, hp_ref, ri_ref, rw_ref, cnt_ref, carry_ref):
    tm = MERGE_TM

    @pl.when(pl.program_id(0) == 0)
    def _():
        carry_ref[...] = jnp.zeros_like(carry_ref)

    h = _rms(x2, g_ref[...])
    hp_ref[...] = _pack_rows(h)
    logits = lax.dot_general(wr_ref[...], h.astype(BF16), (((1,), (1,)), ((), ())),
                             preferred_element_type=F32) + br_ref[...]
    eid = lax.broadcasted_iota(I32, (N_EXPERTS, tm), 0).astype(F32)

    work = logits
    onehot_sum = jnp.zeros((N_EXPERTS, tm), F32)
    vals, sels, idxs = [], [], []
    for _ in range(TOP_K):
        mx = jnp.max(work, axis=0, keepdims=True)
        idx = jnp.min(jnp.where(work == mx, eid, float(N_EXPERTS)), axis=0, keepdims=True)
        sel = eid == idx
        onehot_sum = onehot_sum + sel.astype(F32)
        work = jnp.where(sel, -jnp.inf, work)
        vals.append(mx)
        sels.append(sel)
        idxs.append(idx)
    exps = [jnp.exp(v - vals[0]) for v in vals]
    total = exps[0] + exps[1] + exps[2] + exps[3]

    earlier = (lax.broadcasted_iota(I32, (tm, tm), 0) < lax.broadcasted_iota(I32, (tm, tm), 1)).astype(BF16)
    before = jnp.dot(onehot_sum.astype(BF16), earlier, preferred_element_type=F32) + carry_ref[...]
    carry_ref[...] = carry_ref[...] + jnp.sum(onehot_sum, axis=1, keepdims=True)
    cnt_ref[...] = carry_ref[...]

    out_row = lax.broadcasted_iota(I32, (2 * TOP_K, tm), 0)
    ri = jnp.zeros((2 * TOP_K, tm), I32)
    rw = jnp.zeros((2 * TOP_K, tm), F32)
    for k in range(TOP_K):
        rank = jnp.sum(jnp.where(sels[k], before, 0.0), axis=0, keepdims=True)
        ri = jnp.where(out_row == k, idxs[k].astype(I32), ri)
        ri = jnp.where(out_row == TOP_K + k, rank.astype(I32), ri)
        rw = jnp.where(out_row == k, exps[k] / total, rw)
    ri_ref[...] = ri
    rw_ref[...] = rw


EXPERT_TM = 512
SC_CORES = 2
SC_SUBCORES = 16
SC_WORKERS = SC_CORES * SC_SUBCORES
SC_CHUNK = 64
PAD_SLOTS = N_EXPERTS * EXPERT_TM


def _sc_mesh():
    return plsc.VectorSubcoreMesh(core_axis_name="c", subcore_axis_name="s")


def _sc_worker():
    return lax.axis_index("s") * SC_CORES + lax.axis_index("c")


def _scatter_indices(dest):
    T = dest.shape[1]
    n_ch = T // (SC_WORKERS * SC_CHUNK)
    idx = dest.reshape(TOP_K, SC_WORKERS, n_ch, SC_CHUNK).transpose(1, 2, 0, 3)
    return idx.reshape(SC_WORKERS, n_ch * TOP_K, SC_CHUNK)


def _sc_dispatch(hp, idx, pad_idx, n_rows):
    T = hp.shape[0]
    per_w = T // SC_WORKERS
    n_ch = per_w // SC_CHUNK
    n_pc = PAD_SLOTS // (SC_WORKERS * SC_CHUNK)
    assert per_w % SC_CHUNK == 0 and n_ch >= 2 and n_ch % 2 == 0
    pidx = pad_idx.reshape(SC_WORKERS, n_pc, SC_CHUNK)
    zeros = jnp.zeros((SC_CHUNK, HALF), I32)

    @functools.partial(
        pl.kernel, mesh=_sc_mesh(),
        out_type=jax.ShapeDtypeStruct((n_rows, HALF), I32),
        scratch_types=[
            pltpu.VMEM((n_ch * TOP_K, SC_CHUNK), I32),
            pltpu.VMEM((n_pc, SC_CHUNK), I32),
            pltpu.VMEM((2, SC_CHUNK, HALF), I32),
            pltpu.SemaphoreType.DMA((2,)),
            pltpu.SemaphoreType.DMA((2,)),
        ],
        name="sc_dispatch",
    )
    def k(hp_hbm, idx_hbm, pidx_hbm, zeros_hbm, xs_hbm, idx_v, pidx_v, rows_v, lsem, ssem):
        wid = _sc_worker()
        base = wid * per_w
        pltpu.sync_copy(idx_hbm.at[wid], idx_v)
        pltpu.sync_copy(pidx_hbm.at[wid], pidx_v)

        pltpu.sync_copy(zeros_hbm, rows_v.at[0])
        for p in range(n_pc):
            pltpu.make_async_copy(rows_v.at[0], xs_hbm.at[pidx_v.at[p]], ssem.at[0]).start()
        for p in range(n_pc):
            pltpu.make_async_copy(rows_v.at[0], xs_hbm.at[pidx_v.at[p]], ssem.at[0]).wait()

        def load(i, slot):
            return pltpu.make_async_copy(hp_hbm.at[pl.ds(base + i * SC_CHUNK, SC_CHUNK)], rows_v.at[slot],
                                         lsem.at[slot])

        def scatter(i, kk, slot):
            return pltpu.make_async_copy(rows_v.at[slot], xs_hbm.at[idx_v.at[i * TOP_K + kk]], ssem.at[slot])

        load(0, 0).start()

        def body(i2, carry):
            for slot in range(2):
                i = i2 * 2 + slot
                nxt = 1 - slot

                @pl.when(i + 1 < n_ch)
                def _():
                    @pl.when(i >= 1)
                    def _():
                        for kk in range(TOP_K):
                            scatter(i - 1, kk, nxt).wait()
                    load(i + 1, nxt).start()

                load(i, slot).wait()
                for kk in range(TOP_K):
                    scatter(i, kk, slot).start()
            return carry

        lax.fori_loop(0, n_ch // 2, body, 0)
        for kk in range(TOP_K):
            scatter(n_ch - 2, kk, 0).wait()
            scatter(n_ch - 1, kk, 1).wait()

    return k(hp, idx, pidx, zeros)


def _sc_gather(table, idx):
    n = idx.shape[0]
    per_w = n // SC_WORKERS
    n_ch = per_w // SC_CHUNK
    assert per_w % SC_CHUNK == 0 and n_ch >= 2 and n_ch % 2 == 0

    @functools.partial(
        pl.kernel, mesh=_sc_mesh(),
        out_type=jax.ShapeDtypeStruct((n, HALF), I32),
        scratch_types=[
            pltpu.VMEM((n_ch, SC_CHUNK), I32),
            pltpu.VMEM((2, SC_CHUNK, HALF), I32),
            pltpu.SemaphoreType.DMA((2,)),
            pltpu.SemaphoreType.DMA((2,)),
        ],
        name="sc_gather",
    )
    def k(table_hbm, idx_hbm, out_hbm, idx_v, rows_v, gsem, wsem):
        wid = _sc_worker()
        base = wid * per_w
        pltpu.sync_copy(idx_hbm.at[wid], idx_v)

        def gather(i, slot):
            return pltpu.make_async_copy(table_hbm.at[idx_v.at[i]], rows_v.at[slot], gsem.at[slot])

        def writeback(i, slot):
            return pltpu.make_async_copy(rows_v.at[slot], out_hbm.at[pl.ds(base + i * SC_CHUNK, SC_CHUNK)],
                                         wsem.at[slot])

        gather(0, 0).start()

        def body(i2, carry):
            for slot in range(2):
                i = i2 * 2 + slot
                nxt = 1 - slot

                @pl.when(i + 1 < n_ch)
                def _():
                    @pl.when(i >= 1)
                    def _():
                        writeback(i - 1, nxt).wait()
                    gather(i + 1, nxt).start()

                gather(i, slot).wait()
                writeback(i, slot).start()
            return carry

        lax.fori_loop(0, n_ch // 2, body, 0)
        writeback(n_ch - 2, 0).wait()
        writeback(n_ch - 1, 1).wait()

    return k(table, idx.reshape(SC_WORKERS, n_ch, SC_CHUNK))


FF_CHUNK = 1024


def _expert_body(te_ref, nv_ref, nx_ref, x_ref, w1_hbm, b1_ref, w2_hbm, b2_ref, y_ref,
                 w1_stage, w2_stage, w1_ref, w2_ref, sem):
    i = pl.program_id(0)
    valid = i < nv_ref[0]
    first = jnp.logical_or(i == 0, te_ref[i] != te_ref[jnp.maximum(i - 1, 0)])

    def fetch(e):
        return (pltpu.make_async_copy(w1_hbm.at[e], w1_stage, sem.at[0]),
                pltpu.make_async_copy(w2_hbm.at[e], w2_stage, sem.at[1]))

    @pl.when(i == 0)
    def _():
        for copy in fetch(te_ref[0]):
            copy.start()

    @pl.when(jnp.logical_and(valid, first))
    def _():
        for copy in fetch(te_ref[i]):
            copy.wait()
        w1_ref[...] = w1_stage[...].astype(BF16)
        w2_ref[...] = w2_stage[...].astype(BF16)

        @pl.when(nx_ref[i] < N_EXPERTS)
        def _():
            for copy in fetch(nx_ref[i]):
                copy.start()

    @pl.when(valid)
    def _():
        lo, hi = _unpack_rows(x_ref[...])
        xb = jnp.concatenate([lo.astype(BF16), hi.astype(BF16)], axis=-1)
        acc = jnp.zeros((EXPERT_TM, D_MODEL), F32) + b2_ref[...]
        for c in range(D_FF // FF_CHUNK):
            def up(off):
                cs = slice(off + c * FF_CHUNK, off + (c + 1) * FF_CHUNK)
                return jnp.dot(xb, w1_ref[:, cs], preferred_element_type=F32) + b1_ref[:, cs]
            g = jnp.minimum(up(0), SWIGLU_LIMIT)
            lin = jnp.clip(up(D_FF), -SWIGLU_LIMIT, SWIGLU_LIMIT)
            a = g * jax.nn.sigmoid(SWIGLU_ALPHA * g) * (lin + 1.0)
            acc = acc + jnp.dot(a.astype(BF16), w2_ref[c * FF_CHUNK:(c + 1) * FF_CHUNK, :],
                                preferred_element_type=F32)
        y_ref[...] = _pack_rows(acc)


def _experts(tile_expert, n_valid, next_expert, xs, w1, b1, w2, b2):
    n_rows = xs.shape[0]
    tm = EXPERT_TM
    n_tiles = n_rows // tm
    row = lambda i, te, nv, nx: (jnp.minimum(i, nv[0] - 1), 0)
    grid_spec = pltpu.PrefetchScalarGridSpec(
        num_scalar_prefetch=3,
        grid=(n_tiles,),
        in_specs=[
            pl.BlockSpec((tm, HALF), row),
            pl.BlockSpec(memory_space=pl.ANY),
            pl.BlockSpec((None, 1, 2 * D_FF), lambda i, te, nv, nx: (te[i], 0, 0)),
            pl.BlockSpec(memory_space=pl.ANY),
            pl.BlockSpec((None, 1, D_MODEL), lambda i, te, nv, nx: (te[i], 0, 0)),
        ],
        out_specs=pl.BlockSpec((tm, HALF), row),
        scratch_shapes=[pltpu.VMEM((D_MODEL, 2 * D_FF), F32), pltpu.VMEM((D_FF, D_MODEL), F32),
                        pltpu.VMEM((D_MODEL, 2 * D_FF), BF16), pltpu.VMEM((D_FF, D_MODEL), BF16),
                        pltpu.SemaphoreType.DMA((2,))],
    )
    return pl.pallas_call(
        _expert_body,
        grid_spec=grid_spec,
        out_shape=jax.ShapeDtypeStruct((n_rows, HALF), I32),
        compiler_params=_cparams(("arbitrary",)),
        name="experts",
    )(tile_expert, n_valid, next_expert, xs, w1, b1, w2, b2)


COMBINE_TM = 1024


def _combine_body(y0_ref, y1_ref, y2_ref, y3_ref, rw_ref, x_ref, g_ref, *rest):
    o_ref = rest[-1]
    acc = x_ref[...]
    rw = jnp.concatenate([rw_ref[...], jnp.zeros((LANES - 2 * TOP_K, COMBINE_TM), F32)], axis=0).T
    for k, y_ref in enumerate((y0_ref, y1_ref, y2_ref, y3_ref)):
        lo, hi = _unpack_rows(y_ref[...])
        acc = acc + rw[:, k:k + 1] * jnp.concatenate([lo, hi], axis=-1)
    o_ref[...] = _rms(acc, g_ref[...])


def _combine(yg, rw, x2, g, part, out_prev):
    T = x2.shape[0]
    tm = COMBINE_TM
    nt = T // tm
    in_specs = [
        pl.BlockSpec((tm, HALF), lambda i: (i, 0)),
        pl.BlockSpec((tm, HALF), lambda i: (nt + i, 0)),
        pl.BlockSpec((tm, HALF), lambda i: (2 * nt + i, 0)),
        pl.BlockSpec((tm, HALF), lambda i: (3 * nt + i, 0)),
        pl.BlockSpec((2 * TOP_K, tm), lambda i: (0, i)),
        pl.BlockSpec((tm, D_MODEL), lambda i: (i, 0)),
        pl.BlockSpec((1, D_MODEL), lambda i: (0, 0)),
    ]
    args = [yg, yg, yg, yg, rw, x2, g]
    aliases = {}
    if out_prev is not None:
        in_specs.append(pl.BlockSpec(memory_space=pl.ANY))
        args.append(out_prev)
        aliases = {len(args) - 1: 0}
    return pl.pallas_call(
        _combine_body,
        grid=(nt,),
        in_specs=in_specs,
        out_specs=pl.BlockSpec((tm, D_MODEL), lambda i: (part * nt + i, 0)),
        out_shape=jax.ShapeDtypeStruct((T * MOE_PARTS, D_MODEL), F32),
        input_output_aliases=aliases,
        compiler_params=_cparams(("parallel",)),
        name="combine",
    )(*args)


def _pad_lanes(v):
    v = v.reshape(1, -1).astype(F32)
    return jnp.pad(v, ((0, 0), (0, LANES - v.shape[1])))


def _layer(x2d, mem2d, B, S, M, norm_mix, w_in, b_ml_gates, conv_ml, ml_head_norm, b_fx_gate, norm_mem,
           w_mem_kv, w_branch, w_out, norm_moe, w_router, b_router, w_exp_in, b_exp_in, w_exp_out,
           b_exp_out, norm_out):
    T = B * S
    w16 = w_in.astype(BF16)
    w_big = jnp.concatenate([w16[:, 0:2048], w16[:, 2056:3080], w16[:, 3080:6152], w16[:, 6160:7184],
                             w16[:, 7184:10256]], axis=1)
    w_small = jnp.concatenate([w16[:, 2048:2056], w16[:, 6152:6160]], axis=1)
    w_small = jnp.pad(w_small, ((0, 0), (0, LANES - w_small.shape[1])))
    row = lambda v: v.reshape(1, -1).astype(F32)

    proj, small = _inproj(x2d, row(norm_mix), w_big, w_small)

    y_ml = _mlstm(proj, small, conv_ml.astype(F32), _pad_lanes(b_ml_gates), row(ml_head_norm), B, S)

    b_fx = jnp.pad(b_fx_gate.reshape(1, -1).astype(F32), ((0, 0), (2 * ML_HEADS, LANES - 2 * ML_HEADS - FX_HEADS)))
    y_fx = _fox_attn(proj, _fox_gate(small, b_fx, B, S), B, S)

    kv = _memkv(mem2d, row(norm_mem), w_mem_kv.astype(BF16))
    y_ca = _memattn(proj, kv, B, S, M)

    w_r = w_router.T.astype(BF16)
    moe_weights = (w_exp_in.astype(F32), b_exp_in.reshape(N_EXPERTS, 1, -1).astype(F32), w_exp_out.astype(F32),
                   b_exp_out.reshape(N_EXPERTS, 1, -1).astype(F32))
    staged = []
    for part in range(MOE_PARTS):
        x2, hp, ri, rw, cnt = _merge(y_ml, y_fx, y_ca, proj, x2d, w_branch.astype(BF16), w_out.astype(BF16),
                                     row(norm_moe), w_r, b_router.reshape(N_EXPERTS, 1).astype(F32), part)
        scatter_idx, dest, pad_idx, tiles, n_rows = _moe_plan(ri, cnt)
        staged.append((x2, rw, dest, tiles, _sc_dispatch(hp, scatter_idx, pad_idx, n_rows)))
    gathered = [_sc_gather(_experts(tile_e, n_valid, next_e, xs, *moe_weights), dest.reshape(-1))
                for _, _, dest, (tile_e, n_valid, next_e), xs in staged]
    out = None
    for part, ((x2, rw, *_), yg) in enumerate(zip(staged, gathered)):
        out = _combine(yg, rw, x2, row(norm_out), part, out)
    return out


def _moe_plan(ri, cnt):
    T = ri.shape[1]
    tm = EXPERT_TM
    n_tiles = (T * TOP_K) // tm + N_EXPERTS
    counts = cnt[:, 0].astype(I32)
    padded = ((counts + tm - 1) // tm) * tm
    gend = jnp.cumsum(padded)
    gstart = gend - padded
    expert_ids = jnp.arange(N_EXPERTS, dtype=I32)
    start_of = jnp.sum(jnp.where(ri[0:TOP_K, :, None] == expert_ids, gstart, 0), axis=-1)
    dest = start_of + ri[TOP_K:2 * TOP_K, :]
    n_valid = gend[-1] // tm
    tile_ids = jnp.arange(n_tiles, dtype=I32)
    last_tile = jnp.minimum(tile_ids, n_valid - 1)
    tile_e = jnp.minimum(jnp.sum((gend[None, :] <= last_tile[:, None] * tm).astype(I32), axis=1), N_EXPERTS - 1)

    slot = jnp.arange(tm, dtype=I32)
    spare = n_tiles * tm + slot % SC_CHUNK
    pad_idx = jnp.where(slot[None, :] < (padded - counts)[:, None], (gstart + counts)[:, None] + slot[None, :],
                        spare[None, :]).reshape(-1)

    later = jnp.logical_and(expert_ids[None, :] > expert_ids[:, None], (padded > 0)[None, :])
    next_group = jnp.min(jnp.where(later, expert_ids[None, :], N_EXPERTS), axis=1)
    tiles = (tile_e.astype(I32), n_valid.reshape(1).astype(I32), next_group[tile_e].astype(I32))
    return _scatter_indices(dest), dest, pad_idx, tiles, n_tiles * tm + SC_CHUNK


def kernel(x, mem, norm_mix, w_in, b_ml_gates, conv_ml, ml_head_norm, b_fx_gate, norm_mem, w_mem_kv, w_branch,
           w_out, norm_moe, w_router, b_router, w_exp_in, b_exp_in, w_exp_out, b_exp_out, norm_final):
    B, S, D = x.shape
    M = mem.shape[1]
    depth = norm_mix.shape[0]
    assert depth == 1, "the combine kernel fuses the final norm, so exactly one layer is supported"
    assert D == D_MODEL and S % ML_BLOCK == 0 and S % FX_T == 0 and S % CA_TQ == 0
    out = _layer(x.reshape(B * S, D), mem.reshape(B * M, D), B, S, M, norm_mix[0], w_in[0], b_ml_gates[0],
                 conv_ml[0], ml_head_norm[0], b_fx_gate[0], norm_mem[0], w_mem_kv[0], w_branch[0], w_out[0],
                 norm_moe[0], w_router[0], b_router[0], w_exp_in[0], b_exp_in[0], w_exp_out[0], b_exp_out[0],
                 norm_final)
    return out.reshape(B, S, D)
```
